```python
import jax
import jax.numpy as jnp
from jax import lax
import numpy as np

D_MODEL = 1024
BATCH = 32
SEQ = 2048
DEPTH = 2

HEAD_DIM = 64
ROPE_THETA = 10000.0
NORM_EPS = 1e-6
QUERY_BLOCK = 128
A_Q_HEADS = 8
A_KV_HEADS = 2
A_WINDOW = 128
B_HEADS = 8
C_HEADS = 8
C_PATTERNS = ((128, 1), (512, 4), (2048, 16))
D_HEADS = 8
MOBA_BLOCK = 256
MOBA_TOPK = 3
MOBA_Q_CHUNK = 16
PEER_HEADS = 8
PEER_N_KEYS = 128
PEER_N_EXPERTS = PEER_N_KEYS * PEER_N_KEYS
PEER_TOPK = 16
PEER_D_KEY = 256
PEER_TOKEN_CHUNK = 128

N_EVEN = (DEPTH + 1) // 2
N_ODD = DEPTH // 2
AB_IN = (A_Q_HEADS + 2 * A_KV_HEADS + 3 * B_HEADS) * HEAD_DIM
AB_OUT = (A_Q_HEADS + B_HEADS) * HEAD_DIM
CD_IN = 3 * (C_HEADS + D_HEADS) * HEAD_DIM
CD_OUT = (C_HEADS + D_HEADS) * HEAD_DIM

kernel_name = 'hybrid_swa_stick_dilated_moba_peer'


def rms_norm(x, g):
    xf = x.astype(jnp.float32)
    y = xf * lax.rsqrt(jnp.mean(xf * xf, axis=-1, keepdims=True) + NORM_EPS)
    return (y * g.astype(jnp.float32)).astype(x.dtype)


def adaln(x, g, shift, scale):
    return rms_norm(x, g) * (1.0 + scale) + shift


def rope(x, pos):
    half = HEAD_DIM // 2
    inv_freq = ROPE_THETA ** (-jnp.arange(half, dtype=jnp.float32) / half)
    ang = pos.astype(jnp.float32)[:, None] * inv_freq[None, :]
    cos = jnp.cos(ang)[None, :, None, :]
    sin = jnp.sin(ang)[None, :, None, :]
    xf = x.astype(jnp.float32)
    x1, x2 = xf[..., :half], xf[..., half:]
    out = jnp.concatenate([x1 * cos - x2 * sin, x2 * cos + x1 * sin], axis=-1)
    return out.astype(x.dtype)


def qk_prepare(q, k, gq, gk, pos):
    return rope(rms_norm(q, gq), pos), rope(rms_norm(k, gk), pos)


def split_cols(x, widths):
    outs, start = [], 0
    for w in widths:
        outs.append(x[..., start:start + w])
        start += w
    return outs


def banded_attention(q, k, v, max_offset, sink=None):
    n, length, hkv, grp, dh = q.shape
    blk = QUERY_BLOCK
    nb = -(-length // blk)
    pad = nb * blk - length
    qb = jnp.pad(q, ((0, 0), (0, pad), (0, 0), (0, 0), (0, 0))).reshape(n, nb, blk, hkv, grp, dh)

    def band(t):
        tp = jnp.pad(t, ((0, 0), (blk, pad), (0, 0), (0, 0))).reshape(n, nb + 1, blk, hkv, dh)
        return jnp.concatenate([tp[:, :-1], tp[:, 1:]], axis=2)

    kb, vb = band(k), band(v)
    s = jnp.einsum('nbqhgd,nbkhd->nbhgqk', qb, kb, preferred_element_type=jnp.float32) * (dh ** -0.5)
    qi = jnp.arange(blk)[:, None]
    ki = jnp.arange(2 * blk)[None, :]
    dist = blk + qi - ki
    key_pos = jnp.arange(nb)[:, None, None] * blk - blk + ki[None]
    valid = (dist >= 0) & (dist <= max_offset) & (key_pos >= 0)
    s = jnp.where(valid[None, :, None, None], s, -jnp.inf)
    lse = jax.nn.logsumexp(s, axis=-1)
    if sink is not None:
        lse = jnp.logaddexp(lse, sink.astype(jnp.float32)[None, None, :, :, None])
    p = jnp.exp(s - lse[..., None])
    o = jnp.einsum('nbhgqk,nbkhd->nbqhgd', p.astype(v.dtype), vb)
    o = o.reshape(n, nb * blk, hkv, grp, dh)[:, :length]
    lse = lse.transpose(0, 1, 4, 2, 3).reshape(n, nb * blk, hkv, grp)[:, :length]
    return o, lse


def stick_breaking_attention(q, k, v):
    b, s_len, h, dh = q.shape
    nb = s_len // QUERY_BLOCK
    qb = q.reshape(b, nb, QUERY_BLOCK, h, dh).transpose(1, 0, 2, 3, 4)
    key_pos = jnp.arange(s_len)

    def one_block(args):
        q_blk, i = args
        z = jnp.einsum('bqhd,bkhd->bhqk', q_blk, k, preferred_element_type=jnp.float32) * (dh ** -0.5)
        q_pos = i * QUERY_BLOCK + jnp.arange(QUERY_BLOCK)
        past = key_pos[None, :] < q_pos[:, None]
        log_keep = jnp.where(past, jax.nn.log_sigmoid(-z), 0.0)
        later = lax.cumsum(log_keep, axis=3, reverse=True) - log_keep
        w = jnp.where(past, jnp.exp(jax.nn.log_sigmoid(z) + later), 0.0)
        return jnp.einsum('bhqk,bkhd->bqhd', w.astype(v.dtype), v)

    out = lax.map(one_block, (qb, jnp.arange(nb)))
    return out.transpose(1, 0, 2, 3, 4).reshape(b, s_len, h, dh)


def dilated_attention(q, k, v):
    b, s_len, h, dh = q.shape
    outs, lses = [], []
    for window, dil in C_PATTERNS:
        m = s_len // dil

        def to_sub(t):
            return t.reshape(b, m, dil, h, dh).transpose(0, 2, 1, 3, 4).reshape(b * dil, m, h, dh)

        o, lse = banded_attention(to_sub(q)[:, :, :, None], to_sub(k), to_sub(v), window // dil)
        outs.append(o[:, :, :, 0].reshape(b, dil, m, h, dh).transpose(0, 2, 1, 3, 4).reshape(b, s_len, h, dh))
        lses.append(lse[..., 0].reshape(b, dil, m, h).transpose(0, 2, 1, 3).reshape(b, s_len, h))
    wts = jax.nn.softmax(jnp.stack(lses), axis=0)
    return jnp.einsum('pbsh,pbshd->bshd', wts.astype(v.dtype), jnp.stack(outs))


def moba_attention(q, k, v):
    b, s_len, h, dh = q.shape
    nblk = -(-s_len // MOBA_BLOCK)
    pad = nblk * MOBA_BLOCK - s_len

    def blocks(t):
        tp = jnp.pad(t, ((0, 0), (0, pad), (0, 0), (0, 0)))
        return tp.reshape(b, nblk, MOBA_BLOCK, h, dh).transpose(0, 3, 1, 2, 4)

    kb, vb = blocks(k), blocks(v)
    k_mean = jnp.mean(kb.astype(jnp.float32), axis=3)
    n_sel = min(MOBA_TOPK, nblk - 1)
    scale = dh ** -0.5
    nq = s_len // MOBA_Q_CHUNK
    qc = q.reshape(b, nq, MOBA_Q_CHUNK, h, dh).transpose(1, 0, 2, 3, 4)
    bi = jnp.arange(b)[:, None, None, None]
    hi = jnp.arange(h)[None, :, None, None]

    def one_chunk(args):
        q_blk, ci = args
        q_pos = ci * MOBA_Q_CHUNK + jnp.arange(MOBA_Q_CHUNK)
        own = (ci * MOBA_Q_CHUNK) // MOBA_BLOCK
        k_own = lax.dynamic_index_in_dim(kb, own, axis=2, keepdims=False)
        v_own = lax.dynamic_index_in_dim(vb, own, axis=2, keepdims=False)
        own_pos = own * MOBA_BLOCK + jnp.arange(MOBA_BLOCK)
        s_own = jnp.einsum('bqhd,bhkd->bhqk', q_blk, k_own, preferred_element_type=jnp.float32) * scale
        s_own = jnp.where(own_pos[None, :] <= q_pos[:, None], s_own, -jnp.inf)
        if n_sel == 0:
            p = jax.nn.softmax(s_own, axis=-1)
            return jnp.einsum('bhqk,bhkd->bqhd', p.astype(v.dtype), v_own)
        gate = jnp.einsum('bqhd,bhnd->bhqn', q_blk, k_mean, preferred_element_type=jnp.float32)
        gate = jnp.where(jnp.arange(nblk) < own, gate, -jnp.inf)
        _, idx = lax.top_k(gate, n_sel)
        k_sel = kb[bi, hi, idx]
        v_sel = vb[bi, hi, idx].reshape(b, h, MOBA_Q_CHUNK, n_sel * MOBA_BLOCK, dh)
        s_sel = jnp.einsum('bqhd,bhqnkd->bhqnk', q_blk, k_sel, preferred_element_type=jnp.float32) * scale
        s_sel = jnp.where((idx < own)[..., None], s_sel, -jnp.inf).reshape(b, h, MOBA_Q_CHUNK, n_sel * MOBA_BLOCK)
        p = jax.nn.softmax(jnp.concatenate([s_sel, s_own], axis=-1), axis=-1).astype(v.dtype)
        p_sel, p_own = p[..., :n_sel * MOBA_BLOCK], p[..., n_sel * MOBA_BLOCK:]
        return (jnp.einsum('bhqk,bhqkd->bqhd', p_sel, v_sel)
                + jnp.einsum('bhqk,bhkd->bqhd', p_own, v_own))

    out = lax.map(one_chunk, (qc, jnp.arange(nq)))
    return out.transpose(1, 0, 2, 3, 4).reshape(b, s_len, h, dh)


def heads_of(t):
    return t.reshape(t.shape[0], t.shape[1], -1, HEAD_DIM)


def mixer_window_and_stick(h, w_in, w_out, sinks, gq, gk, pos):
    b, s_len, _ = h.shape
    qa, ka, va, qb, kb, vb = split_cols(h @ w_in, (A_Q_HEADS * HEAD_DIM, A_KV_HEADS * HEAD_DIM, A_KV_HEADS * HEAD_DIM,
                                                   B_HEADS * HEAD_DIM, B_HEADS * HEAD_DIM, B_HEADS * HEAD_DIM))
    qa, ka = qk_prepare(heads_of(qa), heads_of(ka), gq, gk, pos)
    grp = A_Q_HEADS // A_KV_HEADS
    oa, _ = banded_attention(qa.reshape(b, s_len, A_KV_HEADS, grp, HEAD_DIM), ka, heads_of(va),
                             A_WINDOW - 1, sinks.reshape(A_KV_HEADS, grp))
    ob = stick_breaking_attention(heads_of(qb), heads_of(kb), heads_of(vb))
    y = jnp.concatenate([oa.reshape(b, s_len, -1), ob.reshape(b, s_len, -1)], axis=-1)
    return y @ w_out


def mixer_dilated_and_moba(h, w_in, w_out, gq_c, gk_c, gq_d, gk_d, pos):
    b, s_len, _ = h.shape
    wc, wd = C_HEADS * HEAD_DIM, D_HEADS * HEAD_DIM
    qc, kc, vc, qd, kd, vd = split_cols(h @ w_in, (wc, wc, wc, wd, wd, wd))
    qc, kc = qk_prepare(heads_of(qc), heads_of(kc), gq_c, gk_c, pos)
    qd, kd = qk_prepare(heads_of(qd), heads_of(kd), gq_d, gk_d, pos)
    oc = dilated_attention(qc, kc, heads_of(vc))
    od = moba_attention(qd, kd, heads_of(vd))
    y = jnp.concatenate([oc.reshape(b, s_len, -1), od.reshape(b, s_len, -1)], axis=-1)
    return y @ w_out


def peer_ffn(h, w_query, sub_keys, expert_u, expert_v):
    b, s_len, d = h.shape
    tokens = h.reshape(-1, PEER_TOKEN_CHUNK, d)
    half = PEER_D_KEY // 2

    def one_chunk(x_c):
        qry = (x_c @ w_query).reshape(PEER_TOKEN_CHUNK, PEER_HEADS, 2, half)
        s1 = jnp.einsum('thd,nd->thn', qry[:, :, 0], sub_keys[0], preferred_element_type=jnp.float32)
        s2 = jnp.einsum('thd,nd->thn', qry[:, :, 1], sub_keys[1], preferred_element_type=jnp.float32)
        v1, i1 = lax.top_k(s1, PEER_TOPK)
        v2, i2 = lax.top_k(s2, PEER_TOPK)
        cand = (v1[..., :, None] + v2[..., None, :]).reshape(PEER_TOKEN_CHUNK, PEER_HEADS, PEER_TOPK * PEER_TOPK)
        cand_idx = (i1[..., :, None] * PEER_N_KEYS + i2[..., None, :]).reshape(PEER_TOKEN_CHUNK, PEER_HEADS, PEER_TOPK * PEER_TOPK)
        top_s, pos = lax.top_k(cand, PEER_TOPK)
        experts = jnp.take_along_axis(cand_idx, pos, axis=-1)
        g = jax.nn.softmax(top_s, axis=-1)
        act = jax.nn.gelu(jnp.einsum('td,thkd->thk', x_c, expert_u[experts], preferred_element_type=jnp.float32),
                          approximate=False)
        return jnp.einsum('thk,thkd->td', (g * act).astype(x_c.dtype), expert_v[experts])

    out = lax.map(one_chunk, tokens)
    return out.reshape(b, s_len, d)


def setup_inputs(seed: int = 0) -> dict:
    key = jax.random.key(seed)
    ks = jax.random.split(key, 21)
    D = D_MODEL

    def nrm(k, shape, scale):
        return jax.random.normal(k, shape, jnp.float32) * scale

    return {
        'x': nrm(ks[0], (BATCH, SEQ, D), 1.0),
        'c': nrm(ks[1], (BATCH, D), 1.0),
        'ada_w': nrm(ks[2], (DEPTH, D, 6 * D), 0.5 * D ** -0.5),
        'ada_b': nrm(ks[3], (DEPTH, 6 * D), 0.02),
        'norm_mix_g': 1.0 + nrm(ks[4], (DEPTH, D), 0.02),
        'norm_ffn_g': 1.0 + nrm(ks[5], (DEPTH, D), 0.02),
        'w_in_ab': nrm(ks[6], (N_EVEN, D, AB_IN), D ** -0.5),
        'w_out_ab': nrm(ks[7], (N_EVEN, AB_OUT, D), AB_OUT ** -0.5),
        'sinks_a': nrm(ks[8], (N_EVEN, A_Q_HEADS), 1.0),
        'qnorm_a': 1.0 + nrm(ks[9], (N_EVEN, HEAD_DIM), 0.02),
        'knorm_a': 1.0 + nrm(ks[10], (N_EVEN, HEAD_DIM), 0.02),
        'w_in_cd': nrm(ks[11], (N_ODD, D, CD_IN), D ** -0.5),
        'w_out_cd': nrm(ks[12], (N_ODD, CD_OUT, D), CD_OUT ** -0.5),
        'qnorm_c': 1.0 + nrm(ks[13], (N_ODD, HEAD_DIM), 0.02),
        'knorm_c': 1.0 + nrm(ks[14], (N_ODD, HEAD_DIM), 0.02),
        'qnorm_d': 1.0 + nrm(ks[15], (N_ODD, HEAD_DIM), 0.02),
        'knorm_d': 1.0 + nrm(ks[16], (N_ODD, HEAD_DIM), 0.02),
        'peer_wq': nrm(ks[17], (DEPTH, D, PEER_HEADS * PEER_D_KEY), D ** -0.5),
        'peer_subkeys': nrm(ks[18], (DEPTH, 2, PEER_N_KEYS, PEER_D_KEY // 2), (PEER_D_KEY // 2) ** -0.5),
        'peer_u': nrm(ks[19], (DEPTH, PEER_N_EXPERTS, D), D ** -0.5),
        'peer_v': nrm(ks[20], (DEPTH, PEER_N_EXPERTS, D), 1.0),
    }


def reference(x, c, ada_w, ada_b, norm_mix_g, norm_ffn_g, w_in_ab, w_out_ab, sinks_a, qnorm_a, knorm_a,
              w_in_cd, w_out_cd, qnorm_c, knorm_c, qnorm_d, knorm_d, peer_wq, peer_subkeys, peer_u, peer_v):
    pos = jnp.arange(x.shape[1])
    cond = jax.nn.silu(c)
    for layer in range(DEPTH):
        mod = cond @ ada_w[layer] + ada_b[layer]
        shift_m, scale_m, gate_m, shift_f, scale_f, gate_f = [m[:, None, :] for m in jnp.split(mod, 6, axis=-1)]
        h = adaln(x, norm_mix_g[layer], shift_m, scale_m)
        if layer % 2 == 0:
            i = layer // 2
            y = mixer_window_and_stick(h, w_in_ab[i], w_out_ab[i], sinks_a[i], qnorm_a[i], knorm_a[i], pos)
        else:
            i = layer // 2
            y = mixer_dilated_and_moba(h, w_in_cd[i], w_out_cd[i], qnorm_c[i], knorm_c[i],
                                       qnorm_d[i], knorm_d[i], pos)
        x = x + gate_m * y
        h = adaln(x, norm_ffn_g[layer], shift_f, scale_f)
        x = x + gate_f * peer_ffn(h, peer_wq[layer], peer_subkeys[layer], peer_u[layer], peer_v[layer])
    return x
```

```python
import functools

import jax
import jax.numpy as jnp
from jax import lax
from jax.experimental import pallas as pl
from jax.experimental.pallas import tpu as pltpu

F32 = jnp.float32
BF16 = jnp.bfloat16
I32 = jnp.int32
U32 = jnp.uint32

HEAD_DIM = 64
ROPE_THETA = 10000.0
NORM_EPS = 1e-6
LANES = 128
QUERY_BLOCK = 128
A_Q_HEADS, A_KV_HEADS = 8, 2
B_HEADS = C_HEADS = D_HEADS = 8
C_PATTERNS = ((128, 1), (512, 4), (2048, 16))
MOBA_BLOCK, MOBA_TOPK = 256, 3
PEER_HEADS, PEER_N_KEYS, PEER_TOPK, PEER_D_KEY = 8, 128, 16, 256
PEER_SLOTS = PEER_HEADS * PEER_TOPK
NEG_BIG = -1e30
PEER_CAND_COUNTS = tuple(PEER_TOPK // (a + 1) for a in range(PEER_TOPK))
PEER_N_CAND = sum(PEER_CAND_COUNTS)
PEER_CAND_ROWS = 56
PACK_ROWS = 4
V7X_VMEM_LIMIT = 56 * 1024 * 1024


def _cparams(n_axes, vmem=None):
    return pltpu.CompilerParams(
        dimension_semantics=("arbitrary",) * n_axes,
        vmem_limit_bytes=vmem)


def _split_bf16(a):
    hi = a.astype(BF16)
    lo = (a - hi.astype(F32)).astype(BF16)
    return hi, lo


def _dot(a, b):
    return jnp.dot(a, b, preferred_element_type=F32)


def _dot_nt(a, b):
    return lax.dot_general(a, b, (((1,), (1,)), ((), ())), preferred_element_type=F32)


def _dot3(a, b):
    ah, al = _split_bf16(a)
    bh, bl = _split_bf16(b)
    return _dot(ah, bh) + _dot(ah, bl) + _dot(al, bh)


def _dot3_nt(a, b):
    ah, al = _split_bf16(a)
    bh, bl = _split_bf16(b)
    return _dot_nt(ah, bh) + _dot_nt(ah, bl) + _dot_nt(al, bh)


def _iota(shape, dim):
    return lax.broadcasted_iota(I32, shape, dim)


def _mod_kernel(c_ref, w_ref, b_ref, o_ref):
    c = c_ref[...]
    cond = c * jax.nn.sigmoid(c)
    o_ref[0] = _dot3(cond, w_ref[0]) + b_ref[0]


def _modulation(c, ada_w, ada_b):
    depth, d, n = ada_w.shape
    b = c.shape[0]
    tn = 1024
    return pl.pallas_call(
        _mod_kernel,
        grid=(depth, n // tn),
        in_specs=[
            pl.BlockSpec((b, d), lambda l, j: (0, 0)),
            pl.BlockSpec((1, d, tn), lambda l, j: (l, 0, j)),
            pl.BlockSpec((1, 1, tn), lambda l, j: (l, 0, j)),
        ],
        out_specs=pl.BlockSpec((1, b, tn), lambda l, j: (l, 0, j)),
        out_shape=jax.ShapeDtypeStruct((depth, b, n), F32),
        compiler_params=_cparams(2, V7X_VMEM_LIMIT),
        name="adaln_modulation",
    )(c, ada_w, ada_b.reshape(depth, 1, n))


def _adaln(x, g, shift, scale):
    ms = jnp.mean(x * x, axis=-1, keepdims=True)
    y = x * lax.rsqrt(ms + NORM_EPS) * g
    return y * (1.0 + scale) + shift


def _norm_proj_kernel(x_ref, g_ref, sh_ref, sc_ref, w_ref, o_ref):
    h = _adaln(x_ref[...], g_ref[...], sh_ref[0], sc_ref[0])
    o_ref[...] = _dot(h.astype(BF16), w_ref[...])


def _norm_proj(x2, g, shift, scale, w_bf, seq):
    t, d = x2.shape
    n = w_bf.shape[1]
    tt = 512
    per_b = seq // tt
    return pl.pallas_call(
        _norm_proj_kernel,
        grid=(t // tt,),
        in_specs=[
            pl.BlockSpec((tt, d), lambda i: (i, 0)),
            pl.BlockSpec((1, d), lambda i: (0, 0)),
            pl.BlockSpec((1, 1, d), lambda i: (i // per_b, 0, 0)),
            pl.BlockSpec((1, 1, d), lambda i: (i // per_b, 0, 0)),
            pl.BlockSpec((d, n), lambda i: (0, 0)),
        ],
        out_specs=pl.BlockSpec((tt, n), lambda i: (i, 0)),
        out_shape=jax.ShapeDtypeStruct((t, n), F32),
        compiler_params=_cparams(1, V7X_VMEM_LIMIT),
        name="adaln_in_proj",
    )(x2, g, shift, scale, w_bf)


def _out_proj_kernel(x_ref, ya_ref, yb_ref, w_ref, gate_ref, o_ref):
    half = ya_ref.shape[1]
    y = _dot(ya_ref[...].astype(BF16), w_ref[0:half, :])
    y = y + _dot(yb_ref[...].astype(BF16), w_ref[half:2 * half, :])
    o_ref[...] = x_ref[...] + gate_ref[0] * y


def _out_proj(x2, ya, yb, w_bf, gate, seq):
    t, d = x2.shape
    half = ya.shape[1]
    tt = 512
    per_b = seq // tt
    return pl.pallas_call(
        _out_proj_kernel,
        grid=(t // tt,),
        in_specs=[
            pl.BlockSpec((tt, d), lambda i: (i, 0)),
            pl.BlockSpec((tt, half), lambda i: (i, 0)),
            pl.BlockSpec((tt, half), lambda i: (i, 0)),
            pl.BlockSpec((2 * half, d), lambda i: (0, 0)),
            pl.BlockSpec((1, 1, d), lambda i: (i // per_b, 0, 0)),
        ],
        out_specs=pl.BlockSpec((tt, d), lambda i: (i, 0)),
        out_shape=jax.ShapeDtypeStruct((t, d), F32),
        compiler_params=_cparams(1, V7X_VMEM_LIMIT),
        name="mixer_out_proj",
    )(x2, ya, yb, w_bf, gate)


def _lane_row():
    return _iota((1, LANES), 1)


def _head_segment_ones():
    r = _iota((LANES, LANES), 0) // HEAD_DIM
    c = _iota((LANES, LANES), 1) // HEAD_DIM
    return jnp.where(r == c, 1.0, 0.0).astype(BF16)


def _headnorm_rope(a, g, cos, sin):
    hi, lo = _split_bf16(a * a)
    seg = _head_segment_ones()
    ms = (_dot(hi, seg) + _dot(lo, seg)) * (1.0 / HEAD_DIM)
    y = a * lax.rsqrt(ms + NORM_EPS) * g
    half = HEAD_DIM // 2
    upper = pltpu.roll(y, LANES - half, axis=1)
    lower = pltpu.roll(y, half, axis=1)
    first_half = (_lane_row() % HEAD_DIM) < half
    rot = jnp.where(first_half, -upper, lower)
    return y * cos + rot * sin


def _head_masks():
    lane = _lane_row()
    return lane < HEAD_DIM, lane >= HEAD_DIM


def _attn_specs(seq, n_pairs, qcol, kcol, vcol, kv_shared):
    blk = (1, seq, LANES)
    q_spec = pl.BlockSpec(blk, lambda b, p: (b, 0, qcol + p))
    if kv_shared:
        k_spec = pl.BlockSpec(blk, lambda b, p: (b, 0, kcol))
        v_spec = pl.BlockSpec(blk, lambda b, p: (b, 0, vcol))
    else:
        k_spec = pl.BlockSpec(blk, lambda b, p: (b, 0, kcol + p))
        v_spec = pl.BlockSpec(blk, lambda b, p: (b, 0, vcol + p))
    return q_spec, k_spec, v_spec


def _row_spec(seq):
    return pl.BlockSpec((seq, LANES), lambda b, p: (0, 0))


def _gain_spec():
    return pl.BlockSpec((1, LANES), lambda b, p: (0, 0))


def _store_heads(o_ref, r0, outs):
    first, _ = _head_masks()
    o_ref[0, pl.ds(r0, QUERY_BLOCK), :] = jnp.where(first, outs[0], outs[1])


def _swa_kernel(sinks_ref, q_ref, k_ref, v_ref, cos_ref, sin_ref, gq_ref, gk_ref, o_ref,
                q0_s, q1_s, k_s, v_s):
    p = pl.program_id(1)
    seq = q_ref.shape[1]
    cos, sin = cos_ref[...], sin_ref[...]
    first, second = _head_masks()
    qn = _headnorm_rope(q_ref[0], gq_ref[...], cos, sin) * (HEAD_DIM ** -0.5)
    q0_s[...] = jnp.where(first, qn, 0.0).astype(BF16)
    q1_s[...] = jnp.where(second, qn, 0.0).astype(BF16)
    pairs_per_kv = (A_Q_HEADS // A_KV_HEADS) // 2
    keep = jnp.logical_xor(first, (p // pairs_per_kv) == 1)
    kn = _headnorm_rope(k_ref[0], gk_ref[...], cos, sin)
    k_s[...] = jnp.where(keep, kn, pltpu.roll(kn, HEAD_DIM, axis=1)).astype(BF16)
    v = v_ref[0]
    v_s[...] = jnp.where(keep, v, pltpu.roll(v, HEAD_DIM, axis=1)).astype(BF16)

    qi = _iota((QUERY_BLOCK, QUERY_BLOCK), 0)
    ki = _iota((QUERY_BLOCK, QUERY_BLOCK), 1)

    def qblock(i, carry):
        r0 = pl.multiple_of(i * QUERY_BLOCK, QUERY_BLOCK)
        rp = pl.multiple_of(jnp.maximum(i - 1, 0) * QUERY_BLOCK, QUERY_BLOCK)
        kc, kp = k_s[pl.ds(r0, QUERY_BLOCK), :], k_s[pl.ds(rp, QUERY_BLOCK), :]
        vc, vp = v_s[pl.ds(r0, QUERY_BLOCK), :], v_s[pl.ds(rp, QUERY_BLOCK), :]
        mask_c = ki <= qi
        mask_p = jnp.logical_and(ki > qi, i > 0)
        outs = []
        for hh, q_s in enumerate((q0_s, q1_s)):
            qh = q_s[pl.ds(r0, QUERY_BLOCK), :]
            sc = jnp.where(mask_c, _dot_nt(qh, kc), NEG_BIG)
            sp = jnp.where(mask_p, _dot_nt(qh, kp), NEG_BIG)
            sink = sinks_ref[2 * p + hh]
            m = jnp.maximum(jnp.max(sc, axis=1, keepdims=True), jnp.max(sp, axis=1, keepdims=True))
            m = jnp.maximum(m, sink)
            ec, ep = jnp.exp(sc - m), jnp.exp(sp - m)
            denom = (jnp.sum(ec, axis=1, keepdims=True) + jnp.sum(ep, axis=1, keepdims=True)
                     + jnp.exp(sink - m))
            o = _dot(ec.astype(BF16), vc) + _dot(ep.astype(BF16), vp)
            outs.append(o / denom)
        _store_heads(o_ref, r0, outs)
        return carry

    lax.fori_loop(0, seq // QUERY_BLOCK, qblock, 0)


def _swa_attention(proj, sinks, cos, sin, gq, gk):
    b, seq, _ = proj.shape
    n_pairs = A_Q_HEADS // 2
    kcol = A_Q_HEADS * HEAD_DIM // LANES
    vcol = kcol + A_KV_HEADS * HEAD_DIM // LANES
    q_spec, k_spec, v_spec = _attn_specs(seq, n_pairs, 0, kcol, vcol, True)
    return pl.pallas_call(
        _swa_kernel,
        grid=(b, n_pairs),
        in_specs=[pl.BlockSpec(memory_space=pltpu.SMEM), q_spec, k_spec, v_spec,
                  _row_spec(seq), _row_spec(seq), _gain_spec(), _gain_spec()],
        out_specs=pl.BlockSpec((1, seq, LANES), lambda b_, p: (b_, 0, p)),
        out_shape=jax.ShapeDtypeStruct((b, seq, n_pairs * LANES), F32),
        scratch_shapes=[pltpu.VMEM((seq, LANES), BF16)] * 4,
        compiler_params=_cparams(2, V7X_VMEM_LIMIT),
        name="swa_gqa_attention",
    )(sinks, proj, proj, proj, cos, sin, gq, gk)


def _stick_kernel(q_ref, k_ref, v_ref, o_ref):
    seq = q_ref.shape[1]
    first, second = _head_masks()
    qi = _iota((QUERY_BLOCK, QUERY_BLOCK), 0)
    ki = _iota((QUERY_BLOCK, QUERY_BLOCK), 1)
    wr = _iota((2 * QUERY_BLOCK, 2 * QUERY_BLOCK), 0) % QUERY_BLOCK
    wc = _iota((2 * QUERY_BLOCK, 2 * QUERY_BLOCK), 1)
    suffix_w = jnp.where(jnp.logical_or(wc >= QUERY_BLOCK, wr > wc), 1.0, 0.0).astype(BF16)

    def qblock(i, carry):
        r0 = pl.multiple_of(i * QUERY_BLOCK, QUERY_BLOCK)
        q = q_ref[0, pl.ds(r0, QUERY_BLOCK), :] * (HEAD_DIM ** -0.5)
        outs = []
        for msk in (first, second):
            qh = jnp.where(msk, q, 0.0).astype(BF16)

            def kblock(jj, st):
                acc, later = st
                j = i - jj
                c0 = pl.multiple_of(j * QUERY_BLOCK, QUERY_BLOCK)
                kb = k_ref[0, pl.ds(c0, QUERY_BLOCK), :].astype(BF16)
                vb = v_ref[0, pl.ds(c0, QUERY_BLOCK), :].astype(BF16)
                past = (c0 + ki) < (r0 + qi)
                z = _dot_nt(qh, kb)
                sp = jnp.maximum(z, 0.0) + jnp.log1p(jnp.exp(-jnp.abs(z)))
                log_keep = jnp.where(past, -sp, 0.0)
                hi, lo = _split_bf16(log_keep)
                sums = _dot(jnp.concatenate([hi, lo], axis=1), suffix_w)
                inner, total = sums[:, :QUERY_BLOCK], sums[:, QUERY_BLOCK:]
                w = jnp.where(past, jnp.exp(z - sp + inner + later), 0.0)
                acc = acc + _dot(w.astype(BF16), vb)
                return acc, later + total

            zero = jnp.zeros((QUERY_BLOCK, LANES), F32)
            acc, _ = lax.fori_loop(0, i + 1, kblock, (zero, zero))
            outs.append(acc)
        _store_heads(o_ref, r0, outs)
        return carry

    lax.fori_loop(0, seq // QUERY_BLOCK, qblock, 0)


def _stick_attention(proj, qcol):
    b, seq, _ = proj.shape
    n_pairs = B_HEADS // 2
    q_spec, k_spec, v_spec = _attn_specs(seq, n_pairs, qcol, qcol + n_pairs, qcol + 2 * n_pairs, False)
    return pl.pallas_call(
        _stick_kernel,
        grid=(b, n_pairs),
        in_specs=[q_spec, k_spec, v_spec],
        out_specs=pl.BlockSpec((1, seq, LANES), lambda b_, p: (b_, 0, p)),
        out_shape=jax.ShapeDtypeStruct((b, seq, n_pairs * LANES), F32),
        compiler_params=_cparams(2, V7X_VMEM_LIMIT),
        name="stick_breaking_attention",
    )(proj, proj, proj)


def _prep_qkv(q_ref, k_ref, v_ref, cos_ref, sin_ref, gq_ref, gk_ref, q0_s, q1_s, k_s, v_s):
    cos, sin = cos_ref[...], sin_ref[...]
    first, second = _head_masks()
    qn = _headnorm_rope(q_ref[0], gq_ref[...], cos, sin) * (HEAD_DIM ** -0.5)
    q0_s[...] = jnp.where(first, qn, 0.0).astype(BF16)
    q1_s[...] = jnp.where(second, qn, 0.0).astype(BF16)
    kn = _headnorm_rope(k_ref[0], gk_ref[...], cos, sin)
    k_s[...] = kn.astype(BF16)
    v_s[...] = v_ref[0].astype(BF16)
    return qn, kn


def _dilated_kernel(q_ref, k_ref, v_ref, cos_ref, sin_ref, gq_ref, gk_ref, o_ref,
                    q0_s, q1_s, k_s, v_s):
    seq = q_ref.shape[1]
    _prep_qkv(q_ref, k_ref, v_ref, cos_ref, sin_ref, gq_ref, gk_ref, q0_s, q1_s, k_s, v_s)
    qi = _iota((QUERY_BLOCK, QUERY_BLOCK), 0)
    ki = _iota((QUERY_BLOCK, QUERY_BLOCK), 1)

    def qblock(i, carry):
        r0 = pl.multiple_of(i * QUERY_BLOCK, QUERY_BLOCK)
        outs = []
        for q_s in (q0_s, q1_s):
            qh = q_s[pl.ds(r0, QUERY_BLOCK), :]

            def kblock(j, st):
                m, l, acc = st
                c0 = pl.multiple_of(j * QUERY_BLOCK, QUERY_BLOCK)
                d = (r0 - c0) + qi - ki
                count = jnp.zeros(d.shape, F32)
                for window, dil in C_PATTERNS:
                    in_branch = jnp.logical_and(d <= window, (d & (dil - 1)) == 0)
                    count = count + jnp.where(in_branch, 1.0, 0.0)
                count = jnp.where(d >= 0, count, 0.0)
                s = jnp.where(count > 0.0, _dot_nt(qh, k_s[pl.ds(c0, QUERY_BLOCK), :]), NEG_BIG)
                m_new = jnp.maximum(m, jnp.max(s, axis=1, keepdims=True))
                pr = count * jnp.exp(s - m_new)
                alpha = jnp.exp(m - m_new)
                l = alpha * l + jnp.sum(pr, axis=1, keepdims=True)
                acc = alpha * acc + _dot(pr.astype(BF16), v_s[pl.ds(c0, QUERY_BLOCK), :])
                return m_new, l, acc

            init = (jnp.full((QUERY_BLOCK, 1), NEG_BIG, F32), jnp.zeros((QUERY_BLOCK, 1), F32),
                    jnp.zeros((QUERY_BLOCK, LANES), F32))
            _, l, acc = lax.fori_loop(0, i + 1, kblock, init)
            outs.append(acc / l)
        _store_heads(o_ref, r0, outs)
        return carry

    lax.fori_loop(0, seq // QUERY_BLOCK, qblock, 0)


def _qkv_attention_call(kernel, name, proj, qcol, n_heads, cos, sin, gq, gk, extra_scratch=()):
    b, seq, _ = proj.shape
    n_pairs = n_heads // 2
    q_spec, k_spec, v_spec = _attn_specs(seq, n_pairs, qcol, qcol + n_pairs, qcol + 2 * n_pairs, False)
    return pl.pallas_call(
        kernel,
        grid=(b, n_pairs),
        in_specs=[q_spec, k_spec, v_spec, _row_spec(seq), _row_spec(seq), _gain_spec(), _gain_spec()],
        out_specs=pl.BlockSpec((1, seq, LANES), lambda b_, p: (b_, 0, p)),
        out_shape=jax.ShapeDtypeStruct((b, seq, n_pairs * LANES), F32),
        scratch_shapes=[pltpu.VMEM((seq, LANES), BF16)] * 4 + list(extra_scratch),
        compiler_params=_cparams(2, V7X_VMEM_LIMIT),
        name=name,
    )(proj, proj, proj, cos, sin, gq, gk)


def _moba_kernel(q_ref, k_ref, v_ref, cos_ref, sin_ref, gq_ref, gk_ref, o_ref,
                 q0_s, q1_s, k_s, v_s, qf_s, km_s):
    seq = q_ref.shape[1]
    n_blocks = seq // MOBA_BLOCK
    qn, kn = _prep_qkv(q_ref, k_ref, v_ref, cos_ref, sin_ref, gq_ref, gk_ref, q0_s, q1_s, k_s, v_s)
    qf_s[...] = qn
    km_s[...] = jnp.zeros(km_s.shape, F32)
    km_s[0:n_blocks, :] = jnp.mean(kn.reshape(n_blocks, MOBA_BLOCK, LANES), axis=1)
    first, second = _head_masks()
    lane_sq = _iota((QUERY_BLOCK, LANES), 1)
    qi = _iota((QUERY_BLOCK, MOBA_BLOCK), 0)
    ki = _iota((QUERY_BLOCK, MOBA_BLOCK), 1)

    def qblock(i, carry):
        r0 = pl.multiple_of(i * QUERY_BLOCK, QUERY_BLOCK)
        own = (i * QUERY_BLOCK) // MOBA_BLOCK
        own0 = pl.multiple_of(own * MOBA_BLOCK, MOBA_BLOCK)
        qf = qf_s[pl.ds(r0, QUERY_BLOCK), :]
        outs = []
        for msk, q_s in ((first, q0_s), (second, q1_s)):
            qh = q_s[pl.ds(r0, QUERY_BLOCK), :]
            gate = _dot3_nt(jnp.where(msk, qf, 0.0), km_s[...])
            valid = lane_sq < own
            gm = jnp.where(valid, gate, -jnp.inf)
            rank = jnp.zeros((QUERY_BLOCK, LANES), F32)
            for n2 in range(n_blocks):
                g2 = gm[:, n2:n2 + 1]
                beats = jnp.logical_or(g2 > gm, jnp.logical_and(g2 == gm, n2 < lane_sq))
                rank = rank + jnp.where(jnp.logical_and(beats, n2 < own), 1.0, 0.0)
            sel = jnp.where(jnp.logical_and(valid, rank < float(MOBA_TOPK)), 1.0, 0.0)

            s = _dot_nt(qh, k_s[pl.ds(own0, MOBA_BLOCK), :])
            s = jnp.where((own0 + ki) <= (r0 + qi), s, NEG_BIG)
            m = jnp.max(s, axis=1, keepdims=True)
            pr = jnp.exp(s - m)
            l = jnp.sum(pr, axis=1, keepdims=True)
            acc = _dot(pr.astype(BF16), v_s[pl.ds(own0, MOBA_BLOCK), :])

            def kblock(n, st):
                m, l, acc = st
                c0 = pl.multiple_of(n * MOBA_BLOCK, MOBA_BLOCK)
                row_sel = jnp.sum(jnp.where(lane_sq == n, sel, 0.0), axis=1, keepdims=True) > 0.0
                s = jnp.where(row_sel, _dot_nt(qh, k_s[pl.ds(c0, MOBA_BLOCK), :]), NEG_BIG)
                m_new = jnp.maximum(m, jnp.max(s, axis=1, keepdims=True))
                pr = jnp.exp(s - m_new)
                alpha = jnp.exp(m - m_new)
                l = alpha * l + jnp.sum(pr, axis=1, keepdims=True)
                acc = alpha * acc + _dot(pr.astype(BF16), v_s[pl.ds(c0, MOBA_BLOCK), :])
                return m_new, l, acc

            _, l, acc = lax.fori_loop(0, own, kblock, (m, l, acc))
            outs.append(acc / l)
        _store_heads(o_ref, r0, outs)
        return carry

    lax.fori_loop(0, seq // QUERY_BLOCK, qblock, 0)


def _top16_rows(s, n_rows, vals_ref, idx_ref):
    rows = _iota(s.shape, 0)
    for it in range(PEER_TOPK):
        m = jnp.max(s, axis=0, keepdims=True)
        pick = jnp.min(jnp.where(s == m, rows, n_rows), axis=0, keepdims=True)
        s = jnp.where(rows == pick, -jnp.inf, s)
        vals_ref[it:it + 1, :] = m
        idx_ref[it:it + 1, :] = pick


def _peer_route_kernel(x_ref, g_ref, sh_ref, sc_ref, wqt_ref, sk_ref, h_ref, ids_ref, gts_ref,
                       q_s, v1_s, i1_s, v2_s, i2_s, cand_s, cidx_s):
    tt = x_ref.shape[0]
    n_chunks = tt // LANES
    half = PEER_D_KEY // 2
    h = _adaln(x_ref[...], g_ref[...], sh_ref[0], sc_ref[0])
    h_ref[...] = h
    qt = _dot_nt(wqt_ref[...], h.astype(BF16))
    for c in range(n_chunks):
        q_s[c] = qt[:, c * LANES:(c + 1) * LANES]
    cand_s[PEER_CAND_ROWS - 8:PEER_CAND_ROWS, :] = jnp.full((8, LANES), -jnp.inf, F32)
    cidx_s[PEER_CAND_ROWS - 8:PEER_CAND_ROWS, :] = jnp.zeros((8, LANES), I32)
    sk1, sk2 = sk_ref[0], sk_ref[1]

    def body(step, carry):
        hh = step // n_chunks
        c = step % n_chunks
        q0 = pl.multiple_of(hh * PEER_D_KEY, PEER_D_KEY)
        s1 = _dot3(sk1, q_s[c, pl.ds(q0, half), :])
        s2 = _dot3(sk2, q_s[c, pl.ds(q0 + half, half), :])
        _top16_rows(s1, PEER_N_KEYS, v1_s, i1_s)
        _top16_rows(s2, PEER_N_KEYS, v2_s, i2_s)
        r = 0
        for a, nb in enumerate(PEER_CAND_COUNTS):
            cand_s[r:r + nb, :] = v1_s[a:a + 1, :] + v2_s[0:nb, :]
            cidx_s[r:r + nb, :] = i1_s[a:a + 1, :] * PEER_N_KEYS + i2_s[0:nb, :]
            r += nb
        cand, cidx = cand_s[...], cidx_s[...]
        rows = _iota(cand.shape, 0)
        for it in range(PEER_TOPK):
            m = jnp.max(cand, axis=0, keepdims=True)
            pick = jnp.min(jnp.where(cand == m, rows, PEER_CAND_ROWS), axis=0, keepdims=True)
            hit = rows == pick
            i1_s[it:it + 1, :] = jnp.sum(jnp.where(hit, cidx, 0), axis=0, keepdims=True)
            v1_s[it:it + 1, :] = m
            cand = jnp.where(hit, -jnp.inf, cand)
        top = v1_s[...]
        e = jnp.exp(top - top[0:1, :])
        r0 = pl.multiple_of(hh * PEER_TOPK, PEER_TOPK)
        gts_ref[c, pl.ds(r0, PEER_TOPK), :] = e / jnp.sum(e, axis=0, keepdims=True)
        ids_ref[c, pl.ds(r0, PEER_TOPK), :] = i1_s[...]
        return carry

    lax.fori_loop(0, PEER_HEADS * n_chunks, body, 0)


def _peer_route(x2, g, shift, scale, wqt_bf, sub_keys, seq):
    t, d = x2.shape
    tt = 512
    per_b = seq // tt
    n_chunks = tt // LANES
    nq = wqt_bf.shape[0]
    out_blk = pl.BlockSpec((n_chunks, PEER_SLOTS, LANES), lambda i: (i, 0, 0))
    return pl.pallas_call(
        _peer_route_kernel,
        grid=(t // tt,),
        in_specs=[
            pl.BlockSpec((tt, d), lambda i: (i, 0)),
            pl.BlockSpec((1, d), lambda i: (0, 0)),
            pl.BlockSpec((1, 1, d), lambda i: (i // per_b, 0, 0)),
            pl.BlockSpec((1, 1, d), lambda i: (i // per_b, 0, 0)),
            pl.BlockSpec((nq, d), lambda i: (0, 0)),
            pl.BlockSpec(sub_keys.shape, lambda i: (0, 0, 0)),
        ],
        out_specs=[pl.BlockSpec((tt, d), lambda i: (i, 0)), out_blk, out_blk],
        out_shape=[jax.ShapeDtypeStruct((t, d), F32),
                   jax.ShapeDtypeStruct((t // LANES, PEER_SLOTS, LANES), I32),
                   jax.ShapeDtypeStruct((t // LANES, PEER_SLOTS, LANES), F32)],
        scratch_shapes=[pltpu.VMEM((n_chunks, nq, LANES), F32),
                        pltpu.VMEM((PEER_TOPK, LANES), F32), pltpu.VMEM((PEER_TOPK, LANES), I32),
                        pltpu.VMEM((PEER_TOPK, LANES), F32), pltpu.VMEM((PEER_TOPK, LANES), I32),
                        pltpu.VMEM((PEER_CAND_ROWS, LANES), F32), pltpu.VMEM((PEER_CAND_ROWS, LANES), I32)],
        compiler_params=_cparams(1, V7X_VMEM_LIMIT),
        name="peer_route",
    )(x2, g, shift, scale, wqt_bf, sub_keys)


def _pack_table(tab):
    e, d = tab.shape
    bits = lax.bitcast_convert_type(tab.astype(BF16), jnp.uint16).astype(U32)
    bits = bits.reshape(e, d // (2 * LANES), 2, LANES)
    words = bits[:, :, 0, :] | (bits[:, :, 1, :] << 16)
    return words.reshape(e * (d // (2 * LANES)), LANES)


def _table_spec(rows):
    return pl.BlockSpec((rows, LANES), lambda i: (0, 0), pipeline_mode=pl.Buffered(1))


def _gelu_exact(a):
    return 0.5 * a * (1.0 + lax.erf(a * (2.0 ** -0.5)))


def _peer_u_kernel(ids_ref, hx_ref, g_ref, tbl_ref, coef_ref, slot_s, rs_s):
    tt = hx_ref.shape[0]
    n_feat_chunks = hx_ref.shape[1]
    width = PEER_SLOTS * n_feat_chunks
    diag = (_iota((n_feat_chunks, width), 1) % n_feat_chunks) == _iota((n_feat_chunks, width), 0)

    def token(t, carry):
        base = t * PEER_SLOTS
        for j in range(PEER_SLOTS):
            row0 = pl.multiple_of(ids_ref[base + j] * PACK_ROWS, PACK_ROWS)
            slot_s[PACK_ROWS * j:PACK_ROWS * (j + 1), :] = tbl_ref[pl.ds(row0, PACK_ROWS), :]
        rows = pltpu.bitcast(slot_s[...], BF16)
        part = _dot_nt(hx_ref[t].astype(BF16), rows)
        rs_s[pl.ds(t, 1), :] = jnp.sum(jnp.where(diag, part, 0.0), axis=0, keepdims=True)
        return carry

    lax.fori_loop(0, tt, token, 0)
    group = jnp.where(_iota((width, PEER_SLOTS), 0) // n_feat_chunks == _iota((width, PEER_SLOTS), 1),
                      1.0, 0.0).astype(BF16)
    hi, lo = _split_bf16(rs_s[...])
    act = _dot(hi, group) + _dot(lo, group)
    coef_ref[...] = g_ref[...] * _gelu_exact(act)


def _peer_u(ids_flat, hx, gates, table, tt):
    t = hx.shape[0]
    return pl.pallas_call(
        _peer_u_kernel,
        grid=(t // tt,),
        in_specs=[
            pl.BlockSpec((tt * PEER_SLOTS,), lambda i: (i,), memory_space=pltpu.SMEM),
            pl.BlockSpec((tt,) + hx.shape[1:], lambda i: (i, 0, 0)),
            pl.BlockSpec((tt, PEER_SLOTS), lambda i: (i, 0)),
            _table_spec(table.shape[0]),
        ],
        out_specs=pl.BlockSpec((tt, PEER_SLOTS), lambda i: (i, 0)),
        out_shape=jax.ShapeDtypeStruct((t, PEER_SLOTS), F32),
        scratch_shapes=[pltpu.VMEM((PEER_SLOTS * PACK_ROWS, LANES), U32),
                        pltpu.VMEM((tt, PEER_SLOTS * hx.shape[1]), F32)],
        compiler_params=_cparams(1, V7X_VMEM_LIMIT),
        name="peer_expert_in",
    )(ids_flat, hx, gates, table)


def _peer_v_kernel(ids_ref, coef_ref, tbl_ref, lo_ref, hi_ref):
    tt = lo_ref.shape[0] // PACK_ROWS
    n_acc = 4

    def token(t, carry):
        base = t * PEER_SLOTS
        acc_lo = [jnp.zeros((PACK_ROWS, LANES), F32) for _ in range(n_acc)]
        acc_hi = [jnp.zeros((PACK_ROWS, LANES), F32) for _ in range(n_acc)]
        for j in range(PEER_SLOTS):
            row0 = pl.multiple_of(ids_ref[base + j] * PACK_ROWS, PACK_ROWS)
            w = coef_ref[base + j]
            words = tbl_ref[pl.ds(row0, PACK_ROWS), :]
            lo = lax.bitcast_convert_type(words << 16, F32)
            hi = lax.bitcast_convert_type(words & jnp.uint32(0xFFFF0000), F32)
            acc_lo[j % n_acc] = acc_lo[j % n_acc] + w * lo
            acc_hi[j % n_acc] = acc_hi[j % n_acc] + w * hi
        o0 = pl.multiple_of(t * PACK_ROWS, PACK_ROWS)
        lo_ref[pl.ds(o0, PACK_ROWS), :] = (acc_lo[0] + acc_lo[1]) + (acc_lo[2] + acc_lo[3])
        hi_ref[pl.ds(o0, PACK_ROWS), :] = (acc_hi[0] + acc_hi[1]) + (acc_hi[2] + acc_hi[3])
        return carry

    lax.fori_loop(0, tt, token, 0)


def _peer_v(ids_flat, coef_flat, table, t, tt):
    smem = pl.BlockSpec((tt * PEER_SLOTS,), lambda i: (i,), memory_space=pltpu.SMEM)
    out_blk = pl.BlockSpec((tt * PACK_ROWS, LANES), lambda i: (i, 0))
    return pl.pallas_call(
        _peer_v_kernel,
        grid=(t // tt,),
        in_specs=[smem, smem, _table_spec(table.shape[0])],
        out_specs=[out_blk, out_blk],
        out_shape=[jax.ShapeDtypeStruct((t * PACK_ROWS, LANES), F32)] * 2,
        compiler_params=_cparams(1, V7X_VMEM_LIMIT),
        name="peer_expert_out",
    )(ids_flat, coef_flat, table)


def _residual_kernel(x_ref, y_ref, gate_ref, o_ref):
    o_ref[...] = x_ref[...] + gate_ref[0] * y_ref[...]


def _gated_residual(x2, y2, gate, seq):
    t, d = x2.shape
    tt = 512
    per_b = seq // tt
    blk = pl.BlockSpec((tt, d), lambda i: (i, 0))
    return pl.pallas_call(
        _residual_kernel,
        grid=(t // tt,),
        in_specs=[blk, blk, pl.BlockSpec((1, 1, d), lambda i: (i // per_b, 0, 0))],
        out_specs=blk,
        out_shape=jax.ShapeDtypeStruct((t, d), F32),
        compiler_params=_cparams(1, V7X_VMEM_LIMIT),
        name="gated_residual",
    )(x2, y2, gate)


def _peer_ffn(x2, g, shift, scale, gate, wq, sub_keys, table_u, table_v, seq):
    t, d = x2.shape
    tt = 128
    h, ids, gts = _peer_route(x2, g, shift, scale, wq.T.astype(BF16), sub_keys, seq)
    ids_flat = ids.transpose(0, 2, 1).reshape(t * PEER_SLOTS)
    gates = gts.transpose(0, 2, 1).reshape(t, PEER_SLOTS)
    coef = _peer_u(ids_flat, h.reshape(t, d // LANES, LANES), gates, table_u, tt)
    lo, hi = _peer_v(ids_flat, coef.reshape(t * PEER_SLOTS), table_v, t, tt)
    y = jnp.stack([lo.reshape(t, PACK_ROWS, LANES), hi.reshape(t, PACK_ROWS, LANES)], axis=2)
    return _gated_residual(x2, y.reshape(t, d), gate, seq)


def _rope_tables(seq):
    half = HEAD_DIM // 2
    inv_freq = ROPE_THETA ** (-jnp.arange(half, dtype=F32) / half)
    ang = jnp.arange(seq).astype(F32)[:, None] * inv_freq[None, :]
    reps = LANES // half
    return jnp.tile(jnp.cos(ang), (1, reps)), jnp.tile(jnp.sin(ang), (1, reps))


def _two_heads(gain):
    return jnp.tile(gain.reshape(1, HEAD_DIM), (1, LANES // HEAD_DIM))


def kernel(x, c, ada_w, ada_b, norm_mix_g, norm_ffn_g, w_in_ab, w_out_ab, sinks_a, qnorm_a, knorm_a,
           w_in_cd, w_out_cd, qnorm_c, knorm_c, qnorm_d, knorm_d, peer_wq, peer_subkeys, peer_u, peer_v):
    b, seq, d = x.shape
    depth = ada_w.shape[0]
    t = b * seq
    cos, sin = _rope_tables(seq)
    mod = _modulation(c, ada_w, ada_b)
    x2 = x.reshape(t, d)
    for layer in range(depth):
        shift_m, scale_m, gate_m, shift_f, scale_f, gate_f = [
            m.reshape(b, 1, d) for m in jnp.split(mod[layer], 6, axis=-1)]
        g_mix = norm_mix_g[layer].reshape(1, d)
        i = layer // 2
        if layer % 2 == 0:
            proj = _norm_proj(x2, g_mix, shift_m, scale_m, w_in_ab[i].astype(BF16), seq)
            proj = proj.reshape(b, seq, -1)
            ya = _swa_attention(proj, sinks_a[i], cos, sin, _two_heads(qnorm_a[i]), _two_heads(knorm_a[i]))
            b_col = (A_Q_HEADS + 2 * A_KV_HEADS) * HEAD_DIM // LANES
            yb = _stick_attention(proj, b_col)
            w_out = w_out_ab[i]
        else:
            proj = _norm_proj(x2, g_mix, shift_m, scale_m, w_in_cd[i].astype(BF16), seq)
            proj = proj.reshape(b, seq, -1)
            ya = _qkv_attention_call(_dilated_kernel, "dilated_attention", proj, 0, C_HEADS, cos, sin,
                                     _two_heads(qnorm_c[i]), _two_heads(knorm_c[i]))
            d_col = 3 * C_HEADS * HEAD_DIM // LANES
            yb = _qkv_attention_call(_moba_kernel, "moba_attention", proj, d_col, D_HEADS, cos, sin,
                                     _two_heads(qnorm_d[i]), _two_heads(knorm_d[i]),
                                     extra_scratch=(pltpu.VMEM((seq, LANES), F32),
                                                    pltpu.VMEM((LANES, LANES), F32)))
            w_out = w_out_cd[i]
        x2 = _out_proj(x2, ya.reshape(t, -1), yb.reshape(t, -1), w_out.astype(BF16), gate_m, seq)
        x2 = _peer_ffn(x2, norm_ffn_g[layer].reshape(1, d), shift_f, scale_f, gate_f,
                       peer_wq[layer], peer_subkeys[layer],
                       _pack_table(peer_u[layer]), _pack_table(peer_v[layer]), seq)
    return x2.reshape(b, seq, d)
```

```python
import functools

import jax
import jax.numpy as jnp
from jax import lax
from jax.experimental import pallas as pl
from jax.experimental.pallas import tpu as pltpu

F32 = jnp.float32
BF16 = jnp.bfloat16
I32 = jnp.int32
U32 = jnp.uint32

HEAD_DIM = 64
ROPE_THETA = 10000.0
NORM_EPS = 1e-6
LANES = 128
QUERY_BLOCK = 128
A_Q_HEADS, A_KV_HEADS = 8, 2
B_HEADS = C_HEADS = D_HEADS = 8
C_PATTERNS = ((128, 1), (512, 4), (2048, 16))
MOBA_BLOCK, MOBA_TOPK = 256, 3
PEER_HEADS, PEER_N_KEYS, PEER_TOPK, PEER_D_KEY = 8, 128, 16, 256
PEER_SLOTS = PEER_HEADS * PEER_TOPK
NEG_BIG = -1e30
PEER_CAND_COUNTS = tuple(PEER_TOPK // (a + 1) for a in range(PEER_TOPK))
PEER_N_CAND = sum(PEER_CAND_COUNTS)
PEER_CAND_ROWS = 56
PACK_ROWS = 4
V7X_VMEM_LIMIT = 56 * 1024 * 1024


def _cparams(n_axes, vmem=None):
    return pltpu.CompilerParams(
        dimension_semantics=("arbitrary",) * n_axes,
        vmem_limit_bytes=vmem)


def _split_bf16(a):
    hi = a.astype(BF16)
    lo = (a - hi.astype(F32)).astype(BF16)
    return hi, lo


def _dot(a, b):
    return jnp.dot(a, b, preferred_element_type=F32)


def _dot_nt(a, b):
    return lax.dot_general(a, b, (((1,), (1,)), ((), ())), preferred_element_type=F32)


def _dot3(a, b):
    ah, al = _split_bf16(a)
    bh, bl = _split_bf16(b)
    return _dot(ah, bh) + _dot(ah, bl) + _dot(al, bh)


def _dot3_nt(a, b):
    ah, al = _split_bf16(a)
    bh, bl = _split_bf16(b)
    return _dot_nt(ah, bh) + _dot_nt(ah, bl) + _dot_nt(al, bh)


def _iota(shape, dim):
    return lax.broadcasted_iota(I32, shape, dim)


def _mod_kernel(c_ref, w_ref, b_ref, o_ref):
    c = c_ref[...]
    cond = c * jax.nn.sigmoid(c)
    o_ref[0] = _dot3(cond, w_ref[0]) + b_ref[0]


def _modulation(c, ada_w, ada_b):
    depth, d, n = ada_w.shape
    b = c.shape[0]
    tn = 1024
    return pl.pallas_call(
        _mod_kernel,
        grid=(depth, n // tn),
        in_specs=[
            pl.BlockSpec((b, d), lambda l, j: (0, 0)),
            pl.BlockSpec((1, d, tn), lambda l, j: (l, 0, j)),
            pl.BlockSpec((1, 1, tn), lambda l, j: (l, 0, j)),
        ],
        out_specs=pl.BlockSpec((1, b, tn), lambda l, j: (l, 0, j)),
        out_shape=jax.ShapeDtypeStruct((depth, b, n), F32),
        compiler_params=_cparams(2, V7X_VMEM_LIMIT),
        name="adaln_modulation",
    )(c, ada_w, ada_b.reshape(depth, 1, n))


def _adaln(x, g, shift, scale):
    ms = jnp.mean(x * x, axis=-1, keepdims=True)
    y = x * lax.rsqrt(ms + NORM_EPS) * g
    return y * (1.0 + scale) + shift


def _norm_proj_kernel(x_ref, g_ref, sh_ref, sc_ref, w_ref, o_ref):
    h = _adaln(x_ref[...], g_ref[...], sh_ref[0], sc_ref[0])
    o_ref[...] = _dot(h.astype(BF16), w_ref[...])


def _norm_proj(x2, g, shift, scale, w_bf, seq):
    t, d = x2.shape
    n = w_bf.shape[1]
    tt = 512
    per_b = seq // tt
    return pl.pallas_call(
        _norm_proj_kernel,
        grid=(t // tt,),
        in_specs=[
            pl.BlockSpec((tt, d), lambda i: (i, 0)),
            pl.BlockSpec((1, d), lambda i: (0, 0)),
            pl.BlockSpec((1, 1, d), lambda i: (i // per_b, 0, 0)),
            pl.BlockSpec((1, 1, d), lambda i: (i // per_b, 0, 0)),
            pl.BlockSpec((d, n), lambda i: (0, 0)),
        ],
        out_specs=pl.BlockSpec((tt, n), lambda i: (i, 0)),
        out_shape=jax.ShapeDtypeStruct((t, n), F32),
        compiler_params=_cparams(1, V7X_VMEM_LIMIT),
        name="adaln_in_proj",
    )(x2, g, shift, scale, w_bf)


def _out_proj_kernel(x_ref, ya_ref, yb_ref, w_ref, gate_ref, o_ref):
    half = ya_ref.shape[1]
    y = _dot(ya_ref[...].astype(BF16), w_ref[0:half, :])
    y = y + _dot(yb_ref[...].astype(BF16), w_ref[half:2 * half, :])
    o_ref[...] = x_ref[...] + gate_ref[0] * y


def _out_proj(x2, ya, yb, w_bf, gate, seq):
    t, d = x2.shape
    half = ya.shape[1]
    tt = 512
    per_b = seq // tt
    return pl.pallas_call(
        _out_proj_kernel,
        grid=(t // tt,),
        in_specs=[
            pl.BlockSpec((tt, d), lambda i: (i, 0)),
            pl.BlockSpec((tt, half), lambda i: (i, 0)),
            pl.BlockSpec((tt, half), lambda i: (i, 0)),
            pl.BlockSpec((2 * half, d), lambda i: (0, 0)),
            pl.BlockSpec((1, 1, d), lambda i: (i // per_b, 0, 0)),
        ],
        out_specs=pl.BlockSpec((tt, d), lambda i: (i, 0)),
        out_shape=jax.ShapeDtypeStruct((t, d), F32),
        compiler_params=_cparams(1, V7X_VMEM_LIMIT),
        name="mixer_out_proj",
    )(x2, ya, yb, w_bf, gate)


def _lane_row():
    return _iota((1, LANES), 1)


def _head_segment_ones():
    r = _iota((LANES, LANES), 0) // HEAD_DIM
    c = _iota((LANES, LANES), 1) // HEAD_DIM
    return jnp.where(r == c, 1.0, 0.0).astype(BF16)


def _headnorm_rope(a, g, cos, sin):
    hi, lo = _split_bf16(a * a)
    seg = _head_segment_ones()
    ms = (_dot(hi, seg) + _dot(lo, seg)) * (1.0 / HEAD_DIM)
    y = a * lax.rsqrt(ms + NORM_EPS) * g
    half = HEAD_DIM // 2
    upper = pltpu.roll(y, LANES - half, axis=1)
    lower = pltpu.roll(y, half, axis=1)
    first_half = (_lane_row() % HEAD_DIM) < half
    rot = jnp.where(first_half, -upper, lower)
    return y * cos + rot * sin


def _head_masks():
    lane = _lane_row()
    return lane < HEAD_DIM, lane >= HEAD_DIM


def _attn_specs(seq, n_pairs, qcol, kcol, vcol, kv_shared):
    blk = (1, seq, LANES)
    q_spec = pl.BlockSpec(blk, lambda b, p: (b, 0, qcol + p))
    if kv_shared:
        k_spec = pl.BlockSpec(blk, lambda b, p: (b, 0, kcol))
        v_spec = pl.BlockSpec(blk, lambda b, p: (b, 0, vcol))
    else:
        k_spec = pl.BlockSpec(blk, lambda b, p: (b, 0, kcol + p))
        v_spec = pl.BlockSpec(blk, lambda b, p: (b, 0, vcol + p))
    return q_spec, k_spec, v_spec


def _row_spec(seq):
    return pl.BlockSpec((seq, LANES), lambda b, p: (0, 0))


def _gain_spec():
    return pl.BlockSpec((1, LANES), lambda b, p: (0, 0))


def _store_heads(o_ref, r0, outs):
    first, _ = _head_masks()
    o_ref[0, pl.ds(r0, QUERY_BLOCK), :] = jnp.where(first, outs[0], outs[1])


def _swa_kernel(sinks_ref, q_ref, k_ref, v_ref, cos_ref, sin_ref, gq_ref, gk_ref, o_ref,
                q0_s, q1_s, k_s, v_s):
    p = pl.program_id(1)
    seq = q_ref.shape[1]
    cos, sin = cos_ref[...], sin_ref[...]
    first, second = _head_masks()
    qn = _headnorm_rope(q_ref[0], gq_ref[...], cos, sin) * (HEAD_DIM ** -0.5)
    q0_s[...] = jnp.where(first, qn, 0.0).astype(BF16)
    q1_s[...] = jnp.where(second, qn, 0.0).astype(BF16)
    pairs_per_kv = (A_Q_HEADS // A_KV_HEADS) // 2
    keep = jnp.logical_xor(first, (p // pairs_per_kv) == 1)
    kn = _headnorm_rope(k_ref[0], gk_ref[...], cos, sin)
    k_s[...] = jnp.where(keep, kn, pltpu.roll(kn, HEAD_DIM, axis=1)).astype(BF16)
    v = v_ref[0]
    v_s[...] = jnp.where(keep, v, pltpu.roll(v, HEAD_DIM, axis=1)).astype(BF16)

    qi = _iota((QUERY_BLOCK, QUERY_BLOCK), 0)
    ki = _iota((QUERY_BLOCK, QUERY_BLOCK), 1)

    def qblock(i, carry):
        r0 = pl.multiple_of(i * QUERY_BLOCK, QUERY_BLOCK)
        rp = pl.multiple_of(jnp.maximum(i - 1, 0) * QUERY_BLOCK, QUERY_BLOCK)
        kc, kp = k_s[pl.ds(r0, QUERY_BLOCK), :], k_s[pl.ds(rp, QUERY_BLOCK), :]
        vc, vp = v_s[pl.ds(r0, QUERY_BLOCK), :], v_s[pl.ds(rp, QUERY_BLOCK), :]
        mask_c = ki <= qi
        mask_p = jnp.logical_and(ki > qi, i > 0)
        outs = []
        for hh, q_s in enumerate((q0_s, q1_s)):
            qh = q_s[pl.ds(r0, QUERY_BLOCK), :]
            sc = jnp.where(mask_c, _dot_nt(qh, kc), NEG_BIG)
            sp = jnp.where(mask_p, _dot_nt(qh, kp), NEG_BIG)
            sink = sinks_ref[2 * p + hh]
            m = jnp.maximum(jnp.max(sc, axis=1, keepdims=True), jnp.max(sp, axis=1, keepdims=True))
            m = jnp.maximum(m, sink)
            ec, ep = jnp.exp(sc - m), jnp.exp(sp - m)
            denom = (jnp.sum(ec, axis=1, keepdims=True) + jnp.sum(ep, axis=1, keepdims=True)
                     + jnp.exp(sink - m))
            o = _dot(ec.astype(BF16), vc) + _dot(ep.astype(BF16), vp)
            outs.append(o / denom)
        _store_heads(o_ref, r0, outs)
        return carry

    lax.fori_loop(0, seq // QUERY_BLOCK, qblock, 0)


def _swa_attention(proj, sinks, cos, sin, gq, gk):
    b, seq, _ = proj.shape
    n_pairs = A_Q_HEADS // 2
    kcol = A_Q_HEADS * HEAD_DIM // LANES
    vcol = kcol + A_KV_HEADS * HEAD_DIM // LANES
    q_spec, k_spec, v_spec = _attn_specs(seq, n_pairs, 0, kcol, vcol, True)
    return pl.pallas_call(
        _swa_kernel,
        grid=(b, n_pairs),
        in_specs=[pl.BlockSpec(memory_space=pltpu.SMEM), q_spec, k_spec, v_spec,
                  _row_spec(seq), _row_spec(seq), _gain_spec(), _gain_spec()],
        out_specs=pl.BlockSpec((1, seq, LANES), lambda b_, p: (b_, 0, p)),
        out_shape=jax.ShapeDtypeStruct((b, seq, n_pairs * LANES), F32),
        scratch_shapes=[pltpu.VMEM((seq, LANES), BF16)] * 4,
        compiler_params=_cparams(2, V7X_VMEM_LIMIT),
        name="swa_gqa_attention",
    )(sinks, proj, proj, proj, cos, sin, gq, gk)


STICK_GROUP = 4


def _stick_kernel(q_ref, k_ref, v_ref, o_ref, k_s, v_s):
    seq = q_ref.shape[1]
    k_s[...] = k_ref[0].astype(BF16)
    v_s[...] = v_ref[0].astype(BF16)
    first, second = _head_masks()
    qi = _iota((QUERY_BLOCK, QUERY_BLOCK), 0)
    ki = _iota((QUERY_BLOCK, QUERY_BLOCK), 1)
    wr = _iota((2 * QUERY_BLOCK, 2 * QUERY_BLOCK), 0) % QUERY_BLOCK
    wc = _iota((2 * QUERY_BLOCK, 2 * QUERY_BLOCK), 1)
    suffix_w = jnp.where(jnp.logical_or(wc >= QUERY_BLOCK, wr > wc), 1.0, 0.0).astype(BF16)

    def qblock(i, carry):
        r0 = pl.multiple_of(i * QUERY_BLOCK, QUERY_BLOCK)
        q = q_ref[0, pl.ds(r0, QUERY_BLOCK), :] * (HEAD_DIM ** -0.5)
        qhs = [jnp.where(msk, q, 0.0).astype(BF16) for msk in (first, second)]

        def kgroup(g, st):
            accs, laters = [st[0], st[1]], [st[2], st[3]]
            for u in range(STICK_GROUP):
                j = i - (g * STICK_GROUP + u)
                live = j >= 0
                c0 = pl.multiple_of(jnp.maximum(j, 0) * QUERY_BLOCK, QUERY_BLOCK)
                kb = k_s[pl.ds(c0, QUERY_BLOCK), :]
                vb = v_s[pl.ds(c0, QUERY_BLOCK), :]
                past = jnp.logical_and((c0 + ki) < (r0 + qi), live)
                for hh in range(2):
                    z = _dot_nt(qhs[hh], kb)
                    sp = jnp.maximum(z, 0.0) + jnp.log1p(jnp.exp(-jnp.abs(z)))
                    log_keep = jnp.where(past, -sp, 0.0)
                    hi, lo = _split_bf16(log_keep)
                    sums = _dot(jnp.concatenate([hi, lo], axis=1), suffix_w)
                    inner, total = sums[:, :QUERY_BLOCK], sums[:, QUERY_BLOCK:]
                    w = jnp.where(past, jnp.exp(z - sp + inner + laters[hh]), 0.0)
                    accs[hh] = accs[hh] + _dot(w.astype(BF16), vb)
                    laters[hh] = laters[hh] + total
            return accs[0], accs[1], laters[0], laters[1]

        zero = jnp.zeros((QUERY_BLOCK, LANES), F32)
        st = lax.fori_loop(0, (i + STICK_GROUP) // STICK_GROUP, kgroup, (zero, zero, zero, zero))
        _store_heads(o_ref, r0, [st[0], st[1]])
        return carry

    lax.fori_loop(0, seq // QUERY_BLOCK, qblock, 0)


def _stick_attention(proj, qcol):
    b, seq, _ = proj.shape
    n_pairs = B_HEADS // 2
    q_spec, k_spec, v_spec = _attn_specs(seq, n_pairs, qcol, qcol + n_pairs, qcol + 2 * n_pairs, False)
    return pl.pallas_call(
        _stick_kernel,
        grid=(b, n_pairs),
        in_specs=[q_spec, k_spec, v_spec],
        out_specs=pl.BlockSpec((1, seq, LANES), lambda b_, p: (b_, 0, p)),
        out_shape=jax.ShapeDtypeStruct((b, seq, n_pairs * LANES), F32),
        scratch_shapes=[pltpu.VMEM((seq, LANES), BF16)] * 2,
        compiler_params=_cparams(2, V7X_VMEM_LIMIT),
        name="stick_breaking_attention",
    )(proj, proj, proj)


def _prep_qkv(q_ref, k_ref, v_ref, cos_ref, sin_ref, gq_ref, gk_ref, q0_s, q1_s, k_s, v_s):
    cos, sin = cos_ref[...], sin_ref[...]
    first, second = _head_masks()
    qn = _headnorm_rope(q_ref[0], gq_ref[...], cos, sin) * (HEAD_DIM ** -0.5)
    q0_s[...] = jnp.where(first, qn, 0.0).astype(BF16)
    q1_s[...] = jnp.where(second, qn, 0.0).astype(BF16)
    kn = _headnorm_rope(k_ref[0], gk_ref[...], cos, sin)
    k_s[...] = kn.astype(BF16)
    v_s[...] = v_ref[0].astype(BF16)
    return qn, kn


DILATED_KEY_TILE = 512


def _dilated_kernel(q_ref, k_ref, v_ref, cos_ref, sin_ref, gq_ref, gk_ref, o_ref,
                    q0_s, q1_s, k_s, v_s):
    seq = q_ref.shape[1]
    _prep_qkv(q_ref, k_ref, v_ref, cos_ref, sin_ref, gq_ref, gk_ref, q0_s, q1_s, k_s, v_s)
    kt = min(DILATED_KEY_TILE, seq)
    qk = _iota((QUERY_BLOCK, kt), 0) - _iota((QUERY_BLOCK, kt), 1)
    on_stride = [jnp.where((qk & (dil - 1)) == 0, 1.0, 0.0) for _, dil in C_PATTERNS]

    def qblock(i, carry):
        r0 = pl.multiple_of(i * QUERY_BLOCK, QUERY_BLOCK)
        qhs = [q_s[pl.ds(r0, QUERY_BLOCK), :] for q_s in (q0_s, q1_s)]

        def ktile(g, st):
            c0 = pl.multiple_of(g * kt, kt)
            d = (r0 - c0) + qk
            count = jnp.zeros(d.shape, F32)
            for (window, _), stride_ok in zip(C_PATTERNS, on_stride):
                count = count + jnp.where(d <= window, stride_ok, 0.0)
            count = jnp.where(d >= 0, count, 0.0)
            kb, vb = k_s[pl.ds(c0, kt), :], v_s[pl.ds(c0, kt), :]
            new = []
            for hh in range(2):
                m, l, acc = st[3 * hh:3 * hh + 3]
                s = jnp.where(count > 0.0, _dot_nt(qhs[hh], kb), NEG_BIG)
                m_new = jnp.maximum(m, jnp.max(s, axis=1, keepdims=True))
                pr = count * jnp.exp(s - m_new)
                alpha = jnp.exp(m - m_new)
                l = alpha * l + jnp.sum(pr, axis=1, keepdims=True)
                acc = alpha * acc + _dot(pr.astype(BF16), vb)
                new += [m_new, l, acc]
            return tuple(new)

        init = (jnp.full((QUERY_BLOCK, 1), NEG_BIG, F32), jnp.zeros((QUERY_BLOCK, 1), F32),
                jnp.zeros((QUERY_BLOCK, LANES), F32)) * 2
        st = lax.fori_loop(0, (r0 + QUERY_BLOCK + kt - 1) // kt, ktile, init)
        _store_heads(o_ref, r0, [st[2] / st[1], st[5] / st[4]])
        return carry

    lax.fori_loop(0, seq // QUERY_BLOCK, qblock, 0)


def _qkv_attention_call(kernel, name, proj, qcol, n_heads, cos, sin, gq, gk, extra_scratch=()):
    b, seq, _ = proj.shape
    n_pairs = n_heads // 2
    q_spec, k_spec, v_spec = _attn_specs(seq, n_pairs, qcol, qcol + n_pairs, qcol + 2 * n_pairs, False)
    return pl.pallas_call(
        kernel,
        grid=(b, n_pairs),
        in_specs=[q_spec, k_spec, v_spec, _row_spec(seq), _row_spec(seq), _gain_spec(), _gain_spec()],
        out_specs=pl.BlockSpec((1, seq, LANES), lambda b_, p: (b_, 0, p)),
        out_shape=jax.ShapeDtypeStruct((b, seq, n_pairs * LANES), F32),
        scratch_shapes=[pltpu.VMEM((seq, LANES), BF16)] * 4 + list(extra_scratch),
        compiler_params=_cparams(2, V7X_VMEM_LIMIT),
        name=name,
    )(proj, proj, proj, cos, sin, gq, gk)


def _moba_kernel(q_ref, k_ref, v_ref, cos_ref, sin_ref, gq_ref, gk_ref, o_ref,
                 q0_s, q1_s, k_s, v_s, km_s, sel0_s, sel1_s):
    seq = q_ref.shape[1]
    n_blocks = seq // MOBA_BLOCK
    qn, kn = _prep_qkv(q_ref, k_ref, v_ref, cos_ref, sin_ref, gq_ref, gk_ref, q0_s, q1_s, k_s, v_s)
    km_s[...] = jnp.zeros(km_s.shape, F32)
    km_s[0:n_blocks, :] = jnp.mean(kn.reshape(n_blocks, MOBA_BLOCK, LANES), axis=1)
    first, second = _head_masks()

    rows8 = _iota((8, seq), 0)
    own8 = _iota((8, seq), 1) // MOBA_BLOCK
    valid = rows8 < own8
    for msk, sel_s in ((first, sel0_s), (second, sel1_s)):
        gate = _dot3_nt(km_s[...], jnp.where(msk, qn, 0.0))[0:8, :]
        gm = jnp.where(valid, gate, -jnp.inf)
        rank = jnp.zeros((8, seq), F32)
        for n2 in range(n_blocks):
            g2 = gm[n2:n2 + 1, :]
            beats = jnp.logical_or(g2 > gm, jnp.logical_and(g2 == gm, n2 < rows8))
            rank = rank + jnp.where(jnp.logical_and(beats, n2 < own8), 1.0, 0.0)
        sel = jnp.where(jnp.logical_and(valid, rank < float(MOBA_TOPK)), 1.0, 0.0)
        sel = jnp.concatenate([sel, jnp.zeros((LANES - 8, seq), F32)], axis=0)
        sel_s[...] = sel.T

    lane_sq = _iota((QUERY_BLOCK, LANES), 1)
    qi = _iota((QUERY_BLOCK, MOBA_BLOCK), 0)
    ki = _iota((QUERY_BLOCK, MOBA_BLOCK), 1)
    second_block = _iota((QUERY_BLOCK, 2 * MOBA_BLOCK), 1) >= MOBA_BLOCK

    def qblock(i, carry):
        r0 = pl.multiple_of(i * QUERY_BLOCK, QUERY_BLOCK)
        own = (i * QUERY_BLOCK) // MOBA_BLOCK
        own0 = pl.multiple_of(own * MOBA_BLOCK, MOBA_BLOCK)
        qhs = [q_s[pl.ds(r0, QUERY_BLOCK), :] for q_s in (q0_s, q1_s)]
        sels = [sel_s[pl.ds(r0, QUERY_BLOCK), :] for sel_s in (sel0_s, sel1_s)]
        kb, vb = k_s[pl.ds(own0, MOBA_BLOCK), :], v_s[pl.ds(own0, MOBA_BLOCK), :]
        causal = (own0 + ki) <= (r0 + qi)
        init = []
        for hh in range(2):
            s = jnp.where(causal, _dot_nt(qhs[hh], kb), NEG_BIG)
            m = jnp.max(s, axis=1, keepdims=True)
            pr = jnp.exp(s - m)
            init += [m, jnp.sum(pr, axis=1, keepdims=True), _dot(pr.astype(BF16), vb)]

        def kpair(g, st):
            c0 = pl.multiple_of(g * 2 * MOBA_BLOCK, 2 * MOBA_BLOCK)
            kb2, vb2 = k_s[pl.ds(c0, 2 * MOBA_BLOCK), :], v_s[pl.ds(c0, 2 * MOBA_BLOCK), :]
            new = []
            for hh in range(2):
                m, l, acc = st[3 * hh:3 * hh + 3]
                sel_a = jnp.sum(jnp.where(lane_sq == 2 * g, sels[hh], 0.0), axis=1, keepdims=True)
                sel_b = jnp.sum(jnp.where(lane_sq == 2 * g + 1, sels[hh], 0.0), axis=1, keepdims=True)
                keep = jnp.where(second_block, sel_b, sel_a) > 0.0
                s = jnp.where(keep, _dot_nt(qhs[hh], kb2), NEG_BIG)
                m_new = jnp.maximum(m, jnp.max(s, axis=1, keepdims=True))
                pr = jnp.exp(s - m_new)
                alpha = jnp.exp(m - m_new)
                l = alpha * l + jnp.sum(pr, axis=1, keepdims=True)
                acc = alpha * acc + _dot(pr.astype(BF16), vb2)
                new += [m_new, l, acc]
            return tuple(new)

        st = lax.fori_loop(0, (own + 1) // 2, kpair, tuple(init))
        _store_heads(o_ref, r0, [st[2] / st[1], st[5] / st[4]])
        return carry

    lax.fori_loop(0, seq // QUERY_BLOCK, qblock, 0)


def _top16_rows(s, n_rows, vals_ref, idx_ref):
    rows = _iota(s.shape, 0)
    for it in range(PEER_TOPK):
        m = jnp.max(s, axis=0, keepdims=True)
        pick = jnp.min(jnp.where(s == m, rows, n_rows), axis=0, keepdims=True)
        s = jnp.where(rows == pick, -jnp.inf, s)
        vals_ref[it:it + 1, :] = m
        idx_ref[it:it + 1, :] = pick


def _peer_route_kernel(x_ref, g_ref, sh_ref, sc_ref, wqt_ref, sk_ref, h_ref, ids_ref, gts_ref,
                       q_s, v1_s, i1_s, v2_s, i2_s, cand_s, cidx_s):
    tt = x_ref.shape[0]
    n_chunks = tt // LANES
    half = PEER_D_KEY // 2
    h = _adaln(x_ref[...], g_ref[...], sh_ref[0], sc_ref[0])
    h_ref[...] = h
    qt = _dot_nt(wqt_ref[...], h.astype(BF16))
    for c in range(n_chunks):
        q_s[c] = qt[:, c * LANES:(c + 1) * LANES]
    cand_s[PEER_CAND_ROWS - 8:PEER_CAND_ROWS, :] = jnp.full((8, LANES), -jnp.inf, F32)
    cidx_s[PEER_CAND_ROWS - 8:PEER_CAND_ROWS, :] = jnp.zeros((8, LANES), I32)
    sk1, sk2 = sk_ref[0], sk_ref[1]

    def body(step, carry):
        hh = step // n_chunks
        c = step % n_chunks
        q0 = pl.multiple_of(hh * PEER_D_KEY, PEER_D_KEY)
        s1 = _dot3(sk1, q_s[c, pl.ds(q0, half), :])
        s2 = _dot3(sk2, q_s[c, pl.ds(q0 + half, half), :])
        _top16_rows(s1, PEER_N_KEYS, v1_s, i1_s)
        _top16_rows(s2, PEER_N_KEYS, v2_s, i2_s)
        r = 0
        for a, nb in enumerate(PEER_CAND_COUNTS):
            cand_s[r:r + nb, :] = v1_s[a:a + 1, :] + v2_s[0:nb, :]
            cidx_s[r:r + nb, :] = i1_s[a:a + 1, :] * PEER_N_KEYS + i2_s[0:nb, :]
            r += nb
        cand, cidx = cand_s[...], cidx_s[...]
        rows = _iota(cand.shape, 0)
        for it in range(PEER_TOPK):
            m = jnp.max(cand, axis=0, keepdims=True)
            pick = jnp.min(jnp.where(cand == m, rows, PEER_CAND_ROWS), axis=0, keepdims=True)
            hit = rows == pick
            i1_s[it:it + 1, :] = jnp.sum(jnp.where(hit, cidx, 0), axis=0, keepdims=True)
            v1_s[it:it + 1, :] = m
            cand = jnp.where(hit, -jnp.inf, cand)
        top = v1_s[...]
        e = jnp.exp(top - top[0:1, :])
        r0 = pl.multiple_of(hh * PEER_TOPK, PEER_TOPK)
        gts_ref[c, pl.ds(r0, PEER_TOPK), :] = e / jnp.sum(e, axis=0, keepdims=True)
        ids_ref[c, pl.ds(r0, PEER_TOPK), :] = i1_s[...] * PACK_ROWS
        return carry

    lax.fori_loop(0, PEER_HEADS * n_chunks, body, 0)


def _peer_route(x2, g, shift, scale, wqt_bf, sub_keys, seq):
    t, d = x2.shape
    tt = 512
    per_b = seq // tt
    n_chunks = tt // LANES
    nq = wqt_bf.shape[0]
    out_blk = pl.BlockSpec((n_chunks, PEER_SLOTS, LANES), lambda i: (i, 0, 0))
    return pl.pallas_call(
        _peer_route_kernel,
        grid=(t // tt,),
        in_specs=[
            pl.BlockSpec((tt, d), lambda i: (i, 0)),
            pl.BlockSpec((1, d), lambda i: (0, 0)),
            pl.BlockSpec((1, 1, d), lambda i: (i // per_b, 0, 0)),
            pl.BlockSpec((1, 1, d), lambda i: (i // per_b, 0, 0)),
            pl.BlockSpec((nq, d), lambda i: (0, 0)),
            pl.BlockSpec(sub_keys.shape, lambda i: (0, 0, 0)),
        ],
        out_specs=[pl.BlockSpec((tt, d), lambda i: (i, 0)), out_blk, out_blk],
        out_shape=[jax.ShapeDtypeStruct((t, d), F32),
                   jax.ShapeDtypeStruct((t // LANES, PEER_SLOTS, LANES), I32),
                   jax.ShapeDtypeStruct((t // LANES, PEER_SLOTS, LANES), F32)],
        scratch_shapes=[pltpu.VMEM((n_chunks, nq, LANES), F32),
                        pltpu.VMEM((PEER_TOPK, LANES), F32), pltpu.VMEM((PEER_TOPK, LANES), I32),
                        pltpu.VMEM((PEER_TOPK, LANES), F32), pltpu.VMEM((PEER_TOPK, LANES), I32),
                        pltpu.VMEM((PEER_CAND_ROWS, LANES), F32), pltpu.VMEM((PEER_CAND_ROWS, LANES), I32)],
        compiler_params=_cparams(1, V7X_VMEM_LIMIT),
        name="peer_route",
    )(x2, g, shift, scale, wqt_bf, sub_keys)


def _pack_table(tab):
    e, d = tab.shape
    bits = lax.bitcast_convert_type(tab.astype(BF16), jnp.uint16).astype(U32)
    bits = bits.reshape(e, d // (2 * LANES), 2, LANES)
    words = bits[:, :, 0, :] | (bits[:, :, 1, :] << 16)
    return words.reshape(e * (d // (2 * LANES)), LANES)


def _table_spec(rows):
    return pl.BlockSpec((rows, LANES), lambda i: (0, 0), pipeline_mode=pl.Buffered(1))


def _gelu_exact(a):
    return 0.5 * a * (1.0 + lax.erf(a * (2.0 ** -0.5)))


FEAT_CHUNKS = 8
SLOT_WIDTH = PEER_SLOTS * FEAT_CHUNKS


def _gather_rows(ids_ref, base, tbl_ref, slot):
    for j in range(PEER_SLOTS):
        row0 = pl.multiple_of(ids_ref[base + j], PACK_ROWS)
        slot[PACK_ROWS * j:PACK_ROWS * (j + 1), :] = tbl_ref[pl.ds(row0, PACK_ROWS), :]


def _pipelined_tokens(tt, ids_ref, tbl_ref, slots, compute):
    s_a, s_b, s_c, s_d = slots
    last = tt - 1

    def gather(t, slot):
        _gather_rows(ids_ref, jnp.minimum(t, last) * PEER_SLOTS, tbl_ref, slot)

    gather(0, s_a)
    gather(1, s_b)

    def quad(q, carry):
        t = 4 * q
        compute(t, s_a)
        compute(t + 1, s_b)
        gather(t + 2, s_c)
        gather(t + 3, s_d)
        compute(t + 2, s_c)
        compute(t + 3, s_d)
        gather(t + 4, s_a)
        gather(t + 5, s_b)
        return carry

    lax.fori_loop(0, tt // 4, quad, 0)


def _chunk_diag():
    return (_iota((FEAT_CHUNKS, SLOT_WIDTH), 1) % FEAT_CHUNKS) == _iota((FEAT_CHUNKS, SLOT_WIDTH), 0)


def _peer_u_kernel(ids_ref, h_ref, g_ref, tbl_ref, coef_ref, *scratch):
    slots, (hx_s, rs_s) = scratch[:N_SLOTS], scratch[N_SLOTS:]
    tt = h_ref.shape[0]
    for c in range(FEAT_CHUNKS):
        hx_s[pl.ds(c, tt, stride=FEAT_CHUNKS), :] = h_ref[:, c * LANES:(c + 1) * LANES]
    diag = _chunk_diag()

    def compute(t, slot):
        rows = pltpu.bitcast(slot[...], BF16)
        x8 = hx_s[pl.ds(pl.multiple_of(t * FEAT_CHUNKS, FEAT_CHUNKS), FEAT_CHUNKS), :]
        part = _dot_nt(x8.astype(BF16), rows)
        rs_s[pl.ds(t, 1), :] = jnp.sum(jnp.where(diag, part, 0.0), axis=0, keepdims=True)

    _pipelined_tokens(tt, ids_ref, tbl_ref, slots, compute)
    group = jnp.where(_iota((SLOT_WIDTH, PEER_SLOTS), 0) // FEAT_CHUNKS == _iota((SLOT_WIDTH, PEER_SLOTS), 1),
                      1.0, 0.0).astype(BF16)
    hi, lo = _split_bf16(rs_s[...])
    act = _dot(hi, group) + _dot(lo, group)
    coef_ref[...] = g_ref[...] * _gelu_exact(act)


N_SLOTS = 4


def _slot_scratch():
    return [pltpu.VMEM((PEER_SLOTS * PACK_ROWS, LANES), U32)] * N_SLOTS


def _peer_u(ids_flat, h, gates, table, tt):
    t, d = h.shape
    return pl.pallas_call(
        _peer_u_kernel,
        grid=(t // tt,),
        in_specs=[
            pl.BlockSpec((tt * PEER_SLOTS,), lambda i: (i,), memory_space=pltpu.SMEM),
            pl.BlockSpec((tt, d), lambda i: (i, 0)),
            pl.BlockSpec((tt, PEER_SLOTS), lambda i: (i, 0)),
            _table_spec(table.shape[0]),
        ],
        out_specs=pl.BlockSpec((tt, PEER_SLOTS), lambda i: (i, 0)),
        out_shape=jax.ShapeDtypeStruct((t, PEER_SLOTS), F32),
        scratch_shapes=_slot_scratch() + [
                        pltpu.VMEM((tt * FEAT_CHUNKS, LANES), F32),
                        pltpu.VMEM((tt, SLOT_WIDTH), F32)],
        compiler_params=_cparams(1, V7X_VMEM_LIMIT),
        name="peer_expert_in",
    )(ids_flat, h, gates, table)


def _peer_v_kernel(ids_ref, coef_ref, x_ref, gate_ref, tbl_ref, o_ref, *scratch):
    slots, (ce_hi_s, ce_lo_s, res_s) = scratch[:N_SLOTS], scratch[N_SLOTS:]
    tt = x_ref.shape[0]
    spread = jnp.where(_iota((PEER_SLOTS, SLOT_WIDTH), 1) // FEAT_CHUNKS == _iota((PEER_SLOTS, SLOT_WIDTH), 0),
                       1.0, 0.0).astype(BF16)
    hi, lo = _split_bf16(coef_ref[...])
    ce_hi_s[...] = _dot(hi, spread)
    ce_lo_s[...] = _dot(lo, spread)
    diag = _chunk_diag()

    def compute(t, slot):
        rows = pltpu.bitcast(slot[...], BF16)
        a_hi = jnp.where(diag, ce_hi_s[pl.ds(t, 1), :], 0.0)
        a_lo = jnp.where(diag, ce_lo_s[pl.ds(t, 1), :], 0.0)
        both = _dot(jnp.concatenate([a_hi, a_lo], axis=0).astype(BF16), rows)
        r0 = pl.multiple_of(t * FEAT_CHUNKS, FEAT_CHUNKS)
        res_s[pl.ds(r0, FEAT_CHUNKS), :] = both[0:FEAT_CHUNKS, :] + both[FEAT_CHUNKS:2 * FEAT_CHUNKS, :]

    _pipelined_tokens(tt, ids_ref, tbl_ref, slots, compute)
    for c in range(FEAT_CHUNKS):
        cols = slice(c * LANES, (c + 1) * LANES)
        y = res_s[pl.ds(c, tt, stride=FEAT_CHUNKS), :]
        o_ref[:, cols] = x_ref[:, cols] + gate_ref[0][:, cols] * y


def _peer_v(ids_flat, coef, x2, gate, table, seq, tt):
    t, d = x2.shape
    per_b = seq // tt
    blk = pl.BlockSpec((tt, d), lambda i: (i, 0))
    return pl.pallas_call(
        _peer_v_kernel,
        grid=(t // tt,),
        in_specs=[
            pl.BlockSpec((tt * PEER_SLOTS,), lambda i: (i,), memory_space=pltpu.SMEM),
            pl.BlockSpec((tt, PEER_SLOTS), lambda i: (i, 0)),
            blk,
            pl.BlockSpec((1, 1, d), lambda i: (i // per_b, 0, 0)),
            _table_spec(table.shape[0]),
        ],
        out_specs=blk,
        out_shape=jax.ShapeDtypeStruct((t, d), F32),
        scratch_shapes=_slot_scratch() + [
                        pltpu.VMEM((tt, SLOT_WIDTH), F32), pltpu.VMEM((tt, SLOT_WIDTH), F32),
                        pltpu.VMEM((tt * FEAT_CHUNKS, LANES), F32)],
        compiler_params=_cparams(1, V7X_VMEM_LIMIT),
        name="peer_expert_out",
    )(ids_flat, coef, x2, gate, table)


def _peer_ffn(x2, g, shift, scale, gate, wq, sub_keys, table_u, table_v, seq):
    t, d = x2.shape
    tt = 128
    h, ids, gts = _peer_route(x2, g, shift, scale, wq.T.astype(BF16), sub_keys, seq)
    ids_flat = ids.transpose(0, 2, 1).reshape(t * PEER_SLOTS)
    gates = gts.transpose(0, 2, 1).reshape(t, PEER_SLOTS)
    coef = _peer_u(ids_flat, h, gates, table_u, tt)
    return _peer_v(ids_flat, coef, x2, gate, table_v, seq, tt)


def _rope_tables(seq):
    half = HEAD_DIM // 2
    inv_freq = ROPE_THETA ** (-jnp.arange(half, dtype=F32) / half)
    ang = jnp.arange(seq).astype(F32)[:, None] * inv_freq[None, :]
    reps = LANES // half
    return jnp.tile(jnp.cos(ang), (1, reps)), jnp.tile(jnp.sin(ang), (1, reps))


def _two_heads(gain):
    return jnp.tile(gain.reshape(1, HEAD_DIM), (1, LANES // HEAD_DIM))


def kernel(x, c, ada_w, ada_b, norm_mix_g, norm_ffn_g, w_in_ab, w_out_ab, sinks_a, qnorm_a, knorm_a,
           w_in_cd, w_out_cd, qnorm_c, knorm_c, qnorm_d, knorm_d, peer_wq, peer_subkeys, peer_u, peer_v):
    b, seq, d = x.shape
    depth = ada_w.shape[0]
    t = b * seq
    cos, sin = _rope_tables(seq)
    mod = _modulation(c, ada_w, ada_b)
    x2 = x.reshape(t, d)
    for layer in range(depth):
        shift_m, scale_m, gate_m, shift_f, scale_f, gate_f = [
            m.reshape(b, 1, d) for m in jnp.split(mod[layer], 6, axis=-1)]
        g_mix = norm_mix_g[layer].reshape(1, d)
        i = layer // 2
        if layer % 2 == 0:
            proj = _norm_proj(x2, g_mix, shift_m, scale_m, w_in_ab[i].astype(BF16), seq)
            proj = proj.reshape(b, seq, -1)
            ya = _swa_attention(proj, sinks_a[i], cos, sin, _two_heads(qnorm_a[i]), _two_heads(knorm_a[i]))
            b_col = (A_Q_HEADS + 2 * A_KV_HEADS) * HEAD_DIM // LANES
            yb = _stick_attention(proj, b_col)
            w_out = w_out_ab[i]
        else:
            proj = _norm_proj(x2, g_mix, shift_m, scale_m, w_in_cd[i].astype(BF16), seq)
            proj = proj.reshape(b, seq, -1)
            ya = _qkv_attention_call(_dilated_kernel, "dilated_attention", proj, 0, C_HEADS, cos, sin,
                                     _two_heads(qnorm_c[i]), _two_heads(knorm_c[i]))
            d_col = 3 * C_HEADS * HEAD_DIM // LANES
            yb = _qkv_attention_call(_moba_kernel, "moba_attention", proj, d_col, D_HEADS, cos, sin,
                                     _two_heads(qnorm_d[i]), _two_heads(knorm_d[i]),
                                     extra_scratch=(pltpu.VMEM((LANES, LANES), F32),
                                                    pltpu.VMEM((seq, LANES), F32),
                                                    pltpu.VMEM((seq, LANES), F32)))
            w_out = w_out_cd[i]
        x2 = _out_proj(x2, ya.reshape(t, -1), yb.reshape(t, -1), w_out.astype(BF16), gate_m, seq)
        x2 = _peer_ffn(x2, norm_ffn_g[layer].reshape(1, d), shift_f, scale_f, gate_f,
                       peer_wq[layer], peer_subkeys[layer],
                       _pack_table(peer_u[layer]), _pack_table(peer_v[layer]), seq)
    return x2.reshape(b, seq, d)
```

```python
import functools

import jax
import jax.numpy as jnp
from jax import lax
from jax.experimental import pallas as pl
from jax.experimental.pallas import tpu as pltpu

F32 = jnp.float32
BF16 = jnp.bfloat16
I32 = jnp.int32
U32 = jnp.uint32

HEAD_DIM = 64
ROPE_THETA = 10000.0
NORM_EPS = 1e-6
LANES = 128
QUERY_BLOCK = 128
A_Q_HEADS, A_KV_HEADS = 8, 2
B_HEADS = C_HEADS = D_HEADS = 8
C_PATTERNS = ((128, 1), (512, 4), (2048, 16))
MOBA_BLOCK, MOBA_TOPK = 256, 3
PEER_HEADS, PEER_N_KEYS, PEER_TOPK, PEER_D_KEY = 8, 128, 16, 256
PEER_SLOTS = PEER_HEADS * PEER_TOPK
NEG_BIG = -1e30
PEER_CAND_COUNTS = tuple(PEER_TOPK // (a + 1) for a in range(PEER_TOPK))
PEER_N_CAND = sum(PEER_CAND_COUNTS)
PEER_CAND_ROWS = 56
PACK_ROWS = 4
V7X_VMEM_LIMIT = 56 * 1024 * 1024


def _cparams(n_axes, vmem=None):
    return pltpu.CompilerParams(
        dimension_semantics=("arbitrary",) * n_axes,
        vmem_limit_bytes=vmem)


def _split_bf16(a):
    hi = a.astype(BF16)
    lo = (a - hi.astype(F32)).astype(BF16)
    return hi, lo


def _dot(a, b):
    return jnp.dot(a, b, preferred_element_type=F32)


def _dot_nt(a, b):
    return lax.dot_general(a, b, (((1,), (1,)), ((), ())), preferred_element_type=F32)


def _dot3(a, b):
    ah, al = _split_bf16(a)
    bh, bl = _split_bf16(b)
    return _dot(ah, bh) + _dot(ah, bl) + _dot(al, bh)


def _dot3_nt(a, b):
    ah, al = _split_bf16(a)
    bh, bl = _split_bf16(b)
    return _dot_nt(ah, bh) + _dot_nt(ah, bl) + _dot_nt(al, bh)


def _iota(shape, dim):
    return lax.broadcasted_iota(I32, shape, dim)


def _mod_kernel(c_ref, w_ref, b_ref, o_ref):
    c = c_ref[...]
    cond = c * jax.nn.sigmoid(c)
    o_ref[0] = _dot3(cond, w_ref[0]) + b_ref[0]


def _modulation(c, ada_w, ada_b):
    depth, d, n = ada_w.shape
    b = c.shape[0]
    tn = 1024
    return pl.pallas_call(
        _mod_kernel,
        grid=(depth, n // tn),
        in_specs=[
            pl.BlockSpec((b, d), lambda l, j: (0, 0)),
            pl.BlockSpec((1, d, tn), lambda l, j: (l, 0, j)),
            pl.BlockSpec((1, 1, tn), lambda l, j: (l, 0, j)),
        ],
        out_specs=pl.BlockSpec((1, b, tn), lambda l, j: (l, 0, j)),
        out_shape=jax.ShapeDtypeStruct((depth, b, n), F32),
        compiler_params=_cparams(2, V7X_VMEM_LIMIT),
        name="adaln_modulation",
    )(c, ada_w, ada_b.reshape(depth, 1, n))


def _adaln(x, g, shift, scale):
    ms = jnp.mean(x * x, axis=-1, keepdims=True)
    y = x * lax.rsqrt(ms + NORM_EPS) * g
    return y * (1.0 + scale) + shift


def _norm_proj_kernel(x_ref, g_ref, sh_ref, sc_ref, w_ref, o_ref):
    h = _adaln(x_ref[...], g_ref[...], sh_ref[0], sc_ref[0])
    o_ref[...] = _dot(h.astype(BF16), w_ref[...])


def _norm_proj(x2, g, shift, scale, w_bf, seq):
    t, d = x2.shape
    n = w_bf.shape[1]
    tt = 512
    per_b = seq // tt
    return pl.pallas_call(
        _norm_proj_kernel,
        grid=(t // tt,),
        in_specs=[
            pl.BlockSpec((tt, d), lambda i: (i, 0)),
            pl.BlockSpec((1, d), lambda i: (0, 0)),
            pl.BlockSpec((1, 1, d), lambda i: (i // per_b, 0, 0)),
            pl.BlockSpec((1, 1, d), lambda i: (i // per_b, 0, 0)),
            pl.BlockSpec((d, n), lambda i: (0, 0)),
        ],
        out_specs=pl.BlockSpec((tt, n), lambda i: (i, 0)),
        out_shape=jax.ShapeDtypeStruct((t, n), F32),
        compiler_params=_cparams(1, V7X_VMEM_LIMIT),
        name="adaln_in_proj",
    )(x2, g, shift, scale, w_bf)


def _out_proj_kernel(x_ref, ya_ref, yb_ref, w_ref, gate_ref, o_ref):
    half = ya_ref.shape[1]
    y = _dot(ya_ref[...].astype(BF16), w_ref[0:half, :])
    y = y + _dot(yb_ref[...].astype(BF16), w_ref[half:2 * half, :])
    o_ref[...] = x_ref[...] + gate_ref[0] * y


def _out_proj(x2, ya, yb, w_bf, gate, seq):
    t, d = x2.shape
    half = ya.shape[1]
    tt = 512
    per_b = seq // tt
    return pl.pallas_call(
        _out_proj_kernel,
        grid=(t // tt,),
        in_specs=[
            pl.BlockSpec((tt, d), lambda i: (i, 0)),
            pl.BlockSpec((tt, half), lambda i: (i, 0)),
            pl.BlockSpec((tt, half), lambda i: (i, 0)),
            pl.BlockSpec((2 * half, d), lambda i: (0, 0)),
            pl.BlockSpec((1, 1, d), lambda i: (i // per_b, 0, 0)),
        ],
        out_specs=pl.BlockSpec((tt, d), lambda i: (i, 0)),
        out_shape=jax.ShapeDtypeStruct((t, d), F32),
        compiler_params=_cparams(1, V7X_VMEM_LIMIT),
        name="mixer_out_proj",
    )(x2, ya, yb, w_bf, gate)


def _lane_row():
    return _iota((1, LANES), 1)


def _head_segment_ones():
    r = _iota((LANES, LANES), 0) // HEAD_DIM
    c = _iota((LANES, LANES), 1) // HEAD_DIM
    return jnp.where(r == c, 1.0, 0.0).astype(BF16)


def _headnorm_rope(a, g, cos, sin):
    hi, lo = _split_bf16(a * a)
    seg = _head_segment_ones()
    ms = (_dot(hi, seg) + _dot(lo, seg)) * (1.0 / HEAD_DIM)
    y = a * lax.rsqrt(ms + NORM_EPS) * g
    half = HEAD_DIM // 2
    upper = pltpu.roll(y, LANES - half, axis=1)
    lower = pltpu.roll(y, half, axis=1)
    first_half = (_lane_row() % HEAD_DIM) < half
    rot = jnp.where(first_half, -upper, lower)
    return y * cos + rot * sin


def _head_masks():
    lane = _lane_row()
    return lane < HEAD_DIM, lane >= HEAD_DIM


def _attn_specs(seq, n_pairs, qcol, kcol, vcol, kv_shared):
    blk = (1, seq, LANES)
    q_spec = pl.BlockSpec(blk, lambda b, p: (b, 0, qcol + p))
    if kv_shared:
        k_spec = pl.BlockSpec(blk, lambda b, p: (b, 0, kcol))
        v_spec = pl.BlockSpec(blk, lambda b, p: (b, 0, vcol))
    else:
        k_spec = pl.BlockSpec(blk, lambda b, p: (b, 0, kcol + p))
        v_spec = pl.BlockSpec(blk, lambda b, p: (b, 0, vcol + p))
    return q_spec, k_spec, v_spec


def _row_spec(seq):
    return pl.BlockSpec((seq, LANES), lambda b, p: (0, 0))


def _gain_spec():
    return pl.BlockSpec((1, LANES), lambda b, p: (0, 0))


def _store_heads(o_ref, r0, outs):
    first, _ = _head_masks()
    o_ref[0, pl.ds(r0, QUERY_BLOCK), :] = jnp.where(first, outs[0], outs[1])


def _swa_kernel(sinks_ref, q_ref, k_ref, v_ref, cos_ref, sin_ref, gq_ref, gk_ref, o_ref,
                q0_s, q1_s, k_s, v_s):
    p = pl.program_id(1)
    seq = q_ref.shape[1]
    cos, sin = cos_ref[...], sin_ref[...]
    first, second = _head_masks()
    qn = _headnorm_rope(q_ref[0], gq_ref[...], cos, sin) * (HEAD_DIM ** -0.5)
    q0_s[...] = jnp.where(first, qn, 0.0).astype(BF16)
    q1_s[...] = jnp.where(second, qn, 0.0).astype(BF16)
    pairs_per_kv = (A_Q_HEADS // A_KV_HEADS) // 2
    keep = jnp.logical_xor(first, (p // pairs_per_kv) == 1)
    kn = _headnorm_rope(k_ref[0], gk_ref[...], cos, sin)
    k_s[...] = jnp.where(keep, kn, pltpu.roll(kn, HEAD_DIM, axis=1)).astype(BF16)
    v = v_ref[0]
    v_s[...] = jnp.where(keep, v, pltpu.roll(v, HEAD_DIM, axis=1)).astype(BF16)

    qi = _iota((QUERY_BLOCK, QUERY_BLOCK), 0)
    ki = _iota((QUERY_BLOCK, QUERY_BLOCK), 1)

    def qblock(i, carry):
        r0 = pl.multiple_of(i * QUERY_BLOCK, QUERY_BLOCK)
        rp = pl.multiple_of(jnp.maximum(i - 1, 0) * QUERY_BLOCK, QUERY_BLOCK)
        kc, kp = k_s[pl.ds(r0, QUERY_BLOCK), :], k_s[pl.ds(rp, QUERY_BLOCK), :]
        vc, vp = v_s[pl.ds(r0, QUERY_BLOCK), :], v_s[pl.ds(rp, QUERY_BLOCK), :]
        mask_c = ki <= qi
        mask_p = jnp.logical_and(ki > qi, i > 0)
        outs = []
        for hh, q_s in enumerate((q0_s, q1_s)):
            qh = q_s[pl.ds(r0, QUERY_BLOCK), :]
            sc = jnp.where(mask_c, _dot_nt(qh, kc), NEG_BIG)
            sp = jnp.where(mask_p, _dot_nt(qh, kp), NEG_BIG)
            sink = sinks_ref[2 * p + hh]
            m = jnp.maximum(jnp.max(sc, axis=1, keepdims=True), jnp.max(sp, axis=1, keepdims=True))
            m = jnp.maximum(m, sink)
            ec, ep = jnp.exp(sc - m), jnp.exp(sp - m)
            denom = (jnp.sum(ec, axis=1, keepdims=True) + jnp.sum(ep, axis=1, keepdims=True)
                     + jnp.exp(sink - m))
            o = _dot(ec.astype(BF16), vc) + _dot(ep.astype(BF16), vp)
            outs.append(o / denom)
        _store_heads(o_ref, r0, outs)
        return carry

    lax.fori_loop(0, seq // QUERY_BLOCK, qblock, 0)


def _swa_attention(proj, sinks, cos, sin, gq, gk):
    b, seq, _ = proj.shape
    n_pairs = A_Q_HEADS // 2
    kcol = A_Q_HEADS * HEAD_DIM // LANES
    vcol = kcol + A_KV_HEADS * HEAD_DIM // LANES
    q_spec, k_spec, v_spec = _attn_specs(seq, n_pairs, 0, kcol, vcol, True)
    return pl.pallas_call(
        _swa_kernel,
        grid=(b, n_pairs),
        in_specs=[pl.BlockSpec(memory_space=pltpu.SMEM), q_spec, k_spec, v_spec,
                  _row_spec(seq), _row_spec(seq), _gain_spec(), _gain_spec()],
        out_specs=pl.BlockSpec((1, seq, LANES), lambda b_, p: (b_, 0, p)),
        out_shape=jax.ShapeDtypeStruct((b, seq, n_pairs * LANES), F32),
        scratch_shapes=[pltpu.VMEM((seq, LANES), BF16)] * 4,
        compiler_params=_cparams(2, V7X_VMEM_LIMIT),
        name="swa_gqa_attention",
    )(sinks, proj, proj, proj, cos, sin, gq, gk)


STICK_GROUP = 4


def _stick_kernel(q_ref, k_ref, v_ref, o_ref, k_s, v_s):
    seq = q_ref.shape[1]
    k_s[...] = k_ref[0].astype(BF16)
    v_s[...] = v_ref[0].astype(BF16)
    first, second = _head_masks()
    qi = _iota((QUERY_BLOCK, QUERY_BLOCK), 0)
    ki = _iota((QUERY_BLOCK, QUERY_BLOCK), 1)
    wr = _iota((2 * QUERY_BLOCK, 2 * QUERY_BLOCK), 0) % QUERY_BLOCK
    wc = _iota((2 * QUERY_BLOCK, 2 * QUERY_BLOCK), 1)
    suffix_w = jnp.where(jnp.logical_or(wc >= QUERY_BLOCK, wr > wc), 1.0, 0.0).astype(BF16)

    def qblock(i, carry):
        r0 = pl.multiple_of(i * QUERY_BLOCK, QUERY_BLOCK)
        q = q_ref[0, pl.ds(r0, QUERY_BLOCK), :] * (HEAD_DIM ** -0.5)
        qhs = [jnp.where(msk, q, 0.0).astype(BF16) for msk in (first, second)]

        def kgroup(g, st):
            accs, laters = [st[0], st[1]], [st[2], st[3]]
            chains = [(u, hh) for u in range(STICK_GROUP) for hh in range(2)]
            vbs, pasts, zs = [], [], {}
            for u in range(STICK_GROUP):
                j = i - (g * STICK_GROUP + u)
                live = j >= 0
                c0 = pl.multiple_of(jnp.maximum(j, 0) * QUERY_BLOCK, QUERY_BLOCK)
                kb = k_s[pl.ds(c0, QUERY_BLOCK), :]
                vbs.append(v_s[pl.ds(c0, QUERY_BLOCK), :])
                pasts.append(jnp.logical_and((c0 + ki) < (r0 + qi), live))
                for hh in range(2):
                    zs[u, hh] = _dot_nt(qhs[hh], kb)
            logit, sums = {}, {}
            for u, hh in chains:
                z = zs[u, hh]
                sp = jnp.maximum(z, 0.0) + jnp.log1p(jnp.exp(-jnp.abs(z)))
                log_keep = jnp.where(pasts[u], -sp, 0.0)
                logit[u, hh] = z - sp
                hi, lo = _split_bf16(log_keep)
                sums[u, hh] = _dot(jnp.concatenate([hi, lo], axis=1), suffix_w)
            ws = {}
            for u, hh in chains:
                inner, total = sums[u, hh][:, :QUERY_BLOCK], sums[u, hh][:, QUERY_BLOCK:]
                ws[u, hh] = jnp.where(pasts[u], jnp.exp(logit[u, hh] + inner + laters[hh]), 0.0).astype(BF16)
                laters[hh] = laters[hh] + total
            for u, hh in chains:
                accs[hh] = accs[hh] + _dot(ws[u, hh], vbs[u])
            return accs[0], accs[1], laters[0], laters[1]

        zero = jnp.zeros((QUERY_BLOCK, LANES), F32)
        st = lax.fori_loop(0, (i + STICK_GROUP) // STICK_GROUP, kgroup, (zero, zero, zero, zero))
        _store_heads(o_ref, r0, [st[0], st[1]])
        return carry

    lax.fori_loop(0, seq // QUERY_BLOCK, qblock, 0)


def _stick_attention(proj, qcol):
    b, seq, _ = proj.shape
    n_pairs = B_HEADS // 2
    q_spec, k_spec, v_spec = _attn_specs(seq, n_pairs, qcol, qcol + n_pairs, qcol + 2 * n_pairs, False)
    return pl.pallas_call(
        _stick_kernel,
        grid=(b, n_pairs),
        in_specs=[q_spec, k_spec, v_spec],
        out_specs=pl.BlockSpec((1, seq, LANES), lambda b_, p: (b_, 0, p)),
        out_shape=jax.ShapeDtypeStruct((b, seq, n_pairs * LANES), F32),
        scratch_shapes=[pltpu.VMEM((seq, LANES), BF16)] * 2,
        compiler_params=_cparams(2, V7X_VMEM_LIMIT),
        name="stick_breaking_attention",
    )(proj, proj, proj)


def _prep_qkv(q_ref, k_ref, v_ref, cos_ref, sin_ref, gq_ref, gk_ref, q0_s, q1_s, k_s, v_s):
    cos, sin = cos_ref[...], sin_ref[...]
    first, second = _head_masks()
    qn = _headnorm_rope(q_ref[0], gq_ref[...], cos, sin) * (HEAD_DIM ** -0.5)
    q0_s[...] = jnp.where(first, qn, 0.0).astype(BF16)
    q1_s[...] = jnp.where(second, qn, 0.0).astype(BF16)
    kn = _headnorm_rope(k_ref[0], gk_ref[...], cos, sin)
    k_s[...] = kn.astype(BF16)
    v_s[...] = v_ref[0].astype(BF16)
    return qn, kn


DILATED_KEY_TILE = 512


def _dilated_kernel(q_ref, k_ref, v_ref, cos_ref, sin_ref, gq_ref, gk_ref, o_ref,
                    q0_s, q1_s, k_s, v_s):
    seq = q_ref.shape[1]
    _prep_qkv(q_ref, k_ref, v_ref, cos_ref, sin_ref, gq_ref, gk_ref, q0_s, q1_s, k_s, v_s)
    kt = min(DILATED_KEY_TILE, seq)
    qk = _iota((QUERY_BLOCK, kt), 0) - _iota((QUERY_BLOCK, kt), 1)
    on_stride = [jnp.where((qk & (dil - 1)) == 0, 1.0, 0.0) for _, dil in C_PATTERNS]

    def qblock(i, carry):
        r0 = pl.multiple_of(i * QUERY_BLOCK, QUERY_BLOCK)
        qhs = [q_s[pl.ds(r0, QUERY_BLOCK), :] for q_s in (q0_s, q1_s)]

        def ktile(g, st):
            c0 = pl.multiple_of(g * kt, kt)
            d = (r0 - c0) + qk
            count = jnp.zeros(d.shape, F32)
            for (window, _), stride_ok in zip(C_PATTERNS, on_stride):
                count = count + jnp.where(d <= window, stride_ok, 0.0)
            count = jnp.where(d >= 0, count, 0.0)
            kb, vb = k_s[pl.ds(c0, kt), :], v_s[pl.ds(c0, kt), :]
            scores = [_dot_nt(qhs[hh], kb) for hh in range(2)]
            new, prs = [], []
            for hh in range(2):
                m, l, acc = st[3 * hh:3 * hh + 3]
                s = jnp.where(count > 0.0, scores[hh], NEG_BIG)
                m_new = jnp.maximum(m, jnp.max(s, axis=1, keepdims=True))
                pr = count * jnp.exp(s - m_new)
                alpha = jnp.exp(m - m_new)
                new += [m_new, alpha * l + jnp.sum(pr, axis=1, keepdims=True), alpha * acc]
                prs.append(pr.astype(BF16))
            for hh in range(2):
                new[3 * hh + 2] = new[3 * hh + 2] + _dot(prs[hh], vb)
            return tuple(new)

        init = (jnp.full((QUERY_BLOCK, 1), NEG_BIG, F32), jnp.zeros((QUERY_BLOCK, 1), F32),
                jnp.zeros((QUERY_BLOCK, LANES), F32)) * 2
        st = lax.fori_loop(0, (r0 + QUERY_BLOCK + kt - 1) // kt, ktile, init)
        _store_heads(o_ref, r0, [st[2] / st[1], st[5] / st[4]])
        return carry

    lax.fori_loop(0, seq // QUERY_BLOCK, qblock, 0)


def _qkv_attention_call(kernel, name, proj, qcol, n_heads, cos, sin, gq, gk, extra_scratch=()):
    b, seq, _ = proj.shape
    n_pairs = n_heads // 2
    q_spec, k_spec, v_spec = _attn_specs(seq, n_pairs, qcol, qcol + n_pairs, qcol + 2 * n_pairs, False)
    return pl.pallas_call(
        kernel,
        grid=(b, n_pairs),
        in_specs=[q_spec, k_spec, v_spec, _row_spec(seq), _row_spec(seq), _gain_spec(), _gain_spec()],
        out_specs=pl.BlockSpec((1, seq, LANES), lambda b_, p: (b_, 0, p)),
        out_shape=jax.ShapeDtypeStruct((b, seq, n_pairs * LANES), F32),
        scratch_shapes=[pltpu.VMEM((seq, LANES), BF16)] * 4 + list(extra_scratch),
        compiler_params=_cparams(2, V7X_VMEM_LIMIT),
        name=name,
    )(proj, proj, proj, cos, sin, gq, gk)


def _moba_kernel(q_ref, k_ref, v_ref, cos_ref, sin_ref, gq_ref, gk_ref, o_ref,
                 q0_s, q1_s, k_s, v_s, km_s, sel0_s, sel1_s):
    seq = q_ref.shape[1]
    n_blocks = seq // MOBA_BLOCK
    qn, kn = _prep_qkv(q_ref, k_ref, v_ref, cos_ref, sin_ref, gq_ref, gk_ref, q0_s, q1_s, k_s, v_s)
    km_s[...] = jnp.zeros(km_s.shape, F32)
    km_s[0:n_blocks, :] = jnp.mean(kn.reshape(n_blocks, MOBA_BLOCK, LANES), axis=1)
    first, second = _head_masks()

    rows8 = _iota((8, seq), 0)
    own8 = _iota((8, seq), 1) // MOBA_BLOCK
    valid = rows8 < own8
    for msk, sel_s in ((first, sel0_s), (second, sel1_s)):
        gate = _dot3_nt(km_s[...], jnp.where(msk, qn, 0.0))[0:8, :]
        gm = jnp.where(valid, gate, -jnp.inf)
        rank = jnp.zeros((8, seq), F32)
        for n2 in range(n_blocks):
            g2 = gm[n2:n2 + 1, :]
            beats = jnp.logical_or(g2 > gm, jnp.logical_and(g2 == gm, n2 < rows8))
            rank = rank + jnp.where(jnp.logical_and(beats, n2 < own8), 1.0, 0.0)
        sel = jnp.where(jnp.logical_and(valid, rank < float(MOBA_TOPK)), 1.0, 0.0)
        sel = jnp.concatenate([sel, jnp.zeros((LANES - 8, seq), F32)], axis=0)
        sel_s[...] = sel.T

    lane_sq = _iota((QUERY_BLOCK, LANES), 1)
    qi = _iota((QUERY_BLOCK, MOBA_BLOCK), 0)
    ki = _iota((QUERY_BLOCK, MOBA_BLOCK), 1)
    second_block = _iota((QUERY_BLOCK, 2 * MOBA_BLOCK), 1) >= MOBA_BLOCK

    def qblock(i, carry):
        r0 = pl.multiple_of(i * QUERY_BLOCK, QUERY_BLOCK)
        own = (i * QUERY_BLOCK) // MOBA_BLOCK
        own0 = pl.multiple_of(own * MOBA_BLOCK, MOBA_BLOCK)
        qhs = [q_s[pl.ds(r0, QUERY_BLOCK), :] for q_s in (q0_s, q1_s)]
        sels = [sel_s[pl.ds(r0, QUERY_BLOCK), :] for sel_s in (sel0_s, sel1_s)]
        kb, vb = k_s[pl.ds(own0, MOBA_BLOCK), :], v_s[pl.ds(own0, MOBA_BLOCK), :]
        causal = (own0 + ki) <= (r0 + qi)
        scores = [_dot_nt(qhs[hh], kb) for hh in range(2)]
        init, prs = [], []
        for hh in range(2):
            s = jnp.where(causal, scores[hh], NEG_BIG)
            m = jnp.max(s, axis=1, keepdims=True)
            pr = jnp.exp(s - m)
            init += [m, jnp.sum(pr, axis=1, keepdims=True), None]
            prs.append(pr.astype(BF16))
        for hh in range(2):
            init[3 * hh + 2] = _dot(prs[hh], vb)

        def kpair(g, st):
            c0 = pl.multiple_of(g * 2 * MOBA_BLOCK, 2 * MOBA_BLOCK)
            kb2, vb2 = k_s[pl.ds(c0, 2 * MOBA_BLOCK), :], v_s[pl.ds(c0, 2 * MOBA_BLOCK), :]
            scores = [_dot_nt(qhs[hh], kb2) for hh in range(2)]
            new, prs = [], []
            for hh in range(2):
                m, l, acc = st[3 * hh:3 * hh + 3]
                sel_a = jnp.sum(jnp.where(lane_sq == 2 * g, sels[hh], 0.0), axis=1, keepdims=True)
                sel_b = jnp.sum(jnp.where(lane_sq == 2 * g + 1, sels[hh], 0.0), axis=1, keepdims=True)
                keep = jnp.where(second_block, sel_b, sel_a) > 0.0
                s = jnp.where(keep, scores[hh], NEG_BIG)
                m_new = jnp.maximum(m, jnp.max(s, axis=1, keepdims=True))
                pr = jnp.exp(s - m_new)
                alpha = jnp.exp(m - m_new)
                new += [m_new, alpha * l + jnp.sum(pr, axis=1, keepdims=True), alpha * acc]
                prs.append(pr.astype(BF16))
            for hh in range(2):
                new[3 * hh + 2] = new[3 * hh + 2] + _dot(prs[hh], vb2)
            return tuple(new)

        st = lax.fori_loop(0, (own + 1) // 2, kpair, tuple(init))
        _store_heads(o_ref, r0, [st[2] / st[1], st[5] / st[4]])
        return carry

    lax.fori_loop(0, seq // QUERY_BLOCK, qblock, 0)


ROUTE_CHUNKS_PER_STEP = 2


def _top16_rows(scores, n_rows, vals_refs, idx_refs):
    scores = list(scores)
    rows = _iota(scores[0].shape, 0)
    for it in range(PEER_TOPK):
        for k, s in enumerate(scores):
            m = jnp.max(s, axis=0, keepdims=True)
            pick = jnp.min(jnp.where(s == m, rows, n_rows), axis=0, keepdims=True)
            scores[k] = jnp.where(rows == pick, -jnp.inf, s)
            vals_refs[k][it:it + 1, :] = m
            idx_refs[k][it:it + 1, :] = pick


def _peer_route_kernel(x_ref, g_ref, sh_ref, sc_ref, wqt_ref, sk_ref, h_ref, ids_ref, gts_ref,
                       q_s, val_s, idx_s, cand_s, cidx_s):
    tt = x_ref.shape[0]
    n_chunks = tt // LANES
    per_step = ROUTE_CHUNKS_PER_STEP
    half = PEER_D_KEY // 2
    h = _adaln(x_ref[...], g_ref[...], sh_ref[0], sc_ref[0])
    h_ref[...] = h
    qt = _dot_nt(wqt_ref[...], h.astype(BF16))
    for c in range(n_chunks):
        q_s[c] = qt[:, c * LANES:(c + 1) * LANES]
    for k in range(per_step):
        cand_s[k, PEER_CAND_ROWS - 8:PEER_CAND_ROWS, :] = jnp.full((8, LANES), -jnp.inf, F32)
        cidx_s[k, PEER_CAND_ROWS - 8:PEER_CAND_ROWS, :] = jnp.zeros((8, LANES), I32)
    sk1, sk2 = sk_ref[0], sk_ref[1]
    vals = [val_s.at[i] for i in range(2 * per_step)]
    idxs = [idx_s.at[i] for i in range(2 * per_step)]

    def body(step, carry):
        hh = step // (n_chunks // per_step)
        c0 = (step % (n_chunks // per_step)) * per_step
        q0 = pl.multiple_of(hh * PEER_D_KEY, PEER_D_KEY)
        scores = []
        for k in range(per_step):
            scores.append(_dot3(sk1, q_s[c0 + k, pl.ds(q0, half), :]))
            scores.append(_dot3(sk2, q_s[c0 + k, pl.ds(q0 + half, half), :]))
        _top16_rows(scores, PEER_N_KEYS, vals, idxs)
        cands, cidxs = [], []
        for k in range(per_step):
            v1, i1, v2, i2 = vals[2 * k], idxs[2 * k], vals[2 * k + 1], idxs[2 * k + 1]
            r = 0
            for a, nb in enumerate(PEER_CAND_COUNTS):
                cand_s[k, r:r + nb, :] = v1[a:a + 1, :] + v2[0:nb, :]
                cidx_s[k, r:r + nb, :] = i1[a:a + 1, :] * PEER_N_KEYS + i2[0:nb, :]
                r += nb
            cands.append(cand_s[k])
            cidxs.append(cidx_s[k])
        rows = _iota(cands[0].shape, 0)
        for it in range(PEER_TOPK):
            for k in range(per_step):
                m = jnp.max(cands[k], axis=0, keepdims=True)
                pick = jnp.min(jnp.where(cands[k] == m, rows, PEER_CAND_ROWS), axis=0, keepdims=True)
                hit = rows == pick
                idxs[2 * k][it:it + 1, :] = jnp.sum(jnp.where(hit, cidxs[k], 0), axis=0, keepdims=True)
                vals[2 * k][it:it + 1, :] = m
                cands[k] = jnp.where(hit, -jnp.inf, cands[k])
        r0 = pl.multiple_of(hh * PEER_TOPK, PEER_TOPK)
        for k in range(per_step):
            top = vals[2 * k][...]
            e = jnp.exp(top - top[0:1, :])
            gts_ref[c0 + k, pl.ds(r0, PEER_TOPK), :] = e / jnp.sum(e, axis=0, keepdims=True)
            ids_ref[c0 + k, pl.ds(r0, PEER_TOPK), :] = idxs[2 * k][...] * PACK_ROWS
        return carry

    lax.fori_loop(0, PEER_HEADS * n_chunks // per_step, body, 0)


def _peer_route(x2, g, shift, scale, wqt_bf, sub_keys, seq):
    t, d = x2.shape
    tt = 512
    per_b = seq // tt
    n_chunks = tt // LANES
    nq = wqt_bf.shape[0]
    out_blk = pl.BlockSpec((n_chunks, PEER_SLOTS, LANES), lambda i: (i, 0, 0))
    return pl.pallas_call(
        _peer_route_kernel,
        grid=(t // tt,),
        in_specs=[
            pl.BlockSpec((tt, d), lambda i: (i, 0)),
            pl.BlockSpec((1, d), lambda i: (0, 0)),
            pl.BlockSpec((1, 1, d), lambda i: (i // per_b, 0, 0)),
            pl.BlockSpec((1, 1, d), lambda i: (i // per_b, 0, 0)),
            pl.BlockSpec((nq, d), lambda i: (0, 0)),
            pl.BlockSpec(sub_keys.shape, lambda i: (0, 0, 0)),
        ],
        out_specs=[pl.BlockSpec((tt, d), lambda i: (i, 0)), out_blk, out_blk],
        out_shape=[jax.ShapeDtypeStruct((t, d), F32),
                   jax.ShapeDtypeStruct((t // LANES, PEER_SLOTS, LANES), I32),
                   jax.ShapeDtypeStruct((t // LANES, PEER_SLOTS, LANES), F32)],
        scratch_shapes=[pltpu.VMEM((n_chunks, nq, LANES), F32),
                        pltpu.VMEM((2 * ROUTE_CHUNKS_PER_STEP, PEER_TOPK, LANES), F32),
                        pltpu.VMEM((2 * ROUTE_CHUNKS_PER_STEP, PEER_TOPK, LANES), I32),
                        pltpu.VMEM((ROUTE_CHUNKS_PER_STEP, PEER_CAND_ROWS, LANES), F32),
                        pltpu.VMEM((ROUTE_CHUNKS_PER_STEP, PEER_CAND_ROWS, LANES), I32)],
        compiler_params=_cparams(1, V7X_VMEM_LIMIT),
        name="peer_route",
    )(x2, g, shift, scale, wqt_bf, sub_keys)


def _pack_table(tab):
    e, d = tab.shape
    bits = lax.bitcast_convert_type(tab.astype(BF16), jnp.uint16).astype(U32)
    bits = bits.reshape(e, d // (2 * LANES), 2, LANES)
    words = bits[:, :, 0, :] | (bits[:, :, 1, :] << 16)
    return words.reshape(e * (d // (2 * LANES)), LANES)


def _table_spec(rows):
    return pl.BlockSpec((rows, LANES), lambda i: (0, 0), pipeline_mode=pl.Buffered(1))


def _gelu_exact(a):
    return 0.5 * a * (1.0 + lax.erf(a * (2.0 ** -0.5)))


FEAT_CHUNKS = 8
SLOT_WIDTH = PEER_SLOTS * FEAT_CHUNKS


def _gather_rows(ids_ref, base, tbl_ref, slot):
    for j in range(PEER_SLOTS):
        row0 = pl.multiple_of(ids_ref[base + j], PACK_ROWS)
        slot[PACK_ROWS * j:PACK_ROWS * (j + 1), :] = tbl_ref[pl.ds(row0, PACK_ROWS), :]


def _pipelined_tokens(tt, ids_ref, tbl_ref, slots, compute):
    group = len(slots) // 2
    first, second = slots[:group], slots[group:]
    last = tt - 1

    def gather(t, slot):
        _gather_rows(ids_ref, jnp.minimum(t, last) * PEER_SLOTS, tbl_ref, slot)

    for k in range(group):
        gather(k, first[k])

    def step(q, carry):
        t = 2 * group * q
        for k in range(group):
            compute(t + k, first[k])
        for k in range(group):
            gather(t + group + k, second[k])
        for k in range(group):
            compute(t + group + k, second[k])
        for k in range(group):
            gather(t + 2 * group + k, first[k])
        return carry

    lax.fori_loop(0, tt // (2 * group), step, 0)


def _chunk_diag():
    return (_iota((FEAT_CHUNKS, SLOT_WIDTH), 1) % FEAT_CHUNKS) == _iota((FEAT_CHUNKS, SLOT_WIDTH), 0)


def _peer_u_kernel(ids_ref, h_ref, g_ref, tbl_ref, coef_ref, *scratch):
    slots, (hx_s, rs_s) = scratch[:N_SLOTS], scratch[N_SLOTS:]
    tt = h_ref.shape[0]
    for c in range(FEAT_CHUNKS):
        hx_s[pl.ds(c, tt, stride=FEAT_CHUNKS), :] = h_ref[:, c * LANES:(c + 1) * LANES]
    diag = _chunk_diag()

    def compute(t, slot):
        rows = pltpu.bitcast(slot[...], BF16)
        x8 = hx_s[pl.ds(pl.multiple_of(t * FEAT_CHUNKS, FEAT_CHUNKS), FEAT_CHUNKS), :]
        part = _dot_nt(x8.astype(BF16), rows)
        rs_s[pl.ds(t, 1), :] = jnp.sum(jnp.where(diag, part, 0.0), axis=0, keepdims=True)

    _pipelined_tokens(tt, ids_ref, tbl_ref, slots, compute)
    group = jnp.where(_iota((SLOT_WIDTH, PEER_SLOTS), 0) // FEAT_CHUNKS == _iota((SLOT_WIDTH, PEER_SLOTS), 1),
                      1.0, 0.0).astype(BF16)
    hi, lo = _split_bf16(rs_s[...])
    act = _dot(hi, group) + _dot(lo, group)
    coef_ref[...] = g_ref[...] * _gelu_exact(act)


N_SLOTS = 8


def _slot_scratch():
    return [pltpu.VMEM((PEER_SLOTS * PACK_ROWS, LANES), U32)] * N_SLOTS


def _peer_u(ids_flat, h, gates, table, tt):
    t, d = h.shape
    return pl.pallas_call(
        _peer_u_kernel,
        grid=(t // tt,),
        in_specs=[
            pl.BlockSpec((tt * PEER_SLOTS,), lambda i: (i,), memory_space=pltpu.SMEM),
            pl.BlockSpec((tt, d), lambda i: (i, 0)),
            pl.BlockSpec((tt, PEER_SLOTS), lambda i: (i, 0)),
            _table_spec(table.shape[0]),
        ],
        out_specs=pl.BlockSpec((tt, PEER_SLOTS), lambda i: (i, 0)),
        out_shape=jax.ShapeDtypeStruct((t, PEER_SLOTS), F32),
        scratch_shapes=_slot_scratch() + [
                        pltpu.VMEM((tt * FEAT_CHUNKS, LANES), F32),
                        pltpu.VMEM((tt, SLOT_WIDTH), F32)],
        compiler_params=_cparams(1, V7X_VMEM_LIMIT),
        name="peer_expert_in",
    )(ids_flat, h, gates, table)


def _peer_v_kernel(ids_ref, coef_ref, x_ref, gate_ref, tbl_ref, o_ref, *scratch):
    slots, (ce_hi_s, ce_lo_s, res_s) = scratch[:N_SLOTS], scratch[N_SLOTS:]
    tt = x_ref.shape[0]
    spread = jnp.where(_iota((PEER_SLOTS, SLOT_WIDTH), 1) // FEAT_CHUNKS == _iota((PEER_SLOTS, SLOT_WIDTH), 0),
                       1.0, 0.0).astype(BF16)
    hi, lo = _split_bf16(coef_ref[...])
    ce_hi_s[...] = _dot(hi, spread)
    ce_lo_s[...] = _dot(lo, spread)
    diag = _chunk_diag()

    def compute(t, slot):
        rows = pltpu.bitcast(slot[...], BF16)
        a_hi = jnp.where(diag, ce_hi_s[pl.ds(t, 1), :], 0.0)
        a_lo = jnp.where(diag, ce_lo_s[pl.ds(t, 1), :], 0.0)
        both = _dot(jnp.concatenate([a_hi, a_lo], axis=0).astype(BF16), rows)
        r0 = pl.multiple_of(t * FEAT_CHUNKS, FEAT_CHUNKS)
        res_s[pl.ds(r0, FEAT_CHUNKS), :] = both[0:FEAT_CHUNKS, :] + both[FEAT_CHUNKS:2 * FEAT_CHUNKS, :]

    _pipelined_tokens(tt, ids_ref, tbl_ref, slots, compute)
    for c in range(FEAT_CHUNKS):
        cols = slice(c * LANES, (c + 1) * LANES)
        y = res_s[pl.ds(c, tt, stride=FEAT_CHUNKS), :]
        o_ref[:, cols] = x_ref[:, cols] + gate_ref[0][:, cols] * y


def _peer_v(ids_flat, coef, x2, gate, table, seq, tt):
    t, d = x2.shape
    per_b = seq // tt
    blk = pl.BlockSpec((tt, d), lambda i: (i, 0))
    return pl.pallas_call(
        _peer_v_kernel,
        grid=(t // tt,),
        in_specs=[
            pl.BlockSpec((tt * PEER_SLOTS,), lambda i: (i,), memory_space=pltpu.SMEM),
            pl.BlockSpec((tt, PEER_SLOTS), lambda i: (i, 0)),
            blk,
            pl.BlockSpec((1, 1, d), lambda i: (i // per_b, 0, 0)),
            _table_spec(table.shape[0]),
        ],
        out_specs=blk,
        out_shape=jax.ShapeDtypeStruct((t, d), F32),
        scratch_shapes=_slot_scratch() + [
                        pltpu.VMEM((tt, SLOT_WIDTH), F32), pltpu.VMEM((tt, SLOT_WIDTH), F32),
                        pltpu.VMEM((tt * FEAT_CHUNKS, LANES), F32)],
        compiler_params=_cparams(1, V7X_VMEM_LIMIT),
        name="peer_expert_out",
    )(ids_flat, coef, x2, gate, table)


def _peer_ffn(x2, g, shift, scale, gate, wq, sub_keys, table_u, table_v, seq):
    t, d = x2.shape
    tt = 128
    h, ids, gts = _peer_route(x2, g, shift, scale, wq.T.astype(BF16), sub_keys, seq)
    ids_flat = ids.transpose(0, 2, 1).reshape(t * PEER_SLOTS)
    gates = gts.transpose(0, 2, 1).reshape(t, PEER_SLOTS)
    coef = _peer_u(ids_flat, h, gates, table_u, tt)
    return _peer_v(ids_flat, coef, x2, gate, table_v, seq, tt)


def _rope_tables(seq):
    half = HEAD_DIM // 2
    inv_freq = ROPE_THETA ** (-jnp.arange(half, dtype=F32) / half)
    ang = jnp.arange(seq).astype(F32)[:, None] * inv_freq[None, :]
    reps = LANES // half
    return jnp.tile(jnp.cos(ang), (1, reps)), jnp.tile(jnp.sin(ang), (1, reps))


def _two_heads(gain):
    return jnp.tile(gain.reshape(1, HEAD_DIM), (1, LANES // HEAD_DIM))


def kernel(x, c, ada_w, ada_b, norm_mix_g, norm_ffn_g, w_in_ab, w_out_ab, sinks_a, qnorm_a, knorm_a,
           w_in_cd, w_out_cd, qnorm_c, knorm_c, qnorm_d, knorm_d, peer_wq, peer_subkeys, peer_u, peer_v):
    b, seq, d = x.shape
    depth = ada_w.shape[0]
    t = b * seq
    cos, sin = _rope_tables(seq)
    mod = _modulation(c, ada_w, ada_b)
    x2 = x.reshape(t, d)
    for layer in range(depth):
        shift_m, scale_m, gate_m, shift_f, scale_f, gate_f = [
            m.reshape(b, 1, d) for m in jnp.split(mod[layer], 6, axis=-1)]
        g_mix = norm_mix_g[layer].reshape(1, d)
        i = layer // 2
        if layer % 2 == 0:
            proj = _norm_proj(x2, g_mix, shift_m, scale_m, w_in_ab[i].astype(BF16), seq)
            proj = proj.reshape(b, seq, -1)
            ya = _swa_attention(proj, sinks_a[i], cos, sin, _two_heads(qnorm_a[i]), _two_heads(knorm_a[i]))
            b_col = (A_Q_HEADS + 2 * A_KV_HEADS) * HEAD_DIM // LANES
            yb = _stick_attention(proj, b_col)
            w_out = w_out_ab[i]
        else:
            proj = _norm_proj(x2, g_mix, shift_m, scale_m, w_in_cd[i].astype(BF16), seq)
            proj = proj.reshape(b, seq, -1)
            ya = _qkv_attention_call(_dilated_kernel, "dilated_attention", proj, 0, C_HEADS, cos, sin,
                                     _two_heads(qnorm_c[i]), _two_heads(knorm_c[i]))
            d_col = 3 * C_HEADS * HEAD_DIM // LANES
            yb = _qkv_attention_call(_moba_kernel, "moba_attention", proj, d_col, D_HEADS, cos, sin,
                                     _two_heads(qnorm_d[i]), _two_heads(knorm_d[i]),
                                     extra_scratch=(pltpu.VMEM((LANES, LANES), F32),
                                                    pltpu.VMEM((seq, LANES), F32),
                                                    pltpu.VMEM((seq, LANES), F32)))
            w_out = w_out_cd[i]
        x2 = _out_proj(x2, ya.reshape(t, -1), yb.reshape(t, -1), w_out.astype(BF16), gate_m, seq)
        x2 = _peer_ffn(x2, norm_ffn_g[layer].reshape(1, d), shift_f, scale_f, gate_f,
                       peer_wq[layer], peer_subkeys[layer],
                       _pack_table(peer_u[layer]), _pack_table(peer_v[layer]), seq)
    return x2.reshape(b, seq, d)
```

```python
import functools

import jax
import jax.numpy as jnp
from jax import lax
from jax.experimental import pallas as pl
from jax.experimental.pallas import tpu as pltpu

F32 = jnp.float32
BF16 = jnp.bfloat16
I32 = jnp.int32
U32 = jnp.uint32

HEAD_DIM = 64
ROPE_THETA = 10000.0
NORM_EPS = 1e-6
LANES = 128
QUERY_BLOCK = 128
A_Q_HEADS, A_KV_HEADS = 8, 2
B_HEADS = C_HEADS = D_HEADS = 8
C_PATTERNS = ((128, 1), (512, 4), (2048, 16))
MOBA_BLOCK, MOBA_TOPK = 256, 3
PEER_HEADS, PEER_N_KEYS, PEER_TOPK, PEER_D_KEY = 8, 128, 16, 256
PEER_SLOTS = PEER_HEADS * PEER_TOPK
NEG_BIG = -1e30
PEER_CAND_COUNTS = tuple(PEER_TOPK // (a + 1) for a in range(PEER_TOPK))
PEER_N_CAND = sum(PEER_CAND_COUNTS)
PEER_CAND_ROWS = 56
PACK_ROWS = 4
V7X_VMEM_LIMIT = 56 * 1024 * 1024


def _cparams(n_axes, vmem=None):
    return pltpu.CompilerParams(
        dimension_semantics=("arbitrary",) * n_axes,
        vmem_limit_bytes=vmem)


def _split_bf16(a):
    hi = a.astype(BF16)
    lo = (a - hi.astype(F32)).astype(BF16)
    return hi, lo


def _dot(a, b):
    return jnp.dot(a, b, preferred_element_type=F32)


def _dot_nt(a, b):
    return lax.dot_general(a, b, (((1,), (1,)), ((), ())), preferred_element_type=F32)


def _dot3(a, b):
    ah, al = _split_bf16(a)
    bh, bl = _split_bf16(b)
    return _dot(ah, bh) + _dot(ah, bl) + _dot(al, bh)


def _dot3_nt(a, b):
    ah, al = _split_bf16(a)
    bh, bl = _split_bf16(b)
    return _dot_nt(ah, bh) + _dot_nt(ah, bl) + _dot_nt(al, bh)


def _iota(shape, dim):
    return lax.broadcasted_iota(I32, shape, dim)


def _mod_kernel(c_ref, w_ref, b_ref, o_ref):
    c = c_ref[...]
    cond = c * jax.nn.sigmoid(c)
    o_ref[0] = _dot3(cond, w_ref[0]) + b_ref[0]


def _modulation(c, ada_w, ada_b):
    depth, d, n = ada_w.shape
    b = c.shape[0]
    tn = 1024
    return pl.pallas_call(
        _mod_kernel,
        grid=(depth, n // tn),
        in_specs=[
            pl.BlockSpec((b, d), lambda l, j: (0, 0)),
            pl.BlockSpec((1, d, tn), lambda l, j: (l, 0, j)),
            pl.BlockSpec((1, 1, tn), lambda l, j: (l, 0, j)),
        ],
        out_specs=pl.BlockSpec((1, b, tn), lambda l, j: (l, 0, j)),
        out_shape=jax.ShapeDtypeStruct((depth, b, n), F32),
        compiler_params=_cparams(2, V7X_VMEM_LIMIT),
        name="adaln_modulation",
    )(c, ada_w, ada_b.reshape(depth, 1, n))


def _adaln(x, g, shift, scale):
    ms = jnp.mean(x * x, axis=-1, keepdims=True)
    y = x * lax.rsqrt(ms + NORM_EPS) * g
    return y * (1.0 + scale) + shift


def _norm_proj_kernel(x_ref, g_ref, sh_ref, sc_ref, w_ref, o_ref):
    h = _adaln(x_ref[...], g_ref[...], sh_ref[0], sc_ref[0])
    o_ref[...] = _dot(h.astype(BF16), w_ref[...])


def _norm_proj(x2, g, shift, scale, w_bf, seq):
    t, d = x2.shape
    n = w_bf.shape[1]
    tt = 512
    per_b = seq // tt
    return pl.pallas_call(
        _norm_proj_kernel,
        grid=(t // tt,),
        in_specs=[
            pl.BlockSpec((tt, d), lambda i: (i, 0)),
            pl.BlockSpec((1, d), lambda i: (0, 0)),
            pl.BlockSpec((1, 1, d), lambda i: (i // per_b, 0, 0)),
            pl.BlockSpec((1, 1, d), lambda i: (i // per_b, 0, 0)),
            pl.BlockSpec((d, n), lambda i: (0, 0)),
        ],
        out_specs=pl.BlockSpec((tt, n), lambda i: (i, 0)),
        out_shape=jax.ShapeDtypeStruct((t, n), F32),
        compiler_params=_cparams(1, V7X_VMEM_LIMIT),
        name="adaln_in_proj",
    )(x2, g, shift, scale, w_bf)


def _out_proj_kernel(x_ref, ya_ref, yb_ref, w_ref, gate_ref, o_ref):
    half = ya_ref.shape[1]
    y = _dot(ya_ref[...].astype(BF16), w_ref[0:half, :])
    y = y + _dot(yb_ref[...].astype(BF16), w_ref[half:2 * half, :])
    o_ref[...] = x_ref[...] + gate_ref[0] * y


def _out_proj(x2, ya, yb, w_bf, gate, seq):
    t, d = x2.shape
    half = ya.shape[1]
    tt = 512
    per_b = seq // tt
    return pl.pallas_call(
        _out_proj_kernel,
        grid=(t // tt,),
        in_specs=[
            pl.BlockSpec((tt, d), lambda i: (i, 0)),
            pl.BlockSpec((tt, half), lambda i: (i, 0)),
            pl.BlockSpec((tt, half), lambda i: (i, 0)),
            pl.BlockSpec((2 * half, d), lambda i: (0, 0)),
            pl.BlockSpec((1, 1, d), lambda i: (i // per_b, 0, 0)),
        ],
        out_specs=pl.BlockSpec((tt, d), lambda i: (i, 0)),
        out_shape=jax.ShapeDtypeStruct((t, d), F32),
        compiler_params=_cparams(1, V7X_VMEM_LIMIT),
        name="mixer_out_proj",
    )(x2, ya, yb, w_bf, gate)


def _lane_row():
    return _iota((1, LANES), 1)


def _head_segment_ones():
    r = _iota((LANES, LANES), 0) // HEAD_DIM
    c = _iota((LANES, LANES), 1) // HEAD_DIM
    return jnp.where(r == c, 1.0, 0.0).astype(BF16)


def _headnorm_rope(a, g, cos, sin):
    hi, lo = _split_bf16(a * a)
    seg = _head_segment_ones()
    ms = (_dot(hi, seg) + _dot(lo, seg)) * (1.0 / HEAD_DIM)
    y = a * lax.rsqrt(ms + NORM_EPS) * g
    half = HEAD_DIM // 2
    upper = pltpu.roll(y, LANES - half, axis=1)
    lower = pltpu.roll(y, half, axis=1)
    first_half = (_lane_row() % HEAD_DIM) < half
    rot = jnp.where(first_half, -upper, lower)
    return y * cos + rot * sin


def _head_masks():
    lane = _lane_row()
    return lane < HEAD_DIM, lane >= HEAD_DIM


def _attn_specs(seq, n_pairs, qcol, kcol, vcol, kv_shared):
    blk = (1, seq, LANES)
    q_spec = pl.BlockSpec(blk, lambda b, p: (b, 0, qcol + p))
    if kv_shared:
        k_spec = pl.BlockSpec(blk, lambda b, p: (b, 0, kcol))
        v_spec = pl.BlockSpec(blk, lambda b, p: (b, 0, vcol))
    else:
        k_spec = pl.BlockSpec(blk, lambda b, p: (b, 0, kcol + p))
        v_spec = pl.BlockSpec(blk, lambda b, p: (b, 0, vcol + p))
    return q_spec, k_spec, v_spec


def _row_spec(seq):
    return pl.BlockSpec((seq, LANES), lambda b, p: (0, 0))


def _gain_spec():
    return pl.BlockSpec((1, LANES), lambda b, p: (0, 0))


def _store_heads(o_ref, r0, outs):
    first, _ = _head_masks()
    o_ref[0, pl.ds(r0, QUERY_BLOCK), :] = jnp.where(first, outs[0], outs[1])


def _swa_kernel(sinks_ref, q_ref, k_ref, v_ref, cos_ref, sin_ref, gq_ref, gk_ref, o_ref,
                q0_s, q1_s, k_s, v_s):
    p = pl.program_id(1)
    seq = q_ref.shape[1]
    cos, sin = cos_ref[...], sin_ref[...]
    first, second = _head_masks()
    qn = _headnorm_rope(q_ref[0], gq_ref[...], cos, sin) * (HEAD_DIM ** -0.5)
    q0_s[...] = jnp.where(first, qn, 0.0).astype(BF16)
    q1_s[...] = jnp.where(second, qn, 0.0).astype(BF16)
    pairs_per_kv = (A_Q_HEADS // A_KV_HEADS) // 2
    keep = jnp.logical_xor(first, (p // pairs_per_kv) == 1)
    kn = _headnorm_rope(k_ref[0], gk_ref[...], cos, sin)
    k_s[...] = jnp.where(keep, kn, pltpu.roll(kn, HEAD_DIM, axis=1)).astype(BF16)
    v = v_ref[0]
    v_s[...] = jnp.where(keep, v, pltpu.roll(v, HEAD_DIM, axis=1)).astype(BF16)

    qi = _iota((QUERY_BLOCK, QUERY_BLOCK), 0)
    ki = _iota((QUERY_BLOCK, QUERY_BLOCK), 1)

    def qblock(i, carry):
        r0 = pl.multiple_of(i * QUERY_BLOCK, QUERY_BLOCK)
        rp = pl.multiple_of(jnp.maximum(i - 1, 0) * QUERY_BLOCK, QUERY_BLOCK)
        kc, kp = k_s[pl.ds(r0, QUERY_BLOCK), :], k_s[pl.ds(rp, QUERY_BLOCK), :]
        vc, vp = v_s[pl.ds(r0, QUERY_BLOCK), :], v_s[pl.ds(rp, QUERY_BLOCK), :]
        mask_c = ki <= qi
        mask_p = jnp.logical_and(ki > qi, i > 0)
        qhs = [q_s[pl.ds(r0, QUERY_BLOCK), :] for q_s in (q0_s, q1_s)]
        scores = [(_dot_nt(qh, kc), _dot_nt(qh, kp)) for qh in qhs]
        probs, denoms = [], []
        for hh in range(2):
            sc = jnp.where(mask_c, scores[hh][0], NEG_BIG)
            sp = jnp.where(mask_p, scores[hh][1], NEG_BIG)
            sink = sinks_ref[2 * p + hh]
            m = jnp.maximum(jnp.max(sc, axis=1, keepdims=True), jnp.max(sp, axis=1, keepdims=True))
            m = jnp.maximum(m, sink)
            ec, ep = jnp.exp(sc - m), jnp.exp(sp - m)
            denoms.append(jnp.sum(ec, axis=1, keepdims=True) + jnp.sum(ep, axis=1, keepdims=True)
                          + jnp.exp(sink - m))
            probs.append((ec.astype(BF16), ep.astype(BF16)))
        outs = [(_dot(probs[hh][0], vc) + _dot(probs[hh][1], vp)) / denoms[hh] for hh in range(2)]
        _store_heads(o_ref, r0, outs)
        return carry

    lax.fori_loop(0, seq // QUERY_BLOCK, qblock, 0)


def _swa_attention(proj, sinks, cos, sin, gq, gk):
    b, seq, _ = proj.shape
    n_pairs = A_Q_HEADS // 2
    kcol = A_Q_HEADS * HEAD_DIM // LANES
    vcol = kcol + A_KV_HEADS * HEAD_DIM // LANES
    q_spec, k_spec, v_spec = _attn_specs(seq, n_pairs, 0, kcol, vcol, True)
    return pl.pallas_call(
        _swa_kernel,
        grid=(b, n_pairs),
        in_specs=[pl.BlockSpec(memory_space=pltpu.SMEM), q_spec, k_spec, v_spec,
                  _row_spec(seq), _row_spec(seq), _gain_spec(), _gain_spec()],
        out_specs=pl.BlockSpec((1, seq, LANES), lambda b_, p: (b_, 0, p)),
        out_shape=jax.ShapeDtypeStruct((b, seq, n_pairs * LANES), F32),
        scratch_shapes=[pltpu.VMEM((seq, LANES), BF16)] * 4,
        compiler_params=_cparams(2, V7X_VMEM_LIMIT),
        name="swa_gqa_attention",
    )(sinks, proj, proj, proj, cos, sin, gq, gk)


STICK_GROUP = 4


def _stick_kernel(q_ref, k_ref, v_ref, o_ref, k_s, v_s):
    seq = q_ref.shape[1]
    k_s[...] = k_ref[0].astype(BF16)
    v_s[...] = v_ref[0].astype(BF16)
    first, second = _head_masks()
    qi = _iota((QUERY_BLOCK, QUERY_BLOCK), 0)
    ki = _iota((QUERY_BLOCK, QUERY_BLOCK), 1)
    wr = _iota((2 * QUERY_BLOCK, 2 * QUERY_BLOCK), 0) % QUERY_BLOCK
    wc = _iota((2 * QUERY_BLOCK, 2 * QUERY_BLOCK), 1)
    suffix_w = jnp.where(jnp.logical_or(wc >= QUERY_BLOCK, wr > wc), 1.0, 0.0).astype(BF16)

    def qblock(i, carry):
        r0 = pl.multiple_of(i * QUERY_BLOCK, QUERY_BLOCK)
        q = q_ref[0, pl.ds(r0, QUERY_BLOCK), :] * (HEAD_DIM ** -0.5)
        qhs = [jnp.where(msk, q, 0.0).astype(BF16) for msk in (first, second)]

        def kgroup(g, st):
            accs, laters = [st[0], st[1]], [st[2], st[3]]
            chains = [(u, hh) for u in range(STICK_GROUP) for hh in range(2)]
            vbs, pasts, zs = [], [], {}
            for u in range(STICK_GROUP):
                j = i - (g * STICK_GROUP + u)
                live = j >= 0
                c0 = pl.multiple_of(jnp.maximum(j, 0) * QUERY_BLOCK, QUERY_BLOCK)
                kb = k_s[pl.ds(c0, QUERY_BLOCK), :]
                vbs.append(v_s[pl.ds(c0, QUERY_BLOCK), :])
                pasts.append(jnp.logical_and((c0 + ki) < (r0 + qi), live))
                for hh in range(2):
                    zs[u, hh] = _dot_nt(qhs[hh], kb)
            logit, sums = {}, {}
            for u, hh in chains:
                z = zs[u, hh]
                sp = jnp.maximum(z, 0.0) + jnp.log1p(jnp.exp(-jnp.abs(z)))
                log_keep = jnp.where(pasts[u], -sp, 0.0)
                logit[u, hh] = z - sp
                hi, lo = _split_bf16(log_keep)
                sums[u, hh] = _dot(jnp.concatenate([hi, lo], axis=1), suffix_w)
            ws = {}
            for u, hh in chains:
                inner, total = sums[u, hh][:, :QUERY_BLOCK], sums[u, hh][:, QUERY_BLOCK:]
                ws[u, hh] = jnp.where(pasts[u], jnp.exp(logit[u, hh] + inner + laters[hh]), 0.0).astype(BF16)
                laters[hh] = laters[hh] + total
            for u, hh in chains:
                accs[hh] = accs[hh] + _dot(ws[u, hh], vbs[u])
            return accs[0], accs[1], laters[0], laters[1]

        zero = jnp.zeros((QUERY_BLOCK, LANES), F32)
        st = lax.fori_loop(0, (i + STICK_GROUP) // STICK_GROUP, kgroup, (zero, zero, zero, zero))
        _store_heads(o_ref, r0, [st[0], st[1]])
        return carry

    lax.fori_loop(0, seq // QUERY_BLOCK, qblock, 0)


def _stick_attention(proj, qcol):
    b, seq, _ = proj.shape
    n_pairs = B_HEADS // 2
    q_spec, k_spec, v_spec = _attn_specs(seq, n_pairs, qcol, qcol + n_pairs, qcol + 2 * n_pairs, False)
    return pl.pallas_call(
        _stick_kernel,
        grid=(b, n_pairs),
        in_specs=[q_spec, k_spec, v_spec],
        out_specs=pl.BlockSpec((1, seq, LANES), lambda b_, p: (b_, 0, p)),
        out_shape=jax.ShapeDtypeStruct((b, seq, n_pairs * LANES), F32),
        scratch_shapes=[pltpu.VMEM((seq, LANES), BF16)] * 2,
        compiler_params=_cparams(2, V7X_VMEM_LIMIT),
        name="stick_breaking_attention",
    )(proj, proj, proj)


def _prep_qkv(q_ref, k_ref, v_ref, cos_ref, sin_ref, gq_ref, gk_ref, q0_s, q1_s, k_s, v_s):
    cos, sin = cos_ref[...], sin_ref[...]
    first, second = _head_masks()
    qn = _headnorm_rope(q_ref[0], gq_ref[...], cos, sin) * (HEAD_DIM ** -0.5)
    q0_s[...] = jnp.where(first, qn, 0.0).astype(BF16)
    q1_s[...] = jnp.where(second, qn, 0.0).astype(BF16)
    kn = _headnorm_rope(k_ref[0], gk_ref[...], cos, sin)
    k_s[...] = kn.astype(BF16)
    v_s[...] = v_ref[0].astype(BF16)
    return qn, kn


DILATED_KEY_TILE = 512


def _dilated_kernel(q_ref, k_ref, v_ref, cos_ref, sin_ref, gq_ref, gk_ref, o_ref,
                    q0_s, q1_s, k_s, v_s):
    seq = q_ref.shape[1]
    _prep_qkv(q_ref, k_ref, v_ref, cos_ref, sin_ref, gq_ref, gk_ref, q0_s, q1_s, k_s, v_s)
    kt = min(DILATED_KEY_TILE, seq)
    qk = _iota((QUERY_BLOCK, kt), 0) - _iota((QUERY_BLOCK, kt), 1)
    on_stride = [jnp.where((qk & (dil - 1)) == 0, 1.0, 0.0) for _, dil in C_PATTERNS]

    def qblock(i, carry):
        r0 = pl.multiple_of(i * QUERY_BLOCK, QUERY_BLOCK)
        qhs = [q_s[pl.ds(r0, QUERY_BLOCK), :] for q_s in (q0_s, q1_s)]

        def ktile(g, st):
            c0 = pl.multiple_of(g * kt, kt)
            d = (r0 - c0) + qk
            count = jnp.zeros(d.shape, F32)
            for (window, _), stride_ok in zip(C_PATTERNS, on_stride):
                count = count + jnp.where(d <= window, stride_ok, 0.0)
            count = jnp.where(d >= 0, count, 0.0)
            kb, vb = k_s[pl.ds(c0, kt), :], v_s[pl.ds(c0, kt), :]
            scores = [_dot_nt(qhs[hh], kb) for hh in range(2)]
            new, prs = [], []
            for hh in range(2):
                m, l, acc = st[3 * hh:3 * hh + 3]
                s = jnp.where(count > 0.0, scores[hh], NEG_BIG)
                m_new = jnp.maximum(m, jnp.max(s, axis=1, keepdims=True))
                pr = count * jnp.exp(s - m_new)
                alpha = jnp.exp(m - m_new)
                new += [m_new, alpha * l + jnp.sum(pr, axis=1, keepdims=True), alpha * acc]
                prs.append(pr.astype(BF16))
            for hh in range(2):
                new[3 * hh + 2] = new[3 * hh + 2] + _dot(prs[hh], vb)
            return tuple(new)

        init = (jnp.full((QUERY_BLOCK, 1), NEG_BIG, F32), jnp.zeros((QUERY_BLOCK, 1), F32),
                jnp.zeros((QUERY_BLOCK, LANES), F32)) * 2
        st = lax.fori_loop(0, (r0 + QUERY_BLOCK + kt - 1) // kt, ktile, init)
        _store_heads(o_ref, r0, [st[2] / st[1], st[5] / st[4]])
        return carry

    lax.fori_loop(0, seq // QUERY_BLOCK, qblock, 0)


def _qkv_attention_call(kernel, name, proj, qcol, n_heads, cos, sin, gq, gk, extra_scratch=()):
    b, seq, _ = proj.shape
    n_pairs = n_heads // 2
    q_spec, k_spec, v_spec = _attn_specs(seq, n_pairs, qcol, qcol + n_pairs, qcol + 2 * n_pairs, False)
    return pl.pallas_call(
        kernel,
        grid=(b, n_pairs),
        in_specs=[q_spec, k_spec, v_spec, _row_spec(seq), _row_spec(seq), _gain_spec(), _gain_spec()],
        out_specs=pl.BlockSpec((1, seq, LANES), lambda b_, p: (b_, 0, p)),
        out_shape=jax.ShapeDtypeStruct((b, seq, n_pairs * LANES), F32),
        scratch_shapes=[pltpu.VMEM((seq, LANES), BF16)] * 4 + list(extra_scratch),
        compiler_params=_cparams(2, V7X_VMEM_LIMIT),
        name=name,
    )(proj, proj, proj, cos, sin, gq, gk)


def _moba_kernel(q_ref, k_ref, v_ref, cos_ref, sin_ref, gq_ref, gk_ref, o_ref,
                 q0_s, q1_s, k_s, v_s, km_s, sel0_s, sel1_s):
    seq = q_ref.shape[1]
    n_blocks = seq // MOBA_BLOCK
    qn, kn = _prep_qkv(q_ref, k_ref, v_ref, cos_ref, sin_ref, gq_ref, gk_ref, q0_s, q1_s, k_s, v_s)
    km_s[...] = jnp.zeros(km_s.shape, F32)
    km_s[0:n_blocks, :] = jnp.mean(kn.reshape(n_blocks, MOBA_BLOCK, LANES), axis=1)
    first, second = _head_masks()

    rows8 = _iota((8, seq), 0)
    own8 = _iota((8, seq), 1) // MOBA_BLOCK
    valid = rows8 < own8
    for msk, sel_s in ((first, sel0_s), (second, sel1_s)):
        gate = _dot3_nt(km_s[...], jnp.where(msk, qn, 0.0))[0:8, :]
        gm = jnp.where(valid, gate, -jnp.inf)
        rank = jnp.zeros((8, seq), F32)
        for n2 in range(n_blocks):
            g2 = gm[n2:n2 + 1, :]
            beats = jnp.logical_or(g2 > gm, jnp.logical_and(g2 == gm, n2 < rows8))
            rank = rank + jnp.where(jnp.logical_and(beats, n2 < own8), 1.0, 0.0)
        sel = jnp.where(jnp.logical_and(valid, rank < float(MOBA_TOPK)), 1.0, 0.0)
        sel = jnp.concatenate([sel, jnp.zeros((LANES - 8, seq), F32)], axis=0)
        sel_s[...] = sel.T

    lane_sq = _iota((QUERY_BLOCK, LANES), 1)
    qi = _iota((QUERY_BLOCK, MOBA_BLOCK), 0)
    ki = _iota((QUERY_BLOCK, MOBA_BLOCK), 1)
    second_block = _iota((QUERY_BLOCK, 2 * MOBA_BLOCK), 1) >= MOBA_BLOCK

    def qblock(i, carry):
        r0 = pl.multiple_of(i * QUERY_BLOCK, QUERY_BLOCK)
        own = (i * QUERY_BLOCK) // MOBA_BLOCK
        own0 = pl.multiple_of(own * MOBA_BLOCK, MOBA_BLOCK)
        qhs = [q_s[pl.ds(r0, QUERY_BLOCK), :] for q_s in (q0_s, q1_s)]
        sels = [sel_s[pl.ds(r0, QUERY_BLOCK), :] for sel_s in (sel0_s, sel1_s)]
        kb, vb = k_s[pl.ds(own0, MOBA_BLOCK), :], v_s[pl.ds(own0, MOBA_BLOCK), :]
        causal = (own0 + ki) <= (r0 + qi)
        scores = [_dot_nt(qhs[hh], kb) for hh in range(2)]
        init, prs = [], []
        for hh in range(2):
            s = jnp.where(causal, scores[hh], NEG_BIG)
            m = jnp.max(s, axis=1, keepdims=True)
            pr = jnp.exp(s - m)
            init += [m, jnp.sum(pr, axis=1, keepdims=True), None]
            prs.append(pr.astype(BF16))
        for hh in range(2):
            init[3 * hh + 2] = _dot(prs[hh], vb)

        def kpair(g, st):
            c0 = pl.multiple_of(g * 2 * MOBA_BLOCK, 2 * MOBA_BLOCK)
            kb2, vb2 = k_s[pl.ds(c0, 2 * MOBA_BLOCK), :], v_s[pl.ds(c0, 2 * MOBA_BLOCK), :]
            scores = [_dot_nt(qhs[hh], kb2) for hh in range(2)]
            new, prs = [], []
            for hh in range(2):
                m, l, acc = st[3 * hh:3 * hh + 3]
                sel_a = jnp.sum(jnp.where(lane_sq == 2 * g, sels[hh], 0.0), axis=1, keepdims=True)
                sel_b = jnp.sum(jnp.where(lane_sq == 2 * g + 1, sels[hh], 0.0), axis=1, keepdims=True)
                keep = jnp.where(second_block, sel_b, sel_a) > 0.0
                s = jnp.where(keep, scores[hh], NEG_BIG)
                m_new = jnp.maximum(m, jnp.max(s, axis=1, keepdims=True))
                pr = jnp.exp(s - m_new)
                alpha = jnp.exp(m - m_new)
                new += [m_new, alpha * l + jnp.sum(pr, axis=1, keepdims=True), alpha * acc]
                prs.append(pr.astype(BF16))
            for hh in range(2):
                new[3 * hh + 2] = new[3 * hh + 2] + _dot(prs[hh], vb2)
            return tuple(new)

        st = lax.fori_loop(0, (own + 1) // 2, kpair, tuple(init))
        _store_heads(o_ref, r0, [st[2] / st[1], st[5] / st[4]])
        return carry

    lax.fori_loop(0, seq // QUERY_BLOCK, qblock, 0)


ROUTE_CHUNKS_PER_STEP = 2


def _top16_rows(scores, n_rows, vals_refs, idx_refs):
    scores = list(scores)
    rows = _iota(scores[0].shape, 0)
    for it in range(PEER_TOPK):
        for k, s in enumerate(scores):
            m = jnp.max(s, axis=0, keepdims=True)
            pick = jnp.min(jnp.where(s == m, rows, n_rows), axis=0, keepdims=True)
            scores[k] = jnp.where(rows == pick, -jnp.inf, s)
            vals_refs[k][it:it + 1, :] = m
            idx_refs[k][it:it + 1, :] = pick


def _peer_route_kernel(x_ref, g_ref, sh_ref, sc_ref, wqt_ref, sk_ref, h_ref, ids_ref, gts_ref,
                       q_s, val_s, idx_s, cand_s, cidx_s):
    tt = x_ref.shape[0]
    n_chunks = tt // LANES
    per_step = ROUTE_CHUNKS_PER_STEP
    half = PEER_D_KEY // 2
    h = _adaln(x_ref[...], g_ref[...], sh_ref[0], sc_ref[0])
    h_ref[...] = h
    qt = _dot_nt(wqt_ref[...], h.astype(BF16))
    for c in range(n_chunks):
        q_s[c] = qt[:, c * LANES:(c + 1) * LANES]
    for k in range(per_step):
        cand_s[k, PEER_CAND_ROWS - 8:PEER_CAND_ROWS, :] = jnp.full((8, LANES), -jnp.inf, F32)
        cidx_s[k, PEER_CAND_ROWS - 8:PEER_CAND_ROWS, :] = jnp.zeros((8, LANES), I32)
    sk1, sk2 = sk_ref[0], sk_ref[1]
    vals = [val_s.at[i] for i in range(2 * per_step)]
    idxs = [idx_s.at[i] for i in range(2 * per_step)]

    def body(step, carry):
        hh = step // (n_chunks // per_step)
        c0 = (step % (n_chunks // per_step)) * per_step
        q0 = pl.multiple_of(hh * PEER_D_KEY, PEER_D_KEY)
        scores = []
        for k in range(per_step):
            scores.append(_dot3(sk1, q_s[c0 + k, pl.ds(q0, half), :]))
            scores.append(_dot3(sk2, q_s[c0 + k, pl.ds(q0 + half, half), :]))
        _top16_rows(scores, PEER_N_KEYS, vals, idxs)
        cands, cidxs = [], []
        for k in range(per_step):
            v1, i1, v2, i2 = vals[2 * k], idxs[2 * k], vals[2 * k + 1], idxs[2 * k + 1]
            r = 0
            for a, nb in enumerate(PEER_CAND_COUNTS):
                cand_s[k, r:r + nb, :] = v1[a:a + 1, :] + v2[0:nb, :]
                cidx_s[k, r:r + nb, :] = i1[a:a + 1, :] * PEER_N_KEYS + i2[0:nb, :]
                r += nb
            cands.append(cand_s[k])
            cidxs.append(cidx_s[k])
        rows = _iota(cands[0].shape, 0)
        for it in range(PEER_TOPK):
            for k in range(per_step):
                m = jnp.max(cands[k], axis=0, keepdims=True)
                pick = jnp.min(jnp.where(cands[k] == m, rows, PEER_CAND_ROWS), axis=0, keepdims=True)
                hit = rows == pick
                idxs[2 * k][it:it + 1, :] = jnp.sum(jnp.where(hit, cidxs[k], 0), axis=0, keepdims=True)
                vals[2 * k][it:it + 1, :] = m
                cands[k] = jnp.where(hit, -jnp.inf, cands[k])
        r0 = pl.multiple_of(hh * PEER_TOPK, PEER_TOPK)
        for k in range(per_step):
            top = vals[2 * k][...]
            e = jnp.exp(top - top[0:1, :])
            gts_ref[c0 + k, pl.ds(r0, PEER_TOPK), :] = e / jnp.sum(e, axis=0, keepdims=True)
            ids_ref[c0 + k, pl.ds(r0, PEER_TOPK), :] = idxs[2 * k][...] * PACK_ROWS
        return carry

    lax.fori_loop(0, PEER_HEADS * n_chunks // per_step, body, 0)


def _peer_route(x2, g, shift, scale, wqt_bf, sub_keys, seq):
    t, d = x2.shape
    tt = 512
    per_b = seq // tt
    n_chunks = tt // LANES
    nq = wqt_bf.shape[0]
    out_blk = pl.BlockSpec((n_chunks, PEER_SLOTS, LANES), lambda i: (i, 0, 0))
    return pl.pallas_call(
        _peer_route_kernel,
        grid=(t // tt,),
        in_specs=[
            pl.BlockSpec((tt, d), lambda i: (i, 0)),
            pl.BlockSpec((1, d), lambda i: (0, 0)),
            pl.BlockSpec((1, 1, d), lambda i: (i // per_b, 0, 0)),
            pl.BlockSpec((1, 1, d), lambda i: (i // per_b, 0, 0)),
            pl.BlockSpec((nq, d), lambda i: (0, 0)),
            pl.BlockSpec(sub_keys.shape, lambda i: (0, 0, 0)),
        ],
        out_specs=[pl.BlockSpec((tt, d), lambda i: (i, 0)), out_blk, out_blk],
        out_shape=[jax.ShapeDtypeStruct((t, d), F32),
                   jax.ShapeDtypeStruct((t // LANES, PEER_SLOTS, LANES), I32),
                   jax.ShapeDtypeStruct((t // LANES, PEER_SLOTS, LANES), F32)],
        scratch_shapes=[pltpu.VMEM((n_chunks, nq, LANES), F32),
                        pltpu.VMEM((2 * ROUTE_CHUNKS_PER_STEP, PEER_TOPK, LANES), F32),
                        pltpu.VMEM((2 * ROUTE_CHUNKS_PER_STEP, PEER_TOPK, LANES), I32),
                        pltpu.VMEM((ROUTE_CHUNKS_PER_STEP, PEER_CAND_ROWS, LANES), F32),
                        pltpu.VMEM((ROUTE_CHUNKS_PER_STEP, PEER_CAND_ROWS, LANES), I32)],
        compiler_params=_cparams(1, V7X_VMEM_LIMIT),
        name="peer_route",
    )(x2, g, shift, scale, wqt_bf, sub_keys)


def _pack_table(tab):
    e, d = tab.shape
    bits = lax.bitcast_convert_type(tab.astype(BF16), jnp.uint16).astype(U32)
    bits = bits.reshape(e, d // (2 * LANES), 2, LANES)
    words = bits[:, :, 0, :] | (bits[:, :, 1, :] << 16)
    return words.reshape(e * (d // (2 * LANES)), LANES)


def _table_spec(rows):
    return pl.BlockSpec((rows, LANES), lambda i: (0, 0), pipeline_mode=pl.Buffered(1))


def _gelu_exact(a):
    return 0.5 * a * (1.0 + lax.erf(a * (2.0 ** -0.5)))


FEAT_CHUNKS = 8
SLOT_WIDTH = PEER_SLOTS * FEAT_CHUNKS


def _gather_rows(ids_ref, base, tbl_ref, slot):
    for j in range(PEER_SLOTS):
        if j % ID_VIEW == 0:
            ids_part = ids_ref.at[pl.ds(base + j, ID_VIEW)]
        row0 = pl.multiple_of(ids_part[j % ID_VIEW], PACK_ROWS)
        slot[PACK_ROWS * j:PACK_ROWS * (j + 1), :] = tbl_ref[pl.ds(row0, PACK_ROWS), :]


def _pipelined_tokens(tt, ids_ref, tbl_ref, slots, compute):
    group = len(slots) // 2
    first, second = slots[:group], slots[group:]
    last = tt - 1

    def gather(t, slot):
        _gather_rows(ids_ref, jnp.minimum(t, last) * PEER_SLOTS, tbl_ref, slot)

    for k in range(group):
        gather(k, first[k])

    def step(q, carry):
        t = 2 * group * q
        for k in range(group):
            compute(t + k, first[k])
        for k in range(group):
            gather(t + group + k, second[k])
        for k in range(group):
            compute(t + group + k, second[k])
        for k in range(group):
            gather(t + 2 * group + k, first[k])
        return carry

    lax.fori_loop(0, tt // (2 * group), step, 0)


def _chunk_diag():
    return (_iota((FEAT_CHUNKS, SLOT_WIDTH), 1) % FEAT_CHUNKS) == _iota((FEAT_CHUNKS, SLOT_WIDTH), 0)


def _peer_u_kernel(ids_ref, h_ref, g_ref, tbl_ref, coef_ref, *scratch):
    slots, (hx_s, rs_s) = scratch[:N_SLOTS], scratch[N_SLOTS:]
    tt = h_ref.shape[0]
    for c in range(FEAT_CHUNKS):
        hx_s[pl.ds(c, tt, stride=FEAT_CHUNKS), :] = h_ref[:, c * LANES:(c + 1) * LANES]
    diag = _chunk_diag()

    def compute(t, slot):
        rows = pltpu.bitcast(slot[...], BF16)
        x8 = hx_s[pl.ds(pl.multiple_of(t * FEAT_CHUNKS, FEAT_CHUNKS), FEAT_CHUNKS), :]
        part = _dot_nt(x8.astype(BF16), rows)
        rs_s[pl.ds(t, 1), :] = jnp.sum(jnp.where(diag, part, 0.0), axis=0, keepdims=True)

    _pipelined_tokens(tt, ids_ref, tbl_ref, slots, compute)
    group = jnp.where(_iota((SLOT_WIDTH, PEER_SLOTS), 0) // FEAT_CHUNKS == _iota((SLOT_WIDTH, PEER_SLOTS), 1),
                      1.0, 0.0).astype(BF16)
    hi, lo = _split_bf16(rs_s[...])
    act = _dot(hi, group) + _dot(lo, group)
    coef_ref[...] = g_ref[...] * _gelu_exact(act)


N_SLOTS = 16
ID_VIEW = 16
PEER_TOKEN_TILE = 256


def _slot_scratch():
    return [pltpu.VMEM((PEER_SLOTS * PACK_ROWS, LANES), U32)] * N_SLOTS


def _peer_u(ids_flat, h, gates, table, tt):
    t, d = h.shape
    return pl.pallas_call(
        _peer_u_kernel,
        grid=(t // tt,),
        in_specs=[
            pl.BlockSpec((tt * PEER_SLOTS,), lambda i: (i,), memory_space=pltpu.SMEM),
            pl.BlockSpec((tt, d), lambda i: (i, 0)),
            pl.BlockSpec((tt, PEER_SLOTS), lambda i: (i, 0)),
            _table_spec(table.shape[0]),
        ],
        out_specs=pl.BlockSpec((tt, PEER_SLOTS), lambda i: (i, 0)),
        out_shape=jax.ShapeDtypeStruct((t, PEER_SLOTS), F32),
        scratch_shapes=_slot_scratch() + [
                        pltpu.VMEM((tt * FEAT_CHUNKS, LANES), F32),
                        pltpu.VMEM((tt, SLOT_WIDTH), F32)],
        compiler_params=_cparams(1, V7X_VMEM_LIMIT),
        name="peer_expert_in",
    )(ids_flat, h, gates, table)


def _peer_v_kernel(ids_ref, coef_ref, x_ref, gate_ref, tbl_ref, o_ref, *scratch):
    slots, (ce_hi_s, ce_lo_s, res_s) = scratch[:N_SLOTS], scratch[N_SLOTS:]
    tt = x_ref.shape[0]
    spread = jnp.where(_iota((PEER_SLOTS, SLOT_WIDTH), 1) // FEAT_CHUNKS == _iota((PEER_SLOTS, SLOT_WIDTH), 0),
                       1.0, 0.0).astype(BF16)
    hi, lo = _split_bf16(coef_ref[...])
    ce_hi_s[...] = _dot(hi, spread)
    ce_lo_s[...] = _dot(lo, spread)
    diag = _chunk_diag()

    def compute(t, slot):
        rows = pltpu.bitcast(slot[...], BF16)
        a_hi = jnp.where(diag, ce_hi_s[pl.ds(t, 1), :], 0.0)
        a_lo = jnp.where(diag, ce_lo_s[pl.ds(t, 1), :], 0.0)
        both = _dot(jnp.concatenate([a_hi, a_lo], axis=0).astype(BF16), rows)
        r0 = pl.multiple_of(t * FEAT_CHUNKS, FEAT_CHUNKS)
        res_s[pl.ds(r0, FEAT_CHUNKS), :] = both[0:FEAT_CHUNKS, :] + both[FEAT_CHUNKS:2 * FEAT_CHUNKS, :]

    _pipelined_tokens(tt, ids_ref, tbl_ref, slots, compute)
    for c in range(FEAT_CHUNKS):
        cols = slice(c * LANES, (c + 1) * LANES)
        y = res_s[pl.ds(c, tt, stride=FEAT_CHUNKS), :]
        o_ref[:, cols] = x_ref[:, cols] + gate_ref[0][:, cols] * y


def _peer_v(ids_flat, coef, x2, gate, table, seq, tt):
    t, d = x2.shape
    per_b = seq // tt
    blk = pl.BlockSpec((tt, d), lambda i: (i, 0))
    return pl.pallas_call(
        _peer_v_kernel,
        grid=(t // tt,),
        in_specs=[
            pl.BlockSpec((tt * PEER_SLOTS,), lambda i: (i,), memory_space=pltpu.SMEM),
            pl.BlockSpec((tt, PEER_SLOTS), lambda i: (i, 0)),
            blk,
            pl.BlockSpec((1, 1, d), lambda i: (i // per_b, 0, 0)),
            _table_spec(table.shape[0]),
        ],
        out_specs=blk,
        out_shape=jax.ShapeDtypeStruct((t, d), F32),
        scratch_shapes=_slot_scratch() + [
                        pltpu.VMEM((tt, SLOT_WIDTH), F32), pltpu.VMEM((tt, SLOT_WIDTH), F32),
                        pltpu.VMEM((tt * FEAT_CHUNKS, LANES), F32)],
        compiler_params=_cparams(1, V7X_VMEM_LIMIT),
        name="peer_expert_out",
    )(ids_flat, coef, x2, gate, table)


def _peer_ffn(x2, g, shift, scale, gate, wq, sub_keys, table_u, table_v, seq):
    t, d = x2.shape
    tt = PEER_TOKEN_TILE
    h, ids, gts = _peer_route(x2, g, shift, scale, wq.T.astype(BF16), sub_keys, seq)
    ids_flat = ids.transpose(0, 2, 1).reshape(t * PEER_SLOTS)
    gates = gts.transpose(0, 2, 1).reshape(t, PEER_SLOTS)
    coef = _peer_u(ids_flat, h, gates, table_u, tt)
    return _peer_v(ids_flat, coef, x2, gate, table_v, seq, tt)


def _rope_tables(seq):
    half = HEAD_DIM // 2
    inv_freq = ROPE_THETA ** (-jnp.arange(half, dtype=F32) / half)
    ang = jnp.arange(seq).astype(F32)[:, None] * inv_freq[None, :]
    reps = LANES // half
    return jnp.tile(jnp.cos(ang), (1, reps)), jnp.tile(jnp.sin(ang), (1, reps))


def _two_heads(gain):
    return jnp.tile(gain.reshape(1, HEAD_DIM), (1, LANES // HEAD_DIM))


def kernel(x, c, ada_w, ada_b, norm_mix_g, norm_ffn_g, w_in_ab, w_out_ab, sinks_a, qnorm_a, knorm_a,
           w_in_cd, w_out_cd, qnorm_c, knorm_c, qnorm_d, knorm_d, peer_wq, peer_subkeys, peer_u, peer_v):
    b, seq, d = x.shape
    depth = ada_w.shape[0]
    t = b * seq
    cos, sin = _rope_tables(seq)
    mod = _modulation(c, ada_w, ada_b)
    x2 = x.reshape(t, d)
    for layer in range(depth):
        shift_m, scale_m, gate_m, shift_f, scale_f, gate_f = [
            m.reshape(b, 1, d) for m in jnp.split(mod[layer], 6, axis=-1)]
        g_mix = norm_mix_g[layer].reshape(1, d)
        i = layer // 2
        if layer % 2 == 0:
            proj = _norm_proj(x2, g_mix, shift_m, scale_m, w_in_ab[i].astype(BF16), seq)
            proj = proj.reshape(b, seq, -1)
            ya = _swa_attention(proj, sinks_a[i], cos, sin, _two_heads(qnorm_a[i]), _two_heads(knorm_a[i]))
            b_col = (A_Q_HEADS + 2 * A_KV_HEADS) * HEAD_DIM // LANES
            yb = _stick_attention(proj, b_col)
            w_out = w_out_ab[i]
        else:
            proj = _norm_proj(x2, g_mix, shift_m, scale_m, w_in_cd[i].astype(BF16), seq)
            proj = proj.reshape(b, seq, -1)
            ya = _qkv_attention_call(_dilated_kernel, "dilated_attention", proj, 0, C_HEADS, cos, sin,
                                     _two_heads(qnorm_c[i]), _two_heads(knorm_c[i]))
            d_col = 3 * C_HEADS * HEAD_DIM // LANES
            yb = _qkv_attention_call(_moba_kernel, "moba_attention", proj, d_col, D_HEADS, cos, sin,
                                     _two_heads(qnorm_d[i]), _two_heads(knorm_d[i]),
                                     extra_scratch=(pltpu.VMEM((LANES, LANES), F32),
                                                    pltpu.VMEM((seq, LANES), F32),
                                                    pltpu.VMEM((seq, LANES), F32)))
            w_out = w_out_cd[i]
        x2 = _out_proj(x2, ya.reshape(t, -1), yb.reshape(t, -1), w_out.astype(BF16), gate_m, seq)
        x2 = _peer_ffn(x2, norm_ffn_g[layer].reshape(1, d), shift_f, scale_f, gate_f,
                       peer_wq[layer], peer_subkeys[layer],
                       _pack_table(peer_u[layer]), _pack_table(peer_v[layer]), seq)
    return x2.reshape(b, seq, d)
```

```python
import functools

import jax
import jax.numpy as jnp
from jax import lax
from jax.experimental import pallas as pl
from jax.experimental.pallas import tpu as pltpu

F32 = jnp.float32
BF16 = jnp.bfloat16
I32 = jnp.int32
U32 = jnp.uint32

HEAD_DIM = 64
ROPE_THETA = 10000.0
NORM_EPS = 1e-6
LANES = 128
QUERY_BLOCK = 128
A_Q_HEADS, A_KV_HEADS = 8, 2
B_HEADS = C_HEADS = D_HEADS = 8
C_PATTERNS = ((128, 1), (512, 4), (2048, 16))
MOBA_BLOCK, MOBA_TOPK = 256, 3
PEER_HEADS, PEER_N_KEYS, PEER_TOPK, PEER_D_KEY = 8, 128, 16, 256
PEER_SLOTS = PEER_HEADS * PEER_TOPK
NEG_BIG = -1e30
PEER_CAND_COUNTS = tuple(PEER_TOPK // (a + 1) for a in range(PEER_TOPK))
PEER_N_CAND = sum(PEER_CAND_COUNTS)
PEER_CAND_ROWS = 56
PACK_ROWS = 4
V7X_VMEM_LIMIT = 56 * 1024 * 1024


def _cparams(n_axes, vmem=None):
    return pltpu.CompilerParams(
        dimension_semantics=("arbitrary",) * n_axes,
        vmem_limit_bytes=vmem)


def _split_bf16(a):
    hi = a.astype(BF16)
    lo = (a - hi.astype(F32)).astype(BF16)
    return hi, lo


def _dot(a, b):
    return jnp.dot(a, b, preferred_element_type=F32)


def _dot_nt(a, b):
    return lax.dot_general(a, b, (((1,), (1,)), ((), ())), preferred_element_type=F32)


def _dot3(a, b):
    ah, al = _split_bf16(a)
    bh, bl = _split_bf16(b)
    return _dot(ah, bh) + _dot(ah, bl) + _dot(al, bh)


def _dot3_nt(a, b):
    ah, al = _split_bf16(a)
    bh, bl = _split_bf16(b)
    return _dot_nt(ah, bh) + _dot_nt(ah, bl) + _dot_nt(al, bh)


def _iota(shape, dim):
    return lax.broadcasted_iota(I32, shape, dim)


def _mod_kernel(c_ref, w_ref, b_ref, o_ref):
    c = c_ref[...]
    cond = c * jax.nn.sigmoid(c)
    o_ref[0] = _dot3(cond, w_ref[0]) + b_ref[0]


def _modulation(c, ada_w, ada_b):
    depth, d, n = ada_w.shape
    b = c.shape[0]
    tn = 1024
    return pl.pallas_call(
        _mod_kernel,
        grid=(depth, n // tn),
        in_specs=[
            pl.BlockSpec((b, d), lambda l, j: (0, 0)),
            pl.BlockSpec((1, d, tn), lambda l, j: (l, 0, j)),
            pl.BlockSpec((1, 1, tn), lambda l, j: (l, 0, j)),
        ],
        out_specs=pl.BlockSpec((1, b, tn), lambda l, j: (l, 0, j)),
        out_shape=jax.ShapeDtypeStruct((depth, b, n), F32),
        compiler_params=_cparams(2, V7X_VMEM_LIMIT),
        name="adaln_modulation",
    )(c, ada_w, ada_b.reshape(depth, 1, n))


def _adaln(x, g, shift, scale):
    ms = jnp.mean(x * x, axis=-1, keepdims=True)
    y = x * lax.rsqrt(ms + NORM_EPS) * g
    return y * (1.0 + scale) + shift


def _norm_proj_kernel(x_ref, g_ref, sh_ref, sc_ref, w_ref, o_ref):
    h = _adaln(x_ref[...], g_ref[...], sh_ref[0], sc_ref[0])
    o_ref[...] = _dot(h.astype(BF16), w_ref[...])


def _norm_proj(x2, g, shift, scale, w_bf, seq):
    t, d = x2.shape
    n = w_bf.shape[1]
    tt = 512
    per_b = seq // tt
    return pl.pallas_call(
        _norm_proj_kernel,
        grid=(t // tt,),
        in_specs=[
            pl.BlockSpec((tt, d), lambda i: (i, 0)),
            pl.BlockSpec((1, d), lambda i: (0, 0)),
            pl.BlockSpec((1, 1, d), lambda i: (i // per_b, 0, 0)),
            pl.BlockSpec((1, 1, d), lambda i: (i // per_b, 0, 0)),
            pl.BlockSpec((d, n), lambda i: (0, 0)),
        ],
        out_specs=pl.BlockSpec((tt, n), lambda i: (i, 0)),
        out_shape=jax.ShapeDtypeStruct((t, n), F32),
        compiler_params=_cparams(1, V7X_VMEM_LIMIT),
        name="adaln_in_proj",
    )(x2, g, shift, scale, w_bf)


def _out_proj_kernel(x_ref, ya_ref, yb_ref, w_ref, gate_ref, o_ref):
    half = ya_ref.shape[1]
    y = _dot(ya_ref[...].astype(BF16), w_ref[0:half, :])
    y = y + _dot(yb_ref[...].astype(BF16), w_ref[half:2 * half, :])
    o_ref[...] = x_ref[...] + gate_ref[0] * y


def _out_proj(x2, ya, yb, w_bf, gate, seq):
    t, d = x2.shape
    half = ya.shape[1]
    tt = 512
    per_b = seq // tt
    return pl.pallas_call(
        _out_proj_kernel,
        grid=(t // tt,),
        in_specs=[
            pl.BlockSpec((tt, d), lambda i: (i, 0)),
            pl.BlockSpec((tt, half), lambda i: (i, 0)),
            pl.BlockSpec((tt, half), lambda i: (i, 0)),
            pl.BlockSpec((2 * half, d), lambda i: (0, 0)),
            pl.BlockSpec((1, 1, d), lambda i: (i // per_b, 0, 0)),
        ],
        out_specs=pl.BlockSpec((tt, d), lambda i: (i, 0)),
        out_shape=jax.ShapeDtypeStruct((t, d), F32),
        compiler_params=_cparams(1, V7X_VMEM_LIMIT),
        name="mixer_out_proj",
    )(x2, ya, yb, w_bf, gate)


def _lane_row():
    return _iota((1, LANES), 1)


def _head_segment_ones():
    r = _iota((LANES, LANES), 0) // HEAD_DIM
    c = _iota((LANES, LANES), 1) // HEAD_DIM
    return jnp.where(r == c, 1.0, 0.0).astype(BF16)


def _headnorm_rope(a, g, cos, sin):
    hi, lo = _split_bf16(a * a)
    seg = _head_segment_ones()
    ms = (_dot(hi, seg) + _dot(lo, seg)) * (1.0 / HEAD_DIM)
    y = a * lax.rsqrt(ms + NORM_EPS) * g
    half = HEAD_DIM // 2
    upper = pltpu.roll(y, LANES - half, axis=1)
    lower = pltpu.roll(y, half, axis=1)
    first_half = (_lane_row() % HEAD_DIM) < half
    rot = jnp.where(first_half, -upper, lower)
    return y * cos + rot * sin


def _head_masks():
    lane = _lane_row()
    return lane < HEAD_DIM, lane >= HEAD_DIM


def _attn_specs(seq, n_pairs, qcol, kcol, vcol, kv_shared):
    blk = (1, seq, LANES)
    q_spec = pl.BlockSpec(blk, lambda b, p: (b, 0, qcol + p))
    if kv_shared:
        k_spec = pl.BlockSpec(blk, lambda b, p: (b, 0, kcol))
        v_spec = pl.BlockSpec(blk, lambda b, p: (b, 0, vcol))
    else:
        k_spec = pl.BlockSpec(blk, lambda b, p: (b, 0, kcol + p))
        v_spec = pl.BlockSpec(blk, lambda b, p: (b, 0, vcol + p))
    return q_spec, k_spec, v_spec


def _row_spec(seq):
    return pl.BlockSpec((seq, LANES), lambda b, p: (0, 0))


def _gain_spec():
    return pl.BlockSpec((1, LANES), lambda b, p: (0, 0))


def _store_heads(o_ref, r0, outs):
    first, _ = _head_masks()
    o_ref[0, pl.ds(r0, QUERY_BLOCK), :] = jnp.where(first, outs[0], outs[1])


def _swa_kernel(sinks_ref, q_ref, k_ref, v_ref, cos_ref, sin_ref, gq_ref, gk_ref, o_ref,
                q0_s, q1_s, k_s, v_s):
    p = pl.program_id(1)
    seq = q_ref.shape[1]
    cos, sin = cos_ref[...], sin_ref[...]
    first, second = _head_masks()
    qn = _headnorm_rope(q_ref[0], gq_ref[...], cos, sin) * (HEAD_DIM ** -0.5)
    q0_s[...] = jnp.where(first, qn, 0.0).astype(BF16)
    q1_s[...] = jnp.where(second, qn, 0.0).astype(BF16)
    pairs_per_kv = (A_Q_HEADS // A_KV_HEADS) // 2
    keep = jnp.logical_xor(first, (p // pairs_per_kv) == 1)
    kn = _headnorm_rope(k_ref[0], gk_ref[...], cos, sin)
    k_s[...] = jnp.where(keep, kn, pltpu.roll(kn, HEAD_DIM, axis=1)).astype(BF16)
    v = v_ref[0]
    v_s[...] = jnp.where(keep, v, pltpu.roll(v, HEAD_DIM, axis=1)).astype(BF16)

    qi = _iota((QUERY_BLOCK, QUERY_BLOCK), 0)
    ki = _iota((QUERY_BLOCK, QUERY_BLOCK), 1)

    def qblock(i, carry):
        r0 = pl.multiple_of(i * QUERY_BLOCK, QUERY_BLOCK)
        rp = pl.multiple_of(jnp.maximum(i - 1, 0) * QUERY_BLOCK, QUERY_BLOCK)
        kc, kp = k_s[pl.ds(r0, QUERY_BLOCK), :], k_s[pl.ds(rp, QUERY_BLOCK), :]
        vc, vp = v_s[pl.ds(r0, QUERY_BLOCK), :], v_s[pl.ds(rp, QUERY_BLOCK), :]
        mask_c = ki <= qi
        mask_p = jnp.logical_and(ki > qi, i > 0)
        qhs = [q_s[pl.ds(r0, QUERY_BLOCK), :] for q_s in (q0_s, q1_s)]
        scores = [(_dot_nt(qh, kc), _dot_nt(qh, kp)) for qh in qhs]
        probs, denoms = [], []
        for hh in range(2):
            sc = jnp.where(mask_c, scores[hh][0], NEG_BIG)
            sp = jnp.where(mask_p, scores[hh][1], NEG_BIG)
            sink = sinks_ref[2 * p + hh]
            m = jnp.maximum(jnp.max(sc, axis=1, keepdims=True), jnp.max(sp, axis=1, keepdims=True))
            m = jnp.maximum(m, sink)
            ec, ep = jnp.exp(sc - m), jnp.exp(sp - m)
            denoms.append(jnp.sum(ec, axis=1, keepdims=True) + jnp.sum(ep, axis=1, keepdims=True)
                          + jnp.exp(sink - m))
            probs.append((ec.astype(BF16), ep.astype(BF16)))
        outs = [(_dot(probs[hh][0], vc) + _dot(probs[hh][1], vp)) / denoms[hh] for hh in range(2)]
        _store_heads(o_ref, r0, outs)
        return carry

    lax.fori_loop(0, seq // QUERY_BLOCK, qblock, 0)


def _swa_attention(proj, sinks, cos, sin, gq, gk):
    b, seq, _ = proj.shape
    n_pairs = A_Q_HEADS // 2
    kcol = A_Q_HEADS * HEAD_DIM // LANES
    vcol = kcol + A_KV_HEADS * HEAD_DIM // LANES
    q_spec, k_spec, v_spec = _attn_specs(seq, n_pairs, 0, kcol, vcol, True)
    return pl.pallas_call(
        _swa_kernel,
        grid=(b, n_pairs),
        in_specs=[pl.BlockSpec(memory_space=pltpu.SMEM), q_spec, k_spec, v_spec,
                  _row_spec(seq), _row_spec(seq), _gain_spec(), _gain_spec()],
        out_specs=pl.BlockSpec((1, seq, LANES), lambda b_, p: (b_, 0, p)),
        out_shape=jax.ShapeDtypeStruct((b, seq, n_pairs * LANES), F32),
        scratch_shapes=[pltpu.VMEM((seq, LANES), BF16)] * 4,
        compiler_params=_cparams(2, V7X_VMEM_LIMIT),
        name="swa_gqa_attention",
    )(sinks, proj, proj, proj, cos, sin, gq, gk)


STICK_GROUP = 4


def _stick_kernel(q_ref, k_ref, v_ref, o_ref, k_s, v_s):
    seq = q_ref.shape[1]
    k_s[...] = k_ref[0].astype(BF16)
    v_s[...] = v_ref[0].astype(BF16)
    first, second = _head_masks()
    lower_tri = _iota((QUERY_BLOCK, QUERY_BLOCK), 1) < _iota((QUERY_BLOCK, QUERY_BLOCK), 0)
    wr = _iota((2 * QUERY_BLOCK, 2 * QUERY_BLOCK), 0) % QUERY_BLOCK
    wc = _iota((2 * QUERY_BLOCK, 2 * QUERY_BLOCK), 1)
    suffix_w = jnp.where(jnp.logical_or(wc >= QUERY_BLOCK, wr > wc), 1.0, 0.0).astype(BF16)

    def qblock(i, carry):
        r0 = pl.multiple_of(i * QUERY_BLOCK, QUERY_BLOCK)
        q = q_ref[0, pl.ds(r0, QUERY_BLOCK), :] * (HEAD_DIM ** -0.5)
        qhs = [jnp.where(msk, q, 0.0).astype(BF16) for msk in (first, second)]

        def kgroup(g, st):
            accs, laters = [st[0], st[1]], [st[2], st[3]]
            chains = [(u, hh) for u in range(STICK_GROUP) for hh in range(2)]
            vbs, pasts, zs = [], [], {}
            for u in range(STICK_GROUP):
                j = i - (g * STICK_GROUP + u)
                live = j >= 0
                c0 = pl.multiple_of(jnp.maximum(j, 0) * QUERY_BLOCK, QUERY_BLOCK)
                kb = k_s[pl.ds(c0, QUERY_BLOCK), :]
                vbs.append(v_s[pl.ds(c0, QUERY_BLOCK), :])
                pasts.append(jnp.logical_and(jnp.logical_or(lower_tri, j < i), live))
                for hh in range(2):
                    zs[u, hh] = _dot_nt(qhs[hh], kb)
            logit, sums = {}, {}
            for u, hh in chains:
                z = zs[u, hh]
                sp = jnp.maximum(z, 0.0) + jnp.log(1.0 + jnp.exp(-jnp.abs(z)))
                log_keep = jnp.where(pasts[u], -sp, 0.0)
                logit[u, hh] = z - sp
                hi, lo = _split_bf16(log_keep)
                sums[u, hh] = _dot(jnp.concatenate([hi, lo], axis=1), suffix_w)
            ws = {}
            for u, hh in chains:
                inner, total = sums[u, hh][:, :QUERY_BLOCK], sums[u, hh][:, QUERY_BLOCK:]
                ws[u, hh] = jnp.where(pasts[u], jnp.exp(logit[u, hh] + inner + laters[hh]), 0.0).astype(BF16)
                laters[hh] = laters[hh] + total
            for u, hh in chains:
                accs[hh] = accs[hh] + _dot(ws[u, hh], vbs[u])
            return accs[0], accs[1], laters[0], laters[1]

        zero = jnp.zeros((QUERY_BLOCK, LANES), F32)
        st = lax.fori_loop(0, (i + STICK_GROUP) // STICK_GROUP, kgroup, (zero, zero, zero, zero))
        _store_heads(o_ref, r0, [st[0], st[1]])
        return carry

    lax.fori_loop(0, seq // QUERY_BLOCK, qblock, 0)


def _stick_attention(proj, qcol):
    b, seq, _ = proj.shape
    n_pairs = B_HEADS // 2
    q_spec, k_spec, v_spec = _attn_specs(seq, n_pairs, qcol, qcol + n_pairs, qcol + 2 * n_pairs, False)
    return pl.pallas_call(
        _stick_kernel,
        grid=(b, n_pairs),
        in_specs=[q_spec, k_spec, v_spec],
        out_specs=pl.BlockSpec((1, seq, LANES), lambda b_, p: (b_, 0, p)),
        out_shape=jax.ShapeDtypeStruct((b, seq, n_pairs * LANES), F32),
        scratch_shapes=[pltpu.VMEM((seq, LANES), BF16)] * 2,
        compiler_params=_cparams(2, V7X_VMEM_LIMIT),
        name="stick_breaking_attention",
    )(proj, proj, proj)


def _prep_qkv(q_ref, k_ref, v_ref, cos_ref, sin_ref, gq_ref, gk_ref, q0_s, q1_s, k_s, v_s):
    cos, sin = cos_ref[...], sin_ref[...]
    first, second = _head_masks()
    qn = _headnorm_rope(q_ref[0], gq_ref[...], cos, sin) * (HEAD_DIM ** -0.5)
    q0_s[...] = jnp.where(first, qn, 0.0).astype(BF16)
    q1_s[...] = jnp.where(second, qn, 0.0).astype(BF16)
    kn = _headnorm_rope(k_ref[0], gk_ref[...], cos, sin)
    k_s[...] = kn.astype(BF16)
    v_s[...] = v_ref[0].astype(BF16)
    return qn, kn


DILATED_KEY_TILE = 512


def _dilated_kernel(q_ref, k_ref, v_ref, cos_ref, sin_ref, gq_ref, gk_ref, o_ref,
                    q0_s, q1_s, k_s, v_s):
    seq = q_ref.shape[1]
    _prep_qkv(q_ref, k_ref, v_ref, cos_ref, sin_ref, gq_ref, gk_ref, q0_s, q1_s, k_s, v_s)
    kt = min(DILATED_KEY_TILE, seq)
    qk = _iota((QUERY_BLOCK, kt), 0) - _iota((QUERY_BLOCK, kt), 1)
    on_stride = [jnp.where((qk & (dil - 1)) == 0, 1.0, 0.0) for _, dil in C_PATTERNS]

    def qblock(i, carry):
        r0 = pl.multiple_of(i * QUERY_BLOCK, QUERY_BLOCK)
        qhs = [q_s[pl.ds(r0, QUERY_BLOCK), :] for q_s in (q0_s, q1_s)]

        def ktile(g, st):
            c0 = pl.multiple_of(g * kt, kt)
            d = (r0 - c0) + qk
            count = jnp.zeros(d.shape, F32)
            for (window, _), stride_ok in zip(C_PATTERNS, on_stride):
                count = count + jnp.where(d <= window, stride_ok, 0.0)
            count = jnp.where(d >= 0, count, 0.0)
            kb, vb = k_s[pl.ds(c0, kt), :], v_s[pl.ds(c0, kt), :]
            scores = [_dot_nt(qhs[hh], kb) for hh in range(2)]
            new, prs = [], []
            for hh in range(2):
                m, l, acc = st[3 * hh:3 * hh + 3]
                s = jnp.where(count > 0.0, scores[hh], NEG_BIG)
                m_new = jnp.maximum(m, jnp.max(s, axis=1, keepdims=True))
                pr = count * jnp.exp(s - m_new)
                alpha = jnp.exp(m - m_new)
                new += [m_new, alpha * l + jnp.sum(pr, axis=1, keepdims=True), alpha * acc]
                prs.append(pr.astype(BF16))
            for hh in range(2):
                new[3 * hh + 2] = new[3 * hh + 2] + _dot(prs[hh], vb)
            return tuple(new)

        init = (jnp.full((QUERY_BLOCK, 1), NEG_BIG, F32), jnp.zeros((QUERY_BLOCK, 1), F32),
                jnp.zeros((QUERY_BLOCK, LANES), F32)) * 2
        st = lax.fori_loop(0, (r0 + QUERY_BLOCK + kt - 1) // kt, ktile, init)
        _store_heads(o_ref, r0, [st[2] / st[1], st[5] / st[4]])
        return carry

    lax.fori_loop(0, seq // QUERY_BLOCK, qblock, 0)


def _qkv_attention_call(kernel, name, proj, qcol, n_heads, cos, sin, gq, gk, extra_scratch=()):
    b, seq, _ = proj.shape
    n_pairs = n_heads // 2
    q_spec, k_spec, v_spec = _attn_specs(seq, n_pairs, qcol, qcol + n_pairs, qcol + 2 * n_pairs, False)
    return pl.pallas_call(
        kernel,
        grid=(b, n_pairs),
        in_specs=[q_spec, k_spec, v_spec, _row_spec(seq), _row_spec(seq), _gain_spec(), _gain_spec()],
        out_specs=pl.BlockSpec((1, seq, LANES), lambda b_, p: (b_, 0, p)),
        out_shape=jax.ShapeDtypeStruct((b, seq, n_pairs * LANES), F32),
        scratch_shapes=[pltpu.VMEM((seq, LANES), BF16)] * 4 + list(extra_scratch),
        compiler_params=_cparams(2, V7X_VMEM_LIMIT),
        name=name,
    )(proj, proj, proj, cos, sin, gq, gk)


def _moba_kernel(q_ref, k_ref, v_ref, cos_ref, sin_ref, gq_ref, gk_ref, o_ref,
                 q0_s, q1_s, k_s, v_s, km_s, sel0_s, sel1_s):
    seq = q_ref.shape[1]
    n_blocks = seq // MOBA_BLOCK
    qn, kn = _prep_qkv(q_ref, k_ref, v_ref, cos_ref, sin_ref, gq_ref, gk_ref, q0_s, q1_s, k_s, v_s)
    km_s[...] = jnp.zeros(km_s.shape, F32)
    km_s[0:n_blocks, :] = jnp.mean(kn.reshape(n_blocks, MOBA_BLOCK, LANES), axis=1)
    first, second = _head_masks()

    rows8 = _iota((8, seq), 0)
    own8 = _iota((8, seq), 1) // MOBA_BLOCK
    valid = rows8 < own8
    for msk, sel_s in ((first, sel0_s), (second, sel1_s)):
        gate = _dot3_nt(km_s[...], jnp.where(msk, qn, 0.0))[0:8, :]
        gm = jnp.where(valid, gate, -jnp.inf)
        rank = jnp.zeros((8, seq), F32)
        for n2 in range(n_blocks):
            g2 = gm[n2:n2 + 1, :]
            beats = jnp.logical_or(g2 > gm, jnp.logical_and(g2 == gm, n2 < rows8))
            rank = rank + jnp.where(jnp.logical_and(beats, n2 < own8), 1.0, 0.0)
        sel = jnp.where(jnp.logical_and(valid, rank < float(MOBA_TOPK)), 1.0, 0.0)
        sel = jnp.concatenate([sel, jnp.zeros((LANES - 8, seq), F32)], axis=0)
        sel_s[...] = sel.T

    lane_sq = _iota((QUERY_BLOCK, LANES), 1)
    qi = _iota((QUERY_BLOCK, MOBA_BLOCK), 0)
    ki = _iota((QUERY_BLOCK, MOBA_BLOCK), 1)
    second_block = _iota((QUERY_BLOCK, 2 * MOBA_BLOCK), 1) >= MOBA_BLOCK

    def qblock(i, carry):
        r0 = pl.multiple_of(i * QUERY_BLOCK, QUERY_BLOCK)
        own = (i * QUERY_BLOCK) // MOBA_BLOCK
        own0 = pl.multiple_of(own * MOBA_BLOCK, MOBA_BLOCK)
        qhs = [q_s[pl.ds(r0, QUERY_BLOCK), :] for q_s in (q0_s, q1_s)]
        sels = [sel_s[pl.ds(r0, QUERY_BLOCK), :] for sel_s in (sel0_s, sel1_s)]
        kb, vb = k_s[pl.ds(own0, MOBA_BLOCK), :], v_s[pl.ds(own0, MOBA_BLOCK), :]
        causal = (own0 + ki) <= (r0 + qi)
        scores = [_dot_nt(qhs[hh], kb) for hh in range(2)]
        init, prs = [], []
        for hh in range(2):
            s = jnp.where(causal, scores[hh], NEG_BIG)
            m = jnp.max(s, axis=1, keepdims=True)
            pr = jnp.exp(s - m)
            init += [m, jnp.sum(pr, axis=1, keepdims=True), None]
            prs.append(pr.astype(BF16))
        for hh in range(2):
            init[3 * hh + 2] = _dot(prs[hh], vb)

        def kpair(g, st):
            c0 = pl.multiple_of(g * 2 * MOBA_BLOCK, 2 * MOBA_BLOCK)
            kb2, vb2 = k_s[pl.ds(c0, 2 * MOBA_BLOCK), :], v_s[pl.ds(c0, 2 * MOBA_BLOCK), :]
            scores = [_dot_nt(qhs[hh], kb2) for hh in range(2)]
            new, prs = [], []
            for hh in range(2):
                m, l, acc = st[3 * hh:3 * hh + 3]
                sel_a = jnp.sum(jnp.where(lane_sq == 2 * g, sels[hh], 0.0), axis=1, keepdims=True)
                sel_b = jnp.sum(jnp.where(lane_sq == 2 * g + 1, sels[hh], 0.0), axis=1, keepdims=True)
                keep = jnp.where(second_block, sel_b, sel_a) > 0.0
                s = jnp.where(keep, scores[hh], NEG_BIG)
                m_new = jnp.maximum(m, jnp.max(s, axis=1, keepdims=True))
                pr = jnp.exp(s - m_new)
                alpha = jnp.exp(m - m_new)
                new += [m_new, alpha * l + jnp.sum(pr, axis=1, keepdims=True), alpha * acc]
                prs.append(pr.astype(BF16))
            for hh in range(2):
                new[3 * hh + 2] = new[3 * hh + 2] + _dot(prs[hh], vb2)
            return tuple(new)

        st = lax.fori_loop(0, (own + 1) // 2, kpair, tuple(init))
        _store_heads(o_ref, r0, [st[2] / st[1], st[5] / st[4]])
        return carry

    lax.fori_loop(0, seq // QUERY_BLOCK, qblock, 0)


ROUTE_CHUNKS_PER_STEP = 4


def _oddeven_merge_sort_pairs(n):
    pairs = []

    def merge(lo, hi, r):
        step = 2 * r
        if step < hi - lo:
            merge(lo, hi, step)
            merge(lo + r, hi, step)
            pairs.extend((i, i + r) for i in range(lo + r, hi - r, step))
        else:
            pairs.append((lo, lo + r))

    def sort(lo, hi):
        if hi - lo >= 1:
            mid = lo + (hi - lo) // 2
            sort(lo, mid)
            sort(mid + 1, hi)
            merge(lo, hi, 1)

    sort(0, n - 1)
    return pairs


SUBLANES = 8


def _top16_rows(scores, n_rows, vals_refs, idx_refs):
    n_slabs = n_rows // SUBLANES
    sub = _iota((SUBLANES, LANES), 0)
    vals = [[s[SUBLANES * v:SUBLANES * (v + 1), :] for v in range(n_slabs)] for s in scores]
    idxs = [[sub + SUBLANES * v for v in range(n_slabs)] for _ in scores]
    for i, j in _oddeven_merge_sort_pairs(n_slabs):
        for va, ia in zip(vals, idxs):
            a, b = va[i], va[j]
            a_first = jnp.logical_or(a > b, jnp.logical_and(a == b, ia[i] < ia[j]))
            va[i], va[j] = jnp.maximum(a, b), jnp.minimum(a, b)
            ia[i], ia[j] = jnp.where(a_first, ia[i], ia[j]), jnp.where(a_first, ia[j], ia[i])
    for it in range(PEER_TOPK):
        for k, (va, ia) in enumerate(zip(vals, idxs)):
            m = jnp.max(va[0], axis=0, keepdims=True)
            pick = jnp.min(jnp.where(va[0] == m, ia[0], n_rows), axis=0, keepdims=True)
            vals_refs[k][it:it + 1, :] = m
            idx_refs[k][it:it + 1, :] = pick
            win = ia[0] == pick
            depth = PEER_TOPK - 1 - it
            for d in range(min(depth, n_slabs - 1)):
                va[d] = jnp.where(win, va[d + 1], va[d])
                ia[d] = jnp.where(win, ia[d + 1], ia[d])
            if depth >= n_slabs:
                va[n_slabs - 1] = jnp.where(win, -jnp.inf, va[n_slabs - 1])


def _peer_route_kernel(x_ref, g_ref, sh_ref, sc_ref, wqt_ref, sk_ref, h_ref, ids_ref, gts_ref,
                       q_s, val_s, idx_s, cand_s, cidx_s):
    tt = x_ref.shape[0]
    n_chunks = tt // LANES
    per_step = ROUTE_CHUNKS_PER_STEP
    half = PEER_D_KEY // 2
    h = _adaln(x_ref[...], g_ref[...], sh_ref[0], sc_ref[0])
    h_ref[...] = h
    qt = _dot_nt(wqt_ref[...], h.astype(BF16))
    for c in range(n_chunks):
        q_s[c] = qt[:, c * LANES:(c + 1) * LANES]
    for k in range(per_step):
        cand_s[k, PEER_CAND_ROWS - 8:PEER_CAND_ROWS, :] = jnp.full((8, LANES), -jnp.inf, F32)
        cidx_s[k, PEER_CAND_ROWS - 8:PEER_CAND_ROWS, :] = jnp.zeros((8, LANES), I32)
    sk1, sk2 = sk_ref[0], sk_ref[1]
    vals = [val_s.at[i] for i in range(2 * per_step)]
    idxs = [idx_s.at[i] for i in range(2 * per_step)]

    def body(step, carry):
        hh = step // (n_chunks // per_step)
        c0 = (step % (n_chunks // per_step)) * per_step
        q0 = pl.multiple_of(hh * PEER_D_KEY, PEER_D_KEY)
        scores = []
        for k in range(per_step):
            scores.append(_dot3(sk1, q_s[c0 + k, pl.ds(q0, half), :]))
            scores.append(_dot3(sk2, q_s[c0 + k, pl.ds(q0 + half, half), :]))
        _top16_rows(scores, PEER_N_KEYS, vals, idxs)
        cands, cidxs = [], []
        for k in range(per_step):
            v1, i1, v2, i2 = vals[2 * k], idxs[2 * k], vals[2 * k + 1], idxs[2 * k + 1]
            r = 0
            for a, nb in enumerate(PEER_CAND_COUNTS):
                cand_s[k, r:r + nb, :] = v1[a:a + 1, :] + v2[0:nb, :]
                cidx_s[k, r:r + nb, :] = i1[a:a + 1, :] * PEER_N_KEYS + i2[0:nb, :]
                r += nb
            cands.append(cand_s[k])
            cidxs.append(cidx_s[k])
        rows = _iota(cands[0].shape, 0)
        for it in range(PEER_TOPK):
            for k in range(per_step):
                m = jnp.max(cands[k], axis=0, keepdims=True)
                pick = jnp.min(jnp.where(cands[k] == m, rows, PEER_CAND_ROWS), axis=0, keepdims=True)
                hit = rows == pick
                idxs[2 * k][it:it + 1, :] = jnp.sum(jnp.where(hit, cidxs[k], 0), axis=0, keepdims=True)
                vals[2 * k][it:it + 1, :] = m
                cands[k] = jnp.where(hit, -jnp.inf, cands[k])
        r0 = pl.multiple_of(hh * PEER_TOPK, PEER_TOPK)
        for k in range(per_step):
            top = vals[2 * k][...]
            e = jnp.exp(top - top[0:1, :])
            gts_ref[c0 + k, pl.ds(r0, PEER_TOPK), :] = e / jnp.sum(e, axis=0, keepdims=True)
            ids_ref[c0 + k, pl.ds(r0, PEER_TOPK), :] = idxs[2 * k][...] * PACK_ROWS
        return carry

    lax.fori_loop(0, PEER_HEADS * n_chunks // per_step, body, 0)


def _peer_route(x2, g, shift, scale, wqt_bf, sub_keys, seq):
    t, d = x2.shape
    tt = 512
    per_b = seq // tt
    n_chunks = tt // LANES
    nq = wqt_bf.shape[0]
    out_blk = pl.BlockSpec((n_chunks, PEER_SLOTS, LANES), lambda i: (i, 0, 0))
    return pl.pallas_call(
        _peer_route_kernel,
        grid=(t // tt,),
        in_specs=[
            pl.BlockSpec((tt, d), lambda i: (i, 0)),
            pl.BlockSpec((1, d), lambda i: (0, 0)),
            pl.BlockSpec((1, 1, d), lambda i: (i // per_b, 0, 0)),
            pl.BlockSpec((1, 1, d), lambda i: (i // per_b, 0, 0)),
            pl.BlockSpec((nq, d), lambda i: (0, 0)),
            pl.BlockSpec(sub_keys.shape, lambda i: (0, 0, 0)),
        ],
        out_specs=[pl.BlockSpec((tt, d), lambda i: (i, 0)), out_blk, out_blk],
        out_shape=[jax.ShapeDtypeStruct((t, d), F32),
                   jax.ShapeDtypeStruct((t // LANES, PEER_SLOTS, LANES), I32),
                   jax.ShapeDtypeStruct((t // LANES, PEER_SLOTS, LANES), F32)],
        scratch_shapes=[pltpu.VMEM((n_chunks, nq, LANES), F32),
                        pltpu.VMEM((2 * ROUTE_CHUNKS_PER_STEP, PEER_TOPK, LANES), F32),
                        pltpu.VMEM((2 * ROUTE_CHUNKS_PER_STEP, PEER_TOPK, LANES), I32),
                        pltpu.VMEM((ROUTE_CHUNKS_PER_STEP, PEER_CAND_ROWS, LANES), F32),
                        pltpu.VMEM((ROUTE_CHUNKS_PER_STEP, PEER_CAND_ROWS, LANES), I32)],
        compiler_params=_cparams(1, V7X_VMEM_LIMIT),
        name="peer_route",
    )(x2, g, shift, scale, wqt_bf, sub_keys)


def _pack_table(tab):
    e, d = tab.shape
    bits = lax.bitcast_convert_type(tab.astype(BF16), jnp.uint16).astype(U32)
    bits = bits.reshape(e, d // (2 * LANES), 2, LANES)
    words = bits[:, :, 0, :] | (bits[:, :, 1, :] << 16)
    return words.reshape(e * (d // (2 * LANES)), LANES)


def _table_spec(rows):
    return pl.BlockSpec((rows, LANES), lambda i: (0, 0), pipeline_mode=pl.Buffered(1))


def _gelu_exact(a):
    return 0.5 * a * (1.0 + lax.erf(a * (2.0 ** -0.5)))


FEAT_CHUNKS = 8
SLOT_WIDTH = PEER_SLOTS * FEAT_CHUNKS


def _gather_rows(ids_ref, base, tbl_ref, slot):
    for j in range(PEER_SLOTS):
        if j % ID_VIEW == 0:
            ids_part = ids_ref.at[pl.ds(base + j, ID_VIEW)]
        row0 = pl.multiple_of(ids_part[j % ID_VIEW], PACK_ROWS)
        slot[PACK_ROWS * j:PACK_ROWS * (j + 1), :] = tbl_ref[pl.ds(row0, PACK_ROWS), :]


def _pipelined_tokens(tt, ids_ref, tbl_ref, slots, compute):
    group = len(slots) // 2
    first, second = slots[:group], slots[group:]
    last = tt - 1

    def gather(t, slot):
        _gather_rows(ids_ref, jnp.minimum(t, last) * PEER_SLOTS, tbl_ref, slot)

    for k in range(group):
        gather(k, first[k])

    def step(q, carry):
        t = 2 * group * q
        for k in range(group):
            compute(t + k, first[k])
        for k in range(group):
            gather(t + group + k, second[k])
        for k in range(group):
            compute(t + group + k, second[k])
        for k in range(group):
            gather(t + 2 * group + k, first[k])
        return carry

    lax.fori_loop(0, tt // (2 * group), step, 0)


def _chunk_diag():
    return (_iota((FEAT_CHUNKS, SLOT_WIDTH), 1) % FEAT_CHUNKS) == _iota((FEAT_CHUNKS, SLOT_WIDTH), 0)


def _peer_u_kernel(ids_ref, h_ref, g_ref, tbl_ref, coef_ref, *scratch):
    slots, (hx_s, rs_s) = scratch[:N_SLOTS], scratch[N_SLOTS:]
    tt = h_ref.shape[0]
    for c in range(FEAT_CHUNKS):
        hx_s[pl.ds(c, tt, stride=FEAT_CHUNKS), :] = h_ref[:, c * LANES:(c + 1) * LANES]
    diag = _chunk_diag()

    def compute(t, slot):
        rows = pltpu.bitcast(slot[...], BF16)
        x8 = hx_s[pl.ds(pl.multiple_of(t * FEAT_CHUNKS, FEAT_CHUNKS), FEAT_CHUNKS), :]
        part = _dot_nt(x8.astype(BF16), rows)
        rs_s[pl.ds(t, 1), :] = jnp.sum(jnp.where(diag, part, 0.0), axis=0, keepdims=True)

    _pipelined_tokens(tt, ids_ref, tbl_ref, slots, compute)
    group = jnp.where(_iota((SLOT_WIDTH, PEER_SLOTS), 0) // FEAT_CHUNKS == _iota((SLOT_WIDTH, PEER_SLOTS), 1),
                      1.0, 0.0).astype(BF16)
    hi, lo = _split_bf16(rs_s[...])
    act = _dot(hi, group) + _dot(lo, group)
    coef_ref[...] = g_ref[...] * _gelu_exact(act)


N_SLOTS = 16
ID_VIEW = 16
PEER_TOKEN_TILE = 512


def _slot_scratch():
    return [pltpu.VMEM((PEER_SLOTS * PACK_ROWS, LANES), U32)] * N_SLOTS


def _peer_u(ids_flat, h, gates, table, tt):
    t, d = h.shape
    return pl.pallas_call(
        _peer_u_kernel,
        grid=(t // tt,),
        in_specs=[
            pl.BlockSpec((tt * PEER_SLOTS,), lambda i: (i,), memory_space=pltpu.SMEM),
            pl.BlockSpec((tt, d), lambda i: (i, 0)),
            pl.BlockSpec((tt, PEER_SLOTS), lambda i: (i, 0)),
            _table_spec(table.shape[0]),
        ],
        out_specs=pl.BlockSpec((tt, PEER_SLOTS), lambda i: (i, 0)),
        out_shape=jax.ShapeDtypeStruct((t, PEER_SLOTS), F32),
        scratch_shapes=_slot_scratch() + [
                        pltpu.VMEM((tt * FEAT_CHUNKS, LANES), F32),
                        pltpu.VMEM((tt, SLOT_WIDTH), F32)],
        compiler_params=_cparams(1, V7X_VMEM_LIMIT),
        name="peer_expert_in",
    )(ids_flat, h, gates, table)


def _peer_v_kernel(ids_ref, coef_ref, x_ref, gate_ref, tbl_ref, o_ref, *scratch):
    slots, (ce_hi_s, ce_lo_s, res_s) = scratch[:N_SLOTS], scratch[N_SLOTS:]
    tt = x_ref.shape[0]
    spread = jnp.where(_iota((PEER_SLOTS, SLOT_WIDTH), 1) // FEAT_CHUNKS == _iota((PEER_SLOTS, SLOT_WIDTH), 0),
                       1.0, 0.0).astype(BF16)
    hi, lo = _split_bf16(coef_ref[...])
    ce_hi_s[...] = _dot(hi, spread)
    ce_lo_s[...] = _dot(lo, spread)
    diag = _chunk_diag()

    def compute(t, slot):
        rows = pltpu.bitcast(slot[...], BF16)
        a_hi = jnp.where(diag, ce_hi_s[pl.ds(t, 1), :], 0.0)
        a_lo = jnp.where(diag, ce_lo_s[pl.ds(t, 1), :], 0.0)
        both = _dot(jnp.concatenate([a_hi, a_lo], axis=0).astype(BF16), rows)
        r0 = pl.multiple_of(t * FEAT_CHUNKS, FEAT_CHUNKS)
        res_s[pl.ds(r0, FEAT_CHUNKS), :] = both[0:FEAT_CHUNKS, :] + both[FEAT_CHUNKS:2 * FEAT_CHUNKS, :]

    _pipelined_tokens(tt, ids_ref, tbl_ref, slots, compute)
    for c in range(FEAT_CHUNKS):
        cols = slice(c * LANES, (c + 1) * LANES)
        y = res_s[pl.ds(c, tt, stride=FEAT_CHUNKS), :]
        o_ref[:, cols] = x_ref[:, cols] + gate_ref[0][:, cols] * y


def _peer_v(ids_flat, coef, x2, gate, table, seq, tt):
    t, d = x2.shape
    per_b = seq // tt
    blk = pl.BlockSpec((tt, d), lambda i: (i, 0))
    return pl.pallas_call(
        _peer_v_kernel,
        grid=(t // tt,),
        in_specs=[
            pl.BlockSpec((tt * PEER_SLOTS,), lambda i: (i,), memory_space=pltpu.SMEM),
            pl.BlockSpec((tt, PEER_SLOTS), lambda i: (i, 0)),
            blk,
            pl.BlockSpec((1, 1, d), lambda i: (i // per_b, 0, 0)),
            _table_spec(table.shape[0]),
        ],
        out_specs=blk,
        out_shape=jax.ShapeDtypeStruct((t, d), F32),
        scratch_shapes=_slot_scratch() + [
                        pltpu.VMEM((tt, SLOT_WIDTH), F32), pltpu.VMEM((tt, SLOT_WIDTH), F32),
                        pltpu.VMEM((tt * FEAT_CHUNKS, LANES), F32)],
        compiler_params=_cparams(1, V7X_VMEM_LIMIT),
        name="peer_expert_out",
    )(ids_flat, coef, x2, gate, table)


def _peer_ffn(x2, g, shift, scale, gate, wq, sub_keys, table_u, table_v, seq):
    t, d = x2.shape
    tt = PEER_TOKEN_TILE
    h, ids, gts = _peer_route(x2, g, shift, scale, wq.T.astype(BF16), sub_keys, seq)
    ids_flat = ids.transpose(0, 2, 1).reshape(t * PEER_SLOTS)
    gates = gts.transpose(0, 2, 1).reshape(t, PEER_SLOTS)
    coef = _peer_u(ids_flat, h, gates, table_u, tt)
    return _peer_v(ids_flat, coef, x2, gate, table_v, seq, tt)


def _rope_tables(seq):
    half = HEAD_DIM // 2
    inv_freq = ROPE_THETA ** (-jnp.arange(half, dtype=F32) / half)
    ang = jnp.arange(seq).astype(F32)[:, None] * inv_freq[None, :]
    reps = LANES // half
    return jnp.tile(jnp.cos(ang), (1, reps)), jnp.tile(jnp.sin(ang), (1, reps))


def _two_heads(gain):
    return jnp.tile(gain.reshape(1, HEAD_DIM), (1, LANES // HEAD_DIM))


def kernel(x, c, ada_w, ada_b, norm_mix_g, norm_ffn_g, w_in_ab, w_out_ab, sinks_a, qnorm_a, knorm_a,
           w_in_cd, w_out_cd, qnorm_c, knorm_c, qnorm_d, knorm_d, peer_wq, peer_subkeys, peer_u, peer_v):
    b, seq, d = x.shape
    depth = ada_w.shape[0]
    t = b * seq
    cos, sin = _rope_tables(seq)
    mod = _modulation(c, ada_w, ada_b)
    x2 = x.reshape(t, d)
    for layer in range(depth):
        shift_m, scale_m, gate_m, shift_f, scale_f, gate_f = [
            m.reshape(b, 1, d) for m in jnp.split(mod[layer], 6, axis=-1)]
        g_mix = norm_mix_g[layer].reshape(1, d)
        i = layer // 2
        if layer % 2 == 0:
            proj = _norm_proj(x2, g_mix, shift_m, scale_m, w_in_ab[i].astype(BF16), seq)
            proj = proj.reshape(b, seq, -1)
            ya = _swa_attention(proj, sinks_a[i], cos, sin, _two_heads(qnorm_a[i]), _two_heads(knorm_a[i]))
            b_col = (A_Q_HEADS + 2 * A_KV_HEADS) * HEAD_DIM // LANES
            yb = _stick_attention(proj, b_col)
            w_out = w_out_ab[i]
        else:
            proj = _norm_proj(x2, g_mix, shift_m, scale_m, w_in_cd[i].astype(BF16), seq)
            proj = proj.reshape(b, seq, -1)
            ya = _qkv_attention_call(_dilated_kernel, "dilated_attention", proj, 0, C_HEADS, cos, sin,
                                     _two_heads(qnorm_c[i]), _two_heads(knorm_c[i]))
            d_col = 3 * C_HEADS * HEAD_DIM // LANES
            yb = _qkv_attention_call(_moba_kernel, "moba_attention", proj, d_col, D_HEADS, cos, sin,
                                     _two_heads(qnorm_d[i]), _two_heads(knorm_d[i]),
                                     extra_scratch=(pltpu.VMEM((LANES, LANES), F32),
                                                    pltpu.VMEM((seq, LANES), F32),
                                                    pltpu.VMEM((seq, LANES), F32)))
            w_out = w_out_cd[i]
        x2 = _out_proj(x2, ya.reshape(t, -1), yb.reshape(t, -1), w_out.astype(BF16), gate_m, seq)
        x2 = _peer_ffn(x2, norm_ffn_g[layer].reshape(1, d), shift_f, scale_f, gate_f,
                       peer_wq[layer], peer_subkeys[layer],
                       _pack_table(peer_u[layer]), _pack_table(peer_v[layer]), seq)
    return x2.reshape(b, seq, d)
```

```python
import functools

import jax
import jax.numpy as jnp
from jax import lax
from jax.experimental import pallas as pl
from jax.experimental.pallas import tpu as pltpu

F32 = jnp.float32
BF16 = jnp.bfloat16
I32 = jnp.int32
U32 = jnp.uint32

HEAD_DIM = 64
ROPE_THETA = 10000.0
NORM_EPS = 1e-6
LANES = 128
QUERY_BLOCK = 128
A_Q_HEADS, A_KV_HEADS = 8, 2
B_HEADS = C_HEADS = D_HEADS = 8
C_PATTERNS = ((128, 1), (512, 4), (2048, 16))
MOBA_BLOCK, MOBA_TOPK = 256, 3
PEER_HEADS, PEER_N_KEYS, PEER_TOPK, PEER_D_KEY = 8, 128, 16, 256
PEER_SLOTS = PEER_HEADS * PEER_TOPK
NEG_BIG = -1e30
PEER_CAND_COUNTS = tuple(PEER_TOPK // (a + 1) for a in range(PEER_TOPK))
PEER_N_CAND = sum(PEER_CAND_COUNTS)
PEER_CAND_ROWS = 56
PACK_ROWS = 4
V7X_VMEM_LIMIT = 56 * 1024 * 1024


def _cparams(n_axes, vmem=None):
    return pltpu.CompilerParams(
        dimension_semantics=("arbitrary",) * n_axes,
        vmem_limit_bytes=vmem)


def _split_bf16(a):
    hi = a.astype(BF16)
    lo = (a - hi.astype(F32)).astype(BF16)
    return hi, lo


def _dot(a, b):
    return jnp.dot(a, b, preferred_element_type=F32)


def _dot_nt(a, b):
    return lax.dot_general(a, b, (((1,), (1,)), ((), ())), preferred_element_type=F32)


def _dot3(a, b):
    ah, al = _split_bf16(a)
    bh, bl = _split_bf16(b)
    return _dot(ah, bh) + _dot(ah, bl) + _dot(al, bh)


def _dot3_nt(a, b):
    ah, al = _split_bf16(a)
    bh, bl = _split_bf16(b)
    return _dot_nt(ah, bh) + _dot_nt(ah, bl) + _dot_nt(al, bh)


def _iota(shape, dim):
    return lax.broadcasted_iota(I32, shape, dim)


def _mod_kernel(c_ref, w_ref, b_ref, o_ref):
    c = c_ref[...]
    cond = c * jax.nn.sigmoid(c)
    o_ref[0] = _dot3(cond, w_ref[0]) + b_ref[0]


def _modulation(c, ada_w, ada_b):
    depth, d, n = ada_w.shape
    b = c.shape[0]
    tn = 1024
    return pl.pallas_call(
        _mod_kernel,
        grid=(depth, n // tn),
        in_specs=[
            pl.BlockSpec((b, d), lambda l, j: (0, 0)),
            pl.BlockSpec((1, d, tn), lambda l, j: (l, 0, j)),
            pl.BlockSpec((1, 1, tn), lambda l, j: (l, 0, j)),
        ],
        out_specs=pl.BlockSpec((1, b, tn), lambda l, j: (l, 0, j)),
        out_shape=jax.ShapeDtypeStruct((depth, b, n), F32),
        compiler_params=_cparams(2, V7X_VMEM_LIMIT),
        name="adaln_modulation",
    )(c, ada_w, ada_b.reshape(depth, 1, n))


def _adaln(x, g, shift, scale):
    ms = jnp.mean(x * x, axis=-1, keepdims=True)
    y = x * lax.rsqrt(ms + NORM_EPS) * g
    return y * (1.0 + scale) + shift


def _norm_proj_kernel(x_ref, g_ref, sh_ref, sc_ref, w_ref, o_ref):
    h = _adaln(x_ref[...], g_ref[...], sh_ref[0], sc_ref[0])
    o_ref[...] = _dot(h.astype(BF16), w_ref[...])


def _norm_proj(x2, g, shift, scale, w_bf, seq):
    t, d = x2.shape
    n = w_bf.shape[1]
    tt = 512
    per_b = seq // tt
    return pl.pallas_call(
        _norm_proj_kernel,
        grid=(t // tt,),
        in_specs=[
            pl.BlockSpec((tt, d), lambda i: (i, 0)),
            pl.BlockSpec((1, d), lambda i: (0, 0)),
            pl.BlockSpec((1, 1, d), lambda i: (i // per_b, 0, 0)),
            pl.BlockSpec((1, 1, d), lambda i: (i // per_b, 0, 0)),
            pl.BlockSpec((d, n), lambda i: (0, 0)),
        ],
        out_specs=pl.BlockSpec((tt, n), lambda i: (i, 0)),
        out_shape=jax.ShapeDtypeStruct((t, n), F32),
        compiler_params=_cparams(1, V7X_VMEM_LIMIT),
        name="adaln_in_proj",
    )(x2, g, shift, scale, w_bf)


def _out_proj_kernel(x_ref, ya_ref, yb_ref, w_ref, gate_ref, o_ref):
    half = ya_ref.shape[1]
    y = _dot(ya_ref[...].astype(BF16), w_ref[0:half, :])
    y = y + _dot(yb_ref[...].astype(BF16), w_ref[half:2 * half, :])
    o_ref[...] = x_ref[...] + gate_ref[0] * y


def _out_proj(x2, ya, yb, w_bf, gate, seq):
    t, d = x2.shape
    half = ya.shape[1]
    tt = 512
    per_b = seq // tt
    return pl.pallas_call(
        _out_proj_kernel,
        grid=(t // tt,),
        in_specs=[
            pl.BlockSpec((tt, d), lambda i: (i, 0)),
            pl.BlockSpec((tt, half), lambda i: (i, 0)),
            pl.BlockSpec((tt, half), lambda i: (i, 0)),
            pl.BlockSpec((2 * half, d), lambda i: (0, 0)),
            pl.BlockSpec((1, 1, d), lambda i: (i // per_b, 0, 0)),
        ],
        out_specs=pl.BlockSpec((tt, d), lambda i: (i, 0)),
        out_shape=jax.ShapeDtypeStruct((t, d), F32),
        compiler_params=_cparams(1, V7X_VMEM_LIMIT),
        name="mixer_out_proj",
    )(x2, ya, yb, w_bf, gate)


def _lane_row():
    return _iota((1, LANES), 1)


def _head_segment_ones():
    r = _iota((LANES, LANES), 0) // HEAD_DIM
    c = _iota((LANES, LANES), 1) // HEAD_DIM
    return jnp.where(r == c, 1.0, 0.0).astype(BF16)


def _headnorm_rope(a, g, cos, sin):
    hi, lo = _split_bf16(a * a)
    seg = _head_segment_ones()
    ms = (_dot(hi, seg) + _dot(lo, seg)) * (1.0 / HEAD_DIM)
    y = a * lax.rsqrt(ms + NORM_EPS) * g
    half = HEAD_DIM // 2
    upper = pltpu.roll(y, LANES - half, axis=1)
    lower = pltpu.roll(y, half, axis=1)
    first_half = (_lane_row() % HEAD_DIM) < half
    rot = jnp.where(first_half, -upper, lower)
    return y * cos + rot * sin


def _head_masks():
    lane = _lane_row()
    return lane < HEAD_DIM, lane >= HEAD_DIM


def _attn_specs(seq, n_pairs, qcol, kcol, vcol, kv_shared):
    blk = (1, seq, LANES)
    q_spec = pl.BlockSpec(blk, lambda b, p: (b, 0, qcol + p))
    if kv_shared:
        k_spec = pl.BlockSpec(blk, lambda b, p: (b, 0, kcol))
        v_spec = pl.BlockSpec(blk, lambda b, p: (b, 0, vcol))
    else:
        k_spec = pl.BlockSpec(blk, lambda b, p: (b, 0, kcol + p))
        v_spec = pl.BlockSpec(blk, lambda b, p: (b, 0, vcol + p))
    return q_spec, k_spec, v_spec


def _row_spec(seq):
    return pl.BlockSpec((seq, LANES), lambda b, p: (0, 0))


def _gain_spec():
    return pl.BlockSpec((1, LANES), lambda b, p: (0, 0))


def _store_heads(o_ref, r0, outs):
    first, _ = _head_masks()
    o_ref[0, pl.ds(r0, QUERY_BLOCK), :] = jnp.where(first, outs[0], outs[1])


def _swa_kernel(sinks_ref, q_ref, k_ref, v_ref, cos_ref, sin_ref, gq_ref, gk_ref, o_ref,
                q0_s, q1_s, k_s, v_s):
    p = pl.program_id(1)
    seq = q_ref.shape[1]
    cos, sin = cos_ref[...], sin_ref[...]
    first, second = _head_masks()
    qn = _headnorm_rope(q_ref[0], gq_ref[...], cos, sin) * (HEAD_DIM ** -0.5)
    q0_s[...] = jnp.where(first, qn, 0.0).astype(BF16)
    q1_s[...] = jnp.where(second, qn, 0.0).astype(BF16)
    pairs_per_kv = (A_Q_HEADS // A_KV_HEADS) // 2
    keep = jnp.logical_xor(first, (p // pairs_per_kv) == 1)
    kn = _headnorm_rope(k_ref[0], gk_ref[...], cos, sin)
    k_s[...] = jnp.where(keep, kn, pltpu.roll(kn, HEAD_DIM, axis=1)).astype(BF16)
    v = v_ref[0]
    v_s[...] = jnp.where(keep, v, pltpu.roll(v, HEAD_DIM, axis=1)).astype(BF16)

    qi = _iota((QUERY_BLOCK, QUERY_BLOCK), 0)
    ki = _iota((QUERY_BLOCK, QUERY_BLOCK), 1)

    def qblock(i, carry):
        r0 = pl.multiple_of(i * QUERY_BLOCK, QUERY_BLOCK)
        rp = pl.multiple_of(jnp.maximum(i - 1, 0) * QUERY_BLOCK, QUERY_BLOCK)
        kc, kp = k_s[pl.ds(r0, QUERY_BLOCK), :], k_s[pl.ds(rp, QUERY_BLOCK), :]
        vc, vp = v_s[pl.ds(r0, QUERY_BLOCK), :], v_s[pl.ds(rp, QUERY_BLOCK), :]
        mask_c = ki <= qi
        mask_p = jnp.logical_and(ki > qi, i > 0)
        qhs = [q_s[pl.ds(r0, QUERY_BLOCK), :] for q_s in (q0_s, q1_s)]
        scores = [(_dot_nt(qh, kc), _dot_nt(qh, kp)) for qh in qhs]
        probs, denoms = [], []
        for hh in range(2):
            sc = jnp.where(mask_c, scores[hh][0], NEG_BIG)
            sp = jnp.where(mask_p, scores[hh][1], NEG_BIG)
            sink = sinks_ref[2 * p + hh]
            m = jnp.maximum(jnp.max(sc, axis=1, keepdims=True), jnp.max(sp, axis=1, keepdims=True))
            m = jnp.maximum(m, sink)
            ec, ep = jnp.exp(sc - m), jnp.exp(sp - m)
            denoms.append(jnp.sum(ec, axis=1, keepdims=True) + jnp.sum(ep, axis=1, keepdims=True)
                          + jnp.exp(sink - m))
            probs.append((ec.astype(BF16), ep.astype(BF16)))
        outs = [(_dot(probs[hh][0], vc) + _dot(probs[hh][1], vp)) / denoms[hh] for hh in range(2)]
        _store_heads(o_ref, r0, outs)
        return carry

    lax.fori_loop(0, seq // QUERY_BLOCK, qblock, 0)


def _swa_attention(proj, sinks, cos, sin, gq, gk):
    b, seq, _ = proj.shape
    n_pairs = A_Q_HEADS // 2
    kcol = A_Q_HEADS * HEAD_DIM // LANES
    vcol = kcol + A_KV_HEADS * HEAD_DIM // LANES
    q_spec, k_spec, v_spec = _attn_specs(seq, n_pairs, 0, kcol, vcol, True)
    return pl.pallas_call(
        _swa_kernel,
        grid=(b, n_pairs),
        in_specs=[pl.BlockSpec(memory_space=pltpu.SMEM), q_spec, k_spec, v_spec,
                  _row_spec(seq), _row_spec(seq), _gain_spec(), _gain_spec()],
        out_specs=pl.BlockSpec((1, seq, LANES), lambda b_, p: (b_, 0, p)),
        out_shape=jax.ShapeDtypeStruct((b, seq, n_pairs * LANES), F32),
        scratch_shapes=[pltpu.VMEM((seq, LANES), BF16)] * 4,
        compiler_params=_cparams(2, V7X_VMEM_LIMIT),
        name="swa_gqa_attention",
    )(sinks, proj, proj, proj, cos, sin, gq, gk)


STICK_GROUP = 4
STICK_QUERY_ROWS = 256


def _stick_kernel(q_ref, k_ref, v_ref, o_ref, k_s, v_s):
    seq = q_ref.shape[1]
    k_s[...] = k_ref[0].astype(BF16)
    v_s[...] = v_ref[0].astype(BF16)
    first, second = _head_masks()
    qr = min(STICK_QUERY_ROWS, seq)
    kw = QUERY_BLOCK
    qi = _iota((qr, kw), 0)
    ki = _iota((qr, kw), 1)
    wr = _iota((2 * kw, 2 * kw), 0) % kw
    wc = _iota((2 * kw, 2 * kw), 1)
    suffix_w = jnp.where(jnp.logical_or(wc >= kw, wr > wc), 1.0, 0.0).astype(BF16)

    def qblock(i, carry):
        r0 = pl.multiple_of(i * qr, qr)
        q = q_ref[0, pl.ds(r0, qr), :] * (HEAD_DIM ** -0.5)
        qhs = [jnp.where(msk, q, 0.0).astype(BF16) for msk in (first, second)]
        n_blocks = (r0 + qr) // kw

        def kgroup(g, st):
            accs, laters = [st[0], st[1]], [st[2], st[3]]
            chains = [(u, hh) for u in range(STICK_GROUP) for hh in range(2)]
            vbs, pasts, zs = [], [], {}
            for u in range(STICK_GROUP):
                j = n_blocks - 1 - (g * STICK_GROUP + u)
                live = j >= 0
                c0 = pl.multiple_of(jnp.maximum(j, 0) * kw, kw)
                kb = k_s[pl.ds(c0, kw), :]
                vbs.append(v_s[pl.ds(c0, kw), :])
                pasts.append(jnp.logical_and((c0 + ki) < (r0 + qi), live))
                for hh in range(2):
                    zs[u, hh] = _dot_nt(qhs[hh], kb)
            logit, sums = {}, {}
            for u, hh in chains:
                z = zs[u, hh]
                sp = jnp.maximum(z, 0.0) + jnp.log(1.0 + jnp.exp(-jnp.abs(z)))
                log_keep = jnp.where(pasts[u], -sp, 0.0)
                logit[u, hh] = z - sp
                hi, lo = _split_bf16(log_keep)
                sums[u, hh] = _dot(jnp.concatenate([hi, lo], axis=1), suffix_w)
            ws = {}
            for u, hh in chains:
                inner, total = sums[u, hh][:, :kw], sums[u, hh][:, kw:]
                ws[u, hh] = jnp.where(pasts[u], jnp.exp(logit[u, hh] + inner + laters[hh]), 0.0).astype(BF16)
                laters[hh] = laters[hh] + total
            for u, hh in chains:
                accs[hh] = accs[hh] + _dot(ws[u, hh], vbs[u])
            return accs[0], accs[1], laters[0], laters[1]

        zero = jnp.zeros((qr, LANES), F32)
        n_groups = (n_blocks + STICK_GROUP - 1) // STICK_GROUP
        st = lax.fori_loop(0, n_groups, kgroup, (zero, zero, zero, zero))
        o_ref[0, pl.ds(r0, qr), :] = jnp.where(first, st[0], st[1])
        return carry

    lax.fori_loop(0, seq // qr, qblock, 0)


def _stick_attention(proj, qcol):
    b, seq, _ = proj.shape
    n_pairs = B_HEADS // 2
    q_spec, k_spec, v_spec = _attn_specs(seq, n_pairs, qcol, qcol + n_pairs, qcol + 2 * n_pairs, False)
    return pl.pallas_call(
        _stick_kernel,
        grid=(b, n_pairs),
        in_specs=[q_spec, k_spec, v_spec],
        out_specs=pl.BlockSpec((1, seq, LANES), lambda b_, p: (b_, 0, p)),
        out_shape=jax.ShapeDtypeStruct((b, seq, n_pairs * LANES), F32),
        scratch_shapes=[pltpu.VMEM((seq, LANES), BF16)] * 2,
        compiler_params=_cparams(2, V7X_VMEM_LIMIT),
        name="stick_breaking_attention",
    )(proj, proj, proj)


def _prep_qkv(q_ref, k_ref, v_ref, cos_ref, sin_ref, gq_ref, gk_ref, q0_s, q1_s, k_s, v_s):
    cos, sin = cos_ref[...], sin_ref[...]
    first, second = _head_masks()
    qn = _headnorm_rope(q_ref[0], gq_ref[...], cos, sin) * (HEAD_DIM ** -0.5)
    q0_s[...] = jnp.where(first, qn, 0.0).astype(BF16)
    q1_s[...] = jnp.where(second, qn, 0.0).astype(BF16)
    kn = _headnorm_rope(k_ref[0], gk_ref[...], cos, sin)
    k_s[...] = kn.astype(BF16)
    v_s[...] = v_ref[0].astype(BF16)
    return qn, kn


DILATED_KEY_TILE = 512
DILATED_QUERY_ROWS = 256


def _dilated_kernel(q_ref, k_ref, v_ref, cos_ref, sin_ref, gq_ref, gk_ref, o_ref,
                    q0_s, q1_s, k_s, v_s):
    seq = q_ref.shape[1]
    _prep_qkv(q_ref, k_ref, v_ref, cos_ref, sin_ref, gq_ref, gk_ref, q0_s, q1_s, k_s, v_s)
    kt = min(DILATED_KEY_TILE, seq)
    qr = min(DILATED_QUERY_ROWS, seq)
    qk = _iota((qr, kt), 0) - _iota((qr, kt), 1)
    on_stride = [jnp.where((qk & (dil - 1)) == 0, 1.0, 0.0) for _, dil in C_PATTERNS]
    first_head, _ = _head_masks()

    def qblock(i, carry):
        r0 = pl.multiple_of(i * qr, qr)
        qhs = [q_s[pl.ds(r0, qr), :] for q_s in (q0_s, q1_s)]

        def ktile(g, st):
            c0 = pl.multiple_of(g * kt, kt)
            d = (r0 - c0) + qk
            count = jnp.zeros(d.shape, F32)
            for (window, _), stride_ok in zip(C_PATTERNS, on_stride):
                count = count + jnp.where(d <= window, stride_ok, 0.0)
            count = jnp.where(d >= 0, count, 0.0)
            kb, vb = k_s[pl.ds(c0, kt), :], v_s[pl.ds(c0, kt), :]
            scores = [_dot_nt(qhs[hh], kb) for hh in range(2)]
            new, prs = [], []
            for hh in range(2):
                m, l, acc = st[3 * hh:3 * hh + 3]
                s = jnp.where(count > 0.0, scores[hh], NEG_BIG)
                m_new = jnp.maximum(m, jnp.max(s, axis=1, keepdims=True))
                pr = count * jnp.exp(s - m_new)
                alpha = jnp.exp(m - m_new)
                new += [m_new, alpha * l + jnp.sum(pr, axis=1, keepdims=True), alpha * acc]
                prs.append(pr.astype(BF16))
            for hh in range(2):
                new[3 * hh + 2] = new[3 * hh + 2] + _dot(prs[hh], vb)
            return tuple(new)

        init = (jnp.full((qr, 1), NEG_BIG, F32), jnp.zeros((qr, 1), F32),
                jnp.zeros((qr, LANES), F32)) * 2
        st = lax.fori_loop(0, (r0 + qr + kt - 1) // kt, ktile, init)
        o_ref[0, pl.ds(r0, qr), :] = jnp.where(first_head, st[2] / st[1], st[5] / st[4])
        return carry

    lax.fori_loop(0, seq // qr, qblock, 0)


def _qkv_attention_call(kernel, name, proj, qcol, n_heads, cos, sin, gq, gk, extra_scratch=()):
    b, seq, _ = proj.shape
    n_pairs = n_heads // 2
    q_spec, k_spec, v_spec = _attn_specs(seq, n_pairs, qcol, qcol + n_pairs, qcol + 2 * n_pairs, False)
    return pl.pallas_call(
        kernel,
        grid=(b, n_pairs),
        in_specs=[q_spec, k_spec, v_spec, _row_spec(seq), _row_spec(seq), _gain_spec(), _gain_spec()],
        out_specs=pl.BlockSpec((1, seq, LANES), lambda b_, p: (b_, 0, p)),
        out_shape=jax.ShapeDtypeStruct((b, seq, n_pairs * LANES), F32),
        scratch_shapes=[pltpu.VMEM((seq, LANES), BF16)] * 4 + list(extra_scratch),
        compiler_params=_cparams(2, V7X_VMEM_LIMIT),
        name=name,
    )(proj, proj, proj, cos, sin, gq, gk)


def _moba_kernel(q_ref, k_ref, v_ref, cos_ref, sin_ref, gq_ref, gk_ref, o_ref,
                 q0_s, q1_s, k_s, v_s, km_s, sel0_s, sel1_s):
    seq = q_ref.shape[1]
    n_blocks = seq // MOBA_BLOCK
    qn, kn = _prep_qkv(q_ref, k_ref, v_ref, cos_ref, sin_ref, gq_ref, gk_ref, q0_s, q1_s, k_s, v_s)
    km_s[...] = jnp.zeros(km_s.shape, F32)
    km_s[0:n_blocks, :] = jnp.mean(kn.reshape(n_blocks, MOBA_BLOCK, LANES), axis=1)
    first, second = _head_masks()

    rows8 = _iota((8, seq), 0)
    own8 = _iota((8, seq), 1) // MOBA_BLOCK
    valid = rows8 < own8
    for msk, sel_s in ((first, sel0_s), (second, sel1_s)):
        gate = _dot3_nt(km_s[...], jnp.where(msk, qn, 0.0))[0:8, :]
        gm = jnp.where(valid, gate, -jnp.inf)
        rank = jnp.zeros((8, seq), F32)
        for n2 in range(n_blocks):
            g2 = gm[n2:n2 + 1, :]
            beats = jnp.logical_or(g2 > gm, jnp.logical_and(g2 == gm, n2 < rows8))
            rank = rank + jnp.where(jnp.logical_and(beats, n2 < own8), 1.0, 0.0)
        sel = jnp.where(jnp.logical_and(valid, rank < float(MOBA_TOPK)), 1.0, 0.0)
        sel = jnp.concatenate([sel, jnp.zeros((LANES - 8, seq), F32)], axis=0)
        sel_s[...] = sel.T

    qrows = MOBA_BLOCK
    lane_sq = _iota((qrows, LANES), 1)
    causal = _iota((qrows, MOBA_BLOCK), 1) <= _iota((qrows, MOBA_BLOCK), 0)
    second_block = _iota((qrows, 2 * MOBA_BLOCK), 1) >= MOBA_BLOCK
    first_head, _ = _head_masks()

    def qblock(own, carry):
        r0 = pl.multiple_of(own * qrows, qrows)
        qhs = [q_s[pl.ds(r0, qrows), :] for q_s in (q0_s, q1_s)]
        sels = [sel_s[pl.ds(r0, qrows), :] for sel_s in (sel0_s, sel1_s)]
        kb, vb = k_s[pl.ds(r0, MOBA_BLOCK), :], v_s[pl.ds(r0, MOBA_BLOCK), :]
        scores = [_dot_nt(qhs[hh], kb) for hh in range(2)]
        init, prs = [], []
        for hh in range(2):
            s = jnp.where(causal, scores[hh], NEG_BIG)
            m = jnp.max(s, axis=1, keepdims=True)
            pr = jnp.exp(s - m)
            init += [m, jnp.sum(pr, axis=1, keepdims=True), None]
            prs.append(pr.astype(BF16))
        for hh in range(2):
            init[3 * hh + 2] = _dot(prs[hh], vb)

        def kpair(g, st):
            c0 = pl.multiple_of(g * 2 * MOBA_BLOCK, 2 * MOBA_BLOCK)
            kb2, vb2 = k_s[pl.ds(c0, 2 * MOBA_BLOCK), :], v_s[pl.ds(c0, 2 * MOBA_BLOCK), :]
            scores = [_dot_nt(qhs[hh], kb2) for hh in range(2)]
            new, prs = [], []
            for hh in range(2):
                m, l, acc = st[3 * hh:3 * hh + 3]
                sel_a = jnp.sum(jnp.where(lane_sq == 2 * g, sels[hh], 0.0), axis=1, keepdims=True)
                sel_b = jnp.sum(jnp.where(lane_sq == 2 * g + 1, sels[hh], 0.0), axis=1, keepdims=True)
                keep = jnp.where(second_block, sel_b, sel_a) > 0.0
                s = jnp.where(keep, scores[hh], NEG_BIG)
                m_new = jnp.maximum(m, jnp.max(s, axis=1, keepdims=True))
                pr = jnp.exp(s - m_new)
                alpha = jnp.exp(m - m_new)
                new += [m_new, alpha * l + jnp.sum(pr, axis=1, keepdims=True), alpha * acc]
                prs.append(pr.astype(BF16))
            for hh in range(2):
                new[3 * hh + 2] = new[3 * hh + 2] + _dot(prs[hh], vb2)
            return tuple(new)

        st = lax.fori_loop(0, (own + 1) // 2, kpair, tuple(init))
        o_ref[0, pl.ds(r0, qrows), :] = jnp.where(first_head, st[2] / st[1], st[5] / st[4])
        return carry

    lax.fori_loop(0, seq // qrows, qblock, 0)


ROUTE_CHUNKS_PER_STEP = 4


def _oddeven_merge_sort_pairs(n):
    pairs = []

    def merge(lo, hi, r):
        step = 2 * r
        if step < hi - lo:
            merge(lo, hi, step)
            merge(lo + r, hi, step)
            pairs.extend((i, i + r) for i in range(lo + r, hi - r, step))
        else:
            pairs.append((lo, lo + r))

    def sort(lo, hi):
        if hi - lo >= 1:
            mid = lo + (hi - lo) // 2
            sort(lo, mid)
            sort(mid + 1, hi)
            merge(lo, hi, 1)

    sort(0, n - 1)
    return pairs


SUBLANES = 8


def _top16_rows(scores, n_rows, vals_refs, idx_refs):
    n_slabs = n_rows // SUBLANES
    sub = _iota((SUBLANES, LANES), 0)
    vals = [[s[SUBLANES * v:SUBLANES * (v + 1), :] for v in range(n_slabs)] for s in scores]
    idxs = [[sub + SUBLANES * v for v in range(n_slabs)] for _ in scores]
    for i, j in _oddeven_merge_sort_pairs(n_slabs):
        for va, ia in zip(vals, idxs):
            a, b = va[i], va[j]
            a_first = jnp.logical_or(a > b, jnp.logical_and(a == b, ia[i] < ia[j]))
            va[i], va[j] = jnp.maximum(a, b), jnp.minimum(a, b)
            ia[i], ia[j] = jnp.where(a_first, ia[i], ia[j]), jnp.where(a_first, ia[j], ia[i])
    for it in range(PEER_TOPK):
        for k, (va, ia) in enumerate(zip(vals, idxs)):
            m = jnp.max(va[0], axis=0, keepdims=True)
            pick = jnp.min(jnp.where(va[0] == m, ia[0], n_rows), axis=0, keepdims=True)
            vals_refs[k][it:it + 1, :] = m
            idx_refs[k][it:it + 1, :] = pick
            win = ia[0] == pick
            depth = PEER_TOPK - 1 - it
            for d in range(min(depth, n_slabs - 1)):
                va[d] = jnp.where(win, va[d + 1], va[d])
                ia[d] = jnp.where(win, ia[d + 1], ia[d])
            if depth >= n_slabs:
                va[n_slabs - 1] = jnp.where(win, -jnp.inf, va[n_slabs - 1])


def _peer_route_kernel(x_ref, g_ref, sh_ref, sc_ref, wqt_ref, sk_ref, h_ref, ids_ref, gts_ref,
                       q_s, val_s, idx_s, cand_s, cidx_s):
    tt = x_ref.shape[0]
    n_chunks = tt // LANES
    per_step = ROUTE_CHUNKS_PER_STEP
    half = PEER_D_KEY // 2
    h = _adaln(x_ref[...], g_ref[...], sh_ref[0], sc_ref[0])
    h_ref[...] = h
    qt = _dot_nt(wqt_ref[...], h.astype(BF16))
    for c in range(n_chunks):
        q_s[c] = qt[:, c * LANES:(c + 1) * LANES]
    for k in range(per_step):
        cand_s[k, PEER_CAND_ROWS - 8:PEER_CAND_ROWS, :] = jnp.full((8, LANES), -jnp.inf, F32)
        cidx_s[k, PEER_CAND_ROWS - 8:PEER_CAND_ROWS, :] = jnp.zeros((8, LANES), I32)
    sk1, sk2 = sk_ref[0], sk_ref[1]
    vals = [val_s.at[i] for i in range(2 * per_step)]
    idxs = [idx_s.at[i] for i in range(2 * per_step)]

    def body(step, carry):
        hh = step // (n_chunks // per_step)
        c0 = (step % (n_chunks // per_step)) * per_step
        q0 = pl.multiple_of(hh * PEER_D_KEY, PEER_D_KEY)
        scores = []
        for k in range(per_step):
            scores.append(_dot3(sk1, q_s[c0 + k, pl.ds(q0, half), :]))
            scores.append(_dot3(sk2, q_s[c0 + k, pl.ds(q0 + half, half), :]))
        _top16_rows(scores, PEER_N_KEYS, vals, idxs)
        cands, cidxs = [], []
        for k in range(per_step):
            v1, i1, v2, i2 = vals[2 * k], idxs[2 * k], vals[2 * k + 1], idxs[2 * k + 1]
            r = 0
            for a, nb in enumerate(PEER_CAND_COUNTS):
                cand_s[k, r:r + nb, :] = v1[a:a + 1, :] + v2[0:nb, :]
                cidx_s[k, r:r + nb, :] = i1[a:a + 1, :] * PEER_N_KEYS + i2[0:nb, :]
                r += nb
            cands.append(cand_s[k])
            cidxs.append(cidx_s[k])
        rows = _iota(cands[0].shape, 0)
        for it in range(PEER_TOPK):
            for k in range(per_step):
                m = jnp.max(cands[k], axis=0, keepdims=True)
                pick = jnp.min(jnp.where(cands[k] == m, rows, PEER_CAND_ROWS), axis=0, keepdims=True)
                hit = rows == pick
                idxs[2 * k][it:it + 1, :] = jnp.sum(jnp.where(hit, cidxs[k], 0), axis=0, keepdims=True)
                vals[2 * k][it:it + 1, :] = m
                cands[k] = jnp.where(hit, -jnp.inf, cands[k])
        r0 = pl.multiple_of(hh * PEER_TOPK, PEER_TOPK)
        for k in range(per_step):
            top = vals[2 * k][...]
            e = jnp.exp(top - top[0:1, :])
            gts_ref[c0 + k, pl.ds(r0, PEER_TOPK), :] = e / jnp.sum(e, axis=0, keepdims=True)
            ids_ref[c0 + k, pl.ds(r0, PEER_TOPK), :] = idxs[2 * k][...] * PACK_ROWS
        return carry

    lax.fori_loop(0, PEER_HEADS * n_chunks // per_step, body, 0)


def _peer_route(x2, g, shift, scale, wqt_bf, sub_keys, seq):
    t, d = x2.shape
    tt = 512
    per_b = seq // tt
    n_chunks = tt // LANES
    nq = wqt_bf.shape[0]
    out_blk = pl.BlockSpec((n_chunks, PEER_SLOTS, LANES), lambda i: (i, 0, 0))
    return pl.pallas_call(
        _peer_route_kernel,
        grid=(t // tt,),
        in_specs=[
            pl.BlockSpec((tt, d), lambda i: (i, 0)),
            pl.BlockSpec((1, d), lambda i: (0, 0)),
            pl.BlockSpec((1, 1, d), lambda i: (i // per_b, 0, 0)),
            pl.BlockSpec((1, 1, d), lambda i: (i // per_b, 0, 0)),
            pl.BlockSpec((nq, d), lambda i: (0, 0)),
            pl.BlockSpec(sub_keys.shape, lambda i: (0, 0, 0)),
        ],
        out_specs=[pl.BlockSpec((tt, d), lambda i: (i, 0)), out_blk, out_blk],
        out_shape=[jax.ShapeDtypeStruct((t, d), F32),
                   jax.ShapeDtypeStruct((t // LANES, PEER_SLOTS, LANES), I32),
                   jax.ShapeDtypeStruct((t // LANES, PEER_SLOTS, LANES), F32)],
        scratch_shapes=[pltpu.VMEM((n_chunks, nq, LANES), F32),
                        pltpu.VMEM((2 * ROUTE_CHUNKS_PER_STEP, PEER_TOPK, LANES), F32),
                        pltpu.VMEM((2 * ROUTE_CHUNKS_PER_STEP, PEER_TOPK, LANES), I32),
                        pltpu.VMEM((ROUTE_CHUNKS_PER_STEP, PEER_CAND_ROWS, LANES), F32),
                        pltpu.VMEM((ROUTE_CHUNKS_PER_STEP, PEER_CAND_ROWS, LANES), I32)],
        compiler_params=_cparams(1, V7X_VMEM_LIMIT),
        name="peer_route",
    )(x2, g, shift, scale, wqt_bf, sub_keys)


def _pack_table(tab):
    e, d = tab.shape
    bits = lax.bitcast_convert_type(tab.astype(BF16), jnp.uint16).astype(U32)
    bits = bits.reshape(e, d // (2 * LANES), 2, LANES)
    words = bits[:, :, 0, :] | (bits[:, :, 1, :] << 16)
    return words.reshape(e * (d // (2 * LANES)), LANES)


def _table_spec(rows):
    return pl.BlockSpec((rows, LANES), lambda i: (0, 0), pipeline_mode=pl.Buffered(1))


def _gelu_exact(a):
    return 0.5 * a * (1.0 + lax.erf(a * (2.0 ** -0.5)))


FEAT_CHUNKS = 8
SLOT_WIDTH = PEER_SLOTS * FEAT_CHUNKS


def _gather_rows(ids_ref, base, tbl_ref, slot):
    for j in range(PEER_SLOTS):
        if j % ID_VIEW == 0:
            ids_part = ids_ref.at[pl.ds(base + j, ID_VIEW)]
        row0 = pl.multiple_of(ids_part[j % ID_VIEW], PACK_ROWS)
        slot[PACK_ROWS * j:PACK_ROWS * (j + 1), :] = tbl_ref[pl.ds(row0, PACK_ROWS), :]


def _pipelined_tokens(tt, ids_ref, tbl_ref, slots, compute):
    group = len(slots) // 2
    first, second = slots[:group], slots[group:]
    last = tt - 1

    def gather(t, slot):
        _gather_rows(ids_ref, jnp.minimum(t, last) * PEER_SLOTS, tbl_ref, slot)

    for k in range(group):
        gather(k, first[k])

    def step(q, carry):
        t = 2 * group * q
        for k in range(group):
            compute(t + k, first[k])
        for k in range(group):
            gather(t + group + k, second[k])
        for k in range(group):
            compute(t + group + k, second[k])
        for k in range(group):
            gather(t + 2 * group + k, first[k])
        return carry

    lax.fori_loop(0, tt // (2 * group), step, 0)


def _chunk_diag():
    return (_iota((FEAT_CHUNKS, SLOT_WIDTH), 1) % FEAT_CHUNKS) == _iota((FEAT_CHUNKS, SLOT_WIDTH), 0)


def _peer_u_kernel(ids_ref, h_ref, g_ref, tbl_ref, coef_ref, *scratch):
    slots, (hx_s, rs_s) = scratch[:N_SLOTS], scratch[N_SLOTS:]
    tt = h_ref.shape[0]
    for c in range(FEAT_CHUNKS):
        hx_s[pl.ds(c, tt, stride=FEAT_CHUNKS), :] = h_ref[:, c * LANES:(c + 1) * LANES]
    diag = _chunk_diag()

    def compute(t, slot):
        rows = pltpu.bitcast(slot[...], BF16)
        x8 = hx_s[pl.ds(pl.multiple_of(t * FEAT_CHUNKS, FEAT_CHUNKS), FEAT_CHUNKS), :]
        part = _dot_nt(x8.astype(BF16), rows)
        rs_s[pl.ds(t, 1), :] = jnp.sum(jnp.where(diag, part, 0.0), axis=0, keepdims=True)

    _pipelined_tokens(tt, ids_ref, tbl_ref, slots, compute)
    group = jnp.where(_iota((SLOT_WIDTH, PEER_SLOTS), 0) // FEAT_CHUNKS == _iota((SLOT_WIDTH, PEER_SLOTS), 1),
                      1.0, 0.0).astype(BF16)
    hi, lo = _split_bf16(rs_s[...])
    act = _dot(hi, group) + _dot(lo, group)
    coef_ref[...] = g_ref[...] * _gelu_exact(act)


N_SLOTS = 16
ID_VIEW = 16
PEER_TOKEN_TILE = 512


def _slot_scratch():
    return [pltpu.VMEM((PEER_SLOTS * PACK_ROWS, LANES), U32)] * N_SLOTS


def _peer_u(ids_flat, h, gates, table, tt):
    t, d = h.shape
    return pl.pallas_call(
        _peer_u_kernel,
        grid=(t // tt,),
        in_specs=[
            pl.BlockSpec((tt * PEER_SLOTS,), lambda i: (i,), memory_space=pltpu.SMEM),
            pl.BlockSpec((tt, d), lambda i: (i, 0)),
            pl.BlockSpec((tt, PEER_SLOTS), lambda i: (i, 0)),
            _table_spec(table.shape[0]),
        ],
        out_specs=pl.BlockSpec((tt, PEER_SLOTS), lambda i: (i, 0)),
        out_shape=jax.ShapeDtypeStruct((t, PEER_SLOTS), F32),
        scratch_shapes=_slot_scratch() + [
                        pltpu.VMEM((tt * FEAT_CHUNKS, LANES), F32),
                        pltpu.VMEM((tt, SLOT_WIDTH), F32)],
        compiler_params=_cparams(1, V7X_VMEM_LIMIT),
        name="peer_expert_in",
    )(ids_flat, h, gates, table)


def _peer_v_kernel(ids_ref, coef_ref, x_ref, gate_ref, tbl_ref, o_ref, *scratch):
    slots, (ce_hi_s, ce_lo_s, res_s) = scratch[:N_SLOTS], scratch[N_SLOTS:]
    tt = x_ref.shape[0]
    spread = jnp.where(_iota((PEER_SLOTS, SLOT_WIDTH), 1) // FEAT_CHUNKS == _iota((PEER_SLOTS, SLOT_WIDTH), 0),
                       1.0, 0.0).astype(BF16)
    hi, lo = _split_bf16(coef_ref[...])
    ce_hi_s[...] = _dot(hi, spread)
    ce_lo_s[...] = _dot(lo, spread)
    diag = _chunk_diag()

    def compute(t, slot):
        rows = pltpu.bitcast(slot[...], BF16)
        a_hi = jnp.where(diag, ce_hi_s[pl.ds(t, 1), :], 0.0)
        a_lo = jnp.where(diag, ce_lo_s[pl.ds(t, 1), :], 0.0)
        both = _dot(jnp.concatenate([a_hi, a_lo], axis=0).astype(BF16), rows)
        r0 = pl.multiple_of(t * FEAT_CHUNKS, FEAT_CHUNKS)
        res_s[pl.ds(r0, FEAT_CHUNKS), :] = both[0:FEAT_CHUNKS, :] + both[FEAT_CHUNKS:2 * FEAT_CHUNKS, :]

    _pipelined_tokens(tt, ids_ref, tbl_ref, slots, compute)
    for c in range(FEAT_CHUNKS):
        cols = slice(c * LANES, (c + 1) * LANES)
        y = res_s[pl.ds(c, tt, stride=FEAT_CHUNKS), :]
        o_ref[:, cols] = x_ref[:, cols] + gate_ref[0][:, cols] * y


def _peer_v(ids_flat, coef, x2, gate, table, seq, tt):
    t, d = x2.shape
    per_b = seq // tt
    blk = pl.BlockSpec((tt, d), lambda i: (i, 0))
    return pl.pallas_call(
        _peer_v_kernel,
        grid=(t // tt,),
        in_specs=[
            pl.BlockSpec((tt * PEER_SLOTS,), lambda i: (i,), memory_space=pltpu.SMEM),
            pl.BlockSpec((tt, PEER_SLOTS), lambda i: (i, 0)),
            blk,
            pl.BlockSpec((1, 1, d), lambda i: (i // per_b, 0, 0)),
            _table_spec(table.shape[0]),
        ],
        out_specs=blk,
        out_shape=jax.ShapeDtypeStruct((t, d), F32),
        scratch_shapes=_slot_scratch() + [
                        pltpu.VMEM((tt, SLOT_WIDTH), F32), pltpu.VMEM((tt, SLOT_WIDTH), F32),
                        pltpu.VMEM((tt * FEAT_CHUNKS, LANES), F32)],
        compiler_params=_cparams(1, V7X_VMEM_LIMIT),
        name="peer_expert_out",
    )(ids_flat, coef, x2, gate, table)


def _peer_ffn(x2, g, shift, scale, gate, wq, sub_keys, table_u, table_v, seq):
    t, d = x2.shape
    tt = PEER_TOKEN_TILE
    h, ids, gts = _peer_route(x2, g, shift, scale, wq.T.astype(BF16), sub_keys, seq)
    ids_flat = ids.transpose(0, 2, 1).reshape(t * PEER_SLOTS)
    gates = gts.transpose(0, 2, 1).reshape(t, PEER_SLOTS)
    coef = _peer_u(ids_flat, h, gates, table_u, tt)
    return _peer_v(ids_flat, coef, x2, gate, table_v, seq, tt)


def _rope_tables(seq):
    half = HEAD_DIM // 2
    inv_freq = ROPE_THETA ** (-jnp.arange(half, dtype=F32) / half)
    ang = jnp.arange(seq).astype(F32)[:, None] * inv_freq[None, :]
    reps = LANES // half
    return jnp.tile(jnp.cos(ang), (1, reps)), jnp.tile(jnp.sin(ang), (1, reps))


def _two_heads(gain):
    return jnp.tile(gain.reshape(1, HEAD_DIM), (1, LANES // HEAD_DIM))


def kernel(x, c, ada_w, ada_b, norm_mix_g, norm_ffn_g, w_in_ab, w_out_ab, sinks_a, qnorm_a, knorm_a,
           w_in_cd, w_out_cd, qnorm_c, knorm_c, qnorm_d, knorm_d, peer_wq, peer_subkeys, peer_u, peer_v):
    b, seq, d = x.shape
    depth = ada_w.shape[0]
    t = b * seq
    cos, sin = _rope_tables(seq)
    mod = _modulation(c, ada_w, ada_b)
    x2 = x.reshape(t, d)
    for layer in range(depth):
        shift_m, scale_m, gate_m, shift_f, scale_f, gate_f = [
            m.reshape(b, 1, d) for m in jnp.split(mod[layer], 6, axis=-1)]
        g_mix = norm_mix_g[layer].reshape(1, d)
        i = layer // 2
        if layer % 2 == 0:
            proj = _norm_proj(x2, g_mix, shift_m, scale_m, w_in_ab[i].astype(BF16), seq)
            proj = proj.reshape(b, seq, -1)
            ya = _swa_attention(proj, sinks_a[i], cos, sin, _two_heads(qnorm_a[i]), _two_heads(knorm_a[i]))
            b_col = (A_Q_HEADS + 2 * A_KV_HEADS) * HEAD_DIM // LANES
            yb = _stick_attention(proj, b_col)
            w_out = w_out_ab[i]
        else:
            proj = _norm_proj(x2, g_mix, shift_m, scale_m, w_in_cd[i].astype(BF16), seq)
            proj = proj.reshape(b, seq, -1)
            ya = _qkv_attention_call(_dilated_kernel, "dilated_attention", proj, 0, C_HEADS, cos, sin,
                                     _two_heads(qnorm_c[i]), _two_heads(knorm_c[i]))
            d_col = 3 * C_HEADS * HEAD_DIM // LANES
            yb = _qkv_attention_call(_moba_kernel, "moba_attention", proj, d_col, D_HEADS, cos, sin,
                                     _two_heads(qnorm_d[i]), _two_heads(knorm_d[i]),
                                     extra_scratch=(pltpu.VMEM((LANES, LANES), F32),
                                                    pltpu.VMEM((seq, LANES), F32),
                                                    pltpu.VMEM((seq, LANES), F32)))
            w_out = w_out_cd[i]
        x2 = _out_proj(x2, ya.reshape(t, -1), yb.reshape(t, -1), w_out.astype(BF16), gate_m, seq)
        x2 = _peer_ffn(x2, norm_ffn_g[layer].reshape(1, d), shift_f, scale_f, gate_f,
                       peer_wq[layer], peer_subkeys[layer],
                       _pack_table(peer_u[layer]), _pack_table(peer_v[layer]), seq)
    return x2.reshape(b, seq, d)
```

```python
import functools

import jax
import jax.numpy as jnp
from jax import lax
from jax.experimental import pallas as pl
from jax.experimental.pallas import tpu as pltpu

F32 = jnp.float32
BF16 = jnp.bfloat16
I32 = jnp.int32
U32 = jnp.uint32

HEAD_DIM = 64
ROPE_THETA = 10000.0
NORM_EPS = 1e-6
LANES = 128
QUERY_BLOCK = 128
A_Q_HEADS, A_KV_HEADS = 8, 2
B_HEADS = C_HEADS = D_HEADS = 8
C_PATTERNS = ((128, 1), (512, 4), (2048, 16))
MOBA_BLOCK, MOBA_TOPK = 256, 3
PEER_HEADS, PEER_N_KEYS, PEER_TOPK, PEER_D_KEY = 8, 128, 16, 256
PEER_SLOTS = PEER_HEADS * PEER_TOPK
NEG_BIG = -1e30
PEER_CAND_COUNTS = tuple(PEER_TOPK // (a + 1) for a in range(PEER_TOPK))
PEER_N_CAND = sum(PEER_CAND_COUNTS)
PEER_CAND_ROWS = 56
PACK_ROWS = 4
V7X_VMEM_LIMIT = 56 * 1024 * 1024


def _cparams(n_axes, vmem=None):
    return pltpu.CompilerParams(
        dimension_semantics=("arbitrary",) * n_axes,
        vmem_limit_bytes=vmem)


def _split_bf16(a):
    hi = a.astype(BF16)
    lo = (a - hi.astype(F32)).astype(BF16)
    return hi, lo


def _dot(a, b):
    return jnp.dot(a, b, preferred_element_type=F32)


def _dot_nt(a, b):
    return lax.dot_general(a, b, (((1,), (1,)), ((), ())), preferred_element_type=F32)


def _dot3(a, b):
    ah, al = _split_bf16(a)
    bh, bl = _split_bf16(b)
    return _dot(ah, bh) + _dot(ah, bl) + _dot(al, bh)


def _dot3_nt(a, b):
    ah, al = _split_bf16(a)
    bh, bl = _split_bf16(b)
    return _dot_nt(ah, bh) + _dot_nt(ah, bl) + _dot_nt(al, bh)


def _iota(shape, dim):
    return lax.broadcasted_iota(I32, shape, dim)


def _mod_kernel(c_ref, w_ref, b_ref, o_ref):
    c = c_ref[...]
    cond = c * jax.nn.sigmoid(c)
    o_ref[0] = _dot3(cond, w_ref[0]) + b_ref[0]


def _modulation(c, ada_w, ada_b):
    depth, d, n = ada_w.shape
    b = c.shape[0]
    tn = 1024
    return pl.pallas_call(
        _mod_kernel,
        grid=(depth, n // tn),
        in_specs=[
            pl.BlockSpec((b, d), lambda l, j: (0, 0)),
            pl.BlockSpec((1, d, tn), lambda l, j: (l, 0, j)),
            pl.BlockSpec((1, 1, tn), lambda l, j: (l, 0, j)),
        ],
        out_specs=pl.BlockSpec((1, b, tn), lambda l, j: (l, 0, j)),
        out_shape=jax.ShapeDtypeStruct((depth, b, n), F32),
        compiler_params=_cparams(2, V7X_VMEM_LIMIT),
        name="adaln_modulation",
    )(c, ada_w, ada_b.reshape(depth, 1, n))


def _adaln(x, g, shift, scale):
    ms = jnp.mean(x * x, axis=-1, keepdims=True)
    y = x * lax.rsqrt(ms + NORM_EPS) * g
    return y * (1.0 + scale) + shift


def _norm_proj_kernel(x_ref, g_ref, sh_ref, sc_ref, w_ref, o_ref):
    h = _adaln(x_ref[...], g_ref[...], sh_ref[0], sc_ref[0])
    o_ref[...] = _dot(h.astype(BF16), w_ref[...])


def _norm_proj(x2, g, shift, scale, w_bf, seq):
    t, d = x2.shape
    n = w_bf.shape[1]
    tt = 512
    per_b = seq // tt
    return pl.pallas_call(
        _norm_proj_kernel,
        grid=(t // tt,),
        in_specs=[
            pl.BlockSpec((tt, d), lambda i: (i, 0)),
            pl.BlockSpec((1, d), lambda i: (0, 0)),
            pl.BlockSpec((1, 1, d), lambda i: (i // per_b, 0, 0)),
            pl.BlockSpec((1, 1, d), lambda i: (i // per_b, 0, 0)),
            pl.BlockSpec((d, n), lambda i: (0, 0)),
        ],
        out_specs=pl.BlockSpec((tt, n), lambda i: (i, 0)),
        out_shape=jax.ShapeDtypeStruct((t, n), F32),
        compiler_params=_cparams(1, V7X_VMEM_LIMIT),
        name="adaln_in_proj",
    )(x2, g, shift, scale, w_bf)


def _out_proj_kernel(x_ref, ya_ref, yb_ref, w_ref, gate_ref, o_ref):
    half = ya_ref.shape[1]
    y = _dot(ya_ref[...].astype(BF16), w_ref[0:half, :])
    y = y + _dot(yb_ref[...].astype(BF16), w_ref[half:2 * half, :])
    o_ref[...] = x_ref[...] + gate_ref[0] * y


def _out_proj(x2, ya, yb, w_bf, gate, seq):
    t, d = x2.shape
    half = ya.shape[1]
    tt = 512
    per_b = seq // tt
    return pl.pallas_call(
        _out_proj_kernel,
        grid=(t // tt,),
        in_specs=[
            pl.BlockSpec((tt, d), lambda i: (i, 0)),
            pl.BlockSpec((tt, half), lambda i: (i, 0)),
            pl.BlockSpec((tt, half), lambda i: (i, 0)),
            pl.BlockSpec((2 * half, d), lambda i: (0, 0)),
            pl.BlockSpec((1, 1, d), lambda i: (i // per_b, 0, 0)),
        ],
        out_specs=pl.BlockSpec((tt, d), lambda i: (i, 0)),
        out_shape=jax.ShapeDtypeStruct((t, d), F32),
        compiler_params=_cparams(1, V7X_VMEM_LIMIT),
        name="mixer_out_proj",
    )(x2, ya, yb, w_bf, gate)


def _lane_row():
    return _iota((1, LANES), 1)


def _head_segment_ones():
    r = _iota((LANES, LANES), 0) // HEAD_DIM
    c = _iota((LANES, LANES), 1) // HEAD_DIM
    return jnp.where(r == c, 1.0, 0.0).astype(BF16)


def _headnorm_rope(a, g, cos, sin):
    hi, lo = _split_bf16(a * a)
    seg = _head_segment_ones()
    ms = (_dot(hi, seg) + _dot(lo, seg)) * (1.0 / HEAD_DIM)
    y = a * lax.rsqrt(ms + NORM_EPS) * g
    half = HEAD_DIM // 2
    upper = pltpu.roll(y, LANES - half, axis=1)
    lower = pltpu.roll(y, half, axis=1)
    first_half = (_lane_row() % HEAD_DIM) < half
    rot = jnp.where(first_half, -upper, lower)
    return y * cos + rot * sin


def _head_masks():
    lane = _lane_row()
    return lane < HEAD_DIM, lane >= HEAD_DIM


def _attn_specs(seq, n_pairs, qcol, kcol, vcol, kv_shared):
    blk = (1, seq, LANES)
    q_spec = pl.BlockSpec(blk, lambda b, p: (b, 0, qcol + p))
    if kv_shared:
        k_spec = pl.BlockSpec(blk, lambda b, p: (b, 0, kcol))
        v_spec = pl.BlockSpec(blk, lambda b, p: (b, 0, vcol))
    else:
        k_spec = pl.BlockSpec(blk, lambda b, p: (b, 0, kcol + p))
        v_spec = pl.BlockSpec(blk, lambda b, p: (b, 0, vcol + p))
    return q_spec, k_spec, v_spec


def _row_spec(seq):
    return pl.BlockSpec((seq, LANES), lambda b, p: (0, 0))


def _gain_spec():
    return pl.BlockSpec((1, LANES), lambda b, p: (0, 0))


def _store_heads(o_ref, r0, outs):
    first, _ = _head_masks()
    o_ref[0, pl.ds(r0, QUERY_BLOCK), :] = jnp.where(first, outs[0], outs[1])


def _swa_kernel(sinks_ref, q_ref, k_ref, v_ref, cos_ref, sin_ref, gq_ref, gk_ref, o_ref,
                q0_s, q1_s, k_s, v_s):
    p = pl.program_id(1)
    seq = q_ref.shape[1]
    cos, sin = cos_ref[...], sin_ref[...]
    first, second = _head_masks()
    qn = _headnorm_rope(q_ref[0], gq_ref[...], cos, sin) * (HEAD_DIM ** -0.5)
    q0_s[...] = jnp.where(first, qn, 0.0).astype(BF16)
    q1_s[...] = jnp.where(second, qn, 0.0).astype(BF16)
    pairs_per_kv = (A_Q_HEADS // A_KV_HEADS) // 2
    keep = jnp.logical_xor(first, (p // pairs_per_kv) == 1)
    kn = _headnorm_rope(k_ref[0], gk_ref[...], cos, sin)
    k_s[...] = jnp.where(keep, kn, pltpu.roll(kn, HEAD_DIM, axis=1)).astype(BF16)
    v = v_ref[0]
    v_s[...] = jnp.where(keep, v, pltpu.roll(v, HEAD_DIM, axis=1)).astype(BF16)

    qi = _iota((QUERY_BLOCK, QUERY_BLOCK), 0)
    ki = _iota((QUERY_BLOCK, QUERY_BLOCK), 1)

    def qblock(i, carry):
        r0 = pl.multiple_of(i * QUERY_BLOCK, QUERY_BLOCK)
        rp = pl.multiple_of(jnp.maximum(i - 1, 0) * QUERY_BLOCK, QUERY_BLOCK)
        kc, kp = k_s[pl.ds(r0, QUERY_BLOCK), :], k_s[pl.ds(rp, QUERY_BLOCK), :]
        vc, vp = v_s[pl.ds(r0, QUERY_BLOCK), :], v_s[pl.ds(rp, QUERY_BLOCK), :]
        mask_c = ki <= qi
        mask_p = jnp.logical_and(ki > qi, i > 0)
        qhs = [q_s[pl.ds(r0, QUERY_BLOCK), :] for q_s in (q0_s, q1_s)]
        scores = [(_dot_nt(qh, kc), _dot_nt(qh, kp)) for qh in qhs]
        probs, denoms = [], []
        for hh in range(2):
            sc = jnp.where(mask_c, scores[hh][0], NEG_BIG)
            sp = jnp.where(mask_p, scores[hh][1], NEG_BIG)
            sink = sinks_ref[2 * p + hh]
            m = jnp.maximum(jnp.max(sc, axis=1, keepdims=True), jnp.max(sp, axis=1, keepdims=True))
            m = jnp.maximum(m, sink)
            ec, ep = jnp.exp(sc - m), jnp.exp(sp - m)
            denoms.append(jnp.sum(ec, axis=1, keepdims=True) + jnp.sum(ep, axis=1, keepdims=True)
                          + jnp.exp(sink - m))
            probs.append((ec.astype(BF16), ep.astype(BF16)))
        outs = [(_dot(probs[hh][0], vc) + _dot(probs[hh][1], vp)) / denoms[hh] for hh in range(2)]
        _store_heads(o_ref, r0, outs)
        return carry

    lax.fori_loop(0, seq // QUERY_BLOCK, qblock, 0)


def _swa_attention(proj, sinks, cos, sin, gq, gk):
    b, seq, _ = proj.shape
    n_pairs = A_Q_HEADS // 2
    kcol = A_Q_HEADS * HEAD_DIM // LANES
    vcol = kcol + A_KV_HEADS * HEAD_DIM // LANES
    q_spec, k_spec, v_spec = _attn_specs(seq, n_pairs, 0, kcol, vcol, True)
    return pl.pallas_call(
        _swa_kernel,
        grid=(b, n_pairs),
        in_specs=[pl.BlockSpec(memory_space=pltpu.SMEM), q_spec, k_spec, v_spec,
                  _row_spec(seq), _row_spec(seq), _gain_spec(), _gain_spec()],
        out_specs=pl.BlockSpec((1, seq, LANES), lambda b_, p: (b_, 0, p)),
        out_shape=jax.ShapeDtypeStruct((b, seq, n_pairs * LANES), F32),
        scratch_shapes=[pltpu.VMEM((seq, LANES), BF16)] * 4,
        compiler_params=_cparams(2, V7X_VMEM_LIMIT),
        name="swa_gqa_attention",
    )(sinks, proj, proj, proj, cos, sin, gq, gk)


STICK_GROUP = 4
STICK_QUERY_ROWS = 256


def _stick_kernel(q_ref, k_ref, v_ref, o_ref, k_s, v_s):
    seq = q_ref.shape[1]
    k_s[...] = k_ref[0].astype(BF16)
    v_s[...] = v_ref[0].astype(BF16)
    first, second = _head_masks()
    qr = min(STICK_QUERY_ROWS, seq)
    kw = QUERY_BLOCK
    qi = _iota((qr, kw), 0)
    ki = _iota((qr, kw), 1)
    wr = _iota((2 * kw, 2 * kw), 0) % kw
    wc = _iota((2 * kw, 2 * kw), 1)
    suffix_w = jnp.where(jnp.logical_or(wc >= kw, wr > wc), 1.0, 0.0).astype(BF16)

    def qblock(i, carry):
        r0 = pl.multiple_of(i * qr, qr)
        q = q_ref[0, pl.ds(r0, qr), :] * (HEAD_DIM ** -0.5)
        qhs = [jnp.where(msk, q, 0.0).astype(BF16) for msk in (first, second)]
        n_blocks = (r0 + qr) // kw

        def kgroup(g, st):
            accs, laters = [st[0], st[1]], [st[2], st[3]]
            chains = [(u, hh) for u in range(STICK_GROUP) for hh in range(2)]
            vbs, pasts, zs = [], [], {}
            for u in range(STICK_GROUP):
                j = n_blocks - 1 - (g * STICK_GROUP + u)
                live = j >= 0
                c0 = pl.multiple_of(jnp.maximum(j, 0) * kw, kw)
                kb = k_s[pl.ds(c0, kw), :]
                vbs.append(v_s[pl.ds(c0, kw), :])
                pasts.append(jnp.logical_and((c0 + ki) < (r0 + qi), live))
                for hh in range(2):
                    zs[u, hh] = _dot_nt(qhs[hh], kb)
            logit, sums = {}, {}
            for u, hh in chains:
                z = zs[u, hh]
                sp = jnp.maximum(z, 0.0) + jnp.log(1.0 + jnp.exp(-jnp.abs(z)))
                log_keep = jnp.where(pasts[u], -sp, 0.0)
                logit[u, hh] = z - sp
                hi, lo = _split_bf16(log_keep)
                sums[u, hh] = _dot(jnp.concatenate([hi, lo], axis=1), suffix_w)
            ws = {}
            for u, hh in chains:
                inner, total = sums[u, hh][:, :kw], sums[u, hh][:, kw:]
                ws[u, hh] = jnp.where(pasts[u], jnp.exp(logit[u, hh] + inner + laters[hh]), 0.0).astype(BF16)
                laters[hh] = laters[hh] + total
            for u, hh in chains:
                accs[hh] = accs[hh] + _dot(ws[u, hh], vbs[u])
            return accs[0], accs[1], laters[0], laters[1]

        zero = jnp.zeros((qr, LANES), F32)
        n_groups = (n_blocks + STICK_GROUP - 1) // STICK_GROUP
        st = lax.fori_loop(0, n_groups, kgroup, (zero, zero, zero, zero))
        o_ref[0, pl.ds(r0, qr), :] = jnp.where(first, st[0], st[1])
        return carry

    lax.fori_loop(0, seq // qr, qblock, 0)


def _stick_attention(proj, qcol):
    b, seq, _ = proj.shape
    n_pairs = B_HEADS // 2
    q_spec, k_spec, v_spec = _attn_specs(seq, n_pairs, qcol, qcol + n_pairs, qcol + 2 * n_pairs, False)
    return pl.pallas_call(
        _stick_kernel,
        grid=(b, n_pairs),
        in_specs=[q_spec, k_spec, v_spec],
        out_specs=pl.BlockSpec((1, seq, LANES), lambda b_, p: (b_, 0, p)),
        out_shape=jax.ShapeDtypeStruct((b, seq, n_pairs * LANES), F32),
        scratch_shapes=[pltpu.VMEM((seq, LANES), BF16)] * 2,
        compiler_params=_cparams(2, V7X_VMEM_LIMIT),
        name="stick_breaking_attention",
    )(proj, proj, proj)


def _prep_qkv(q_ref, k_ref, v_ref, cos_ref, sin_ref, gq_ref, gk_ref, q0_s, q1_s, k_s, v_s):
    cos, sin = cos_ref[...], sin_ref[...]
    first, second = _head_masks()
    qn = _headnorm_rope(q_ref[0], gq_ref[...], cos, sin) * (HEAD_DIM ** -0.5)
    q0_s[...] = jnp.where(first, qn, 0.0).astype(BF16)
    q1_s[...] = jnp.where(second, qn, 0.0).astype(BF16)
    kn = _headnorm_rope(k_ref[0], gk_ref[...], cos, sin)
    k_s[...] = kn.astype(BF16)
    v_s[...] = v_ref[0].astype(BF16)
    return qn, kn


DILATED_KEY_TILE = 512
DILATED_QUERY_ROWS = 256


def _dilated_kernel(q_ref, k_ref, v_ref, cos_ref, sin_ref, gq_ref, gk_ref, o_ref,
                    q0_s, q1_s, k_s, v_s):
    seq = q_ref.shape[1]
    _prep_qkv(q_ref, k_ref, v_ref, cos_ref, sin_ref, gq_ref, gk_ref, q0_s, q1_s, k_s, v_s)
    kt = min(DILATED_KEY_TILE, seq)
    qr = min(DILATED_QUERY_ROWS, seq)
    qk = _iota((qr, kt), 0) - _iota((qr, kt), 1)
    on_stride = [jnp.where((qk & (dil - 1)) == 0, 1.0, 0.0) for _, dil in C_PATTERNS]
    first_head, _ = _head_masks()

    def qblock(i, carry):
        r0 = pl.multiple_of(i * qr, qr)
        qhs = [q_s[pl.ds(r0, qr), :] for q_s in (q0_s, q1_s)]

        def ktile(g, st):
            c0 = pl.multiple_of(g * kt, kt)
            d = (r0 - c0) + qk
            count = jnp.zeros(d.shape, F32)
            for (window, _), stride_ok in zip(C_PATTERNS, on_stride):
                count = count + jnp.where(d <= window, stride_ok, 0.0)
            count = jnp.where(d >= 0, count, 0.0)
            kb, vb = k_s[pl.ds(c0, kt), :], v_s[pl.ds(c0, kt), :]
            scores = [_dot_nt(qhs[hh], kb) for hh in range(2)]
            new, prs = [], []
            for hh in range(2):
                m, l, acc = st[3 * hh:3 * hh + 3]
                s = jnp.where(count > 0.0, scores[hh], NEG_BIG)
                m_new = jnp.maximum(m, jnp.max(s, axis=1, keepdims=True))
                pr = count * jnp.exp(s - m_new)
                alpha = jnp.exp(m - m_new)
                new += [m_new, alpha * l + jnp.sum(pr, axis=1, keepdims=True), alpha * acc]
                prs.append(pr.astype(BF16))
            for hh in range(2):
                new[3 * hh + 2] = new[3 * hh + 2] + _dot(prs[hh], vb)
            return tuple(new)

        init = (jnp.full((qr, 1), NEG_BIG, F32), jnp.zeros((qr, 1), F32),
                jnp.zeros((qr, LANES), F32)) * 2
        st = lax.fori_loop(0, (r0 + qr + kt - 1) // kt, ktile, init)
        o_ref[0, pl.ds(r0, qr), :] = jnp.where(first_head, st[2] / st[1], st[5] / st[4])
        return carry

    lax.fori_loop(0, seq // qr, qblock, 0)


def _qkv_attention_call(kernel, name, proj, qcol, n_heads, cos, sin, gq, gk, extra_scratch=()):
    b, seq, _ = proj.shape
    n_pairs = n_heads // 2
    q_spec, k_spec, v_spec = _attn_specs(seq, n_pairs, qcol, qcol + n_pairs, qcol + 2 * n_pairs, False)
    return pl.pallas_call(
        kernel,
        grid=(b, n_pairs),
        in_specs=[q_spec, k_spec, v_spec, _row_spec(seq), _row_spec(seq), _gain_spec(), _gain_spec()],
        out_specs=pl.BlockSpec((1, seq, LANES), lambda b_, p: (b_, 0, p)),
        out_shape=jax.ShapeDtypeStruct((b, seq, n_pairs * LANES), F32),
        scratch_shapes=[pltpu.VMEM((seq, LANES), BF16)] * 4 + list(extra_scratch),
        compiler_params=_cparams(2, V7X_VMEM_LIMIT),
        name=name,
    )(proj, proj, proj, cos, sin, gq, gk)


def _moba_kernel(q_ref, k_ref, v_ref, cos_ref, sin_ref, gq_ref, gk_ref, o_ref,
                 q0_s, q1_s, k_s, v_s, km_s, sel0_s, sel1_s):
    seq = q_ref.shape[1]
    n_blocks = seq // MOBA_BLOCK
    qn, kn = _prep_qkv(q_ref, k_ref, v_ref, cos_ref, sin_ref, gq_ref, gk_ref, q0_s, q1_s, k_s, v_s)
    km_s[...] = jnp.zeros(km_s.shape, F32)
    km_s[0:n_blocks, :] = jnp.mean(kn.reshape(n_blocks, MOBA_BLOCK, LANES), axis=1)
    first, second = _head_masks()

    rows8 = _iota((8, seq), 0)
    own8 = _iota((8, seq), 1) // MOBA_BLOCK
    valid = rows8 < own8
    for msk, sel_s in ((first, sel0_s), (second, sel1_s)):
        gate = _dot3_nt(km_s[...], jnp.where(msk, qn, 0.0))[0:8, :]
        gm = jnp.where(valid, gate, -jnp.inf)
        rank = jnp.zeros((8, seq), F32)
        for n2 in range(n_blocks):
            g2 = gm[n2:n2 + 1, :]
            beats = jnp.logical_or(g2 > gm, jnp.logical_and(g2 == gm, n2 < rows8))
            rank = rank + jnp.where(jnp.logical_and(beats, n2 < own8), 1.0, 0.0)
        sel = jnp.where(jnp.logical_and(valid, rank < float(MOBA_TOPK)), 1.0, 0.0)
        sel = jnp.concatenate([sel, jnp.zeros((LANES - 8, seq), F32)], axis=0)
        sel_s[...] = sel.T

    qrows = MOBA_BLOCK
    lane_sq = _iota((qrows, LANES), 1)
    causal = _iota((qrows, MOBA_BLOCK), 1) <= _iota((qrows, MOBA_BLOCK), 0)
    second_block = _iota((qrows, 2 * MOBA_BLOCK), 1) >= MOBA_BLOCK
    first_head, _ = _head_masks()

    def qblock(own, carry):
        r0 = pl.multiple_of(own * qrows, qrows)
        qhs = [q_s[pl.ds(r0, qrows), :] for q_s in (q0_s, q1_s)]
        sels = [sel_s[pl.ds(r0, qrows), :] for sel_s in (sel0_s, sel1_s)]
        kb, vb = k_s[pl.ds(r0, MOBA_BLOCK), :], v_s[pl.ds(r0, MOBA_BLOCK), :]
        scores = [_dot_nt(qhs[hh], kb) for hh in range(2)]
        init, prs = [], []
        for hh in range(2):
            s = jnp.where(causal, scores[hh], NEG_BIG)
            m = jnp.max(s, axis=1, keepdims=True)
            pr = jnp.exp(s - m)
            init += [m, jnp.sum(pr, axis=1, keepdims=True), None]
            prs.append(pr.astype(BF16))
        for hh in range(2):
            init[3 * hh + 2] = _dot(prs[hh], vb)

        def kpair(g, st):
            c0 = pl.multiple_of(g * 2 * MOBA_BLOCK, 2 * MOBA_BLOCK)
            kb2, vb2 = k_s[pl.ds(c0, 2 * MOBA_BLOCK), :], v_s[pl.ds(c0, 2 * MOBA_BLOCK), :]
            scores = [_dot_nt(qhs[hh], kb2) for hh in range(2)]
            new, prs = [], []
            for hh in range(2):
                m, l, acc = st[3 * hh:3 * hh + 3]
                sel_a = jnp.sum(jnp.where(lane_sq == 2 * g, sels[hh], 0.0), axis=1, keepdims=True)
                sel_b = jnp.sum(jnp.where(lane_sq == 2 * g + 1, sels[hh], 0.0), axis=1, keepdims=True)
                keep = jnp.where(second_block, sel_b, sel_a) > 0.0
                s = jnp.where(keep, scores[hh], NEG_BIG)
                m_new = jnp.maximum(m, jnp.max(s, axis=1, keepdims=True))
                pr = jnp.exp(s - m_new)
                alpha = jnp.exp(m - m_new)
                new += [m_new, alpha * l + jnp.sum(pr, axis=1, keepdims=True), alpha * acc]
                prs.append(pr.astype(BF16))
            for hh in range(2):
                new[3 * hh + 2] = new[3 * hh + 2] + _dot(prs[hh], vb2)
            return tuple(new)

        st = lax.fori_loop(0, (own + 1) // 2, kpair, tuple(init))
        o_ref[0, pl.ds(r0, qrows), :] = jnp.where(first_head, st[2] / st[1], st[5] / st[4])
        return carry

    lax.fori_loop(0, seq // qrows, qblock, 0)


ROUTE_CHUNKS_PER_STEP = 4


def _oddeven_merge_sort_pairs(n):
    pairs = []

    def merge(lo, hi, r):
        step = 2 * r
        if step < hi - lo:
            merge(lo, hi, step)
            merge(lo + r, hi, step)
            pairs.extend((i, i + r) for i in range(lo + r, hi - r, step))
        else:
            pairs.append((lo, lo + r))

    def sort(lo, hi):
        if hi - lo >= 1:
            mid = lo + (hi - lo) // 2
            sort(lo, mid)
            sort(mid + 1, hi)
            merge(lo, hi, 1)

    sort(0, n - 1)
    return pairs


SUBLANES = 8


def _top16_rows(scores, n_rows, vals_refs, idx_refs):
    n_slabs = n_rows // SUBLANES
    sub = _iota((SUBLANES, LANES), 0)
    vals = [[s[SUBLANES * v:SUBLANES * (v + 1), :] for v in range(n_slabs)] for s in scores]
    idxs = [[sub + SUBLANES * v for v in range(n_slabs)] for _ in scores]
    for i, j in _oddeven_merge_sort_pairs(n_slabs):
        for va, ia in zip(vals, idxs):
            a, b = va[i], va[j]
            a_first = jnp.logical_or(a > b, jnp.logical_and(a == b, ia[i] < ia[j]))
            va[i], va[j] = jnp.maximum(a, b), jnp.minimum(a, b)
            ia[i], ia[j] = jnp.where(a_first, ia[i], ia[j]), jnp.where(a_first, ia[j], ia[i])
    for it in range(PEER_TOPK):
        for k, (va, ia) in enumerate(zip(vals, idxs)):
            m = jnp.max(va[0], axis=0, keepdims=True)
            pick = jnp.min(jnp.where(va[0] == m, ia[0], n_rows), axis=0, keepdims=True)
            vals_refs[k][it:it + 1, :] = m
            idx_refs[k][it:it + 1, :] = pick
            win = ia[0] == pick
            depth = PEER_TOPK - 1 - it
            for d in range(min(depth, n_slabs - 1)):
                va[d] = jnp.where(win, va[d + 1], va[d])
                ia[d] = jnp.where(win, ia[d + 1], ia[d])
            if depth >= n_slabs:
                va[n_slabs - 1] = jnp.where(win, -jnp.inf, va[n_slabs - 1])


def _peer_route_kernel(x_ref, g_ref, sh_ref, sc_ref, wqt_ref, sk_ref, h_ref, ids_ref, gts_ref,
                       q_s, val_s, idx_s, ids_s, gts_s):
    tt = x_ref.shape[0]
    n_chunks = tt // LANES
    per_step = ROUTE_CHUNKS_PER_STEP
    half = PEER_D_KEY // 2
    h = _adaln(x_ref[...], g_ref[...], sh_ref[0], sc_ref[0])
    h_ref[...] = h
    qt = _dot_nt(wqt_ref[...], h.astype(BF16))
    for c in range(n_chunks):
        q_s[c] = qt[:, c * LANES:(c + 1) * LANES]
    sk1, sk2 = sk_ref[0], sk_ref[1]
    col_id = _iota((PEER_TOPK, LANES), 0)
    sub = _iota((SUBLANES, LANES), 0)
    col_depth = jnp.zeros((SUBLANES, LANES), I32)
    for a in range(SUBLANES):
        col_depth = jnp.where(sub == a, PEER_CAND_COUNTS[a], col_depth)
    vals = [val_s.at[i] for i in range(2 * per_step)]
    idxs = [idx_s.at[i] for i in range(2 * per_step)]

    def body(step, carry):
        hh = step // (n_chunks // per_step)
        c0 = (step % (n_chunks // per_step)) * per_step
        q0 = pl.multiple_of(hh * PEER_D_KEY, PEER_D_KEY)
        scores = []
        for k in range(per_step):
            scores.append(_dot3(sk1, q_s[c0 + k, pl.ds(q0, half), :]))
            scores.append(_dot3(sk2, q_s[c0 + k, pl.ds(q0 + half, half), :]))
        _top16_rows(scores, PEER_N_KEYS, vals, idxs)
        state = []
        for k in range(per_step):
            v1, i1, v2, i2 = vals[2 * k], idxs[2 * k], vals[2 * k + 1], idxs[2 * k + 1]
            v1x, e1x = v1[0:SUBLANES, :], i1[0:SUBLANES, :] * PEER_N_KEYS
            xs = [jnp.where(col_depth > b, v1x + v2[b:b + 1, :], -jnp.inf) for b in range(PEER_TOPK)]
            ex = [e1x + i2[b:b + 1, :] for b in range(PEER_TOPK)]
            y = v1[SUBLANES:PEER_TOPK, :] + v2[0:1, :]
            ey = i1[SUBLANES:PEER_TOPK, :] * PEER_N_KEYS + i2[0:1, :]
            state.append([xs, ex, y, ey])
        for it in range(PEER_TOPK):
            for k in range(per_step):
                xs, ex, y, ey = state[k]
                heads = jnp.concatenate([xs[0], y], axis=0)
                m = jnp.max(heads, axis=0, keepdims=True)
                pick = jnp.min(jnp.where(heads == m, col_id, PEER_TOPK), axis=0, keepdims=True)
                win = col_id == pick
                eids = jnp.where(win, jnp.concatenate([ex[0], ey], axis=0), 0)
                idxs[2 * k][it:it + 1, :] = jnp.sum(eids, axis=0, keepdims=True)
                vals[2 * k][it:it + 1, :] = m
                win_x, win_y = win[0:SUBLANES, :], win[SUBLANES:PEER_TOPK, :]
                for dd in range(PEER_TOPK - 1 - it):
                    xs[dd] = jnp.where(win_x, xs[dd + 1], xs[dd])
                    ex[dd] = jnp.where(win_x, ex[dd + 1], ex[dd])
                state[k][2] = jnp.where(win_y, -jnp.inf, y)
        r0 = pl.multiple_of(hh * PEER_TOPK, PEER_TOPK)
        for k in range(per_step):
            top = vals[2 * k][...]
            e = jnp.exp(top - top[0:1, :])
            gts_s[c0 + k, pl.ds(r0, PEER_TOPK), :] = e / jnp.sum(e, axis=0, keepdims=True)
            ids_s[c0 + k, pl.ds(r0, PEER_TOPK), :] = idxs[2 * k][...] * PACK_ROWS
        return carry

    lax.fori_loop(0, PEER_HEADS * n_chunks // per_step, body, 0)
    for c in range(n_chunks):
        rows = slice(c * LANES, (c + 1) * LANES)
        ids_ref[rows, :] = ids_s[c].T
        gts_ref[rows, :] = gts_s[c].T


def _peer_route(x2, g, shift, scale, wqt_bf, sub_keys, seq):
    t, d = x2.shape
    tt = 512
    per_b = seq // tt
    n_chunks = tt // LANES
    nq = wqt_bf.shape[0]
    out_blk = pl.BlockSpec((tt, PEER_SLOTS), lambda i: (i, 0))
    return pl.pallas_call(
        _peer_route_kernel,
        grid=(t // tt,),
        in_specs=[
            pl.BlockSpec((tt, d), lambda i: (i, 0)),
            pl.BlockSpec((1, d), lambda i: (0, 0)),
            pl.BlockSpec((1, 1, d), lambda i: (i // per_b, 0, 0)),
            pl.BlockSpec((1, 1, d), lambda i: (i // per_b, 0, 0)),
            pl.BlockSpec((nq, d), lambda i: (0, 0)),
            pl.BlockSpec(sub_keys.shape, lambda i: (0, 0, 0)),
        ],
        out_specs=[pl.BlockSpec((tt, d), lambda i: (i, 0)), out_blk, out_blk],
        out_shape=[jax.ShapeDtypeStruct((t, d), F32),
                   jax.ShapeDtypeStruct((t, PEER_SLOTS), I32),
                   jax.ShapeDtypeStruct((t, PEER_SLOTS), F32)],
        scratch_shapes=[pltpu.VMEM((n_chunks, nq, LANES), F32),
                        pltpu.VMEM((2 * ROUTE_CHUNKS_PER_STEP, PEER_TOPK, LANES), F32),
                        pltpu.VMEM((2 * ROUTE_CHUNKS_PER_STEP, PEER_TOPK, LANES), I32),
                        pltpu.VMEM((n_chunks, PEER_SLOTS, LANES), I32),
                        pltpu.VMEM((n_chunks, PEER_SLOTS, LANES), F32)],
        compiler_params=_cparams(1, V7X_VMEM_LIMIT),
        name="peer_route",
    )(x2, g, shift, scale, wqt_bf, sub_keys)


def _pack_table(tab):
    e, d = tab.shape
    bits = lax.bitcast_convert_type(tab.astype(BF16), jnp.uint16).astype(U32)
    bits = bits.reshape(e, d // (2 * LANES), 2, LANES)
    words = bits[:, :, 0, :] | (bits[:, :, 1, :] << 16)
    return words.reshape(e * (d // (2 * LANES)), LANES)


def _table_spec(rows):
    return pl.BlockSpec((rows, LANES), lambda i: (0, 0), pipeline_mode=pl.Buffered(1))


def _gelu_exact(a):
    return 0.5 * a * (1.0 + lax.erf(a * (2.0 ** -0.5)))


FEAT_CHUNKS = 8
SLOT_WIDTH = PEER_SLOTS * FEAT_CHUNKS


def _gather_rows(ids_ref, base, tbl_ref, slot):
    for j in range(PEER_SLOTS):
        if j % ID_VIEW == 0:
            ids_part = ids_ref.at[pl.ds(base + j, ID_VIEW)]
        row0 = pl.multiple_of(ids_part[j % ID_VIEW], PACK_ROWS)
        slot[PACK_ROWS * j:PACK_ROWS * (j + 1), :] = tbl_ref[pl.ds(row0, PACK_ROWS), :]


def _pipelined_tokens(tt, ids_ref, tbl_ref, slots, compute):
    group = len(slots) // 2
    first, second = slots[:group], slots[group:]
    last = tt - 1

    def gather(t, slot):
        _gather_rows(ids_ref, jnp.minimum(t, last) * PEER_SLOTS, tbl_ref, slot)

    for k in range(group):
        gather(k, first[k])

    def step(q, carry):
        t = 2 * group * q
        for k in range(group):
            compute(t + k, first[k])
        for k in range(group):
            gather(t + group + k, second[k])
        for k in range(group):
            compute(t + group + k, second[k])
        for k in range(group):
            gather(t + 2 * group + k, first[k])
        return carry

    lax.fori_loop(0, tt // (2 * group), step, 0)


def _chunk_diag():
    return (_iota((FEAT_CHUNKS, SLOT_WIDTH), 1) % FEAT_CHUNKS) == _iota((FEAT_CHUNKS, SLOT_WIDTH), 0)


def _peer_u_kernel(ids_ref, h_ref, g_ref, tbl_ref, coef_ref, *scratch):
    slots, (hx_s, rs_s) = scratch[:N_SLOTS], scratch[N_SLOTS:]
    tt = h_ref.shape[0]
    for c in range(FEAT_CHUNKS):
        hx_s[pl.ds(c, tt, stride=FEAT_CHUNKS), :] = h_ref[:, c * LANES:(c + 1) * LANES]
    diag = _chunk_diag()

    def compute(t, slot):
        rows = pltpu.bitcast(slot[...], BF16)
        x8 = hx_s[pl.ds(pl.multiple_of(t * FEAT_CHUNKS, FEAT_CHUNKS), FEAT_CHUNKS), :]
        part = _dot_nt(x8.astype(BF16), rows)
        rs_s[pl.ds(t, 1), :] = jnp.sum(jnp.where(diag, part, 0.0), axis=0, keepdims=True)

    _pipelined_tokens(tt, ids_ref, tbl_ref, slots, compute)
    group = jnp.where(_iota((SLOT_WIDTH, PEER_SLOTS), 0) // FEAT_CHUNKS == _iota((SLOT_WIDTH, PEER_SLOTS), 1),
                      1.0, 0.0).astype(BF16)
    hi, lo = _split_bf16(rs_s[...])
    act = _dot(hi, group) + _dot(lo, group)
    coef_ref[...] = g_ref[...] * _gelu_exact(act)


N_SLOTS = 16
ID_VIEW = 16
PEER_TOKEN_TILE = 512


def _slot_scratch():
    return [pltpu.VMEM((PEER_SLOTS * PACK_ROWS, LANES), U32)] * N_SLOTS


def _peer_u(ids_flat, h, gates, table, tt):
    t, d = h.shape
    return pl.pallas_call(
        _peer_u_kernel,
        grid=(t // tt,),
        in_specs=[
            pl.BlockSpec((tt * PEER_SLOTS,), lambda i: (i,), memory_space=pltpu.SMEM),
            pl.BlockSpec((tt, d), lambda i: (i, 0)),
            pl.BlockSpec((tt, PEER_SLOTS), lambda i: (i, 0)),
            _table_spec(table.shape[0]),
        ],
        out_specs=pl.BlockSpec((tt, PEER_SLOTS), lambda i: (i, 0)),
        out_shape=jax.ShapeDtypeStruct((t, PEER_SLOTS), F32),
        scratch_shapes=_slot_scratch() + [
                        pltpu.VMEM((tt * FEAT_CHUNKS, LANES), F32),
                        pltpu.VMEM((tt, SLOT_WIDTH), F32)],
        compiler_params=_cparams(1, V7X_VMEM_LIMIT),
        name="peer_expert_in",
    )(ids_flat, h, gates, table)


def _peer_v_kernel(ids_ref, coef_ref, x_ref, gate_ref, tbl_ref, o_ref, *scratch):
    slots, (ce_hi_s, ce_lo_s, res_s) = scratch[:N_SLOTS], scratch[N_SLOTS:]
    tt = x_ref.shape[0]
    spread = jnp.where(_iota((PEER_SLOTS, SLOT_WIDTH), 1) // FEAT_CHUNKS == _iota((PEER_SLOTS, SLOT_WIDTH), 0),
                       1.0, 0.0).astype(BF16)
    hi, lo = _split_bf16(coef_ref[...])
    ce_hi_s[...] = _dot(hi, spread)
    ce_lo_s[...] = _dot(lo, spread)
    diag = _chunk_diag()

    def compute(t, slot):
        rows = pltpu.bitcast(slot[...], BF16)
        a_hi = jnp.where(diag, ce_hi_s[pl.ds(t, 1), :], 0.0)
        a_lo = jnp.where(diag, ce_lo_s[pl.ds(t, 1), :], 0.0)
        both = _dot(jnp.concatenate([a_hi, a_lo], axis=0).astype(BF16), rows)
        r0 = pl.multiple_of(t * FEAT_CHUNKS, FEAT_CHUNKS)
        res_s[pl.ds(r0, FEAT_CHUNKS), :] = both[0:FEAT_CHUNKS, :] + both[FEAT_CHUNKS:2 * FEAT_CHUNKS, :]

    _pipelined_tokens(tt, ids_ref, tbl_ref, slots, compute)
    for c in range(FEAT_CHUNKS):
        cols = slice(c * LANES, (c + 1) * LANES)
        y = res_s[pl.ds(c, tt, stride=FEAT_CHUNKS), :]
        o_ref[:, cols] = x_ref[:, cols] + gate_ref[0][:, cols] * y


def _peer_v(ids_flat, coef, x2, gate, table, seq, tt):
    t, d = x2.shape
    per_b = seq // tt
    blk = pl.BlockSpec((tt, d), lambda i: (i, 0))
    return pl.pallas_call(
        _peer_v_kernel,
        grid=(t // tt,),
        in_specs=[
            pl.BlockSpec((tt * PEER_SLOTS,), lambda i: (i,), memory_space=pltpu.SMEM),
            pl.BlockSpec((tt, PEER_SLOTS), lambda i: (i, 0)),
            blk,
            pl.BlockSpec((1, 1, d), lambda i: (i // per_b, 0, 0)),
            _table_spec(table.shape[0]),
        ],
        out_specs=blk,
        out_shape=jax.ShapeDtypeStruct((t, d), F32),
        scratch_shapes=_slot_scratch() + [
                        pltpu.VMEM((tt, SLOT_WIDTH), F32), pltpu.VMEM((tt, SLOT_WIDTH), F32),
                        pltpu.VMEM((tt * FEAT_CHUNKS, LANES), F32)],
        compiler_params=_cparams(1, V7X_VMEM_LIMIT),
        name="peer_expert_out",
    )(ids_flat, coef, x2, gate, table)


def _peer_ffn(x2, g, shift, scale, gate, wq, sub_keys, table_u, table_v, seq):
    t, d = x2.shape
    tt = PEER_TOKEN_TILE
    h, ids, gates = _peer_route(x2, g, shift, scale, wq.T.astype(BF16), sub_keys, seq)
    ids_flat = ids.reshape(t * PEER_SLOTS)
    coef = _peer_u(ids_flat, h, gates, table_u, tt)
    return _peer_v(ids_flat, coef, x2, gate, table_v, seq, tt)


def _rope_tables(seq):
    half = HEAD_DIM // 2
    inv_freq = ROPE_THETA ** (-jnp.arange(half, dtype=F32) / half)
    ang = jnp.arange(seq).astype(F32)[:, None] * inv_freq[None, :]
    reps = LANES // half
    return jnp.tile(jnp.cos(ang), (1, reps)), jnp.tile(jnp.sin(ang), (1, reps))


def _two_heads(gain):
    return jnp.tile(gain.reshape(1, HEAD_DIM), (1, LANES // HEAD_DIM))


def kernel(x, c, ada_w, ada_b, norm_mix_g, norm_ffn_g, w_in_ab, w_out_ab, sinks_a, qnorm_a, knorm_a,
           w_in_cd, w_out_cd, qnorm_c, knorm_c, qnorm_d, knorm_d, peer_wq, peer_subkeys, peer_u, peer_v):
    b, seq, d = x.shape
    depth = ada_w.shape[0]
    t = b * seq
    cos, sin = _rope_tables(seq)
    mod = _modulation(c, ada_w, ada_b)
    x2 = x.reshape(t, d)
    for layer in range(depth):
        shift_m, scale_m, gate_m, shift_f, scale_f, gate_f = [
            m.reshape(b, 1, d) for m in jnp.split(mod[layer], 6, axis=-1)]
        g_mix = norm_mix_g[layer].reshape(1, d)
        i = layer // 2
        if layer % 2 == 0:
            proj = _norm_proj(x2, g_mix, shift_m, scale_m, w_in_ab[i].astype(BF16), seq)
            proj = proj.reshape(b, seq, -1)
            ya = _swa_attention(proj, sinks_a[i], cos, sin, _two_heads(qnorm_a[i]), _two_heads(knorm_a[i]))
            b_col = (A_Q_HEADS + 2 * A_KV_HEADS) * HEAD_DIM // LANES
            yb = _stick_attention(proj, b_col)
            w_out = w_out_ab[i]
        else:
            proj = _norm_proj(x2, g_mix, shift_m, scale_m, w_in_cd[i].astype(BF16), seq)
            proj = proj.reshape(b, seq, -1)
            ya = _qkv_attention_call(_dilated_kernel, "dilated_attention", proj, 0, C_HEADS, cos, sin,
                                     _two_heads(qnorm_c[i]), _two_heads(knorm_c[i]))
            d_col = 3 * C_HEADS * HEAD_DIM // LANES
            yb = _qkv_attention_call(_moba_kernel, "moba_attention", proj, d_col, D_HEADS, cos, sin,
                                     _two_heads(qnorm_d[i]), _two_heads(knorm_d[i]),
                                     extra_scratch=(pltpu.VMEM((LANES, LANES), F32),
                                                    pltpu.VMEM((seq, LANES), F32),
                                                    pltpu.VMEM((seq, LANES), F32)))
            w_out = w_out_cd[i]
        x2 = _out_proj(x2, ya.reshape(t, -1), yb.reshape(t, -1), w_out.astype(BF16), gate_m, seq)
        x2 = _peer_ffn(x2, norm_ffn_g[layer].reshape(1, d), shift_f, scale_f, gate_f,
                       peer_wq[layer], peer_subkeys[layer],
                       _pack_table(peer_u[layer]), _pack_table(peer_v[layer]), seq)
    return x2.reshape(b, seq, d)
```

```python
import jax
import jax.numpy as jnp
from jax import lax
from jax.experimental import pallas as pl
from jax.experimental.pallas import tpu as pltpu

F32 = jnp.float32
BF16 = jnp.bfloat16
I32 = jnp.int32
U32 = jnp.uint32

HEAD_DIM = 64
ROPE_THETA = 10000.0
NORM_EPS = 1e-6
LANES = 128
QUERY_BLOCK = 128
A_Q_HEADS, A_KV_HEADS = 8, 2
B_HEADS = C_HEADS = D_HEADS = 8
C_PATTERNS = ((128, 1), (512, 4), (2048, 16))
MOBA_BLOCK, MOBA_TOPK = 256, 3
PEER_HEADS, PEER_N_KEYS, PEER_TOPK, PEER_D_KEY = 8, 128, 16, 256
PEER_SLOTS = PEER_HEADS * PEER_TOPK
NEG_BIG = -1e30
PEER_CAND_COUNTS = tuple(PEER_TOPK // (a + 1) for a in range(PEER_TOPK))
PACK_ROWS = 4
MIB = 1024 * 1024
V7X_VMEM_BYTES = 64 * MIB
STREAM_VMEM_LIMIT = 40 * MIB
TABLE_VMEM_LIMIT = V7X_VMEM_BYTES - 8 * MIB


def _cparams(n_axes, vmem_bytes=STREAM_VMEM_LIMIT):
    return pltpu.CompilerParams(
        dimension_semantics=("arbitrary",) * n_axes,
        vmem_limit_bytes=vmem_bytes)


def _split_bf16(a):
    hi = a.astype(BF16)
    lo = (a - hi.astype(F32)).astype(BF16)
    return hi, lo


def _dot(a, b):
    return jnp.dot(a, b, preferred_element_type=F32)


def _dot_nt(a, b):
    return lax.dot_general(a, b, (((1,), (1,)), ((), ())), preferred_element_type=F32)


def _dot3(a, b):
    ah, al = _split_bf16(a)
    bh, bl = _split_bf16(b)
    return _dot(ah, bh) + _dot(ah, bl) + _dot(al, bh)


def _dot3_nt(a, b):
    ah, al = _split_bf16(a)
    bh, bl = _split_bf16(b)
    return _dot_nt(ah, bh) + _dot_nt(ah, bl) + _dot_nt(al, bh)


def _iota(shape, dim):
    return lax.broadcasted_iota(I32, shape, dim)


def _mod_kernel(c_ref, w_ref, b_ref, o_ref):
    c = c_ref[...]
    cond = c * jax.nn.sigmoid(c)
    o_ref[0] = _dot3(cond, w_ref[0]) + b_ref[0]


def _modulation(c, ada_w, ada_b):
    depth, d, n = ada_w.shape
    b = c.shape[0]
    tn = 1024
    return pl.pallas_call(
        _mod_kernel,
        grid=(depth, n // tn),
        in_specs=[
            pl.BlockSpec((b, d), lambda l, j: (0, 0)),
            pl.BlockSpec((1, d, tn), lambda l, j: (l, 0, j)),
            pl.BlockSpec((1, 1, tn), lambda l, j: (l, 0, j)),
        ],
        out_specs=pl.BlockSpec((1, b, tn), lambda l, j: (l, 0, j)),
        out_shape=jax.ShapeDtypeStruct((depth, b, n), F32),
        compiler_params=_cparams(2),
        name="adaln_modulation",
    )(c, ada_w, ada_b.reshape(depth, 1, n))


def _adaln(x, g, shift, scale):
    ms = jnp.mean(x * x, axis=-1, keepdims=True)
    y = x * lax.rsqrt(ms + NORM_EPS) * g
    return y * (1.0 + scale) + shift


def _norm_proj_kernel(x_ref, g_ref, sh_ref, sc_ref, w_ref, o_ref):
    h = _adaln(x_ref[...], g_ref[...], sh_ref[0], sc_ref[0])
    o_ref[...] = _dot(h.astype(BF16), w_ref[...])


def _norm_proj(x2, g, shift, scale, w_bf, seq):
    t, d = x2.shape
    n = w_bf.shape[1]
    tt = 512
    per_b = seq // tt
    return pl.pallas_call(
        _norm_proj_kernel,
        grid=(t // tt,),
        in_specs=[
            pl.BlockSpec((tt, d), lambda i: (i, 0)),
            pl.BlockSpec((1, d), lambda i: (0, 0)),
            pl.BlockSpec((1, 1, d), lambda i: (i // per_b, 0, 0)),
            pl.BlockSpec((1, 1, d), lambda i: (i // per_b, 0, 0)),
            pl.BlockSpec((d, n), lambda i: (0, 0)),
        ],
        out_specs=pl.BlockSpec((tt, n), lambda i: (i, 0)),
        out_shape=jax.ShapeDtypeStruct((t, n), F32),
        compiler_params=_cparams(1),
        name="adaln_in_proj",
    )(x2, g, shift, scale, w_bf)


def _out_proj_kernel(x_ref, ya_ref, yb_ref, w_ref, gate_ref, o_ref):
    half = ya_ref.shape[1]
    y = _dot(ya_ref[...].astype(BF16), w_ref[0:half, :])
    y = y + _dot(yb_ref[...].astype(BF16), w_ref[half:2 * half, :])
    o_ref[...] = x_ref[...] + gate_ref[0] * y


def _out_proj(x2, ya, yb, w_bf, gate, seq):
    t, d = x2.shape
    half = ya.shape[1]
    tt = 512
    per_b = seq // tt
    return pl.pallas_call(
        _out_proj_kernel,
        grid=(t // tt,),
        in_specs=[
            pl.BlockSpec((tt, d), lambda i: (i, 0)),
            pl.BlockSpec((tt, half), lambda i: (i, 0)),
            pl.BlockSpec((tt, half), lambda i: (i, 0)),
            pl.BlockSpec((2 * half, d), lambda i: (0, 0)),
            pl.BlockSpec((1, 1, d), lambda i: (i // per_b, 0, 0)),
        ],
        out_specs=pl.BlockSpec((tt, d), lambda i: (i, 0)),
        out_shape=jax.ShapeDtypeStruct((t, d), F32),
        compiler_params=_cparams(1),
        name="mixer_out_proj",
    )(x2, ya, yb, w_bf, gate)


def _lane_row():
    return _iota((1, LANES), 1)


def _head_segment_ones():
    r = _iota((LANES, LANES), 0) // HEAD_DIM
    c = _iota((LANES, LANES), 1) // HEAD_DIM
    return jnp.where(r == c, 1.0, 0.0).astype(BF16)


def _headnorm_rope(a, g, cos, sin):
    hi, lo = _split_bf16(a * a)
    seg = _head_segment_ones()
    ms = (_dot(hi, seg) + _dot(lo, seg)) * (1.0 / HEAD_DIM)
    y = a * lax.rsqrt(ms + NORM_EPS) * g
    half = HEAD_DIM // 2
    upper = pltpu.roll(y, LANES - half, axis=1)
    lower = pltpu.roll(y, half, axis=1)
    first_half = (_lane_row() % HEAD_DIM) < half
    rot = jnp.where(first_half, -upper, lower)
    return y * cos + rot * sin


def _head_masks():
    lane = _lane_row()
    return lane < HEAD_DIM, lane >= HEAD_DIM


def _attn_specs(seq, qcol, kcol, vcol, kv_shared):
    blk = (1, seq, LANES)
    q_spec = pl.BlockSpec(blk, lambda b, p: (b, 0, qcol + p))
    if kv_shared:
        k_spec = pl.BlockSpec(blk, lambda b, p: (b, 0, kcol))
        v_spec = pl.BlockSpec(blk, lambda b, p: (b, 0, vcol))
    else:
        k_spec = pl.BlockSpec(blk, lambda b, p: (b, 0, kcol + p))
        v_spec = pl.BlockSpec(blk, lambda b, p: (b, 0, vcol + p))
    return q_spec, k_spec, v_spec


def _row_spec(seq):
    return pl.BlockSpec((seq, LANES), lambda b, p: (0, 0))


def _gain_spec():
    return pl.BlockSpec((1, LANES), lambda b, p: (0, 0))


def _store_heads(o_ref, r0, outs):
    first, _ = _head_masks()
    o_ref[0, pl.ds(r0, QUERY_BLOCK), :] = jnp.where(first, outs[0], outs[1])


def _swa_kernel(sinks_ref, q_ref, k_ref, v_ref, cos_ref, sin_ref, gq_ref, gk_ref, o_ref,
                q0_s, q1_s, k_s, v_s):
    p = pl.program_id(1)
    seq = q_ref.shape[1]
    cos, sin = cos_ref[...], sin_ref[...]
    first, second = _head_masks()
    qn = _headnorm_rope(q_ref[0], gq_ref[...], cos, sin) * (HEAD_DIM ** -0.5)
    q0_s[...] = jnp.where(first, qn, 0.0).astype(BF16)
    q1_s[...] = jnp.where(second, qn, 0.0).astype(BF16)
    pairs_per_kv = (A_Q_HEADS // A_KV_HEADS) // 2
    keep = jnp.logical_xor(first, (p // pairs_per_kv) == 1)
    kn = _headnorm_rope(k_ref[0], gk_ref[...], cos, sin)
    k_s[...] = jnp.where(keep, kn, pltpu.roll(kn, HEAD_DIM, axis=1)).astype(BF16)
    v = v_ref[0]
    v_s[...] = jnp.where(keep, v, pltpu.roll(v, HEAD_DIM, axis=1)).astype(BF16)

    qi = _iota((QUERY_BLOCK, QUERY_BLOCK), 0)
    ki = _iota((QUERY_BLOCK, QUERY_BLOCK), 1)

    def qblock(i, carry):
        r0 = pl.multiple_of(i * QUERY_BLOCK, QUERY_BLOCK)
        rp = pl.multiple_of(jnp.maximum(i - 1, 0) * QUERY_BLOCK, QUERY_BLOCK)
        kc, kp = k_s[pl.ds(r0, QUERY_BLOCK), :], k_s[pl.ds(rp, QUERY_BLOCK), :]
        vc, vp = v_s[pl.ds(r0, QUERY_BLOCK), :], v_s[pl.ds(rp, QUERY_BLOCK), :]
        mask_c = ki <= qi
        mask_p = jnp.logical_and(ki > qi, i > 0)
        qhs = [q_s[pl.ds(r0, QUERY_BLOCK), :] for q_s in (q0_s, q1_s)]
        scores = [(_dot_nt(qh, kc), _dot_nt(qh, kp)) for qh in qhs]
        probs, denoms = [], []
        for hh in range(2):
            sc = jnp.where(mask_c, scores[hh][0], NEG_BIG)
            sp = jnp.where(mask_p, scores[hh][1], NEG_BIG)
            sink = sinks_ref[2 * p + hh]
            m = jnp.maximum(jnp.max(sc, axis=1, keepdims=True), jnp.max(sp, axis=1, keepdims=True))
            m = jnp.maximum(m, sink)
            ec, ep = jnp.exp(sc - m), jnp.exp(sp - m)
            denoms.append(jnp.sum(ec, axis=1, keepdims=True) + jnp.sum(ep, axis=1, keepdims=True)
                          + jnp.exp(sink - m))
            probs.append((ec.astype(BF16), ep.astype(BF16)))
        outs = [(_dot(probs[hh][0], vc) + _dot(probs[hh][1], vp)) / denoms[hh] for hh in range(2)]
        _store_heads(o_ref, r0, outs)
        return carry

    lax.fori_loop(0, seq // QUERY_BLOCK, qblock, 0)


def _swa_attention(proj, sinks, cos, sin, gq, gk):
    b, seq, _ = proj.shape
    n_pairs = A_Q_HEADS // 2
    kcol = A_Q_HEADS * HEAD_DIM // LANES
    vcol = kcol + A_KV_HEADS * HEAD_DIM // LANES
    q_spec, k_spec, v_spec = _attn_specs(seq, 0, kcol, vcol, True)
    return pl.pallas_call(
        _swa_kernel,
        grid=(b, n_pairs),
        in_specs=[pl.BlockSpec(memory_space=pltpu.SMEM), q_spec, k_spec, v_spec,
                  _row_spec(seq), _row_spec(seq), _gain_spec(), _gain_spec()],
        out_specs=pl.BlockSpec((1, seq, LANES), lambda b_, p: (b_, 0, p)),
        out_shape=jax.ShapeDtypeStruct((b, seq, n_pairs * LANES), F32),
        scratch_shapes=[pltpu.VMEM((seq, LANES), BF16)] * 4,
        compiler_params=_cparams(2),
        name="swa_gqa_attention",
    )(sinks, proj, proj, proj, cos, sin, gq, gk)


STICK_GROUP = 4
STICK_QUERY_ROWS = 256


def _stick_kernel(q_ref, k_ref, v_ref, o_ref, k_s, v_s):
    seq = q_ref.shape[1]
    k_s[...] = k_ref[0].astype(BF16)
    v_s[...] = v_ref[0].astype(BF16)
    first, second = _head_masks()
    qr = min(STICK_QUERY_ROWS, seq)
    kw = QUERY_BLOCK
    qi = _iota((qr, kw), 0)
    ki = _iota((qr, kw), 1)
    wr = _iota((2 * kw, 2 * kw), 0) % kw
    wc = _iota((2 * kw, 2 * kw), 1)
    suffix_w = jnp.where(jnp.logical_or(wc >= kw, wr > wc), 1.0, 0.0).astype(BF16)

    def qblock(i, carry):
        r0 = pl.multiple_of(i * qr, qr)
        q = q_ref[0, pl.ds(r0, qr), :] * (HEAD_DIM ** -0.5)
        qhs = [jnp.where(msk, q, 0.0).astype(BF16) for msk in (first, second)]
        n_blocks = (r0 + qr) // kw

        def kgroup(g, st):
            accs, laters = [st[0], st[1]], [st[2], st[3]]
            chains = [(u, hh) for u in range(STICK_GROUP) for hh in range(2)]
            vbs, pasts, zs = [], [], {}
            for u in range(STICK_GROUP):
                j = n_blocks - 1 - (g * STICK_GROUP + u)
                live = j >= 0
                c0 = pl.multiple_of(jnp.maximum(j, 0) * kw, kw)
                kb = k_s[pl.ds(c0, kw), :]
                vbs.append(v_s[pl.ds(c0, kw), :])
                pasts.append(jnp.logical_and((c0 + ki) < (r0 + qi), live))
                for hh in range(2):
                    zs[u, hh] = _dot_nt(qhs[hh], kb)
            logit, sums = {}, {}
            for u, hh in chains:
                z = zs[u, hh]
                sp = jnp.maximum(z, 0.0) + jnp.log(1.0 + jnp.exp(-jnp.abs(z)))
                log_keep = jnp.where(pasts[u], -sp, 0.0)
                logit[u, hh] = z - sp
                hi, lo = _split_bf16(log_keep)
                sums[u, hh] = _dot(jnp.concatenate([hi, lo], axis=1), suffix_w)
            ws = {}
            for u, hh in chains:
                inner, total = sums[u, hh][:, :kw], sums[u, hh][:, kw:]
                ws[u, hh] = jnp.where(pasts[u], jnp.exp(logit[u, hh] + inner + laters[hh]), 0.0).astype(BF16)
                laters[hh] = laters[hh] + total
            for u, hh in chains:
                accs[hh] = accs[hh] + _dot(ws[u, hh], vbs[u])
            return accs[0], accs[1], laters[0], laters[1]

        zero = jnp.zeros((qr, LANES), F32)
        n_groups = (n_blocks + STICK_GROUP - 1) // STICK_GROUP
        st = lax.fori_loop(0, n_groups, kgroup, (zero, zero, zero, zero))
        o_ref[0, pl.ds(r0, qr), :] = jnp.where(first, st[0], st[1])
        return carry

    lax.fori_loop(0, seq // qr, qblock, 0)


def _stick_attention(proj, qcol):
    b, seq, _ = proj.shape
    n_pairs = B_HEADS // 2
    q_spec, k_spec, v_spec = _attn_specs(seq, qcol, qcol + n_pairs, qcol + 2 * n_pairs, False)
    return pl.pallas_call(
        _stick_kernel,
        grid=(b, n_pairs),
        in_specs=[q_spec, k_spec, v_spec],
        out_specs=pl.BlockSpec((1, seq, LANES), lambda b_, p: (b_, 0, p)),
        out_shape=jax.ShapeDtypeStruct((b, seq, n_pairs * LANES), F32),
        scratch_shapes=[pltpu.VMEM((seq, LANES), BF16)] * 2,
        compiler_params=_cparams(2),
        name="stick_breaking_attention",
    )(proj, proj, proj)


def _prep_qkv(q_ref, k_ref, v_ref, cos_ref, sin_ref, gq_ref, gk_ref, q0_s, q1_s, k_s, v_s):
    cos, sin = cos_ref[...], sin_ref[...]
    first, second = _head_masks()
    qn = _headnorm_rope(q_ref[0], gq_ref[...], cos, sin) * (HEAD_DIM ** -0.5)
    q0_s[...] = jnp.where(first, qn, 0.0).astype(BF16)
    q1_s[...] = jnp.where(second, qn, 0.0).astype(BF16)
    kn = _headnorm_rope(k_ref[0], gk_ref[...], cos, sin)
    k_s[...] = kn.astype(BF16)
    v_s[...] = v_ref[0].astype(BF16)
    return qn, kn


DILATED_KEY_TILE = 512
DILATED_QUERY_ROWS = 256


def _dilated_kernel(q_ref, k_ref, v_ref, cos_ref, sin_ref, gq_ref, gk_ref, o_ref,
                    q0_s, q1_s, k_s, v_s):
    seq = q_ref.shape[1]
    _prep_qkv(q_ref, k_ref, v_ref, cos_ref, sin_ref, gq_ref, gk_ref, q0_s, q1_s, k_s, v_s)
    kt = min(DILATED_KEY_TILE, seq)
    qr = min(DILATED_QUERY_ROWS, seq)
    qk = _iota((qr, kt), 0) - _iota((qr, kt), 1)
    on_stride = [jnp.where((qk & (dil - 1)) == 0, 1.0, 0.0) for _, dil in C_PATTERNS]
    first_head, _ = _head_masks()

    def qblock(i, carry):
        r0 = pl.multiple_of(i * qr, qr)
        qhs = [q_s[pl.ds(r0, qr), :] for q_s in (q0_s, q1_s)]

        def ktile(g, st):
            c0 = pl.multiple_of(g * kt, kt)
            d = (r0 - c0) + qk
            count = jnp.zeros(d.shape, F32)
            for (window, _), stride_ok in zip(C_PATTERNS, on_stride):
                count = count + jnp.where(d <= window, stride_ok, 0.0)
            count = jnp.where(d >= 0, count, 0.0)
            kb, vb = k_s[pl.ds(c0, kt), :], v_s[pl.ds(c0, kt), :]
            scores = [_dot_nt(qhs[hh], kb) for hh in range(2)]
            new, prs = [], []
            for hh in range(2):
                m, l, acc = st[3 * hh:3 * hh + 3]
                s = jnp.where(count > 0.0, scores[hh], NEG_BIG)
                m_new = jnp.maximum(m, jnp.max(s, axis=1, keepdims=True))
                pr = count * jnp.exp(s - m_new)
                alpha = jnp.exp(m - m_new)
                new += [m_new, alpha * l + jnp.sum(pr, axis=1, keepdims=True), alpha * acc]
                prs.append(pr.astype(BF16))
            for hh in range(2):
                new[3 * hh + 2] = new[3 * hh + 2] + _dot(prs[hh], vb)
            return tuple(new)

        init = (jnp.full((qr, 1), NEG_BIG, F32), jnp.zeros((qr, 1), F32),
                jnp.zeros((qr, LANES), F32)) * 2
        st = lax.fori_loop(0, (r0 + qr + kt - 1) // kt, ktile, init)
        o_ref[0, pl.ds(r0, qr), :] = jnp.where(first_head, st[2] / st[1], st[5] / st[4])
        return carry

    lax.fori_loop(0, seq // qr, qblock, 0)


def _qkv_attention_call(kernel, name, proj, qcol, n_heads, cos, sin, gq, gk, extra_scratch=()):
    b, seq, _ = proj.shape
    n_pairs = n_heads // 2
    q_spec, k_spec, v_spec = _attn_specs(seq, qcol, qcol + n_pairs, qcol + 2 * n_pairs, False)
    return pl.pallas_call(
        kernel,
        grid=(b, n_pairs),
        in_specs=[q_spec, k_spec, v_spec, _row_spec(seq), _row_spec(seq), _gain_spec(), _gain_spec()],
        out_specs=pl.BlockSpec((1, seq, LANES), lambda b_, p: (b_, 0, p)),
        out_shape=jax.ShapeDtypeStruct((b, seq, n_pairs * LANES), F32),
        scratch_shapes=[pltpu.VMEM((seq, LANES), BF16)] * 4 + list(extra_scratch),
        compiler_params=_cparams(2),
        name=name,
    )(proj, proj, proj, cos, sin, gq, gk)


def _moba_kernel(q_ref, k_ref, v_ref, cos_ref, sin_ref, gq_ref, gk_ref, o_ref,
                 q0_s, q1_s, k_s, v_s, km_s, sel0_s, sel1_s):
    seq = q_ref.shape[1]
    n_blocks = seq // MOBA_BLOCK
    qn, kn = _prep_qkv(q_ref, k_ref, v_ref, cos_ref, sin_ref, gq_ref, gk_ref, q0_s, q1_s, k_s, v_s)
    km_s[...] = jnp.zeros(km_s.shape, F32)
    km_s[0:n_blocks, :] = jnp.mean(kn.reshape(n_blocks, MOBA_BLOCK, LANES), axis=1)
    first, second = _head_masks()

    rows8 = _iota((8, seq), 0)
    own8 = _iota((8, seq), 1) // MOBA_BLOCK
    valid = rows8 < own8
    for msk, sel_s in ((first, sel0_s), (second, sel1_s)):
        gate = _dot3_nt(km_s[...], jnp.where(msk, qn, 0.0))[0:8, :]
        gm = jnp.where(valid, gate, -jnp.inf)
        rank = jnp.zeros((8, seq), F32)
        for n2 in range(n_blocks):
            g2 = gm[n2:n2 + 1, :]
            beats = jnp.logical_or(g2 > gm, jnp.logical_and(g2 == gm, n2 < rows8))
            rank = rank + jnp.where(jnp.logical_and(beats, n2 < own8), 1.0, 0.0)
        sel = jnp.where(jnp.logical_and(valid, rank < float(MOBA_TOPK)), 1.0, 0.0)
        sel = jnp.concatenate([sel, jnp.zeros((LANES - 8, seq), F32)], axis=0)
        sel_s[...] = sel.T

    qrows = MOBA_BLOCK
    lane_sq = _iota((qrows, LANES), 1)
    causal = _iota((qrows, MOBA_BLOCK), 1) <= _iota((qrows, MOBA_BLOCK), 0)
    second_block = _iota((qrows, 2 * MOBA_BLOCK), 1) >= MOBA_BLOCK
    first_head, _ = _head_masks()

    def qblock(own, carry):
        r0 = pl.multiple_of(own * qrows, qrows)
        qhs = [q_s[pl.ds(r0, qrows), :] for q_s in (q0_s, q1_s)]
        sels = [sel_s[pl.ds(r0, qrows), :] for sel_s in (sel0_s, sel1_s)]
        kb, vb = k_s[pl.ds(r0, MOBA_BLOCK), :], v_s[pl.ds(r0, MOBA_BLOCK), :]
        scores = [_dot_nt(qhs[hh], kb) for hh in range(2)]
        init, prs = [], []
        for hh in range(2):
            s = jnp.where(causal, scores[hh], NEG_BIG)
            m = jnp.max(s, axis=1, keepdims=True)
            pr = jnp.exp(s - m)
            init += [m, jnp.sum(pr, axis=1, keepdims=True), None]
            prs.append(pr.astype(BF16))
        for hh in range(2):
            init[3 * hh + 2] = _dot(prs[hh], vb)

        def kpair(g, st):
            c0 = pl.multiple_of(g * 2 * MOBA_BLOCK, 2 * MOBA_BLOCK)
            kb2, vb2 = k_s[pl.ds(c0, 2 * MOBA_BLOCK), :], v_s[pl.ds(c0, 2 * MOBA_BLOCK), :]
            scores = [_dot_nt(qhs[hh], kb2) for hh in range(2)]
            new, prs = [], []
            for hh in range(2):
                m, l, acc = st[3 * hh:3 * hh + 3]
                sel_a = jnp.sum(jnp.where(lane_sq == 2 * g, sels[hh], 0.0), axis=1, keepdims=True)
                sel_b = jnp.sum(jnp.where(lane_sq == 2 * g + 1, sels[hh], 0.0), axis=1, keepdims=True)
                keep = jnp.where(second_block, sel_b, sel_a) > 0.0
                s = jnp.where(keep, scores[hh], NEG_BIG)
                m_new = jnp.maximum(m, jnp.max(s, axis=1, keepdims=True))
                pr = jnp.exp(s - m_new)
                alpha = jnp.exp(m - m_new)
                new += [m_new, alpha * l + jnp.sum(pr, axis=1, keepdims=True), alpha * acc]
                prs.append(pr.astype(BF16))
            for hh in range(2):
                new[3 * hh + 2] = new[3 * hh + 2] + _dot(prs[hh], vb2)
            return tuple(new)

        st = lax.fori_loop(0, (own + 1) // 2, kpair, tuple(init))
        o_ref[0, pl.ds(r0, qrows), :] = jnp.where(first_head, st[2] / st[1], st[5] / st[4])
        return carry

    lax.fori_loop(0, seq // qrows, qblock, 0)


ROUTE_CHUNKS_PER_STEP = 4


def _oddeven_merge_sort_pairs(n):
    pairs = []

    def merge(lo, hi, r):
        step = 2 * r
        if step < hi - lo:
            merge(lo, hi, step)
            merge(lo + r, hi, step)
            pairs.extend((i, i + r) for i in range(lo + r, hi - r, step))
        else:
            pairs.append((lo, lo + r))

    def sort(lo, hi):
        if hi - lo >= 1:
            mid = lo + (hi - lo) // 2
            sort(lo, mid)
            sort(mid + 1, hi)
            merge(lo, hi, 1)

    sort(0, n - 1)
    return pairs


SUBLANES = 8


def _top16_rows(scores, n_rows, vals_refs, idx_refs):
    n_slabs = n_rows // SUBLANES
    sub = _iota((SUBLANES, LANES), 0)
    vals = [[s[SUBLANES * v:SUBLANES * (v + 1), :] for v in range(n_slabs)] for s in scores]
    idxs = [[sub + SUBLANES * v for v in range(n_slabs)] for _ in scores]
    for i, j in _oddeven_merge_sort_pairs(n_slabs):
        for va, ia in zip(vals, idxs):
            a, b = va[i], va[j]
            a_first = jnp.logical_or(a > b, jnp.logical_and(a == b, ia[i] < ia[j]))
            va[i], va[j] = jnp.maximum(a, b), jnp.minimum(a, b)
            ia[i], ia[j] = jnp.where(a_first, ia[i], ia[j]), jnp.where(a_first, ia[j], ia[i])
    for it in range(PEER_TOPK):
        for k, (va, ia) in enumerate(zip(vals, idxs)):
            m = jnp.max(va[0], axis=0, keepdims=True)
            pick = jnp.min(jnp.where(va[0] == m, ia[0], n_rows), axis=0, keepdims=True)
            vals_refs[k][it:it + 1, :] = m
            idx_refs[k][it:it + 1, :] = pick
            win = ia[0] == pick
            depth = PEER_TOPK - 1 - it
            for d in range(min(depth, n_slabs - 1)):
                va[d] = jnp.where(win, va[d + 1], va[d])
                ia[d] = jnp.where(win, ia[d + 1], ia[d])
            if depth >= n_slabs:
                va[n_slabs - 1] = jnp.where(win, -jnp.inf, va[n_slabs - 1])


def _peer_route_kernel(x_ref, g_ref, sh_ref, sc_ref, wqt_ref, sk_ref, h_ref, ids_ref, gts_ref,
                       q_s, val_s, idx_s, ids_s, gts_s):
    tt = x_ref.shape[0]
    n_chunks = tt // LANES
    per_step = ROUTE_CHUNKS_PER_STEP
    half = PEER_D_KEY // 2
    h = _adaln(x_ref[...], g_ref[...], sh_ref[0], sc_ref[0])
    h_ref[...] = h
    qt = _dot_nt(wqt_ref[...], h.astype(BF16))
    for c in range(n_chunks):
        q_s[c] = qt[:, c * LANES:(c + 1) * LANES]
    sk1, sk2 = sk_ref[0], sk_ref[1]
    col_id = _iota((PEER_TOPK, LANES), 0)
    sub = _iota((SUBLANES, LANES), 0)
    col_depth = jnp.zeros((SUBLANES, LANES), I32)
    for a in range(SUBLANES):
        col_depth = jnp.where(sub == a, PEER_CAND_COUNTS[a], col_depth)
    vals = [val_s.at[i] for i in range(2 * per_step)]
    idxs = [idx_s.at[i] for i in range(2 * per_step)]

    def body(step, carry):
        hh = step // (n_chunks // per_step)
        c0 = (step % (n_chunks // per_step)) * per_step
        q0 = pl.multiple_of(hh * PEER_D_KEY, PEER_D_KEY)
        scores = []
        for k in range(per_step):
            scores.append(_dot3(sk1, q_s[c0 + k, pl.ds(q0, half), :]))
            scores.append(_dot3(sk2, q_s[c0 + k, pl.ds(q0 + half, half), :]))
        _top16_rows(scores, PEER_N_KEYS, vals, idxs)
        state = []
        for k in range(per_step):
            v1, i1, v2, i2 = vals[2 * k], idxs[2 * k], vals[2 * k + 1], idxs[2 * k + 1]
            v1x, e1x = v1[0:SUBLANES, :], i1[0:SUBLANES, :] * PEER_N_KEYS
            xs = [jnp.where(col_depth > b, v1x + v2[b:b + 1, :], -jnp.inf) for b in range(PEER_TOPK)]
            ex = [e1x + i2[b:b + 1, :] for b in range(PEER_TOPK)]
            y = v1[SUBLANES:PEER_TOPK, :] + v2[0:1, :]
            ey = i1[SUBLANES:PEER_TOPK, :] * PEER_N_KEYS + i2[0:1, :]
            state.append([xs, ex, y, ey])
        for it in range(PEER_TOPK):
            for k in range(per_step):
                xs, ex, y, ey = state[k]
                heads = jnp.concatenate([xs[0], y], axis=0)
                m = jnp.max(heads, axis=0, keepdims=True)
                pick = jnp.min(jnp.where(heads == m, col_id, PEER_TOPK), axis=0, keepdims=True)
                win = col_id == pick
                eids = jnp.where(win, jnp.concatenate([ex[0], ey], axis=0), 0)
                idxs[2 * k][it:it + 1, :] = jnp.sum(eids, axis=0, keepdims=True)
                vals[2 * k][it:it + 1, :] = m
                win_x, win_y = win[0:SUBLANES, :], win[SUBLANES:PEER_TOPK, :]
                for dd in range(PEER_TOPK - 1 - it):
                    xs[dd] = jnp.where(win_x, xs[dd + 1], xs[dd])
                    ex[dd] = jnp.where(win_x, ex[dd + 1], ex[dd])
                state[k][2] = jnp.where(win_y, -jnp.inf, y)
        r0 = pl.multiple_of(hh * PEER_TOPK, PEER_TOPK)
        for k in range(per_step):
            top = vals[2 * k][...]
            e = jnp.exp(top - top[0:1, :])
            gts_s[c0 + k, pl.ds(r0, PEER_TOPK), :] = e / jnp.sum(e, axis=0, keepdims=True)
            ids_s[c0 + k, pl.ds(r0, PEER_TOPK), :] = idxs[2 * k][...] * PACK_ROWS
        return carry

    lax.fori_loop(0, PEER_HEADS * n_chunks // per_step, body, 0)
    for c in range(n_chunks):
        rows = slice(c * LANES, (c + 1) * LANES)
        ids_ref[rows, :] = ids_s[c].T
        gts_ref[rows, :] = gts_s[c].T


def _peer_route(x2, g, shift, scale, wqt_bf, sub_keys, seq):
    t, d = x2.shape
    tt = 512
    per_b = seq // tt
    n_chunks = tt // LANES
    nq = wqt_bf.shape[0]
    out_blk = pl.BlockSpec((tt, PEER_SLOTS), lambda i: (i, 0))
    return pl.pallas_call(
        _peer_route_kernel,
        grid=(t // tt,),
        in_specs=[
            pl.BlockSpec((tt, d), lambda i: (i, 0)),
            pl.BlockSpec((1, d), lambda i: (0, 0)),
            pl.BlockSpec((1, 1, d), lambda i: (i // per_b, 0, 0)),
            pl.BlockSpec((1, 1, d), lambda i: (i // per_b, 0, 0)),
            pl.BlockSpec((nq, d), lambda i: (0, 0)),
            pl.BlockSpec(sub_keys.shape, lambda i: (0, 0, 0)),
        ],
        out_specs=[pl.BlockSpec((tt, d), lambda i: (i, 0)), out_blk, out_blk],
        out_shape=[jax.ShapeDtypeStruct((t, d), F32),
                   jax.ShapeDtypeStruct((t, PEER_SLOTS), I32),
                   jax.ShapeDtypeStruct((t, PEER_SLOTS), F32)],
        scratch_shapes=[pltpu.VMEM((n_chunks, nq, LANES), F32),
                        pltpu.VMEM((2 * ROUTE_CHUNKS_PER_STEP, PEER_TOPK, LANES), F32),
                        pltpu.VMEM((2 * ROUTE_CHUNKS_PER_STEP, PEER_TOPK, LANES), I32),
                        pltpu.VMEM((n_chunks, PEER_SLOTS, LANES), I32),
                        pltpu.VMEM((n_chunks, PEER_SLOTS, LANES), F32)],
        compiler_params=_cparams(1),
        name="peer_route",
    )(x2, g, shift, scale, wqt_bf, sub_keys)


def _pack_table(tab):
    e, d = tab.shape
    bits = lax.bitcast_convert_type(tab.astype(BF16), jnp.uint16).astype(U32)
    bits = bits.reshape(e, d // (2 * LANES), 2, LANES)
    words = bits[:, :, 0, :] | (bits[:, :, 1, :] << 16)
    return words.reshape(e * (d // (2 * LANES)), LANES)


def _table_spec(rows):
    return pl.BlockSpec((rows, LANES), lambda i: (0, 0), pipeline_mode=pl.Buffered(1))


def _gelu_exact(a):
    return 0.5 * a * (1.0 + lax.erf(a * (2.0 ** -0.5)))


FEAT_CHUNKS = 8
SLOT_WIDTH = PEER_SLOTS * FEAT_CHUNKS


def _gather_rows(ids_ref, base, tbl_ref, slot):
    for j in range(PEER_SLOTS):
        if j % ID_VIEW == 0:
            ids_part = ids_ref.at[pl.ds(base + j, ID_VIEW)]
        row0 = pl.multiple_of(ids_part[j % ID_VIEW], PACK_ROWS)
        slot[PACK_ROWS * j:PACK_ROWS * (j + 1), :] = tbl_ref[pl.ds(row0, PACK_ROWS), :]


def _pipelined_tokens(tt, ids_ref, tbl_ref, slots, compute):
    group = len(slots) // 2
    first, second = slots[:group], slots[group:]
    last = tt - 1

    def gather(t, slot):
        _gather_rows(ids_ref, jnp.minimum(t, last) * PEER_SLOTS, tbl_ref, slot)

    for k in range(group):
        gather(k, first[k])

    def step(q, carry):
        t = 2 * group * q
        for k in range(group):
            compute(t + k, first[k])
        for k in range(group):
            gather(t + group + k, second[k])
        for k in range(group):
            compute(t + group + k, second[k])
        for k in range(group):
            gather(t + 2 * group + k, first[k])
        return carry

    lax.fori_loop(0, tt // (2 * group), step, 0)


def _chunk_diag():
    return (_iota((FEAT_CHUNKS, SLOT_WIDTH), 1) % FEAT_CHUNKS) == _iota((FEAT_CHUNKS, SLOT_WIDTH), 0)


def _peer_u_kernel(ids_ref, h_ref, g_ref, tbl_ref, coef_ref, *scratch):
    slots, (hx_s, rs_s) = scratch[:N_SLOTS], scratch[N_SLOTS:]
    tt = h_ref.shape[0]
    for c in range(FEAT_CHUNKS):
        hx_s[pl.ds(c, tt, stride=FEAT_CHUNKS), :] = h_ref[:, c * LANES:(c + 1) * LANES]
    diag = _chunk_diag()

    def compute(t, slot):
        rows = pltpu.bitcast(slot[...], BF16)
        x8 = hx_s[pl.ds(pl.multiple_of(t * FEAT_CHUNKS, FEAT_CHUNKS), FEAT_CHUNKS), :]
        part = _dot_nt(x8.astype(BF16), rows)
        rs_s[pl.ds(t, 1), :] = jnp.sum(jnp.where(diag, part, 0.0), axis=0, keepdims=True)

    _pipelined_tokens(tt, ids_ref, tbl_ref, slots, compute)
    group = jnp.where(_iota((SLOT_WIDTH, PEER_SLOTS), 0) // FEAT_CHUNKS == _iota((SLOT_WIDTH, PEER_SLOTS), 1),
                      1.0, 0.0).astype(BF16)
    hi, lo = _split_bf16(rs_s[...])
    act = _dot(hi, group) + _dot(lo, group)
    coef_ref[...] = g_ref[...] * _gelu_exact(act)


N_SLOTS = 16
ID_VIEW = 16
PEER_TOKEN_TILE = 512


def _slot_scratch():
    return [pltpu.VMEM((PEER_SLOTS * PACK_ROWS, LANES), U32)] * N_SLOTS


def _peer_u(ids_flat, h, gates, table, tt):
    t, d = h.shape
    return pl.pallas_call(
        _peer_u_kernel,
        grid=(t // tt,),
        in_specs=[
            pl.BlockSpec((tt * PEER_SLOTS,), lambda i: (i,), memory_space=pltpu.SMEM),
            pl.BlockSpec((tt, d), lambda i: (i, 0)),
            pl.BlockSpec((tt, PEER_SLOTS), lambda i: (i, 0)),
            _table_spec(table.shape[0]),
        ],
        out_specs=pl.BlockSpec((tt, PEER_SLOTS), lambda i: (i, 0)),
        out_shape=jax.ShapeDtypeStruct((t, PEER_SLOTS), F32),
        scratch_shapes=_slot_scratch() + [
                        pltpu.VMEM((tt * FEAT_CHUNKS, LANES), F32),
                        pltpu.VMEM((tt, SLOT_WIDTH), F32)],
        compiler_params=_cparams(1, TABLE_VMEM_LIMIT),
        name="peer_expert_in",
    )(ids_flat, h, gates, table)


def _peer_v_kernel(ids_ref, coef_ref, x_ref, gate_ref, tbl_ref, o_ref, *scratch):
    slots, (ce_hi_s, ce_lo_s, res_s) = scratch[:N_SLOTS], scratch[N_SLOTS:]
    tt = x_ref.shape[0]
    spread = jnp.where(_iota((PEER_SLOTS, SLOT_WIDTH), 1) // FEAT_CHUNKS == _iota((PEER_SLOTS, SLOT_WIDTH), 0),
                       1.0, 0.0).astype(BF16)
    hi, lo = _split_bf16(coef_ref[...])
    ce_hi_s[...] = _dot(hi, spread)
    ce_lo_s[...] = _dot(lo, spread)
    diag = _chunk_diag()

    def compute(t, slot):
        rows = pltpu.bitcast(slot[...], BF16)
        a_hi = jnp.where(diag, ce_hi_s[pl.ds(t, 1), :], 0.0)
        a_lo = jnp.where(diag, ce_lo_s[pl.ds(t, 1), :], 0.0)
        both = _dot(jnp.concatenate([a_hi, a_lo], axis=0).astype(BF16), rows)
        r0 = pl.multiple_of(t * FEAT_CHUNKS, FEAT_CHUNKS)
        res_s[pl.ds(r0, FEAT_CHUNKS), :] = both[0:FEAT_CHUNKS, :] + both[FEAT_CHUNKS:2 * FEAT_CHUNKS, :]

    _pipelined_tokens(tt, ids_ref, tbl_ref, slots, compute)
    for c in range(FEAT_CHUNKS):
        cols = slice(c * LANES, (c + 1) * LANES)
        y = res_s[pl.ds(c, tt, stride=FEAT_CHUNKS), :]
        o_ref[:, cols] = x_ref[:, cols] + gate_ref[0][:, cols] * y


def _peer_v(ids_flat, coef, x2, gate, table, seq, tt):
    t, d = x2.shape
    per_b = seq // tt
    blk = pl.BlockSpec((tt, d), lambda i: (i, 0))
    return pl.pallas_call(
        _peer_v_kernel,
        grid=(t // tt,),
        in_specs=[
            pl.BlockSpec((tt * PEER_SLOTS,), lambda i: (i,), memory_space=pltpu.SMEM),
            pl.BlockSpec((tt, PEER_SLOTS), lambda i: (i, 0)),
            blk,
            pl.BlockSpec((1, 1, d), lambda i: (i // per_b, 0, 0)),
            _table_spec(table.shape[0]),
        ],
        out_specs=blk,
        out_shape=jax.ShapeDtypeStruct((t, d), F32),
        scratch_shapes=_slot_scratch() + [
                        pltpu.VMEM((tt, SLOT_WIDTH), F32), pltpu.VMEM((tt, SLOT_WIDTH), F32),
                        pltpu.VMEM((tt * FEAT_CHUNKS, LANES), F32)],
        compiler_params=_cparams(1, TABLE_VMEM_LIMIT),
        name="peer_expert_out",
    )(ids_flat, coef, x2, gate, table)


def _peer_ffn(x2, g, shift, scale, gate, wq, sub_keys, table_u, table_v, seq):
    t, d = x2.shape
    tt = PEER_TOKEN_TILE
    h, ids, gates = _peer_route(x2, g, shift, scale, wq.T.astype(BF16), sub_keys, seq)
    ids_flat = ids.reshape(t * PEER_SLOTS)
    coef = _peer_u(ids_flat, h, gates, table_u, tt)
    return _peer_v(ids_flat, coef, x2, gate, table_v, seq, tt)


def _rope_tables(seq):
    half = HEAD_DIM // 2
    inv_freq = ROPE_THETA ** (-jnp.arange(half, dtype=F32) / half)
    ang = jnp.arange(seq).astype(F32)[:, None] * inv_freq[None, :]
    reps = LANES // half
    return jnp.tile(jnp.cos(ang), (1, reps)), jnp.tile(jnp.sin(ang), (1, reps))


def _two_heads(gain):
    return jnp.tile(gain.reshape(1, HEAD_DIM), (1, LANES // HEAD_DIM))


def kernel(x, c, ada_w, ada_b, norm_mix_g, norm_ffn_g, w_in_ab, w_out_ab, sinks_a, qnorm_a, knorm_a,
           w_in_cd, w_out_cd, qnorm_c, knorm_c, qnorm_d, knorm_d, peer_wq, peer_subkeys, peer_u, peer_v):
    b, seq, d = x.shape
    depth = ada_w.shape[0]
    t = b * seq
    cos, sin = _rope_tables(seq)
    mod = _modulation(c, ada_w, ada_b)
    x2 = x.reshape(t, d)
    for layer in range(depth):
        shift_m, scale_m, gate_m, shift_f, scale_f, gate_f = [
            m.reshape(b, 1, d) for m in jnp.split(mod[layer], 6, axis=-1)]
        g_mix = norm_mix_g[layer].reshape(1, d)
        i = layer // 2
        if layer % 2 == 0:
            proj = _norm_proj(x2, g_mix, shift_m, scale_m, w_in_ab[i].astype(BF16), seq)
            proj = proj.reshape(b, seq, -1)
            ya = _swa_attention(proj, sinks_a[i], cos, sin, _two_heads(qnorm_a[i]), _two_heads(knorm_a[i]))
            b_col = (A_Q_HEADS + 2 * A_KV_HEADS) * HEAD_DIM // LANES
            yb = _stick_attention(proj, b_col)
            w_out = w_out_ab[i]
        else:
            proj = _norm_proj(x2, g_mix, shift_m, scale_m, w_in_cd[i].astype(BF16), seq)
            proj = proj.reshape(b, seq, -1)
            ya = _qkv_attention_call(_dilated_kernel, "dilated_attention", proj, 0, C_HEADS, cos, sin,
                                     _two_heads(qnorm_c[i]), _two_heads(knorm_c[i]))
            d_col = 3 * C_HEADS * HEAD_DIM // LANES
            yb = _qkv_attention_call(_moba_kernel, "moba_attention", proj, d_col, D_HEADS, cos, sin,
                                     _two_heads(qnorm_d[i]), _two_heads(knorm_d[i]),
                                     extra_scratch=(pltpu.VMEM((LANES, LANES), F32),
                                                    pltpu.VMEM((seq, LANES), F32),
                                                    pltpu.VMEM((seq, LANES), F32)))
            w_out = w_out_cd[i]
        x2 = _out_proj(x2, ya.reshape(t, -1), yb.reshape(t, -1), w_out.astype(BF16), gate_m, seq)
        x2 = _peer_ffn(x2, norm_ffn_g[layer].reshape(1, d), shift_f, scale_f, gate_f,
                       peer_wq[layer], peer_subkeys[layer],
                       _pack_table(peer_u[layer]), _pack_table(peer_v[layer]), seq)
    return x2.reshape(b, seq, d)
```

```python
import jax
import jax.numpy as jnp
from jax import lax
from jax.experimental import pallas as pl
from jax.experimental.pallas import tpu as pltpu

F32 = jnp.float32
BF16 = jnp.bfloat16
I32 = jnp.int32
U32 = jnp.uint32

HEAD_DIM = 64
ROPE_THETA = 10000.0
NORM_EPS = 1e-6
LANES = 128
QUERY_BLOCK = 128
A_Q_HEADS, A_KV_HEADS = 8, 2
B_HEADS = C_HEADS = D_HEADS = 8
C_PATTERNS = ((128, 1), (512, 4), (2048, 16))
MOBA_BLOCK, MOBA_TOPK = 256, 3
PEER_HEADS, PEER_N_KEYS, PEER_TOPK, PEER_D_KEY = 8, 128, 16, 256
PEER_SLOTS = PEER_HEADS * PEER_TOPK
NEG_BIG = -1e30
PEER_CAND_COUNTS = tuple(PEER_TOPK // (a + 1) for a in range(PEER_TOPK))
PACK_ROWS = 4
MIB = 1024 * 1024
V7X_VMEM_BYTES = 64 * MIB
STREAM_VMEM_LIMIT = 40 * MIB
TABLE_VMEM_LIMIT = V7X_VMEM_BYTES - 8 * MIB


def _cparams(n_axes, vmem_bytes=STREAM_VMEM_LIMIT):
    return pltpu.CompilerParams(
        dimension_semantics=("arbitrary",) * n_axes,
        vmem_limit_bytes=vmem_bytes)


def _split_bf16(a):
    hi = a.astype(BF16)
    lo = (a - hi.astype(F32)).astype(BF16)
    return hi, lo


def _dot(a, b):
    return jnp.dot(a, b, preferred_element_type=F32)


def _dot_nt(a, b):
    return lax.dot_general(a, b, (((1,), (1,)), ((), ())), preferred_element_type=F32)


def _dot3(a, b):
    ah, al = _split_bf16(a)
    bh, bl = _split_bf16(b)
    return _dot(ah, bh) + _dot(ah, bl) + _dot(al, bh)


def _dot3_nt(a, b):
    ah, al = _split_bf16(a)
    bh, bl = _split_bf16(b)
    return _dot_nt(ah, bh) + _dot_nt(ah, bl) + _dot_nt(al, bh)


def _iota(shape, dim):
    return lax.broadcasted_iota(I32, shape, dim)


def _mod_kernel(c_ref, w_ref, b_ref, o_ref):
    c = c_ref[...]
    cond = c * jax.nn.sigmoid(c)
    o_ref[0] = _dot3(cond, w_ref[0]) + b_ref[0]


def _modulation(c, ada_w, ada_b):
    depth, d, n = ada_w.shape
    b = c.shape[0]
    tn = 1024
    return pl.pallas_call(
        _mod_kernel,
        grid=(depth, n // tn),
        in_specs=[
            pl.BlockSpec((b, d), lambda l, j: (0, 0)),
            pl.BlockSpec((1, d, tn), lambda l, j: (l, 0, j)),
            pl.BlockSpec((1, 1, tn), lambda l, j: (l, 0, j)),
        ],
        out_specs=pl.BlockSpec((1, b, tn), lambda l, j: (l, 0, j)),
        out_shape=jax.ShapeDtypeStruct((depth, b, n), F32),
        compiler_params=_cparams(2),
        name="adaln_modulation",
    )(c, ada_w, ada_b.reshape(depth, 1, n))


def _adaln(x, g, shift, scale):
    ms = jnp.mean(x * x, axis=-1, keepdims=True)
    y = x * lax.rsqrt(ms + NORM_EPS) * g
    return y * (1.0 + scale) + shift


def _norm_proj_kernel(x_ref, g_ref, sh_ref, sc_ref, w_ref, o_ref):
    h = _adaln(x_ref[...], g_ref[...], sh_ref[0], sc_ref[0])
    o_ref[...] = _dot(h.astype(BF16), w_ref[...])


def _norm_proj(x2, g, shift, scale, w_bf, seq):
    t, d = x2.shape
    n = w_bf.shape[1]
    tt = 512
    per_b = seq // tt
    return pl.pallas_call(
        _norm_proj_kernel,
        grid=(t // tt,),
        in_specs=[
            pl.BlockSpec((tt, d), lambda i: (i, 0)),
            pl.BlockSpec((1, d), lambda i: (0, 0)),
            pl.BlockSpec((1, 1, d), lambda i: (i // per_b, 0, 0)),
            pl.BlockSpec((1, 1, d), lambda i: (i // per_b, 0, 0)),
            pl.BlockSpec((d, n), lambda i: (0, 0)),
        ],
        out_specs=pl.BlockSpec((tt, n), lambda i: (i, 0)),
        out_shape=jax.ShapeDtypeStruct((t, n), F32),
        compiler_params=_cparams(1),
        name="adaln_in_proj",
    )(x2, g, shift, scale, w_bf)


def _out_proj_kernel(x_ref, ya_ref, yb_ref, w_ref, gate_ref, o_ref):
    half = ya_ref.shape[1]
    y = _dot(ya_ref[...].astype(BF16), w_ref[0:half, :])
    y = y + _dot(yb_ref[...].astype(BF16), w_ref[half:2 * half, :])
    o_ref[...] = x_ref[...] + gate_ref[0] * y


def _out_proj(x2, ya, yb, w_bf, gate, seq):
    t, d = x2.shape
    half = ya.shape[1]
    tt = 512
    per_b = seq // tt
    return pl.pallas_call(
        _out_proj_kernel,
        grid=(t // tt,),
        in_specs=[
            pl.BlockSpec((tt, d), lambda i: (i, 0)),
            pl.BlockSpec((tt, half), lambda i: (i, 0)),
            pl.BlockSpec((tt, half), lambda i: (i, 0)),
            pl.BlockSpec((2 * half, d), lambda i: (0, 0)),
            pl.BlockSpec((1, 1, d), lambda i: (i // per_b, 0, 0)),
        ],
        out_specs=pl.BlockSpec((tt, d), lambda i: (i, 0)),
        out_shape=jax.ShapeDtypeStruct((t, d), F32),
        compiler_params=_cparams(1),
        name="mixer_out_proj",
    )(x2, ya, yb, w_bf, gate)


def _lane_row():
    return _iota((1, LANES), 1)


def _head_segment_ones():
    r = _iota((LANES, LANES), 0) // HEAD_DIM
    c = _iota((LANES, LANES), 1) // HEAD_DIM
    return jnp.where(r == c, 1.0, 0.0).astype(BF16)


def _headnorm_rope(a, g, cos, sin):
    hi, lo = _split_bf16(a * a)
    seg = _head_segment_ones()
    ms = (_dot(hi, seg) + _dot(lo, seg)) * (1.0 / HEAD_DIM)
    y = a * lax.rsqrt(ms + NORM_EPS) * g
    half = HEAD_DIM // 2
    upper = pltpu.roll(y, LANES - half, axis=1)
    lower = pltpu.roll(y, half, axis=1)
    first_half = (_lane_row() % HEAD_DIM) < half
    rot = jnp.where(first_half, -upper, lower)
    return y * cos + rot * sin


def _head_masks():
    lane = _lane_row()
    return lane < HEAD_DIM, lane >= HEAD_DIM


def _attn_specs(seq, qcol, kcol, vcol, kv_shared):
    blk = (1, seq, LANES)
    q_spec = pl.BlockSpec(blk, lambda b, p: (b, 0, qcol + p))
    if kv_shared:
        k_spec = pl.BlockSpec(blk, lambda b, p: (b, 0, kcol))
        v_spec = pl.BlockSpec(blk, lambda b, p: (b, 0, vcol))
    else:
        k_spec = pl.BlockSpec(blk, lambda b, p: (b, 0, kcol + p))
        v_spec = pl.BlockSpec(blk, lambda b, p: (b, 0, vcol + p))
    return q_spec, k_spec, v_spec


def _row_spec(seq):
    return pl.BlockSpec((seq, LANES), lambda b, p: (0, 0))


def _gain_spec():
    return pl.BlockSpec((1, LANES), lambda b, p: (0, 0))


def _store_heads(o_ref, r0, outs):
    first, _ = _head_masks()
    o_ref[0, pl.ds(r0, QUERY_BLOCK), :] = jnp.where(first, outs[0], outs[1])


SWA_QBLOCKS_PER_STEP = 4


def _swa_kernel(sinks_ref, q_ref, k_ref, v_ref, cos_ref, sin_ref, gq_ref, gk_ref, o_ref,
                q0_s, q1_s, k_s, v_s):
    p = pl.program_id(1)
    seq = q_ref.shape[1]
    cos, sin = cos_ref[...], sin_ref[...]
    first, second = _head_masks()
    qn = _headnorm_rope(q_ref[0], gq_ref[...], cos, sin) * (HEAD_DIM ** -0.5)
    q0_s[...] = jnp.where(first, qn, 0.0).astype(BF16)
    q1_s[...] = jnp.where(second, qn, 0.0).astype(BF16)
    pairs_per_kv = (A_Q_HEADS // A_KV_HEADS) // 2
    keep = jnp.logical_xor(first, (p // pairs_per_kv) == 1)
    kn = _headnorm_rope(k_ref[0], gk_ref[...], cos, sin)
    k_s[...] = jnp.where(keep, kn, pltpu.roll(kn, HEAD_DIM, axis=1)).astype(BF16)
    v = v_ref[0]
    v_s[...] = jnp.where(keep, v, pltpu.roll(v, HEAD_DIM, axis=1)).astype(BF16)

    qi = _iota((QUERY_BLOCK, QUERY_BLOCK), 0)
    ki = _iota((QUERY_BLOCK, QUERY_BLOCK), 1)

    mask_c = ki <= qi
    per_step = min(SWA_QBLOCKS_PER_STEP, seq // QUERY_BLOCK)

    def qstep(step, carry):
        chains, scores = [], []
        for u in range(per_step):
            i = step * per_step + u
            r0 = pl.multiple_of(i * QUERY_BLOCK, QUERY_BLOCK)
            rp = pl.multiple_of(jnp.maximum(i - 1, 0) * QUERY_BLOCK, QUERY_BLOCK)
            kc, kp = k_s[pl.ds(r0, QUERY_BLOCK), :], k_s[pl.ds(rp, QUERY_BLOCK), :]
            vc, vp = v_s[pl.ds(r0, QUERY_BLOCK), :], v_s[pl.ds(rp, QUERY_BLOCK), :]
            mask_p = jnp.logical_and(ki > qi, i > 0)
            chains.append((r0, vc, vp, mask_p))
            for q_s in (q0_s, q1_s):
                qh = q_s[pl.ds(r0, QUERY_BLOCK), :]
                scores.append((_dot_nt(qh, kc), _dot_nt(qh, kp)))
        probs, denoms = [], []
        for n, (s_cur, s_prev) in enumerate(scores):
            mask_p = chains[n // 2][3]
            sc = jnp.where(mask_c, s_cur, NEG_BIG)
            sp = jnp.where(mask_p, s_prev, NEG_BIG)
            sink = sinks_ref[2 * p + n % 2]
            m = jnp.maximum(jnp.max(sc, axis=1, keepdims=True), jnp.max(sp, axis=1, keepdims=True))
            m = jnp.maximum(m, sink)
            ec, ep = jnp.exp(sc - m), jnp.exp(sp - m)
            denoms.append(jnp.sum(ec, axis=1, keepdims=True) + jnp.sum(ep, axis=1, keepdims=True)
                          + jnp.exp(sink - m))
            probs.append((ec.astype(BF16), ep.astype(BF16)))
        for u, (r0, vc, vp, _) in enumerate(chains):
            outs = [(_dot(probs[2 * u + hh][0], vc) + _dot(probs[2 * u + hh][1], vp)) / denoms[2 * u + hh]
                    for hh in range(2)]
            _store_heads(o_ref, r0, outs)
        return carry

    lax.fori_loop(0, seq // (QUERY_BLOCK * per_step), qstep, 0)


def _swa_attention(proj, sinks, cos, sin, gq, gk):
    b, seq, _ = proj.shape
    n_pairs = A_Q_HEADS // 2
    kcol = A_Q_HEADS * HEAD_DIM // LANES
    vcol = kcol + A_KV_HEADS * HEAD_DIM // LANES
    q_spec, k_spec, v_spec = _attn_specs(seq, 0, kcol, vcol, True)
    return pl.pallas_call(
        _swa_kernel,
        grid=(b, n_pairs),
        in_specs=[pl.BlockSpec(memory_space=pltpu.SMEM), q_spec, k_spec, v_spec,
                  _row_spec(seq), _row_spec(seq), _gain_spec(), _gain_spec()],
        out_specs=pl.BlockSpec((1, seq, LANES), lambda b_, p: (b_, 0, p)),
        out_shape=jax.ShapeDtypeStruct((b, seq, n_pairs * LANES), F32),
        scratch_shapes=[pltpu.VMEM((seq, LANES), BF16)] * 4,
        compiler_params=_cparams(2),
        name="swa_gqa_attention",
    )(sinks, proj, proj, proj, cos, sin, gq, gk)


STICK_GROUP = 4
STICK_QUERY_ROWS = 256


def _stick_kernel(q_ref, k_ref, v_ref, o_ref, k_s, v_s):
    seq = q_ref.shape[1]
    k_s[...] = k_ref[0].astype(BF16)
    v_s[...] = v_ref[0].astype(BF16)
    first, second = _head_masks()
    qr = min(STICK_QUERY_ROWS, seq)
    kw = QUERY_BLOCK
    qi = _iota((qr, kw), 0)
    ki = _iota((qr, kw), 1)
    wr = _iota((2 * kw, 2 * kw), 0) % kw
    wc = _iota((2 * kw, 2 * kw), 1)
    suffix_w = jnp.where(jnp.logical_or(wc >= kw, wr > wc), 1.0, 0.0).astype(BF16)

    def qblock(i, carry):
        r0 = pl.multiple_of(i * qr, qr)
        q = q_ref[0, pl.ds(r0, qr), :] * (HEAD_DIM ** -0.5)
        qhs = [jnp.where(msk, q, 0.0).astype(BF16) for msk in (first, second)]
        n_blocks = (r0 + qr) // kw

        def kgroup(g, st):
            accs, laters = [st[0], st[1]], [st[2], st[3]]
            chains = [(u, hh) for u in range(STICK_GROUP) for hh in range(2)]
            vbs, pasts, zs = [], [], {}
            for u in range(STICK_GROUP):
                j = n_blocks - 1 - (g * STICK_GROUP + u)
                live = j >= 0
                c0 = pl.multiple_of(jnp.maximum(j, 0) * kw, kw)
                kb = k_s[pl.ds(c0, kw), :]
                vbs.append(v_s[pl.ds(c0, kw), :])
                pasts.append(jnp.logical_and((c0 + ki) < (r0 + qi), live))
                for hh in range(2):
                    zs[u, hh] = _dot_nt(qhs[hh], kb)
            logit, sums = {}, {}
            for u, hh in chains:
                z = zs[u, hh]
                sp = jnp.maximum(z, 0.0) + jnp.log(1.0 + jnp.exp(-jnp.abs(z)))
                log_keep = jnp.where(pasts[u], -sp, 0.0)
                logit[u, hh] = z - sp
                hi, lo = _split_bf16(log_keep)
                sums[u, hh] = _dot(jnp.concatenate([hi, lo], axis=1), suffix_w)
            ws = {}
            for u, hh in chains:
                inner, total = sums[u, hh][:, :kw], sums[u, hh][:, kw:]
                ws[u, hh] = jnp.where(pasts[u], jnp.exp(logit[u, hh] + inner + laters[hh]), 0.0).astype(BF16)
                laters[hh] = laters[hh] + total
            for u, hh in chains:
                accs[hh] = accs[hh] + _dot(ws[u, hh], vbs[u])
            return accs[0], accs[1], laters[0], laters[1]

        zero = jnp.zeros((qr, LANES), F32)
        n_groups = (n_blocks + STICK_GROUP - 1) // STICK_GROUP
        st = lax.fori_loop(0, n_groups, kgroup, (zero, zero, zero, zero))
        o_ref[0, pl.ds(r0, qr), :] = jnp.where(first, st[0], st[1])
        return carry

    lax.fori_loop(0, seq // qr, qblock, 0)


def _stick_attention(proj, qcol):
    b, seq, _ = proj.shape
    n_pairs = B_HEADS // 2
    q_spec, k_spec, v_spec = _attn_specs(seq, qcol, qcol + n_pairs, qcol + 2 * n_pairs, False)
    return pl.pallas_call(
        _stick_kernel,
        grid=(b, n_pairs),
        in_specs=[q_spec, k_spec, v_spec],
        out_specs=pl.BlockSpec((1, seq, LANES), lambda b_, p: (b_, 0, p)),
        out_shape=jax.ShapeDtypeStruct((b, seq, n_pairs * LANES), F32),
        scratch_shapes=[pltpu.VMEM((seq, LANES), BF16)] * 2,
        compiler_params=_cparams(2),
        name="stick_breaking_attention",
    )(proj, proj, proj)


def _prep_qkv(q_ref, k_ref, v_ref, cos_ref, sin_ref, gq_ref, gk_ref, q0_s, q1_s, k_s, v_s):
    cos, sin = cos_ref[...], sin_ref[...]
    first, second = _head_masks()
    qn = _headnorm_rope(q_ref[0], gq_ref[...], cos, sin) * (HEAD_DIM ** -0.5)
    q0_s[...] = jnp.where(first, qn, 0.0).astype(BF16)
    q1_s[...] = jnp.where(second, qn, 0.0).astype(BF16)
    kn = _headnorm_rope(k_ref[0], gk_ref[...], cos, sin)
    k_s[...] = kn.astype(BF16)
    v_s[...] = v_ref[0].astype(BF16)
    return qn, kn


DILATED_KEY_TILE = 512
DILATED_QUERY_ROWS = 256


def _dilated_kernel(q_ref, k_ref, v_ref, cos_ref, sin_ref, gq_ref, gk_ref, o_ref,
                    q0_s, q1_s, k_s, v_s):
    seq = q_ref.shape[1]
    _prep_qkv(q_ref, k_ref, v_ref, cos_ref, sin_ref, gq_ref, gk_ref, q0_s, q1_s, k_s, v_s)
    kt = min(DILATED_KEY_TILE, seq)
    qr = min(DILATED_QUERY_ROWS, seq)
    qk = _iota((qr, kt), 0) - _iota((qr, kt), 1)
    on_stride = [jnp.where((qk & (dil - 1)) == 0, 1.0, 0.0) for _, dil in C_PATTERNS]
    first_head, _ = _head_masks()

    def qblock(i, carry):
        r0 = pl.multiple_of(i * qr, qr)
        qhs = [q_s[pl.ds(r0, qr), :] for q_s in (q0_s, q1_s)]

        def ktile(g, st):
            c0 = pl.multiple_of(g * kt, kt)
            d = (r0 - c0) + qk
            count = jnp.zeros(d.shape, F32)
            for (window, _), stride_ok in zip(C_PATTERNS, on_stride):
                count = count + jnp.where(d <= window, stride_ok, 0.0)
            count = jnp.where(d >= 0, count, 0.0)
            kb, vb = k_s[pl.ds(c0, kt), :], v_s[pl.ds(c0, kt), :]
            scores = [_dot_nt(qhs[hh], kb) for hh in range(2)]
            new, prs = [], []
            for hh in range(2):
                m, l, acc = st[3 * hh:3 * hh + 3]
                s = jnp.where(count > 0.0, scores[hh], NEG_BIG)
                m_new = jnp.maximum(m, jnp.max(s, axis=1, keepdims=True))
                pr = count * jnp.exp(s - m_new)
                alpha = jnp.exp(m - m_new)
                new += [m_new, alpha * l + jnp.sum(pr, axis=1, keepdims=True), alpha * acc]
                prs.append(pr.astype(BF16))
            for hh in range(2):
                new[3 * hh + 2] = new[3 * hh + 2] + _dot(prs[hh], vb)
            return tuple(new)

        init = (jnp.full((qr, 1), NEG_BIG, F32), jnp.zeros((qr, 1), F32),
                jnp.zeros((qr, LANES), F32)) * 2
        st = lax.fori_loop(0, (r0 + qr + kt - 1) // kt, ktile, init)
        o_ref[0, pl.ds(r0, qr), :] = jnp.where(first_head, st[2] / st[1], st[5] / st[4])
        return carry

    lax.fori_loop(0, seq // qr, qblock, 0)


def _qkv_attention_call(kernel, name, proj, qcol, n_heads, cos, sin, gq, gk, extra_scratch=()):
    b, seq, _ = proj.shape
    n_pairs = n_heads // 2
    q_spec, k_spec, v_spec = _attn_specs(seq, qcol, qcol + n_pairs, qcol + 2 * n_pairs, False)
    return pl.pallas_call(
        kernel,
        grid=(b, n_pairs),
        in_specs=[q_spec, k_spec, v_spec, _row_spec(seq), _row_spec(seq), _gain_spec(), _gain_spec()],
        out_specs=pl.BlockSpec((1, seq, LANES), lambda b_, p: (b_, 0, p)),
        out_shape=jax.ShapeDtypeStruct((b, seq, n_pairs * LANES), F32),
        scratch_shapes=[pltpu.VMEM((seq, LANES), BF16)] * 4 + list(extra_scratch),
        compiler_params=_cparams(2),
        name=name,
    )(proj, proj, proj, cos, sin, gq, gk)


def _moba_kernel(q_ref, k_ref, v_ref, cos_ref, sin_ref, gq_ref, gk_ref, o_ref,
                 q0_s, q1_s, k_s, v_s, km_s, sel0_s, sel1_s):
    seq = q_ref.shape[1]
    n_blocks = seq // MOBA_BLOCK
    qn, kn = _prep_qkv(q_ref, k_ref, v_ref, cos_ref, sin_ref, gq_ref, gk_ref, q0_s, q1_s, k_s, v_s)
    km_s[...] = jnp.zeros(km_s.shape, F32)
    km_s[0:n_blocks, :] = jnp.mean(kn.reshape(n_blocks, MOBA_BLOCK, LANES), axis=1)
    first, second = _head_masks()

    rows8 = _iota((8, seq), 0)
    own8 = _iota((8, seq), 1) // MOBA_BLOCK
    valid = rows8 < own8
    for msk, sel_s in ((first, sel0_s), (second, sel1_s)):
        gate = _dot3_nt(km_s[...], jnp.where(msk, qn, 0.0))[0:8, :]
        gm = jnp.where(valid, gate, -jnp.inf)
        rank = jnp.zeros((8, seq), F32)
        for n2 in range(n_blocks):
            g2 = gm[n2:n2 + 1, :]
            beats = jnp.logical_or(g2 > gm, jnp.logical_and(g2 == gm, n2 < rows8))
            rank = rank + jnp.where(jnp.logical_and(beats, n2 < own8), 1.0, 0.0)
        sel = jnp.where(jnp.logical_and(valid, rank < float(MOBA_TOPK)), 1.0, 0.0)
        sel = jnp.concatenate([sel, jnp.zeros((LANES - 8, seq), F32)], axis=0)
        sel_s[...] = sel.T

    qrows = MOBA_BLOCK
    lane_sq = _iota((qrows, LANES), 1)
    causal = _iota((qrows, MOBA_BLOCK), 1) <= _iota((qrows, MOBA_BLOCK), 0)
    second_block = _iota((qrows, 2 * MOBA_BLOCK), 1) >= MOBA_BLOCK
    first_head, _ = _head_masks()

    def qblock(own, carry):
        r0 = pl.multiple_of(own * qrows, qrows)
        qhs = [q_s[pl.ds(r0, qrows), :] for q_s in (q0_s, q1_s)]
        sels = [sel_s[pl.ds(r0, qrows), :] for sel_s in (sel0_s, sel1_s)]
        kb, vb = k_s[pl.ds(r0, MOBA_BLOCK), :], v_s[pl.ds(r0, MOBA_BLOCK), :]
        scores = [_dot_nt(qhs[hh], kb) for hh in range(2)]
        init, prs = [], []
        for hh in range(2):
            s = jnp.where(causal, scores[hh], NEG_BIG)
            m = jnp.max(s, axis=1, keepdims=True)
            pr = jnp.exp(s - m)
            init += [m, jnp.sum(pr, axis=1, keepdims=True), None]
            prs.append(pr.astype(BF16))
        for hh in range(2):
            init[3 * hh + 2] = _dot(prs[hh], vb)

        def kpair(g, st):
            c0 = pl.multiple_of(g * 2 * MOBA_BLOCK, 2 * MOBA_BLOCK)
            kb2, vb2 = k_s[pl.ds(c0, 2 * MOBA_BLOCK), :], v_s[pl.ds(c0, 2 * MOBA_BLOCK), :]
            scores = [_dot_nt(qhs[hh], kb2) for hh in range(2)]
            new, prs = [], []
            for hh in range(2):
                m, l, acc = st[3 * hh:3 * hh + 3]
                sel_a = jnp.sum(jnp.where(lane_sq == 2 * g, sels[hh], 0.0), axis=1, keepdims=True)
                sel_b = jnp.sum(jnp.where(lane_sq == 2 * g + 1, sels[hh], 0.0), axis=1, keepdims=True)
                keep = jnp.where(second_block, sel_b, sel_a) > 0.0
                s = jnp.where(keep, scores[hh], NEG_BIG)
                m_new = jnp.maximum(m, jnp.max(s, axis=1, keepdims=True))
                pr = jnp.exp(s - m_new)
                alpha = jnp.exp(m - m_new)
                new += [m_new, alpha * l + jnp.sum(pr, axis=1, keepdims=True), alpha * acc]
                prs.append(pr.astype(BF16))
            for hh in range(2):
                new[3 * hh + 2] = new[3 * hh + 2] + _dot(prs[hh], vb2)
            return tuple(new)

        st = lax.fori_loop(0, (own + 1) // 2, kpair, tuple(init))
        o_ref[0, pl.ds(r0, qrows), :] = jnp.where(first_head, st[2] / st[1], st[5] / st[4])
        return carry

    lax.fori_loop(0, seq // qrows, qblock, 0)


ROUTE_CHUNKS_PER_STEP = 4


def _oddeven_merge_sort_pairs(n):
    pairs = []

    def merge(lo, hi, r):
        step = 2 * r
        if step < hi - lo:
            merge(lo, hi, step)
            merge(lo + r, hi, step)
            pairs.extend((i, i + r) for i in range(lo + r, hi - r, step))
        else:
            pairs.append((lo, lo + r))

    def sort(lo, hi):
        if hi - lo >= 1:
            mid = lo + (hi - lo) // 2
            sort(lo, mid)
            sort(mid + 1, hi)
            merge(lo, hi, 1)

    sort(0, n - 1)
    return pairs


SUBLANES = 8


def _top16_rows(scores, n_rows, vals_refs, idx_refs):
    n_slabs = n_rows // SUBLANES
    sub = _iota((SUBLANES, LANES), 0)
    vals = [[s[SUBLANES * v:SUBLANES * (v + 1), :] for v in range(n_slabs)] for s in scores]
    idxs = [[sub + SUBLANES * v for v in range(n_slabs)] for _ in scores]
    for i, j in _oddeven_merge_sort_pairs(n_slabs):
        for va, ia in zip(vals, idxs):
            a, b = va[i], va[j]
            a_first = jnp.logical_or(a > b, jnp.logical_and(a == b, ia[i] < ia[j]))
            va[i], va[j] = jnp.maximum(a, b), jnp.minimum(a, b)
            ia[i], ia[j] = jnp.where(a_first, ia[i], ia[j]), jnp.where(a_first, ia[j], ia[i])
    for it in range(PEER_TOPK):
        for k, (va, ia) in enumerate(zip(vals, idxs)):
            m = jnp.max(va[0], axis=0, keepdims=True)
            pick = jnp.min(jnp.where(va[0] == m, ia[0], n_rows), axis=0, keepdims=True)
            vals_refs[k][it:it + 1, :] = m
            idx_refs[k][it:it + 1, :] = pick
            win = ia[0] == pick
            depth = PEER_TOPK - 1 - it
            for d in range(min(depth, n_slabs - 1)):
                va[d] = jnp.where(win, va[d + 1], va[d])
                ia[d] = jnp.where(win, ia[d + 1], ia[d])
            if depth >= n_slabs:
                va[n_slabs - 1] = jnp.where(win, -jnp.inf, va[n_slabs - 1])


def _peer_route_kernel(x_ref, g_ref, sh_ref, sc_ref, wqt_ref, sk_ref, h_ref, ids_ref, gts_ref,
                       q_s, val_s, idx_s, ids_s, gts_s):
    tt = x_ref.shape[0]
    n_chunks = tt // LANES
    per_step = ROUTE_CHUNKS_PER_STEP
    half = PEER_D_KEY // 2
    h = _adaln(x_ref[...], g_ref[...], sh_ref[0], sc_ref[0])
    h_ref[...] = h
    qt = _dot_nt(wqt_ref[...], h.astype(BF16))
    for c in range(n_chunks):
        q_s[c] = qt[:, c * LANES:(c + 1) * LANES]
    sk1, sk2 = sk_ref[0], sk_ref[1]
    col_id = _iota((PEER_TOPK, LANES), 0)
    sub = _iota((SUBLANES, LANES), 0)
    col_depth = jnp.zeros((SUBLANES, LANES), I32)
    for a in range(SUBLANES):
        col_depth = jnp.where(sub == a, PEER_CAND_COUNTS[a], col_depth)
    vals = [val_s.at[i] for i in range(2 * per_step)]
    idxs = [idx_s.at[i] for i in range(2 * per_step)]

    def body(step, carry):
        hh = step // (n_chunks // per_step)
        c0 = (step % (n_chunks // per_step)) * per_step
        q0 = pl.multiple_of(hh * PEER_D_KEY, PEER_D_KEY)
        scores = []
        for k in range(per_step):
            scores.append(_dot3(sk1, q_s[c0 + k, pl.ds(q0, half), :]))
            scores.append(_dot3(sk2, q_s[c0 + k, pl.ds(q0 + half, half), :]))
        _top16_rows(scores, PEER_N_KEYS, vals, idxs)
        state = []
        for k in range(per_step):
            v1, i1, v2, i2 = vals[2 * k], idxs[2 * k], vals[2 * k + 1], idxs[2 * k + 1]
            v1x, e1x = v1[0:SUBLANES, :], i1[0:SUBLANES, :] * PEER_N_KEYS
            xs = [jnp.where(col_depth > b, v1x + v2[b:b + 1, :], -jnp.inf) for b in range(PEER_TOPK)]
            ex = [e1x + i2[b:b + 1, :] for b in range(PEER_TOPK)]
            y = v1[SUBLANES:PEER_TOPK, :] + v2[0:1, :]
            ey = i1[SUBLANES:PEER_TOPK, :] * PEER_N_KEYS + i2[0:1, :]
            state.append([xs, ex, y, ey])
        for it in range(PEER_TOPK):
            for k in range(per_step):
                xs, ex, y, ey = state[k]
                heads = jnp.concatenate([xs[0], y], axis=0)
                m = jnp.max(heads, axis=0, keepdims=True)
                pick = jnp.min(jnp.where(heads == m, col_id, PEER_TOPK), axis=0, keepdims=True)
                win = col_id == pick
                eids = jnp.where(win, jnp.concatenate([ex[0], ey], axis=0), 0)
                idxs[2 * k][it:it + 1, :] = jnp.sum(eids, axis=0, keepdims=True)
                vals[2 * k][it:it + 1, :] = m
                win_x, win_y = win[0:SUBLANES, :], win[SUBLANES:PEER_TOPK, :]
                for dd in range(PEER_TOPK - 1 - it):
                    xs[dd] = jnp.where(win_x, xs[dd + 1], xs[dd])
                    ex[dd] = jnp.where(win_x, ex[dd + 1], ex[dd])
                state[k][2] = jnp.where(win_y, -jnp.inf, y)
        r0 = pl.multiple_of(hh * PEER_TOPK, PEER_TOPK)
        for k in range(per_step):
            top = vals[2 * k][...]
            e = jnp.exp(top - top[0:1, :])
            gts_s[c0 + k, pl.ds(r0, PEER_TOPK), :] = e / jnp.sum(e, axis=0, keepdims=True)
            ids_s[c0 + k, pl.ds(r0, PEER_TOPK), :] = idxs[2 * k][...] * PACK_ROWS
        return carry

    lax.fori_loop(0, PEER_HEADS * n_chunks // per_step, body, 0)
    for c in range(n_chunks):
        rows = slice(c * LANES, (c + 1) * LANES)
        ids_ref[rows, :] = ids_s[c].T
        gts_ref[rows, :] = gts_s[c].T


def _peer_route(x2, g, shift, scale, wqt_bf, sub_keys, seq):
    t, d = x2.shape
    tt = 512
    per_b = seq // tt
    n_chunks = tt // LANES
    nq = wqt_bf.shape[0]
    out_blk = pl.BlockSpec((tt, PEER_SLOTS), lambda i: (i, 0))
    return pl.pallas_call(
        _peer_route_kernel,
        grid=(t // tt,),
        in_specs=[
            pl.BlockSpec((tt, d), lambda i: (i, 0)),
            pl.BlockSpec((1, d), lambda i: (0, 0)),
            pl.BlockSpec((1, 1, d), lambda i: (i // per_b, 0, 0)),
            pl.BlockSpec((1, 1, d), lambda i: (i // per_b, 0, 0)),
            pl.BlockSpec((nq, d), lambda i: (0, 0)),
            pl.BlockSpec(sub_keys.shape, lambda i: (0, 0, 0)),
        ],
        out_specs=[pl.BlockSpec((tt, d), lambda i: (i, 0)), out_blk, out_blk],
        out_shape=[jax.ShapeDtypeStruct((t, d), F32),
                   jax.ShapeDtypeStruct((t, PEER_SLOTS), I32),
                   jax.ShapeDtypeStruct((t, PEER_SLOTS), F32)],
        scratch_shapes=[pltpu.VMEM((n_chunks, nq, LANES), F32),
                        pltpu.VMEM((2 * ROUTE_CHUNKS_PER_STEP, PEER_TOPK, LANES), F32),
                        pltpu.VMEM((2 * ROUTE_CHUNKS_PER_STEP, PEER_TOPK, LANES), I32),
                        pltpu.VMEM((n_chunks, PEER_SLOTS, LANES), I32),
                        pltpu.VMEM((n_chunks, PEER_SLOTS, LANES), F32)],
        compiler_params=_cparams(1),
        name="peer_route",
    )(x2, g, shift, scale, wqt_bf, sub_keys)


def _pack_table(tab):
    e, d = tab.shape
    bits = lax.bitcast_convert_type(tab.astype(BF16), jnp.uint16).astype(U32)
    bits = bits.reshape(e, d // (2 * LANES), 2, LANES)
    words = bits[:, :, 0, :] | (bits[:, :, 1, :] << 16)
    return words.reshape(e * (d // (2 * LANES)), LANES)


def _table_spec(rows):
    return pl.BlockSpec((rows, LANES), lambda i: (0, 0), pipeline_mode=pl.Buffered(1))


def _gelu_exact(a):
    return 0.5 * a * (1.0 + lax.erf(a * (2.0 ** -0.5)))


FEAT_CHUNKS = 8
SLOT_WIDTH = PEER_SLOTS * FEAT_CHUNKS


def _gather_rows(ids_ref, base, tbl_ref, slot):
    for j in range(PEER_SLOTS):
        if j % ID_VIEW == 0:
            ids_part = ids_ref.at[pl.ds(base + j, ID_VIEW)]
        row0 = pl.multiple_of(ids_part[j % ID_VIEW], PACK_ROWS)
        slot[PACK_ROWS * j:PACK_ROWS * (j + 1), :] = tbl_ref[pl.ds(row0, PACK_ROWS), :]


def _pipelined_tokens(tt, ids_ref, tbl_ref, slots, compute):
    group = len(slots) // 2
    first, second = slots[:group], slots[group:]
    last = tt - 1

    def gather(t, slot):
        _gather_rows(ids_ref, jnp.minimum(t, last) * PEER_SLOTS, tbl_ref, slot)

    for k in range(group):
        gather(k, first[k])

    def step(q, carry):
        t = 2 * group * q
        for k in range(group):
            compute(t + k, first[k])
        for k in range(group):
            gather(t + group + k, second[k])
        for k in range(group):
            compute(t + group + k, second[k])
        for k in range(group):
            gather(t + 2 * group + k, first[k])
        return carry

    lax.fori_loop(0, tt // (2 * group), step, 0)


def _chunk_diag():
    return (_iota((FEAT_CHUNKS, SLOT_WIDTH), 1) % FEAT_CHUNKS) == _iota((FEAT_CHUNKS, SLOT_WIDTH), 0)


def _peer_u_kernel(ids_ref, h_ref, g_ref, tbl_ref, coef_ref, *scratch):
    slots, (hx_s, rs_s) = scratch[:N_SLOTS], scratch[N_SLOTS:]
    tt = h_ref.shape[0]
    for c in range(FEAT_CHUNKS):
        hx_s[pl.ds(c, tt, stride=FEAT_CHUNKS), :] = h_ref[:, c * LANES:(c + 1) * LANES]
    diag = _chunk_diag()

    def compute(t, slot):
        rows = pltpu.bitcast(slot[...], BF16)
        x8 = hx_s[pl.ds(pl.multiple_of(t * FEAT_CHUNKS, FEAT_CHUNKS), FEAT_CHUNKS), :]
        part = _dot_nt(x8.astype(BF16), rows)
        rs_s[pl.ds(t, 1), :] = jnp.sum(jnp.where(diag, part, 0.0), axis=0, keepdims=True)

    _pipelined_tokens(tt, ids_ref, tbl_ref, slots, compute)
    group = jnp.where(_iota((SLOT_WIDTH, PEER_SLOTS), 0) // FEAT_CHUNKS == _iota((SLOT_WIDTH, PEER_SLOTS), 1),
                      1.0, 0.0).astype(BF16)
    hi, lo = _split_bf16(rs_s[...])
    act = _dot(hi, group) + _dot(lo, group)
    coef_ref[...] = g_ref[...] * _gelu_exact(act)


N_SLOTS = 16
ID_VIEW = 16
PEER_TOKEN_TILE = 512


def _slot_scratch():
    return [pltpu.VMEM((PEER_SLOTS * PACK_ROWS, LANES), U32)] * N_SLOTS


def _peer_u(ids_flat, h, gates, table, tt):
    t, d = h.shape
    return pl.pallas_call(
        _peer_u_kernel,
        grid=(t // tt,),
        in_specs=[
            pl.BlockSpec((tt * PEER_SLOTS,), lambda i: (i,), memory_space=pltpu.SMEM),
            pl.BlockSpec((tt, d), lambda i: (i, 0)),
            pl.BlockSpec((tt, PEER_SLOTS), lambda i: (i, 0)),
            _table_spec(table.shape[0]),
        ],
        out_specs=pl.BlockSpec((tt, PEER_SLOTS), lambda i: (i, 0)),
        out_shape=jax.ShapeDtypeStruct((t, PEER_SLOTS), F32),
        scratch_shapes=_slot_scratch() + [
                        pltpu.VMEM((tt * FEAT_CHUNKS, LANES), F32),
                        pltpu.VMEM((tt, SLOT_WIDTH), F32)],
        compiler_params=_cparams(1, TABLE_VMEM_LIMIT),
        name="peer_expert_in",
    )(ids_flat, h, gates, table)


def _peer_v_kernel(ids_ref, coef_ref, x_ref, gate_ref, tbl_ref, o_ref, *scratch):
    slots, (ce_hi_s, ce_lo_s, res_s) = scratch[:N_SLOTS], scratch[N_SLOTS:]
    tt = x_ref.shape[0]
    spread = jnp.where(_iota((PEER_SLOTS, SLOT_WIDTH), 1) // FEAT_CHUNKS == _iota((PEER_SLOTS, SLOT_WIDTH), 0),
                       1.0, 0.0).astype(BF16)
    hi, lo = _split_bf16(coef_ref[...])
    ce_hi_s[...] = _dot(hi, spread)
    ce_lo_s[...] = _dot(lo, spread)
    diag = _chunk_diag()

    def compute(t, slot):
        rows = pltpu.bitcast(slot[...], BF16)
        a_hi = jnp.where(diag, ce_hi_s[pl.ds(t, 1), :], 0.0)
        a_lo = jnp.where(diag, ce_lo_s[pl.ds(t, 1), :], 0.0)
        both = _dot(jnp.concatenate([a_hi, a_lo], axis=0).astype(BF16), rows)
        r0 = pl.multiple_of(t * FEAT_CHUNKS, FEAT_CHUNKS)
        res_s[pl.ds(r0, FEAT_CHUNKS), :] = both[0:FEAT_CHUNKS, :] + both[FEAT_CHUNKS:2 * FEAT_CHUNKS, :]

    _pipelined_tokens(tt, ids_ref, tbl_ref, slots, compute)
    for c in range(FEAT_CHUNKS):
        cols = slice(c * LANES, (c + 1) * LANES)
        y = res_s[pl.ds(c, tt, stride=FEAT_CHUNKS), :]
        o_ref[:, cols] = x_ref[:, cols] + gate_ref[0][:, cols] * y


def _peer_v(ids_flat, coef, x2, gate, table, seq, tt):
    t, d = x2.shape
    per_b = seq // tt
    blk = pl.BlockSpec((tt, d), lambda i: (i, 0))
    return pl.pallas_call(
        _peer_v_kernel,
        grid=(t // tt,),
        in_specs=[
            pl.BlockSpec((tt * PEER_SLOTS,), lambda i: (i,), memory_space=pltpu.SMEM),
            pl.BlockSpec((tt, PEER_SLOTS), lambda i: (i, 0)),
            blk,
            pl.BlockSpec((1, 1, d), lambda i: (i // per_b, 0, 0)),
            _table_spec(table.shape[0]),
        ],
        out_specs=blk,
        out_shape=jax.ShapeDtypeStruct((t, d), F32),
        scratch_shapes=_slot_scratch() + [
                        pltpu.VMEM((tt, SLOT_WIDTH), F32), pltpu.VMEM((tt, SLOT_WIDTH), F32),
                        pltpu.VMEM((tt * FEAT_CHUNKS, LANES), F32)],
        compiler_params=_cparams(1, TABLE_VMEM_LIMIT),
        name="peer_expert_out",
    )(ids_flat, coef, x2, gate, table)


def _peer_ffn(x2, g, shift, scale, gate, wq, sub_keys, table_u, table_v, seq):
    t, d = x2.shape
    tt = PEER_TOKEN_TILE
    h, ids, gates = _peer_route(x2, g, shift, scale, wq.T.astype(BF16), sub_keys, seq)
    ids_flat = ids.reshape(t * PEER_SLOTS)
    coef = _peer_u(ids_flat, h, gates, table_u, tt)
    return _peer_v(ids_flat, coef, x2, gate, table_v, seq, tt)


def _rope_tables(seq):
    half = HEAD_DIM // 2
    inv_freq = ROPE_THETA ** (-jnp.arange(half, dtype=F32) / half)
    ang = jnp.arange(seq).astype(F32)[:, None] * inv_freq[None, :]
    reps = LANES // half
    return jnp.tile(jnp.cos(ang), (1, reps)), jnp.tile(jnp.sin(ang), (1, reps))


def _two_heads(gain):
    return jnp.tile(gain.reshape(1, HEAD_DIM), (1, LANES // HEAD_DIM))


def kernel(x, c, ada_w, ada_b, norm_mix_g, norm_ffn_g, w_in_ab, w_out_ab, sinks_a, qnorm_a, knorm_a,
           w_in_cd, w_out_cd, qnorm_c, knorm_c, qnorm_d, knorm_d, peer_wq, peer_subkeys, peer_u, peer_v):
    b, seq, d = x.shape
    depth = ada_w.shape[0]
    t = b * seq
    cos, sin = _rope_tables(seq)
    mod = _modulation(c, ada_w, ada_b)
    x2 = x.reshape(t, d)
    for layer in range(depth):
        shift_m, scale_m, gate_m, shift_f, scale_f, gate_f = [
            m.reshape(b, 1, d) for m in jnp.split(mod[layer], 6, axis=-1)]
        g_mix = norm_mix_g[layer].reshape(1, d)
        i = layer // 2
        if layer % 2 == 0:
            proj = _norm_proj(x2, g_mix, shift_m, scale_m, w_in_ab[i].astype(BF16), seq)
            proj = proj.reshape(b, seq, -1)
            ya = _swa_attention(proj, sinks_a[i], cos, sin, _two_heads(qnorm_a[i]), _two_heads(knorm_a[i]))
            b_col = (A_Q_HEADS + 2 * A_KV_HEADS) * HEAD_DIM // LANES
            yb = _stick_attention(proj, b_col)
            w_out = w_out_ab[i]
        else:
            proj = _norm_proj(x2, g_mix, shift_m, scale_m, w_in_cd[i].astype(BF16), seq)
            proj = proj.reshape(b, seq, -1)
            ya = _qkv_attention_call(_dilated_kernel, "dilated_attention", proj, 0, C_HEADS, cos, sin,
                                     _two_heads(qnorm_c[i]), _two_heads(knorm_c[i]))
            d_col = 3 * C_HEADS * HEAD_DIM // LANES
            yb = _qkv_attention_call(_moba_kernel, "moba_attention", proj, d_col, D_HEADS, cos, sin,
                                     _two_heads(qnorm_d[i]), _two_heads(knorm_d[i]),
                                     extra_scratch=(pltpu.VMEM((LANES, LANES), F32),
                                                    pltpu.VMEM((seq, LANES), F32),
                                                    pltpu.VMEM((seq, LANES), F32)))
            w_out = w_out_cd[i]
        x2 = _out_proj(x2, ya.reshape(t, -1), yb.reshape(t, -1), w_out.astype(BF16), gate_m, seq)
        x2 = _peer_ffn(x2, norm_ffn_g[layer].reshape(1, d), shift_f, scale_f, gate_f,
                       peer_wq[layer], peer_subkeys[layer],
                       _pack_table(peer_u[layer]), _pack_table(peer_v[layer]), seq)
    return x2.reshape(b, seq, d)
```

```python
import jax
import jax.numpy as jnp
from jax import lax
from jax.experimental import pallas as pl
from jax.experimental.pallas import tpu as pltpu

F32 = jnp.float32
BF16 = jnp.bfloat16
I32 = jnp.int32
U32 = jnp.uint32

HEAD_DIM = 64
ROPE_THETA = 10000.0
NORM_EPS = 1e-6
LANES = 128
QUERY_BLOCK = 128
A_Q_HEADS, A_KV_HEADS = 8, 2
B_HEADS = C_HEADS = D_HEADS = 8
C_PATTERNS = ((128, 1), (512, 4), (2048, 16))
MOBA_BLOCK, MOBA_TOPK = 256, 3
PEER_HEADS, PEER_N_KEYS, PEER_TOPK, PEER_D_KEY = 8, 128, 16, 256
PEER_SLOTS = PEER_HEADS * PEER_TOPK
NEG_BIG = -1e30
PEER_CAND_COUNTS = tuple(PEER_TOPK // (a + 1) for a in range(PEER_TOPK))
PACK_ROWS = 4
MIB = 1024 * 1024
V7X_VMEM_BYTES = 64 * MIB
STREAM_VMEM_LIMIT = 40 * MIB
TABLE_VMEM_LIMIT = V7X_VMEM_BYTES - 8 * MIB


def _cparams(n_axes, vmem_bytes=STREAM_VMEM_LIMIT):
    return pltpu.CompilerParams(
        dimension_semantics=("arbitrary",) * n_axes,
        vmem_limit_bytes=vmem_bytes)


def _split_bf16(a):
    hi = a.astype(BF16)
    lo = (a - hi.astype(F32)).astype(BF16)
    return hi, lo


def _dot(a, b):
    return jnp.dot(a, b, preferred_element_type=F32)


def _dot_nt(a, b):
    return lax.dot_general(a, b, (((1,), (1,)), ((), ())), preferred_element_type=F32)


def _dot3(a, b):
    ah, al = _split_bf16(a)
    bh, bl = _split_bf16(b)
    return _dot(ah, bh) + _dot(ah, bl) + _dot(al, bh)


def _dot3_nt(a, b):
    ah, al = _split_bf16(a)
    bh, bl = _split_bf16(b)
    return _dot_nt(ah, bh) + _dot_nt(ah, bl) + _dot_nt(al, bh)


def _iota(shape, dim):
    return lax.broadcasted_iota(I32, shape, dim)


def _mod_kernel(c_ref, w_ref, b_ref, o_ref):
    c = c_ref[...]
    cond = c * jax.nn.sigmoid(c)
    o_ref[0] = _dot3(cond, w_ref[0]) + b_ref[0]


def _modulation(c, ada_w, ada_b):
    depth, d, n = ada_w.shape
    b = c.shape[0]
    tn = 1024
    return pl.pallas_call(
        _mod_kernel,
        grid=(depth, n // tn),
        in_specs=[
            pl.BlockSpec((b, d), lambda l, j: (0, 0)),
            pl.BlockSpec((1, d, tn), lambda l, j: (l, 0, j)),
            pl.BlockSpec((1, 1, tn), lambda l, j: (l, 0, j)),
        ],
        out_specs=pl.BlockSpec((1, b, tn), lambda l, j: (l, 0, j)),
        out_shape=jax.ShapeDtypeStruct((depth, b, n), F32),
        compiler_params=_cparams(2),
        name="adaln_modulation",
    )(c, ada_w, ada_b.reshape(depth, 1, n))


def _adaln(x, g, shift, scale):
    ms = jnp.mean(x * x, axis=-1, keepdims=True)
    y = x * lax.rsqrt(ms + NORM_EPS) * g
    return y * (1.0 + scale) + shift


def _norm_proj_kernel(x_ref, g_ref, sh_ref, sc_ref, w_ref, o_ref):
    h = _adaln(x_ref[...], g_ref[...], sh_ref[0], sc_ref[0])
    o_ref[...] = _dot(h.astype(BF16), w_ref[...])


def _norm_proj(x2, g, shift, scale, w_bf, seq):
    t, d = x2.shape
    n = w_bf.shape[1]
    tt = 512
    per_b = seq // tt
    return pl.pallas_call(
        _norm_proj_kernel,
        grid=(t // tt,),
        in_specs=[
            pl.BlockSpec((tt, d), lambda i: (i, 0)),
            pl.BlockSpec((1, d), lambda i: (0, 0)),
            pl.BlockSpec((1, 1, d), lambda i: (i // per_b, 0, 0)),
            pl.BlockSpec((1, 1, d), lambda i: (i // per_b, 0, 0)),
            pl.BlockSpec((d, n), lambda i: (0, 0)),
        ],
        out_specs=pl.BlockSpec((tt, n), lambda i: (i, 0)),
        out_shape=jax.ShapeDtypeStruct((t, n), F32),
        compiler_params=_cparams(1),
        name="adaln_in_proj",
    )(x2, g, shift, scale, w_bf)


def _out_proj_kernel(x_ref, ya_ref, yb_ref, w_ref, gate_ref, o_ref):
    half = ya_ref.shape[1]
    y = _dot(ya_ref[...].astype(BF16), w_ref[0:half, :])
    y = y + _dot(yb_ref[...].astype(BF16), w_ref[half:2 * half, :])
    o_ref[...] = x_ref[...] + gate_ref[0] * y


def _out_proj(x2, ya, yb, w_bf, gate, seq):
    t, d = x2.shape
    half = ya.shape[1]
    tt = 512
    per_b = seq // tt
    return pl.pallas_call(
        _out_proj_kernel,
        grid=(t // tt,),
        in_specs=[
            pl.BlockSpec((tt, d), lambda i: (i, 0)),
            pl.BlockSpec((tt, half), lambda i: (i, 0)),
            pl.BlockSpec((tt, half), lambda i: (i, 0)),
            pl.BlockSpec((2 * half, d), lambda i: (0, 0)),
            pl.BlockSpec((1, 1, d), lambda i: (i // per_b, 0, 0)),
        ],
        out_specs=pl.BlockSpec((tt, d), lambda i: (i, 0)),
        out_shape=jax.ShapeDtypeStruct((t, d), F32),
        compiler_params=_cparams(1),
        name="mixer_out_proj",
    )(x2, ya, yb, w_bf, gate)


def _lane_row():
    return _iota((1, LANES), 1)


def _head_segment_ones():
    r = _iota((LANES, LANES), 0) // HEAD_DIM
    c = _iota((LANES, LANES), 1) // HEAD_DIM
    return jnp.where(r == c, 1.0, 0.0).astype(BF16)


def _headnorm_rope(a, g, cos, sin):
    hi, lo = _split_bf16(a * a)
    seg = _head_segment_ones()
    ms = (_dot(hi, seg) + _dot(lo, seg)) * (1.0 / HEAD_DIM)
    y = a * lax.rsqrt(ms + NORM_EPS) * g
    half = HEAD_DIM // 2
    upper = pltpu.roll(y, LANES - half, axis=1)
    lower = pltpu.roll(y, half, axis=1)
    first_half = (_lane_row() % HEAD_DIM) < half
    rot = jnp.where(first_half, -upper, lower)
    return y * cos + rot * sin


def _head_masks():
    lane = _lane_row()
    return lane < HEAD_DIM, lane >= HEAD_DIM


def _attn_specs(seq, qcol, kcol, vcol, kv_shared):
    blk = (1, seq, LANES)
    q_spec = pl.BlockSpec(blk, lambda b, p: (b, 0, qcol + p))
    if kv_shared:
        k_spec = pl.BlockSpec(blk, lambda b, p: (b, 0, kcol))
        v_spec = pl.BlockSpec(blk, lambda b, p: (b, 0, vcol))
    else:
        k_spec = pl.BlockSpec(blk, lambda b, p: (b, 0, kcol + p))
        v_spec = pl.BlockSpec(blk, lambda b, p: (b, 0, vcol + p))
    return q_spec, k_spec, v_spec


def _row_spec(seq):
    return pl.BlockSpec((seq, LANES), lambda b, p: (0, 0))


def _gain_spec():
    return pl.BlockSpec((1, LANES), lambda b, p: (0, 0))


def _store_heads(o_ref, r0, outs):
    first, _ = _head_masks()
    o_ref[0, pl.ds(r0, QUERY_BLOCK), :] = jnp.where(first, outs[0], outs[1])


SWA_QBLOCKS_PER_STEP = 4


def _swa_kernel(sinks_ref, q_ref, k_ref, v_ref, cos_ref, sin_ref, gq_ref, gk_ref, o_ref,
                q0_s, q1_s, k_s, v_s):
    p = pl.program_id(1)
    seq = q_ref.shape[1]
    cos, sin = cos_ref[...], sin_ref[...]
    first, second = _head_masks()
    qn = _headnorm_rope(q_ref[0], gq_ref[...], cos, sin) * (HEAD_DIM ** -0.5)
    q0_s[...] = jnp.where(first, qn, 0.0).astype(BF16)
    q1_s[...] = jnp.where(second, qn, 0.0).astype(BF16)
    pairs_per_kv = (A_Q_HEADS // A_KV_HEADS) // 2
    keep = jnp.logical_xor(first, (p // pairs_per_kv) == 1)
    kn = _headnorm_rope(k_ref[0], gk_ref[...], cos, sin)
    k_s[...] = jnp.where(keep, kn, pltpu.roll(kn, HEAD_DIM, axis=1)).astype(BF16)
    v = v_ref[0]
    v_s[...] = jnp.where(keep, v, pltpu.roll(v, HEAD_DIM, axis=1)).astype(BF16)

    qi = _iota((QUERY_BLOCK, QUERY_BLOCK), 0)
    ki = _iota((QUERY_BLOCK, QUERY_BLOCK), 1)

    mask_c = ki <= qi
    per_step = min(SWA_QBLOCKS_PER_STEP, seq // QUERY_BLOCK)

    def qstep(step, carry):
        chains, scores = [], []
        for u in range(per_step):
            i = step * per_step + u
            r0 = pl.multiple_of(i * QUERY_BLOCK, QUERY_BLOCK)
            rp = pl.multiple_of(jnp.maximum(i - 1, 0) * QUERY_BLOCK, QUERY_BLOCK)
            kc, kp = k_s[pl.ds(r0, QUERY_BLOCK), :], k_s[pl.ds(rp, QUERY_BLOCK), :]
            vc, vp = v_s[pl.ds(r0, QUERY_BLOCK), :], v_s[pl.ds(rp, QUERY_BLOCK), :]
            mask_p = jnp.logical_and(ki > qi, i > 0)
            chains.append((r0, vc, vp, mask_p))
            for q_s in (q0_s, q1_s):
                qh = q_s[pl.ds(r0, QUERY_BLOCK), :]
                scores.append((_dot_nt(qh, kc), _dot_nt(qh, kp)))
        probs, denoms = [], []
        for n, (s_cur, s_prev) in enumerate(scores):
            mask_p = chains[n // 2][3]
            sc = jnp.where(mask_c, s_cur, NEG_BIG)
            sp = jnp.where(mask_p, s_prev, NEG_BIG)
            sink = sinks_ref[2 * p + n % 2]
            m = jnp.maximum(jnp.max(sc, axis=1, keepdims=True), jnp.max(sp, axis=1, keepdims=True))
            m = jnp.maximum(m, sink)
            ec, ep = jnp.exp(sc - m), jnp.exp(sp - m)
            denoms.append(jnp.sum(ec, axis=1, keepdims=True) + jnp.sum(ep, axis=1, keepdims=True)
                          + jnp.exp(sink - m))
            probs.append((ec.astype(BF16), ep.astype(BF16)))
        for u, (r0, vc, vp, _) in enumerate(chains):
            outs = [(_dot(probs[2 * u + hh][0], vc) + _dot(probs[2 * u + hh][1], vp)) / denoms[2 * u + hh]
                    for hh in range(2)]
            _store_heads(o_ref, r0, outs)
        return carry

    lax.fori_loop(0, seq // (QUERY_BLOCK * per_step), qstep, 0)


def _swa_attention(proj, sinks, cos, sin, gq, gk):
    b, seq, _ = proj.shape
    n_pairs = A_Q_HEADS // 2
    kcol = A_Q_HEADS * HEAD_DIM // LANES
    vcol = kcol + A_KV_HEADS * HEAD_DIM // LANES
    q_spec, k_spec, v_spec = _attn_specs(seq, 0, kcol, vcol, True)
    return pl.pallas_call(
        _swa_kernel,
        grid=(b, n_pairs),
        in_specs=[pl.BlockSpec(memory_space=pltpu.SMEM), q_spec, k_spec, v_spec,
                  _row_spec(seq), _row_spec(seq), _gain_spec(), _gain_spec()],
        out_specs=pl.BlockSpec((1, seq, LANES), lambda b_, p: (b_, 0, p)),
        out_shape=jax.ShapeDtypeStruct((b, seq, n_pairs * LANES), F32),
        scratch_shapes=[pltpu.VMEM((seq, LANES), BF16)] * 4,
        compiler_params=_cparams(2),
        name="swa_gqa_attention",
    )(sinks, proj, proj, proj, cos, sin, gq, gk)


STICK_GROUP = 4
STICK_QUERY_ROWS = 256


def _stick_kernel(q_ref, k_ref, v_ref, o_ref, k_s, v_s):
    seq = q_ref.shape[1]
    k_s[...] = k_ref[0].astype(BF16)
    v_s[...] = v_ref[0].astype(BF16)
    first, second = _head_masks()
    qr = min(STICK_QUERY_ROWS, seq)
    kw = QUERY_BLOCK
    qi = _iota((qr, kw), 0)
    ki = _iota((qr, kw), 1)
    wr = _iota((2 * kw, 2 * kw), 0) % kw
    wc = _iota((2 * kw, 2 * kw), 1)
    suffix_w = jnp.where(jnp.logical_or(wc >= kw, wr > wc), 1.0, 0.0).astype(BF16)

    def qblock(i, carry):
        r0 = pl.multiple_of(i * qr, qr)
        q = q_ref[0, pl.ds(r0, qr), :] * (HEAD_DIM ** -0.5)
        qhs = [jnp.where(msk, q, 0.0).astype(BF16) for msk in (first, second)]
        n_blocks = (r0 + qr) // kw

        def kgroup(g, st):
            accs, laters = [st[0], st[1]], [st[2], st[3]]
            chains = [(u, hh) for u in range(STICK_GROUP) for hh in range(2)]
            vbs, pasts, zs = [], [], {}
            for u in range(STICK_GROUP):
                j = n_blocks - 1 - (g * STICK_GROUP + u)
                live = j >= 0
                c0 = pl.multiple_of(jnp.maximum(j, 0) * kw, kw)
                kb = k_s[pl.ds(c0, kw), :]
                vbs.append(v_s[pl.ds(c0, kw), :])
                pasts.append(jnp.logical_and((c0 + ki) < (r0 + qi), live))
                for hh in range(2):
                    zs[u, hh] = _dot_nt(qhs[hh], kb)
            logit, sums = {}, {}
            for u, hh in chains:
                z = zs[u, hh]
                sp = jnp.maximum(z, 0.0) + jnp.log(1.0 + jnp.exp(-jnp.abs(z)))
                log_keep = jnp.where(pasts[u], -sp, 0.0)
                logit[u, hh] = z - sp
                hi, lo = _split_bf16(log_keep)
                sums[u, hh] = _dot(jnp.concatenate([hi, lo], axis=1), suffix_w)
            ws = {}
            for u, hh in chains:
                inner, total = sums[u, hh][:, :kw], sums[u, hh][:, kw:]
                ws[u, hh] = jnp.where(pasts[u], jnp.exp(logit[u, hh] + inner + laters[hh]), 0.0).astype(BF16)
                laters[hh] = laters[hh] + total
            for u, hh in chains:
                accs[hh] = accs[hh] + _dot(ws[u, hh], vbs[u])
            return accs[0], accs[1], laters[0], laters[1]

        zero = jnp.zeros((qr, LANES), F32)
        n_groups = (n_blocks + STICK_GROUP - 1) // STICK_GROUP
        st = lax.fori_loop(0, n_groups, kgroup, (zero, zero, zero, zero))
        o_ref[0, pl.ds(r0, qr), :] = jnp.where(first, st[0], st[1])
        return carry

    lax.fori_loop(0, seq // qr, qblock, 0)


def _stick_attention(proj, qcol):
    b, seq, _ = proj.shape
    n_pairs = B_HEADS // 2
    q_spec, k_spec, v_spec = _attn_specs(seq, qcol, qcol + n_pairs, qcol + 2 * n_pairs, False)
    return pl.pallas_call(
        _stick_kernel,
        grid=(b, n_pairs),
        in_specs=[q_spec, k_spec, v_spec],
        out_specs=pl.BlockSpec((1, seq, LANES), lambda b_, p: (b_, 0, p)),
        out_shape=jax.ShapeDtypeStruct((b, seq, n_pairs * LANES), F32),
        scratch_shapes=[pltpu.VMEM((seq, LANES), BF16)] * 2,
        compiler_params=_cparams(2),
        name="stick_breaking_attention",
    )(proj, proj, proj)


def _prep_qkv(q_ref, k_ref, v_ref, cos_ref, sin_ref, gq_ref, gk_ref, q0_s, q1_s, k_s, v_s):
    cos, sin = cos_ref[...], sin_ref[...]
    first, second = _head_masks()
    qn = _headnorm_rope(q_ref[0], gq_ref[...], cos, sin) * (HEAD_DIM ** -0.5)
    q0_s[...] = jnp.where(first, qn, 0.0).astype(BF16)
    q1_s[...] = jnp.where(second, qn, 0.0).astype(BF16)
    kn = _headnorm_rope(k_ref[0], gk_ref[...], cos, sin)
    k_s[...] = kn.astype(BF16)
    v_s[...] = v_ref[0].astype(BF16)
    return qn, kn


DILATED_KEY_TILE = 512
DILATED_QUERY_ROWS = 256


def _dilated_kernel(q_ref, k_ref, v_ref, cos_ref, sin_ref, gq_ref, gk_ref, o_ref,
                    q0_s, q1_s, k_s, v_s):
    seq = q_ref.shape[1]
    _prep_qkv(q_ref, k_ref, v_ref, cos_ref, sin_ref, gq_ref, gk_ref, q0_s, q1_s, k_s, v_s)
    kt = min(DILATED_KEY_TILE, seq)
    qr = min(DILATED_QUERY_ROWS, seq)
    qk = _iota((qr, kt), 0) - _iota((qr, kt), 1)
    on_stride = [jnp.where((qk & (dil - 1)) == 0, 1.0, 0.0) for _, dil in C_PATTERNS]
    first_head, _ = _head_masks()

    def qblock(i, carry):
        r0 = pl.multiple_of(i * qr, qr)
        qhs = [q_s[pl.ds(r0, qr), :] for q_s in (q0_s, q1_s)]

        def ktile(g, st):
            c0 = pl.multiple_of(g * kt, kt)
            d = (r0 - c0) + qk
            count = jnp.zeros(d.shape, F32)
            for (window, _), stride_ok in zip(C_PATTERNS, on_stride):
                count = count + jnp.where(d <= window, stride_ok, 0.0)
            count = jnp.where(d >= 0, count, 0.0)
            kb, vb = k_s[pl.ds(c0, kt), :], v_s[pl.ds(c0, kt), :]
            scores = [_dot_nt(qhs[hh], kb) for hh in range(2)]
            new, prs = [], []
            for hh in range(2):
                m, l, acc = st[3 * hh:3 * hh + 3]
                s = jnp.where(count > 0.0, scores[hh], NEG_BIG)
                m_new = jnp.maximum(m, jnp.max(s, axis=1, keepdims=True))
                pr = count * jnp.exp(s - m_new)
                alpha = jnp.exp(m - m_new)
                new += [m_new, alpha * l + jnp.sum(pr, axis=1, keepdims=True), alpha * acc]
                prs.append(pr.astype(BF16))
            for hh in range(2):
                new[3 * hh + 2] = new[3 * hh + 2] + _dot(prs[hh], vb)
            return tuple(new)

        init = (jnp.full((qr, 1), NEG_BIG, F32), jnp.zeros((qr, 1), F32),
                jnp.zeros((qr, LANES), F32)) * 2
        st = lax.fori_loop(0, (r0 + qr + kt - 1) // kt, ktile, init)
        o_ref[0, pl.ds(r0, qr), :] = jnp.where(first_head, st[2] / st[1], st[5] / st[4])
        return carry

    lax.fori_loop(0, seq // qr, qblock, 0)


def _qkv_attention_call(kernel, name, proj, qcol, n_heads, cos, sin, gq, gk, extra_scratch=()):
    b, seq, _ = proj.shape
    n_pairs = n_heads // 2
    q_spec, k_spec, v_spec = _attn_specs(seq, qcol, qcol + n_pairs, qcol + 2 * n_pairs, False)
    return pl.pallas_call(
        kernel,
        grid=(b, n_pairs),
        in_specs=[q_spec, k_spec, v_spec, _row_spec(seq), _row_spec(seq), _gain_spec(), _gain_spec()],
        out_specs=pl.BlockSpec((1, seq, LANES), lambda b_, p: (b_, 0, p)),
        out_shape=jax.ShapeDtypeStruct((b, seq, n_pairs * LANES), F32),
        scratch_shapes=[pltpu.VMEM((seq, LANES), BF16)] * 4 + list(extra_scratch),
        compiler_params=_cparams(2),
        name=name,
    )(proj, proj, proj, cos, sin, gq, gk)


def _moba_kernel(q_ref, k_ref, v_ref, cos_ref, sin_ref, gq_ref, gk_ref, o_ref,
                 q0_s, q1_s, k_s, v_s, km_s, sel0_s, sel1_s):
    seq = q_ref.shape[1]
    n_blocks = seq // MOBA_BLOCK
    qn, kn = _prep_qkv(q_ref, k_ref, v_ref, cos_ref, sin_ref, gq_ref, gk_ref, q0_s, q1_s, k_s, v_s)
    km_s[...] = jnp.zeros(km_s.shape, F32)
    km_s[0:n_blocks, :] = jnp.mean(kn.reshape(n_blocks, MOBA_BLOCK, LANES), axis=1)
    first, second = _head_masks()

    rows8 = _iota((8, seq), 0)
    own8 = _iota((8, seq), 1) // MOBA_BLOCK
    valid = rows8 < own8
    for msk, sel_s in ((first, sel0_s), (second, sel1_s)):
        gate = _dot3_nt(km_s[...], jnp.where(msk, qn, 0.0))[0:8, :]
        gm = jnp.where(valid, gate, -jnp.inf)
        rank = jnp.zeros((8, seq), F32)
        for n2 in range(n_blocks):
            g2 = gm[n2:n2 + 1, :]
            beats = jnp.logical_or(g2 > gm, jnp.logical_and(g2 == gm, n2 < rows8))
            rank = rank + jnp.where(jnp.logical_and(beats, n2 < own8), 1.0, 0.0)
        sel = jnp.where(jnp.logical_and(valid, rank < float(MOBA_TOPK)), 1.0, 0.0)
        sel = jnp.concatenate([sel, jnp.zeros((LANES - 8, seq), F32)], axis=0)
        sel_s[...] = sel.T

    qrows = MOBA_BLOCK
    lane_sq = _iota((qrows, LANES), 1)
    causal = _iota((qrows, MOBA_BLOCK), 1) <= _iota((qrows, MOBA_BLOCK), 0)
    second_block = _iota((qrows, 2 * MOBA_BLOCK), 1) >= MOBA_BLOCK
    first_head, _ = _head_masks()

    def qblock(own, carry):
        r0 = pl.multiple_of(own * qrows, qrows)
        qhs = [q_s[pl.ds(r0, qrows), :] for q_s in (q0_s, q1_s)]
        sels = [sel_s[pl.ds(r0, qrows), :] for sel_s in (sel0_s, sel1_s)]
        kb, vb = k_s[pl.ds(r0, MOBA_BLOCK), :], v_s[pl.ds(r0, MOBA_BLOCK), :]
        scores = [_dot_nt(qhs[hh], kb) for hh in range(2)]
        init, prs = [], []
        for hh in range(2):
            s = jnp.where(causal, scores[hh], NEG_BIG)
            m = jnp.max(s, axis=1, keepdims=True)
            pr = jnp.exp(s - m)
            init += [m, jnp.sum(pr, axis=1, keepdims=True), None]
            prs.append(pr.astype(BF16))
        for hh in range(2):
            init[3 * hh + 2] = _dot(prs[hh], vb)

        def kpair(g, st):
            c0 = pl.multiple_of(g * 2 * MOBA_BLOCK, 2 * MOBA_BLOCK)
            kb2, vb2 = k_s[pl.ds(c0, 2 * MOBA_BLOCK), :], v_s[pl.ds(c0, 2 * MOBA_BLOCK), :]
            scores = [_dot_nt(qhs[hh], kb2) for hh in range(2)]
            new, prs = [], []
            for hh in range(2):
                m, l, acc = st[3 * hh:3 * hh + 3]
                sel_a = jnp.sum(jnp.where(lane_sq == 2 * g, sels[hh], 0.0), axis=1, keepdims=True)
                sel_b = jnp.sum(jnp.where(lane_sq == 2 * g + 1, sels[hh], 0.0), axis=1, keepdims=True)
                keep = jnp.where(second_block, sel_b, sel_a) > 0.0
                s = jnp.where(keep, scores[hh], NEG_BIG)
                m_new = jnp.maximum(m, jnp.max(s, axis=1, keepdims=True))
                pr = jnp.exp(s - m_new)
                alpha = jnp.exp(m - m_new)
                new += [m_new, alpha * l + jnp.sum(pr, axis=1, keepdims=True), alpha * acc]
                prs.append(pr.astype(BF16))
            for hh in range(2):
                new[3 * hh + 2] = new[3 * hh + 2] + _dot(prs[hh], vb2)
            return tuple(new)

        st = lax.fori_loop(0, (own + 1) // 2, kpair, tuple(init))
        o_ref[0, pl.ds(r0, qrows), :] = jnp.where(first_head, st[2] / st[1], st[5] / st[4])
        return carry

    lax.fori_loop(0, seq // qrows, qblock, 0)


ROUTE_CHUNKS_PER_STEP = 4


def _oddeven_merge_sort_pairs(n):
    pairs = []

    def merge(lo, hi, r):
        step = 2 * r
        if step < hi - lo:
            merge(lo, hi, step)
            merge(lo + r, hi, step)
            pairs.extend((i, i + r) for i in range(lo + r, hi - r, step))
        else:
            pairs.append((lo, lo + r))

    def sort(lo, hi):
        if hi - lo >= 1:
            mid = lo + (hi - lo) // 2
            sort(lo, mid)
            sort(mid + 1, hi)
            merge(lo, hi, 1)

    sort(0, n - 1)
    return pairs


SUBLANES = 8


def _top16_rows(scores, n_rows, vals_refs, idx_refs):
    n_slabs = n_rows // SUBLANES
    sub = _iota((SUBLANES, LANES), 0)
    vals = [[s[SUBLANES * v:SUBLANES * (v + 1), :] for v in range(n_slabs)] for s in scores]
    idxs = [[sub + SUBLANES * v for v in range(n_slabs)] for _ in scores]
    for i, j in _oddeven_merge_sort_pairs(n_slabs):
        for va, ia in zip(vals, idxs):
            a, b = va[i], va[j]
            a_first = jnp.logical_or(a > b, jnp.logical_and(a == b, ia[i] < ia[j]))
            va[i], va[j] = jnp.maximum(a, b), jnp.minimum(a, b)
            ia[i], ia[j] = jnp.where(a_first, ia[i], ia[j]), jnp.where(a_first, ia[j], ia[i])
    for it in range(PEER_TOPK):
        for k, (va, ia) in enumerate(zip(vals, idxs)):
            m = jnp.max(va[0], axis=0, keepdims=True)
            pick = jnp.min(jnp.where(va[0] == m, ia[0], n_rows), axis=0, keepdims=True)
            vals_refs[k][it:it + 1, :] = m
            idx_refs[k][it:it + 1, :] = pick
            win = ia[0] == pick
            depth = PEER_TOPK - 1 - it
            for d in range(min(depth, n_slabs - 1)):
                va[d] = jnp.where(win, va[d + 1], va[d])
                ia[d] = jnp.where(win, ia[d + 1], ia[d])
            if depth >= n_slabs:
                va[n_slabs - 1] = jnp.where(win, -jnp.inf, va[n_slabs - 1])


def _peer_route_kernel(x_ref, g_ref, sh_ref, sc_ref, wqt_ref, sk_ref, h_ref, ids_ref, gts_ref,
                       q_s, val_s, idx_s, ids_s, gts_s):
    tt = x_ref.shape[0]
    n_chunks = tt // LANES
    per_step = ROUTE_CHUNKS_PER_STEP
    half = PEER_D_KEY // 2
    h = _adaln(x_ref[...], g_ref[...], sh_ref[0], sc_ref[0])
    h_ref[...] = h
    qt = _dot_nt(wqt_ref[...], h.astype(BF16))
    for c in range(n_chunks):
        q_s[c] = qt[:, c * LANES:(c + 1) * LANES]
    sk1, sk2 = sk_ref[0], sk_ref[1]
    col_id = _iota((PEER_TOPK, LANES), 0)
    sub = _iota((SUBLANES, LANES), 0)
    col_depth = jnp.zeros((SUBLANES, LANES), I32)
    for a in range(SUBLANES):
        col_depth = jnp.where(sub == a, PEER_CAND_COUNTS[a], col_depth)
    vals = [val_s.at[i] for i in range(2 * per_step)]
    idxs = [idx_s.at[i] for i in range(2 * per_step)]

    def body(step, carry):
        hh = step // (n_chunks // per_step)
        c0 = (step % (n_chunks // per_step)) * per_step
        q0 = pl.multiple_of(hh * PEER_D_KEY, PEER_D_KEY)
        scores = []
        for k in range(per_step):
            scores.append(_dot3(sk1, q_s[c0 + k, pl.ds(q0, half), :]))
            scores.append(_dot3(sk2, q_s[c0 + k, pl.ds(q0 + half, half), :]))
        _top16_rows(scores, PEER_N_KEYS, vals, idxs)
        state = []
        for k in range(per_step):
            v1, i1, v2, i2 = vals[2 * k], idxs[2 * k], vals[2 * k + 1], idxs[2 * k + 1]
            v1x, e1x = v1[0:SUBLANES, :], i1[0:SUBLANES, :] * PEER_N_KEYS
            xs = [jnp.where(col_depth > b, v1x + v2[b:b + 1, :], -jnp.inf) for b in range(PEER_TOPK)]
            ex = [e1x + i2[b:b + 1, :] for b in range(PEER_TOPK)]
            y = v1[SUBLANES:PEER_TOPK, :] + v2[0:1, :]
            ey = i1[SUBLANES:PEER_TOPK, :] * PEER_N_KEYS + i2[0:1, :]
            state.append([xs, ex, y, ey])
        for it in range(PEER_TOPK):
            for k in range(per_step):
                xs, ex, y, ey = state[k]
                heads = jnp.concatenate([xs[0], y], axis=0)
                m = jnp.max(heads, axis=0, keepdims=True)
                pick = jnp.min(jnp.where(heads == m, col_id, PEER_TOPK), axis=0, keepdims=True)
                win = col_id == pick
                eids = jnp.where(win, jnp.concatenate([ex[0], ey], axis=0), 0)
                idxs[2 * k][it:it + 1, :] = jnp.sum(eids, axis=0, keepdims=True)
                vals[2 * k][it:it + 1, :] = m
                win_x, win_y = win[0:SUBLANES, :], win[SUBLANES:PEER_TOPK, :]
                for dd in range(PEER_TOPK - 1 - it):
                    xs[dd] = jnp.where(win_x, xs[dd + 1], xs[dd])
                    ex[dd] = jnp.where(win_x, ex[dd + 1], ex[dd])
                state[k][2] = jnp.where(win_y, -jnp.inf, y)
        r0 = pl.multiple_of(hh * PEER_TOPK, PEER_TOPK)
        for k in range(per_step):
            top = vals[2 * k][...]
            e = jnp.exp(top - top[0:1, :])
            gts_s[c0 + k, pl.ds(r0, PEER_TOPK), :] = e / jnp.sum(e, axis=0, keepdims=True)
            ids_s[c0 + k, pl.ds(r0, PEER_TOPK), :] = idxs[2 * k][...] * PACK_ROWS
        return carry

    lax.fori_loop(0, PEER_HEADS * n_chunks // per_step, body, 0)
    for c in range(n_chunks):
        rows = slice(c * LANES, (c + 1) * LANES)
        ids_ref[rows, :] = ids_s[c].T
        gts_ref[rows, :] = gts_s[c].T


def _peer_route(x2, g, shift, scale, wqt_bf, sub_keys, seq):
    t, d = x2.shape
    tt = 512
    per_b = seq // tt
    n_chunks = tt // LANES
    nq = wqt_bf.shape[0]
    out_blk = pl.BlockSpec((tt, PEER_SLOTS), lambda i: (i, 0))
    return pl.pallas_call(
        _peer_route_kernel,
        grid=(t // tt,),
        in_specs=[
            pl.BlockSpec((tt, d), lambda i: (i, 0)),
            pl.BlockSpec((1, d), lambda i: (0, 0)),
            pl.BlockSpec((1, 1, d), lambda i: (i // per_b, 0, 0)),
            pl.BlockSpec((1, 1, d), lambda i: (i // per_b, 0, 0)),
            pl.BlockSpec((nq, d), lambda i: (0, 0)),
            pl.BlockSpec(sub_keys.shape, lambda i: (0, 0, 0)),
        ],
        out_specs=[pl.BlockSpec((tt, d), lambda i: (i, 0)), out_blk, out_blk],
        out_shape=[jax.ShapeDtypeStruct((t, d), F32),
                   jax.ShapeDtypeStruct((t, PEER_SLOTS), I32),
                   jax.ShapeDtypeStruct((t, PEER_SLOTS), F32)],
        scratch_shapes=[pltpu.VMEM((n_chunks, nq, LANES), F32),
                        pltpu.VMEM((2 * ROUTE_CHUNKS_PER_STEP, PEER_TOPK, LANES), F32),
                        pltpu.VMEM((2 * ROUTE_CHUNKS_PER_STEP, PEER_TOPK, LANES), I32),
                        pltpu.VMEM((n_chunks, PEER_SLOTS, LANES), I32),
                        pltpu.VMEM((n_chunks, PEER_SLOTS, LANES), F32)],
        compiler_params=_cparams(1),
        name="peer_route",
    )(x2, g, shift, scale, wqt_bf, sub_keys)


def _pack_table(tab):
    e, d = tab.shape
    bits = lax.bitcast_convert_type(tab.astype(BF16), jnp.uint16).astype(U32)
    bits = bits.reshape(e, d // (2 * LANES), 2, LANES)
    words = bits[:, :, 0, :] | (bits[:, :, 1, :] << 16)
    return words.reshape(e * (d // (2 * LANES)), LANES)


def _table_spec(rows):
    return pl.BlockSpec((rows, LANES), lambda i: (0, 0), pipeline_mode=pl.Buffered(1))


def _gelu_exact(a):
    return 0.5 * a * (1.0 + lax.erf(a * (2.0 ** -0.5)))


FEAT_CHUNKS = 8
SLOT_WIDTH = PEER_SLOTS * FEAT_CHUNKS


def _gather_rows(ids_ref, base, tbl_ref, slot):
    for j in range(PEER_SLOTS):
        if j % ID_VIEW == 0:
            ids_part = ids_ref.at[pl.ds(base + j, ID_VIEW)]
        row0 = pl.multiple_of(ids_part[j % ID_VIEW], PACK_ROWS)
        slot[PACK_ROWS * j:PACK_ROWS * (j + 1), :] = tbl_ref[pl.ds(row0, PACK_ROWS), :]


def _pipelined_tokens(tt, ids_ref, tbl_ref, slots, compute):
    group = len(slots)
    last = tt - 1

    def gather(t, slot):
        _gather_rows(ids_ref, jnp.minimum(t, last) * PEER_SLOTS, tbl_ref, slot)

    for k in range(group):
        gather(k, slots[k])

    def step(q, carry):
        t = group * q
        for k in range(group):
            compute(t + k, slots[k])
        for k in range(group):
            gather(t + group + k, slots[k])
        return carry

    lax.fori_loop(0, tt // group, step, 0)


def _chunk_diag():
    return (_iota((FEAT_CHUNKS, SLOT_WIDTH), 1) % FEAT_CHUNKS) == _iota((FEAT_CHUNKS, SLOT_WIDTH), 0)


def _peer_u_kernel(ids_ref, h_ref, g_ref, tbl_ref, coef_ref, *scratch):
    slots, (hx_s, rs_s) = scratch[:N_SLOTS], scratch[N_SLOTS:]
    tt = h_ref.shape[0]
    for c in range(FEAT_CHUNKS):
        hx_s[pl.ds(c, tt, stride=FEAT_CHUNKS), :] = h_ref[:, c * LANES:(c + 1) * LANES]
    diag = _chunk_diag()

    def compute(t, slot):
        rows = pltpu.bitcast(slot[...], BF16)
        x8 = hx_s[pl.ds(pl.multiple_of(t * FEAT_CHUNKS, FEAT_CHUNKS), FEAT_CHUNKS), :]
        part = _dot_nt(x8.astype(BF16), rows)
        rs_s[pl.ds(t, 1), :] = jnp.sum(jnp.where(diag, part, 0.0), axis=0, keepdims=True)

    _pipelined_tokens(tt, ids_ref, tbl_ref, slots, compute)
    group = jnp.where(_iota((SLOT_WIDTH, PEER_SLOTS), 0) // FEAT_CHUNKS == _iota((SLOT_WIDTH, PEER_SLOTS), 1),
                      1.0, 0.0).astype(BF16)
    hi, lo = _split_bf16(rs_s[...])
    act = _dot(hi, group) + _dot(lo, group)
    coef_ref[...] = g_ref[...] * _gelu_exact(act)


N_SLOTS = 16
ID_VIEW = 16
PEER_TOKEN_TILE = 512


def _slot_scratch():
    return [pltpu.VMEM((PEER_SLOTS * PACK_ROWS, LANES), U32)] * N_SLOTS


def _peer_u(ids_flat, h, gates, table, tt):
    t, d = h.shape
    return pl.pallas_call(
        _peer_u_kernel,
        grid=(t // tt,),
        in_specs=[
            pl.BlockSpec((tt * PEER_SLOTS,), lambda i: (i,), memory_space=pltpu.SMEM),
            pl.BlockSpec((tt, d), lambda i: (i, 0)),
            pl.BlockSpec((tt, PEER_SLOTS), lambda i: (i, 0)),
            _table_spec(table.shape[0]),
        ],
        out_specs=pl.BlockSpec((tt, PEER_SLOTS), lambda i: (i, 0)),
        out_shape=jax.ShapeDtypeStruct((t, PEER_SLOTS), F32),
        scratch_shapes=_slot_scratch() + [
                        pltpu.VMEM((tt * FEAT_CHUNKS, LANES), F32),
                        pltpu.VMEM((tt, SLOT_WIDTH), F32)],
        compiler_params=_cparams(1, TABLE_VMEM_LIMIT),
        name="peer_expert_in",
    )(ids_flat, h, gates, table)


def _peer_v_kernel(ids_ref, coef_ref, x_ref, gate_ref, tbl_ref, o_ref, *scratch):
    slots, (ce_hi_s, ce_lo_s, res_s) = scratch[:N_SLOTS], scratch[N_SLOTS:]
    tt = x_ref.shape[0]
    spread = jnp.where(_iota((PEER_SLOTS, SLOT_WIDTH), 1) // FEAT_CHUNKS == _iota((PEER_SLOTS, SLOT_WIDTH), 0),
                       1.0, 0.0).astype(BF16)
    hi, lo = _split_bf16(coef_ref[...])
    ce_hi_s[...] = _dot(hi, spread)
    ce_lo_s[...] = _dot(lo, spread)
    diag = _chunk_diag()

    def compute(t, slot):
        rows = pltpu.bitcast(slot[...], BF16)
        a_hi = jnp.where(diag, ce_hi_s[pl.ds(t, 1), :], 0.0)
        a_lo = jnp.where(diag, ce_lo_s[pl.ds(t, 1), :], 0.0)
        both = _dot(jnp.concatenate([a_hi, a_lo], axis=0).astype(BF16), rows)
        r0 = pl.multiple_of(t * FEAT_CHUNKS, FEAT_CHUNKS)
        res_s[pl.ds(r0, FEAT_CHUNKS), :] = both[0:FEAT_CHUNKS, :] + both[FEAT_CHUNKS:2 * FEAT_CHUNKS, :]

    _pipelined_tokens(tt, ids_ref, tbl_ref, slots, compute)
    for c in range(FEAT_CHUNKS):
        cols = slice(c * LANES, (c + 1) * LANES)
        y = res_s[pl.ds(c, tt, stride=FEAT_CHUNKS), :]
        o_ref[:, cols] = x_ref[:, cols] + gate_ref[0][:, cols] * y


def _peer_v(ids_flat, coef, x2, gate, table, seq, tt):
    t, d = x2.shape
    per_b = seq // tt
    blk = pl.BlockSpec((tt, d), lambda i: (i, 0))
    return pl.pallas_call(
        _peer_v_kernel,
        grid=(t // tt,),
        in_specs=[
            pl.BlockSpec((tt * PEER_SLOTS,), lambda i: (i,), memory_space=pltpu.SMEM),
            pl.BlockSpec((tt, PEER_SLOTS), lambda i: (i, 0)),
            blk,
            pl.BlockSpec((1, 1, d), lambda i: (i // per_b, 0, 0)),
            _table_spec(table.shape[0]),
        ],
        out_specs=blk,
        out_shape=jax.ShapeDtypeStruct((t, d), F32),
        scratch_shapes=_slot_scratch() + [
                        pltpu.VMEM((tt, SLOT_WIDTH), F32), pltpu.VMEM((tt, SLOT_WIDTH), F32),
                        pltpu.VMEM((tt * FEAT_CHUNKS, LANES), F32)],
        compiler_params=_cparams(1, TABLE_VMEM_LIMIT),
        name="peer_expert_out",
    )(ids_flat, coef, x2, gate, table)


def _peer_ffn(x2, g, shift, scale, gate, wq, sub_keys, table_u, table_v, seq):
    t, d = x2.shape
    tt = PEER_TOKEN_TILE
    h, ids, gates = _peer_route(x2, g, shift, scale, wq.T.astype(BF16), sub_keys, seq)
    ids_flat = ids.reshape(t * PEER_SLOTS)
    coef = _peer_u(ids_flat, h, gates, table_u, tt)
    return _peer_v(ids_flat, coef, x2, gate, table_v, seq, tt)


def _rope_tables(seq):
    half = HEAD_DIM // 2
    inv_freq = ROPE_THETA ** (-jnp.arange(half, dtype=F32) / half)
    ang = jnp.arange(seq).astype(F32)[:, None] * inv_freq[None, :]
    reps = LANES // half
    return jnp.tile(jnp.cos(ang), (1, reps)), jnp.tile(jnp.sin(ang), (1, reps))


def _two_heads(gain):
    return jnp.tile(gain.reshape(1, HEAD_DIM), (1, LANES // HEAD_DIM))


def kernel(x, c, ada_w, ada_b, norm_mix_g, norm_ffn_g, w_in_ab, w_out_ab, sinks_a, qnorm_a, knorm_a,
           w_in_cd, w_out_cd, qnorm_c, knorm_c, qnorm_d, knorm_d, peer_wq, peer_subkeys, peer_u, peer_v):
    b, seq, d = x.shape
    depth = ada_w.shape[0]
    t = b * seq
    cos, sin = _rope_tables(seq)
    mod = _modulation(c, ada_w, ada_b)
    x2 = x.reshape(t, d)
    for layer in range(depth):
        shift_m, scale_m, gate_m, shift_f, scale_f, gate_f = [
            m.reshape(b, 1, d) for m in jnp.split(mod[layer], 6, axis=-1)]
        g_mix = norm_mix_g[layer].reshape(1, d)
        i = layer // 2
        if layer % 2 == 0:
            proj = _norm_proj(x2, g_mix, shift_m, scale_m, w_in_ab[i].astype(BF16), seq)
            proj = proj.reshape(b, seq, -1)
            ya = _swa_attention(proj, sinks_a[i], cos, sin, _two_heads(qnorm_a[i]), _two_heads(knorm_a[i]))
            b_col = (A_Q_HEADS + 2 * A_KV_HEADS) * HEAD_DIM // LANES
            yb = _stick_attention(proj, b_col)
            w_out = w_out_ab[i]
        else:
            proj = _norm_proj(x2, g_mix, shift_m, scale_m, w_in_cd[i].astype(BF16), seq)
            proj = proj.reshape(b, seq, -1)
            ya = _qkv_attention_call(_dilated_kernel, "dilated_attention", proj, 0, C_HEADS, cos, sin,
                                     _two_heads(qnorm_c[i]), _two_heads(knorm_c[i]))
            d_col = 3 * C_HEADS * HEAD_DIM // LANES
            yb = _qkv_attention_call(_moba_kernel, "moba_attention", proj, d_col, D_HEADS, cos, sin,
                                     _two_heads(qnorm_d[i]), _two_heads(knorm_d[i]),
                                     extra_scratch=(pltpu.VMEM((LANES, LANES), F32),
                                                    pltpu.VMEM((seq, LANES), F32),
                                                    pltpu.VMEM((seq, LANES), F32)))
            w_out = w_out_cd[i]
        x2 = _out_proj(x2, ya.reshape(t, -1), yb.reshape(t, -1), w_out.astype(BF16), gate_m, seq)
        x2 = _peer_ffn(x2, norm_ffn_g[layer].reshape(1, d), shift_f, scale_f, gate_f,
                       peer_wq[layer], peer_subkeys[layer],
                       _pack_table(peer_u[layer]), _pack_table(peer_v[layer]), seq)
    return x2.reshape(b, seq, d)
```

```python
import jax
import jax.numpy as jnp
from jax import lax
from jax.experimental import pallas as pl
from jax.experimental.pallas import tpu as pltpu

F32 = jnp.float32
BF16 = jnp.bfloat16
I32 = jnp.int32
U32 = jnp.uint32

HEAD_DIM = 64
ROPE_THETA = 10000.0
NORM_EPS = 1e-6
LANES = 128
QUERY_BLOCK = 128
A_Q_HEADS, A_KV_HEADS = 8, 2
B_HEADS = C_HEADS = D_HEADS = 8
C_PATTERNS = ((128, 1), (512, 4), (2048, 16))
MOBA_BLOCK, MOBA_TOPK = 256, 3
PEER_HEADS, PEER_N_KEYS, PEER_TOPK, PEER_D_KEY = 8, 128, 16, 256
PEER_SLOTS = PEER_HEADS * PEER_TOPK
NEG_BIG = -1e30
PEER_CAND_COUNTS = tuple(PEER_TOPK // (a + 1) for a in range(PEER_TOPK))
PACK_ROWS = 4
MIB = 1024 * 1024
V7X_VMEM_BYTES = 64 * MIB
STREAM_VMEM_LIMIT = 40 * MIB
TABLE_VMEM_LIMIT = V7X_VMEM_BYTES - 8 * MIB


def _cparams(n_axes, vmem_bytes=STREAM_VMEM_LIMIT):
    return pltpu.CompilerParams(
        dimension_semantics=("arbitrary",) * n_axes,
        vmem_limit_bytes=vmem_bytes)


def _split_bf16(a):
    hi = a.astype(BF16)
    lo = (a - hi.astype(F32)).astype(BF16)
    return hi, lo


def _dot(a, b):
    return jnp.dot(a, b, preferred_element_type=F32)


def _dot_nt(a, b):
    return lax.dot_general(a, b, (((1,), (1,)), ((), ())), preferred_element_type=F32)


def _dot3(a, b):
    ah, al = _split_bf16(a)
    bh, bl = _split_bf16(b)
    return _dot(ah, bh) + _dot(ah, bl) + _dot(al, bh)


def _dot3_nt(a, b):
    ah, al = _split_bf16(a)
    bh, bl = _split_bf16(b)
    return _dot_nt(ah, bh) + _dot_nt(ah, bl) + _dot_nt(al, bh)


def _iota(shape, dim):
    return lax.broadcasted_iota(I32, shape, dim)


def _mod_kernel(c_ref, w_ref, b_ref, o_ref):
    c = c_ref[...]
    cond = c * jax.nn.sigmoid(c)
    o_ref[0] = _dot3(cond, w_ref[0]) + b_ref[0]


def _modulation(c, ada_w, ada_b):
    depth, d, n = ada_w.shape
    b = c.shape[0]
    tn = 1024
    return pl.pallas_call(
        _mod_kernel,
        grid=(depth, n // tn),
        in_specs=[
            pl.BlockSpec((b, d), lambda l, j: (0, 0)),
            pl.BlockSpec((1, d, tn), lambda l, j: (l, 0, j)),
            pl.BlockSpec((1, 1, tn), lambda l, j: (l, 0, j)),
        ],
        out_specs=pl.BlockSpec((1, b, tn), lambda l, j: (l, 0, j)),
        out_shape=jax.ShapeDtypeStruct((depth, b, n), F32),
        compiler_params=_cparams(2),
        name="adaln_modulation",
    )(c, ada_w, ada_b.reshape(depth, 1, n))


def _adaln(x, g, shift, scale):
    ms = jnp.mean(x * x, axis=-1, keepdims=True)
    y = x * lax.rsqrt(ms + NORM_EPS) * g
    return y * (1.0 + scale) + shift


def _norm_proj_kernel(x_ref, g_ref, sh_ref, sc_ref, w_ref, o_ref):
    h = _adaln(x_ref[...], g_ref[...], sh_ref[0], sc_ref[0])
    o_ref[...] = _dot(h.astype(BF16), w_ref[...])


def _norm_proj(x2, g, shift, scale, w_bf, seq):
    t, d = x2.shape
    n = w_bf.shape[1]
    tt = 512
    per_b = seq // tt
    return pl.pallas_call(
        _norm_proj_kernel,
        grid=(t // tt,),
        in_specs=[
            pl.BlockSpec((tt, d), lambda i: (i, 0)),
            pl.BlockSpec((1, d), lambda i: (0, 0)),
            pl.BlockSpec((1, 1, d), lambda i: (i // per_b, 0, 0)),
            pl.BlockSpec((1, 1, d), lambda i: (i // per_b, 0, 0)),
            pl.BlockSpec((d, n), lambda i: (0, 0)),
        ],
        out_specs=pl.BlockSpec((tt, n), lambda i: (i, 0)),
        out_shape=jax.ShapeDtypeStruct((t, n), F32),
        compiler_params=_cparams(1),
        name="adaln_in_proj",
    )(x2, g, shift, scale, w_bf)


def _out_proj_kernel(x_ref, ya_ref, yb_ref, w_ref, gate_ref, o_ref):
    half = ya_ref.shape[1]
    y = _dot(ya_ref[...].astype(BF16), w_ref[0:half, :])
    y = y + _dot(yb_ref[...].astype(BF16), w_ref[half:2 * half, :])
    o_ref[...] = x_ref[...] + gate_ref[0] * y


def _out_proj(x2, ya, yb, w_bf, gate, seq):
    t, d = x2.shape
    half = ya.shape[1]
    tt = 512
    per_b = seq // tt
    return pl.pallas_call(
        _out_proj_kernel,
        grid=(t // tt,),
        in_specs=[
            pl.BlockSpec((tt, d), lambda i: (i, 0)),
            pl.BlockSpec((tt, half), lambda i: (i, 0)),
            pl.BlockSpec((tt, half), lambda i: (i, 0)),
            pl.BlockSpec((2 * half, d), lambda i: (0, 0)),
            pl.BlockSpec((1, 1, d), lambda i: (i // per_b, 0, 0)),
        ],
        out_specs=pl.BlockSpec((tt, d), lambda i: (i, 0)),
        out_shape=jax.ShapeDtypeStruct((t, d), F32),
        compiler_params=_cparams(1),
        name="mixer_out_proj",
    )(x2, ya, yb, w_bf, gate)


def _lane_row():
    return _iota((1, LANES), 1)


def _head_segment_ones():
    r = _iota((LANES, LANES), 0) // HEAD_DIM
    c = _iota((LANES, LANES), 1) // HEAD_DIM
    return jnp.where(r == c, 1.0, 0.0).astype(BF16)


def _headnorm_rope(a, g, cos, sin):
    hi, lo = _split_bf16(a * a)
    seg = _head_segment_ones()
    ms = (_dot(hi, seg) + _dot(lo, seg)) * (1.0 / HEAD_DIM)
    y = a * lax.rsqrt(ms + NORM_EPS) * g
    half = HEAD_DIM // 2
    upper = pltpu.roll(y, LANES - half, axis=1)
    lower = pltpu.roll(y, half, axis=1)
    first_half = (_lane_row() % HEAD_DIM) < half
    rot = jnp.where(first_half, -upper, lower)
    return y * cos + rot * sin


def _head_masks():
    lane = _lane_row()
    return lane < HEAD_DIM, lane >= HEAD_DIM


def _attn_specs(seq, qcol, kcol, vcol, kv_shared):
    blk = (1, seq, LANES)
    q_spec = pl.BlockSpec(blk, lambda b, p: (b, 0, qcol + p))
    if kv_shared:
        k_spec = pl.BlockSpec(blk, lambda b, p: (b, 0, kcol))
        v_spec = pl.BlockSpec(blk, lambda b, p: (b, 0, vcol))
    else:
        k_spec = pl.BlockSpec(blk, lambda b, p: (b, 0, kcol + p))
        v_spec = pl.BlockSpec(blk, lambda b, p: (b, 0, vcol + p))
    return q_spec, k_spec, v_spec


def _row_spec(seq):
    return pl.BlockSpec((seq, LANES), lambda b, p: (0, 0))


def _gain_spec():
    return pl.BlockSpec((1, LANES), lambda b, p: (0, 0))


def _store_heads(o_ref, r0, outs):
    first, _ = _head_masks()
    o_ref[0, pl.ds(r0, QUERY_BLOCK), :] = jnp.where(first, outs[0], outs[1])


SWA_QBLOCKS_PER_STEP = 4


def _swa_kernel(sinks_ref, q_ref, k_ref, v_ref, cos_ref, sin_ref, gq_ref, gk_ref, o_ref,
                q0_s, q1_s, k_s, v_s):
    p = pl.program_id(1)
    seq = q_ref.shape[1]
    cos, sin = cos_ref[...], sin_ref[...]
    first, second = _head_masks()
    qn = _headnorm_rope(q_ref[0], gq_ref[...], cos, sin) * (HEAD_DIM ** -0.5)
    q0_s[...] = jnp.where(first, qn, 0.0).astype(BF16)
    q1_s[...] = jnp.where(second, qn, 0.0).astype(BF16)
    pairs_per_kv = (A_Q_HEADS // A_KV_HEADS) // 2
    keep = jnp.logical_xor(first, (p // pairs_per_kv) == 1)
    kn = _headnorm_rope(k_ref[0], gk_ref[...], cos, sin)
    k_s[...] = jnp.where(keep, kn, pltpu.roll(kn, HEAD_DIM, axis=1)).astype(BF16)
    v = v_ref[0]
    v_s[...] = jnp.where(keep, v, pltpu.roll(v, HEAD_DIM, axis=1)).astype(BF16)

    qi = _iota((QUERY_BLOCK, QUERY_BLOCK), 0)
    ki = _iota((QUERY_BLOCK, QUERY_BLOCK), 1)

    mask_c = ki <= qi
    per_step = min(SWA_QBLOCKS_PER_STEP, seq // QUERY_BLOCK)

    def qstep(step, carry):
        chains, scores = [], []
        for u in range(per_step):
            i = step * per_step + u
            r0 = pl.multiple_of(i * QUERY_BLOCK, QUERY_BLOCK)
            rp = pl.multiple_of(jnp.maximum(i - 1, 0) * QUERY_BLOCK, QUERY_BLOCK)
            kc, kp = k_s[pl.ds(r0, QUERY_BLOCK), :], k_s[pl.ds(rp, QUERY_BLOCK), :]
            vc, vp = v_s[pl.ds(r0, QUERY_BLOCK), :], v_s[pl.ds(rp, QUERY_BLOCK), :]
            mask_p = jnp.logical_and(ki > qi, i > 0)
            chains.append((r0, vc, vp, mask_p))
            for q_s in (q0_s, q1_s):
                qh = q_s[pl.ds(r0, QUERY_BLOCK), :]
                scores.append((_dot_nt(qh, kc), _dot_nt(qh, kp)))
        probs, denoms = [], []
        for n, (s_cur, s_prev) in enumerate(scores):
            mask_p = chains[n // 2][3]
            sc = jnp.where(mask_c, s_cur, NEG_BIG)
            sp = jnp.where(mask_p, s_prev, NEG_BIG)
            sink = sinks_ref[2 * p + n % 2]
            m = jnp.maximum(jnp.max(sc, axis=1, keepdims=True), jnp.max(sp, axis=1, keepdims=True))
            m = jnp.maximum(m, sink)
            ec, ep = jnp.exp(sc - m), jnp.exp(sp - m)
            denoms.append(jnp.sum(ec, axis=1, keepdims=True) + jnp.sum(ep, axis=1, keepdims=True)
                          + jnp.exp(sink - m))
            probs.append((ec.astype(BF16), ep.astype(BF16)))
        for u, (r0, vc, vp, _) in enumerate(chains):
            outs = [(_dot(probs[2 * u + hh][0], vc) + _dot(probs[2 * u + hh][1], vp)) / denoms[2 * u + hh]
                    for hh in range(2)]
            _store_heads(o_ref, r0, outs)
        return carry

    lax.fori_loop(0, seq // (QUERY_BLOCK * per_step), qstep, 0)


def _swa_attention(proj, sinks, cos, sin, gq, gk):
    b, seq, _ = proj.shape
    n_pairs = A_Q_HEADS // 2
    kcol = A_Q_HEADS * HEAD_DIM // LANES
    vcol = kcol + A_KV_HEADS * HEAD_DIM // LANES
    q_spec, k_spec, v_spec = _attn_specs(seq, 0, kcol, vcol, True)
    return pl.pallas_call(
        _swa_kernel,
        grid=(b, n_pairs),
        in_specs=[pl.BlockSpec(memory_space=pltpu.SMEM), q_spec, k_spec, v_spec,
                  _row_spec(seq), _row_spec(seq), _gain_spec(), _gain_spec()],
        out_specs=pl.BlockSpec((1, seq, LANES), lambda b_, p: (b_, 0, p)),
        out_shape=jax.ShapeDtypeStruct((b, seq, n_pairs * LANES), F32),
        scratch_shapes=[pltpu.VMEM((seq, LANES), BF16)] * 4,
        compiler_params=_cparams(2),
        name="swa_gqa_attention",
    )(sinks, proj, proj, proj, cos, sin, gq, gk)


STICK_GROUP = 4
STICK_QUERY_ROWS = 256


def _stick_kernel(q_ref, k_ref, v_ref, o_ref, k_s, v_s):
    seq = q_ref.shape[1]
    k_s[...] = k_ref[0].astype(BF16)
    v_s[...] = v_ref[0].astype(BF16)
    first, second = _head_masks()
    qr = min(STICK_QUERY_ROWS, seq)
    kw = QUERY_BLOCK
    qi = _iota((qr, kw), 0)
    ki = _iota((qr, kw), 1)
    wr = _iota((2 * kw, 2 * kw), 0) % kw
    wc = _iota((2 * kw, 2 * kw), 1)
    suffix_w = jnp.where(jnp.logical_or(wc >= kw, wr > wc), 1.0, 0.0).astype(BF16)

    def qblock(i, carry):
        r0 = pl.multiple_of(i * qr, qr)
        q = q_ref[0, pl.ds(r0, qr), :] * (HEAD_DIM ** -0.5)
        qhs = [jnp.where(msk, q, 0.0).astype(BF16) for msk in (first, second)]
        n_blocks = (r0 + qr) // kw

        def kgroup(g, st):
            accs, laters = [st[0], st[1]], [st[2], st[3]]
            chains = [(u, hh) for u in range(STICK_GROUP) for hh in range(2)]
            vbs, pasts, zs = [], [], {}
            for u in range(STICK_GROUP):
                j = n_blocks - 1 - (g * STICK_GROUP + u)
                live = j >= 0
                c0 = pl.multiple_of(jnp.maximum(j, 0) * kw, kw)
                kb = k_s[pl.ds(c0, kw), :]
                vbs.append(v_s[pl.ds(c0, kw), :])
                pasts.append(jnp.logical_and((c0 + ki) < (r0 + qi), live))
                for hh in range(2):
                    zs[u, hh] = _dot_nt(qhs[hh], kb)
            logit, sums = {}, {}
            for u, hh in chains:
                z = zs[u, hh]
                sp = jnp.maximum(z, 0.0) + jnp.log(1.0 + jnp.exp(-jnp.abs(z)))
                log_keep = jnp.where(pasts[u], -sp, 0.0)
                logit[u, hh] = z - sp
                hi, lo = _split_bf16(log_keep)
                sums[u, hh] = _dot(jnp.concatenate([hi, lo], axis=1), suffix_w)
            ws = {}
            for u, hh in chains:
                inner, total = sums[u, hh][:, :kw], sums[u, hh][:, kw:]
                ws[u, hh] = jnp.where(pasts[u], jnp.exp(logit[u, hh] + inner + laters[hh]), 0.0).astype(BF16)
                laters[hh] = laters[hh] + total
            for u, hh in chains:
                accs[hh] = accs[hh] + _dot(ws[u, hh], vbs[u])
            return accs[0], accs[1], laters[0], laters[1]

        zero = jnp.zeros((qr, LANES), F32)
        n_groups = (n_blocks + STICK_GROUP - 1) // STICK_GROUP
        st = lax.fori_loop(0, n_groups, kgroup, (zero, zero, zero, zero))
        o_ref[0, pl.ds(r0, qr), :] = jnp.where(first, st[0], st[1])
        return carry

    lax.fori_loop(0, seq // qr, qblock, 0)


def _stick_attention(proj, qcol):
    b, seq, _ = proj.shape
    n_pairs = B_HEADS // 2
    q_spec, k_spec, v_spec = _attn_specs(seq, qcol, qcol + n_pairs, qcol + 2 * n_pairs, False)
    return pl.pallas_call(
        _stick_kernel,
        grid=(b, n_pairs),
        in_specs=[q_spec, k_spec, v_spec],
        out_specs=pl.BlockSpec((1, seq, LANES), lambda b_, p: (b_, 0, p)),
        out_shape=jax.ShapeDtypeStruct((b, seq, n_pairs * LANES), F32),
        scratch_shapes=[pltpu.VMEM((seq, LANES), BF16)] * 2,
        compiler_params=_cparams(2),
        name="stick_breaking_attention",
    )(proj, proj, proj)


def _prep_qkv(q_ref, k_ref, v_ref, cos_ref, sin_ref, gq_ref, gk_ref, q0_s, q1_s, k_s, v_s):
    cos, sin = cos_ref[...], sin_ref[...]
    first, second = _head_masks()
    qn = _headnorm_rope(q_ref[0], gq_ref[...], cos, sin) * (HEAD_DIM ** -0.5)
    q0_s[...] = jnp.where(first, qn, 0.0).astype(BF16)
    q1_s[...] = jnp.where(second, qn, 0.0).astype(BF16)
    kn = _headnorm_rope(k_ref[0], gk_ref[...], cos, sin)
    k_s[...] = kn.astype(BF16)
    v_s[...] = v_ref[0].astype(BF16)
    return qn, kn


DILATED_KEY_TILE = 512
DILATED_QUERY_ROWS = 256


def _dilated_kernel(q_ref, k_ref, v_ref, cos_ref, sin_ref, gq_ref, gk_ref, o_ref,
                    q0_s, q1_s, k_s, v_s):
    seq = q_ref.shape[1]
    _prep_qkv(q_ref, k_ref, v_ref, cos_ref, sin_ref, gq_ref, gk_ref, q0_s, q1_s, k_s, v_s)
    kt = min(DILATED_KEY_TILE, seq)
    qr = min(DILATED_QUERY_ROWS, seq)
    qk = _iota((qr, kt), 0) - _iota((qr, kt), 1)
    on_stride = [jnp.where((qk & (dil - 1)) == 0, 1.0, 0.0) for _, dil in C_PATTERNS]
    first_head, _ = _head_masks()

    def qblock(i, carry):
        r0 = pl.multiple_of(i * qr, qr)
        qhs = [q_s[pl.ds(r0, qr), :] for q_s in (q0_s, q1_s)]

        def ktile(g, st):
            c0 = pl.multiple_of(g * kt, kt)
            d = (r0 - c0) + qk
            count = jnp.zeros(d.shape, F32)
            for (window, _), stride_ok in zip(C_PATTERNS, on_stride):
                count = count + jnp.where(d <= window, stride_ok, 0.0)
            count = jnp.where(d >= 0, count, 0.0)
            kb, vb = k_s[pl.ds(c0, kt), :], v_s[pl.ds(c0, kt), :]
            scores = [_dot_nt(qhs[hh], kb) for hh in range(2)]
            new, prs = [], []
            for hh in range(2):
                m, l, acc = st[3 * hh:3 * hh + 3]
                s = jnp.where(count > 0.0, scores[hh], NEG_BIG)
                m_new = jnp.maximum(m, jnp.max(s, axis=1, keepdims=True))
                pr = count * jnp.exp(s - m_new)
                alpha = jnp.exp(m - m_new)
                new += [m_new, alpha * l + jnp.sum(pr, axis=1, keepdims=True), alpha * acc]
                prs.append(pr.astype(BF16))
            for hh in range(2):
                new[3 * hh + 2] = new[3 * hh + 2] + _dot(prs[hh], vb)
            return tuple(new)

        init = (jnp.full((qr, 1), NEG_BIG, F32), jnp.zeros((qr, 1), F32),
                jnp.zeros((qr, LANES), F32)) * 2
        st = lax.fori_loop(0, (r0 + qr + kt - 1) // kt, ktile, init)
        o_ref[0, pl.ds(r0, qr), :] = jnp.where(first_head, st[2] / st[1], st[5] / st[4])
        return carry

    lax.fori_loop(0, seq // qr, qblock, 0)


def _qkv_attention_call(kernel, name, proj, qcol, n_heads, cos, sin, gq, gk, extra_scratch=()):
    b, seq, _ = proj.shape
    n_pairs = n_heads // 2
    q_spec, k_spec, v_spec = _attn_specs(seq, qcol, qcol + n_pairs, qcol + 2 * n_pairs, False)
    return pl.pallas_call(
        kernel,
        grid=(b, n_pairs),
        in_specs=[q_spec, k_spec, v_spec, _row_spec(seq), _row_spec(seq), _gain_spec(), _gain_spec()],
        out_specs=pl.BlockSpec((1, seq, LANES), lambda b_, p: (b_, 0, p)),
        out_shape=jax.ShapeDtypeStruct((b, seq, n_pairs * LANES), F32),
        scratch_shapes=[pltpu.VMEM((seq, LANES), BF16)] * 4 + list(extra_scratch),
        compiler_params=_cparams(2),
        name=name,
    )(proj, proj, proj, cos, sin, gq, gk)


def _moba_kernel(q_ref, k_ref, v_ref, cos_ref, sin_ref, gq_ref, gk_ref, o_ref,
                 q0_s, q1_s, k_s, v_s, km_s, sel0_s, sel1_s):
    seq = q_ref.shape[1]
    n_blocks = seq // MOBA_BLOCK
    qn, kn = _prep_qkv(q_ref, k_ref, v_ref, cos_ref, sin_ref, gq_ref, gk_ref, q0_s, q1_s, k_s, v_s)
    km_s[...] = jnp.zeros(km_s.shape, F32)
    km_s[0:n_blocks, :] = jnp.mean(kn.reshape(n_blocks, MOBA_BLOCK, LANES), axis=1)
    first, second = _head_masks()

    rows8 = _iota((8, seq), 0)
    own8 = _iota((8, seq), 1) // MOBA_BLOCK
    valid = rows8 < own8
    for msk, sel_s in ((first, sel0_s), (second, sel1_s)):
        gate = _dot3_nt(km_s[...], jnp.where(msk, qn, 0.0))[0:8, :]
        gm = jnp.where(valid, gate, -jnp.inf)
        rank = jnp.zeros((8, seq), F32)
        for n2 in range(n_blocks):
            g2 = gm[n2:n2 + 1, :]
            beats = jnp.logical_or(g2 > gm, jnp.logical_and(g2 == gm, n2 < rows8))
            rank = rank + jnp.where(jnp.logical_and(beats, n2 < own8), 1.0, 0.0)
        sel = jnp.where(jnp.logical_and(valid, rank < float(MOBA_TOPK)), 1.0, 0.0)
        sel = jnp.concatenate([sel, jnp.zeros((LANES - 8, seq), F32)], axis=0)
        sel_s[...] = sel.T

    qrows = MOBA_BLOCK
    lane_sq = _iota((qrows, LANES), 1)
    causal = _iota((qrows, MOBA_BLOCK), 1) <= _iota((qrows, MOBA_BLOCK), 0)
    second_block = _iota((qrows, 2 * MOBA_BLOCK), 1) >= MOBA_BLOCK
    first_head, _ = _head_masks()

    def qblock(own, carry):
        r0 = pl.multiple_of(own * qrows, qrows)
        qhs = [q_s[pl.ds(r0, qrows), :] for q_s in (q0_s, q1_s)]
        sels = [sel_s[pl.ds(r0, qrows), :] for sel_s in (sel0_s, sel1_s)]
        kb, vb = k_s[pl.ds(r0, MOBA_BLOCK), :], v_s[pl.ds(r0, MOBA_BLOCK), :]
        scores = [_dot_nt(qhs[hh], kb) for hh in range(2)]
        init, prs = [], []
        for hh in range(2):
            s = jnp.where(causal, scores[hh], NEG_BIG)
            m = jnp.max(s, axis=1, keepdims=True)
            pr = jnp.exp(s - m)
            init += [m, jnp.sum(pr, axis=1, keepdims=True), None]
            prs.append(pr.astype(BF16))
        for hh in range(2):
            init[3 * hh + 2] = _dot(prs[hh], vb)

        def kpair(g, st):
            c0 = pl.multiple_of(g * 2 * MOBA_BLOCK, 2 * MOBA_BLOCK)
            kb2, vb2 = k_s[pl.ds(c0, 2 * MOBA_BLOCK), :], v_s[pl.ds(c0, 2 * MOBA_BLOCK), :]
            scores = [_dot_nt(qhs[hh], kb2) for hh in range(2)]
            new, prs = [], []
            for hh in range(2):
                m, l, acc = st[3 * hh:3 * hh + 3]
                sel_a = jnp.sum(jnp.where(lane_sq == 2 * g, sels[hh], 0.0), axis=1, keepdims=True)
                sel_b = jnp.sum(jnp.where(lane_sq == 2 * g + 1, sels[hh], 0.0), axis=1, keepdims=True)
                keep = jnp.where(second_block, sel_b, sel_a) > 0.0
                s = jnp.where(keep, scores[hh], NEG_BIG)
                m_new = jnp.maximum(m, jnp.max(s, axis=1, keepdims=True))
                pr = jnp.exp(s - m_new)
                alpha = jnp.exp(m - m_new)
                new += [m_new, alpha * l + jnp.sum(pr, axis=1, keepdims=True), alpha * acc]
                prs.append(pr.astype(BF16))
            for hh in range(2):
                new[3 * hh + 2] = new[3 * hh + 2] + _dot(prs[hh], vb2)
            return tuple(new)

        st = lax.fori_loop(0, (own + 1) // 2, kpair, tuple(init))
        o_ref[0, pl.ds(r0, qrows), :] = jnp.where(first_head, st[2] / st[1], st[5] / st[4])
        return carry

    lax.fori_loop(0, seq // qrows, qblock, 0)


ROUTE_CHUNKS_PER_STEP = 4


def _oddeven_merge_sort_pairs(n):
    pairs = []

    def merge(lo, hi, r):
        step = 2 * r
        if step < hi - lo:
            merge(lo, hi, step)
            merge(lo + r, hi, step)
            pairs.extend((i, i + r) for i in range(lo + r, hi - r, step))
        else:
            pairs.append((lo, lo + r))

    def sort(lo, hi):
        if hi - lo >= 1:
            mid = lo + (hi - lo) // 2
            sort(lo, mid)
            sort(mid + 1, hi)
            merge(lo, hi, 1)

    sort(0, n - 1)
    return pairs


SUBLANES = 8


def _top16_rows(scores, n_rows, vals_refs, idx_refs):
    n_slabs = n_rows // SUBLANES
    sub = _iota((SUBLANES, LANES), 0)
    vals = [[s[SUBLANES * v:SUBLANES * (v + 1), :] for v in range(n_slabs)] for s in scores]
    idxs = [[sub + SUBLANES * v for v in range(n_slabs)] for _ in scores]
    for i, j in _oddeven_merge_sort_pairs(n_slabs):
        for va, ia in zip(vals, idxs):
            a, b = va[i], va[j]
            a_first = jnp.logical_or(a > b, jnp.logical_and(a == b, ia[i] < ia[j]))
            va[i], va[j] = jnp.maximum(a, b), jnp.minimum(a, b)
            ia[i], ia[j] = jnp.where(a_first, ia[i], ia[j]), jnp.where(a_first, ia[j], ia[i])
    for it in range(PEER_TOPK):
        for k, (va, ia) in enumerate(zip(vals, idxs)):
            m = jnp.max(va[0], axis=0, keepdims=True)
            pick = jnp.min(jnp.where(va[0] == m, ia[0], n_rows), axis=0, keepdims=True)
            vals_refs[k][it:it + 1, :] = m
            idx_refs[k][it:it + 1, :] = pick
            win = ia[0] == pick
            depth = PEER_TOPK - 1 - it
            for d in range(min(depth, n_slabs - 1)):
                va[d] = jnp.where(win, va[d + 1], va[d])
                ia[d] = jnp.where(win, ia[d + 1], ia[d])
            if depth >= n_slabs:
                va[n_slabs - 1] = jnp.where(win, -jnp.inf, va[n_slabs - 1])


def _peer_route_kernel(x_ref, g_ref, sh_ref, sc_ref, wqt_ref, sk_ref, h_ref, ids_ref, gts_ref,
                       q_s, val_s, idx_s, ids_s, gts_s):
    tt = x_ref.shape[0]
    n_chunks = tt // LANES
    per_step = ROUTE_CHUNKS_PER_STEP
    half = PEER_D_KEY // 2
    h = _adaln(x_ref[...], g_ref[...], sh_ref[0], sc_ref[0])
    h_ref[...] = h
    qt = _dot_nt(wqt_ref[...], h.astype(BF16))
    for c in range(n_chunks):
        q_s[c] = qt[:, c * LANES:(c + 1) * LANES]
    sk1, sk2 = sk_ref[0], sk_ref[1]
    col_id = _iota((PEER_TOPK, LANES), 0)
    sub = _iota((SUBLANES, LANES), 0)
    col_depth = jnp.zeros((SUBLANES, LANES), I32)
    for a in range(SUBLANES):
        col_depth = jnp.where(sub == a, PEER_CAND_COUNTS[a], col_depth)
    vals = [val_s.at[i] for i in range(2 * per_step)]
    idxs = [idx_s.at[i] for i in range(2 * per_step)]

    def body(step, carry):
        hh = step // (n_chunks // per_step)
        c0 = (step % (n_chunks // per_step)) * per_step
        q0 = pl.multiple_of(hh * PEER_D_KEY, PEER_D_KEY)
        scores = []
        for k in range(per_step):
            scores.append(_dot3(sk1, q_s[c0 + k, pl.ds(q0, half), :]))
            scores.append(_dot3(sk2, q_s[c0 + k, pl.ds(q0 + half, half), :]))
        _top16_rows(scores, PEER_N_KEYS, vals, idxs)
        state = []
        for k in range(per_step):
            v1, i1, v2, i2 = vals[2 * k], idxs[2 * k], vals[2 * k + 1], idxs[2 * k + 1]
            v1x, e1x = v1[0:SUBLANES, :], i1[0:SUBLANES, :] * PEER_N_KEYS
            xs = [jnp.where(col_depth > b, v1x + v2[b:b + 1, :], -jnp.inf) for b in range(PEER_TOPK)]
            ex = [e1x + i2[b:b + 1, :] for b in range(PEER_TOPK)]
            y = v1[SUBLANES:PEER_TOPK, :] + v2[0:1, :]
            ey = i1[SUBLANES:PEER_TOPK, :] * PEER_N_KEYS + i2[0:1, :]
            state.append([xs, ex, y, ey])
        for it in range(PEER_TOPK):
            for k in range(per_step):
                xs, ex, y, ey = state[k]
                heads = jnp.concatenate([xs[0], y], axis=0)
                m = jnp.max(heads, axis=0, keepdims=True)
                pick = jnp.min(jnp.where(heads == m, col_id, PEER_TOPK), axis=0, keepdims=True)
                win = col_id == pick
                eids = jnp.where(win, jnp.concatenate([ex[0], ey], axis=0), 0)
                idxs[2 * k][it:it + 1, :] = jnp.sum(eids, axis=0, keepdims=True)
                vals[2 * k][it:it + 1, :] = m
                win_x, win_y = win[0:SUBLANES, :], win[SUBLANES:PEER_TOPK, :]
                for dd in range(PEER_TOPK - 1 - it):
                    xs[dd] = jnp.where(win_x, xs[dd + 1], xs[dd])
                    ex[dd] = jnp.where(win_x, ex[dd + 1], ex[dd])
                state[k][2] = jnp.where(win_y, -jnp.inf, y)
        r0 = pl.multiple_of(hh * PEER_TOPK, PEER_TOPK)
        for k in range(per_step):
            top = vals[2 * k][...]
            e = jnp.exp(top - top[0:1, :])
            gts_s[c0 + k, pl.ds(r0, PEER_TOPK), :] = e / jnp.sum(e, axis=0, keepdims=True)
            ids_s[c0 + k, pl.ds(r0, PEER_TOPK), :] = idxs[2 * k][...] * PACK_ROWS
        return carry

    lax.fori_loop(0, PEER_HEADS * n_chunks // per_step, body, 0)
    for c in range(n_chunks):
        rows = slice(c * LANES, (c + 1) * LANES)
        ids_ref[rows, :] = ids_s[c].T
        gts_ref[rows, :] = gts_s[c].T


def _peer_route(x2, g, shift, scale, wqt_bf, sub_keys, seq):
    t, d = x2.shape
    tt = 512
    per_b = seq // tt
    n_chunks = tt // LANES
    nq = wqt_bf.shape[0]
    out_blk = pl.BlockSpec((tt, PEER_SLOTS), lambda i: (i, 0))
    return pl.pallas_call(
        _peer_route_kernel,
        grid=(t // tt,),
        in_specs=[
            pl.BlockSpec((tt, d), lambda i: (i, 0)),
            pl.BlockSpec((1, d), lambda i: (0, 0)),
            pl.BlockSpec((1, 1, d), lambda i: (i // per_b, 0, 0)),
            pl.BlockSpec((1, 1, d), lambda i: (i // per_b, 0, 0)),
            pl.BlockSpec((nq, d), lambda i: (0, 0)),
            pl.BlockSpec(sub_keys.shape, lambda i: (0, 0, 0)),
        ],
        out_specs=[pl.BlockSpec((tt, d), lambda i: (i, 0)), out_blk, out_blk],
        out_shape=[jax.ShapeDtypeStruct((t, d), F32),
                   jax.ShapeDtypeStruct((t, PEER_SLOTS), I32),
                   jax.ShapeDtypeStruct((t, PEER_SLOTS), F32)],
        scratch_shapes=[pltpu.VMEM((n_chunks, nq, LANES), F32),
                        pltpu.VMEM((2 * ROUTE_CHUNKS_PER_STEP, PEER_TOPK, LANES), F32),
                        pltpu.VMEM((2 * ROUTE_CHUNKS_PER_STEP, PEER_TOPK, LANES), I32),
                        pltpu.VMEM((n_chunks, PEER_SLOTS, LANES), I32),
                        pltpu.VMEM((n_chunks, PEER_SLOTS, LANES), F32)],
        compiler_params=_cparams(1),
        name="peer_route",
    )(x2, g, shift, scale, wqt_bf, sub_keys)


def _pack_table(tab):
    e, d = tab.shape
    bits = lax.bitcast_convert_type(tab.astype(BF16), jnp.uint16).astype(U32)
    bits = bits.reshape(e, d // (2 * LANES), 2, LANES)
    words = bits[:, :, 0, :] | (bits[:, :, 1, :] << 16)
    return words.reshape(e * (d // (2 * LANES)), LANES)


def _table_spec(rows):
    return pl.BlockSpec((rows, LANES), lambda i: (0, 0), pipeline_mode=pl.Buffered(1))


def _gelu_exact(a):
    return 0.5 * a * (1.0 + lax.erf(a * (2.0 ** -0.5)))


FEAT_CHUNKS = 8
SLOT_WIDTH = PEER_SLOTS * FEAT_CHUNKS


def _gather_rows(ids_ref, base, tbl_ref, slot):
    for j in range(PEER_SLOTS):
        if j % ID_VIEW == 0:
            ids_part = ids_ref.at[pl.ds(base + j, ID_VIEW)]
        row0 = pl.multiple_of(ids_part[j % ID_VIEW], PACK_ROWS)
        slot[PACK_ROWS * j:PACK_ROWS * (j + 1), :] = tbl_ref[pl.ds(row0, PACK_ROWS), :]


def _pipelined_tokens(tt, ids_ref, tbl_ref, slots, compute):
    group = len(slots)
    n_steps = tt // group

    def gather(t, slot):
        _gather_rows(ids_ref, t * PEER_SLOTS, tbl_ref, slot)

    for k in range(group):
        gather(k, slots[k])

    def step(q, carry):
        t = group * q
        for k in range(group):
            compute(t + k, slots[k])
        for k in range(group):
            gather(t + group + k, slots[k])
        return carry

    lax.fori_loop(0, n_steps - 1, step, 0)
    for k in range(group):
        compute((n_steps - 1) * group + k, slots[k])


def _chunk_diag():
    return (_iota((FEAT_CHUNKS, SLOT_WIDTH), 1) % FEAT_CHUNKS) == _iota((FEAT_CHUNKS, SLOT_WIDTH), 0)


def _peer_u_kernel(ids_ref, h_ref, g_ref, tbl_ref, coef_ref, *scratch):
    slots, (hx_s, rs_s) = scratch[:N_SLOTS], scratch[N_SLOTS:]
    tt = h_ref.shape[0]
    for c in range(FEAT_CHUNKS):
        hx_s[pl.ds(c, tt, stride=FEAT_CHUNKS), :] = h_ref[:, c * LANES:(c + 1) * LANES]
    diag = _chunk_diag()

    def compute(t, slot):
        rows = pltpu.bitcast(slot[...], BF16)
        x8 = hx_s[pl.ds(pl.multiple_of(t * FEAT_CHUNKS, FEAT_CHUNKS), FEAT_CHUNKS), :]
        part = _dot_nt(x8.astype(BF16), rows)
        rs_s[pl.ds(t, 1), :] = jnp.sum(jnp.where(diag, part, 0.0), axis=0, keepdims=True)

    _pipelined_tokens(tt, ids_ref, tbl_ref, slots, compute)
    group = jnp.where(_iota((SLOT_WIDTH, PEER_SLOTS), 0) // FEAT_CHUNKS == _iota((SLOT_WIDTH, PEER_SLOTS), 1),
                      1.0, 0.0).astype(BF16)
    hi, lo = _split_bf16(rs_s[...])
    act = _dot(hi, group) + _dot(lo, group)
    coef_ref[...] = g_ref[...] * _gelu_exact(act)


N_SLOTS = 16
ID_VIEW = 16
PEER_TOKEN_TILE = 512


def _slot_scratch():
    return [pltpu.VMEM((PEER_SLOTS * PACK_ROWS, LANES), U32)] * N_SLOTS


def _peer_u(ids_flat, h, gates, table, tt):
    t, d = h.shape
    return pl.pallas_call(
        _peer_u_kernel,
        grid=(t // tt,),
        in_specs=[
            pl.BlockSpec((tt * PEER_SLOTS,), lambda i: (i,), memory_space=pltpu.SMEM),
            pl.BlockSpec((tt, d), lambda i: (i, 0)),
            pl.BlockSpec((tt, PEER_SLOTS), lambda i: (i, 0)),
            _table_spec(table.shape[0]),
        ],
        out_specs=pl.BlockSpec((tt, PEER_SLOTS), lambda i: (i, 0)),
        out_shape=jax.ShapeDtypeStruct((t, PEER_SLOTS), F32),
        scratch_shapes=_slot_scratch() + [
                        pltpu.VMEM((tt * FEAT_CHUNKS, LANES), F32),
                        pltpu.VMEM((tt, SLOT_WIDTH), F32)],
        compiler_params=_cparams(1, TABLE_VMEM_LIMIT),
        name="peer_expert_in",
    )(ids_flat, h, gates, table)


def _peer_v_kernel(ids_ref, coef_ref, x_ref, gate_ref, tbl_ref, o_ref, *scratch):
    slots, (ce_hi_s, ce_lo_s, res_s) = scratch[:N_SLOTS], scratch[N_SLOTS:]
    tt = x_ref.shape[0]
    spread = jnp.where(_iota((PEER_SLOTS, SLOT_WIDTH), 1) // FEAT_CHUNKS == _iota((PEER_SLOTS, SLOT_WIDTH), 0),
                       1.0, 0.0).astype(BF16)
    hi, lo = _split_bf16(coef_ref[...])
    ce_hi_s[...] = _dot(hi, spread)
    ce_lo_s[...] = _dot(lo, spread)
    diag = _chunk_diag()

    def compute(t, slot):
        rows = pltpu.bitcast(slot[...], BF16)
        a_hi = jnp.where(diag, ce_hi_s[pl.ds(t, 1), :], 0.0)
        a_lo = jnp.where(diag, ce_lo_s[pl.ds(t, 1), :], 0.0)
        both = _dot(jnp.concatenate([a_hi, a_lo], axis=0).astype(BF16), rows)
        r0 = pl.multiple_of(t * FEAT_CHUNKS, FEAT_CHUNKS)
        res_s[pl.ds(r0, FEAT_CHUNKS), :] = both[0:FEAT_CHUNKS, :] + both[FEAT_CHUNKS:2 * FEAT_CHUNKS, :]

    _pipelined_tokens(tt, ids_ref, tbl_ref, slots, compute)
    for c in range(FEAT_CHUNKS):
        cols = slice(c * LANES, (c + 1) * LANES)
        y = res_s[pl.ds(c, tt, stride=FEAT_CHUNKS), :]
        o_ref[:, cols] = x_ref[:, cols] + gate_ref[0][:, cols] * y


def _peer_v(ids_flat, coef, x2, gate, table, seq, tt):
    t, d = x2.shape
    per_b = seq // tt
    blk = pl.BlockSpec((tt, d), lambda i: (i, 0))
    return pl.pallas_call(
        _peer_v_kernel,
        grid=(t // tt,),
        in_specs=[
            pl.BlockSpec((tt * PEER_SLOTS,), lambda i: (i,), memory_space=pltpu.SMEM),
            pl.BlockSpec((tt, PEER_SLOTS), lambda i: (i, 0)),
            blk,
            pl.BlockSpec((1, 1, d), lambda i: (i // per_b, 0, 0)),
            _table_spec(table.shape[0]),
        ],
        out_specs=blk,
        out_shape=jax.ShapeDtypeStruct((t, d), F32),
        scratch_shapes=_slot_scratch() + [
                        pltpu.VMEM((tt, SLOT_WIDTH), F32), pltpu.VMEM((tt, SLOT_WIDTH), F32),
                        pltpu.VMEM((tt * FEAT_CHUNKS, LANES), F32)],
        compiler_params=_cparams(1, TABLE_VMEM_LIMIT),
        name="peer_expert_out",
    )(ids_flat, coef, x2, gate, table)


def _peer_ffn(x2, g, shift, scale, gate, wq, sub_keys, table_u, table_v, seq):
    t, d = x2.shape
    tt = PEER_TOKEN_TILE
    h, ids, gates = _peer_route(x2, g, shift, scale, wq.T.astype(BF16), sub_keys, seq)
    ids_flat = ids.reshape(t * PEER_SLOTS)
    coef = _peer_u(ids_flat, h, gates, table_u, tt)
    return _peer_v(ids_flat, coef, x2, gate, table_v, seq, tt)


def _rope_tables(seq):
    half = HEAD_DIM // 2
    inv_freq = ROPE_THETA ** (-jnp.arange(half, dtype=F32) / half)
    ang = jnp.arange(seq).astype(F32)[:, None] * inv_freq[None, :]
    reps = LANES // half
    return jnp.tile(jnp.cos(ang), (1, reps)), jnp.tile(jnp.sin(ang), (1, reps))


def _two_heads(gain):
    return jnp.tile(gain.reshape(1, HEAD_DIM), (1, LANES // HEAD_DIM))


def kernel(x, c, ada_w, ada_b, norm_mix_g, norm_ffn_g, w_in_ab, w_out_ab, sinks_a, qnorm_a, knorm_a,
           w_in_cd, w_out_cd, qnorm_c, knorm_c, qnorm_d, knorm_d, peer_wq, peer_subkeys, peer_u, peer_v):
    b, seq, d = x.shape
    depth = ada_w.shape[0]
    t = b * seq
    cos, sin = _rope_tables(seq)
    mod = _modulation(c, ada_w, ada_b)
    x2 = x.reshape(t, d)
    for layer in range(depth):
        shift_m, scale_m, gate_m, shift_f, scale_f, gate_f = [
            m.reshape(b, 1, d) for m in jnp.split(mod[layer], 6, axis=-1)]
        g_mix = norm_mix_g[layer].reshape(1, d)
        i = layer // 2
        if layer % 2 == 0:
            proj = _norm_proj(x2, g_mix, shift_m, scale_m, w_in_ab[i].astype(BF16), seq)
            proj = proj.reshape(b, seq, -1)
            ya = _swa_attention(proj, sinks_a[i], cos, sin, _two_heads(qnorm_a[i]), _two_heads(knorm_a[i]))
            b_col = (A_Q_HEADS + 2 * A_KV_HEADS) * HEAD_DIM // LANES
            yb = _stick_attention(proj, b_col)
            w_out = w_out_ab[i]
        else:
            proj = _norm_proj(x2, g_mix, shift_m, scale_m, w_in_cd[i].astype(BF16), seq)
            proj = proj.reshape(b, seq, -1)
            ya = _qkv_attention_call(_dilated_kernel, "dilated_attention", proj, 0, C_HEADS, cos, sin,
                                     _two_heads(qnorm_c[i]), _two_heads(knorm_c[i]))
            d_col = 3 * C_HEADS * HEAD_DIM // LANES
            yb = _qkv_attention_call(_moba_kernel, "moba_attention", proj, d_col, D_HEADS, cos, sin,
                                     _two_heads(qnorm_d[i]), _two_heads(knorm_d[i]),
                                     extra_scratch=(pltpu.VMEM((LANES, LANES), F32),
                                                    pltpu.VMEM((seq, LANES), F32),
                                                    pltpu.VMEM((seq, LANES), F32)))
            w_out = w_out_cd[i]
        x2 = _out_proj(x2, ya.reshape(t, -1), yb.reshape(t, -1), w_out.astype(BF16), gate_m, seq)
        x2 = _peer_ffn(x2, norm_ffn_g[layer].reshape(1, d), shift_f, scale_f, gate_f,
                       peer_wq[layer], peer_subkeys[layer],
                       _pack_table(peer_u[layer]), _pack_table(peer_v[layer]), seq)
    return x2.reshape(b, seq, d)
```

```python
import jax
import jax.numpy as jnp
from jax import lax
from jax.experimental import pallas as pl
from jax.experimental.pallas import tpu as pltpu

F32 = jnp.float32
BF16 = jnp.bfloat16
I32 = jnp.int32
U32 = jnp.uint32

HEAD_DIM = 64
ROPE_THETA = 10000.0
NORM_EPS = 1e-6
LANES = 128
QUERY_BLOCK = 128
A_Q_HEADS, A_KV_HEADS = 8, 2
B_HEADS = C_HEADS = D_HEADS = 8
C_PATTERNS = ((128, 1), (512, 4), (2048, 16))
MOBA_BLOCK, MOBA_TOPK = 256, 3
PEER_HEADS, PEER_N_KEYS, PEER_TOPK, PEER_D_KEY = 8, 128, 16, 256
PEER_SLOTS = PEER_HEADS * PEER_TOPK
NEG_BIG = -1e30
PEER_CAND_COUNTS = tuple(PEER_TOPK // (a + 1) for a in range(PEER_TOPK))
PACK_ROWS = 4
MIB = 1024 * 1024
V7X_VMEM_BYTES = 64 * MIB
STREAM_VMEM_LIMIT = 40 * MIB
TABLE_VMEM_LIMIT = V7X_VMEM_BYTES - 8 * MIB


def _cparams(n_axes, vmem_bytes=STREAM_VMEM_LIMIT):
    return pltpu.CompilerParams(
        dimension_semantics=("arbitrary",) * n_axes,
        vmem_limit_bytes=vmem_bytes)


def _split_bf16(a):
    hi = a.astype(BF16)
    lo = (a - hi.astype(F32)).astype(BF16)
    return hi, lo


def _dot(a, b):
    return jnp.dot(a, b, preferred_element_type=F32)


def _dot_nt(a, b):
    return lax.dot_general(a, b, (((1,), (1,)), ((), ())), preferred_element_type=F32)


def _dot3(a, b):
    ah, al = _split_bf16(a)
    bh, bl = _split_bf16(b)
    return _dot(ah, bh) + _dot(ah, bl) + _dot(al, bh)


def _dot3_nt(a, b):
    ah, al = _split_bf16(a)
    bh, bl = _split_bf16(b)
    return _dot_nt(ah, bh) + _dot_nt(ah, bl) + _dot_nt(al, bh)


def _iota(shape, dim):
    return lax.broadcasted_iota(I32, shape, dim)


def _mod_kernel(c_ref, w_ref, b_ref, o_ref):
    c = c_ref[...]
    cond = c * jax.nn.sigmoid(c)
    o_ref[0] = _dot3(cond, w_ref[0]) + b_ref[0]


def _modulation(c, ada_w, ada_b):
    depth, d, n = ada_w.shape
    b = c.shape[0]
    tn = 1024
    return pl.pallas_call(
        _mod_kernel,
        grid=(depth, n // tn),
        in_specs=[
            pl.BlockSpec((b, d), lambda l, j: (0, 0)),
            pl.BlockSpec((1, d, tn), lambda l, j: (l, 0, j)),
            pl.BlockSpec((1, 1, tn), lambda l, j: (l, 0, j)),
        ],
        out_specs=pl.BlockSpec((1, b, tn), lambda l, j: (l, 0, j)),
        out_shape=jax.ShapeDtypeStruct((depth, b, n), F32),
        compiler_params=_cparams(2),
        name="adaln_modulation",
    )(c, ada_w, ada_b.reshape(depth, 1, n))


def _adaln(x, g, shift, scale):
    ms = jnp.mean(x * x, axis=-1, keepdims=True)
    y = x * lax.rsqrt(ms + NORM_EPS) * g
    return y * (1.0 + scale) + shift


def _norm_proj_kernel(x_ref, g_ref, sh_ref, sc_ref, w_ref, o_ref):
    h = _adaln(x_ref[...], g_ref[...], sh_ref[0], sc_ref[0])
    o_ref[...] = _dot(h.astype(BF16), w_ref[...])


def _norm_proj(x2, g, shift, scale, w_bf, seq):
    t, d = x2.shape
    n = w_bf.shape[1]
    tt = 512
    per_b = seq // tt
    return pl.pallas_call(
        _norm_proj_kernel,
        grid=(t // tt,),
        in_specs=[
            pl.BlockSpec((tt, d), lambda i: (i, 0)),
            pl.BlockSpec((1, d), lambda i: (0, 0)),
            pl.BlockSpec((1, 1, d), lambda i: (i // per_b, 0, 0)),
            pl.BlockSpec((1, 1, d), lambda i: (i // per_b, 0, 0)),
            pl.BlockSpec((d, n), lambda i: (0, 0)),
        ],
        out_specs=pl.BlockSpec((tt, n), lambda i: (i, 0)),
        out_shape=jax.ShapeDtypeStruct((t, n), F32),
        compiler_params=_cparams(1),
        name="adaln_in_proj",
    )(x2, g, shift, scale, w_bf)


def _out_proj_kernel(x_ref, ya_ref, yb_ref, w_ref, gate_ref, o_ref):
    half = ya_ref.shape[1]
    y = _dot(ya_ref[...].astype(BF16), w_ref[0:half, :])
    y = y + _dot(yb_ref[...].astype(BF16), w_ref[half:2 * half, :])
    o_ref[...] = x_ref[...] + gate_ref[0] * y


def _out_proj(x2, ya, yb, w_bf, gate, seq):
    t, d = x2.shape
    half = ya.shape[1]
    tt = 512
    per_b = seq // tt
    return pl.pallas_call(
        _out_proj_kernel,
        grid=(t // tt,),
        in_specs=[
            pl.BlockSpec((tt, d), lambda i: (i, 0)),
            pl.BlockSpec((tt, half), lambda i: (i, 0)),
            pl.BlockSpec((tt, half), lambda i: (i, 0)),
            pl.BlockSpec((2 * half, d), lambda i: (0, 0)),
            pl.BlockSpec((1, 1, d), lambda i: (i // per_b, 0, 0)),
        ],
        out_specs=pl.BlockSpec((tt, d), lambda i: (i, 0)),
        out_shape=jax.ShapeDtypeStruct((t, d), F32),
        compiler_params=_cparams(1),
        name="mixer_out_proj",
    )(x2, ya, yb, w_bf, gate)


def _lane_row():
    return _iota((1, LANES), 1)


def _head_segment_ones():
    r = _iota((LANES, LANES), 0) // HEAD_DIM
    c = _iota((LANES, LANES), 1) // HEAD_DIM
    return jnp.where(r == c, 1.0, 0.0).astype(BF16)


def _headnorm_rope(a, g, cos, sin):
    hi, lo = _split_bf16(a * a)
    seg = _head_segment_ones()
    ms = (_dot(hi, seg) + _dot(lo, seg)) * (1.0 / HEAD_DIM)
    y = a * lax.rsqrt(ms + NORM_EPS) * g
    half = HEAD_DIM // 2
    upper = pltpu.roll(y, LANES - half, axis=1)
    lower = pltpu.roll(y, half, axis=1)
    first_half = (_lane_row() % HEAD_DIM) < half
    rot = jnp.where(first_half, -upper, lower)
    return y * cos + rot * sin


def _head_masks():
    lane = _lane_row()
    return lane < HEAD_DIM, lane >= HEAD_DIM


def _attn_specs(seq, qcol, kcol, vcol, kv_shared):
    blk = (1, seq, LANES)
    q_spec = pl.BlockSpec(blk, lambda b, p: (b, 0, qcol + p))
    if kv_shared:
        k_spec = pl.BlockSpec(blk, lambda b, p: (b, 0, kcol))
        v_spec = pl.BlockSpec(blk, lambda b, p: (b, 0, vcol))
    else:
        k_spec = pl.BlockSpec(blk, lambda b, p: (b, 0, kcol + p))
        v_spec = pl.BlockSpec(blk, lambda b, p: (b, 0, vcol + p))
    return q_spec, k_spec, v_spec


def _row_spec(seq):
    return pl.BlockSpec((seq, LANES), lambda b, p: (0, 0))


def _gain_spec():
    return pl.BlockSpec((1, LANES), lambda b, p: (0, 0))


def _store_heads(o_ref, r0, outs):
    first, _ = _head_masks()
    o_ref[0, pl.ds(r0, QUERY_BLOCK), :] = jnp.where(first, outs[0], outs[1])


SWA_QBLOCKS_PER_STEP = 4


def _swa_kernel(sinks_ref, q_ref, k_ref, v_ref, cos_ref, sin_ref, gq_ref, gk_ref, o_ref,
                q0_s, q1_s, k_s, v_s):
    p = pl.program_id(1)
    seq = q_ref.shape[1]
    cos, sin = cos_ref[...], sin_ref[...]
    first, second = _head_masks()
    qn = _headnorm_rope(q_ref[0], gq_ref[...], cos, sin) * (HEAD_DIM ** -0.5)
    q0_s[...] = jnp.where(first, qn, 0.0).astype(BF16)
    q1_s[...] = jnp.where(second, qn, 0.0).astype(BF16)
    pairs_per_kv = (A_Q_HEADS // A_KV_HEADS) // 2
    keep = jnp.logical_xor(first, (p // pairs_per_kv) == 1)
    kn = _headnorm_rope(k_ref[0], gk_ref[...], cos, sin)
    k_s[...] = jnp.where(keep, kn, pltpu.roll(kn, HEAD_DIM, axis=1)).astype(BF16)
    v = v_ref[0]
    v_s[...] = jnp.where(keep, v, pltpu.roll(v, HEAD_DIM, axis=1)).astype(BF16)

    qi = _iota((QUERY_BLOCK, QUERY_BLOCK), 0)
    ki = _iota((QUERY_BLOCK, QUERY_BLOCK), 1)

    mask_c = ki <= qi
    per_step = min(SWA_QBLOCKS_PER_STEP, seq // QUERY_BLOCK)

    def qstep(step, carry):
        chains, scores = [], []
        for u in range(per_step):
            i = step * per_step + u
            r0 = pl.multiple_of(i * QUERY_BLOCK, QUERY_BLOCK)
            rp = pl.multiple_of(jnp.maximum(i - 1, 0) * QUERY_BLOCK, QUERY_BLOCK)
            kc, kp = k_s[pl.ds(r0, QUERY_BLOCK), :], k_s[pl.ds(rp, QUERY_BLOCK), :]
            vc, vp = v_s[pl.ds(r0, QUERY_BLOCK), :], v_s[pl.ds(rp, QUERY_BLOCK), :]
            mask_p = jnp.logical_and(ki > qi, i > 0)
            chains.append((r0, vc, vp, mask_p))
            for q_s in (q0_s, q1_s):
                qh = q_s[pl.ds(r0, QUERY_BLOCK), :]
                scores.append((_dot_nt(qh, kc), _dot_nt(qh, kp)))
        probs, denoms = [], []
        for n, (s_cur, s_prev) in enumerate(scores):
            mask_p = chains[n // 2][3]
            sc = jnp.where(mask_c, s_cur, NEG_BIG)
            sp = jnp.where(mask_p, s_prev, NEG_BIG)
            sink = sinks_ref[2 * p + n % 2]
            m = jnp.maximum(jnp.max(sc, axis=1, keepdims=True), jnp.max(sp, axis=1, keepdims=True))
            m = jnp.maximum(m, sink)
            ec, ep = jnp.exp(sc - m), jnp.exp(sp - m)
            denoms.append(jnp.sum(ec, axis=1, keepdims=True) + jnp.sum(ep, axis=1, keepdims=True)
                          + jnp.exp(sink - m))
            probs.append((ec.astype(BF16), ep.astype(BF16)))
        for u, (r0, vc, vp, _) in enumerate(chains):
            outs = [(_dot(probs[2 * u + hh][0], vc) + _dot(probs[2 * u + hh][1], vp)) / denoms[2 * u + hh]
                    for hh in range(2)]
            _store_heads(o_ref, r0, outs)
        return carry

    lax.fori_loop(0, seq // (QUERY_BLOCK * per_step), qstep, 0)


def _swa_attention(proj, sinks, cos, sin, gq, gk):
    b, seq, _ = proj.shape
    n_pairs = A_Q_HEADS // 2
    kcol = A_Q_HEADS * HEAD_DIM // LANES
    vcol = kcol + A_KV_HEADS * HEAD_DIM // LANES
    q_spec, k_spec, v_spec = _attn_specs(seq, 0, kcol, vcol, True)
    return pl.pallas_call(
        _swa_kernel,
        grid=(b, n_pairs),
        in_specs=[pl.BlockSpec(memory_space=pltpu.SMEM), q_spec, k_spec, v_spec,
                  _row_spec(seq), _row_spec(seq), _gain_spec(), _gain_spec()],
        out_specs=pl.BlockSpec((1, seq, LANES), lambda b_, p: (b_, 0, p)),
        out_shape=jax.ShapeDtypeStruct((b, seq, n_pairs * LANES), F32),
        scratch_shapes=[pltpu.VMEM((seq, LANES), BF16)] * 4,
        compiler_params=_cparams(2),
        name="swa_gqa_attention",
    )(sinks, proj, proj, proj, cos, sin, gq, gk)


STICK_GROUP = 4
STICK_QUERY_ROWS = 256


def _stick_kernel(q_ref, k_ref, v_ref, o_ref, k_s, v_s):
    seq = q_ref.shape[1]
    k_s[...] = k_ref[0].astype(BF16)
    v_s[...] = v_ref[0].astype(BF16)
    first, second = _head_masks()
    qr = min(STICK_QUERY_ROWS, seq)
    kw = QUERY_BLOCK
    key_minus_query = _iota((qr, kw), 1) - _iota((qr, kw), 0)
    wr = _iota((2 * kw, 2 * kw), 0) % kw
    wc = _iota((2 * kw, 2 * kw), 1)
    suffix_w = jnp.where(jnp.logical_or(wc >= kw, wr > wc), 1.0, 0.0).astype(BF16)

    def qblock(i, carry):
        r0 = pl.multiple_of(i * qr, qr)
        q = q_ref[0, pl.ds(r0, qr), :] * (HEAD_DIM ** -0.5)
        qhs = [jnp.where(msk, q, 0.0).astype(BF16) for msk in (first, second)]
        n_blocks = (r0 + qr) // kw

        def kgroup(g, st):
            accs, laters = [st[0], st[1]], [st[2], st[3]]
            chains = [(u, hh) for u in range(STICK_GROUP) for hh in range(2)]
            vbs, pasts, zs = [], [], {}
            for u in range(STICK_GROUP):
                j = n_blocks - 1 - (g * STICK_GROUP + u)
                live = j >= 0
                c0 = pl.multiple_of(jnp.maximum(j, 0) * kw, kw)
                kb = k_s[pl.ds(c0, kw), :]
                vbs.append(v_s[pl.ds(c0, kw), :])
                pasts.append(jnp.logical_and(key_minus_query < r0 - c0, live))
                for hh in range(2):
                    zs[u, hh] = _dot_nt(qhs[hh], kb)
            logit, sums = {}, {}
            for u, hh in chains:
                z = zs[u, hh]
                sp = jnp.maximum(z, 0.0) + jnp.log(1.0 + jnp.exp(-jnp.abs(z)))
                neg_log_keep = jnp.where(pasts[u], sp, 0.0)
                logit[u, hh] = z - sp
                hi, lo = _split_bf16(neg_log_keep)
                sums[u, hh] = _dot(jnp.concatenate([hi, lo], axis=1), suffix_w)
            ws = {}
            for u, hh in chains:
                inner, total = sums[u, hh][:, :kw], sums[u, hh][:, kw:]
                ws[u, hh] = jnp.where(pasts[u], jnp.exp(logit[u, hh] - inner - laters[hh]), 0.0).astype(BF16)
                laters[hh] = laters[hh] + total
            for u, hh in chains:
                accs[hh] = accs[hh] + _dot(ws[u, hh], vbs[u])
            return accs[0], accs[1], laters[0], laters[1]

        zero = jnp.zeros((qr, LANES), F32)
        n_groups = (n_blocks + STICK_GROUP - 1) // STICK_GROUP
        st = lax.fori_loop(0, n_groups, kgroup, (zero, zero, zero, zero))
        o_ref[0, pl.ds(r0, qr), :] = jnp.where(first, st[0], st[1])
        return carry

    lax.fori_loop(0, seq // qr, qblock, 0)


def _stick_attention(proj, qcol):
    b, seq, _ = proj.shape
    n_pairs = B_HEADS // 2
    q_spec, k_spec, v_spec = _attn_specs(seq, qcol, qcol + n_pairs, qcol + 2 * n_pairs, False)
    return pl.pallas_call(
        _stick_kernel,
        grid=(b, n_pairs),
        in_specs=[q_spec, k_spec, v_spec],
        out_specs=pl.BlockSpec((1, seq, LANES), lambda b_, p: (b_, 0, p)),
        out_shape=jax.ShapeDtypeStruct((b, seq, n_pairs * LANES), F32),
        scratch_shapes=[pltpu.VMEM((seq, LANES), BF16)] * 2,
        compiler_params=_cparams(2),
        name="stick_breaking_attention",
    )(proj, proj, proj)


def _prep_qkv(q_ref, k_ref, v_ref, cos_ref, sin_ref, gq_ref, gk_ref, q0_s, q1_s, k_s, v_s):
    cos, sin = cos_ref[...], sin_ref[...]
    first, second = _head_masks()
    qn = _headnorm_rope(q_ref[0], gq_ref[...], cos, sin) * (HEAD_DIM ** -0.5)
    q0_s[...] = jnp.where(first, qn, 0.0).astype(BF16)
    q1_s[...] = jnp.where(second, qn, 0.0).astype(BF16)
    kn = _headnorm_rope(k_ref[0], gk_ref[...], cos, sin)
    k_s[...] = kn.astype(BF16)
    v_s[...] = v_ref[0].astype(BF16)
    return qn, kn


DILATED_KEY_TILE = 512
DILATED_QUERY_ROWS = 256


def _dilated_kernel(q_ref, k_ref, v_ref, cos_ref, sin_ref, gq_ref, gk_ref, o_ref,
                    q0_s, q1_s, k_s, v_s):
    seq = q_ref.shape[1]
    _prep_qkv(q_ref, k_ref, v_ref, cos_ref, sin_ref, gq_ref, gk_ref, q0_s, q1_s, k_s, v_s)
    kt = min(DILATED_KEY_TILE, seq)
    qr = min(DILATED_QUERY_ROWS, seq)
    qk = _iota((qr, kt), 0) - _iota((qr, kt), 1)
    on_stride = [jnp.where((qk & (dil - 1)) == 0, 1.0, 0.0) for _, dil in C_PATTERNS]
    first_head, _ = _head_masks()

    def qblock(i, carry):
        r0 = pl.multiple_of(i * qr, qr)
        qhs = [q_s[pl.ds(r0, qr), :] for q_s in (q0_s, q1_s)]

        def ktile(g, st):
            c0 = pl.multiple_of(g * kt, kt)
            d = (r0 - c0) + qk
            count = jnp.zeros(d.shape, F32)
            for (window, _), stride_ok in zip(C_PATTERNS, on_stride):
                count = count + jnp.where(d <= window, stride_ok, 0.0)
            count = jnp.where(d >= 0, count, 0.0)
            kb, vb = k_s[pl.ds(c0, kt), :], v_s[pl.ds(c0, kt), :]
            scores = [_dot_nt(qhs[hh], kb) for hh in range(2)]
            new, prs = [], []
            for hh in range(2):
                m, l, acc = st[3 * hh:3 * hh + 3]
                s = jnp.where(count > 0.0, scores[hh], NEG_BIG)
                m_new = jnp.maximum(m, jnp.max(s, axis=1, keepdims=True))
                pr = count * jnp.exp(s - m_new)
                alpha = jnp.exp(m - m_new)
                new += [m_new, alpha * l + jnp.sum(pr, axis=1, keepdims=True), alpha * acc]
                prs.append(pr.astype(BF16))
            for hh in range(2):
                new[3 * hh + 2] = new[3 * hh + 2] + _dot(prs[hh], vb)
            return tuple(new)

        init = (jnp.full((qr, 1), NEG_BIG, F32), jnp.zeros((qr, 1), F32),
                jnp.zeros((qr, LANES), F32)) * 2
        st = lax.fori_loop(0, (r0 + qr + kt - 1) // kt, ktile, init)
        o_ref[0, pl.ds(r0, qr), :] = jnp.where(first_head, st[2] / st[1], st[5] / st[4])
        return carry

    lax.fori_loop(0, seq // qr, qblock, 0)


def _qkv_attention_call(kernel, name, proj, qcol, n_heads, cos, sin, gq, gk, extra_scratch=()):
    b, seq, _ = proj.shape
    n_pairs = n_heads // 2
    q_spec, k_spec, v_spec = _attn_specs(seq, qcol, qcol + n_pairs, qcol + 2 * n_pairs, False)
    return pl.pallas_call(
        kernel,
        grid=(b, n_pairs),
        in_specs=[q_spec, k_spec, v_spec, _row_spec(seq), _row_spec(seq), _gain_spec(), _gain_spec()],
        out_specs=pl.BlockSpec((1, seq, LANES), lambda b_, p: (b_, 0, p)),
        out_shape=jax.ShapeDtypeStruct((b, seq, n_pairs * LANES), F32),
        scratch_shapes=[pltpu.VMEM((seq, LANES), BF16)] * 4 + list(extra_scratch),
        compiler_params=_cparams(2),
        name=name,
    )(proj, proj, proj, cos, sin, gq, gk)


def _moba_kernel(q_ref, k_ref, v_ref, cos_ref, sin_ref, gq_ref, gk_ref, o_ref,
                 q0_s, q1_s, k_s, v_s, km_s, sel0_s, sel1_s):
    seq = q_ref.shape[1]
    n_blocks = seq // MOBA_BLOCK
    qn, kn = _prep_qkv(q_ref, k_ref, v_ref, cos_ref, sin_ref, gq_ref, gk_ref, q0_s, q1_s, k_s, v_s)
    km_s[...] = jnp.zeros(km_s.shape, F32)
    km_s[0:n_blocks, :] = jnp.mean(kn.reshape(n_blocks, MOBA_BLOCK, LANES), axis=1)
    first, second = _head_masks()

    rows8 = _iota((8, seq), 0)
    own8 = _iota((8, seq), 1) // MOBA_BLOCK
    valid = rows8 < own8
    for msk, sel_s in ((first, sel0_s), (second, sel1_s)):
        gate = _dot3_nt(km_s[...], jnp.where(msk, qn, 0.0))[0:8, :]
        gm = jnp.where(valid, gate, -jnp.inf)
        rank = jnp.zeros((8, seq), F32)
        for n2 in range(n_blocks):
            g2 = gm[n2:n2 + 1, :]
            beats = jnp.logical_or(g2 > gm, jnp.logical_and(g2 == gm, n2 < rows8))
            rank = rank + jnp.where(jnp.logical_and(beats, n2 < own8), 1.0, 0.0)
        sel = jnp.where(jnp.logical_and(valid, rank < float(MOBA_TOPK)), 1.0, 0.0)
        sel = jnp.concatenate([sel, jnp.zeros((LANES - 8, seq), F32)], axis=0)
        sel_s[...] = sel.T

    qrows = MOBA_BLOCK
    lane_sq = _iota((qrows, LANES), 1)
    causal = _iota((qrows, MOBA_BLOCK), 1) <= _iota((qrows, MOBA_BLOCK), 0)
    second_block = _iota((qrows, 2 * MOBA_BLOCK), 1) >= MOBA_BLOCK
    first_head, _ = _head_masks()

    def qblock(own, carry):
        r0 = pl.multiple_of(own * qrows, qrows)
        qhs = [q_s[pl.ds(r0, qrows), :] for q_s in (q0_s, q1_s)]
        sels = [sel_s[pl.ds(r0, qrows), :] for sel_s in (sel0_s, sel1_s)]
        kb, vb = k_s[pl.ds(r0, MOBA_BLOCK), :], v_s[pl.ds(r0, MOBA_BLOCK), :]
        scores = [_dot_nt(qhs[hh], kb) for hh in range(2)]
        init, prs = [], []
        for hh in range(2):
            s = jnp.where(causal, scores[hh], NEG_BIG)
            m = jnp.max(s, axis=1, keepdims=True)
            pr = jnp.exp(s - m)
            init += [m, jnp.sum(pr, axis=1, keepdims=True), None]
            prs.append(pr.astype(BF16))
        for hh in range(2):
            init[3 * hh + 2] = _dot(prs[hh], vb)

        def kpair(g, st):
            c0 = pl.multiple_of(g * 2 * MOBA_BLOCK, 2 * MOBA_BLOCK)
            kb2, vb2 = k_s[pl.ds(c0, 2 * MOBA_BLOCK), :], v_s[pl.ds(c0, 2 * MOBA_BLOCK), :]
            scores = [_dot_nt(qhs[hh], kb2) for hh in range(2)]
            new, prs = [], []
            for hh in range(2):
                m, l, acc = st[3 * hh:3 * hh + 3]
                sel_a = jnp.sum(jnp.where(lane_sq == 2 * g, sels[hh], 0.0), axis=1, keepdims=True)
                sel_b = jnp.sum(jnp.where(lane_sq == 2 * g + 1, sels[hh], 0.0), axis=1, keepdims=True)
                keep = jnp.where(second_block, sel_b, sel_a) > 0.0
                s = jnp.where(keep, scores[hh], NEG_BIG)
                m_new = jnp.maximum(m, jnp.max(s, axis=1, keepdims=True))
                pr = jnp.exp(s - m_new)
                alpha = jnp.exp(m - m_new)
                new += [m_new, alpha * l + jnp.sum(pr, axis=1, keepdims=True), alpha * acc]
                prs.append(pr.astype(BF16))
            for hh in range(2):
                new[3 * hh + 2] = new[3 * hh + 2] + _dot(prs[hh], vb2)
            return tuple(new)

        st = lax.fori_loop(0, (own + 1) // 2, kpair, tuple(init))
        o_ref[0, pl.ds(r0, qrows), :] = jnp.where(first_head, st[2] / st[1], st[5] / st[4])
        return carry

    lax.fori_loop(0, seq // qrows, qblock, 0)


ROUTE_CHUNKS_PER_STEP = 4


def _oddeven_merge_sort_pairs(n):
    pairs = []

    def merge(lo, hi, r):
        step = 2 * r
        if step < hi - lo:
            merge(lo, hi, step)
            merge(lo + r, hi, step)
            pairs.extend((i, i + r) for i in range(lo + r, hi - r, step))
        else:
            pairs.append((lo, lo + r))

    def sort(lo, hi):
        if hi - lo >= 1:
            mid = lo + (hi - lo) // 2
            sort(lo, mid)
            sort(mid + 1, hi)
            merge(lo, hi, 1)

    sort(0, n - 1)
    return pairs


SUBLANES = 8


def _top16_rows(scores, n_rows, vals_refs, idx_refs):
    n_slabs = n_rows // SUBLANES
    sub = _iota((SUBLANES, LANES), 0)
    vals = [[s[SUBLANES * v:SUBLANES * (v + 1), :] for v in range(n_slabs)] for s in scores]
    idxs = [[sub + SUBLANES * v for v in range(n_slabs)] for _ in scores]
    for i, j in _oddeven_merge_sort_pairs(n_slabs):
        for va, ia in zip(vals, idxs):
            a, b = va[i], va[j]
            a_first = jnp.logical_or(a > b, jnp.logical_and(a == b, ia[i] < ia[j]))
            va[i], va[j] = jnp.maximum(a, b), jnp.minimum(a, b)
            ia[i], ia[j] = jnp.where(a_first, ia[i], ia[j]), jnp.where(a_first, ia[j], ia[i])
    for it in range(PEER_TOPK):
        for k, (va, ia) in enumerate(zip(vals, idxs)):
            m = jnp.max(va[0], axis=0, keepdims=True)
            pick = jnp.min(jnp.where(va[0] == m, ia[0], n_rows), axis=0, keepdims=True)
            vals_refs[k][it:it + 1, :] = m
            idx_refs[k][it:it + 1, :] = pick
            win = ia[0] == pick
            depth = PEER_TOPK - 1 - it
            for d in range(min(depth, n_slabs - 1)):
                va[d] = jnp.where(win, va[d + 1], va[d])
                ia[d] = jnp.where(win, ia[d + 1], ia[d])
            if depth >= n_slabs:
                va[n_slabs - 1] = jnp.where(win, -jnp.inf, va[n_slabs - 1])


def _peer_route_kernel(x_ref, g_ref, sh_ref, sc_ref, wqt_ref, sk_ref, h_ref, ids_ref, gts_ref,
                       q_s, val_s, idx_s, ids_s, gts_s):
    tt = x_ref.shape[0]
    n_chunks = tt // LANES
    per_step = ROUTE_CHUNKS_PER_STEP
    half = PEER_D_KEY // 2
    h = _adaln(x_ref[...], g_ref[...], sh_ref[0], sc_ref[0])
    h_ref[...] = h
    qt = _dot_nt(wqt_ref[...], h.astype(BF16))
    for c in range(n_chunks):
        q_s[c] = qt[:, c * LANES:(c + 1) * LANES]
    sk1, sk2 = sk_ref[0], sk_ref[1]
    col_id = _iota((PEER_TOPK, LANES), 0)
    sub = _iota((SUBLANES, LANES), 0)
    col_depth = jnp.zeros((SUBLANES, LANES), I32)
    for a in range(SUBLANES):
        col_depth = jnp.where(sub == a, PEER_CAND_COUNTS[a], col_depth)
    vals = [val_s.at[i] for i in range(2 * per_step)]
    idxs = [idx_s.at[i] for i in range(2 * per_step)]

    def body(step, carry):
        hh = step // (n_chunks // per_step)
        c0 = (step % (n_chunks // per_step)) * per_step
        q0 = pl.multiple_of(hh * PEER_D_KEY, PEER_D_KEY)
        scores = []
        for k in range(per_step):
            scores.append(_dot3(sk1, q_s[c0 + k, pl.ds(q0, half), :]))
            scores.append(_dot3(sk2, q_s[c0 + k, pl.ds(q0 + half, half), :]))
        _top16_rows(scores, PEER_N_KEYS, vals, idxs)
        state = []
        for k in range(per_step):
            v1, i1, v2, i2 = vals[2 * k], idxs[2 * k], vals[2 * k + 1], idxs[2 * k + 1]
            v1x, e1x = v1[0:SUBLANES, :], i1[0:SUBLANES, :] * PEER_N_KEYS
            xs = [jnp.where(col_depth > b, v1x + v2[b:b + 1, :], -jnp.inf) for b in range(PEER_TOPK)]
            ex = [e1x + i2[b:b + 1, :] for b in range(PEER_TOPK)]
            y = v1[SUBLANES:PEER_TOPK, :] + v2[0:1, :]
            ey = i1[SUBLANES:PEER_TOPK, :] * PEER_N_KEYS + i2[0:1, :]
            state.append([xs, ex, y, ey])
        for it in range(PEER_TOPK):
            for k in range(per_step):
                xs, ex, y, ey = state[k]
                heads = jnp.concatenate([xs[0], y], axis=0)
                m = jnp.max(heads, axis=0, keepdims=True)
                pick = jnp.min(jnp.where(heads == m, col_id, PEER_TOPK), axis=0, keepdims=True)
                win = col_id == pick
                eids = jnp.where(win, jnp.concatenate([ex[0], ey], axis=0), 0)
                idxs[2 * k][it:it + 1, :] = jnp.sum(eids, axis=0, keepdims=True)
                vals[2 * k][it:it + 1, :] = m
                win_x, win_y = win[0:SUBLANES, :], win[SUBLANES:PEER_TOPK, :]
                for dd in range(PEER_TOPK - 1 - it):
                    xs[dd] = jnp.where(win_x, xs[dd + 1], xs[dd])
                    ex[dd] = jnp.where(win_x, ex[dd + 1], ex[dd])
                state[k][2] = jnp.where(win_y, -jnp.inf, y)
        r0 = pl.multiple_of(hh * PEER_TOPK, PEER_TOPK)
        for k in range(per_step):
            top = vals[2 * k][...]
            e = jnp.exp(top - top[0:1, :])
            gts_s[c0 + k, pl.ds(r0, PEER_TOPK), :] = e / jnp.sum(e, axis=0, keepdims=True)
            ids_s[c0 + k, pl.ds(r0, PEER_TOPK), :] = idxs[2 * k][...] * PACK_ROWS
        return carry

    lax.fori_loop(0, PEER_HEADS * n_chunks // per_step, body, 0)
    for c in range(n_chunks):
        rows = slice(c * LANES, (c + 1) * LANES)
        ids_ref[rows, :] = ids_s[c].T
        gts_ref[rows, :] = gts_s[c].T


def _peer_route(x2, g, shift, scale, wqt_bf, sub_keys, seq):
    t, d = x2.shape
    tt = 512
    per_b = seq // tt
    n_chunks = tt // LANES
    nq = wqt_bf.shape[0]
    out_blk = pl.BlockSpec((tt, PEER_SLOTS), lambda i: (i, 0))
    return pl.pallas_call(
        _peer_route_kernel,
        grid=(t // tt,),
        in_specs=[
            pl.BlockSpec((tt, d), lambda i: (i, 0)),
            pl.BlockSpec((1, d), lambda i: (0, 0)),
            pl.BlockSpec((1, 1, d), lambda i: (i // per_b, 0, 0)),
            pl.BlockSpec((1, 1, d), lambda i: (i // per_b, 0, 0)),
            pl.BlockSpec((nq, d), lambda i: (0, 0)),
            pl.BlockSpec(sub_keys.shape, lambda i: (0, 0, 0)),
        ],
        out_specs=[pl.BlockSpec((tt, d), lambda i: (i, 0)), out_blk, out_blk],
        out_shape=[jax.ShapeDtypeStruct((t, d), F32),
                   jax.ShapeDtypeStruct((t, PEER_SLOTS), I32),
                   jax.ShapeDtypeStruct((t, PEER_SLOTS), F32)],
        scratch_shapes=[pltpu.VMEM((n_chunks, nq, LANES), F32),
                        pltpu.VMEM((2 * ROUTE_CHUNKS_PER_STEP, PEER_TOPK, LANES), F32),
                        pltpu.VMEM((2 * ROUTE_CHUNKS_PER_STEP, PEER_TOPK, LANES), I32),
                        pltpu.VMEM((n_chunks, PEER_SLOTS, LANES), I32),
                        pltpu.VMEM((n_chunks, PEER_SLOTS, LANES), F32)],
        compiler_params=_cparams(1),
        name="peer_route",
    )(x2, g, shift, scale, wqt_bf, sub_keys)


def _pack_table(tab):
    e, d = tab.shape
    bits = lax.bitcast_convert_type(tab.astype(BF16), jnp.uint16).astype(U32)
    bits = bits.reshape(e, d // (2 * LANES), 2, LANES)
    words = bits[:, :, 0, :] | (bits[:, :, 1, :] << 16)
    return words.reshape(e * (d // (2 * LANES)), LANES)


def _table_spec(rows):
    return pl.BlockSpec((rows, LANES), lambda i: (0, 0), pipeline_mode=pl.Buffered(1))


def _gelu_exact(a):
    return 0.5 * a * (1.0 + lax.erf(a * (2.0 ** -0.5)))


FEAT_CHUNKS = 8
SLOT_WIDTH = PEER_SLOTS * FEAT_CHUNKS


def _gather_rows(ids_ref, base, tbl_ref, slot):
    for j in range(PEER_SLOTS):
        if j % ID_VIEW == 0:
            ids_part = ids_ref.at[pl.ds(base + j, ID_VIEW)]
        row0 = pl.multiple_of(ids_part[j % ID_VIEW], PACK_ROWS)
        slot[PACK_ROWS * j:PACK_ROWS * (j + 1), :] = tbl_ref[pl.ds(row0, PACK_ROWS), :]


def _pipelined_tokens(tt, ids_ref, tbl_ref, slots, compute):
    group = len(slots)
    n_steps = tt // group

    def gather(t, slot):
        _gather_rows(ids_ref, t * PEER_SLOTS, tbl_ref, slot)

    for k in range(group):
        gather(k, slots[k])

    def step(q, carry):
        t = group * q
        for k in range(group):
            compute(t + k, slots[k])
        for k in range(group):
            gather(t + group + k, slots[k])
        return carry

    lax.fori_loop(0, n_steps - 1, step, 0)
    for k in range(group):
        compute((n_steps - 1) * group + k, slots[k])


def _chunk_diag():
    return (_iota((FEAT_CHUNKS, SLOT_WIDTH), 1) % FEAT_CHUNKS) == _iota((FEAT_CHUNKS, SLOT_WIDTH), 0)


def _peer_u_kernel(ids_ref, h_ref, g_ref, tbl_ref, coef_ref, *scratch):
    slots, (hx_s, rs_s) = scratch[:N_SLOTS], scratch[N_SLOTS:]
    tt = h_ref.shape[0]
    for c in range(FEAT_CHUNKS):
        hx_s[pl.ds(c, tt, stride=FEAT_CHUNKS), :] = h_ref[:, c * LANES:(c + 1) * LANES]
    diag = _chunk_diag()

    def compute(t, slot):
        rows = pltpu.bitcast(slot[...], BF16)
        x8 = hx_s[pl.ds(pl.multiple_of(t * FEAT_CHUNKS, FEAT_CHUNKS), FEAT_CHUNKS), :]
        part = _dot_nt(x8.astype(BF16), rows)
        rs_s[pl.ds(t, 1), :] = jnp.sum(jnp.where(diag, part, 0.0), axis=0, keepdims=True)

    _pipelined_tokens(tt, ids_ref, tbl_ref, slots, compute)
    group = jnp.where(_iota((SLOT_WIDTH, PEER_SLOTS), 0) // FEAT_CHUNKS == _iota((SLOT_WIDTH, PEER_SLOTS), 1),
                      1.0, 0.0).astype(BF16)
    hi, lo = _split_bf16(rs_s[...])
    act = _dot(hi, group) + _dot(lo, group)
    coef_ref[...] = g_ref[...] * _gelu_exact(act)


N_SLOTS = 16
ID_VIEW = 16
PEER_TOKEN_TILE = 512


def _slot_scratch():
    return [pltpu.VMEM((PEER_SLOTS * PACK_ROWS, LANES), U32)] * N_SLOTS


def _peer_u(ids_flat, h, gates, table, tt):
    t, d = h.shape
    return pl.pallas_call(
        _peer_u_kernel,
        grid=(t // tt,),
        in_specs=[
            pl.BlockSpec((tt * PEER_SLOTS,), lambda i: (i,), memory_space=pltpu.SMEM),
            pl.BlockSpec((tt, d), lambda i: (i, 0)),
            pl.BlockSpec((tt, PEER_SLOTS), lambda i: (i, 0)),
            _table_spec(table.shape[0]),
        ],
        out_specs=pl.BlockSpec((tt, PEER_SLOTS), lambda i: (i, 0)),
        out_shape=jax.ShapeDtypeStruct((t, PEER_SLOTS), F32),
        scratch_shapes=_slot_scratch() + [
                        pltpu.VMEM((tt * FEAT_CHUNKS, LANES), F32),
                        pltpu.VMEM((tt, SLOT_WIDTH), F32)],
        compiler_params=_cparams(1, TABLE_VMEM_LIMIT),
        name="peer_expert_in",
    )(ids_flat, h, gates, table)


def _peer_v_kernel(ids_ref, coef_ref, x_ref, gate_ref, tbl_ref, o_ref, *scratch):
    slots, (ce_hi_s, ce_lo_s, res_s) = scratch[:N_SLOTS], scratch[N_SLOTS:]
    tt = x_ref.shape[0]
    spread = jnp.where(_iota((PEER_SLOTS, SLOT_WIDTH), 1) // FEAT_CHUNKS == _iota((PEER_SLOTS, SLOT_WIDTH), 0),
                       1.0, 0.0).astype(BF16)
    hi, lo = _split_bf16(coef_ref[...])
    ce_hi_s[...] = _dot(hi, spread)
    ce_lo_s[...] = _dot(lo, spread)
    diag = _chunk_diag()

    def compute(t, slot):
        rows = pltpu.bitcast(slot[...], BF16)
        a_hi = jnp.where(diag, ce_hi_s[pl.ds(t, 1), :], 0.0)
        a_lo = jnp.where(diag, ce_lo_s[pl.ds(t, 1), :], 0.0)
        both = _dot(jnp.concatenate([a_hi, a_lo], axis=0).astype(BF16), rows)
        r0 = pl.multiple_of(t * FEAT_CHUNKS, FEAT_CHUNKS)
        res_s[pl.ds(r0, FEAT_CHUNKS), :] = both[0:FEAT_CHUNKS, :] + both[FEAT_CHUNKS:2 * FEAT_CHUNKS, :]

    _pipelined_tokens(tt, ids_ref, tbl_ref, slots, compute)
    for c in range(FEAT_CHUNKS):
        cols = slice(c * LANES, (c + 1) * LANES)
        y = res_s[pl.ds(c, tt, stride=FEAT_CHUNKS), :]
        o_ref[:, cols] = x_ref[:, cols] + gate_ref[0][:, cols] * y


def _peer_v(ids_flat, coef, x2, gate, table, seq, tt):
    t, d = x2.shape
    per_b = seq // tt
    blk = pl.BlockSpec((tt, d), lambda i: (i, 0))
    return pl.pallas_call(
        _peer_v_kernel,
        grid=(t // tt,),
        in_specs=[
            pl.BlockSpec((tt * PEER_SLOTS,), lambda i: (i,), memory_space=pltpu.SMEM),
            pl.BlockSpec((tt, PEER_SLOTS), lambda i: (i, 0)),
            blk,
            pl.BlockSpec((1, 1, d), lambda i: (i // per_b, 0, 0)),
            _table_spec(table.shape[0]),
        ],
        out_specs=blk,
        out_shape=jax.ShapeDtypeStruct((t, d), F32),
        scratch_shapes=_slot_scratch() + [
                        pltpu.VMEM((tt, SLOT_WIDTH), F32), pltpu.VMEM((tt, SLOT_WIDTH), F32),
                        pltpu.VMEM((tt * FEAT_CHUNKS, LANES), F32)],
        compiler_params=_cparams(1, TABLE_VMEM_LIMIT),
        name="peer_expert_out",
    )(ids_flat, coef, x2, gate, table)


def _peer_ffn(x2, g, shift, scale, gate, wq, sub_keys, table_u, table_v, seq):
    t, d = x2.shape
    tt = PEER_TOKEN_TILE
    h, ids, gates = _peer_route(x2, g, shift, scale, wq.T.astype(BF16), sub_keys, seq)
    ids_flat = ids.reshape(t * PEER_SLOTS)
    coef = _peer_u(ids_flat, h, gates, table_u, tt)
    return _peer_v(ids_flat, coef, x2, gate, table_v, seq, tt)


def _rope_tables(seq):
    half = HEAD_DIM // 2
    inv_freq = ROPE_THETA ** (-jnp.arange(half, dtype=F32) / half)
    ang = jnp.arange(seq).astype(F32)[:, None] * inv_freq[None, :]
    reps = LANES // half
    return jnp.tile(jnp.cos(ang), (1, reps)), jnp.tile(jnp.sin(ang), (1, reps))


def _two_heads(gain):
    return jnp.tile(gain.reshape(1, HEAD_DIM), (1, LANES // HEAD_DIM))


def kernel(x, c, ada_w, ada_b, norm_mix_g, norm_ffn_g, w_in_ab, w_out_ab, sinks_a, qnorm_a, knorm_a,
           w_in_cd, w_out_cd, qnorm_c, knorm_c, qnorm_d, knorm_d, peer_wq, peer_subkeys, peer_u, peer_v):
    b, seq, d = x.shape
    depth = ada_w.shape[0]
    t = b * seq
    cos, sin = _rope_tables(seq)
    mod = _modulation(c, ada_w, ada_b)
    x2 = x.reshape(t, d)
    for layer in range(depth):
        shift_m, scale_m, gate_m, shift_f, scale_f, gate_f = [
            m.reshape(b, 1, d) for m in jnp.split(mod[layer], 6, axis=-1)]
        g_mix = norm_mix_g[layer].reshape(1, d)
        i = layer // 2
        if layer % 2 == 0:
            proj = _norm_proj(x2, g_mix, shift_m, scale_m, w_in_ab[i].astype(BF16), seq)
            proj = proj.reshape(b, seq, -1)
            ya = _swa_attention(proj, sinks_a[i], cos, sin, _two_heads(qnorm_a[i]), _two_heads(knorm_a[i]))
            b_col = (A_Q_HEADS + 2 * A_KV_HEADS) * HEAD_DIM // LANES
            yb = _stick_attention(proj, b_col)
            w_out = w_out_ab[i]
        else:
            proj = _norm_proj(x2, g_mix, shift_m, scale_m, w_in_cd[i].astype(BF16), seq)
            proj = proj.reshape(b, seq, -1)
            ya = _qkv_attention_call(_dilated_kernel, "dilated_attention", proj, 0, C_HEADS, cos, sin,
                                     _two_heads(qnorm_c[i]), _two_heads(knorm_c[i]))
            d_col = 3 * C_HEADS * HEAD_DIM // LANES
            yb = _qkv_attention_call(_moba_kernel, "moba_attention", proj, d_col, D_HEADS, cos, sin,
                                     _two_heads(qnorm_d[i]), _two_heads(knorm_d[i]),
                                     extra_scratch=(pltpu.VMEM((LANES, LANES), F32),
                                                    pltpu.VMEM((seq, LANES), F32),
                                                    pltpu.VMEM((seq, LANES), F32)))
            w_out = w_out_cd[i]
        x2 = _out_proj(x2, ya.reshape(t, -1), yb.reshape(t, -1), w_out.astype(BF16), gate_m, seq)
        x2 = _peer_ffn(x2, norm_ffn_g[layer].reshape(1, d), shift_f, scale_f, gate_f,
                       peer_wq[layer], peer_subkeys[layer],
                       _pack_table(peer_u[layer]), _pack_table(peer_v[layer]), seq)
    return x2.reshape(b, seq, d)
```

```python
import jax
import jax.numpy as jnp
from jax import lax
from jax.experimental import pallas as pl
from jax.experimental.pallas import tpu as pltpu

F32 = jnp.float32
BF16 = jnp.bfloat16
I32 = jnp.int32
U32 = jnp.uint32

HEAD_DIM = 64
ROPE_THETA = 10000.0
NORM_EPS = 1e-6
LANES = 128
QUERY_BLOCK = 128
A_Q_HEADS, A_KV_HEADS = 8, 2
B_HEADS = C_HEADS = D_HEADS = 8
C_PATTERNS = ((128, 1), (512, 4), (2048, 16))
MOBA_BLOCK, MOBA_TOPK = 256, 3
PEER_HEADS, PEER_N_KEYS, PEER_TOPK, PEER_D_KEY = 8, 128, 16, 256
PEER_SLOTS = PEER_HEADS * PEER_TOPK
NEG_BIG = -1e30
PEER_CAND_COUNTS = tuple(PEER_TOPK // (a + 1) for a in range(PEER_TOPK))
PACK_ROWS = 4
MIB = 1024 * 1024
V7X_VMEM_BYTES = 64 * MIB
STREAM_VMEM_LIMIT = 40 * MIB
TABLE_VMEM_LIMIT = V7X_VMEM_BYTES - 8 * MIB


def _cparams(n_axes, vmem_bytes=STREAM_VMEM_LIMIT):
    return pltpu.CompilerParams(
        dimension_semantics=("arbitrary",) * n_axes,
        vmem_limit_bytes=vmem_bytes)


def _split_bf16(a):
    hi = a.astype(BF16)
    lo = (a - hi.astype(F32)).astype(BF16)
    return hi, lo


def _dot(a, b):
    return jnp.dot(a, b, preferred_element_type=F32)


def _dot_nt(a, b):
    return lax.dot_general(a, b, (((1,), (1,)), ((), ())), preferred_element_type=F32)


def _dot3(a, b):
    ah, al = _split_bf16(a)
    bh, bl = _split_bf16(b)
    return _dot(ah, bh) + _dot(ah, bl) + _dot(al, bh)


def _dot3_nt(a, b):
    ah, al = _split_bf16(a)
    bh, bl = _split_bf16(b)
    return _dot_nt(ah, bh) + _dot_nt(ah, bl) + _dot_nt(al, bh)


def _iota(shape, dim):
    return lax.broadcasted_iota(I32, shape, dim)


def _mod_kernel(c_ref, w_ref, b_ref, o_ref):
    c = c_ref[...]
    cond = c * jax.nn.sigmoid(c)
    o_ref[0] = _dot3(cond, w_ref[0]) + b_ref[0]


def _modulation(c, ada_w, ada_b):
    depth, d, n = ada_w.shape
    b = c.shape[0]
    tn = 1024
    return pl.pallas_call(
        _mod_kernel,
        grid=(depth, n // tn),
        in_specs=[
            pl.BlockSpec((b, d), lambda l, j: (0, 0)),
            pl.BlockSpec((1, d, tn), lambda l, j: (l, 0, j)),
            pl.BlockSpec((1, 1, tn), lambda l, j: (l, 0, j)),
        ],
        out_specs=pl.BlockSpec((1, b, tn), lambda l, j: (l, 0, j)),
        out_shape=jax.ShapeDtypeStruct((depth, b, n), F32),
        compiler_params=_cparams(2),
        name="adaln_modulation",
    )(c, ada_w, ada_b.reshape(depth, 1, n))


def _adaln(x, g, shift, scale):
    ms = jnp.mean(x * x, axis=-1, keepdims=True)
    y = x * lax.rsqrt(ms + NORM_EPS) * g
    return y * (1.0 + scale) + shift


def _norm_proj_kernel(x_ref, g_ref, sh_ref, sc_ref, w_ref, o_ref):
    h = _adaln(x_ref[...], g_ref[...], sh_ref[0], sc_ref[0])
    o_ref[...] = _dot(h.astype(BF16), w_ref[...])


def _norm_proj(x2, g, shift, scale, w_bf, seq):
    t, d = x2.shape
    n = w_bf.shape[1]
    tt = 512
    per_b = seq // tt
    return pl.pallas_call(
        _norm_proj_kernel,
        grid=(t // tt,),
        in_specs=[
            pl.BlockSpec((tt, d), lambda i: (i, 0)),
            pl.BlockSpec((1, d), lambda i: (0, 0)),
            pl.BlockSpec((1, 1, d), lambda i: (i // per_b, 0, 0)),
            pl.BlockSpec((1, 1, d), lambda i: (i // per_b, 0, 0)),
            pl.BlockSpec((d, n), lambda i: (0, 0)),
        ],
        out_specs=pl.BlockSpec((tt, n), lambda i: (i, 0)),
        out_shape=jax.ShapeDtypeStruct((t, n), F32),
        compiler_params=_cparams(1),
        name="adaln_in_proj",
    )(x2, g, shift, scale, w_bf)


def _out_proj_kernel(x_ref, ya_ref, yb_ref, w_ref, gate_ref, o_ref):
    half = ya_ref.shape[1]
    y = _dot(ya_ref[...].astype(BF16), w_ref[0:half, :])
    y = y + _dot(yb_ref[...].astype(BF16), w_ref[half:2 * half, :])
    o_ref[...] = x_ref[...] + gate_ref[0] * y


def _out_proj(x2, ya, yb, w_bf, gate, seq):
    t, d = x2.shape
    half = ya.shape[1]
    tt = 512
    per_b = seq // tt
    return pl.pallas_call(
        _out_proj_kernel,
        grid=(t // tt,),
        in_specs=[
            pl.BlockSpec((tt, d), lambda i: (i, 0)),
            pl.BlockSpec((tt, half), lambda i: (i, 0)),
            pl.BlockSpec((tt, half), lambda i: (i, 0)),
            pl.BlockSpec((2 * half, d), lambda i: (0, 0)),
            pl.BlockSpec((1, 1, d), lambda i: (i // per_b, 0, 0)),
        ],
        out_specs=pl.BlockSpec((tt, d), lambda i: (i, 0)),
        out_shape=jax.ShapeDtypeStruct((t, d), F32),
        compiler_params=_cparams(1),
        name="mixer_out_proj",
    )(x2, ya, yb, w_bf, gate)


def _lane_row():
    return _iota((1, LANES), 1)


def _head_segment_ones():
    r = _iota((LANES, LANES), 0) // HEAD_DIM
    c = _iota((LANES, LANES), 1) // HEAD_DIM
    return jnp.where(r == c, 1.0, 0.0).astype(BF16)


def _headnorm_rope(a, g, cos, sin):
    hi, lo = _split_bf16(a * a)
    seg = _head_segment_ones()
    ms = (_dot(hi, seg) + _dot(lo, seg)) * (1.0 / HEAD_DIM)
    y = a * lax.rsqrt(ms + NORM_EPS) * g
    half = HEAD_DIM // 2
    upper = pltpu.roll(y, LANES - half, axis=1)
    lower = pltpu.roll(y, half, axis=1)
    first_half = (_lane_row() % HEAD_DIM) < half
    rot = jnp.where(first_half, -upper, lower)
    return y * cos + rot * sin


def _head_masks():
    lane = _lane_row()
    return lane < HEAD_DIM, lane >= HEAD_DIM


def _attn_specs(seq, qcol, kcol, vcol, kv_shared):
    blk = (1, seq, LANES)
    q_spec = pl.BlockSpec(blk, lambda b, p: (b, 0, qcol + p))
    if kv_shared:
        k_spec = pl.BlockSpec(blk, lambda b, p: (b, 0, kcol))
        v_spec = pl.BlockSpec(blk, lambda b, p: (b, 0, vcol))
    else:
        k_spec = pl.BlockSpec(blk, lambda b, p: (b, 0, kcol + p))
        v_spec = pl.BlockSpec(blk, lambda b, p: (b, 0, vcol + p))
    return q_spec, k_spec, v_spec


def _row_spec(seq):
    return pl.BlockSpec((seq, LANES), lambda b, p: (0, 0))


def _gain_spec():
    return pl.BlockSpec((1, LANES), lambda b, p: (0, 0))


def _store_heads(o_ref, r0, outs):
    first, _ = _head_masks()
    o_ref[0, pl.ds(r0, QUERY_BLOCK), :] = jnp.where(first, outs[0], outs[1])


SWA_QBLOCKS_PER_STEP = 4


def _swa_kernel(sinks_ref, q_ref, k_ref, v_ref, cos_ref, sin_ref, gq_ref, gk_ref, o_ref,
                q0_s, q1_s, k_s, v_s):
    p = pl.program_id(1)
    seq = q_ref.shape[1]
    cos, sin = cos_ref[...], sin_ref[...]
    first, second = _head_masks()
    qn = _headnorm_rope(q_ref[0], gq_ref[...], cos, sin) * (HEAD_DIM ** -0.5)
    q0_s[...] = jnp.where(first, qn, 0.0).astype(BF16)
    q1_s[...] = jnp.where(second, qn, 0.0).astype(BF16)
    pairs_per_kv = (A_Q_HEADS // A_KV_HEADS) // 2
    keep = jnp.logical_xor(first, (p // pairs_per_kv) == 1)
    kn = _headnorm_rope(k_ref[0], gk_ref[...], cos, sin)
    k_s[...] = jnp.where(keep, kn, pltpu.roll(kn, HEAD_DIM, axis=1)).astype(BF16)
    v = v_ref[0]
    v_s[...] = jnp.where(keep, v, pltpu.roll(v, HEAD_DIM, axis=1)).astype(BF16)

    qi = _iota((QUERY_BLOCK, QUERY_BLOCK), 0)
    ki = _iota((QUERY_BLOCK, QUERY_BLOCK), 1)

    mask_c = ki <= qi
    per_step = min(SWA_QBLOCKS_PER_STEP, seq // QUERY_BLOCK)

    def qstep(step, carry):
        chains, scores = [], []
        for u in range(per_step):
            i = step * per_step + u
            r0 = pl.multiple_of(i * QUERY_BLOCK, QUERY_BLOCK)
            rp = pl.multiple_of(jnp.maximum(i - 1, 0) * QUERY_BLOCK, QUERY_BLOCK)
            kc, kp = k_s[pl.ds(r0, QUERY_BLOCK), :], k_s[pl.ds(rp, QUERY_BLOCK), :]
            vc, vp = v_s[pl.ds(r0, QUERY_BLOCK), :], v_s[pl.ds(rp, QUERY_BLOCK), :]
            mask_p = jnp.logical_and(ki > qi, i > 0)
            chains.append((r0, vc, vp, mask_p))
            for q_s in (q0_s, q1_s):
                qh = q_s[pl.ds(r0, QUERY_BLOCK), :]
                scores.append((_dot_nt(qh, kc), _dot_nt(qh, kp)))
        probs, denoms = [], []
        for n, (s_cur, s_prev) in enumerate(scores):
            mask_p = chains[n // 2][3]
            sc = jnp.where(mask_c, s_cur, NEG_BIG)
            sp = jnp.where(mask_p, s_prev, NEG_BIG)
            sink = sinks_ref[2 * p + n % 2]
            m = jnp.maximum(jnp.max(sc, axis=1, keepdims=True), jnp.max(sp, axis=1, keepdims=True))
            m = jnp.maximum(m, sink)
            ec, ep = jnp.exp(sc - m), jnp.exp(sp - m)
            denoms.append(jnp.sum(ec, axis=1, keepdims=True) + jnp.sum(ep, axis=1, keepdims=True)
                          + jnp.exp(sink - m))
            probs.append((ec.astype(BF16), ep.astype(BF16)))
        for u, (r0, vc, vp, _) in enumerate(chains):
            outs = [(_dot(probs[2 * u + hh][0], vc) + _dot(probs[2 * u + hh][1], vp)) / denoms[2 * u + hh]
                    for hh in range(2)]
            _store_heads(o_ref, r0, outs)
        return carry

    lax.fori_loop(0, seq // (QUERY_BLOCK * per_step), qstep, 0)


def _swa_attention(proj, sinks, cos, sin, gq, gk):
    b, seq, _ = proj.shape
    n_pairs = A_Q_HEADS // 2
    kcol = A_Q_HEADS * HEAD_DIM // LANES
    vcol = kcol + A_KV_HEADS * HEAD_DIM // LANES
    q_spec, k_spec, v_spec = _attn_specs(seq, 0, kcol, vcol, True)
    return pl.pallas_call(
        _swa_kernel,
        grid=(b, n_pairs),
        in_specs=[pl.BlockSpec(memory_space=pltpu.SMEM), q_spec, k_spec, v_spec,
                  _row_spec(seq), _row_spec(seq), _gain_spec(), _gain_spec()],
        out_specs=pl.BlockSpec((1, seq, LANES), lambda b_, p: (b_, 0, p)),
        out_shape=jax.ShapeDtypeStruct((b, seq, n_pairs * LANES), F32),
        scratch_shapes=[pltpu.VMEM((seq, LANES), BF16)] * 4,
        compiler_params=_cparams(2),
        name="swa_gqa_attention",
    )(sinks, proj, proj, proj, cos, sin, gq, gk)


STICK_GROUP = 4
STICK_QUERY_ROWS = 256


def _stick_kernel(q_ref, k_ref, v_ref, o_ref, k_s, v_s):
    seq = q_ref.shape[1]
    k_s[...] = k_ref[0].astype(BF16)
    v_s[...] = v_ref[0].astype(BF16)
    first, second = _head_masks()
    qr = min(STICK_QUERY_ROWS, seq)
    kw = QUERY_BLOCK
    key_minus_query = _iota((qr, kw), 1) - _iota((qr, kw), 0)
    wr = _iota((2 * kw, 2 * kw), 0) % kw
    wc = _iota((2 * kw, 2 * kw), 1)
    suffix_w = jnp.where(jnp.logical_or(wc >= kw, wr > wc), 1.0, 0.0).astype(BF16)

    def qblock(i, carry):
        r0 = pl.multiple_of(i * qr, qr)
        q = q_ref[0, pl.ds(r0, qr), :] * (HEAD_DIM ** -0.5)
        qhs = [jnp.where(msk, q, 0.0).astype(BF16) for msk in (first, second)]
        n_blocks = (r0 + qr) // kw

        def kgroup(g, st):
            accs, laters = [st[0], st[1]], [st[2], st[3]]
            chains = [(u, hh) for u in range(STICK_GROUP) for hh in range(2)]
            vbs, pasts, zs = [], [], {}
            for u in range(STICK_GROUP):
                j = n_blocks - 1 - (g * STICK_GROUP + u)
                live = j >= 0
                c0 = pl.multiple_of(jnp.maximum(j, 0) * kw, kw)
                kb = k_s[pl.ds(c0, kw), :]
                vbs.append(v_s[pl.ds(c0, kw), :])
                pasts.append(jnp.logical_and(key_minus_query < r0 - c0, live))
                for hh in range(2):
                    zs[u, hh] = _dot_nt(qhs[hh], kb)
            logit, sums = {}, {}
            for u, hh in chains:
                z = zs[u, hh]
                sp = jnp.maximum(z, 0.0) + jnp.log(1.0 + jnp.exp(-jnp.abs(z)))
                neg_log_keep = jnp.where(pasts[u], sp, 0.0)
                logit[u, hh] = z - sp
                hi, lo = _split_bf16(neg_log_keep)
                sums[u, hh] = _dot(jnp.concatenate([hi, lo], axis=1), suffix_w)
            ws = {}
            for u, hh in chains:
                inner, total = sums[u, hh][:, :kw], sums[u, hh][:, kw:]
                ws[u, hh] = jnp.where(pasts[u], jnp.exp(logit[u, hh] - inner - laters[hh]), 0.0).astype(BF16)
                laters[hh] = laters[hh] + total
            for u, hh in chains:
                accs[hh] = accs[hh] + _dot(ws[u, hh], vbs[u])
            return accs[0], accs[1], laters[0], laters[1]

        zero = jnp.zeros((qr, LANES), F32)
        n_groups = (n_blocks + STICK_GROUP - 1) // STICK_GROUP
        st = lax.fori_loop(0, n_groups, kgroup, (zero, zero, zero, zero))
        o_ref[0, pl.ds(r0, qr), :] = jnp.where(first, st[0], st[1])
        return carry

    lax.fori_loop(0, seq // qr, qblock, 0)


def _stick_attention(proj, qcol):
    b, seq, _ = proj.shape
    n_pairs = B_HEADS // 2
    q_spec, k_spec, v_spec = _attn_specs(seq, qcol, qcol + n_pairs, qcol + 2 * n_pairs, False)
    return pl.pallas_call(
        _stick_kernel,
        grid=(b, n_pairs),
        in_specs=[q_spec, k_spec, v_spec],
        out_specs=pl.BlockSpec((1, seq, LANES), lambda b_, p: (b_, 0, p)),
        out_shape=jax.ShapeDtypeStruct((b, seq, n_pairs * LANES), F32),
        scratch_shapes=[pltpu.VMEM((seq, LANES), BF16)] * 2,
        compiler_params=_cparams(2),
        name="stick_breaking_attention",
    )(proj, proj, proj)


def _prep_qkv(q_ref, k_ref, v_ref, cos_ref, sin_ref, gq_ref, gk_ref, q0_s, q1_s, k_s, v_s):
    cos, sin = cos_ref[...], sin_ref[...]
    first, second = _head_masks()
    qn = _headnorm_rope(q_ref[0], gq_ref[...], cos, sin) * (HEAD_DIM ** -0.5)
    q0_s[...] = jnp.where(first, qn, 0.0).astype(BF16)
    q1_s[...] = jnp.where(second, qn, 0.0).astype(BF16)
    kn = _headnorm_rope(k_ref[0], gk_ref[...], cos, sin)
    k_s[...] = kn.astype(BF16)
    v_s[...] = v_ref[0].astype(BF16)
    return qn, kn


DILATED_KEY_TILE = 512
DILATED_QUERY_ROWS = 256


def _dilated_kernel(q_ref, k_ref, v_ref, cos_ref, sin_ref, gq_ref, gk_ref, o_ref,
                    q0_s, q1_s, k_s, v_s):
    seq = q_ref.shape[1]
    _prep_qkv(q_ref, k_ref, v_ref, cos_ref, sin_ref, gq_ref, gk_ref, q0_s, q1_s, k_s, v_s)
    kt = min(DILATED_KEY_TILE, seq)
    qr = min(DILATED_QUERY_ROWS, seq)
    qk = _iota((qr, kt), 0) - _iota((qr, kt), 1)
    on_stride = [jnp.where((qk & (dil - 1)) == 0, 1.0, 0.0) for _, dil in C_PATTERNS]
    first_head, _ = _head_masks()

    def qblock(i, carry):
        r0 = pl.multiple_of(i * qr, qr)
        qhs = [q_s[pl.ds(r0, qr), :] for q_s in (q0_s, q1_s)]

        def ktile(g, st):
            c0 = pl.multiple_of(g * kt, kt)
            d = (r0 - c0) + qk
            count = jnp.zeros(d.shape, F32)
            for (window, _), stride_ok in zip(C_PATTERNS, on_stride):
                count = count + jnp.where(d <= window, stride_ok, 0.0)
            count = jnp.where(d >= 0, count, 0.0)
            kb, vb = k_s[pl.ds(c0, kt), :], v_s[pl.ds(c0, kt), :]
            scores = [_dot_nt(qhs[hh], kb) for hh in range(2)]
            new, prs = [], []
            for hh in range(2):
                m, l, acc = st[3 * hh:3 * hh + 3]
                s = jnp.where(count > 0.0, scores[hh], NEG_BIG)
                m_new = jnp.maximum(m, jnp.max(s, axis=1, keepdims=True))
                pr = count * jnp.exp(s - m_new)
                alpha = jnp.exp(m - m_new)
                new += [m_new, alpha * l + jnp.sum(pr, axis=1, keepdims=True), alpha * acc]
                prs.append(pr.astype(BF16))
            for hh in range(2):
                new[3 * hh + 2] = new[3 * hh + 2] + _dot(prs[hh], vb)
            return tuple(new)

        init = (jnp.full((qr, 1), NEG_BIG, F32), jnp.zeros((qr, 1), F32),
                jnp.zeros((qr, LANES), F32)) * 2
        st = lax.fori_loop(0, (r0 + qr + kt - 1) // kt, ktile, init)
        o_ref[0, pl.ds(r0, qr), :] = jnp.where(first_head, st[2] / st[1], st[5] / st[4])
        return carry

    lax.fori_loop(0, seq // qr, qblock, 0)


def _qkv_attention_call(kernel, name, proj, qcol, n_heads, cos, sin, gq, gk, extra_scratch=()):
    b, seq, _ = proj.shape
    n_pairs = n_heads // 2
    q_spec, k_spec, v_spec = _attn_specs(seq, qcol, qcol + n_pairs, qcol + 2 * n_pairs, False)
    return pl.pallas_call(
        kernel,
        grid=(b, n_pairs),
        in_specs=[q_spec, k_spec, v_spec, _row_spec(seq), _row_spec(seq), _gain_spec(), _gain_spec()],
        out_specs=pl.BlockSpec((1, seq, LANES), lambda b_, p: (b_, 0, p)),
        out_shape=jax.ShapeDtypeStruct((b, seq, n_pairs * LANES), F32),
        scratch_shapes=[pltpu.VMEM((seq, LANES), BF16)] * 4 + list(extra_scratch),
        compiler_params=_cparams(2),
        name=name,
    )(proj, proj, proj, cos, sin, gq, gk)


def _moba_kernel(q_ref, k_ref, v_ref, cos_ref, sin_ref, gq_ref, gk_ref, o_ref,
                 q0_s, q1_s, k_s, v_s, km_s, sel0_s, sel1_s):
    seq = q_ref.shape[1]
    n_blocks = seq // MOBA_BLOCK
    qn, kn = _prep_qkv(q_ref, k_ref, v_ref, cos_ref, sin_ref, gq_ref, gk_ref, q0_s, q1_s, k_s, v_s)
    km_s[...] = jnp.zeros(km_s.shape, F32)
    km_s[0:n_blocks, :] = jnp.mean(kn.reshape(n_blocks, MOBA_BLOCK, LANES), axis=1)
    first, second = _head_masks()

    rows8 = _iota((8, seq), 0)
    own8 = _iota((8, seq), 1) // MOBA_BLOCK
    valid = rows8 < own8
    for msk, sel_s in ((first, sel0_s), (second, sel1_s)):
        gate = _dot3_nt(km_s[...], jnp.where(msk, qn, 0.0))[0:8, :]
        gm = jnp.where(valid, gate, -jnp.inf)
        rank = jnp.zeros((8, seq), F32)
        for n2 in range(n_blocks):
            g2 = gm[n2:n2 + 1, :]
            beats = jnp.logical_or(g2 > gm, jnp.logical_and(g2 == gm, n2 < rows8))
            rank = rank + jnp.where(jnp.logical_and(beats, n2 < own8), 1.0, 0.0)
        sel = jnp.where(jnp.logical_and(valid, rank < float(MOBA_TOPK)), 1.0, 0.0)
        sel = jnp.concatenate([sel, jnp.zeros((LANES - 8, seq), F32)], axis=0)
        sel_s[...] = sel.T

    qrows = MOBA_BLOCK
    lane_sq = _iota((qrows, LANES), 1)
    causal = _iota((qrows, MOBA_BLOCK), 1) <= _iota((qrows, MOBA_BLOCK), 0)
    second_block = _iota((qrows, 2 * MOBA_BLOCK), 1) >= MOBA_BLOCK
    first_head, _ = _head_masks()

    def qblock(own, carry):
        r0 = pl.multiple_of(own * qrows, qrows)
        qhs = [q_s[pl.ds(r0, qrows), :] for q_s in (q0_s, q1_s)]
        sels = [sel_s[pl.ds(r0, qrows), :] for sel_s in (sel0_s, sel1_s)]
        kb, vb = k_s[pl.ds(r0, MOBA_BLOCK), :], v_s[pl.ds(r0, MOBA_BLOCK), :]
        scores = [_dot_nt(qhs[hh], kb) for hh in range(2)]
        init, prs = [], []
        for hh in range(2):
            s = jnp.where(causal, scores[hh], NEG_BIG)
            m = jnp.max(s, axis=1, keepdims=True)
            pr = jnp.exp(s - m)
            init += [m, jnp.sum(pr, axis=1, keepdims=True), None]
            prs.append(pr.astype(BF16))
        for hh in range(2):
            init[3 * hh + 2] = _dot(prs[hh], vb)

        def kpair(g, st):
            c0 = pl.multiple_of(g * 2 * MOBA_BLOCK, 2 * MOBA_BLOCK)
            kb2, vb2 = k_s[pl.ds(c0, 2 * MOBA_BLOCK), :], v_s[pl.ds(c0, 2 * MOBA_BLOCK), :]
            scores = [_dot_nt(qhs[hh], kb2) for hh in range(2)]
            new, prs = [], []
            for hh in range(2):
                m, l, acc = st[3 * hh:3 * hh + 3]
                sel_a = jnp.sum(jnp.where(lane_sq == 2 * g, sels[hh], 0.0), axis=1, keepdims=True)
                sel_b = jnp.sum(jnp.where(lane_sq == 2 * g + 1, sels[hh], 0.0), axis=1, keepdims=True)
                keep = jnp.where(second_block, sel_b, sel_a) > 0.0
                s = jnp.where(keep, scores[hh], NEG_BIG)
                m_new = jnp.maximum(m, jnp.max(s, axis=1, keepdims=True))
                pr = jnp.exp(s - m_new)
                alpha = jnp.exp(m - m_new)
                new += [m_new, alpha * l + jnp.sum(pr, axis=1, keepdims=True), alpha * acc]
                prs.append(pr.astype(BF16))
            for hh in range(2):
                new[3 * hh + 2] = new[3 * hh + 2] + _dot(prs[hh], vb2)
            return tuple(new)

        st = lax.fori_loop(0, (own + 1) // 2, kpair, tuple(init))
        o_ref[0, pl.ds(r0, qrows), :] = jnp.where(first_head, st[2] / st[1], st[5] / st[4])
        return carry

    lax.fori_loop(0, seq // qrows, qblock, 0)


ROUTE_CHUNKS_PER_STEP = 4


def _oddeven_merge_sort_pairs(n):
    pairs = []

    def merge(lo, hi, r):
        step = 2 * r
        if step < hi - lo:
            merge(lo, hi, step)
            merge(lo + r, hi, step)
            pairs.extend((i, i + r) for i in range(lo + r, hi - r, step))
        else:
            pairs.append((lo, lo + r))

    def sort(lo, hi):
        if hi - lo >= 1:
            mid = lo + (hi - lo) // 2
            sort(lo, mid)
            sort(mid + 1, hi)
            merge(lo, hi, 1)

    sort(0, n - 1)
    return pairs


SUBLANES = 8


def _top16_rows(scores, n_rows, vals_refs, idx_refs):
    n_slabs = n_rows // SUBLANES
    sub = _iota((SUBLANES, LANES), 0)
    vals = [[s[SUBLANES * v:SUBLANES * (v + 1), :] for v in range(n_slabs)] for s in scores]
    idxs = [[sub + SUBLANES * v for v in range(n_slabs)] for _ in scores]
    for i, j in _oddeven_merge_sort_pairs(n_slabs):
        for va, ia in zip(vals, idxs):
            a, b = va[i], va[j]
            a_first = jnp.logical_or(a > b, jnp.logical_and(a == b, ia[i] < ia[j]))
            va[i], va[j] = jnp.maximum(a, b), jnp.minimum(a, b)
            ia[i], ia[j] = jnp.where(a_first, ia[i], ia[j]), jnp.where(a_first, ia[j], ia[i])
    for it in range(PEER_TOPK):
        for k, (va, ia) in enumerate(zip(vals, idxs)):
            m = jnp.max(va[0], axis=0, keepdims=True)
            pick = jnp.min(jnp.where(va[0] == m, ia[0], n_rows), axis=0, keepdims=True)
            vals_refs[k][it:it + 1, :] = m
            idx_refs[k][it:it + 1, :] = pick
            win = ia[0] == pick
            depth = PEER_TOPK - 1 - it
            for d in range(min(depth, n_slabs - 1)):
                va[d] = jnp.where(win, va[d + 1], va[d])
                ia[d] = jnp.where(win, ia[d + 1], ia[d])
            if depth >= n_slabs:
                va[n_slabs - 1] = jnp.where(win, -jnp.inf, va[n_slabs - 1])


def _peer_route_kernel(x_ref, g_ref, sh_ref, sc_ref, wqt_ref, sk_ref, h_ref, ids_ref, gts_ref,
                       q_s, val_s, idx_s, ids_s, gts_s):
    tt = x_ref.shape[0]
    n_chunks = tt // LANES
    per_step = ROUTE_CHUNKS_PER_STEP
    half = PEER_D_KEY // 2
    h = _adaln(x_ref[...], g_ref[...], sh_ref[0], sc_ref[0])
    h_ref[...] = h
    qt = _dot_nt(wqt_ref[...], h.astype(BF16))
    for c in range(n_chunks):
        q_s[c] = qt[:, c * LANES:(c + 1) * LANES]
    sk1, sk2 = sk_ref[0], sk_ref[1]
    col_id = _iota((PEER_TOPK, LANES), 0)
    sub = _iota((SUBLANES, LANES), 0)
    col_depth = jnp.zeros((SUBLANES, LANES), I32)
    for a in range(SUBLANES):
        col_depth = jnp.where(sub == a, PEER_CAND_COUNTS[a], col_depth)
    vals = [val_s.at[i] for i in range(2 * per_step)]
    idxs = [idx_s.at[i] for i in range(2 * per_step)]

    def body(step, carry):
        hh = step // (n_chunks // per_step)
        c0 = (step % (n_chunks // per_step)) * per_step
        q0 = pl.multiple_of(hh * PEER_D_KEY, PEER_D_KEY)
        scores = []
        for k in range(per_step):
            scores.append(_dot3(sk1, q_s[c0 + k, pl.ds(q0, half), :]))
            scores.append(_dot3(sk2, q_s[c0 + k, pl.ds(q0 + half, half), :]))
        _top16_rows(scores, PEER_N_KEYS, vals, idxs)
        state = []
        for k in range(per_step):
            v1, i1, v2, i2 = vals[2 * k], idxs[2 * k], vals[2 * k + 1], idxs[2 * k + 1]
            v1x, e1x = v1[0:SUBLANES, :], i1[0:SUBLANES, :] * PEER_N_KEYS
            xs = [jnp.where(col_depth > b, v1x + v2[b:b + 1, :], -jnp.inf) for b in range(PEER_TOPK)]
            ex = [e1x + i2[b:b + 1, :] for b in range(PEER_TOPK)]
            y = v1[SUBLANES:PEER_TOPK, :] + v2[0:1, :]
            ey = i1[SUBLANES:PEER_TOPK, :] * PEER_N_KEYS + i2[0:1, :]
            state.append([xs, ex, y, ey])
        for it in range(PEER_TOPK):
            for k in range(per_step):
                xs, ex, y, ey = state[k]
                heads = jnp.concatenate([xs[0], y], axis=0)
                m = jnp.max(heads, axis=0, keepdims=True)
                pick = jnp.min(jnp.where(heads == m, col_id, PEER_TOPK), axis=0, keepdims=True)
                win = col_id == pick
                eids = jnp.where(win, jnp.concatenate([ex[0], ey], axis=0), 0)
                idxs[2 * k][it:it + 1, :] = jnp.sum(eids, axis=0, keepdims=True)
                vals[2 * k][it:it + 1, :] = m
                win_x, win_y = win[0:SUBLANES, :], win[SUBLANES:PEER_TOPK, :]
                for dd in range(PEER_TOPK - 1 - it):
                    xs[dd] = jnp.where(win_x, xs[dd + 1], xs[dd])
                    ex[dd] = jnp.where(win_x, ex[dd + 1], ex[dd])
                state[k][2] = jnp.where(win_y, -jnp.inf, y)
        r0 = pl.multiple_of(hh * PEER_TOPK, PEER_TOPK)
        for k in range(per_step):
            top = vals[2 * k][...]
            e = jnp.exp(top - top[0:1, :])
            gts_s[c0 + k, pl.ds(r0, PEER_TOPK), :] = e / jnp.sum(e, axis=0, keepdims=True)
            ids_s[c0 + k, pl.ds(r0, PEER_TOPK), :] = idxs[2 * k][...] * PACK_ROWS
        return carry

    lax.fori_loop(0, PEER_HEADS * n_chunks // per_step, body, 0)
    for c in range(n_chunks):
        rows = slice(c * LANES, (c + 1) * LANES)
        ids_ref[rows, :] = ids_s[c].T
        gts_ref[rows, :] = gts_s[c].T


def _peer_route(x2, g, shift, scale, wqt_bf, sub_keys, seq):
    t, d = x2.shape
    tt = 512
    per_b = seq // tt
    n_chunks = tt // LANES
    nq = wqt_bf.shape[0]
    out_blk = pl.BlockSpec((tt, PEER_SLOTS), lambda i: (i, 0))
    return pl.pallas_call(
        _peer_route_kernel,
        grid=(t // tt,),
        in_specs=[
            pl.BlockSpec((tt, d), lambda i: (i, 0)),
            pl.BlockSpec((1, d), lambda i: (0, 0)),
            pl.BlockSpec((1, 1, d), lambda i: (i // per_b, 0, 0)),
            pl.BlockSpec((1, 1, d), lambda i: (i // per_b, 0, 0)),
            pl.BlockSpec((nq, d), lambda i: (0, 0)),
            pl.BlockSpec(sub_keys.shape, lambda i: (0, 0, 0)),
        ],
        out_specs=[pl.BlockSpec((tt, d), lambda i: (i, 0)), out_blk, out_blk],
        out_shape=[jax.ShapeDtypeStruct((t, d), F32),
                   jax.ShapeDtypeStruct((t, PEER_SLOTS), I32),
                   jax.ShapeDtypeStruct((t, PEER_SLOTS), F32)],
        scratch_shapes=[pltpu.VMEM((n_chunks, nq, LANES), F32),
                        pltpu.VMEM((2 * ROUTE_CHUNKS_PER_STEP, PEER_TOPK, LANES), F32),
                        pltpu.VMEM((2 * ROUTE_CHUNKS_PER_STEP, PEER_TOPK, LANES), I32),
                        pltpu.VMEM((n_chunks, PEER_SLOTS, LANES), I32),
                        pltpu.VMEM((n_chunks, PEER_SLOTS, LANES), F32)],
        compiler_params=_cparams(1),
        name="peer_route",
    )(x2, g, shift, scale, wqt_bf, sub_keys)


def _pack_table(tab):
    e, d = tab.shape
    bits = lax.bitcast_convert_type(tab.astype(BF16), jnp.uint16).astype(U32)
    bits = bits.reshape(e, d // (2 * LANES), 2, LANES)
    words = bits[:, :, 0, :] | (bits[:, :, 1, :] << 16)
    return words.reshape(e * (d // (2 * LANES)), LANES)


def _table_spec(rows):
    return pl.BlockSpec((rows, LANES), lambda i: (0, 0), pipeline_mode=pl.Buffered(1))


def _gelu_exact(a):
    return 0.5 * a * (1.0 + lax.erf(a * (2.0 ** -0.5)))


FEAT_CHUNKS = 8
SLOT_WIDTH = PEER_SLOTS * FEAT_CHUNKS


def _gather_rows(ids_ref, base, tbl_ref, slot):
    for j in range(PEER_SLOTS):
        if j % ID_VIEW == 0:
            ids_part = ids_ref.at[pl.ds(base + j, ID_VIEW)]
        row0 = pl.multiple_of(ids_part[j % ID_VIEW], PACK_ROWS)
        slot[PACK_ROWS * j:PACK_ROWS * (j + 1), :] = tbl_ref[pl.ds(row0, PACK_ROWS), :]


def _pipelined_tokens(tt, ids_ref, tbl_ref, slots, compute):
    group = len(slots)
    n_steps = tt // group
    tiles = group // SUBLANES

    def gather(t, slot):
        _gather_rows(ids_ref, t * PEER_SLOTS, tbl_ref, slot)

    def compute_group(q):
        for k in range(group):
            compute(group * q + k, tiles * q + k // SUBLANES, k % SUBLANES, slots[k])

    for k in range(group):
        gather(k, slots[k])

    def step(q, carry):
        compute_group(q)
        for k in range(group):
            gather(group * (q + 1) + k, slots[k])
        return carry

    lax.fori_loop(0, n_steps - 1, step, 0)
    compute_group(n_steps - 1)


def _chunk_diag():
    return (_iota((FEAT_CHUNKS, SLOT_WIDTH), 1) % FEAT_CHUNKS) == _iota((FEAT_CHUNKS, SLOT_WIDTH), 0)


def _peer_u_kernel(ids_ref, h_ref, g_ref, tbl_ref, coef_ref, *scratch):
    slots, (hx_s, rs_s) = scratch[:N_SLOTS], scratch[N_SLOTS:]
    tt = h_ref.shape[0]
    for c in range(FEAT_CHUNKS):
        hx_s[pl.ds(c, tt, stride=FEAT_CHUNKS), :] = h_ref[:, c * LANES:(c + 1) * LANES]
    diag = _chunk_diag()

    def compute(t, tile, sub, slot):
        rows = pltpu.bitcast(slot[...], BF16)
        x8 = hx_s[pl.ds(pl.multiple_of(t * FEAT_CHUNKS, FEAT_CHUNKS), FEAT_CHUNKS), :]
        part = _dot_nt(x8.astype(BF16), rows)
        rs_s[tile, sub:sub + 1, :] = jnp.sum(jnp.where(diag, part, 0.0), axis=0, keepdims=True)

    _pipelined_tokens(tt, ids_ref, tbl_ref, slots, compute)
    group = jnp.where(_iota((SLOT_WIDTH, PEER_SLOTS), 0) // FEAT_CHUNKS == _iota((SLOT_WIDTH, PEER_SLOTS), 1),
                      1.0, 0.0).astype(BF16)
    hi, lo = _split_bf16(rs_s[...].reshape(tt, SLOT_WIDTH))
    act = _dot(hi, group) + _dot(lo, group)
    coef_ref[...] = g_ref[...] * _gelu_exact(act)


N_SLOTS = 16
ID_VIEW = 16
PEER_TOKEN_TILE = 512


def _slot_scratch():
    return [pltpu.VMEM((PEER_SLOTS * PACK_ROWS, LANES), U32)] * N_SLOTS


def _peer_u(ids_flat, h, gates, table, tt):
    t, d = h.shape
    return pl.pallas_call(
        _peer_u_kernel,
        grid=(t // tt,),
        in_specs=[
            pl.BlockSpec((tt * PEER_SLOTS,), lambda i: (i,), memory_space=pltpu.SMEM),
            pl.BlockSpec((tt, d), lambda i: (i, 0)),
            pl.BlockSpec((tt, PEER_SLOTS), lambda i: (i, 0)),
            _table_spec(table.shape[0]),
        ],
        out_specs=pl.BlockSpec((tt, PEER_SLOTS), lambda i: (i, 0)),
        out_shape=jax.ShapeDtypeStruct((t, PEER_SLOTS), F32),
        scratch_shapes=_slot_scratch() + [
                        pltpu.VMEM((tt * FEAT_CHUNKS, LANES), F32),
                        pltpu.VMEM((tt // SUBLANES, SUBLANES, SLOT_WIDTH), F32)],
        compiler_params=_cparams(1, TABLE_VMEM_LIMIT),
        name="peer_expert_in",
    )(ids_flat, h, gates, table)


def _peer_v_kernel(ids_ref, coef_ref, x_ref, gate_ref, tbl_ref, o_ref, *scratch):
    slots, (ce_hi_s, ce_lo_s, res_s) = scratch[:N_SLOTS], scratch[N_SLOTS:]
    tt = x_ref.shape[0]
    spread = jnp.where(_iota((PEER_SLOTS, SLOT_WIDTH), 1) // FEAT_CHUNKS == _iota((PEER_SLOTS, SLOT_WIDTH), 0),
                       1.0, 0.0).astype(BF16)
    hi, lo = _split_bf16(coef_ref[...])
    ce_hi_s[...] = _dot(hi, spread).reshape(ce_hi_s.shape)
    ce_lo_s[...] = _dot(lo, spread).reshape(ce_lo_s.shape)
    diag = _chunk_diag()

    def compute(t, tile, sub, slot):
        rows = pltpu.bitcast(slot[...], BF16)
        a_hi = jnp.where(diag, ce_hi_s[tile, sub:sub + 1, :], 0.0)
        a_lo = jnp.where(diag, ce_lo_s[tile, sub:sub + 1, :], 0.0)
        both = _dot(jnp.concatenate([a_hi, a_lo], axis=0).astype(BF16), rows)
        r0 = pl.multiple_of(t * FEAT_CHUNKS, FEAT_CHUNKS)
        res_s[pl.ds(r0, FEAT_CHUNKS), :] = both[0:FEAT_CHUNKS, :] + both[FEAT_CHUNKS:2 * FEAT_CHUNKS, :]

    _pipelined_tokens(tt, ids_ref, tbl_ref, slots, compute)
    for c in range(FEAT_CHUNKS):
        cols = slice(c * LANES, (c + 1) * LANES)
        y = res_s[pl.ds(c, tt, stride=FEAT_CHUNKS), :]
        o_ref[:, cols] = x_ref[:, cols] + gate_ref[0][:, cols] * y


def _peer_v(ids_flat, coef, x2, gate, table, seq, tt):
    t, d = x2.shape
    per_b = seq // tt
    blk = pl.BlockSpec((tt, d), lambda i: (i, 0))
    return pl.pallas_call(
        _peer_v_kernel,
        grid=(t // tt,),
        in_specs=[
            pl.BlockSpec((tt * PEER_SLOTS,), lambda i: (i,), memory_space=pltpu.SMEM),
            pl.BlockSpec((tt, PEER_SLOTS), lambda i: (i, 0)),
            blk,
            pl.BlockSpec((1, 1, d), lambda i: (i // per_b, 0, 0)),
            _table_spec(table.shape[0]),
        ],
        out_specs=blk,
        out_shape=jax.ShapeDtypeStruct((t, d), F32),
        scratch_shapes=_slot_scratch() + [
                        pltpu.VMEM((tt // SUBLANES, SUBLANES, SLOT_WIDTH), F32),
                        pltpu.VMEM((tt // SUBLANES, SUBLANES, SLOT_WIDTH), F32),
                        pltpu.VMEM((tt * FEAT_CHUNKS, LANES), F32)],
        compiler_params=_cparams(1, TABLE_VMEM_LIMIT),
        name="peer_expert_out",
    )(ids_flat, coef, x2, gate, table)


def _peer_ffn(x2, g, shift, scale, gate, wq, sub_keys, table_u, table_v, seq):
    t, d = x2.shape
    tt = PEER_TOKEN_TILE
    h, ids, gates = _peer_route(x2, g, shift, scale, wq.T.astype(BF16), sub_keys, seq)
    ids_flat = ids.reshape(t * PEER_SLOTS)
    coef = _peer_u(ids_flat, h, gates, table_u, tt)
    return _peer_v(ids_flat, coef, x2, gate, table_v, seq, tt)


def _rope_tables(seq):
    half = HEAD_DIM // 2
    inv_freq = ROPE_THETA ** (-jnp.arange(half, dtype=F32) / half)
    ang = jnp.arange(seq).astype(F32)[:, None] * inv_freq[None, :]
    reps = LANES // half
    return jnp.tile(jnp.cos(ang), (1, reps)), jnp.tile(jnp.sin(ang), (1, reps))


def _two_heads(gain):
    return jnp.tile(gain.reshape(1, HEAD_DIM), (1, LANES // HEAD_DIM))


def kernel(x, c, ada_w, ada_b, norm_mix_g, norm_ffn_g, w_in_ab, w_out_ab, sinks_a, qnorm_a, knorm_a,
           w_in_cd, w_out_cd, qnorm_c, knorm_c, qnorm_d, knorm_d, peer_wq, peer_subkeys, peer_u, peer_v):
    b, seq, d = x.shape
    depth = ada_w.shape[0]
    t = b * seq
    cos, sin = _rope_tables(seq)
    mod = _modulation(c, ada_w, ada_b)
    x2 = x.reshape(t, d)
    for layer in range(depth):
        shift_m, scale_m, gate_m, shift_f, scale_f, gate_f = [
            m.reshape(b, 1, d) for m in jnp.split(mod[layer], 6, axis=-1)]
        g_mix = norm_mix_g[layer].reshape(1, d)
        i = layer // 2
        if layer % 2 == 0:
            proj = _norm_proj(x2, g_mix, shift_m, scale_m, w_in_ab[i].astype(BF16), seq)
            proj = proj.reshape(b, seq, -1)
            ya = _swa_attention(proj, sinks_a[i], cos, sin, _two_heads(qnorm_a[i]), _two_heads(knorm_a[i]))
            b_col = (A_Q_HEADS + 2 * A_KV_HEADS) * HEAD_DIM // LANES
            yb = _stick_attention(proj, b_col)
            w_out = w_out_ab[i]
        else:
            proj = _norm_proj(x2, g_mix, shift_m, scale_m, w_in_cd[i].astype(BF16), seq)
            proj = proj.reshape(b, seq, -1)
            ya = _qkv_attention_call(_dilated_kernel, "dilated_attention", proj, 0, C_HEADS, cos, sin,
                                     _two_heads(qnorm_c[i]), _two_heads(knorm_c[i]))
            d_col = 3 * C_HEADS * HEAD_DIM // LANES
            yb = _qkv_attention_call(_moba_kernel, "moba_attention", proj, d_col, D_HEADS, cos, sin,
                                     _two_heads(qnorm_d[i]), _two_heads(knorm_d[i]),
                                     extra_scratch=(pltpu.VMEM((LANES, LANES), F32),
                                                    pltpu.VMEM((seq, LANES), F32),
                                                    pltpu.VMEM((seq, LANES), F32)))
            w_out = w_out_cd[i]
        x2 = _out_proj(x2, ya.reshape(t, -1), yb.reshape(t, -1), w_out.astype(BF16), gate_m, seq)
        x2 = _peer_ffn(x2, norm_ffn_g[layer].reshape(1, d), shift_f, scale_f, gate_f,
                       peer_wq[layer], peer_subkeys[layer],
                       _pack_table(peer_u[layer]), _pack_table(peer_v[layer]), seq)
    return x2.reshape(b, seq, d)
```

```python
import jax
import jax.numpy as jnp
from jax import lax
from jax.experimental import pallas as pl
from jax.experimental.pallas import tpu as pltpu

F32 = jnp.float32
BF16 = jnp.bfloat16
I32 = jnp.int32
U32 = jnp.uint32

HEAD_DIM = 64
ROPE_THETA = 10000.0
NORM_EPS = 1e-6
LANES = 128
QUERY_BLOCK = 128
A_Q_HEADS, A_KV_HEADS = 8, 2
B_HEADS = C_HEADS = D_HEADS = 8
C_PATTERNS = ((128, 1), (512, 4), (2048, 16))
MOBA_BLOCK, MOBA_TOPK = 256, 3
PEER_HEADS, PEER_N_KEYS, PEER_TOPK, PEER_D_KEY = 8, 128, 16, 256
PEER_SLOTS = PEER_HEADS * PEER_TOPK
NEG_BIG = -1e30
PEER_CAND_COUNTS = tuple(PEER_TOPK // (a + 1) for a in range(PEER_TOPK))
PACK_ROWS = 4
MIB = 1024 * 1024
V7X_VMEM_BYTES = 64 * MIB
STREAM_VMEM_LIMIT = 40 * MIB
TABLE_VMEM_LIMIT = V7X_VMEM_BYTES - 8 * MIB


def _cparams(n_axes, vmem_bytes=STREAM_VMEM_LIMIT):
    return pltpu.CompilerParams(
        dimension_semantics=("arbitrary",) * n_axes,
        vmem_limit_bytes=vmem_bytes)


def _split_bf16(a):
    hi = a.astype(BF16)
    lo = (a - hi.astype(F32)).astype(BF16)
    return hi, lo


def _dot(a, b):
    return jnp.dot(a, b, preferred_element_type=F32)


def _dot_nt(a, b):
    return lax.dot_general(a, b, (((1,), (1,)), ((), ())), preferred_element_type=F32)


def _dot3(a, b):
    ah, al = _split_bf16(a)
    bh, bl = _split_bf16(b)
    return _dot(ah, bh) + _dot(ah, bl) + _dot(al, bh)


def _dot3_nt(a, b):
    ah, al = _split_bf16(a)
    bh, bl = _split_bf16(b)
    return _dot_nt(ah, bh) + _dot_nt(ah, bl) + _dot_nt(al, bh)


def _iota(shape, dim):
    return lax.broadcasted_iota(I32, shape, dim)


def _mod_kernel(c_ref, w_ref, b_ref, o_ref):
    c = c_ref[...]
    cond = c * jax.nn.sigmoid(c)
    o_ref[0] = _dot3(cond, w_ref[0]) + b_ref[0]


def _modulation(c, ada_w, ada_b):
    depth, d, n = ada_w.shape
    b = c.shape[0]
    tn = 1024
    return pl.pallas_call(
        _mod_kernel,
        grid=(depth, n // tn),
        in_specs=[
            pl.BlockSpec((b, d), lambda l, j: (0, 0)),
            pl.BlockSpec((1, d, tn), lambda l, j: (l, 0, j)),
            pl.BlockSpec((1, 1, tn), lambda l, j: (l, 0, j)),
        ],
        out_specs=pl.BlockSpec((1, b, tn), lambda l, j: (l, 0, j)),
        out_shape=jax.ShapeDtypeStruct((depth, b, n), F32),
        compiler_params=_cparams(2),
        name="adaln_modulation",
    )(c, ada_w, ada_b.reshape(depth, 1, n))


def _adaln(x, g, shift, scale):
    ms = jnp.mean(x * x, axis=-1, keepdims=True)
    y = x * lax.rsqrt(ms + NORM_EPS) * g
    return y * (1.0 + scale) + shift


def _norm_proj_kernel(x_ref, g_ref, sh_ref, sc_ref, w_ref, o_ref):
    h = _adaln(x_ref[...], g_ref[...], sh_ref[0], sc_ref[0])
    o_ref[...] = _dot(h.astype(BF16), w_ref[...])


def _norm_proj(x2, g, shift, scale, w_bf, seq):
    t, d = x2.shape
    n = w_bf.shape[1]
    tt = 512
    per_b = seq // tt
    return pl.pallas_call(
        _norm_proj_kernel,
        grid=(t // tt,),
        in_specs=[
            pl.BlockSpec((tt, d), lambda i: (i, 0)),
            pl.BlockSpec((1, d), lambda i: (0, 0)),
            pl.BlockSpec((1, 1, d), lambda i: (i // per_b, 0, 0)),
            pl.BlockSpec((1, 1, d), lambda i: (i // per_b, 0, 0)),
            pl.BlockSpec((d, n), lambda i: (0, 0)),
        ],
        out_specs=pl.BlockSpec((tt, n), lambda i: (i, 0)),
        out_shape=jax.ShapeDtypeStruct((t, n), F32),
        compiler_params=_cparams(1),
        name="adaln_in_proj",
    )(x2, g, shift, scale, w_bf)


def _out_proj_kernel(x_ref, ya_ref, yb_ref, w_ref, gate_ref, o_ref):
    half = ya_ref.shape[1]
    y = _dot(ya_ref[...].astype(BF16), w_ref[0:half, :])
    y = y + _dot(yb_ref[...].astype(BF16), w_ref[half:2 * half, :])
    o_ref[...] = x_ref[...] + gate_ref[0] * y


def _out_proj(x2, ya, yb, w_bf, gate, seq):
    t, d = x2.shape
    half = ya.shape[1]
    tt = 512
    per_b = seq // tt
    return pl.pallas_call(
        _out_proj_kernel,
        grid=(t // tt,),
        in_specs=[
            pl.BlockSpec((tt, d), lambda i: (i, 0)),
            pl.BlockSpec((tt, half), lambda i: (i, 0)),
            pl.BlockSpec((tt, half), lambda i: (i, 0)),
            pl.BlockSpec((2 * half, d), lambda i: (0, 0)),
            pl.BlockSpec((1, 1, d), lambda i: (i // per_b, 0, 0)),
        ],
        out_specs=pl.BlockSpec((tt, d), lambda i: (i, 0)),
        out_shape=jax.ShapeDtypeStruct((t, d), F32),
        compiler_params=_cparams(1),
        name="mixer_out_proj",
    )(x2, ya, yb, w_bf, gate)


def _lane_row():
    return _iota((1, LANES), 1)


def _head_segment_ones():
    r = _iota((LANES, LANES), 0) // HEAD_DIM
    c = _iota((LANES, LANES), 1) // HEAD_DIM
    return jnp.where(r == c, 1.0, 0.0).astype(BF16)


def _headnorm_rope(a, g, cos, sin):
    hi, lo = _split_bf16(a * a)
    seg = _head_segment_ones()
    ms = (_dot(hi, seg) + _dot(lo, seg)) * (1.0 / HEAD_DIM)
    y = a * lax.rsqrt(ms + NORM_EPS) * g
    half = HEAD_DIM // 2
    upper = pltpu.roll(y, LANES - half, axis=1)
    lower = pltpu.roll(y, half, axis=1)
    first_half = (_lane_row() % HEAD_DIM) < half
    rot = jnp.where(first_half, -upper, lower)
    return y * cos + rot * sin


def _head_masks():
    lane = _lane_row()
    return lane < HEAD_DIM, lane >= HEAD_DIM


def _attn_specs(seq, qcol, kcol, vcol, kv_shared):
    blk = (1, seq, LANES)
    q_spec = pl.BlockSpec(blk, lambda b, p: (b, 0, qcol + p))
    if kv_shared:
        k_spec = pl.BlockSpec(blk, lambda b, p: (b, 0, kcol))
        v_spec = pl.BlockSpec(blk, lambda b, p: (b, 0, vcol))
    else:
        k_spec = pl.BlockSpec(blk, lambda b, p: (b, 0, kcol + p))
        v_spec = pl.BlockSpec(blk, lambda b, p: (b, 0, vcol + p))
    return q_spec, k_spec, v_spec


def _row_spec(seq):
    return pl.BlockSpec((seq, LANES), lambda b, p: (0, 0))


def _gain_spec():
    return pl.BlockSpec((1, LANES), lambda b, p: (0, 0))


def _store_heads(o_ref, r0, outs):
    first, _ = _head_masks()
    o_ref[0, pl.ds(r0, QUERY_BLOCK), :] = jnp.where(first, outs[0], outs[1])


SWA_QBLOCKS_PER_STEP = 4


def _swa_kernel(sinks_ref, q_ref, k_ref, v_ref, cos_ref, sin_ref, gq_ref, gk_ref, o_ref,
                q0_s, q1_s, k_s, v_s):
    p = pl.program_id(1)
    seq = q_ref.shape[1]
    cos, sin = cos_ref[...], sin_ref[...]
    first, second = _head_masks()
    qn = _headnorm_rope(q_ref[0], gq_ref[...], cos, sin) * (HEAD_DIM ** -0.5)
    q0_s[...] = jnp.where(first, qn, 0.0).astype(BF16)
    q1_s[...] = jnp.where(second, qn, 0.0).astype(BF16)
    pairs_per_kv = (A_Q_HEADS // A_KV_HEADS) // 2
    keep = jnp.logical_xor(first, (p // pairs_per_kv) == 1)
    kn = _headnorm_rope(k_ref[0], gk_ref[...], cos, sin)
    k_s[...] = jnp.where(keep, kn, pltpu.roll(kn, HEAD_DIM, axis=1)).astype(BF16)
    v = v_ref[0]
    v_s[...] = jnp.where(keep, v, pltpu.roll(v, HEAD_DIM, axis=1)).astype(BF16)

    qi = _iota((QUERY_BLOCK, QUERY_BLOCK), 0)
    ki = _iota((QUERY_BLOCK, QUERY_BLOCK), 1)

    mask_c = ki <= qi
    per_step = min(SWA_QBLOCKS_PER_STEP, seq // QUERY_BLOCK)

    def qstep(step, carry):
        chains, scores = [], []
        for u in range(per_step):
            i = step * per_step + u
            r0 = pl.multiple_of(i * QUERY_BLOCK, QUERY_BLOCK)
            rp = pl.multiple_of(jnp.maximum(i - 1, 0) * QUERY_BLOCK, QUERY_BLOCK)
            kc, kp = k_s[pl.ds(r0, QUERY_BLOCK), :], k_s[pl.ds(rp, QUERY_BLOCK), :]
            vc, vp = v_s[pl.ds(r0, QUERY_BLOCK), :], v_s[pl.ds(rp, QUERY_BLOCK), :]
            mask_p = jnp.logical_and(ki > qi, i > 0)
            chains.append((r0, vc, vp, mask_p))
            for q_s in (q0_s, q1_s):
                qh = q_s[pl.ds(r0, QUERY_BLOCK), :]
                scores.append((_dot_nt(qh, kc), _dot_nt(qh, kp)))
        probs, denoms = [], []
        for n, (s_cur, s_prev) in enumerate(scores):
            mask_p = chains[n // 2][3]
            sc = jnp.where(mask_c, s_cur, NEG_BIG)
            sp = jnp.where(mask_p, s_prev, NEG_BIG)
            sink = sinks_ref[2 * p + n % 2]
            m = jnp.maximum(jnp.max(sc, axis=1, keepdims=True), jnp.max(sp, axis=1, keepdims=True))
            m = jnp.maximum(m, sink)
            ec, ep = jnp.exp(sc - m), jnp.exp(sp - m)
            denoms.append(jnp.sum(ec, axis=1, keepdims=True) + jnp.sum(ep, axis=1, keepdims=True)
                          + jnp.exp(sink - m))
            probs.append((ec.astype(BF16), ep.astype(BF16)))
        for u, (r0, vc, vp, _) in enumerate(chains):
            outs = [(_dot(probs[2 * u + hh][0], vc) + _dot(probs[2 * u + hh][1], vp)) / denoms[2 * u + hh]
                    for hh in range(2)]
            _store_heads(o_ref, r0, outs)
        return carry

    lax.fori_loop(0, seq // (QUERY_BLOCK * per_step), qstep, 0)


def _swa_attention(proj, sinks, cos, sin, gq, gk):
    b, seq, _ = proj.shape
    n_pairs = A_Q_HEADS // 2
    kcol = A_Q_HEADS * HEAD_DIM // LANES
    vcol = kcol + A_KV_HEADS * HEAD_DIM // LANES
    q_spec, k_spec, v_spec = _attn_specs(seq, 0, kcol, vcol, True)
    return pl.pallas_call(
        _swa_kernel,
        grid=(b, n_pairs),
        in_specs=[pl.BlockSpec(memory_space=pltpu.SMEM), q_spec, k_spec, v_spec,
                  _row_spec(seq), _row_spec(seq), _gain_spec(), _gain_spec()],
        out_specs=pl.BlockSpec((1, seq, LANES), lambda b_, p: (b_, 0, p)),
        out_shape=jax.ShapeDtypeStruct((b, seq, n_pairs * LANES), F32),
        scratch_shapes=[pltpu.VMEM((seq, LANES), BF16)] * 4,
        compiler_params=_cparams(2),
        name="swa_gqa_attention",
    )(sinks, proj, proj, proj, cos, sin, gq, gk)


STICK_GROUP = 4
STICK_QUERY_ROWS = 256


def _stick_kernel(q_ref, k_ref, v_ref, o_ref, k_s, v_s):
    seq = q_ref.shape[1]
    k_s[...] = k_ref[0].astype(BF16)
    v_s[...] = v_ref[0].astype(BF16)
    first, second = _head_masks()
    qr = min(STICK_QUERY_ROWS, seq)
    kw = QUERY_BLOCK
    key_minus_query = _iota((qr, kw), 1) - _iota((qr, kw), 0)
    wr = _iota((2 * kw, 2 * kw), 0) % kw
    wc = _iota((2 * kw, 2 * kw), 1)
    suffix_w = jnp.where(jnp.logical_or(wc >= kw, wr > wc), 1.0, 0.0).astype(BF16)

    def qblock(i, carry):
        r0 = pl.multiple_of(i * qr, qr)
        q = q_ref[0, pl.ds(r0, qr), :] * (HEAD_DIM ** -0.5)
        qhs = [jnp.where(msk, q, 0.0).astype(BF16) for msk in (first, second)]
        n_blocks = (r0 + qr) // kw

        def kgroup(g, st):
            accs, laters = [st[0], st[1]], [st[2], st[3]]
            chains = [(u, hh) for u in range(STICK_GROUP) for hh in range(2)]
            vbs, pasts, zs = [], [], {}
            for u in range(STICK_GROUP):
                j = n_blocks - 1 - (g * STICK_GROUP + u)
                live = j >= 0
                c0 = pl.multiple_of(jnp.maximum(j, 0) * kw, kw)
                kb = k_s[pl.ds(c0, kw), :]
                vbs.append(v_s[pl.ds(c0, kw), :])
                pasts.append(jnp.logical_and(key_minus_query < r0 - c0, live))
                for hh in range(2):
                    zs[u, hh] = _dot_nt(qhs[hh], kb)
            logit, sums = {}, {}
            for u, hh in chains:
                z = zs[u, hh]
                sp = jnp.maximum(z, 0.0) + jnp.log(1.0 + jnp.exp(-jnp.abs(z)))
                neg_log_keep = jnp.where(pasts[u], sp, 0.0)
                logit[u, hh] = z - sp
                hi, lo = _split_bf16(neg_log_keep)
                sums[u, hh] = _dot(jnp.concatenate([hi, lo], axis=1), suffix_w)
            ws = {}
            for u, hh in chains:
                inner, total = sums[u, hh][:, :kw], sums[u, hh][:, kw:]
                ws[u, hh] = jnp.where(pasts[u], jnp.exp(logit[u, hh] - inner - laters[hh]), 0.0).astype(BF16)
                laters[hh] = laters[hh] + total
            for u, hh in chains:
                accs[hh] = accs[hh] + _dot(ws[u, hh], vbs[u])
            return accs[0], accs[1], laters[0], laters[1]

        zero = jnp.zeros((qr, LANES), F32)
        n_groups = (n_blocks + STICK_GROUP - 1) // STICK_GROUP
        st = lax.fori_loop(0, n_groups, kgroup, (zero, zero, zero, zero))
        o_ref[0, pl.ds(r0, qr), :] = jnp.where(first, st[0], st[1])
        return carry

    lax.fori_loop(0, seq // qr, qblock, 0)


def _stick_attention(proj, qcol):
    b, seq, _ = proj.shape
    n_pairs = B_HEADS // 2
    q_spec, k_spec, v_spec = _attn_specs(seq, qcol, qcol + n_pairs, qcol + 2 * n_pairs, False)
    return pl.pallas_call(
        _stick_kernel,
        grid=(b, n_pairs),
        in_specs=[q_spec, k_spec, v_spec],
        out_specs=pl.BlockSpec((1, seq, LANES), lambda b_, p: (b_, 0, p)),
        out_shape=jax.ShapeDtypeStruct((b, seq, n_pairs * LANES), F32),
        scratch_shapes=[pltpu.VMEM((seq, LANES), BF16)] * 2,
        compiler_params=_cparams(2),
        name="stick_breaking_attention",
    )(proj, proj, proj)


def _prep_qkv(q_ref, k_ref, v_ref, cos_ref, sin_ref, gq_ref, gk_ref, q0_s, q1_s, k_s, v_s):
    cos, sin = cos_ref[...], sin_ref[...]
    first, second = _head_masks()
    qn = _headnorm_rope(q_ref[0], gq_ref[...], cos, sin) * (HEAD_DIM ** -0.5)
    q0_s[...] = jnp.where(first, qn, 0.0).astype(BF16)
    q1_s[...] = jnp.where(second, qn, 0.0).astype(BF16)
    kn = _headnorm_rope(k_ref[0], gk_ref[...], cos, sin)
    k_s[...] = kn.astype(BF16)
    v_s[...] = v_ref[0].astype(BF16)
    return qn, kn


DILATED_KEY_TILE = 512
DILATED_QUERY_ROWS = 256


def _dilated_kernel(q_ref, k_ref, v_ref, cos_ref, sin_ref, gq_ref, gk_ref, o_ref,
                    q0_s, q1_s, k_s, v_s):
    seq = q_ref.shape[1]
    _prep_qkv(q_ref, k_ref, v_ref, cos_ref, sin_ref, gq_ref, gk_ref, q0_s, q1_s, k_s, v_s)
    kt = min(DILATED_KEY_TILE, seq)
    qr = min(DILATED_QUERY_ROWS, seq)
    qk = _iota((qr, kt), 0) - _iota((qr, kt), 1)
    on_stride = [jnp.where((qk & (dil - 1)) == 0, 1.0, 0.0) for _, dil in C_PATTERNS]
    first_head, _ = _head_masks()

    def qblock(i, carry):
        r0 = pl.multiple_of(i * qr, qr)
        qhs = [q_s[pl.ds(r0, qr), :] for q_s in (q0_s, q1_s)]

        def ktile(g, st):
            c0 = pl.multiple_of(g * kt, kt)
            d = (r0 - c0) + qk
            count = jnp.zeros(d.shape, F32)
            for (window, _), stride_ok in zip(C_PATTERNS, on_stride):
                count = count + jnp.where(d <= window, stride_ok, 0.0)
            count = jnp.where(d >= 0, count, 0.0)
            kb, vb = k_s[pl.ds(c0, kt), :], v_s[pl.ds(c0, kt), :]
            scores = [_dot_nt(qhs[hh], kb) for hh in range(2)]
            new, prs = [], []
            for hh in range(2):
                m, l, acc = st[3 * hh:3 * hh + 3]
                s = jnp.where(count > 0.0, scores[hh], NEG_BIG)
                m_new = jnp.maximum(m, jnp.max(s, axis=1, keepdims=True))
                pr = count * jnp.exp(s - m_new)
                alpha = jnp.exp(m - m_new)
                new += [m_new, alpha * l + jnp.sum(pr, axis=1, keepdims=True), alpha * acc]
                prs.append(pr.astype(BF16))
            for hh in range(2):
                new[3 * hh + 2] = new[3 * hh + 2] + _dot(prs[hh], vb)
            return tuple(new)

        init = (jnp.full((qr, 1), NEG_BIG, F32), jnp.zeros((qr, 1), F32),
                jnp.zeros((qr, LANES), F32)) * 2
        st = lax.fori_loop(0, (r0 + qr + kt - 1) // kt, ktile, init)
        o_ref[0, pl.ds(r0, qr), :] = jnp.where(first_head, st[2] / st[1], st[5] / st[4])
        return carry

    lax.fori_loop(0, seq // qr, qblock, 0)


def _qkv_attention_call(kernel, name, proj, qcol, n_heads, cos, sin, gq, gk, extra_scratch=()):
    b, seq, _ = proj.shape
    n_pairs = n_heads // 2
    q_spec, k_spec, v_spec = _attn_specs(seq, qcol, qcol + n_pairs, qcol + 2 * n_pairs, False)
    return pl.pallas_call(
        kernel,
        grid=(b, n_pairs),
        in_specs=[q_spec, k_spec, v_spec, _row_spec(seq), _row_spec(seq), _gain_spec(), _gain_spec()],
        out_specs=pl.BlockSpec((1, seq, LANES), lambda b_, p: (b_, 0, p)),
        out_shape=jax.ShapeDtypeStruct((b, seq, n_pairs * LANES), F32),
        scratch_shapes=[pltpu.VMEM((seq, LANES), BF16)] * 4 + list(extra_scratch),
        compiler_params=_cparams(2),
        name=name,
    )(proj, proj, proj, cos, sin, gq, gk)


def _moba_kernel(q_ref, k_ref, v_ref, cos_ref, sin_ref, gq_ref, gk_ref, o_ref,
                 q0_s, q1_s, k_s, v_s, km_s, sel0_s, sel1_s):
    seq = q_ref.shape[1]
    n_blocks = seq // MOBA_BLOCK
    qn, kn = _prep_qkv(q_ref, k_ref, v_ref, cos_ref, sin_ref, gq_ref, gk_ref, q0_s, q1_s, k_s, v_s)
    km_s[...] = jnp.zeros(km_s.shape, F32)
    km_s[0:n_blocks, :] = jnp.mean(kn.reshape(n_blocks, MOBA_BLOCK, LANES), axis=1)
    first, second = _head_masks()

    rows8 = _iota((8, seq), 0)
    own8 = _iota((8, seq), 1) // MOBA_BLOCK
    valid = rows8 < own8
    for msk, sel_s in ((first, sel0_s), (second, sel1_s)):
        gate = _dot3_nt(km_s[...], jnp.where(msk, qn, 0.0))[0:8, :]
        gm = jnp.where(valid, gate, -jnp.inf)
        rank = jnp.zeros((8, seq), F32)
        for n2 in range(n_blocks):
            g2 = gm[n2:n2 + 1, :]
            beats = jnp.logical_or(g2 > gm, jnp.logical_and(g2 == gm, n2 < rows8))
            rank = rank + jnp.where(jnp.logical_and(beats, n2 < own8), 1.0, 0.0)
        sel = jnp.where(jnp.logical_and(valid, rank < float(MOBA_TOPK)), 1.0, 0.0)
        sel = jnp.concatenate([sel, jnp.zeros((LANES - 8, seq), F32)], axis=0)
        sel_s[...] = sel.T

    qrows = MOBA_BLOCK
    lane_sq = _iota((qrows, LANES), 1)
    causal = _iota((qrows, MOBA_BLOCK), 1) <= _iota((qrows, MOBA_BLOCK), 0)
    second_block = _iota((qrows, 2 * MOBA_BLOCK), 1) >= MOBA_BLOCK
    first_head, _ = _head_masks()

    def qblock(own, carry):
        r0 = pl.multiple_of(own * qrows, qrows)
        qhs = [q_s[pl.ds(r0, qrows), :] for q_s in (q0_s, q1_s)]
        sels = [sel_s[pl.ds(r0, qrows), :] for sel_s in (sel0_s, sel1_s)]
        kb, vb = k_s[pl.ds(r0, MOBA_BLOCK), :], v_s[pl.ds(r0, MOBA_BLOCK), :]
        scores = [_dot_nt(qhs[hh], kb) for hh in range(2)]
        init, prs = [], []
        for hh in range(2):
            s = jnp.where(causal, scores[hh], NEG_BIG)
            m = jnp.max(s, axis=1, keepdims=True)
            pr = jnp.exp(s - m)
            init += [m, jnp.sum(pr, axis=1, keepdims=True), None]
            prs.append(pr.astype(BF16))
        for hh in range(2):
            init[3 * hh + 2] = _dot(prs[hh], vb)

        def kpair(g, st):
            c0 = pl.multiple_of(g * 2 * MOBA_BLOCK, 2 * MOBA_BLOCK)
            kb2, vb2 = k_s[pl.ds(c0, 2 * MOBA_BLOCK), :], v_s[pl.ds(c0, 2 * MOBA_BLOCK), :]
            scores = [_dot_nt(qhs[hh], kb2) for hh in range(2)]
            new, prs = [], []
            for hh in range(2):
                m, l, acc = st[3 * hh:3 * hh + 3]
                sel_a = jnp.sum(jnp.where(lane_sq == 2 * g, sels[hh], 0.0), axis=1, keepdims=True)
                sel_b = jnp.sum(jnp.where(lane_sq == 2 * g + 1, sels[hh], 0.0), axis=1, keepdims=True)
                keep = jnp.where(second_block, sel_b, sel_a) > 0.0
                s = jnp.where(keep, scores[hh], NEG_BIG)
                m_new = jnp.maximum(m, jnp.max(s, axis=1, keepdims=True))
                pr = jnp.exp(s - m_new)
                alpha = jnp.exp(m - m_new)
                new += [m_new, alpha * l + jnp.sum(pr, axis=1, keepdims=True), alpha * acc]
                prs.append(pr.astype(BF16))
            for hh in range(2):
                new[3 * hh + 2] = new[3 * hh + 2] + _dot(prs[hh], vb2)
            return tuple(new)

        st = lax.fori_loop(0, (own + 1) // 2, kpair, tuple(init))
        o_ref[0, pl.ds(r0, qrows), :] = jnp.where(first_head, st[2] / st[1], st[5] / st[4])
        return carry

    lax.fori_loop(0, seq // qrows, qblock, 0)


ROUTE_CHUNKS_PER_STEP = 4


def _oddeven_merge_sort_pairs(n):
    pairs = []

    def merge(lo, hi, r):
        step = 2 * r
        if step < hi - lo:
            merge(lo, hi, step)
            merge(lo + r, hi, step)
            pairs.extend((i, i + r) for i in range(lo + r, hi - r, step))
        else:
            pairs.append((lo, lo + r))

    def sort(lo, hi):
        if hi - lo >= 1:
            mid = lo + (hi - lo) // 2
            sort(lo, mid)
            sort(mid + 1, hi)
            merge(lo, hi, 1)

    sort(0, n - 1)
    return pairs


SUBLANES = 8


def _top16_rows(scores, n_rows, vals_refs, idx_refs):
    n_slabs = n_rows // SUBLANES
    sub = _iota((SUBLANES, LANES), 0)
    vals = [[s[SUBLANES * v:SUBLANES * (v + 1), :] for v in range(n_slabs)] for s in scores]
    idxs = [[sub + SUBLANES * v for v in range(n_slabs)] for _ in scores]
    for i, j in _oddeven_merge_sort_pairs(n_slabs):
        for va, ia in zip(vals, idxs):
            a, b = va[i], va[j]
            a_first = jnp.logical_or(a > b, jnp.logical_and(a == b, ia[i] < ia[j]))
            va[i], va[j] = jnp.maximum(a, b), jnp.minimum(a, b)
            ia[i], ia[j] = jnp.where(a_first, ia[i], ia[j]), jnp.where(a_first, ia[j], ia[i])
    for it in range(PEER_TOPK):
        for k, (va, ia) in enumerate(zip(vals, idxs)):
            m = jnp.max(va[0], axis=0, keepdims=True)
            pick = jnp.min(jnp.where(va[0] == m, ia[0], n_rows), axis=0, keepdims=True)
            vals_refs[k][it:it + 1, :] = m
            idx_refs[k][it:it + 1, :] = pick
            win = ia[0] == pick
            depth = PEER_TOPK - 1 - it
            for d in range(min(depth, n_slabs - 1)):
                va[d] = jnp.where(win, va[d + 1], va[d])
                ia[d] = jnp.where(win, ia[d + 1], ia[d])
            if depth >= n_slabs:
                va[n_slabs - 1] = jnp.where(win, -jnp.inf, va[n_slabs - 1])


def _peer_route_kernel(x_ref, g_ref, sh_ref, sc_ref, wqt_ref, sk_ref, h_ref, ids_ref, gts_ref,
                       q_s, val_s, idx_s, ids_s, gts_s):
    tt = x_ref.shape[0]
    n_chunks = tt // LANES
    per_step = ROUTE_CHUNKS_PER_STEP
    half = PEER_D_KEY // 2
    h = _adaln(x_ref[...], g_ref[...], sh_ref[0], sc_ref[0])
    h_ref[...] = h
    qt = _dot_nt(wqt_ref[...], h.astype(BF16))
    for c in range(n_chunks):
        q_s[c] = qt[:, c * LANES:(c + 1) * LANES]
    sk1, sk2 = sk_ref[0], sk_ref[1]
    col_id = _iota((PEER_TOPK, LANES), 0)
    sub = _iota((SUBLANES, LANES), 0)
    col_depth = jnp.zeros((SUBLANES, LANES), I32)
    for a in range(SUBLANES):
        col_depth = jnp.where(sub == a, PEER_CAND_COUNTS[a], col_depth)
    vals = [val_s.at[i] for i in range(2 * per_step)]
    idxs = [idx_s.at[i] for i in range(2 * per_step)]

    def body(step, carry):
        hh = step // (n_chunks // per_step)
        c0 = (step % (n_chunks // per_step)) * per_step
        q0 = pl.multiple_of(hh * PEER_D_KEY, PEER_D_KEY)
        scores = []
        for k in range(per_step):
            scores.append(_dot3(sk1, q_s[c0 + k, pl.ds(q0, half), :]))
            scores.append(_dot3(sk2, q_s[c0 + k, pl.ds(q0 + half, half), :]))
        _top16_rows(scores, PEER_N_KEYS, vals, idxs)
        state = []
        for k in range(per_step):
            v1, i1, v2, i2 = vals[2 * k], idxs[2 * k], vals[2 * k + 1], idxs[2 * k + 1]
            v1x, e1x = v1[0:SUBLANES, :], i1[0:SUBLANES, :] * PEER_N_KEYS
            xs = [jnp.where(col_depth > b, v1x + v2[b:b + 1, :], -jnp.inf) for b in range(PEER_TOPK)]
            ex = [e1x + i2[b:b + 1, :] for b in range(PEER_TOPK)]
            y = v1[SUBLANES:PEER_TOPK, :] + v2[0:1, :]
            ey = i1[SUBLANES:PEER_TOPK, :] * PEER_N_KEYS + i2[0:1, :]
            state.append([xs, ex, y, ey])
        for it in range(PEER_TOPK):
            for k in range(per_step):
                xs, ex, y, ey = state[k]
                heads = jnp.concatenate([xs[0], y], axis=0)
                m = jnp.max(heads, axis=0, keepdims=True)
                pick = jnp.min(jnp.where(heads == m, col_id, PEER_TOPK), axis=0, keepdims=True)
                win = col_id == pick
                eids = jnp.where(win, jnp.concatenate([ex[0], ey], axis=0), 0)
                idxs[2 * k][it:it + 1, :] = jnp.sum(eids, axis=0, keepdims=True)
                vals[2 * k][it:it + 1, :] = m
                win_x, win_y = win[0:SUBLANES, :], win[SUBLANES:PEER_TOPK, :]
                for dd in range(PEER_TOPK - 1 - it):
                    xs[dd] = jnp.where(win_x, xs[dd + 1], xs[dd])
                    ex[dd] = jnp.where(win_x, ex[dd + 1], ex[dd])
                state[k][2] = jnp.where(win_y, -jnp.inf, y)
        r0 = pl.multiple_of(hh * PEER_TOPK, PEER_TOPK)
        for k in range(per_step):
            top = vals[2 * k][...]
            e = jnp.exp(top - top[0:1, :])
            gts_s[c0 + k, pl.ds(r0, PEER_TOPK), :] = e / jnp.sum(e, axis=0, keepdims=True)
            ids_s[c0 + k, pl.ds(r0, PEER_TOPK), :] = idxs[2 * k][...] * PACK_ROWS
        return carry

    lax.fori_loop(0, PEER_HEADS * n_chunks // per_step, body, 0)
    for c in range(n_chunks):
        rows = slice(c * LANES, (c + 1) * LANES)
        ids_ref[pl.ds(c * LANES * PEER_SLOTS, LANES * PEER_SLOTS)] = ids_s[c].T.reshape(LANES * PEER_SLOTS)
        gts_ref[rows, :] = gts_s[c].T


def _peer_route(x2, g, shift, scale, wqt_bf, sub_keys, seq):
    t, d = x2.shape
    tt = 512
    per_b = seq // tt
    n_chunks = tt // LANES
    nq = wqt_bf.shape[0]
    out_blk = pl.BlockSpec((tt, PEER_SLOTS), lambda i: (i, 0))
    return pl.pallas_call(
        _peer_route_kernel,
        grid=(t // tt,),
        in_specs=[
            pl.BlockSpec((tt, d), lambda i: (i, 0)),
            pl.BlockSpec((1, d), lambda i: (0, 0)),
            pl.BlockSpec((1, 1, d), lambda i: (i // per_b, 0, 0)),
            pl.BlockSpec((1, 1, d), lambda i: (i // per_b, 0, 0)),
            pl.BlockSpec((nq, d), lambda i: (0, 0)),
            pl.BlockSpec(sub_keys.shape, lambda i: (0, 0, 0)),
        ],
        out_specs=[pl.BlockSpec((tt, d), lambda i: (i, 0)),
                   pl.BlockSpec((tt * PEER_SLOTS,), lambda i: (i,)), out_blk],
        out_shape=[jax.ShapeDtypeStruct((t, d), F32),
                   jax.ShapeDtypeStruct((t * PEER_SLOTS,), I32),
                   jax.ShapeDtypeStruct((t, PEER_SLOTS), F32)],
        scratch_shapes=[pltpu.VMEM((n_chunks, nq, LANES), F32),
                        pltpu.VMEM((2 * ROUTE_CHUNKS_PER_STEP, PEER_TOPK, LANES), F32),
                        pltpu.VMEM((2 * ROUTE_CHUNKS_PER_STEP, PEER_TOPK, LANES), I32),
                        pltpu.VMEM((n_chunks, PEER_SLOTS, LANES), I32),
                        pltpu.VMEM((n_chunks, PEER_SLOTS, LANES), F32)],
        compiler_params=_cparams(1),
        name="peer_route",
    )(x2, g, shift, scale, wqt_bf, sub_keys)


def _pack_table(tab):
    e, d = tab.shape
    bits = lax.bitcast_convert_type(tab.astype(BF16), jnp.uint16).astype(U32)
    bits = bits.reshape(e, d // (2 * LANES), 2, LANES)
    words = bits[:, :, 0, :] | (bits[:, :, 1, :] << 16)
    return words.reshape(e * (d // (2 * LANES)), LANES)


def _table_spec(rows):
    return pl.BlockSpec((rows, LANES), lambda i: (0, 0), pipeline_mode=pl.Buffered(1))


def _gelu_exact(a):
    return 0.5 * a * (1.0 + lax.erf(a * (2.0 ** -0.5)))


FEAT_CHUNKS = 8
SLOT_WIDTH = PEER_SLOTS * FEAT_CHUNKS


def _gather_rows(ids_ref, base, tbl_ref, slot):
    for j in range(PEER_SLOTS):
        if j % ID_VIEW == 0:
            ids_part = ids_ref.at[pl.ds(base + j, ID_VIEW)]
        row0 = pl.multiple_of(ids_part[j % ID_VIEW], PACK_ROWS)
        slot[PACK_ROWS * j:PACK_ROWS * (j + 1), :] = tbl_ref[pl.ds(row0, PACK_ROWS), :]


def _pipelined_tokens(tt, ids_ref, tbl_ref, slots, compute):
    group = len(slots)
    n_steps = tt // group
    tiles = group // SUBLANES

    def gather(t, slot):
        _gather_rows(ids_ref, t * PEER_SLOTS, tbl_ref, slot)

    def compute_group(q):
        for k in range(group):
            compute(group * q + k, tiles * q + k // SUBLANES, k % SUBLANES, slots[k])

    for k in range(group):
        gather(k, slots[k])

    def step(q, carry):
        compute_group(q)
        for k in range(group):
            gather(group * (q + 1) + k, slots[k])
        return carry

    lax.fori_loop(0, n_steps - 1, step, 0)
    compute_group(n_steps - 1)


def _chunk_diag():
    return (_iota((FEAT_CHUNKS, SLOT_WIDTH), 1) % FEAT_CHUNKS) == _iota((FEAT_CHUNKS, SLOT_WIDTH), 0)


def _peer_u_kernel(ids_ref, h_ref, g_ref, tbl_ref, coef_ref, *scratch):
    slots, (hx_s, rs_s) = scratch[:N_SLOTS], scratch[N_SLOTS:]
    tt = h_ref.shape[0]
    for c in range(FEAT_CHUNKS):
        hx_s[pl.ds(c, tt, stride=FEAT_CHUNKS), :] = h_ref[:, c * LANES:(c + 1) * LANES]
    diag = _chunk_diag()

    def compute(t, tile, sub, slot):
        rows = pltpu.bitcast(slot[...], BF16)
        x8 = hx_s[pl.ds(pl.multiple_of(t * FEAT_CHUNKS, FEAT_CHUNKS), FEAT_CHUNKS), :]
        part = _dot_nt(x8.astype(BF16), rows)
        rs_s[tile, sub:sub + 1, :] = jnp.sum(jnp.where(diag, part, 0.0), axis=0, keepdims=True)

    _pipelined_tokens(tt, ids_ref, tbl_ref, slots, compute)
    group = jnp.where(_iota((SLOT_WIDTH, PEER_SLOTS), 0) // FEAT_CHUNKS == _iota((SLOT_WIDTH, PEER_SLOTS), 1),
                      1.0, 0.0).astype(BF16)
    hi, lo = _split_bf16(rs_s[...].reshape(tt, SLOT_WIDTH))
    act = _dot(hi, group) + _dot(lo, group)
    coef_ref[...] = g_ref[...] * _gelu_exact(act)


N_SLOTS = 16
ID_VIEW = 16
PEER_TOKEN_TILE = 512


def _slot_scratch():
    return [pltpu.VMEM((PEER_SLOTS * PACK_ROWS, LANES), U32)] * N_SLOTS


def _peer_u(ids_flat, h, gates, table, tt):
    t, d = h.shape
    return pl.pallas_call(
        _peer_u_kernel,
        grid=(t // tt,),
        in_specs=[
            pl.BlockSpec((tt * PEER_SLOTS,), lambda i: (i,), memory_space=pltpu.SMEM),
            pl.BlockSpec((tt, d), lambda i: (i, 0)),
            pl.BlockSpec((tt, PEER_SLOTS), lambda i: (i, 0)),
            _table_spec(table.shape[0]),
        ],
        out_specs=pl.BlockSpec((tt, PEER_SLOTS), lambda i: (i, 0)),
        out_shape=jax.ShapeDtypeStruct((t, PEER_SLOTS), F32),
        scratch_shapes=_slot_scratch() + [
                        pltpu.VMEM((tt * FEAT_CHUNKS, LANES), F32),
                        pltpu.VMEM((tt // SUBLANES, SUBLANES, SLOT_WIDTH), F32)],
        compiler_params=_cparams(1, TABLE_VMEM_LIMIT),
        name="peer_expert_in",
    )(ids_flat, h, gates, table)


def _peer_v_kernel(ids_ref, coef_ref, x_ref, gate_ref, tbl_ref, o_ref, *scratch):
    slots, (ce_hi_s, ce_lo_s, res_s) = scratch[:N_SLOTS], scratch[N_SLOTS:]
    tt = x_ref.shape[0]
    spread = jnp.where(_iota((PEER_SLOTS, SLOT_WIDTH), 1) // FEAT_CHUNKS == _iota((PEER_SLOTS, SLOT_WIDTH), 0),
                       1.0, 0.0).astype(BF16)
    hi, lo = _split_bf16(coef_ref[...])
    ce_hi_s[...] = _dot(hi, spread).reshape(ce_hi_s.shape)
    ce_lo_s[...] = _dot(lo, spread).reshape(ce_lo_s.shape)
    diag = _chunk_diag()

    def compute(t, tile, sub, slot):
        rows = pltpu.bitcast(slot[...], BF16)
        a_hi = jnp.where(diag, ce_hi_s[tile, sub:sub + 1, :], 0.0)
        a_lo = jnp.where(diag, ce_lo_s[tile, sub:sub + 1, :], 0.0)
        both = _dot(jnp.concatenate([a_hi, a_lo], axis=0).astype(BF16), rows)
        r0 = pl.multiple_of(t * FEAT_CHUNKS, FEAT_CHUNKS)
        res_s[pl.ds(r0, FEAT_CHUNKS), :] = both[0:FEAT_CHUNKS, :] + both[FEAT_CHUNKS:2 * FEAT_CHUNKS, :]

    _pipelined_tokens(tt, ids_ref, tbl_ref, slots, compute)
    for c in range(FEAT_CHUNKS):
        cols = slice(c * LANES, (c + 1) * LANES)
        y = res_s[pl.ds(c, tt, stride=FEAT_CHUNKS), :]
        o_ref[:, cols] = x_ref[:, cols] + gate_ref[0][:, cols] * y


def _peer_v(ids_flat, coef, x2, gate, table, seq, tt):
    t, d = x2.shape
    per_b = seq // tt
    blk = pl.BlockSpec((tt, d), lambda i: (i, 0))
    return pl.pallas_call(
        _peer_v_kernel,
        grid=(t // tt,),
        in_specs=[
            pl.BlockSpec((tt * PEER_SLOTS,), lambda i: (i,), memory_space=pltpu.SMEM),
            pl.BlockSpec((tt, PEER_SLOTS), lambda i: (i, 0)),
            blk,
            pl.BlockSpec((1, 1, d), lambda i: (i // per_b, 0, 0)),
            _table_spec(table.shape[0]),
        ],
        out_specs=blk,
        out_shape=jax.ShapeDtypeStruct((t, d), F32),
        scratch_shapes=_slot_scratch() + [
                        pltpu.VMEM((tt // SUBLANES, SUBLANES, SLOT_WIDTH), F32),
                        pltpu.VMEM((tt // SUBLANES, SUBLANES, SLOT_WIDTH), F32),
                        pltpu.VMEM((tt * FEAT_CHUNKS, LANES), F32)],
        compiler_params=_cparams(1, TABLE_VMEM_LIMIT),
        name="peer_expert_out",
    )(ids_flat, coef, x2, gate, table)


def _peer_ffn(x2, g, shift, scale, gate, wq, sub_keys, table_u, table_v, seq):
    t, d = x2.shape
    tt = PEER_TOKEN_TILE
    h, ids, gates = _peer_route(x2, g, shift, scale, wq.T.astype(BF16), sub_keys, seq)
    ids_flat = ids
    coef = _peer_u(ids_flat, h, gates, table_u, tt)
    return _peer_v(ids_flat, coef, x2, gate, table_v, seq, tt)


def _rope_tables(seq):
    half = HEAD_DIM // 2
    inv_freq = ROPE_THETA ** (-jnp.arange(half, dtype=F32) / half)
    ang = jnp.arange(seq).astype(F32)[:, None] * inv_freq[None, :]
    reps = LANES // half
    return jnp.tile(jnp.cos(ang), (1, reps)), jnp.tile(jnp.sin(ang), (1, reps))


def _two_heads(gain):
    return jnp.tile(gain.reshape(1, HEAD_DIM), (1, LANES // HEAD_DIM))


def kernel(x, c, ada_w, ada_b, norm_mix_g, norm_ffn_g, w_in_ab, w_out_ab, sinks_a, qnorm_a, knorm_a,
           w_in_cd, w_out_cd, qnorm_c, knorm_c, qnorm_d, knorm_d, peer_wq, peer_subkeys, peer_u, peer_v):
    b, seq, d = x.shape
    depth = ada_w.shape[0]
    t = b * seq
    cos, sin = _rope_tables(seq)
    mod = _modulation(c, ada_w, ada_b)
    x2 = x.reshape(t, d)
    for layer in range(depth):
        shift_m, scale_m, gate_m, shift_f, scale_f, gate_f = [
            m.reshape(b, 1, d) for m in jnp.split(mod[layer], 6, axis=-1)]
        g_mix = norm_mix_g[layer].reshape(1, d)
        i = layer // 2
        if layer % 2 == 0:
            proj = _norm_proj(x2, g_mix, shift_m, scale_m, w_in_ab[i].astype(BF16), seq)
            proj = proj.reshape(b, seq, -1)
            ya = _swa_attention(proj, sinks_a[i], cos, sin, _two_heads(qnorm_a[i]), _two_heads(knorm_a[i]))
            b_col = (A_Q_HEADS + 2 * A_KV_HEADS) * HEAD_DIM // LANES
            yb = _stick_attention(proj, b_col)
            w_out = w_out_ab[i]
        else:
            proj = _norm_proj(x2, g_mix, shift_m, scale_m, w_in_cd[i].astype(BF16), seq)
            proj = proj.reshape(b, seq, -1)
            ya = _qkv_attention_call(_dilated_kernel, "dilated_attention", proj, 0, C_HEADS, cos, sin,
                                     _two_heads(qnorm_c[i]), _two_heads(knorm_c[i]))
            d_col = 3 * C_HEADS * HEAD_DIM // LANES
            yb = _qkv_attention_call(_moba_kernel, "moba_attention", proj, d_col, D_HEADS, cos, sin,
                                     _two_heads(qnorm_d[i]), _two_heads(knorm_d[i]),
                                     extra_scratch=(pltpu.VMEM((LANES, LANES), F32),
                                                    pltpu.VMEM((seq, LANES), F32),
                                                    pltpu.VMEM((seq, LANES), F32)))
            w_out = w_out_cd[i]
        x2 = _out_proj(x2, ya.reshape(t, -1), yb.reshape(t, -1), w_out.astype(BF16), gate_m, seq)
        x2 = _peer_ffn(x2, norm_ffn_g[layer].reshape(1, d), shift_f, scale_f, gate_f,
                       peer_wq[layer], peer_subkeys[layer],
                       _pack_table(peer_u[layer]), _pack_table(peer_v[layer]), seq)
    return x2.reshape(b, seq, d)
```

```python
import jax
import jax.numpy as jnp
from jax import lax
from jax.experimental import pallas as pl
from jax.experimental.pallas import tpu as pltpu

F32 = jnp.float32
BF16 = jnp.bfloat16
I32 = jnp.int32
U32 = jnp.uint32

HEAD_DIM = 64
ROPE_THETA = 10000.0
NORM_EPS = 1e-6
LANES = 128
QUERY_BLOCK = 128
A_Q_HEADS, A_KV_HEADS = 8, 2
B_HEADS = C_HEADS = D_HEADS = 8
C_PATTERNS = ((128, 1), (512, 4), (2048, 16))
MOBA_BLOCK, MOBA_TOPK = 256, 3
PEER_HEADS, PEER_N_KEYS, PEER_TOPK, PEER_D_KEY = 8, 128, 16, 256
PEER_SLOTS = PEER_HEADS * PEER_TOPK
NEG_BIG = -1e30
PEER_CAND_COUNTS = tuple(PEER_TOPK // (a + 1) for a in range(PEER_TOPK))
PACK_ROWS = 4
MIB = 1024 * 1024
V7X_VMEM_BYTES = 64 * MIB
STREAM_VMEM_LIMIT = 40 * MIB
TABLE_VMEM_LIMIT = V7X_VMEM_BYTES - 8 * MIB


def _cparams(n_axes, vmem_bytes=STREAM_VMEM_LIMIT):
    return pltpu.CompilerParams(
        dimension_semantics=("arbitrary",) * n_axes,
        vmem_limit_bytes=vmem_bytes)


def _split_bf16(a):
    hi = a.astype(BF16)
    lo = (a - hi.astype(F32)).astype(BF16)
    return hi, lo


def _dot(a, b):
    return jnp.dot(a, b, preferred_element_type=F32)


def _dot_nt(a, b):
    return lax.dot_general(a, b, (((1,), (1,)), ((), ())), preferred_element_type=F32)


def _dot3(a, b):
    ah, al = _split_bf16(a)
    bh, bl = _split_bf16(b)
    return _dot(ah, bh) + _dot(ah, bl) + _dot(al, bh)


def _dot3_nt(a, b):
    ah, al = _split_bf16(a)
    bh, bl = _split_bf16(b)
    return _dot_nt(ah, bh) + _dot_nt(ah, bl) + _dot_nt(al, bh)


def _iota(shape, dim):
    return lax.broadcasted_iota(I32, shape, dim)


def _mod_kernel(c_ref, w_ref, b_ref, o_ref):
    c = c_ref[...]
    cond = c * jax.nn.sigmoid(c)
    o_ref[0] = _dot3(cond, w_ref[0]) + b_ref[0]


def _modulation(c, ada_w, ada_b):
    depth, d, n = ada_w.shape
    b = c.shape[0]
    tn = 1024
    return pl.pallas_call(
        _mod_kernel,
        grid=(depth, n // tn),
        in_specs=[
            pl.BlockSpec((b, d), lambda l, j: (0, 0)),
            pl.BlockSpec((1, d, tn), lambda l, j: (l, 0, j)),
            pl.BlockSpec((1, 1, tn), lambda l, j: (l, 0, j)),
        ],
        out_specs=pl.BlockSpec((1, b, tn), lambda l, j: (l, 0, j)),
        out_shape=jax.ShapeDtypeStruct((depth, b, n), F32),
        compiler_params=_cparams(2),
        name="adaln_modulation",
    )(c, ada_w, ada_b.reshape(depth, 1, n))


def _adaln(x, g, shift, scale):
    ms = jnp.mean(x * x, axis=-1, keepdims=True)
    y = x * lax.rsqrt(ms + NORM_EPS) * g
    return y * (1.0 + scale) + shift


def _norm_proj_kernel(x_ref, g_ref, sh_ref, sc_ref, w_ref, o_ref):
    h = _adaln(x_ref[...], g_ref[...], sh_ref[0], sc_ref[0])
    o_ref[...] = _dot(h.astype(BF16), w_ref[...])


def _norm_proj(x2, g, shift, scale, w_bf, seq):
    t, d = x2.shape
    n = w_bf.shape[1]
    tt = 512
    per_b = seq // tt
    return pl.pallas_call(
        _norm_proj_kernel,
        grid=(t // tt,),
        in_specs=[
            pl.BlockSpec((tt, d), lambda i: (i, 0)),
            pl.BlockSpec((1, d), lambda i: (0, 0)),
            pl.BlockSpec((1, 1, d), lambda i: (i // per_b, 0, 0)),
            pl.BlockSpec((1, 1, d), lambda i: (i // per_b, 0, 0)),
            pl.BlockSpec((d, n), lambda i: (0, 0)),
        ],
        out_specs=pl.BlockSpec((tt, n), lambda i: (i, 0)),
        out_shape=jax.ShapeDtypeStruct((t, n), F32),
        compiler_params=_cparams(1),
        name="adaln_in_proj",
    )(x2, g, shift, scale, w_bf)


def _out_proj_kernel(x_ref, ya_ref, yb_ref, w_ref, gate_ref, o_ref):
    half = ya_ref.shape[1]
    y = _dot(ya_ref[...].astype(BF16), w_ref[0:half, :])
    y = y + _dot(yb_ref[...].astype(BF16), w_ref[half:2 * half, :])
    o_ref[...] = x_ref[...] + gate_ref[0] * y


def _out_proj(x2, ya, yb, w_bf, gate, seq):
    t, d = x2.shape
    half = ya.shape[1]
    tt = 512
    per_b = seq // tt
    return pl.pallas_call(
        _out_proj_kernel,
        grid=(t // tt,),
        in_specs=[
            pl.BlockSpec((tt, d), lambda i: (i, 0)),
            pl.BlockSpec((tt, half), lambda i: (i, 0)),
            pl.BlockSpec((tt, half), lambda i: (i, 0)),
            pl.BlockSpec((2 * half, d), lambda i: (0, 0)),
            pl.BlockSpec((1, 1, d), lambda i: (i // per_b, 0, 0)),
        ],
        out_specs=pl.BlockSpec((tt, d), lambda i: (i, 0)),
        out_shape=jax.ShapeDtypeStruct((t, d), F32),
        compiler_params=_cparams(1),
        name="mixer_out_proj",
    )(x2, ya, yb, w_bf, gate)


def _lane_row():
    return _iota((1, LANES), 1)


def _head_segment_ones():
    r = _iota((LANES, LANES), 0) // HEAD_DIM
    c = _iota((LANES, LANES), 1) // HEAD_DIM
    return jnp.where(r == c, 1.0, 0.0).astype(BF16)


def _headnorm_rope(a, g, cos, sin):
    hi, lo = _split_bf16(a * a)
    seg = _head_segment_ones()
    ms = (_dot(hi, seg) + _dot(lo, seg)) * (1.0 / HEAD_DIM)
    y = a * lax.rsqrt(ms + NORM_EPS) * g
    half = HEAD_DIM // 2
    upper = pltpu.roll(y, LANES - half, axis=1)
    lower = pltpu.roll(y, half, axis=1)
    first_half = (_lane_row() % HEAD_DIM) < half
    rot = jnp.where(first_half, -upper, lower)
    return y * cos + rot * sin


def _head_masks():
    lane = _lane_row()
    return lane < HEAD_DIM, lane >= HEAD_DIM


def _attn_specs(seq, qcol, kcol, vcol, kv_shared):
    blk = (1, seq, LANES)
    q_spec = pl.BlockSpec(blk, lambda b, p: (b, 0, qcol + p))
    if kv_shared:
        k_spec = pl.BlockSpec(blk, lambda b, p: (b, 0, kcol))
        v_spec = pl.BlockSpec(blk, lambda b, p: (b, 0, vcol))
    else:
        k_spec = pl.BlockSpec(blk, lambda b, p: (b, 0, kcol + p))
        v_spec = pl.BlockSpec(blk, lambda b, p: (b, 0, vcol + p))
    return q_spec, k_spec, v_spec


def _row_spec(seq):
    return pl.BlockSpec((seq, LANES), lambda b, p: (0, 0))


def _gain_spec():
    return pl.BlockSpec((1, LANES), lambda b, p: (0, 0))


def _store_heads(o_ref, r0, outs):
    first, _ = _head_masks()
    o_ref[0, pl.ds(r0, QUERY_BLOCK), :] = jnp.where(first, outs[0], outs[1])


SWA_QBLOCKS_PER_STEP = 4


def _swa_kernel(sinks_ref, q_ref, k_ref, v_ref, cos_ref, sin_ref, gq_ref, gk_ref, o_ref,
                q0_s, q1_s, k_s, v_s):
    p = pl.program_id(1)
    seq = q_ref.shape[1]
    cos, sin = cos_ref[...], sin_ref[...]
    first, second = _head_masks()
    qn = _headnorm_rope(q_ref[0], gq_ref[...], cos, sin) * (HEAD_DIM ** -0.5)
    q0_s[...] = jnp.where(first, qn, 0.0).astype(BF16)
    q1_s[...] = jnp.where(second, qn, 0.0).astype(BF16)
    pairs_per_kv = (A_Q_HEADS // A_KV_HEADS) // 2
    keep = jnp.logical_xor(first, (p // pairs_per_kv) == 1)
    kn = _headnorm_rope(k_ref[0], gk_ref[...], cos, sin)
    k_s[...] = jnp.where(keep, kn, pltpu.roll(kn, HEAD_DIM, axis=1)).astype(BF16)
    v = v_ref[0]
    v_s[...] = jnp.where(keep, v, pltpu.roll(v, HEAD_DIM, axis=1)).astype(BF16)

    qi = _iota((QUERY_BLOCK, QUERY_BLOCK), 0)
    ki = _iota((QUERY_BLOCK, QUERY_BLOCK), 1)

    mask_c = ki <= qi
    per_step = min(SWA_QBLOCKS_PER_STEP, seq // QUERY_BLOCK)

    def qstep(step, carry):
        chains, scores = [], []
        for u in range(per_step):
            i = step * per_step + u
            r0 = pl.multiple_of(i * QUERY_BLOCK, QUERY_BLOCK)
            rp = pl.multiple_of(jnp.maximum(i - 1, 0) * QUERY_BLOCK, QUERY_BLOCK)
            kc, kp = k_s[pl.ds(r0, QUERY_BLOCK), :], k_s[pl.ds(rp, QUERY_BLOCK), :]
            vc, vp = v_s[pl.ds(r0, QUERY_BLOCK), :], v_s[pl.ds(rp, QUERY_BLOCK), :]
            mask_p = jnp.logical_and(ki > qi, i > 0)
            chains.append((r0, vc, vp, mask_p))
            for q_s in (q0_s, q1_s):
                qh = q_s[pl.ds(r0, QUERY_BLOCK), :]
                scores.append((_dot_nt(qh, kc), _dot_nt(qh, kp)))
        probs, denoms = [], []
        for n, (s_cur, s_prev) in enumerate(scores):
            mask_p = chains[n // 2][3]
            sc = jnp.where(mask_c, s_cur, NEG_BIG)
            sp = jnp.where(mask_p, s_prev, NEG_BIG)
            sink = sinks_ref[2 * p + n % 2]
            m = jnp.maximum(jnp.max(sc, axis=1, keepdims=True), jnp.max(sp, axis=1, keepdims=True))
            m = jnp.maximum(m, sink)
            ec, ep = jnp.exp(sc - m), jnp.exp(sp - m)
            denoms.append(jnp.sum(ec, axis=1, keepdims=True) + jnp.sum(ep, axis=1, keepdims=True)
                          + jnp.exp(sink - m))
            probs.append((ec.astype(BF16), ep.astype(BF16)))
        for u, (r0, vc, vp, _) in enumerate(chains):
            outs = [(_dot(probs[2 * u + hh][0], vc) + _dot(probs[2 * u + hh][1], vp)) / denoms[2 * u + hh]
                    for hh in range(2)]
            _store_heads(o_ref, r0, outs)
        return carry

    lax.fori_loop(0, seq // (QUERY_BLOCK * per_step), qstep, 0)


def _swa_attention(proj, sinks, cos, sin, gq, gk):
    b, seq, _ = proj.shape
    n_pairs = A_Q_HEADS // 2
    kcol = A_Q_HEADS * HEAD_DIM // LANES
    vcol = kcol + A_KV_HEADS * HEAD_DIM // LANES
    q_spec, k_spec, v_spec = _attn_specs(seq, 0, kcol, vcol, True)
    return pl.pallas_call(
        _swa_kernel,
        grid=(b, n_pairs),
        in_specs=[pl.BlockSpec(memory_space=pltpu.SMEM), q_spec, k_spec, v_spec,
                  _row_spec(seq), _row_spec(seq), _gain_spec(), _gain_spec()],
        out_specs=pl.BlockSpec((1, seq, LANES), lambda b_, p: (b_, 0, p)),
        out_shape=jax.ShapeDtypeStruct((b, seq, n_pairs * LANES), F32),
        scratch_shapes=[pltpu.VMEM((seq, LANES), BF16)] * 4,
        compiler_params=_cparams(2),
        name="swa_gqa_attention",
    )(sinks, proj, proj, proj, cos, sin, gq, gk)


STICK_GROUP = 4
STICK_QUERY_ROWS = 256


def _stick_kernel(q_ref, k_ref, v_ref, o_ref, k_s, v_s):
    seq = q_ref.shape[1]
    k_s[...] = k_ref[0].astype(BF16)
    v_s[...] = v_ref[0].astype(BF16)
    first, second = _head_masks()
    qr = min(STICK_QUERY_ROWS, seq)
    kw = QUERY_BLOCK
    key_minus_query = _iota((qr, kw), 1) - _iota((qr, kw), 0)
    wr = _iota((2 * kw, 2 * kw), 0) % kw
    wc = _iota((2 * kw, 2 * kw), 1)
    suffix_w = jnp.where(jnp.logical_or(wc >= kw, wr > wc), 1.0, 0.0).astype(BF16)

    def qblock(i, carry):
        r0 = pl.multiple_of(i * qr, qr)
        q = q_ref[0, pl.ds(r0, qr), :] * (HEAD_DIM ** -0.5)
        qhs = [jnp.where(msk, q, 0.0).astype(BF16) for msk in (first, second)]
        n_blocks = (r0 + qr) // kw

        def kgroup(g, st):
            accs, laters = [st[0], st[1]], [st[2], st[3]]
            chains = [(u, hh) for u in range(STICK_GROUP) for hh in range(2)]
            vbs, pasts, zs = [], [], {}
            for u in range(STICK_GROUP):
                j = n_blocks - 1 - (g * STICK_GROUP + u)
                live = j >= 0
                c0 = pl.multiple_of(jnp.maximum(j, 0) * kw, kw)
                kb = k_s[pl.ds(c0, kw), :]
                vbs.append(v_s[pl.ds(c0, kw), :])
                pasts.append(jnp.logical_and(key_minus_query < r0 - c0, live))
                for hh in range(2):
                    zs[u, hh] = _dot_nt(qhs[hh], kb)
            logit, sums = {}, {}
            for u, hh in chains:
                z = zs[u, hh]
                sp = jnp.maximum(z, 0.0) + jnp.log(1.0 + jnp.exp(-jnp.abs(z)))
                neg_log_keep = jnp.where(pasts[u], sp, 0.0)
                logit[u, hh] = z - sp
                hi, lo = _split_bf16(neg_log_keep)
                sums[u, hh] = _dot(jnp.concatenate([hi, lo], axis=1), suffix_w)
            ws = {}
            for u, hh in chains:
                inner, total = sums[u, hh][:, :kw], sums[u, hh][:, kw:]
                ws[u, hh] = jnp.where(pasts[u], jnp.exp(logit[u, hh] - inner - laters[hh]), 0.0).astype(BF16)
                laters[hh] = laters[hh] + total
            for u, hh in chains:
                accs[hh] = accs[hh] + _dot(ws[u, hh], vbs[u])
            return accs[0], accs[1], laters[0], laters[1]

        zero = jnp.zeros((qr, LANES), F32)
        n_groups = (n_blocks + STICK_GROUP - 1) // STICK_GROUP
        st = lax.fori_loop(0, n_groups, kgroup, (zero, zero, zero, zero))
        o_ref[0, pl.ds(r0, qr), :] = jnp.where(first, st[0], st[1])
        return carry

    lax.fori_loop(0, seq // qr, qblock, 0)


def _stick_attention(proj, qcol):
    b, seq, _ = proj.shape
    n_pairs = B_HEADS // 2
    q_spec, k_spec, v_spec = _attn_specs(seq, qcol, qcol + n_pairs, qcol + 2 * n_pairs, False)
    return pl.pallas_call(
        _stick_kernel,
        grid=(b, n_pairs),
        in_specs=[q_spec, k_spec, v_spec],
        out_specs=pl.BlockSpec((1, seq, LANES), lambda b_, p: (b_, 0, p)),
        out_shape=jax.ShapeDtypeStruct((b, seq, n_pairs * LANES), F32),
        scratch_shapes=[pltpu.VMEM((seq, LANES), BF16)] * 2,
        compiler_params=_cparams(2),
        name="stick_breaking_attention",
    )(proj, proj, proj)


def _prep_qkv(q_ref, k_ref, v_ref, cos_ref, sin_ref, gq_ref, gk_ref, q0_s, q1_s, k_s, v_s):
    cos, sin = cos_ref[...], sin_ref[...]
    first, second = _head_masks()
    qn = _headnorm_rope(q_ref[0], gq_ref[...], cos, sin) * (HEAD_DIM ** -0.5)
    q0_s[...] = jnp.where(first, qn, 0.0).astype(BF16)
    q1_s[...] = jnp.where(second, qn, 0.0).astype(BF16)
    kn = _headnorm_rope(k_ref[0], gk_ref[...], cos, sin)
    k_s[...] = kn.astype(BF16)
    v_s[...] = v_ref[0].astype(BF16)
    return qn, kn


DILATED_KEY_TILE = 512
DILATED_QUERY_ROWS = 256


def _dilated_kernel(q_ref, k_ref, v_ref, cos_ref, sin_ref, gq_ref, gk_ref, o_ref,
                    q0_s, q1_s, k_s, v_s):
    seq = q_ref.shape[1]
    _prep_qkv(q_ref, k_ref, v_ref, cos_ref, sin_ref, gq_ref, gk_ref, q0_s, q1_s, k_s, v_s)
    kt = min(DILATED_KEY_TILE, seq)
    qr = min(DILATED_QUERY_ROWS, seq)
    qk = _iota((qr, kt), 0) - _iota((qr, kt), 1)
    on_stride = [jnp.where((qk & (dil - 1)) == 0, 1.0, 0.0) for _, dil in C_PATTERNS]
    first_head, _ = _head_masks()

    def qblock(i, carry):
        r0 = pl.multiple_of(i * qr, qr)
        qhs = [q_s[pl.ds(r0, qr), :] for q_s in (q0_s, q1_s)]

        def ktile(g, st):
            c0 = pl.multiple_of(g * kt, kt)
            d = (r0 - c0) + qk
            count = jnp.zeros(d.shape, F32)
            for (window, _), stride_ok in zip(C_PATTERNS, on_stride):
                count = count + jnp.where(d <= window, stride_ok, 0.0)
            count = jnp.where(d >= 0, count, 0.0)
            kb, vb = k_s[pl.ds(c0, kt), :], v_s[pl.ds(c0, kt), :]
            scores = [_dot_nt(qhs[hh], kb) for hh in range(2)]
            new, prs = [], []
            for hh in range(2):
                m, l, acc = st[3 * hh:3 * hh + 3]
                s = jnp.where(count > 0.0, scores[hh], NEG_BIG)
                m_new = jnp.maximum(m, jnp.max(s, axis=1, keepdims=True))
                pr = count * jnp.exp(s - m_new)
                alpha = jnp.exp(m - m_new)
                new += [m_new, alpha * l + jnp.sum(pr, axis=1, keepdims=True), alpha * acc]
                prs.append(pr.astype(BF16))
            for hh in range(2):
                new[3 * hh + 2] = new[3 * hh + 2] + _dot(prs[hh], vb)
            return tuple(new)

        init = (jnp.full((qr, 1), NEG_BIG, F32), jnp.zeros((qr, 1), F32),
                jnp.zeros((qr, LANES), F32)) * 2
        st = lax.fori_loop(0, (r0 + qr + kt - 1) // kt, ktile, init)
        o_ref[0, pl.ds(r0, qr), :] = jnp.where(first_head, st[2] / st[1], st[5] / st[4])
        return carry

    lax.fori_loop(0, seq // qr, qblock, 0)


def _qkv_attention_call(kernel, name, proj, qcol, n_heads, cos, sin, gq, gk, extra_scratch=()):
    b, seq, _ = proj.shape
    n_pairs = n_heads // 2
    q_spec, k_spec, v_spec = _attn_specs(seq, qcol, qcol + n_pairs, qcol + 2 * n_pairs, False)
    return pl.pallas_call(
        kernel,
        grid=(b, n_pairs),
        in_specs=[q_spec, k_spec, v_spec, _row_spec(seq), _row_spec(seq), _gain_spec(), _gain_spec()],
        out_specs=pl.BlockSpec((1, seq, LANES), lambda b_, p: (b_, 0, p)),
        out_shape=jax.ShapeDtypeStruct((b, seq, n_pairs * LANES), F32),
        scratch_shapes=[pltpu.VMEM((seq, LANES), BF16)] * 4 + list(extra_scratch),
        compiler_params=_cparams(2),
        name=name,
    )(proj, proj, proj, cos, sin, gq, gk)


def _moba_kernel(q_ref, k_ref, v_ref, cos_ref, sin_ref, gq_ref, gk_ref, o_ref,
                 q0_s, q1_s, k_s, v_s, km_s, sel0_s, sel1_s):
    seq = q_ref.shape[1]
    n_blocks = seq // MOBA_BLOCK
    qn, kn = _prep_qkv(q_ref, k_ref, v_ref, cos_ref, sin_ref, gq_ref, gk_ref, q0_s, q1_s, k_s, v_s)
    km_s[...] = jnp.zeros(km_s.shape, F32)
    km_s[0:n_blocks, :] = jnp.mean(kn.reshape(n_blocks, MOBA_BLOCK, LANES), axis=1)
    first, second = _head_masks()

    rows8 = _iota((8, seq), 0)
    own8 = _iota((8, seq), 1) // MOBA_BLOCK
    valid = rows8 < own8
    for msk, sel_s in ((first, sel0_s), (second, sel1_s)):
        gate = _dot3_nt(km_s[...], jnp.where(msk, qn, 0.0))[0:8, :]
        gm = jnp.where(valid, gate, -jnp.inf)
        rank = jnp.zeros((8, seq), F32)
        for n2 in range(n_blocks):
            g2 = gm[n2:n2 + 1, :]
            beats = jnp.logical_or(g2 > gm, jnp.logical_and(g2 == gm, n2 < rows8))
            rank = rank + jnp.where(jnp.logical_and(beats, n2 < own8), 1.0, 0.0)
        sel = jnp.where(jnp.logical_and(valid, rank < float(MOBA_TOPK)), 1.0, 0.0)
        sel = jnp.concatenate([sel, jnp.zeros((LANES - 8, seq), F32)], axis=0)
        sel_s[...] = sel.T

    qrows = MOBA_BLOCK
    lane_sq = _iota((qrows, LANES), 1)
    causal = _iota((qrows, MOBA_BLOCK), 1) <= _iota((qrows, MOBA_BLOCK), 0)
    second_block = _iota((qrows, 2 * MOBA_BLOCK), 1) >= MOBA_BLOCK
    first_head, _ = _head_masks()

    def qblock(own, carry):
        r0 = pl.multiple_of(own * qrows, qrows)
        qhs = [q_s[pl.ds(r0, qrows), :] for q_s in (q0_s, q1_s)]
        sels = [sel_s[pl.ds(r0, qrows), :] for sel_s in (sel0_s, sel1_s)]
        kb, vb = k_s[pl.ds(r0, MOBA_BLOCK), :], v_s[pl.ds(r0, MOBA_BLOCK), :]
        scores = [_dot_nt(qhs[hh], kb) for hh in range(2)]
        init, prs = [], []
        for hh in range(2):
            s = jnp.where(causal, scores[hh], NEG_BIG)
            m = jnp.max(s, axis=1, keepdims=True)
            pr = jnp.exp(s - m)
            init += [m, jnp.sum(pr, axis=1, keepdims=True), None]
            prs.append(pr.astype(BF16))
        for hh in range(2):
            init[3 * hh + 2] = _dot(prs[hh], vb)

        def kpair(g, st):
            c0 = pl.multiple_of(g * 2 * MOBA_BLOCK, 2 * MOBA_BLOCK)
            kb2, vb2 = k_s[pl.ds(c0, 2 * MOBA_BLOCK), :], v_s[pl.ds(c0, 2 * MOBA_BLOCK), :]
            scores = [_dot_nt(qhs[hh], kb2) for hh in range(2)]
            new, prs = [], []
            for hh in range(2):
                m, l, acc = st[3 * hh:3 * hh + 3]
                sel_a = jnp.sum(jnp.where(lane_sq == 2 * g, sels[hh], 0.0), axis=1, keepdims=True)
                sel_b = jnp.sum(jnp.where(lane_sq == 2 * g + 1, sels[hh], 0.0), axis=1, keepdims=True)
                keep = jnp.where(second_block, sel_b, sel_a) > 0.0
                s = jnp.where(keep, scores[hh], NEG_BIG)
                m_new = jnp.maximum(m, jnp.max(s, axis=1, keepdims=True))
                pr = jnp.exp(s - m_new)
                alpha = jnp.exp(m - m_new)
                new += [m_new, alpha * l + jnp.sum(pr, axis=1, keepdims=True), alpha * acc]
                prs.append(pr.astype(BF16))
            for hh in range(2):
                new[3 * hh + 2] = new[3 * hh + 2] + _dot(prs[hh], vb2)
            return tuple(new)

        st = lax.fori_loop(0, (own + 1) // 2, kpair, tuple(init))
        o_ref[0, pl.ds(r0, qrows), :] = jnp.where(first_head, st[2] / st[1], st[5] / st[4])
        return carry

    lax.fori_loop(0, seq // qrows, qblock, 0)


ROUTE_CHUNKS_PER_STEP = 4


def _oddeven_merge_sort_pairs(n):
    pairs = []

    def merge(lo, hi, r):
        step = 2 * r
        if step < hi - lo:
            merge(lo, hi, step)
            merge(lo + r, hi, step)
            pairs.extend((i, i + r) for i in range(lo + r, hi - r, step))
        else:
            pairs.append((lo, lo + r))

    def sort(lo, hi):
        if hi - lo >= 1:
            mid = lo + (hi - lo) // 2
            sort(lo, mid)
            sort(mid + 1, hi)
            merge(lo, hi, 1)

    sort(0, n - 1)
    return pairs


SUBLANES = 8


def _top16_rows(scores, n_rows, vals_refs, idx_refs):
    n_slabs = n_rows // SUBLANES
    sub = _iota((SUBLANES, LANES), 0)
    vals = [[s[SUBLANES * v:SUBLANES * (v + 1), :] for v in range(n_slabs)] for s in scores]
    idxs = [[sub + SUBLANES * v for v in range(n_slabs)] for _ in scores]
    for i, j in _oddeven_merge_sort_pairs(n_slabs):
        for va, ia in zip(vals, idxs):
            a, b = va[i], va[j]
            a_first = jnp.logical_or(a > b, jnp.logical_and(a == b, ia[i] < ia[j]))
            va[i], va[j] = jnp.maximum(a, b), jnp.minimum(a, b)
            ia[i], ia[j] = jnp.where(a_first, ia[i], ia[j]), jnp.where(a_first, ia[j], ia[i])
    for it in range(PEER_TOPK):
        for k, (va, ia) in enumerate(zip(vals, idxs)):
            m = jnp.max(va[0], axis=0, keepdims=True)
            pick = jnp.min(jnp.where(va[0] == m, ia[0], n_rows), axis=0, keepdims=True)
            vals_refs[k][it:it + 1, :] = m
            idx_refs[k][it:it + 1, :] = pick
            win = ia[0] == pick
            depth = PEER_TOPK - 1 - it
            for d in range(min(depth, n_slabs - 1)):
                va[d] = jnp.where(win, va[d + 1], va[d])
                ia[d] = jnp.where(win, ia[d + 1], ia[d])
            if depth >= n_slabs:
                va[n_slabs - 1] = jnp.where(win, -jnp.inf, va[n_slabs - 1])


def _peer_route_kernel(x_ref, g_ref, sh_ref, sc_ref, wqt_ref, sk_ref, h_ref, ids_ref, gts_ref,
                       q_s, val_s, idx_s, ids_s, gts_s):
    tt = x_ref.shape[0]
    n_chunks = tt // LANES
    per_step = ROUTE_CHUNKS_PER_STEP
    half = PEER_D_KEY // 2
    h = _adaln(x_ref[...], g_ref[...], sh_ref[0], sc_ref[0])
    h_ref[...] = h
    qt = _dot_nt(wqt_ref[...], h.astype(BF16))
    for c in range(n_chunks):
        q_s[c] = qt[:, c * LANES:(c + 1) * LANES]
    sk1, sk2 = sk_ref[0], sk_ref[1]
    col_id = _iota((PEER_TOPK, LANES), 0)
    sub = _iota((SUBLANES, LANES), 0)
    col_depth = jnp.zeros((SUBLANES, LANES), I32)
    for a in range(SUBLANES):
        col_depth = jnp.where(sub == a, PEER_CAND_COUNTS[a], col_depth)
    vals = [val_s.at[i] for i in range(2 * per_step)]
    idxs = [idx_s.at[i] for i in range(2 * per_step)]

    def body(step, carry):
        hh = step // (n_chunks // per_step)
        c0 = (step % (n_chunks // per_step)) * per_step
        q0 = pl.multiple_of(hh * PEER_D_KEY, PEER_D_KEY)
        scores = []
        for k in range(per_step):
            scores.append(_dot3(sk1, q_s[c0 + k, pl.ds(q0, half), :]))
            scores.append(_dot3(sk2, q_s[c0 + k, pl.ds(q0 + half, half), :]))
        _top16_rows(scores, PEER_N_KEYS, vals, idxs)
        state = []
        for k in range(per_step):
            v1, i1, v2, i2 = vals[2 * k], idxs[2 * k], vals[2 * k + 1], idxs[2 * k + 1]
            v1x, e1x = v1[0:SUBLANES, :], i1[0:SUBLANES, :] * PEER_N_KEYS
            xs = [jnp.where(col_depth > b, v1x + v2[b:b + 1, :], -jnp.inf) for b in range(PEER_TOPK)]
            ex = [e1x + i2[b:b + 1, :] for b in range(PEER_TOPK)]
            y = v1[SUBLANES:PEER_TOPK, :] + v2[0:1, :]
            ey = i1[SUBLANES:PEER_TOPK, :] * PEER_N_KEYS + i2[0:1, :]
            state.append([xs, ex, y, ey])
        for it in range(PEER_TOPK):
            for k in range(per_step):
                xs, ex, y, ey = state[k]
                heads = jnp.concatenate([xs[0], y], axis=0)
                m = jnp.max(heads, axis=0, keepdims=True)
                pick = jnp.min(jnp.where(heads == m, col_id, PEER_TOPK), axis=0, keepdims=True)
                win = col_id == pick
                eids = jnp.where(win, jnp.concatenate([ex[0], ey], axis=0), 0)
                idxs[2 * k][it:it + 1, :] = jnp.sum(eids, axis=0, keepdims=True)
                vals[2 * k][it:it + 1, :] = m
                win_x, win_y = win[0:SUBLANES, :], win[SUBLANES:PEER_TOPK, :]
                for dd in range(PEER_TOPK - 1 - it):
                    xs[dd] = jnp.where(win_x, xs[dd + 1], xs[dd])
                    ex[dd] = jnp.where(win_x, ex[dd + 1], ex[dd])
                state[k][2] = jnp.where(win_y, -jnp.inf, y)
        r0 = pl.multiple_of(hh * PEER_TOPK, PEER_TOPK)
        for k in range(per_step):
            top = vals[2 * k][...]
            e = jnp.exp(top - top[0:1, :])
            gts_s[c0 + k, pl.ds(r0, PEER_TOPK), :] = e / jnp.sum(e, axis=0, keepdims=True)
            ids_s[c0 + k, pl.ds(r0, PEER_TOPK), :] = idxs[2 * k][...] * PACK_ROWS
        return carry

    lax.fori_loop(0, PEER_HEADS * n_chunks // per_step, body, 0)
    for c in range(n_chunks):
        rows = slice(c * LANES, (c + 1) * LANES)
        ids_ref[pl.ds(c * LANES * PEER_SLOTS, LANES * PEER_SLOTS)] = ids_s[c].T.reshape(LANES * PEER_SLOTS)
        gts_ref[rows, :] = gts_s[c].T


def _peer_route(x2, g, shift, scale, wqt_bf, sub_keys, seq):
    t, d = x2.shape
    tt = 512
    per_b = seq // tt
    n_chunks = tt // LANES
    nq = wqt_bf.shape[0]
    out_blk = pl.BlockSpec((tt, PEER_SLOTS), lambda i: (i, 0))
    return pl.pallas_call(
        _peer_route_kernel,
        grid=(t // tt,),
        in_specs=[
            pl.BlockSpec((tt, d), lambda i: (i, 0)),
            pl.BlockSpec((1, d), lambda i: (0, 0)),
            pl.BlockSpec((1, 1, d), lambda i: (i // per_b, 0, 0)),
            pl.BlockSpec((1, 1, d), lambda i: (i // per_b, 0, 0)),
            pl.BlockSpec((nq, d), lambda i: (0, 0)),
            pl.BlockSpec(sub_keys.shape, lambda i: (0, 0, 0)),
        ],
        out_specs=[pl.BlockSpec((tt, d), lambda i: (i, 0)),
                   pl.BlockSpec((tt * PEER_SLOTS,), lambda i: (i,)), out_blk],
        out_shape=[jax.ShapeDtypeStruct((t, d), F32),
                   jax.ShapeDtypeStruct((t * PEER_SLOTS,), I32),
                   jax.ShapeDtypeStruct((t, PEER_SLOTS), F32)],
        scratch_shapes=[pltpu.VMEM((n_chunks, nq, LANES), F32),
                        pltpu.VMEM((2 * ROUTE_CHUNKS_PER_STEP, PEER_TOPK, LANES), F32),
                        pltpu.VMEM((2 * ROUTE_CHUNKS_PER_STEP, PEER_TOPK, LANES), I32),
                        pltpu.VMEM((n_chunks, PEER_SLOTS, LANES), I32),
                        pltpu.VMEM((n_chunks, PEER_SLOTS, LANES), F32)],
        compiler_params=_cparams(1),
        name="peer_route",
    )(x2, g, shift, scale, wqt_bf, sub_keys)


def _chunk_row(c):
    return 2 * (c % PACK_ROWS) + c // PACK_ROWS


def _bf16_bits(a):
    return lax.bitcast_convert_type(a.astype(BF16).astype(F32), U32)


def _pack_kernel(t_ref, o_ref):
    n = t_ref.shape[0]
    half = t_ref.shape[1] // 2
    for s in range(PACK_ROWS):
        lo = _bf16_bits(t_ref[:, s * LANES:(s + 1) * LANES])
        hi = _bf16_bits(t_ref[:, half + s * LANES:half + (s + 1) * LANES])
        o_ref[pl.ds(s, n, stride=PACK_ROWS), :] = (lo >> 16) | hi


def _pack_table(tab):
    e, d = tab.shape
    assert d == 2 * PACK_ROWS * LANES
    be = 512
    return pl.pallas_call(
        _pack_kernel,
        grid=(e // be,),
        in_specs=[pl.BlockSpec((be, d), lambda i: (i, 0))],
        out_specs=pl.BlockSpec((be * PACK_ROWS, LANES), lambda i: (i, 0)),
        out_shape=jax.ShapeDtypeStruct((e * PACK_ROWS, LANES), U32),
        compiler_params=_cparams(1),
        name="pack_expert_table",
    )(tab)


def _table_spec(rows):
    return pl.BlockSpec((rows, LANES), lambda i: (0, 0), pipeline_mode=pl.Buffered(1))


def _gelu_exact(a):
    return 0.5 * a * (1.0 + lax.erf(a * (2.0 ** -0.5)))


FEAT_CHUNKS = 8
SLOT_WIDTH = PEER_SLOTS * FEAT_CHUNKS


def _gather_rows(ids_ref, base, tbl_ref, slot):
    for j in range(PEER_SLOTS):
        if j % ID_VIEW == 0:
            ids_part = ids_ref.at[pl.ds(base + j, ID_VIEW)]
        row0 = pl.multiple_of(ids_part[j % ID_VIEW], PACK_ROWS)
        slot[PACK_ROWS * j:PACK_ROWS * (j + 1), :] = tbl_ref[pl.ds(row0, PACK_ROWS), :]


def _pipelined_tokens(tt, ids_ref, tbl_ref, slots, compute):
    group = len(slots)
    n_steps = tt // group
    tiles = group // SUBLANES

    def gather(t, slot):
        _gather_rows(ids_ref, t * PEER_SLOTS, tbl_ref, slot)

    def compute_group(q):
        for k in range(group):
            compute(group * q + k, tiles * q + k // SUBLANES, k % SUBLANES, slots[k])

    for k in range(group):
        gather(k, slots[k])

    def step(q, carry):
        compute_group(q)
        for k in range(group):
            gather(group * (q + 1) + k, slots[k])
        return carry

    lax.fori_loop(0, n_steps - 1, step, 0)
    compute_group(n_steps - 1)


def _chunk_diag():
    return (_iota((FEAT_CHUNKS, SLOT_WIDTH), 1) % FEAT_CHUNKS) == _iota((FEAT_CHUNKS, SLOT_WIDTH), 0)


def _peer_u_kernel(ids_ref, h_ref, g_ref, tbl_ref, coef_ref, *scratch):
    slots, (hx_s, rs_s) = scratch[:N_SLOTS], scratch[N_SLOTS:]
    tt = h_ref.shape[0]
    for c in range(FEAT_CHUNKS):
        hx_s[pl.ds(_chunk_row(c), tt, stride=FEAT_CHUNKS), :] = h_ref[:, c * LANES:(c + 1) * LANES]
    diag = _chunk_diag()

    def compute(t, tile, sub, slot):
        rows = pltpu.bitcast(slot[...], BF16)
        x8 = hx_s[pl.ds(pl.multiple_of(t * FEAT_CHUNKS, FEAT_CHUNKS), FEAT_CHUNKS), :]
        part = _dot_nt(x8.astype(BF16), rows)
        rs_s[tile, sub:sub + 1, :] = jnp.sum(jnp.where(diag, part, 0.0), axis=0, keepdims=True)

    _pipelined_tokens(tt, ids_ref, tbl_ref, slots, compute)
    group = jnp.where(_iota((SLOT_WIDTH, PEER_SLOTS), 0) // FEAT_CHUNKS == _iota((SLOT_WIDTH, PEER_SLOTS), 1),
                      1.0, 0.0).astype(BF16)
    hi, lo = _split_bf16(rs_s[...].reshape(tt, SLOT_WIDTH))
    act = _dot(hi, group) + _dot(lo, group)
    coef_ref[...] = g_ref[...] * _gelu_exact(act)


N_SLOTS = 16
ID_VIEW = 16
PEER_TOKEN_TILE = 512


def _slot_scratch():
    return [pltpu.VMEM((PEER_SLOTS * PACK_ROWS, LANES), U32)] * N_SLOTS


def _peer_u(ids_flat, h, gates, table, tt):
    t, d = h.shape
    return pl.pallas_call(
        _peer_u_kernel,
        grid=(t // tt,),
        in_specs=[
            pl.BlockSpec((tt * PEER_SLOTS,), lambda i: (i,), memory_space=pltpu.SMEM),
            pl.BlockSpec((tt, d), lambda i: (i, 0)),
            pl.BlockSpec((tt, PEER_SLOTS), lambda i: (i, 0)),
            _table_spec(table.shape[0]),
        ],
        out_specs=pl.BlockSpec((tt, PEER_SLOTS), lambda i: (i, 0)),
        out_shape=jax.ShapeDtypeStruct((t, PEER_SLOTS), F32),
        scratch_shapes=_slot_scratch() + [
                        pltpu.VMEM((tt * FEAT_CHUNKS, LANES), F32),
                        pltpu.VMEM((tt // SUBLANES, SUBLANES, SLOT_WIDTH), F32)],
        compiler_params=_cparams(1, TABLE_VMEM_LIMIT),
        name="peer_expert_in",
    )(ids_flat, h, gates, table)


def _peer_v_kernel(ids_ref, coef_ref, x_ref, gate_ref, tbl_ref, o_ref, *scratch):
    slots, (ce_hi_s, ce_lo_s, res_s) = scratch[:N_SLOTS], scratch[N_SLOTS:]
    tt = x_ref.shape[0]
    spread = jnp.where(_iota((PEER_SLOTS, SLOT_WIDTH), 1) // FEAT_CHUNKS == _iota((PEER_SLOTS, SLOT_WIDTH), 0),
                       1.0, 0.0).astype(BF16)
    hi, lo = _split_bf16(coef_ref[...])
    ce_hi_s[...] = _dot(hi, spread).reshape(ce_hi_s.shape)
    ce_lo_s[...] = _dot(lo, spread).reshape(ce_lo_s.shape)
    diag = _chunk_diag()

    def compute(t, tile, sub, slot):
        rows = pltpu.bitcast(slot[...], BF16)
        a_hi = jnp.where(diag, ce_hi_s[tile, sub:sub + 1, :], 0.0)
        a_lo = jnp.where(diag, ce_lo_s[tile, sub:sub + 1, :], 0.0)
        both = _dot(jnp.concatenate([a_hi, a_lo], axis=0).astype(BF16), rows)
        r0 = pl.multiple_of(t * FEAT_CHUNKS, FEAT_CHUNKS)
        res_s[pl.ds(r0, FEAT_CHUNKS), :] = both[0:FEAT_CHUNKS, :] + both[FEAT_CHUNKS:2 * FEAT_CHUNKS, :]

    _pipelined_tokens(tt, ids_ref, tbl_ref, slots, compute)
    for c in range(FEAT_CHUNKS):
        cols = slice(c * LANES, (c + 1) * LANES)
        y = res_s[pl.ds(_chunk_row(c), tt, stride=FEAT_CHUNKS), :]
        o_ref[:, cols] = x_ref[:, cols] + gate_ref[0][:, cols] * y


def _peer_v(ids_flat, coef, x2, gate, table, seq, tt):
    t, d = x2.shape
    per_b = seq // tt
    blk = pl.BlockSpec((tt, d), lambda i: (i, 0))
    return pl.pallas_call(
        _peer_v_kernel,
        grid=(t // tt,),
        in_specs=[
            pl.BlockSpec((tt * PEER_SLOTS,), lambda i: (i,), memory_space=pltpu.SMEM),
            pl.BlockSpec((tt, PEER_SLOTS), lambda i: (i, 0)),
            blk,
            pl.BlockSpec((1, 1, d), lambda i: (i // per_b, 0, 0)),
            _table_spec(table.shape[0]),
        ],
        out_specs=blk,
        out_shape=jax.ShapeDtypeStruct((t, d), F32),
        scratch_shapes=_slot_scratch() + [
                        pltpu.VMEM((tt // SUBLANES, SUBLANES, SLOT_WIDTH), F32),
                        pltpu.VMEM((tt // SUBLANES, SUBLANES, SLOT_WIDTH), F32),
                        pltpu.VMEM((tt * FEAT_CHUNKS, LANES), F32)],
        compiler_params=_cparams(1, TABLE_VMEM_LIMIT),
        name="peer_expert_out",
    )(ids_flat, coef, x2, gate, table)


def _peer_ffn(x2, g, shift, scale, gate, wq, sub_keys, table_u, table_v, seq):
    t, d = x2.shape
    tt = PEER_TOKEN_TILE
    h, ids, gates = _peer_route(x2, g, shift, scale, wq.T.astype(BF16), sub_keys, seq)
    ids_flat = ids
    coef = _peer_u(ids_flat, h, gates, table_u, tt)
    return _peer_v(ids_flat, coef, x2, gate, table_v, seq, tt)


def _rope_tables(seq):
    half = HEAD_DIM // 2
    inv_freq = ROPE_THETA ** (-jnp.arange(half, dtype=F32) / half)
    ang = jnp.arange(seq).astype(F32)[:, None] * inv_freq[None, :]
    reps = LANES // half
    return jnp.tile(jnp.cos(ang), (1, reps)), jnp.tile(jnp.sin(ang), (1, reps))


def _two_heads(gain):
    return jnp.tile(gain.reshape(1, HEAD_DIM), (1, LANES // HEAD_DIM))


def kernel(x, c, ada_w, ada_b, norm_mix_g, norm_ffn_g, w_in_ab, w_out_ab, sinks_a, qnorm_a, knorm_a,
           w_in_cd, w_out_cd, qnorm_c, knorm_c, qnorm_d, knorm_d, peer_wq, peer_subkeys, peer_u, peer_v):
    b, seq, d = x.shape
    depth = ada_w.shape[0]
    t = b * seq
    cos, sin = _rope_tables(seq)
    mod = _modulation(c, ada_w, ada_b)
    x2 = x.reshape(t, d)
    for layer in range(depth):
        shift_m, scale_m, gate_m, shift_f, scale_f, gate_f = [
            m.reshape(b, 1, d) for m in jnp.split(mod[layer], 6, axis=-1)]
        g_mix = norm_mix_g[layer].reshape(1, d)
        i = layer // 2
        if layer % 2 == 0:
            proj = _norm_proj(x2, g_mix, shift_m, scale_m, w_in_ab[i].astype(BF16), seq)
            proj = proj.reshape(b, seq, -1)
            ya = _swa_attention(proj, sinks_a[i], cos, sin, _two_heads(qnorm_a[i]), _two_heads(knorm_a[i]))
            b_col = (A_Q_HEADS + 2 * A_KV_HEADS) * HEAD_DIM // LANES
            yb = _stick_attention(proj, b_col)
            w_out = w_out_ab[i]
        else:
            proj = _norm_proj(x2, g_mix, shift_m, scale_m, w_in_cd[i].astype(BF16), seq)
            proj = proj.reshape(b, seq, -1)
            ya = _qkv_attention_call(_dilated_kernel, "dilated_attention", proj, 0, C_HEADS, cos, sin,
                                     _two_heads(qnorm_c[i]), _two_heads(knorm_c[i]))
            d_col = 3 * C_HEADS * HEAD_DIM // LANES
            yb = _qkv_attention_call(_moba_kernel, "moba_attention", proj, d_col, D_HEADS, cos, sin,
                                     _two_heads(qnorm_d[i]), _two_heads(knorm_d[i]),
                                     extra_scratch=(pltpu.VMEM((LANES, LANES), F32),
                                                    pltpu.VMEM((seq, LANES), F32),
                                                    pltpu.VMEM((seq, LANES), F32)))
            w_out = w_out_cd[i]
        x2 = _out_proj(x2, ya.reshape(t, -1), yb.reshape(t, -1), w_out.astype(BF16), gate_m, seq)
        x2 = _peer_ffn(x2, norm_ffn_g[layer].reshape(1, d), shift_f, scale_f, gate_f,
                       peer_wq[layer], peer_subkeys[layer],
                       _pack_table(peer_u[layer]), _pack_table(peer_v[layer]), seq)
    return x2.reshape(b, seq, d)
```

```python
import jax
import jax.numpy as jnp
from jax import lax
from jax.experimental import pallas as pl
from jax.experimental.pallas import tpu as pltpu

F32 = jnp.float32
BF16 = jnp.bfloat16
I32 = jnp.int32
U32 = jnp.uint32

HEAD_DIM = 64
ROPE_THETA = 10000.0
NORM_EPS = 1e-6
LANES = 128
QUERY_BLOCK = 128
A_Q_HEADS, A_KV_HEADS = 8, 2
B_HEADS = C_HEADS = D_HEADS = 8
C_PATTERNS = ((128, 1), (512, 4), (2048, 16))
MOBA_BLOCK, MOBA_TOPK = 256, 3
PEER_HEADS, PEER_N_KEYS, PEER_TOPK, PEER_D_KEY = 8, 128, 16, 256
PEER_SLOTS = PEER_HEADS * PEER_TOPK
NEG_BIG = -1e30
PEER_CAND_COUNTS = tuple(PEER_TOPK // (a + 1) for a in range(PEER_TOPK))
PACK_ROWS = 4
MIB = 1024 * 1024
V7X_VMEM_BYTES = 64 * MIB
STREAM_VMEM_LIMIT = 40 * MIB
TABLE_VMEM_LIMIT = V7X_VMEM_BYTES - 8 * MIB


def _cparams(n_axes, vmem_bytes=STREAM_VMEM_LIMIT):
    return pltpu.CompilerParams(
        dimension_semantics=("arbitrary",) * n_axes,
        vmem_limit_bytes=vmem_bytes)


def _split_bf16(a):
    hi = a.astype(BF16)
    lo = (a - hi.astype(F32)).astype(BF16)
    return hi, lo


def _dot(a, b):
    return jnp.dot(a, b, preferred_element_type=F32)


def _dot_nt(a, b):
    return lax.dot_general(a, b, (((1,), (1,)), ((), ())), preferred_element_type=F32)


def _dot3(a, b):
    ah, al = _split_bf16(a)
    bh, bl = _split_bf16(b)
    return _dot(ah, bh) + _dot(ah, bl) + _dot(al, bh)


def _dot3_nt(a, b):
    ah, al = _split_bf16(a)
    bh, bl = _split_bf16(b)
    return _dot_nt(ah, bh) + _dot_nt(ah, bl) + _dot_nt(al, bh)


def _iota(shape, dim):
    return lax.broadcasted_iota(I32, shape, dim)


def _mod_kernel(c_ref, w_ref, b_ref, o_ref):
    c = c_ref[...]
    cond = c * jax.nn.sigmoid(c)
    o_ref[0] = _dot3(cond, w_ref[0]) + b_ref[0]


def _modulation(c, ada_w, ada_b):
    depth, d, n = ada_w.shape
    b = c.shape[0]
    tn = 1024
    return pl.pallas_call(
        _mod_kernel,
        grid=(depth, n // tn),
        in_specs=[
            pl.BlockSpec((b, d), lambda l, j: (0, 0)),
            pl.BlockSpec((1, d, tn), lambda l, j: (l, 0, j)),
            pl.BlockSpec((1, 1, tn), lambda l, j: (l, 0, j)),
        ],
        out_specs=pl.BlockSpec((1, b, tn), lambda l, j: (l, 0, j)),
        out_shape=jax.ShapeDtypeStruct((depth, b, n), F32),
        compiler_params=_cparams(2),
        name="adaln_modulation",
    )(c, ada_w, ada_b.reshape(depth, 1, n))


def _adaln(x, g, shift, scale):
    ms = jnp.mean(x * x, axis=-1, keepdims=True)
    y = x * lax.rsqrt(ms + NORM_EPS) * g
    return y * (1.0 + scale) + shift


def _norm_proj_kernel(x_ref, g_ref, sh_ref, sc_ref, w_ref, o_ref):
    h = _adaln(x_ref[...], g_ref[...], sh_ref[0], sc_ref[0])
    o_ref[...] = _dot(h.astype(BF16), w_ref[...])


def _norm_proj(x2, g, shift, scale, w_bf, seq):
    t, d = x2.shape
    n = w_bf.shape[1]
    tt = 512
    per_b = seq // tt
    return pl.pallas_call(
        _norm_proj_kernel,
        grid=(t // tt,),
        in_specs=[
            pl.BlockSpec((tt, d), lambda i: (i, 0)),
            pl.BlockSpec((1, d), lambda i: (0, 0)),
            pl.BlockSpec((1, 1, d), lambda i: (i // per_b, 0, 0)),
            pl.BlockSpec((1, 1, d), lambda i: (i // per_b, 0, 0)),
            pl.BlockSpec((d, n), lambda i: (0, 0)),
        ],
        out_specs=pl.BlockSpec((tt, n), lambda i: (i, 0)),
        out_shape=jax.ShapeDtypeStruct((t, n), F32),
        compiler_params=_cparams(1),
        name="adaln_in_proj",
    )(x2, g, shift, scale, w_bf)


def _out_proj_kernel(x_ref, ya_ref, yb_ref, w_ref, gate_ref, o_ref):
    half = ya_ref.shape[1]
    y = _dot(ya_ref[...].astype(BF16), w_ref[0:half, :])
    y = y + _dot(yb_ref[...].astype(BF16), w_ref[half:2 * half, :])
    o_ref[...] = x_ref[...] + gate_ref[0] * y


def _out_proj(x2, ya, yb, w_bf, gate, seq):
    t, d = x2.shape
    half = ya.shape[1]
    tt = 512
    per_b = seq // tt
    return pl.pallas_call(
        _out_proj_kernel,
        grid=(t // tt,),
        in_specs=[
            pl.BlockSpec((tt, d), lambda i: (i, 0)),
            pl.BlockSpec((tt, half), lambda i: (i, 0)),
            pl.BlockSpec((tt, half), lambda i: (i, 0)),
            pl.BlockSpec((2 * half, d), lambda i: (0, 0)),
            pl.BlockSpec((1, 1, d), lambda i: (i // per_b, 0, 0)),
        ],
        out_specs=pl.BlockSpec((tt, d), lambda i: (i, 0)),
        out_shape=jax.ShapeDtypeStruct((t, d), F32),
        compiler_params=_cparams(1),
        name="mixer_out_proj",
    )(x2, ya, yb, w_bf, gate)


def _lane_row():
    return _iota((1, LANES), 1)


def _head_segment_ones():
    r = _iota((LANES, LANES), 0) // HEAD_DIM
    c = _iota((LANES, LANES), 1) // HEAD_DIM
    return jnp.where(r == c, 1.0, 0.0).astype(BF16)


def _headnorm_rope(a, g, cos, sin):
    hi, lo = _split_bf16(a * a)
    seg = _head_segment_ones()
    ms = (_dot(hi, seg) + _dot(lo, seg)) * (1.0 / HEAD_DIM)
    y = a * lax.rsqrt(ms + NORM_EPS) * g
    half = HEAD_DIM // 2
    upper = pltpu.roll(y, LANES - half, axis=1)
    lower = pltpu.roll(y, half, axis=1)
    first_half = (_lane_row() % HEAD_DIM) < half
    rot = jnp.where(first_half, -upper, lower)
    return y * cos + rot * sin


def _head_masks():
    lane = _lane_row()
    return lane < HEAD_DIM, lane >= HEAD_DIM


def _attn_specs(seq, qcol, kcol, vcol, kv_shared):
    blk = (1, seq, LANES)
    q_spec = pl.BlockSpec(blk, lambda b, p: (b, 0, qcol + p))
    if kv_shared:
        k_spec = pl.BlockSpec(blk, lambda b, p: (b, 0, kcol))
        v_spec = pl.BlockSpec(blk, lambda b, p: (b, 0, vcol))
    else:
        k_spec = pl.BlockSpec(blk, lambda b, p: (b, 0, kcol + p))
        v_spec = pl.BlockSpec(blk, lambda b, p: (b, 0, vcol + p))
    return q_spec, k_spec, v_spec


def _row_spec(seq):
    return pl.BlockSpec((seq, LANES), lambda b, p: (0, 0))


def _gain_spec():
    return pl.BlockSpec((1, LANES), lambda b, p: (0, 0))


def _store_heads(o_ref, r0, outs):
    first, _ = _head_masks()
    o_ref[0, pl.ds(r0, QUERY_BLOCK), :] = jnp.where(first, outs[0], outs[1])


SWA_QBLOCKS_PER_STEP = 4


def _swa_kernel(sinks_ref, q_ref, k_ref, v_ref, cos_ref, sin_ref, gq_ref, gk_ref, o_ref,
                q0_s, q1_s, k_s, v_s):
    p = pl.program_id(1)
    seq = q_ref.shape[1]
    cos, sin = cos_ref[...], sin_ref[...]
    first, second = _head_masks()
    qn = _headnorm_rope(q_ref[0], gq_ref[...], cos, sin) * (HEAD_DIM ** -0.5)
    q0_s[...] = jnp.where(first, qn, 0.0).astype(BF16)
    q1_s[...] = jnp.where(second, qn, 0.0).astype(BF16)
    pairs_per_kv = (A_Q_HEADS // A_KV_HEADS) // 2
    keep = jnp.logical_xor(first, (p // pairs_per_kv) == 1)
    kn = _headnorm_rope(k_ref[0], gk_ref[...], cos, sin)
    k_s[...] = jnp.where(keep, kn, pltpu.roll(kn, HEAD_DIM, axis=1)).astype(BF16)
    v = v_ref[0]
    v_s[...] = jnp.where(keep, v, pltpu.roll(v, HEAD_DIM, axis=1)).astype(BF16)

    qi = _iota((QUERY_BLOCK, QUERY_BLOCK), 0)
    ki = _iota((QUERY_BLOCK, QUERY_BLOCK), 1)

    mask_c = ki <= qi
    per_step = min(SWA_QBLOCKS_PER_STEP, seq // QUERY_BLOCK)

    def qstep(step, carry):
        chains, scores = [], []
        for u in range(per_step):
            i = step * per_step + u
            r0 = pl.multiple_of(i * QUERY_BLOCK, QUERY_BLOCK)
            rp = pl.multiple_of(jnp.maximum(i - 1, 0) * QUERY_BLOCK, QUERY_BLOCK)
            kc, kp = k_s[pl.ds(r0, QUERY_BLOCK), :], k_s[pl.ds(rp, QUERY_BLOCK), :]
            vc, vp = v_s[pl.ds(r0, QUERY_BLOCK), :], v_s[pl.ds(rp, QUERY_BLOCK), :]
            mask_p = jnp.logical_and(ki > qi, i > 0)
            chains.append((r0, vc, vp, mask_p))
            for q_s in (q0_s, q1_s):
                qh = q_s[pl.ds(r0, QUERY_BLOCK), :]
                scores.append((_dot_nt(qh, kc), _dot_nt(qh, kp)))
        probs, denoms = [], []
        for n, (s_cur, s_prev) in enumerate(scores):
            mask_p = chains[n // 2][3]
            sc = jnp.where(mask_c, s_cur, NEG_BIG)
            sp = jnp.where(mask_p, s_prev, NEG_BIG)
            sink = sinks_ref[2 * p + n % 2]
            m = jnp.maximum(jnp.max(sc, axis=1, keepdims=True), jnp.max(sp, axis=1, keepdims=True))
            m = jnp.maximum(m, sink)
            ec, ep = jnp.exp(sc - m), jnp.exp(sp - m)
            denoms.append(jnp.sum(ec, axis=1, keepdims=True) + jnp.sum(ep, axis=1, keepdims=True)
                          + jnp.exp(sink - m))
            probs.append((ec.astype(BF16), ep.astype(BF16)))
        for u, (r0, vc, vp, _) in enumerate(chains):
            outs = [(_dot(probs[2 * u + hh][0], vc) + _dot(probs[2 * u + hh][1], vp)) / denoms[2 * u + hh]
                    for hh in range(2)]
            _store_heads(o_ref, r0, outs)
        return carry

    lax.fori_loop(0, seq // (QUERY_BLOCK * per_step), qstep, 0)


def _swa_attention(proj, sinks, cos, sin, gq, gk):
    b, seq, _ = proj.shape
    n_pairs = A_Q_HEADS // 2
    kcol = A_Q_HEADS * HEAD_DIM // LANES
    vcol = kcol + A_KV_HEADS * HEAD_DIM // LANES
    q_spec, k_spec, v_spec = _attn_specs(seq, 0, kcol, vcol, True)
    return pl.pallas_call(
        _swa_kernel,
        grid=(b, n_pairs),
        in_specs=[pl.BlockSpec(memory_space=pltpu.SMEM), q_spec, k_spec, v_spec,
                  _row_spec(seq), _row_spec(seq), _gain_spec(), _gain_spec()],
        out_specs=pl.BlockSpec((1, seq, LANES), lambda b_, p: (b_, 0, p)),
        out_shape=jax.ShapeDtypeStruct((b, seq, n_pairs * LANES), F32),
        scratch_shapes=[pltpu.VMEM((seq, LANES), BF16)] * 4,
        compiler_params=_cparams(2),
        name="swa_gqa_attention",
    )(sinks, proj, proj, proj, cos, sin, gq, gk)


STICK_GROUP = 4
STICK_QUERY_ROWS = 256


def _stick_kernel(q_ref, k_ref, v_ref, o_ref, k_s, v_s):
    seq = q_ref.shape[1]
    k_s[...] = k_ref[0].astype(BF16)
    v_s[...] = v_ref[0].astype(BF16)
    first, second = _head_masks()
    qr = min(STICK_QUERY_ROWS, seq)
    kw = QUERY_BLOCK
    key_minus_query = _iota((qr, kw), 1) - _iota((qr, kw), 0)
    wr = _iota((2 * kw, 2 * kw), 0) % kw
    wc = _iota((2 * kw, 2 * kw), 1)
    suffix_w = jnp.where(jnp.logical_or(wc >= kw, wr > wc), 1.0, 0.0).astype(BF16)

    def qblock(i, carry):
        r0 = pl.multiple_of(i * qr, qr)
        q = q_ref[0, pl.ds(r0, qr), :] * (HEAD_DIM ** -0.5)
        qhs = [jnp.where(msk, q, 0.0).astype(BF16) for msk in (first, second)]
        n_blocks = (r0 + qr) // kw

        def kgroup(g, st):
            accs, laters = [st[0], st[1]], [st[2], st[3]]
            chains = [(u, hh) for u in range(STICK_GROUP) for hh in range(2)]
            vbs, pasts, zs = [], [], {}
            for u in range(STICK_GROUP):
                j = n_blocks - 1 - (g * STICK_GROUP + u)
                live = j >= 0
                c0 = pl.multiple_of(jnp.maximum(j, 0) * kw, kw)
                kb = k_s[pl.ds(c0, kw), :]
                vbs.append(v_s[pl.ds(c0, kw), :])
                pasts.append(jnp.logical_and(key_minus_query < r0 - c0, live))
                for hh in range(2):
                    zs[u, hh] = _dot_nt(qhs[hh], kb)
            logit, sums = {}, {}
            for u, hh in chains:
                z = zs[u, hh]
                sp = jnp.maximum(z, 0.0) + jnp.log(1.0 + jnp.exp(-jnp.abs(z)))
                neg_log_keep = jnp.where(pasts[u], sp, 0.0)
                logit[u, hh] = z - sp
                hi, lo = _split_bf16(neg_log_keep)
                sums[u, hh] = _dot(jnp.concatenate([hi, lo], axis=1), suffix_w)
            ws = {}
            for u, hh in chains:
                inner, total = sums[u, hh][:, :kw], sums[u, hh][:, kw:]
                ws[u, hh] = jnp.where(pasts[u], jnp.exp(logit[u, hh] - inner - laters[hh]), 0.0).astype(BF16)
                laters[hh] = laters[hh] + total
            for u, hh in chains:
                accs[hh] = accs[hh] + _dot(ws[u, hh], vbs[u])
            return accs[0], accs[1], laters[0], laters[1]

        zero = jnp.zeros((qr, LANES), F32)
        n_groups = (n_blocks + STICK_GROUP - 1) // STICK_GROUP
        st = lax.fori_loop(0, n_groups, kgroup, (zero, zero, zero, zero))
        o_ref[0, pl.ds(r0, qr), :] = jnp.where(first, st[0], st[1])
        return carry

    lax.fori_loop(0, seq // qr, qblock, 0)


def _stick_attention(proj, qcol):
    b, seq, _ = proj.shape
    n_pairs = B_HEADS // 2
    q_spec, k_spec, v_spec = _attn_specs(seq, qcol, qcol + n_pairs, qcol + 2 * n_pairs, False)
    return pl.pallas_call(
        _stick_kernel,
        grid=(b, n_pairs),
        in_specs=[q_spec, k_spec, v_spec],
        out_specs=pl.BlockSpec((1, seq, LANES), lambda b_, p: (b_, 0, p)),
        out_shape=jax.ShapeDtypeStruct((b, seq, n_pairs * LANES), F32),
        scratch_shapes=[pltpu.VMEM((seq, LANES), BF16)] * 2,
        compiler_params=_cparams(2),
        name="stick_breaking_attention",
    )(proj, proj, proj)


def _prep_qkv(q_ref, k_ref, v_ref, cos_ref, sin_ref, gq_ref, gk_ref, q0_s, q1_s, k_s, v_s):
    cos, sin = cos_ref[...], sin_ref[...]
    first, second = _head_masks()
    qn = _headnorm_rope(q_ref[0], gq_ref[...], cos, sin) * (HEAD_DIM ** -0.5)
    q0_s[...] = jnp.where(first, qn, 0.0).astype(BF16)
    q1_s[...] = jnp.where(second, qn, 0.0).astype(BF16)
    kn = _headnorm_rope(k_ref[0], gk_ref[...], cos, sin)
    k_s[...] = kn.astype(BF16)
    v_s[...] = v_ref[0].astype(BF16)
    return qn, kn


DILATED_KEY_TILE = 512
DILATED_QUERY_ROWS = 256


def _dilated_kernel(q_ref, k_ref, v_ref, cos_ref, sin_ref, gq_ref, gk_ref, o_ref,
                    q0_s, q1_s, k_s, v_s):
    seq = q_ref.shape[1]
    _prep_qkv(q_ref, k_ref, v_ref, cos_ref, sin_ref, gq_ref, gk_ref, q0_s, q1_s, k_s, v_s)
    kt = min(DILATED_KEY_TILE, seq)
    qr = min(DILATED_QUERY_ROWS, seq)
    qk = _iota((qr, kt), 0) - _iota((qr, kt), 1)
    on_stride = [jnp.where((qk & (dil - 1)) == 0, 1.0, 0.0) for _, dil in C_PATTERNS]
    first_head, _ = _head_masks()

    def qblock(i, carry):
        r0 = pl.multiple_of(i * qr, qr)
        qhs = [q_s[pl.ds(r0, qr), :] for q_s in (q0_s, q1_s)]

        def ktile(g, st):
            c0 = pl.multiple_of(g * kt, kt)
            d = (r0 - c0) + qk
            count = jnp.zeros(d.shape, F32)
            for (window, _), stride_ok in zip(C_PATTERNS, on_stride):
                count = count + jnp.where(d <= window, stride_ok, 0.0)
            count = jnp.where(d >= 0, count, 0.0)
            kb, vb = k_s[pl.ds(c0, kt), :], v_s[pl.ds(c0, kt), :]
            scores = [_dot_nt(qhs[hh], kb) for hh in range(2)]
            new, prs = [], []
            for hh in range(2):
                m, l, acc = st[3 * hh:3 * hh + 3]
                s = jnp.where(count > 0.0, scores[hh], NEG_BIG)
                m_new = jnp.maximum(m, jnp.max(s, axis=1, keepdims=True))
                pr = count * jnp.exp(s - m_new)
                alpha = jnp.exp(m - m_new)
                new += [m_new, alpha * l + jnp.sum(pr, axis=1, keepdims=True), alpha * acc]
                prs.append(pr.astype(BF16))
            for hh in range(2):
                new[3 * hh + 2] = new[3 * hh + 2] + _dot(prs[hh], vb)
            return tuple(new)

        init = (jnp.full((qr, 1), NEG_BIG, F32), jnp.zeros((qr, 1), F32),
                jnp.zeros((qr, LANES), F32)) * 2
        st = lax.fori_loop(0, (r0 + qr + kt - 1) // kt, ktile, init)
        o_ref[0, pl.ds(r0, qr), :] = jnp.where(first_head, st[2] / st[1], st[5] / st[4])
        return carry

    lax.fori_loop(0, seq // qr, qblock, 0)


def _qkv_attention_call(kernel, name, proj, qcol, n_heads, cos, sin, gq, gk, extra_scratch=()):
    b, seq, _ = proj.shape
    n_pairs = n_heads // 2
    q_spec, k_spec, v_spec = _attn_specs(seq, qcol, qcol + n_pairs, qcol + 2 * n_pairs, False)
    return pl.pallas_call(
        kernel,
        grid=(b, n_pairs),
        in_specs=[q_spec, k_spec, v_spec, _row_spec(seq), _row_spec(seq), _gain_spec(), _gain_spec()],
        out_specs=pl.BlockSpec((1, seq, LANES), lambda b_, p: (b_, 0, p)),
        out_shape=jax.ShapeDtypeStruct((b, seq, n_pairs * LANES), F32),
        scratch_shapes=[pltpu.VMEM((seq, LANES), BF16)] * 4 + list(extra_scratch),
        compiler_params=_cparams(2),
        name=name,
    )(proj, proj, proj, cos, sin, gq, gk)


def _moba_kernel(q_ref, k_ref, v_ref, cos_ref, sin_ref, gq_ref, gk_ref, o_ref,
                 q0_s, q1_s, k_s, v_s, km_s, sel0_s, sel1_s):
    seq = q_ref.shape[1]
    n_blocks = seq // MOBA_BLOCK
    qn, kn = _prep_qkv(q_ref, k_ref, v_ref, cos_ref, sin_ref, gq_ref, gk_ref, q0_s, q1_s, k_s, v_s)
    km_s[...] = jnp.zeros(km_s.shape, F32)
    km_s[0:n_blocks, :] = jnp.mean(kn.reshape(n_blocks, MOBA_BLOCK, LANES), axis=1)
    first, second = _head_masks()

    rows8 = _iota((8, seq), 0)
    own8 = _iota((8, seq), 1) // MOBA_BLOCK
    valid = rows8 < own8
    for msk, sel_s in ((first, sel0_s), (second, sel1_s)):
        gate = _dot3_nt(km_s[...], jnp.where(msk, qn, 0.0))[0:8, :]
        gm = jnp.where(valid, gate, -jnp.inf)
        rank = jnp.zeros((8, seq), F32)
        for n2 in range(n_blocks):
            g2 = gm[n2:n2 + 1, :]
            beats = jnp.logical_or(g2 > gm, jnp.logical_and(g2 == gm, n2 < rows8))
            rank = rank + jnp.where(jnp.logical_and(beats, n2 < own8), 1.0, 0.0)
        sel = jnp.where(jnp.logical_and(valid, rank < float(MOBA_TOPK)), 1.0, 0.0)
        sel = jnp.concatenate([sel, jnp.zeros((LANES - 8, seq), F32)], axis=0)
        sel_s[...] = sel.T

    qrows = MOBA_BLOCK
    lane_sq = _iota((qrows, LANES), 1)
    causal = _iota((qrows, MOBA_BLOCK), 1) <= _iota((qrows, MOBA_BLOCK), 0)
    second_block = _iota((qrows, 2 * MOBA_BLOCK), 1) >= MOBA_BLOCK
    first_head, _ = _head_masks()

    def qblock(own, carry):
        r0 = pl.multiple_of(own * qrows, qrows)
        qhs = [q_s[pl.ds(r0, qrows), :] for q_s in (q0_s, q1_s)]
        sels = [sel_s[pl.ds(r0, qrows), :] for sel_s in (sel0_s, sel1_s)]
        kb, vb = k_s[pl.ds(r0, MOBA_BLOCK), :], v_s[pl.ds(r0, MOBA_BLOCK), :]
        scores = [_dot_nt(qhs[hh], kb) for hh in range(2)]
        init, prs = [], []
        for hh in range(2):
            s = jnp.where(causal, scores[hh], NEG_BIG)
            m = jnp.max(s, axis=1, keepdims=True)
            pr = jnp.exp(s - m)
            init += [m, jnp.sum(pr, axis=1, keepdims=True), None]
            prs.append(pr.astype(BF16))
        for hh in range(2):
            init[3 * hh + 2] = _dot(prs[hh], vb)

        def kpair(g, st):
            c0 = pl.multiple_of(g * 2 * MOBA_BLOCK, 2 * MOBA_BLOCK)
            kb2, vb2 = k_s[pl.ds(c0, 2 * MOBA_BLOCK), :], v_s[pl.ds(c0, 2 * MOBA_BLOCK), :]
            scores = [_dot_nt(qhs[hh], kb2) for hh in range(2)]
            new, prs = [], []
            for hh in range(2):
                m, l, acc = st[3 * hh:3 * hh + 3]
                sel_a = jnp.sum(jnp.where(lane_sq == 2 * g, sels[hh], 0.0), axis=1, keepdims=True)
                sel_b = jnp.sum(jnp.where(lane_sq == 2 * g + 1, sels[hh], 0.0), axis=1, keepdims=True)
                keep = jnp.where(second_block, sel_b, sel_a) > 0.0
                s = jnp.where(keep, scores[hh], NEG_BIG)
                m_new = jnp.maximum(m, jnp.max(s, axis=1, keepdims=True))
                pr = jnp.exp(s - m_new)
                alpha = jnp.exp(m - m_new)
                new += [m_new, alpha * l + jnp.sum(pr, axis=1, keepdims=True), alpha * acc]
                prs.append(pr.astype(BF16))
            for hh in range(2):
                new[3 * hh + 2] = new[3 * hh + 2] + _dot(prs[hh], vb2)
            return tuple(new)

        st = lax.fori_loop(0, (own + 1) // 2, kpair, tuple(init))
        o_ref[0, pl.ds(r0, qrows), :] = jnp.where(first_head, st[2] / st[1], st[5] / st[4])
        return carry

    lax.fori_loop(0, seq // qrows, qblock, 0)


ROUTE_CHUNKS_PER_STEP = 4


def _oddeven_merge_sort_pairs(n):
    pairs = []

    def merge(lo, hi, r):
        step = 2 * r
        if step < hi - lo:
            merge(lo, hi, step)
            merge(lo + r, hi, step)
            pairs.extend((i, i + r) for i in range(lo + r, hi - r, step))
        else:
            pairs.append((lo, lo + r))

    def sort(lo, hi):
        if hi - lo >= 1:
            mid = lo + (hi - lo) // 2
            sort(lo, mid)
            sort(mid + 1, hi)
            merge(lo, hi, 1)

    sort(0, n - 1)
    return pairs


SUBLANES = 8


def _top16_rows(scores, n_rows, vals_refs, idx_refs):
    n_slabs = n_rows // SUBLANES
    sub = _iota((SUBLANES, LANES), 0)
    vals = [[s[SUBLANES * v:SUBLANES * (v + 1), :] for v in range(n_slabs)] for s in scores]
    idxs = [[sub + SUBLANES * v for v in range(n_slabs)] for _ in scores]
    for i, j in _oddeven_merge_sort_pairs(n_slabs):
        for va, ia in zip(vals, idxs):
            a, b = va[i], va[j]
            a_first = jnp.logical_or(a > b, jnp.logical_and(a == b, ia[i] < ia[j]))
            va[i], va[j] = jnp.maximum(a, b), jnp.minimum(a, b)
            ia[i], ia[j] = jnp.where(a_first, ia[i], ia[j]), jnp.where(a_first, ia[j], ia[i])
    for it in range(PEER_TOPK):
        for k, (va, ia) in enumerate(zip(vals, idxs)):
            m = jnp.max(va[0], axis=0, keepdims=True)
            pick = jnp.min(jnp.where(va[0] == m, ia[0], n_rows), axis=0, keepdims=True)
            vals_refs[k][it:it + 1, :] = m
            idx_refs[k][it:it + 1, :] = pick
            win = ia[0] == pick
            depth = PEER_TOPK - 1 - it
            for d in range(min(depth, n_slabs - 1)):
                va[d] = jnp.where(win, va[d + 1], va[d])
                ia[d] = jnp.where(win, ia[d + 1], ia[d])
            if depth >= n_slabs:
                va[n_slabs - 1] = jnp.where(win, -jnp.inf, va[n_slabs - 1])


def _peer_route_kernel(x_ref, g_ref, sh_ref, sc_ref, wqt_ref, sk_ref, h_ref, ids_ref, gts_ref,
                       q_s, val_s, idx_s, ids_s, gts_s):
    tt = x_ref.shape[0]
    n_chunks = tt // LANES
    per_step = ROUTE_CHUNKS_PER_STEP
    half = PEER_D_KEY // 2
    h = _adaln(x_ref[...], g_ref[...], sh_ref[0], sc_ref[0])
    h_ref[...] = h
    qt = _dot_nt(wqt_ref[...], h.astype(BF16))
    for c in range(n_chunks):
        q_s[c] = qt[:, c * LANES:(c + 1) * LANES]
    sk1, sk2 = sk_ref[0], sk_ref[1]
    col_id = _iota((PEER_TOPK, LANES), 0)
    sub = _iota((SUBLANES, LANES), 0)
    col_depth = jnp.zeros((SUBLANES, LANES), I32)
    for a in range(SUBLANES):
        col_depth = jnp.where(sub == a, PEER_CAND_COUNTS[a], col_depth)
    vals = [val_s.at[i] for i in range(2 * per_step)]
    idxs = [idx_s.at[i] for i in range(2 * per_step)]

    def body(step, carry):
        hh = step // (n_chunks // per_step)
        c0 = (step % (n_chunks // per_step)) * per_step
        q0 = pl.multiple_of(hh * PEER_D_KEY, PEER_D_KEY)
        scores = []
        for k in range(per_step):
            scores.append(_dot3(sk1, q_s[c0 + k, pl.ds(q0, half), :]))
            scores.append(_dot3(sk2, q_s[c0 + k, pl.ds(q0 + half, half), :]))
        _top16_rows(scores, PEER_N_KEYS, vals, idxs)
        state = []
        for k in range(per_step):
            v1, i1, v2, i2 = vals[2 * k], idxs[2 * k], vals[2 * k + 1], idxs[2 * k + 1]
            v1x, e1x = v1[0:SUBLANES, :], i1[0:SUBLANES, :] * PEER_N_KEYS
            xs = [jnp.where(col_depth > b, v1x + v2[b:b + 1, :], -jnp.inf) for b in range(PEER_TOPK)]
            ex = [e1x + i2[b:b + 1, :] for b in range(PEER_TOPK)]
            y = v1[SUBLANES:PEER_TOPK, :] + v2[0:1, :]
            ey = i1[SUBLANES:PEER_TOPK, :] * PEER_N_KEYS + i2[0:1, :]
            state.append([xs, ex, y, ey])
        for it in range(PEER_TOPK):
            for k in range(per_step):
                xs, ex, y, ey = state[k]
                heads = jnp.concatenate([xs[0], y], axis=0)
                m = jnp.max(heads, axis=0, keepdims=True)
                pick = jnp.min(jnp.where(heads == m, col_id, PEER_TOPK), axis=0, keepdims=True)
                win = col_id == pick
                eids = jnp.where(win, jnp.concatenate([ex[0], ey], axis=0), 0)
                idxs[2 * k][it:it + 1, :] = jnp.sum(eids, axis=0, keepdims=True)
                vals[2 * k][it:it + 1, :] = m
                win_x, win_y = win[0:SUBLANES, :], win[SUBLANES:PEER_TOPK, :]
                for dd in range(PEER_TOPK - 1 - it):
                    xs[dd] = jnp.where(win_x, xs[dd + 1], xs[dd])
                    ex[dd] = jnp.where(win_x, ex[dd + 1], ex[dd])
                state[k][2] = jnp.where(win_y, -jnp.inf, y)
        r0 = pl.multiple_of(hh * PEER_TOPK, PEER_TOPK)
        for k in range(per_step):
            top = vals[2 * k][...]
            e = jnp.exp(top - top[0:1, :])
            gts_s[c0 + k, pl.ds(r0, PEER_TOPK), :] = e / jnp.sum(e, axis=0, keepdims=True)
            ids_s[c0 + k, pl.ds(r0, PEER_TOPK), :] = idxs[2 * k][...] * PACK_ROWS
        return carry

    lax.fori_loop(0, PEER_HEADS * n_chunks // per_step, body, 0)
    for c in range(n_chunks):
        rows = slice(c * LANES, (c + 1) * LANES)
        ids_ref[pl.ds(c * LANES * PEER_SLOTS, LANES * PEER_SLOTS)] = ids_s[c].T.reshape(LANES * PEER_SLOTS)
        gts_ref[rows, :] = gts_s[c].T


def _peer_route(x2, g, shift, scale, wqt_bf, sub_keys, seq):
    t, d = x2.shape
    tt = 512
    per_b = seq // tt
    n_chunks = tt // LANES
    nq = wqt_bf.shape[0]
    out_blk = pl.BlockSpec((tt, PEER_SLOTS), lambda i: (i, 0))
    return pl.pallas_call(
        _peer_route_kernel,
        grid=(t // tt,),
        in_specs=[
            pl.BlockSpec((tt, d), lambda i: (i, 0)),
            pl.BlockSpec((1, d), lambda i: (0, 0)),
            pl.BlockSpec((1, 1, d), lambda i: (i // per_b, 0, 0)),
            pl.BlockSpec((1, 1, d), lambda i: (i // per_b, 0, 0)),
            pl.BlockSpec((nq, d), lambda i: (0, 0)),
            pl.BlockSpec(sub_keys.shape, lambda i: (0, 0, 0)),
        ],
        out_specs=[pl.BlockSpec((tt, d), lambda i: (i, 0)),
                   pl.BlockSpec((tt * PEER_SLOTS,), lambda i: (i,)), out_blk],
        out_shape=[jax.ShapeDtypeStruct((t, d), F32),
                   jax.ShapeDtypeStruct((t * PEER_SLOTS,), I32),
                   jax.ShapeDtypeStruct((t, PEER_SLOTS), F32)],
        scratch_shapes=[pltpu.VMEM((n_chunks, nq, LANES), F32),
                        pltpu.VMEM((2 * ROUTE_CHUNKS_PER_STEP, PEER_TOPK, LANES), F32),
                        pltpu.VMEM((2 * ROUTE_CHUNKS_PER_STEP, PEER_TOPK, LANES), I32),
                        pltpu.VMEM((n_chunks, PEER_SLOTS, LANES), I32),
                        pltpu.VMEM((n_chunks, PEER_SLOTS, LANES), F32)],
        compiler_params=_cparams(1),
        name="peer_route",
    )(x2, g, shift, scale, wqt_bf, sub_keys)


def _chunk_row(c):
    return 2 * (c % PACK_ROWS) + c // PACK_ROWS


def _bf16_bits(a):
    return lax.bitcast_convert_type(a.astype(BF16).astype(F32), U32)


def _pack_kernel(t_ref, o_ref):
    n = t_ref.shape[1]
    half = t_ref.shape[2] // 2
    for s in range(PACK_ROWS):
        lo = _bf16_bits(t_ref[0, :, s * LANES:(s + 1) * LANES])
        hi = _bf16_bits(t_ref[0, :, half + s * LANES:half + (s + 1) * LANES])
        o_ref[pl.ds(s, n, stride=PACK_ROWS), :] = (lo >> 16) | hi


def _pack_table(tables, layer):
    _, e, d = tables.shape
    assert d == 2 * PACK_ROWS * LANES
    be = 512
    return pl.pallas_call(
        _pack_kernel,
        grid=(e // be,),
        in_specs=[pl.BlockSpec((1, be, d), lambda i: (layer, i, 0))],
        out_specs=pl.BlockSpec((be * PACK_ROWS, LANES), lambda i: (i, 0)),
        out_shape=jax.ShapeDtypeStruct((e * PACK_ROWS, LANES), U32),
        compiler_params=_cparams(1),
        name="pack_expert_table",
    )(tables)


def _table_spec(rows):
    return pl.BlockSpec((rows, LANES), lambda i: (0, 0), pipeline_mode=pl.Buffered(1))


def _gelu_exact(a):
    return 0.5 * a * (1.0 + lax.erf(a * (2.0 ** -0.5)))


FEAT_CHUNKS = 8
SLOT_WIDTH = PEER_SLOTS * FEAT_CHUNKS


def _gather_rows(ids_ref, base, tbl_ref, slot):
    for j in range(PEER_SLOTS):
        if j % ID_VIEW == 0:
            ids_part = ids_ref.at[pl.ds(base + j, ID_VIEW)]
        row0 = pl.multiple_of(ids_part[j % ID_VIEW], PACK_ROWS)
        slot[PACK_ROWS * j:PACK_ROWS * (j + 1), :] = tbl_ref[pl.ds(row0, PACK_ROWS), :]


def _pipelined_tokens(tt, ids_ref, tbl_ref, slots, compute):
    group = len(slots)
    n_steps = tt // group
    tiles = group // SUBLANES

    def gather(t, slot):
        _gather_rows(ids_ref, t * PEER_SLOTS, tbl_ref, slot)

    def compute_group(q):
        for k in range(group):
            compute(group * q + k, tiles * q + k // SUBLANES, k % SUBLANES, slots[k])

    for k in range(group):
        gather(k, slots[k])

    def step(q, carry):
        compute_group(q)
        for k in range(group):
            gather(group * (q + 1) + k, slots[k])
        return carry

    lax.fori_loop(0, n_steps - 1, step, 0)
    compute_group(n_steps - 1)


def _chunk_diag():
    return (_iota((FEAT_CHUNKS, SLOT_WIDTH), 1) % FEAT_CHUNKS) == _iota((FEAT_CHUNKS, SLOT_WIDTH), 0)


def _peer_u_kernel(ids_ref, h_ref, g_ref, tbl_ref, coef_ref, *scratch):
    slots, (hx_s, rs_s) = scratch[:N_SLOTS], scratch[N_SLOTS:]
    tt = h_ref.shape[0]
    for c in range(FEAT_CHUNKS):
        hx_s[pl.ds(_chunk_row(c), tt, stride=FEAT_CHUNKS), :] = h_ref[:, c * LANES:(c + 1) * LANES]
    diag = _chunk_diag()

    def compute(t, tile, sub, slot):
        rows = pltpu.bitcast(slot[...], BF16)
        x8 = hx_s[pl.ds(pl.multiple_of(t * FEAT_CHUNKS, FEAT_CHUNKS), FEAT_CHUNKS), :]
        part = _dot_nt(x8.astype(BF16), rows)
        rs_s[tile, sub:sub + 1, :] = jnp.sum(jnp.where(diag, part, 0.0), axis=0, keepdims=True)

    _pipelined_tokens(tt, ids_ref, tbl_ref, slots, compute)
    group = jnp.where(_iota((SLOT_WIDTH, PEER_SLOTS), 0) // FEAT_CHUNKS == _iota((SLOT_WIDTH, PEER_SLOTS), 1),
                      1.0, 0.0).astype(BF16)
    hi, lo = _split_bf16(rs_s[...].reshape(tt, SLOT_WIDTH))
    act = _dot(hi, group) + _dot(lo, group)
    coef_ref[...] = g_ref[...] * _gelu_exact(act)


N_SLOTS = 16
ID_VIEW = 16
PEER_TOKEN_TILE = 512


def _slot_scratch():
    return [pltpu.VMEM((PEER_SLOTS * PACK_ROWS, LANES), U32)] * N_SLOTS


def _peer_u(ids_flat, h, gates, table, tt):
    t, d = h.shape
    return pl.pallas_call(
        _peer_u_kernel,
        grid=(t // tt,),
        in_specs=[
            pl.BlockSpec((tt * PEER_SLOTS,), lambda i: (i,), memory_space=pltpu.SMEM),
            pl.BlockSpec((tt, d), lambda i: (i, 0)),
            pl.BlockSpec((tt, PEER_SLOTS), lambda i: (i, 0)),
            _table_spec(table.shape[0]),
        ],
        out_specs=pl.BlockSpec((tt, PEER_SLOTS), lambda i: (i, 0)),
        out_shape=jax.ShapeDtypeStruct((t, PEER_SLOTS), F32),
        scratch_shapes=_slot_scratch() + [
                        pltpu.VMEM((tt * FEAT_CHUNKS, LANES), F32),
                        pltpu.VMEM((tt // SUBLANES, SUBLANES, SLOT_WIDTH), F32)],
        compiler_params=_cparams(1, TABLE_VMEM_LIMIT),
        name="peer_expert_in",
    )(ids_flat, h, gates, table)


def _peer_v_kernel(ids_ref, coef_ref, x_ref, gate_ref, tbl_ref, o_ref, *scratch):
    slots, (ce_hi_s, ce_lo_s, res_s) = scratch[:N_SLOTS], scratch[N_SLOTS:]
    tt = x_ref.shape[0]
    spread = jnp.where(_iota((PEER_SLOTS, SLOT_WIDTH), 1) // FEAT_CHUNKS == _iota((PEER_SLOTS, SLOT_WIDTH), 0),
                       1.0, 0.0).astype(BF16)
    hi, lo = _split_bf16(coef_ref[...])
    ce_hi_s[...] = _dot(hi, spread).reshape(ce_hi_s.shape)
    ce_lo_s[...] = _dot(lo, spread).reshape(ce_lo_s.shape)
    diag = _chunk_diag()

    def compute(t, tile, sub, slot):
        rows = pltpu.bitcast(slot[...], BF16)
        a_hi = jnp.where(diag, ce_hi_s[tile, sub:sub + 1, :], 0.0)
        a_lo = jnp.where(diag, ce_lo_s[tile, sub:sub + 1, :], 0.0)
        both = _dot(jnp.concatenate([a_hi, a_lo], axis=0).astype(BF16), rows)
        r0 = pl.multiple_of(t * FEAT_CHUNKS, FEAT_CHUNKS)
        res_s[pl.ds(r0, FEAT_CHUNKS), :] = both[0:FEAT_CHUNKS, :] + both[FEAT_CHUNKS:2 * FEAT_CHUNKS, :]

    _pipelined_tokens(tt, ids_ref, tbl_ref, slots, compute)
    for c in range(FEAT_CHUNKS):
        cols = slice(c * LANES, (c + 1) * LANES)
        y = res_s[pl.ds(_chunk_row(c), tt, stride=FEAT_CHUNKS), :]
        o_ref[:, cols] = x_ref[:, cols] + gate_ref[0][:, cols] * y


def _peer_v(ids_flat, coef, x2, gate, table, seq, tt):
    t, d = x2.shape
    per_b = seq // tt
    blk = pl.BlockSpec((tt, d), lambda i: (i, 0))
    return pl.pallas_call(
        _peer_v_kernel,
        grid=(t // tt,),
        in_specs=[
            pl.BlockSpec((tt * PEER_SLOTS,), lambda i: (i,), memory_space=pltpu.SMEM),
            pl.BlockSpec((tt, PEER_SLOTS), lambda i: (i, 0)),
            blk,
            pl.BlockSpec((1, 1, d), lambda i: (i // per_b, 0, 0)),
            _table_spec(table.shape[0]),
        ],
        out_specs=blk,
        out_shape=jax.ShapeDtypeStruct((t, d), F32),
        scratch_shapes=_slot_scratch() + [
                        pltpu.VMEM((tt // SUBLANES, SUBLANES, SLOT_WIDTH), F32),
                        pltpu.VMEM((tt // SUBLANES, SUBLANES, SLOT_WIDTH), F32),
                        pltpu.VMEM((tt * FEAT_CHUNKS, LANES), F32)],
        compiler_params=_cparams(1, TABLE_VMEM_LIMIT),
        name="peer_expert_out",
    )(ids_flat, coef, x2, gate, table)


def _peer_ffn(x2, g, shift, scale, gate, wq, sub_keys, table_u, table_v, seq):
    t, d = x2.shape
    tt = PEER_TOKEN_TILE
    h, ids, gates = _peer_route(x2, g, shift, scale, wq.T.astype(BF16), sub_keys, seq)
    ids_flat = ids
    coef = _peer_u(ids_flat, h, gates, table_u, tt)
    return _peer_v(ids_flat, coef, x2, gate, table_v, seq, tt)


def _rope_tables(seq):
    half = HEAD_DIM // 2
    inv_freq = ROPE_THETA ** (-jnp.arange(half, dtype=F32) / half)
    ang = jnp.arange(seq).astype(F32)[:, None] * inv_freq[None, :]
    reps = LANES // half
    return jnp.tile(jnp.cos(ang), (1, reps)), jnp.tile(jnp.sin(ang), (1, reps))


def _two_heads(gain):
    return jnp.tile(gain.reshape(1, HEAD_DIM), (1, LANES // HEAD_DIM))


def kernel(x, c, ada_w, ada_b, norm_mix_g, norm_ffn_g, w_in_ab, w_out_ab, sinks_a, qnorm_a, knorm_a,
           w_in_cd, w_out_cd, qnorm_c, knorm_c, qnorm_d, knorm_d, peer_wq, peer_subkeys, peer_u, peer_v):
    b, seq, d = x.shape
    depth = ada_w.shape[0]
    t = b * seq
    cos, sin = _rope_tables(seq)
    mod = _modulation(c, ada_w, ada_b)
    x2 = x.reshape(t, d)
    for layer in range(depth):
        shift_m, scale_m, gate_m, shift_f, scale_f, gate_f = [
            m.reshape(b, 1, d) for m in jnp.split(mod[layer], 6, axis=-1)]
        g_mix = norm_mix_g[layer].reshape(1, d)
        i = layer // 2
        if layer % 2 == 0:
            proj = _norm_proj(x2, g_mix, shift_m, scale_m, w_in_ab[i].astype(BF16), seq)
            proj = proj.reshape(b, seq, -1)
            ya = _swa_attention(proj, sinks_a[i], cos, sin, _two_heads(qnorm_a[i]), _two_heads(knorm_a[i]))
            b_col = (A_Q_HEADS + 2 * A_KV_HEADS) * HEAD_DIM // LANES
            yb = _stick_attention(proj, b_col)
            w_out = w_out_ab[i]
        else:
            proj = _norm_proj(x2, g_mix, shift_m, scale_m, w_in_cd[i].astype(BF16), seq)
            proj = proj.reshape(b, seq, -1)
            ya = _qkv_attention_call(_dilated_kernel, "dilated_attention", proj, 0, C_HEADS, cos, sin,
                                     _two_heads(qnorm_c[i]), _two_heads(knorm_c[i]))
            d_col = 3 * C_HEADS * HEAD_DIM // LANES
            yb = _qkv_attention_call(_moba_kernel, "moba_attention", proj, d_col, D_HEADS, cos, sin,
                                     _two_heads(qnorm_d[i]), _two_heads(knorm_d[i]),
                                     extra_scratch=(pltpu.VMEM((LANES, LANES), F32),
                                                    pltpu.VMEM((seq, LANES), F32),
                                                    pltpu.VMEM((seq, LANES), F32)))
            w_out = w_out_cd[i]
        x2 = _out_proj(x2, ya.reshape(t, -1), yb.reshape(t, -1), w_out.astype(BF16), gate_m, seq)
        x2 = _peer_ffn(x2, norm_ffn_g[layer].reshape(1, d), shift_f, scale_f, gate_f,
                       peer_wq[layer], peer_subkeys[layer],
                       _pack_table(peer_u, layer), _pack_table(peer_v, layer), seq)
    return x2.reshape(b, seq, d)
```

```python
import jax
import jax.numpy as jnp
from jax import lax
from jax.experimental import pallas as pl
from jax.experimental.pallas import tpu as pltpu

F32 = jnp.float32
BF16 = jnp.bfloat16
I32 = jnp.int32
U32 = jnp.uint32

HEAD_DIM = 64
ROPE_THETA = 10000.0
NORM_EPS = 1e-6
LANES = 128
QUERY_BLOCK = 128
A_Q_HEADS, A_KV_HEADS = 8, 2
B_HEADS = C_HEADS = D_HEADS = 8
C_PATTERNS = ((128, 1), (512, 4), (2048, 16))
MOBA_BLOCK, MOBA_TOPK = 256, 3
PEER_HEADS, PEER_N_KEYS, PEER_TOPK, PEER_D_KEY = 8, 128, 16, 256
PEER_SLOTS = PEER_HEADS * PEER_TOPK
NEG_BIG = -1e30
PEER_CAND_COUNTS = tuple(PEER_TOPK // (a + 1) for a in range(PEER_TOPK))
PACK_ROWS = 4
MIB = 1024 * 1024
V7X_VMEM_BYTES = 64 * MIB
STREAM_VMEM_LIMIT = 40 * MIB
TABLE_VMEM_LIMIT = V7X_VMEM_BYTES - 8 * MIB


def _cparams(n_axes, vmem_bytes=STREAM_VMEM_LIMIT):
    return pltpu.CompilerParams(
        dimension_semantics=("arbitrary",) * n_axes,
        vmem_limit_bytes=vmem_bytes)


def _split_bf16(a):
    hi = a.astype(BF16)
    lo = (a - hi.astype(F32)).astype(BF16)
    return hi, lo


def _dot(a, b):
    return jnp.dot(a, b, preferred_element_type=F32)


def _dot_nt(a, b):
    return lax.dot_general(a, b, (((1,), (1,)), ((), ())), preferred_element_type=F32)


def _dot3(a, b):
    ah, al = _split_bf16(a)
    bh, bl = _split_bf16(b)
    return _dot(ah, bh) + _dot(ah, bl) + _dot(al, bh)


def _dot3_nt(a, b):
    ah, al = _split_bf16(a)
    bh, bl = _split_bf16(b)
    return _dot_nt(ah, bh) + _dot_nt(ah, bl) + _dot_nt(al, bh)


def _iota(shape, dim):
    return lax.broadcasted_iota(I32, shape, dim)


def _mod_kernel(c_ref, w_ref, b_ref, o_ref):
    c = c_ref[...]
    cond = c * jax.nn.sigmoid(c)
    o_ref[0] = _dot3(cond, w_ref[0]) + b_ref[0]


def _modulation(c, ada_w, ada_b):
    depth, d, n = ada_w.shape
    b = c.shape[0]
    tn = 1024
    return pl.pallas_call(
        _mod_kernel,
        grid=(depth, n // tn),
        in_specs=[
            pl.BlockSpec((b, d), lambda l, j: (0, 0)),
            pl.BlockSpec((1, d, tn), lambda l, j: (l, 0, j)),
            pl.BlockSpec((1, 1, tn), lambda l, j: (l, 0, j)),
        ],
        out_specs=pl.BlockSpec((1, b, tn), lambda l, j: (l, 0, j)),
        out_shape=jax.ShapeDtypeStruct((depth, b, n), F32),
        compiler_params=_cparams(2),
        name="adaln_modulation",
    )(c, ada_w, ada_b.reshape(depth, 1, n))


def _adaln(x, g, shift, scale):
    ms = jnp.mean(x * x, axis=-1, keepdims=True)
    y = x * lax.rsqrt(ms + NORM_EPS) * g
    return y * (1.0 + scale) + shift


def _norm_proj_kernel(x_ref, g_ref, sh_ref, sc_ref, w_ref, o_ref):
    h = _adaln(x_ref[...], g_ref[...], sh_ref[0], sc_ref[0])
    o_ref[...] = _dot(h.astype(BF16), w_ref[...])


def _norm_proj(x2, g, shift, scale, w_bf, seq):
    t, d = x2.shape
    n = w_bf.shape[1]
    tt = 512
    per_b = seq // tt
    return pl.pallas_call(
        _norm_proj_kernel,
        grid=(t // tt,),
        in_specs=[
            pl.BlockSpec((tt, d), lambda i: (i, 0)),
            pl.BlockSpec((1, d), lambda i: (0, 0)),
            pl.BlockSpec((1, 1, d), lambda i: (i // per_b, 0, 0)),
            pl.BlockSpec((1, 1, d), lambda i: (i // per_b, 0, 0)),
            pl.BlockSpec((d, n), lambda i: (0, 0)),
        ],
        out_specs=pl.BlockSpec((tt, n), lambda i: (i, 0)),
        out_shape=jax.ShapeDtypeStruct((t, n), F32),
        compiler_params=_cparams(1),
        name="adaln_in_proj",
    )(x2, g, shift, scale, w_bf)


def _out_proj_kernel(x_ref, ya_ref, yb_ref, w_ref, gate_ref, o_ref):
    half = ya_ref.shape[1]
    y = _dot(ya_ref[...].astype(BF16), w_ref[0:half, :])
    y = y + _dot(yb_ref[...].astype(BF16), w_ref[half:2 * half, :])
    o_ref[...] = x_ref[...] + gate_ref[0] * y


def _out_proj(x2, ya, yb, w_bf, gate, seq):
    t, d = x2.shape
    half = ya.shape[1]
    tt = 512
    per_b = seq // tt
    return pl.pallas_call(
        _out_proj_kernel,
        grid=(t // tt,),
        in_specs=[
            pl.BlockSpec((tt, d), lambda i: (i, 0)),
            pl.BlockSpec((tt, half), lambda i: (i, 0)),
            pl.BlockSpec((tt, half), lambda i: (i, 0)),
            pl.BlockSpec((2 * half, d), lambda i: (0, 0)),
            pl.BlockSpec((1, 1, d), lambda i: (i // per_b, 0, 0)),
        ],
        out_specs=pl.BlockSpec((tt, d), lambda i: (i, 0)),
        out_shape=jax.ShapeDtypeStruct((t, d), F32),
        compiler_params=_cparams(1),
        name="mixer_out_proj",
    )(x2, ya, yb, w_bf, gate)


def _lane_row():
    return _iota((1, LANES), 1)


def _head_segment_ones():
    r = _iota((LANES, LANES), 0) // HEAD_DIM
    c = _iota((LANES, LANES), 1) // HEAD_DIM
    return jnp.where(r == c, 1.0, 0.0).astype(BF16)


def _headnorm_rope(a, g, cos, sin):
    hi, lo = _split_bf16(a * a)
    seg = _head_segment_ones()
    ms = (_dot(hi, seg) + _dot(lo, seg)) * (1.0 / HEAD_DIM)
    y = a * lax.rsqrt(ms + NORM_EPS) * g
    half = HEAD_DIM // 2
    upper = pltpu.roll(y, LANES - half, axis=1)
    lower = pltpu.roll(y, half, axis=1)
    first_half = (_lane_row() % HEAD_DIM) < half
    rot = jnp.where(first_half, -upper, lower)
    return y * cos + rot * sin


def _head_masks():
    lane = _lane_row()
    return lane < HEAD_DIM, lane >= HEAD_DIM


def _attn_specs(seq, qcol, kcol, vcol, kv_shared):
    blk = (1, seq, LANES)
    q_spec = pl.BlockSpec(blk, lambda b, p: (b, 0, qcol + p))
    if kv_shared:
        k_spec = pl.BlockSpec(blk, lambda b, p: (b, 0, kcol))
        v_spec = pl.BlockSpec(blk, lambda b, p: (b, 0, vcol))
    else:
        k_spec = pl.BlockSpec(blk, lambda b, p: (b, 0, kcol + p))
        v_spec = pl.BlockSpec(blk, lambda b, p: (b, 0, vcol + p))
    return q_spec, k_spec, v_spec


def _row_spec(seq):
    return pl.BlockSpec((seq, LANES), lambda b, p: (0, 0))


def _gain_spec():
    return pl.BlockSpec((1, LANES), lambda b, p: (0, 0))


def _store_heads(o_ref, r0, outs):
    first, _ = _head_masks()
    o_ref[0, pl.ds(r0, QUERY_BLOCK), :] = jnp.where(first, outs[0], outs[1])


SWA_QBLOCKS_PER_STEP = 4


def _swa_kernel(sinks_ref, q_ref, k_ref, v_ref, cos_ref, sin_ref, gq_ref, gk_ref, o_ref,
                q0_s, q1_s, k_s, v_s):
    p = pl.program_id(1)
    seq = q_ref.shape[1]
    cos, sin = cos_ref[...], sin_ref[...]
    first, second = _head_masks()
    qn = _headnorm_rope(q_ref[0], gq_ref[...], cos, sin) * (HEAD_DIM ** -0.5)
    q0_s[...] = jnp.where(first, qn, 0.0).astype(BF16)
    q1_s[...] = jnp.where(second, qn, 0.0).astype(BF16)
    pairs_per_kv = (A_Q_HEADS // A_KV_HEADS) // 2
    keep = jnp.logical_xor(first, (p // pairs_per_kv) == 1)
    kn = _headnorm_rope(k_ref[0], gk_ref[...], cos, sin)
    k_s[...] = jnp.where(keep, kn, pltpu.roll(kn, HEAD_DIM, axis=1)).astype(BF16)
    v = v_ref[0]
    v_s[...] = jnp.where(keep, v, pltpu.roll(v, HEAD_DIM, axis=1)).astype(BF16)

    qi = _iota((QUERY_BLOCK, QUERY_BLOCK), 0)
    ki = _iota((QUERY_BLOCK, QUERY_BLOCK), 1)

    mask_c = ki <= qi
    per_step = min(SWA_QBLOCKS_PER_STEP, seq // QUERY_BLOCK)

    def qstep(step, carry):
        chains, scores = [], []
        for u in range(per_step):
            i = step * per_step + u
            r0 = pl.multiple_of(i * QUERY_BLOCK, QUERY_BLOCK)
            rp = pl.multiple_of(jnp.maximum(i - 1, 0) * QUERY_BLOCK, QUERY_BLOCK)
            kc, kp = k_s[pl.ds(r0, QUERY_BLOCK), :], k_s[pl.ds(rp, QUERY_BLOCK), :]
            vc, vp = v_s[pl.ds(r0, QUERY_BLOCK), :], v_s[pl.ds(rp, QUERY_BLOCK), :]
            mask_p = jnp.logical_and(ki > qi, i > 0)
            chains.append((r0, vc, vp, mask_p))
            for q_s in (q0_s, q1_s):
                qh = q_s[pl.ds(r0, QUERY_BLOCK), :]
                scores.append((_dot_nt(qh, kc), _dot_nt(qh, kp)))
        probs, denoms = [], []
        for n, (s_cur, s_prev) in enumerate(scores):
            mask_p = chains[n // 2][3]
            sc = jnp.where(mask_c, s_cur, NEG_BIG)
            sp = jnp.where(mask_p, s_prev, NEG_BIG)
            sink = sinks_ref[2 * p + n % 2]
            m = jnp.maximum(jnp.max(sc, axis=1, keepdims=True), jnp.max(sp, axis=1, keepdims=True))
            m = jnp.maximum(m, sink)
            ec, ep = jnp.exp(sc - m), jnp.exp(sp - m)
            denoms.append(jnp.sum(ec, axis=1, keepdims=True) + jnp.sum(ep, axis=1, keepdims=True)
                          + jnp.exp(sink - m))
            probs.append((ec.astype(BF16), ep.astype(BF16)))
        for u, (r0, vc, vp, _) in enumerate(chains):
            outs = [(_dot(probs[2 * u + hh][0], vc) + _dot(probs[2 * u + hh][1], vp)) / denoms[2 * u + hh]
                    for hh in range(2)]
            _store_heads(o_ref, r0, outs)
        return carry

    lax.fori_loop(0, seq // (QUERY_BLOCK * per_step), qstep, 0)


def _swa_attention(proj, sinks, cos, sin, gq, gk):
    b, seq, _ = proj.shape
    n_pairs = A_Q_HEADS // 2
    kcol = A_Q_HEADS * HEAD_DIM // LANES
    vcol = kcol + A_KV_HEADS * HEAD_DIM // LANES
    q_spec, k_spec, v_spec = _attn_specs(seq, 0, kcol, vcol, True)
    return pl.pallas_call(
        _swa_kernel,
        grid=(b, n_pairs),
        in_specs=[pl.BlockSpec(memory_space=pltpu.SMEM), q_spec, k_spec, v_spec,
                  _row_spec(seq), _row_spec(seq), _gain_spec(), _gain_spec()],
        out_specs=pl.BlockSpec((1, seq, LANES), lambda b_, p: (b_, 0, p)),
        out_shape=jax.ShapeDtypeStruct((b, seq, n_pairs * LANES), F32),
        scratch_shapes=[pltpu.VMEM((seq, LANES), BF16)] * 4,
        compiler_params=_cparams(2),
        name="swa_gqa_attention",
    )(sinks, proj, proj, proj, cos, sin, gq, gk)


STICK_GROUP = 4
STICK_QUERY_ROWS = 256


def _stick_kernel(q_ref, k_ref, v_ref, o_ref, k_s, v_s):
    seq = q_ref.shape[1]
    k_s[...] = k_ref[0].astype(BF16)
    v_s[...] = v_ref[0].astype(BF16)
    first, second = _head_masks()
    qr = min(STICK_QUERY_ROWS, seq)
    kw = QUERY_BLOCK
    key_minus_query = _iota((qr, kw), 1) - _iota((qr, kw), 0)
    wr = _iota((2 * kw, 2 * kw), 0) % kw
    wc = _iota((2 * kw, 2 * kw), 1)
    suffix_w = jnp.where(jnp.logical_or(wc >= kw, wr > wc), 1.0, 0.0).astype(BF16)

    def qblock(i, carry):
        r0 = pl.multiple_of(i * qr, qr)
        q = q_ref[0, pl.ds(r0, qr), :] * (HEAD_DIM ** -0.5)
        qhs = [jnp.where(msk, q, 0.0).astype(BF16) for msk in (first, second)]
        n_blocks = (r0 + qr) // kw

        def kgroup(g, st):
            accs, laters = [st[0], st[1]], [st[2], st[3]]
            chains = [(u, hh) for u in range(STICK_GROUP) for hh in range(2)]
            vbs, pasts, zs = [], [], {}
            for u in range(STICK_GROUP):
                j = n_blocks - 1 - (g * STICK_GROUP + u)
                live = j >= 0
                c0 = pl.multiple_of(jnp.maximum(j, 0) * kw, kw)
                kb = k_s[pl.ds(c0, kw), :]
                vbs.append(v_s[pl.ds(c0, kw), :])
                pasts.append(jnp.logical_and(key_minus_query < r0 - c0, live))
                for hh in range(2):
                    zs[u, hh] = _dot_nt(qhs[hh], kb)
            logit, sums = {}, {}
            for u, hh in chains:
                z = zs[u, hh]
                sp = jnp.maximum(z, 0.0) + jnp.log(1.0 + jnp.exp(-jnp.abs(z)))
                neg_log_keep = jnp.where(pasts[u], sp, 0.0)
                logit[u, hh] = z - sp
                hi, lo = _split_bf16(neg_log_keep)
                sums[u, hh] = _dot(jnp.concatenate([hi, lo], axis=1), suffix_w)
            ws = {}
            for u, hh in chains:
                inner, total = sums[u, hh][:, :kw], sums[u, hh][:, kw:]
                ws[u, hh] = jnp.where(pasts[u], jnp.exp(logit[u, hh] - inner - laters[hh]), 0.0).astype(BF16)
                laters[hh] = laters[hh] + total
            for u, hh in chains:
                accs[hh] = accs[hh] + _dot(ws[u, hh], vbs[u])
            return accs[0], accs[1], laters[0], laters[1]

        zero = jnp.zeros((qr, LANES), F32)
        n_groups = (n_blocks + STICK_GROUP - 1) // STICK_GROUP
        st = lax.fori_loop(0, n_groups, kgroup, (zero, zero, zero, zero))
        o_ref[0, pl.ds(r0, qr), :] = jnp.where(first, st[0], st[1])
        return carry

    lax.fori_loop(0, seq // qr, qblock, 0)


def _stick_attention(proj, qcol):
    b, seq, _ = proj.shape
    n_pairs = B_HEADS // 2
    q_spec, k_spec, v_spec = _attn_specs(seq, qcol, qcol + n_pairs, qcol + 2 * n_pairs, False)
    return pl.pallas_call(
        _stick_kernel,
        grid=(b, n_pairs),
        in_specs=[q_spec, k_spec, v_spec],
        out_specs=pl.BlockSpec((1, seq, LANES), lambda b_, p: (b_, 0, p)),
        out_shape=jax.ShapeDtypeStruct((b, seq, n_pairs * LANES), F32),
        scratch_shapes=[pltpu.VMEM((seq, LANES), BF16)] * 2,
        compiler_params=_cparams(2),
        name="stick_breaking_attention",
    )(proj, proj, proj)


def _prep_qkv(q_ref, k_ref, v_ref, cos_ref, sin_ref, gq_ref, gk_ref, q0_s, q1_s, k_s, v_s):
    cos, sin = cos_ref[...], sin_ref[...]
    first, second = _head_masks()
    qn = _headnorm_rope(q_ref[0], gq_ref[...], cos, sin) * (HEAD_DIM ** -0.5)
    q0_s[...] = jnp.where(first, qn, 0.0).astype(BF16)
    q1_s[...] = jnp.where(second, qn, 0.0).astype(BF16)
    kn = _headnorm_rope(k_ref[0], gk_ref[...], cos, sin)
    k_s[...] = kn.astype(BF16)
    v_s[...] = v_ref[0].astype(BF16)
    return qn, kn


DILATED_KEY_TILE = 512
DILATED_QUERY_ROWS = 256


def _dilated_kernel(q_ref, k_ref, v_ref, cos_ref, sin_ref, gq_ref, gk_ref, o_ref,
                    q0_s, q1_s, k_s, v_s):
    seq = q_ref.shape[1]
    _prep_qkv(q_ref, k_ref, v_ref, cos_ref, sin_ref, gq_ref, gk_ref, q0_s, q1_s, k_s, v_s)
    kt = min(DILATED_KEY_TILE, seq)
    qr = min(DILATED_QUERY_ROWS, seq)
    qk = _iota((qr, kt), 0) - _iota((qr, kt), 1)
    on_stride = [jnp.where((qk & (dil - 1)) == 0, 1.0, 0.0) for _, dil in C_PATTERNS]
    first_head, _ = _head_masks()

    def qblock(i, carry):
        r0 = pl.multiple_of(i * qr, qr)
        qhs = [q_s[pl.ds(r0, qr), :] for q_s in (q0_s, q1_s)]

        def ktile(g, st):
            c0 = pl.multiple_of(g * kt, kt)
            d = (r0 - c0) + qk
            count = jnp.zeros(d.shape, F32)
            for (window, _), stride_ok in zip(C_PATTERNS, on_stride):
                count = count + jnp.where(d <= window, stride_ok, 0.0)
            count = jnp.where(d >= 0, count, 0.0)
            kb, vb = k_s[pl.ds(c0, kt), :], v_s[pl.ds(c0, kt), :]
            scores = [_dot_nt(qhs[hh], kb) for hh in range(2)]
            new, prs = [], []
            for hh in range(2):
                m, l, acc = st[3 * hh:3 * hh + 3]
                s = jnp.where(count > 0.0, scores[hh], NEG_BIG)
                m_new = jnp.maximum(m, jnp.max(s, axis=1, keepdims=True))
                pr = count * jnp.exp(s - m_new)
                alpha = jnp.exp(m - m_new)
                new += [m_new, alpha * l + jnp.sum(pr, axis=1, keepdims=True), alpha * acc]
                prs.append(pr.astype(BF16))
            for hh in range(2):
                new[3 * hh + 2] = new[3 * hh + 2] + _dot(prs[hh], vb)
            return tuple(new)

        init = (jnp.full((qr, 1), NEG_BIG, F32), jnp.zeros((qr, 1), F32),
                jnp.zeros((qr, LANES), F32)) * 2
        st = lax.fori_loop(0, (r0 + qr + kt - 1) // kt, ktile, init)
        o_ref[0, pl.ds(r0, qr), :] = jnp.where(first_head, st[2] / st[1], st[5] / st[4])
        return carry

    lax.fori_loop(0, seq // qr, qblock, 0)


def _qkv_attention_call(kernel, name, proj, qcol, n_heads, cos, sin, gq, gk, extra_scratch=()):
    b, seq, _ = proj.shape
    n_pairs = n_heads // 2
    q_spec, k_spec, v_spec = _attn_specs(seq, qcol, qcol + n_pairs, qcol + 2 * n_pairs, False)
    return pl.pallas_call(
        kernel,
        grid=(b, n_pairs),
        in_specs=[q_spec, k_spec, v_spec, _row_spec(seq), _row_spec(seq), _gain_spec(), _gain_spec()],
        out_specs=pl.BlockSpec((1, seq, LANES), lambda b_, p: (b_, 0, p)),
        out_shape=jax.ShapeDtypeStruct((b, seq, n_pairs * LANES), F32),
        scratch_shapes=[pltpu.VMEM((seq, LANES), BF16)] * 4 + list(extra_scratch),
        compiler_params=_cparams(2),
        name=name,
    )(proj, proj, proj, cos, sin, gq, gk)


def _moba_kernel(q_ref, k_ref, v_ref, cos_ref, sin_ref, gq_ref, gk_ref, o_ref,
                 q0_s, q1_s, k_s, v_s, km_s, sel0_s, sel1_s):
    seq = q_ref.shape[1]
    n_blocks = seq // MOBA_BLOCK
    qn, kn = _prep_qkv(q_ref, k_ref, v_ref, cos_ref, sin_ref, gq_ref, gk_ref, q0_s, q1_s, k_s, v_s)
    km_s[...] = jnp.zeros(km_s.shape, F32)
    km_s[0:n_blocks, :] = jnp.mean(kn.reshape(n_blocks, MOBA_BLOCK, LANES), axis=1)
    first, second = _head_masks()

    rows8 = _iota((8, seq), 0)
    own8 = _iota((8, seq), 1) // MOBA_BLOCK
    valid = rows8 < own8
    for msk, sel_s in ((first, sel0_s), (second, sel1_s)):
        gate = _dot3_nt(km_s[...], jnp.where(msk, qn, 0.0))[0:8, :]
        gm = jnp.where(valid, gate, -jnp.inf)
        rank = jnp.zeros((8, seq), F32)
        for n2 in range(n_blocks):
            g2 = gm[n2:n2 + 1, :]
            beats = jnp.logical_or(g2 > gm, jnp.logical_and(g2 == gm, n2 < rows8))
            rank = rank + jnp.where(jnp.logical_and(beats, n2 < own8), 1.0, 0.0)
        sel = jnp.where(jnp.logical_and(valid, rank < float(MOBA_TOPK)), 1.0, 0.0)
        sel = jnp.concatenate([sel, jnp.zeros((LANES - 8, seq), F32)], axis=0)
        sel_s[...] = sel.T

    qrows = MOBA_BLOCK
    lane_sq = _iota((qrows, LANES), 1)
    causal = _iota((qrows, MOBA_BLOCK), 1) <= _iota((qrows, MOBA_BLOCK), 0)
    second_block = _iota((qrows, 2 * MOBA_BLOCK), 1) >= MOBA_BLOCK
    first_head, _ = _head_masks()

    def qblock(own, carry):
        r0 = pl.multiple_of(own * qrows, qrows)
        qhs = [q_s[pl.ds(r0, qrows), :] for q_s in (q0_s, q1_s)]
        sels = [sel_s[pl.ds(r0, qrows), :] for sel_s in (sel0_s, sel1_s)]
        kb, vb = k_s[pl.ds(r0, MOBA_BLOCK), :], v_s[pl.ds(r0, MOBA_BLOCK), :]
        scores = [_dot_nt(qhs[hh], kb) for hh in range(2)]
        init, prs = [], []
        for hh in range(2):
            s = jnp.where(causal, scores[hh], NEG_BIG)
            m = jnp.max(s, axis=1, keepdims=True)
            pr = jnp.exp(s - m)
            init += [m, jnp.sum(pr, axis=1, keepdims=True), None]
            prs.append(pr.astype(BF16))
        for hh in range(2):
            init[3 * hh + 2] = _dot(prs[hh], vb)

        def kpair(g, st):
            c0 = pl.multiple_of(g * 2 * MOBA_BLOCK, 2 * MOBA_BLOCK)
            kb2, vb2 = k_s[pl.ds(c0, 2 * MOBA_BLOCK), :], v_s[pl.ds(c0, 2 * MOBA_BLOCK), :]
            scores = [_dot_nt(qhs[hh], kb2) for hh in range(2)]
            new, prs = [], []
            for hh in range(2):
                m, l, acc = st[3 * hh:3 * hh + 3]
                sel_a = jnp.sum(jnp.where(lane_sq == 2 * g, sels[hh], 0.0), axis=1, keepdims=True)
                sel_b = jnp.sum(jnp.where(lane_sq == 2 * g + 1, sels[hh], 0.0), axis=1, keepdims=True)
                keep = jnp.where(second_block, sel_b, sel_a) > 0.0
                s = jnp.where(keep, scores[hh], NEG_BIG)
                m_new = jnp.maximum(m, jnp.max(s, axis=1, keepdims=True))
                pr = jnp.exp(s - m_new)
                alpha = jnp.exp(m - m_new)
                new += [m_new, alpha * l + jnp.sum(pr, axis=1, keepdims=True), alpha * acc]
                prs.append(pr.astype(BF16))
            for hh in range(2):
                new[3 * hh + 2] = new[3 * hh + 2] + _dot(prs[hh], vb2)
            return tuple(new)

        st = lax.fori_loop(0, (own + 1) // 2, kpair, tuple(init))
        o_ref[0, pl.ds(r0, qrows), :] = jnp.where(first_head, st[2] / st[1], st[5] / st[4])
        return carry

    lax.fori_loop(0, seq // qrows, qblock, 0)


ROUTE_CHUNKS_PER_STEP = 4


def _oddeven_merge_sort_pairs(n):
    pairs = []

    def merge(lo, hi, r):
        step = 2 * r
        if step < hi - lo:
            merge(lo, hi, step)
            merge(lo + r, hi, step)
            pairs.extend((i, i + r) for i in range(lo + r, hi - r, step))
        else:
            pairs.append((lo, lo + r))

    def sort(lo, hi):
        if hi - lo >= 1:
            mid = lo + (hi - lo) // 2
            sort(lo, mid)
            sort(mid + 1, hi)
            merge(lo, hi, 1)

    sort(0, n - 1)
    return pairs


SUBLANES = 8


def _top16_rows(scores, n_rows, vals_refs, idx_refs):
    n_slabs = n_rows // SUBLANES
    sub = _iota((SUBLANES, LANES), 0)
    vals = [[s[SUBLANES * v:SUBLANES * (v + 1), :] for v in range(n_slabs)] for s in scores]
    idxs = [[sub + SUBLANES * v for v in range(n_slabs)] for _ in scores]
    for i, j in _oddeven_merge_sort_pairs(n_slabs):
        for va, ia in zip(vals, idxs):
            a, b = va[i], va[j]
            a_first = jnp.logical_or(a > b, jnp.logical_and(a == b, ia[i] < ia[j]))
            va[i], va[j] = jnp.maximum(a, b), jnp.minimum(a, b)
            ia[i], ia[j] = jnp.where(a_first, ia[i], ia[j]), jnp.where(a_first, ia[j], ia[i])
    for it in range(PEER_TOPK):
        for k, (va, ia) in enumerate(zip(vals, idxs)):
            m = jnp.max(va[0], axis=0, keepdims=True)
            pick = jnp.min(jnp.where(va[0] == m, ia[0], n_rows), axis=0, keepdims=True)
            vals_refs[k][it:it + 1, :] = m
            idx_refs[k][it:it + 1, :] = pick
            win = ia[0] == pick
            depth = PEER_TOPK - 1 - it
            for d in range(min(depth, n_slabs - 1)):
                va[d] = jnp.where(win, va[d + 1], va[d])
                ia[d] = jnp.where(win, ia[d + 1], ia[d])
            if depth >= n_slabs:
                va[n_slabs - 1] = jnp.where(win, -jnp.inf, va[n_slabs - 1])


def _peer_route_kernel(x_ref, g_ref, sh_ref, sc_ref, wqt_ref, sk_ref, h_ref, ids_ref, gts_ref,
                       q_s, val_s, idx_s, ids_s, gts_s):
    tt = x_ref.shape[0]
    n_chunks = tt // LANES
    per_step = ROUTE_CHUNKS_PER_STEP
    half = PEER_D_KEY // 2
    h = _adaln(x_ref[...], g_ref[...], sh_ref[0], sc_ref[0])
    h_ref[...] = h
    qt = _dot_nt(wqt_ref[...], h.astype(BF16))
    for c in range(n_chunks):
        q_s[c] = qt[:, c * LANES:(c + 1) * LANES]
    sk1, sk2 = sk_ref[0], sk_ref[1]
    col_id = _iota((PEER_TOPK, LANES), 0)
    sub = _iota((SUBLANES, LANES), 0)
    col_depth = jnp.zeros((SUBLANES, LANES), I32)
    for a in range(SUBLANES):
        col_depth = jnp.where(sub == a, PEER_CAND_COUNTS[a], col_depth)
    vals = [val_s.at[i] for i in range(2 * per_step)]
    idxs = [idx_s.at[i] for i in range(2 * per_step)]

    def body(step, carry):
        hh = step // (n_chunks // per_step)
        c0 = (step % (n_chunks // per_step)) * per_step
        q0 = pl.multiple_of(hh * PEER_D_KEY, PEER_D_KEY)
        scores = []
        for k in range(per_step):
            scores.append(_dot3(sk1, q_s[c0 + k, pl.ds(q0, half), :]))
            scores.append(_dot3(sk2, q_s[c0 + k, pl.ds(q0 + half, half), :]))
        _top16_rows(scores, PEER_N_KEYS, vals, idxs)
        state = []
        for k in range(per_step):
            v1, i1, v2, i2 = vals[2 * k], idxs[2 * k], vals[2 * k + 1], idxs[2 * k + 1]
            v1x, e1x = v1[0:SUBLANES, :], i1[0:SUBLANES, :] * PEER_N_KEYS
            xs = [jnp.where(col_depth > b, v1x + v2[b:b + 1, :], -jnp.inf) for b in range(PEER_TOPK)]
            ex = [e1x + i2[b:b + 1, :] for b in range(PEER_TOPK)]
            y = v1[SUBLANES:PEER_TOPK, :] + v2[0:1, :]
            ey = i1[SUBLANES:PEER_TOPK, :] * PEER_N_KEYS + i2[0:1, :]
            state.append([xs, ex, y, ey])
        for it in range(PEER_TOPK):
            for k in range(per_step):
                xs, ex, y, ey = state[k]
                heads = jnp.concatenate([xs[0], y], axis=0)
                m = jnp.max(heads, axis=0, keepdims=True)
                pick = jnp.min(jnp.where(heads == m, col_id, PEER_TOPK), axis=0, keepdims=True)
                win = col_id == pick
                eids = jnp.where(win, jnp.concatenate([ex[0], ey], axis=0), 0)
                idxs[2 * k][it:it + 1, :] = jnp.sum(eids, axis=0, keepdims=True)
                vals[2 * k][it:it + 1, :] = m
                win_x, win_y = win[0:SUBLANES, :], win[SUBLANES:PEER_TOPK, :]
                for dd in range(PEER_TOPK - 1 - it):
                    xs[dd] = jnp.where(win_x, xs[dd + 1], xs[dd])
                    ex[dd] = jnp.where(win_x, ex[dd + 1], ex[dd])
                state[k][2] = jnp.where(win_y, -jnp.inf, y)
        r0 = pl.multiple_of(hh * PEER_TOPK, PEER_TOPK)
        for k in range(per_step):
            top = vals[2 * k][...]
            e = jnp.exp(top - top[0:1, :])
            gts_s[c0 + k, pl.ds(r0, PEER_TOPK), :] = e / jnp.sum(e, axis=0, keepdims=True)
            ids_s[c0 + k, pl.ds(r0, PEER_TOPK), :] = idxs[2 * k][...] * PACK_ROWS
        return carry

    lax.fori_loop(0, PEER_HEADS * n_chunks // per_step, body, 0)
    for c in range(n_chunks):
        rows = slice(c * LANES, (c + 1) * LANES)
        ids_ref[pl.ds(c * LANES * PEER_SLOTS, LANES * PEER_SLOTS)] = ids_s[c].T.reshape(LANES * PEER_SLOTS)
        gts_ref[rows, :] = gts_s[c].T


def _peer_route(x2, g, shift, scale, wqt_bf, sub_keys, seq):
    t, d = x2.shape
    tt = 512
    per_b = seq // tt
    n_chunks = tt // LANES
    nq = wqt_bf.shape[0]
    out_blk = pl.BlockSpec((tt, PEER_SLOTS), lambda i: (i, 0))
    return pl.pallas_call(
        _peer_route_kernel,
        grid=(t // tt,),
        in_specs=[
            pl.BlockSpec((tt, d), lambda i: (i, 0)),
            pl.BlockSpec((1, d), lambda i: (0, 0)),
            pl.BlockSpec((1, 1, d), lambda i: (i // per_b, 0, 0)),
            pl.BlockSpec((1, 1, d), lambda i: (i // per_b, 0, 0)),
            pl.BlockSpec((nq, d), lambda i: (0, 0)),
            pl.BlockSpec(sub_keys.shape, lambda i: (0, 0, 0)),
        ],
        out_specs=[pl.BlockSpec((tt, d), lambda i: (i, 0)),
                   pl.BlockSpec((tt * PEER_SLOTS,), lambda i: (i,)), out_blk],
        out_shape=[jax.ShapeDtypeStruct((t, d), F32),
                   jax.ShapeDtypeStruct((t * PEER_SLOTS,), I32),
                   jax.ShapeDtypeStruct((t, PEER_SLOTS), F32)],
        scratch_shapes=[pltpu.VMEM((n_chunks, nq, LANES), F32),
                        pltpu.VMEM((2 * ROUTE_CHUNKS_PER_STEP, PEER_TOPK, LANES), F32),
                        pltpu.VMEM((2 * ROUTE_CHUNKS_PER_STEP, PEER_TOPK, LANES), I32),
                        pltpu.VMEM((n_chunks, PEER_SLOTS, LANES), I32),
                        pltpu.VMEM((n_chunks, PEER_SLOTS, LANES), F32)],
        compiler_params=_cparams(1),
        name="peer_route",
    )(x2, g, shift, scale, wqt_bf, sub_keys)


def _chunk_row(c):
    return 2 * (c % PACK_ROWS) + c // PACK_ROWS


def _bf16_bits(a):
    return lax.bitcast_convert_type(a.astype(BF16).astype(F32), U32)


def _pack_kernel(t_ref, o_ref):
    n = t_ref.shape[1]
    half = t_ref.shape[2] // 2
    for s in range(PACK_ROWS):
        lo = _bf16_bits(t_ref[0, :, s * LANES:(s + 1) * LANES])
        hi = _bf16_bits(t_ref[0, :, half + s * LANES:half + (s + 1) * LANES])
        o_ref[pl.ds(s, n, stride=PACK_ROWS), :] = (lo >> 16) | hi


def _pack_table(tables, layer):
    _, e, d = tables.shape
    assert d == 2 * PACK_ROWS * LANES
    be = 512
    return pl.pallas_call(
        _pack_kernel,
        grid=(e // be,),
        in_specs=[pl.BlockSpec((1, be, d), lambda i: (layer, i, 0))],
        out_specs=pl.BlockSpec((be * PACK_ROWS, LANES), lambda i: (i, 0)),
        out_shape=jax.ShapeDtypeStruct((e * PACK_ROWS, LANES), U32),
        compiler_params=_cparams(1),
        name="pack_expert_table",
    )(tables)


def _table_spec(rows):
    return pl.BlockSpec((rows, LANES), lambda i: (0, 0), pipeline_mode=pl.Buffered(1))


def _gelu_exact(a):
    return 0.5 * a * (1.0 + lax.erf(a * (2.0 ** -0.5)))


FEAT_CHUNKS = 8
SLOT_WIDTH = PEER_SLOTS * FEAT_CHUNKS


def _gather_rows(ids_ref, base, tbl_ref, slot):
    for j in range(PEER_SLOTS):
        if j % ID_VIEW == 0:
            ids_part = ids_ref.at[pl.ds(base + j, ID_VIEW)]
        row0 = pl.multiple_of(ids_part[j % ID_VIEW], PACK_ROWS)
        slot[PACK_ROWS * j:PACK_ROWS * (j + 1), :] = tbl_ref[pl.ds(row0, PACK_ROWS), :]


def _pipelined_tokens(tt, ids_ref, next_ids_ref, tbl_ref, slots, compute):
    group = len(slots)
    n_steps = tt // group
    tiles = group // SUBLANES

    def compute_group(q):
        for k in range(group):
            compute(group * q + k, tiles * q + k // SUBLANES, k % SUBLANES, slots[k])

    @pl.when(pl.program_id(0) == 0)
    def _():
        for k in range(group):
            _gather_rows(ids_ref, k * PEER_SLOTS, tbl_ref, slots[k])

    def step(q, carry):
        compute_group(q)
        for k in range(group):
            _gather_rows(ids_ref, (group * (q + 1) + k) * PEER_SLOTS, tbl_ref, slots[k])
        return carry

    lax.fori_loop(0, n_steps - 1, step, 0)
    compute_group(n_steps - 1)
    for k in range(group):
        _gather_rows(next_ids_ref, k * PEER_SLOTS, tbl_ref, slots[k])


def _chunk_diag():
    return (_iota((FEAT_CHUNKS, SLOT_WIDTH), 1) % FEAT_CHUNKS) == _iota((FEAT_CHUNKS, SLOT_WIDTH), 0)


def _peer_u_kernel(ids_ref, next_ids_ref, h_ref, g_ref, tbl_ref, coef_ref, *scratch):
    slots, (hx_s, rs_s) = scratch[:N_SLOTS], scratch[N_SLOTS:]
    tt = h_ref.shape[0]
    for c in range(FEAT_CHUNKS):
        hx_s[pl.ds(_chunk_row(c), tt, stride=FEAT_CHUNKS), :] = h_ref[:, c * LANES:(c + 1) * LANES]
    diag = _chunk_diag()

    def compute(t, tile, sub, slot):
        rows = pltpu.bitcast(slot[...], BF16)
        x8 = hx_s[pl.ds(pl.multiple_of(t * FEAT_CHUNKS, FEAT_CHUNKS), FEAT_CHUNKS), :]
        part = _dot_nt(x8.astype(BF16), rows)
        rs_s[tile, sub:sub + 1, :] = jnp.sum(jnp.where(diag, part, 0.0), axis=0, keepdims=True)

    _pipelined_tokens(tt, ids_ref, next_ids_ref, tbl_ref, slots, compute)
    group = jnp.where(_iota((SLOT_WIDTH, PEER_SLOTS), 0) // FEAT_CHUNKS == _iota((SLOT_WIDTH, PEER_SLOTS), 1),
                      1.0, 0.0).astype(BF16)
    hi, lo = _split_bf16(rs_s[...].reshape(tt, SLOT_WIDTH))
    act = _dot(hi, group) + _dot(lo, group)
    coef_ref[...] = g_ref[...] * _gelu_exact(act)


N_SLOTS = 16
ID_VIEW = 16
PEER_TOKEN_TILE = 512


def _next_group_ids_spec(t, tt):
    groups_per_tile = tt // N_SLOTS
    last_tile = t // tt - 1
    return pl.BlockSpec((N_SLOTS * PEER_SLOTS,),
                        lambda i: (jnp.minimum(i + 1, last_tile) * groups_per_tile,),
                        memory_space=pltpu.SMEM)


def _slot_scratch():
    return [pltpu.VMEM((PEER_SLOTS * PACK_ROWS, LANES), U32)] * N_SLOTS


def _peer_u(ids_flat, h, gates, table, tt):
    t, d = h.shape
    return pl.pallas_call(
        _peer_u_kernel,
        grid=(t // tt,),
        in_specs=[
            pl.BlockSpec((tt * PEER_SLOTS,), lambda i: (i,), memory_space=pltpu.SMEM),
            _next_group_ids_spec(t, tt),
            pl.BlockSpec((tt, d), lambda i: (i, 0)),
            pl.BlockSpec((tt, PEER_SLOTS), lambda i: (i, 0)),
            _table_spec(table.shape[0]),
        ],
        out_specs=pl.BlockSpec((tt, PEER_SLOTS), lambda i: (i, 0)),
        out_shape=jax.ShapeDtypeStruct((t, PEER_SLOTS), F32),
        scratch_shapes=_slot_scratch() + [
                        pltpu.VMEM((tt * FEAT_CHUNKS, LANES), F32),
                        pltpu.VMEM((tt // SUBLANES, SUBLANES, SLOT_WIDTH), F32)],
        compiler_params=_cparams(1, TABLE_VMEM_LIMIT),
        name="peer_expert_in",
    )(ids_flat, ids_flat, h, gates, table)


def _peer_v_kernel(ids_ref, next_ids_ref, coef_ref, x_ref, gate_ref, tbl_ref, o_ref, *scratch):
    slots, (ce_hi_s, ce_lo_s, res_s) = scratch[:N_SLOTS], scratch[N_SLOTS:]
    tt = x_ref.shape[0]
    spread = jnp.where(_iota((PEER_SLOTS, SLOT_WIDTH), 1) // FEAT_CHUNKS == _iota((PEER_SLOTS, SLOT_WIDTH), 0),
                       1.0, 0.0).astype(BF16)
    hi, lo = _split_bf16(coef_ref[...])
    ce_hi_s[...] = _dot(hi, spread).reshape(ce_hi_s.shape)
    ce_lo_s[...] = _dot(lo, spread).reshape(ce_lo_s.shape)
    diag = _chunk_diag()

    def compute(t, tile, sub, slot):
        rows = pltpu.bitcast(slot[...], BF16)
        a_hi = jnp.where(diag, ce_hi_s[tile, sub:sub + 1, :], 0.0)
        a_lo = jnp.where(diag, ce_lo_s[tile, sub:sub + 1, :], 0.0)
        both = _dot(jnp.concatenate([a_hi, a_lo], axis=0).astype(BF16), rows)
        r0 = pl.multiple_of(t * FEAT_CHUNKS, FEAT_CHUNKS)
        res_s[pl.ds(r0, FEAT_CHUNKS), :] = both[0:FEAT_CHUNKS, :] + both[FEAT_CHUNKS:2 * FEAT_CHUNKS, :]

    _pipelined_tokens(tt, ids_ref, next_ids_ref, tbl_ref, slots, compute)
    for c in range(FEAT_CHUNKS):
        cols = slice(c * LANES, (c + 1) * LANES)
        y = res_s[pl.ds(_chunk_row(c), tt, stride=FEAT_CHUNKS), :]
        o_ref[:, cols] = x_ref[:, cols] + gate_ref[0][:, cols] * y


def _peer_v(ids_flat, coef, x2, gate, table, seq, tt):
    t, d = x2.shape
    per_b = seq // tt
    blk = pl.BlockSpec((tt, d), lambda i: (i, 0))
    return pl.pallas_call(
        _peer_v_kernel,
        grid=(t // tt,),
        in_specs=[
            pl.BlockSpec((tt * PEER_SLOTS,), lambda i: (i,), memory_space=pltpu.SMEM),
            _next_group_ids_spec(t, tt),
            pl.BlockSpec((tt, PEER_SLOTS), lambda i: (i, 0)),
            blk,
            pl.BlockSpec((1, 1, d), lambda i: (i // per_b, 0, 0)),
            _table_spec(table.shape[0]),
        ],
        out_specs=blk,
        out_shape=jax.ShapeDtypeStruct((t, d), F32),
        scratch_shapes=_slot_scratch() + [
                        pltpu.VMEM((tt // SUBLANES, SUBLANES, SLOT_WIDTH), F32),
                        pltpu.VMEM((tt // SUBLANES, SUBLANES, SLOT_WIDTH), F32),
                        pltpu.VMEM((tt * FEAT_CHUNKS, LANES), F32)],
        compiler_params=_cparams(1, TABLE_VMEM_LIMIT),
        name="peer_expert_out",
    )(ids_flat, ids_flat, coef, x2, gate, table)


def _peer_ffn(x2, g, shift, scale, gate, wq, sub_keys, table_u, table_v, seq):
    t, d = x2.shape
    tt = PEER_TOKEN_TILE
    h, ids, gates = _peer_route(x2, g, shift, scale, wq.T.astype(BF16), sub_keys, seq)
    ids_flat = ids
    coef = _peer_u(ids_flat, h, gates, table_u, tt)
    return _peer_v(ids_flat, coef, x2, gate, table_v, seq, tt)


def _rope_tables(seq):
    half = HEAD_DIM // 2
    inv_freq = ROPE_THETA ** (-jnp.arange(half, dtype=F32) / half)
    ang = jnp.arange(seq).astype(F32)[:, None] * inv_freq[None, :]
    reps = LANES // half
    return jnp.tile(jnp.cos(ang), (1, reps)), jnp.tile(jnp.sin(ang), (1, reps))


def _two_heads(gain):
    return jnp.tile(gain.reshape(1, HEAD_DIM), (1, LANES // HEAD_DIM))


def kernel(x, c, ada_w, ada_b, norm_mix_g, norm_ffn_g, w_in_ab, w_out_ab, sinks_a, qnorm_a, knorm_a,
           w_in_cd, w_out_cd, qnorm_c, knorm_c, qnorm_d, knorm_d, peer_wq, peer_subkeys, peer_u, peer_v):
    b, seq, d = x.shape
    depth = ada_w.shape[0]
    t = b * seq
    cos, sin = _rope_tables(seq)
    mod = _modulation(c, ada_w, ada_b)
    x2 = x.reshape(t, d)
    for layer in range(depth):
        shift_m, scale_m, gate_m, shift_f, scale_f, gate_f = [
            m.reshape(b, 1, d) for m in jnp.split(mod[layer], 6, axis=-1)]
        g_mix = norm_mix_g[layer].reshape(1, d)
        i = layer // 2
        if layer % 2 == 0:
            proj = _norm_proj(x2, g_mix, shift_m, scale_m, w_in_ab[i].astype(BF16), seq)
            proj = proj.reshape(b, seq, -1)
            ya = _swa_attention(proj, sinks_a[i], cos, sin, _two_heads(qnorm_a[i]), _two_heads(knorm_a[i]))
            b_col = (A_Q_HEADS + 2 * A_KV_HEADS) * HEAD_DIM // LANES
            yb = _stick_attention(proj, b_col)
            w_out = w_out_ab[i]
        else:
            proj = _norm_proj(x2, g_mix, shift_m, scale_m, w_in_cd[i].astype(BF16), seq)
            proj = proj.reshape(b, seq, -1)
            ya = _qkv_attention_call(_dilated_kernel, "dilated_attention", proj, 0, C_HEADS, cos, sin,
                                     _two_heads(qnorm_c[i]), _two_heads(knorm_c[i]))
            d_col = 3 * C_HEADS * HEAD_DIM // LANES
            yb = _qkv_attention_call(_moba_kernel, "moba_attention", proj, d_col, D_HEADS, cos, sin,
                                     _two_heads(qnorm_d[i]), _two_heads(knorm_d[i]),
                                     extra_scratch=(pltpu.VMEM((LANES, LANES), F32),
                                                    pltpu.VMEM((seq, LANES), F32),
                                                    pltpu.VMEM((seq, LANES), F32)))
            w_out = w_out_cd[i]
        x2 = _out_proj(x2, ya.reshape(t, -1), yb.reshape(t, -1), w_out.astype(BF16), gate_m, seq)
        x2 = _peer_ffn(x2, norm_ffn_g[layer].reshape(1, d), shift_f, scale_f, gate_f,
                       peer_wq[layer], peer_subkeys[layer],
                       _pack_table(peer_u, layer), _pack_table(peer_v, layer), seq)
    return x2.reshape(b, seq, d)
```

```python
import jax
import jax.numpy as jnp
from jax import lax
from jax.experimental import pallas as pl
from jax.experimental.pallas import tpu as pltpu

F32 = jnp.float32
BF16 = jnp.bfloat16
I32 = jnp.int32
U32 = jnp.uint32

HEAD_DIM = 64
ROPE_THETA = 10000.0
NORM_EPS = 1e-6
LANES = 128
QUERY_BLOCK = 128
A_Q_HEADS, A_KV_HEADS = 8, 2
B_HEADS = C_HEADS = D_HEADS = 8
C_PATTERNS = ((128, 1), (512, 4), (2048, 16))
MOBA_BLOCK, MOBA_TOPK = 256, 3
PEER_HEADS, PEER_N_KEYS, PEER_TOPK, PEER_D_KEY = 8, 128, 16, 256
PEER_SLOTS = PEER_HEADS * PEER_TOPK
NEG_BIG = -1e30
PEER_CAND_COUNTS = tuple(PEER_TOPK // (a + 1) for a in range(PEER_TOPK))
PACK_ROWS = 4
MIB = 1024 * 1024
V7X_VMEM_BYTES = 64 * MIB
STREAM_VMEM_LIMIT = 40 * MIB
TABLE_VMEM_LIMIT = V7X_VMEM_BYTES - 8 * MIB


def _cparams(n_axes, vmem_bytes=STREAM_VMEM_LIMIT):
    return pltpu.CompilerParams(
        dimension_semantics=("arbitrary",) * n_axes,
        vmem_limit_bytes=vmem_bytes)


def _split_bf16(a):
    hi = a.astype(BF16)
    lo = (a - hi.astype(F32)).astype(BF16)
    return hi, lo


def _dot(a, b):
    return jnp.dot(a, b, preferred_element_type=F32)


def _dot_nt(a, b):
    return lax.dot_general(a, b, (((1,), (1,)), ((), ())), preferred_element_type=F32)


def _dot3(a, b):
    ah, al = _split_bf16(a)
    bh, bl = _split_bf16(b)
    return _dot(ah, bh) + _dot(ah, bl) + _dot(al, bh)


def _dot3_nt(a, b):
    ah, al = _split_bf16(a)
    bh, bl = _split_bf16(b)
    return _dot_nt(ah, bh) + _dot_nt(ah, bl) + _dot_nt(al, bh)


def _iota(shape, dim):
    return lax.broadcasted_iota(I32, shape, dim)


def _mod_kernel(c_ref, w_ref, b_ref, o_ref):
    c = c_ref[...]
    cond = c * jax.nn.sigmoid(c)
    o_ref[0] = _dot3(cond, w_ref[0]) + b_ref[0]


def _modulation(c, ada_w, ada_b):
    depth, d, n = ada_w.shape
    b = c.shape[0]
    tn = 1024
    return pl.pallas_call(
        _mod_kernel,
        grid=(depth, n // tn),
        in_specs=[
            pl.BlockSpec((b, d), lambda l, j: (0, 0)),
            pl.BlockSpec((1, d, tn), lambda l, j: (l, 0, j)),
            pl.BlockSpec((1, 1, tn), lambda l, j: (l, 0, j)),
        ],
        out_specs=pl.BlockSpec((1, b, tn), lambda l, j: (l, 0, j)),
        out_shape=jax.ShapeDtypeStruct((depth, b, n), F32),
        compiler_params=_cparams(2),
        name="adaln_modulation",
    )(c, ada_w, ada_b.reshape(depth, 1, n))


def _adaln(x, g, shift, scale):
    ms = jnp.mean(x * x, axis=-1, keepdims=True)
    y = x * lax.rsqrt(ms + NORM_EPS) * g
    return y * (1.0 + scale) + shift


def _norm_proj_kernel(x_ref, g_ref, sh_ref, sc_ref, w_ref, o_ref):
    h = _adaln(x_ref[...], g_ref[...], sh_ref[0], sc_ref[0])
    o_ref[...] = _dot(h.astype(BF16), w_ref[...])


def _norm_proj(x2, g, shift, scale, w_bf, seq):
    t, d = x2.shape
    n = w_bf.shape[1]
    tt = 512
    per_b = seq // tt
    return pl.pallas_call(
        _norm_proj_kernel,
        grid=(t // tt,),
        in_specs=[
            pl.BlockSpec((tt, d), lambda i: (i, 0)),
            pl.BlockSpec((1, d), lambda i: (0, 0)),
            pl.BlockSpec((1, 1, d), lambda i: (i // per_b, 0, 0)),
            pl.BlockSpec((1, 1, d), lambda i: (i // per_b, 0, 0)),
            pl.BlockSpec((d, n), lambda i: (0, 0)),
        ],
        out_specs=pl.BlockSpec((tt, n), lambda i: (i, 0)),
        out_shape=jax.ShapeDtypeStruct((t, n), F32),
        compiler_params=_cparams(1),
        name="adaln_in_proj",
    )(x2, g, shift, scale, w_bf)


def _out_proj_kernel(x_ref, ya_ref, yb_ref, w_ref, gate_ref, o_ref):
    half = ya_ref.shape[1]
    y = _dot(ya_ref[...].astype(BF16), w_ref[0:half, :])
    y = y + _dot(yb_ref[...].astype(BF16), w_ref[half:2 * half, :])
    o_ref[...] = x_ref[...] + gate_ref[0] * y


def _out_proj(x2, ya, yb, w_bf, gate, seq):
    t, d = x2.shape
    half = ya.shape[1]
    tt = 512
    per_b = seq // tt
    return pl.pallas_call(
        _out_proj_kernel,
        grid=(t // tt,),
        in_specs=[
            pl.BlockSpec((tt, d), lambda i: (i, 0)),
            pl.BlockSpec((tt, half), lambda i: (i, 0)),
            pl.BlockSpec((tt, half), lambda i: (i, 0)),
            pl.BlockSpec((2 * half, d), lambda i: (0, 0)),
            pl.BlockSpec((1, 1, d), lambda i: (i // per_b, 0, 0)),
        ],
        out_specs=pl.BlockSpec((tt, d), lambda i: (i, 0)),
        out_shape=jax.ShapeDtypeStruct((t, d), F32),
        compiler_params=_cparams(1),
        name="mixer_out_proj",
    )(x2, ya, yb, w_bf, gate)


def _lane_row():
    return _iota((1, LANES), 1)


def _head_segment_ones():
    r = _iota((LANES, LANES), 0) // HEAD_DIM
    c = _iota((LANES, LANES), 1) // HEAD_DIM
    return jnp.where(r == c, 1.0, 0.0).astype(BF16)


def _headnorm_rope(a, g, cos, sin):
    hi, lo = _split_bf16(a * a)
    seg = _head_segment_ones()
    ms = (_dot(hi, seg) + _dot(lo, seg)) * (1.0 / HEAD_DIM)
    y = a * lax.rsqrt(ms + NORM_EPS) * g
    half = HEAD_DIM // 2
    upper = pltpu.roll(y, LANES - half, axis=1)
    lower = pltpu.roll(y, half, axis=1)
    first_half = (_lane_row() % HEAD_DIM) < half
    rot = jnp.where(first_half, -upper, lower)
    return y * cos + rot * sin


def _head_masks():
    lane = _lane_row()
    return lane < HEAD_DIM, lane >= HEAD_DIM


def _attn_specs(seq, qcol, kcol, vcol, kv_shared):
    blk = (1, seq, LANES)
    q_spec = pl.BlockSpec(blk, lambda b, p: (b, 0, qcol + p))
    if kv_shared:
        k_spec = pl.BlockSpec(blk, lambda b, p: (b, 0, kcol))
        v_spec = pl.BlockSpec(blk, lambda b, p: (b, 0, vcol))
    else:
        k_spec = pl.BlockSpec(blk, lambda b, p: (b, 0, kcol + p))
        v_spec = pl.BlockSpec(blk, lambda b, p: (b, 0, vcol + p))
    return q_spec, k_spec, v_spec


def _row_spec(seq):
    return pl.BlockSpec((seq, LANES), lambda b, p: (0, 0))


def _gain_spec():
    return pl.BlockSpec((1, LANES), lambda b, p: (0, 0))


def _store_heads(o_ref, r0, outs):
    first, _ = _head_masks()
    o_ref[0, pl.ds(r0, QUERY_BLOCK), :] = jnp.where(first, outs[0], outs[1])


SWA_QBLOCKS_PER_STEP = 4


def _swa_kernel(sinks_ref, q_ref, k_ref, v_ref, cos_ref, sin_ref, gq_ref, gk_ref, o_ref,
                q0_s, q1_s, k_s, v_s):
    p = pl.program_id(1)
    seq = q_ref.shape[1]
    cos, sin = cos_ref[...], sin_ref[...]
    first, second = _head_masks()
    qn = _headnorm_rope(q_ref[0], gq_ref[...], cos, sin) * (HEAD_DIM ** -0.5)
    q0_s[...] = jnp.where(first, qn, 0.0).astype(BF16)
    q1_s[...] = jnp.where(second, qn, 0.0).astype(BF16)
    pairs_per_kv = (A_Q_HEADS // A_KV_HEADS) // 2
    keep = jnp.logical_xor(first, (p // pairs_per_kv) == 1)
    kn = _headnorm_rope(k_ref[0], gk_ref[...], cos, sin)
    k_s[...] = jnp.where(keep, kn, pltpu.roll(kn, HEAD_DIM, axis=1)).astype(BF16)
    v = v_ref[0]
    v_s[...] = jnp.where(keep, v, pltpu.roll(v, HEAD_DIM, axis=1)).astype(BF16)

    qi = _iota((QUERY_BLOCK, QUERY_BLOCK), 0)
    ki = _iota((QUERY_BLOCK, QUERY_BLOCK), 1)

    mask_c = ki <= qi
    per_step = min(SWA_QBLOCKS_PER_STEP, seq // QUERY_BLOCK)

    def qstep(step, carry):
        chains, scores = [], []
        for u in range(per_step):
            i = step * per_step + u
            r0 = pl.multiple_of(i * QUERY_BLOCK, QUERY_BLOCK)
            rp = pl.multiple_of(jnp.maximum(i - 1, 0) * QUERY_BLOCK, QUERY_BLOCK)
            kc, kp = k_s[pl.ds(r0, QUERY_BLOCK), :], k_s[pl.ds(rp, QUERY_BLOCK), :]
            vc, vp = v_s[pl.ds(r0, QUERY_BLOCK), :], v_s[pl.ds(rp, QUERY_BLOCK), :]
            mask_p = jnp.logical_and(ki > qi, i > 0)
            chains.append((r0, vc, vp, mask_p))
            for q_s in (q0_s, q1_s):
                qh = q_s[pl.ds(r0, QUERY_BLOCK), :]
                scores.append((_dot_nt(qh, kc), _dot_nt(qh, kp)))
        probs, denoms = [], []
        for n, (s_cur, s_prev) in enumerate(scores):
            mask_p = chains[n // 2][3]
            sc = jnp.where(mask_c, s_cur, NEG_BIG)
            sp = jnp.where(mask_p, s_prev, NEG_BIG)
            sink = sinks_ref[2 * p + n % 2]
            m = jnp.maximum(jnp.max(sc, axis=1, keepdims=True), jnp.max(sp, axis=1, keepdims=True))
            m = jnp.maximum(m, sink)
            ec, ep = jnp.exp(sc - m), jnp.exp(sp - m)
            denoms.append(jnp.sum(ec, axis=1, keepdims=True) + jnp.sum(ep, axis=1, keepdims=True)
                          + jnp.exp(sink - m))
            probs.append((ec.astype(BF16), ep.astype(BF16)))
        for u, (r0, vc, vp, _) in enumerate(chains):
            outs = [(_dot(probs[2 * u + hh][0], vc) + _dot(probs[2 * u + hh][1], vp)) / denoms[2 * u + hh]
                    for hh in range(2)]
            _store_heads(o_ref, r0, outs)
        return carry

    lax.fori_loop(0, seq // (QUERY_BLOCK * per_step), qstep, 0)


def _swa_attention(proj, sinks, cos, sin, gq, gk):
    b, seq, _ = proj.shape
    n_pairs = A_Q_HEADS // 2
    kcol = A_Q_HEADS * HEAD_DIM // LANES
    vcol = kcol + A_KV_HEADS * HEAD_DIM // LANES
    q_spec, k_spec, v_spec = _attn_specs(seq, 0, kcol, vcol, True)
    return pl.pallas_call(
        _swa_kernel,
        grid=(b, n_pairs),
        in_specs=[pl.BlockSpec(memory_space=pltpu.SMEM), q_spec, k_spec, v_spec,
                  _row_spec(seq), _row_spec(seq), _gain_spec(), _gain_spec()],
        out_specs=pl.BlockSpec((1, seq, LANES), lambda b_, p: (b_, 0, p)),
        out_shape=jax.ShapeDtypeStruct((b, seq, n_pairs * LANES), F32),
        scratch_shapes=[pltpu.VMEM((seq, LANES), BF16)] * 4,
        compiler_params=_cparams(2),
        name="swa_gqa_attention",
    )(sinks, proj, proj, proj, cos, sin, gq, gk)


STICK_GROUP = 4
STICK_QUERY_ROWS = 256


def _stick_kernel(q_ref, k_ref, v_ref, o_ref, k_s, v_s):
    seq = q_ref.shape[1]
    k_s[...] = k_ref[0].astype(BF16)
    v_s[...] = v_ref[0].astype(BF16)
    first, second = _head_masks()
    qr = min(STICK_QUERY_ROWS, seq)
    kw = QUERY_BLOCK
    key_minus_query = _iota((qr, kw), 1) - _iota((qr, kw), 0)
    wr = _iota((2 * kw, 2 * kw), 0) % kw
    wc = _iota((2 * kw, 2 * kw), 1)
    suffix_w = jnp.where(jnp.logical_or(wc >= kw, wr > wc), 1.0, 0.0).astype(BF16)

    def qblock(i, carry):
        r0 = pl.multiple_of(i * qr, qr)
        q = q_ref[0, pl.ds(r0, qr), :] * (HEAD_DIM ** -0.5)
        qhs = [jnp.where(msk, q, 0.0).astype(BF16) for msk in (first, second)]
        n_blocks = (r0 + qr) // kw

        def kgroup(g, st):
            accs, laters = [st[0], st[1]], [st[2], st[3]]
            chains = [(u, hh) for u in range(STICK_GROUP) for hh in range(2)]
            vbs, pasts, zs = [], [], {}
            for u in range(STICK_GROUP):
                j = n_blocks - 1 - (g * STICK_GROUP + u)
                live = j >= 0
                c0 = pl.multiple_of(jnp.maximum(j, 0) * kw, kw)
                kb = k_s[pl.ds(c0, kw), :]
                vbs.append(v_s[pl.ds(c0, kw), :])
                pasts.append(jnp.logical_and(key_minus_query < r0 - c0, live))
                for hh in range(2):
                    zs[u, hh] = _dot_nt(qhs[hh], kb)
            logit, sums = {}, {}
            for u, hh in chains:
                z = zs[u, hh]
                sp = jnp.maximum(z, 0.0) + jnp.log(1.0 + jnp.exp(-jnp.abs(z)))
                neg_log_keep = jnp.where(pasts[u], sp, 0.0)
                logit[u, hh] = z - sp
                hi, lo = _split_bf16(neg_log_keep)
                sums[u, hh] = _dot(jnp.concatenate([hi, lo], axis=1), suffix_w)
            ws = {}
            for u, hh in chains:
                inner, total = sums[u, hh][:, :kw], sums[u, hh][:, kw:]
                ws[u, hh] = jnp.where(pasts[u], jnp.exp(logit[u, hh] - inner - laters[hh]), 0.0).astype(BF16)
                laters[hh] = laters[hh] + total
            for u, hh in chains:
                accs[hh] = accs[hh] + _dot(ws[u, hh], vbs[u])
            return accs[0], accs[1], laters[0], laters[1]

        zero = jnp.zeros((qr, LANES), F32)
        n_groups = (n_blocks + STICK_GROUP - 1) // STICK_GROUP
        st = lax.fori_loop(0, n_groups, kgroup, (zero, zero, zero, zero))
        o_ref[0, pl.ds(r0, qr), :] = jnp.where(first, st[0], st[1])
        return carry

    lax.fori_loop(0, seq // qr, qblock, 0)


def _stick_attention(proj, qcol):
    b, seq, _ = proj.shape
    n_pairs = B_HEADS // 2
    q_spec, k_spec, v_spec = _attn_specs(seq, qcol, qcol + n_pairs, qcol + 2 * n_pairs, False)
    return pl.pallas_call(
        _stick_kernel,
        grid=(b, n_pairs),
        in_specs=[q_spec, k_spec, v_spec],
        out_specs=pl.BlockSpec((1, seq, LANES), lambda b_, p: (b_, 0, p)),
        out_shape=jax.ShapeDtypeStruct((b, seq, n_pairs * LANES), F32),
        scratch_shapes=[pltpu.VMEM((seq, LANES), BF16)] * 2,
        compiler_params=_cparams(2),
        name="stick_breaking_attention",
    )(proj, proj, proj)


def _prep_qkv(q_ref, k_ref, v_ref, cos_ref, sin_ref, gq_ref, gk_ref, q0_s, q1_s, k_s, v_s):
    cos, sin = cos_ref[...], sin_ref[...]
    first, second = _head_masks()
    qn = _headnorm_rope(q_ref[0], gq_ref[...], cos, sin) * (HEAD_DIM ** -0.5)
    q0_s[...] = jnp.where(first, qn, 0.0).astype(BF16)
    q1_s[...] = jnp.where(second, qn, 0.0).astype(BF16)
    kn = _headnorm_rope(k_ref[0], gk_ref[...], cos, sin)
    k_s[...] = kn.astype(BF16)
    v_s[...] = v_ref[0].astype(BF16)
    return qn, kn


DILATED_KEY_TILE = 512
DILATED_QUERY_ROWS = 256


def _dilated_kernel(q_ref, k_ref, v_ref, cos_ref, sin_ref, gq_ref, gk_ref, o_ref,
                    q0_s, q1_s, k_s, v_s):
    seq = q_ref.shape[1]
    _prep_qkv(q_ref, k_ref, v_ref, cos_ref, sin_ref, gq_ref, gk_ref, q0_s, q1_s, k_s, v_s)
    kt = min(DILATED_KEY_TILE, seq)
    qr = min(DILATED_QUERY_ROWS, seq)
    qk = _iota((qr, kt), 0) - _iota((qr, kt), 1)
    on_stride = [jnp.where((qk & (dil - 1)) == 0, 1.0, 0.0) for _, dil in C_PATTERNS]
    first_head, _ = _head_masks()

    def qblock(i, carry):
        r0 = pl.multiple_of(i * qr, qr)
        qhs = [q_s[pl.ds(r0, qr), :] for q_s in (q0_s, q1_s)]

        def ktile(g, st):
            c0 = pl.multiple_of(g * kt, kt)
            d = (r0 - c0) + qk
            count = jnp.zeros(d.shape, F32)
            for (window, _), stride_ok in zip(C_PATTERNS, on_stride):
                count = count + jnp.where(d <= window, stride_ok, 0.0)
            count = jnp.where(d >= 0, count, 0.0)
            kb, vb = k_s[pl.ds(c0, kt), :], v_s[pl.ds(c0, kt), :]
            scores = [_dot_nt(qhs[hh], kb) for hh in range(2)]
            new, prs = [], []
            for hh in range(2):
                m, l, acc = st[3 * hh:3 * hh + 3]
                s = jnp.where(count > 0.0, scores[hh], NEG_BIG)
                m_new = jnp.maximum(m, jnp.max(s, axis=1, keepdims=True))
                pr = count * jnp.exp(s - m_new)
                alpha = jnp.exp(m - m_new)
                new += [m_new, alpha * l + jnp.sum(pr, axis=1, keepdims=True), alpha * acc]
                prs.append(pr.astype(BF16))
            for hh in range(2):
                new[3 * hh + 2] = new[3 * hh + 2] + _dot(prs[hh], vb)
            return tuple(new)

        init = (jnp.full((qr, 1), NEG_BIG, F32), jnp.zeros((qr, 1), F32),
                jnp.zeros((qr, LANES), F32)) * 2
        st = lax.fori_loop(0, (r0 + qr + kt - 1) // kt, ktile, init)
        o_ref[0, pl.ds(r0, qr), :] = jnp.where(first_head, st[2] / st[1], st[5] / st[4])
        return carry

    lax.fori_loop(0, seq // qr, qblock, 0)


def _qkv_attention_call(kernel, name, proj, qcol, n_heads, cos, sin, gq, gk, extra_scratch=()):
    b, seq, _ = proj.shape
    n_pairs = n_heads // 2
    q_spec, k_spec, v_spec = _attn_specs(seq, qcol, qcol + n_pairs, qcol + 2 * n_pairs, False)
    return pl.pallas_call(
        kernel,
        grid=(b, n_pairs),
        in_specs=[q_spec, k_spec, v_spec, _row_spec(seq), _row_spec(seq), _gain_spec(), _gain_spec()],
        out_specs=pl.BlockSpec((1, seq, LANES), lambda b_, p: (b_, 0, p)),
        out_shape=jax.ShapeDtypeStruct((b, seq, n_pairs * LANES), F32),
        scratch_shapes=[pltpu.VMEM((seq, LANES), BF16)] * 4 + list(extra_scratch),
        compiler_params=_cparams(2),
        name=name,
    )(proj, proj, proj, cos, sin, gq, gk)


def _moba_kernel(q_ref, k_ref, v_ref, cos_ref, sin_ref, gq_ref, gk_ref, o_ref,
                 q0_s, q1_s, k_s, v_s, km_s, sel0_s, sel1_s):
    seq = q_ref.shape[1]
    n_blocks = seq // MOBA_BLOCK
    qn, kn = _prep_qkv(q_ref, k_ref, v_ref, cos_ref, sin_ref, gq_ref, gk_ref, q0_s, q1_s, k_s, v_s)
    km_s[...] = jnp.zeros(km_s.shape, F32)
    km_s[0:n_blocks, :] = jnp.mean(kn.reshape(n_blocks, MOBA_BLOCK, LANES), axis=1)
    first, second = _head_masks()

    rows8 = _iota((8, seq), 0)
    own8 = _iota((8, seq), 1) // MOBA_BLOCK
    valid = rows8 < own8
    for msk, sel_s in ((first, sel0_s), (second, sel1_s)):
        gate = _dot3_nt(km_s[...], jnp.where(msk, qn, 0.0))[0:8, :]
        gm = jnp.where(valid, gate, -jnp.inf)
        rank = jnp.zeros((8, seq), F32)
        for n2 in range(n_blocks):
            g2 = gm[n2:n2 + 1, :]
            beats = jnp.logical_or(g2 > gm, jnp.logical_and(g2 == gm, n2 < rows8))
            rank = rank + jnp.where(jnp.logical_and(beats, n2 < own8), 1.0, 0.0)
        sel = jnp.where(jnp.logical_and(valid, rank < float(MOBA_TOPK)), 1.0, 0.0)
        sel = jnp.concatenate([sel, jnp.zeros((LANES - 8, seq), F32)], axis=0)
        sel_s[...] = sel.T

    qrows = MOBA_BLOCK
    lane_sq = _iota((qrows, LANES), 1)
    causal = _iota((qrows, MOBA_BLOCK), 1) <= _iota((qrows, MOBA_BLOCK), 0)
    second_block = _iota((qrows, 2 * MOBA_BLOCK), 1) >= MOBA_BLOCK
    first_head, _ = _head_masks()

    def qblock(own, carry):
        r0 = pl.multiple_of(own * qrows, qrows)
        qhs = [q_s[pl.ds(r0, qrows), :] for q_s in (q0_s, q1_s)]
        sels = [sel_s[pl.ds(r0, qrows), :] for sel_s in (sel0_s, sel1_s)]
        kb, vb = k_s[pl.ds(r0, MOBA_BLOCK), :], v_s[pl.ds(r0, MOBA_BLOCK), :]
        scores = [_dot_nt(qhs[hh], kb) for hh in range(2)]
        init, prs = [], []
        for hh in range(2):
            s = jnp.where(causal, scores[hh], NEG_BIG)
            m = jnp.max(s, axis=1, keepdims=True)
            pr = jnp.exp(s - m)
            init += [m, jnp.sum(pr, axis=1, keepdims=True), None]
            prs.append(pr.astype(BF16))
        for hh in range(2):
            init[3 * hh + 2] = _dot(prs[hh], vb)

        def kpair(g, st):
            c0 = pl.multiple_of(g * 2 * MOBA_BLOCK, 2 * MOBA_BLOCK)
            kb2, vb2 = k_s[pl.ds(c0, 2 * MOBA_BLOCK), :], v_s[pl.ds(c0, 2 * MOBA_BLOCK), :]
            scores = [_dot_nt(qhs[hh], kb2) for hh in range(2)]
            new, prs = [], []
            for hh in range(2):
                m, l, acc = st[3 * hh:3 * hh + 3]
                sel_a = jnp.sum(jnp.where(lane_sq == 2 * g, sels[hh], 0.0), axis=1, keepdims=True)
                sel_b = jnp.sum(jnp.where(lane_sq == 2 * g + 1, sels[hh], 0.0), axis=1, keepdims=True)
                keep = jnp.where(second_block, sel_b, sel_a) > 0.0
                s = jnp.where(keep, scores[hh], NEG_BIG)
                m_new = jnp.maximum(m, jnp.max(s, axis=1, keepdims=True))
                pr = jnp.exp(s - m_new)
                alpha = jnp.exp(m - m_new)
                new += [m_new, alpha * l + jnp.sum(pr, axis=1, keepdims=True), alpha * acc]
                prs.append(pr.astype(BF16))
            for hh in range(2):
                new[3 * hh + 2] = new[3 * hh + 2] + _dot(prs[hh], vb2)
            return tuple(new)

        st = lax.fori_loop(0, (own + 1) // 2, kpair, tuple(init))
        o_ref[0, pl.ds(r0, qrows), :] = jnp.where(first_head, st[2] / st[1], st[5] / st[4])
        return carry

    lax.fori_loop(0, seq // qrows, qblock, 0)


ROUTE_CHUNKS_PER_STEP = 4


def _oddeven_merge_sort_pairs(n):
    pairs = []

    def merge(lo, hi, r):
        step = 2 * r
        if step < hi - lo:
            merge(lo, hi, step)
            merge(lo + r, hi, step)
            pairs.extend((i, i + r) for i in range(lo + r, hi - r, step))
        else:
            pairs.append((lo, lo + r))

    def sort(lo, hi):
        if hi - lo >= 1:
            mid = lo + (hi - lo) // 2
            sort(lo, mid)
            sort(mid + 1, hi)
            merge(lo, hi, 1)

    sort(0, n - 1)
    return pairs


SUBLANES = 8


def _top16_rows(scores, n_rows, vals_refs, idx_refs):
    n_slabs = n_rows // SUBLANES
    sub = _iota((SUBLANES, LANES), 0)
    vals = [[s[SUBLANES * v:SUBLANES * (v + 1), :] for v in range(n_slabs)] for s in scores]
    idxs = [[sub + SUBLANES * v for v in range(n_slabs)] for _ in scores]
    for i, j in _oddeven_merge_sort_pairs(n_slabs):
        for va, ia in zip(vals, idxs):
            a, b = va[i], va[j]
            a_first = jnp.logical_or(a > b, jnp.logical_and(a == b, ia[i] < ia[j]))
            va[i], va[j] = jnp.maximum(a, b), jnp.minimum(a, b)
            ia[i], ia[j] = jnp.where(a_first, ia[i], ia[j]), jnp.where(a_first, ia[j], ia[i])
    for it in range(PEER_TOPK):
        for k, (va, ia) in enumerate(zip(vals, idxs)):
            m = jnp.max(va[0], axis=0, keepdims=True)
            pick = jnp.min(jnp.where(va[0] == m, ia[0], n_rows), axis=0, keepdims=True)
            vals_refs[k][it:it + 1, :] = m
            idx_refs[k][it:it + 1, :] = pick
            win = ia[0] == pick
            depth = PEER_TOPK - 1 - it
            for d in range(min(depth, n_slabs - 1)):
                va[d] = jnp.where(win, va[d + 1], va[d])
                ia[d] = jnp.where(win, ia[d + 1], ia[d])
            if depth >= n_slabs:
                va[n_slabs - 1] = jnp.where(win, -jnp.inf, va[n_slabs - 1])


def _peer_route_kernel(x_ref, g_ref, sh_ref, sc_ref, wqt_ref, sk_ref, h_ref, ids_ref, gts_ref,
                       q_s, val_s, idx_s, ids_s, gts_s):
    tt = x_ref.shape[0]
    n_chunks = tt // LANES
    per_step = ROUTE_CHUNKS_PER_STEP
    half = PEER_D_KEY // 2
    h = _adaln(x_ref[...], g_ref[...], sh_ref[0], sc_ref[0])
    h_ref[...] = h
    qt = _dot_nt(wqt_ref[...], h.astype(BF16))
    for c in range(n_chunks):
        q_s[c] = qt[:, c * LANES:(c + 1) * LANES]
    sk1, sk2 = sk_ref[0], sk_ref[1]
    col_id = _iota((PEER_TOPK, LANES), 0)
    sub = _iota((SUBLANES, LANES), 0)
    col_depth = jnp.zeros((SUBLANES, LANES), I32)
    for a in range(SUBLANES):
        col_depth = jnp.where(sub == a, PEER_CAND_COUNTS[a], col_depth)
    vals = [val_s.at[i] for i in range(2 * per_step)]
    idxs = [idx_s.at[i] for i in range(2 * per_step)]

    def body(step, carry):
        hh = step // (n_chunks // per_step)
        c0 = (step % (n_chunks // per_step)) * per_step
        q0 = pl.multiple_of(hh * PEER_D_KEY, PEER_D_KEY)
        scores = []
        for k in range(per_step):
            scores.append(_dot3(sk1, q_s[c0 + k, pl.ds(q0, half), :]))
            scores.append(_dot3(sk2, q_s[c0 + k, pl.ds(q0 + half, half), :]))
        _top16_rows(scores, PEER_N_KEYS, vals, idxs)
        state = []
        for k in range(per_step):
            v1, i1, v2, i2 = vals[2 * k], idxs[2 * k], vals[2 * k + 1], idxs[2 * k + 1]
            v1x, e1x = v1[0:SUBLANES, :], i1[0:SUBLANES, :] * PEER_N_KEYS
            xs = [jnp.where(col_depth > b, v1x + v2[b:b + 1, :], -jnp.inf) for b in range(PEER_TOPK)]
            ex = [e1x + i2[b:b + 1, :] for b in range(PEER_TOPK)]
            y = v1[SUBLANES:PEER_TOPK, :] + v2[0:1, :]
            ey = i1[SUBLANES:PEER_TOPK, :] * PEER_N_KEYS + i2[0:1, :]
            state.append([xs, ex, y, ey])
        for it in range(PEER_TOPK):
            for k in range(per_step):
                xs, ex, y, ey = state[k]
                heads = jnp.concatenate([xs[0], y], axis=0)
                m = jnp.max(heads, axis=0, keepdims=True)
                pick = jnp.min(jnp.where(heads == m, col_id, PEER_TOPK), axis=0, keepdims=True)
                win = col_id == pick
                eids = jnp.where(win, jnp.concatenate([ex[0], ey], axis=0), 0)
                idxs[2 * k][it:it + 1, :] = jnp.sum(eids, axis=0, keepdims=True)
                vals[2 * k][it:it + 1, :] = m
                win_x, win_y = win[0:SUBLANES, :], win[SUBLANES:PEER_TOPK, :]
                for dd in range(PEER_TOPK - 1 - it):
                    xs[dd] = jnp.where(win_x, xs[dd + 1], xs[dd])
                    ex[dd] = jnp.where(win_x, ex[dd + 1], ex[dd])
                state[k][2] = jnp.where(win_y, -jnp.inf, y)
        r0 = pl.multiple_of(hh * PEER_TOPK, PEER_TOPK)
        for k in range(per_step):
            top = vals[2 * k][...]
            e = jnp.exp(top - top[0:1, :])
            gts_s[c0 + k, pl.ds(r0, PEER_TOPK), :] = e / jnp.sum(e, axis=0, keepdims=True)
            ids_s[c0 + k, pl.ds(r0, PEER_TOPK), :] = idxs[2 * k][...] * PACK_ROWS
        return carry

    lax.fori_loop(0, PEER_HEADS * n_chunks // per_step, body, 0)
    for c in range(n_chunks):
        rows = slice(c * LANES, (c + 1) * LANES)
        ids_ref[pl.ds(c * LANES * PEER_SLOTS, LANES * PEER_SLOTS)] = ids_s[c].T.reshape(LANES * PEER_SLOTS)
        gts_ref[rows, :] = gts_s[c].T


def _peer_route(x2, g, shift, scale, wqt_bf, sub_keys, seq):
    t, d = x2.shape
    tt = 512
    per_b = seq // tt
    n_chunks = tt // LANES
    nq = wqt_bf.shape[0]
    out_blk = pl.BlockSpec((tt, PEER_SLOTS), lambda i: (i, 0))
    return pl.pallas_call(
        _peer_route_kernel,
        grid=(t // tt,),
        in_specs=[
            pl.BlockSpec((tt, d), lambda i: (i, 0)),
            pl.BlockSpec((1, d), lambda i: (0, 0)),
            pl.BlockSpec((1, 1, d), lambda i: (i // per_b, 0, 0)),
            pl.BlockSpec((1, 1, d), lambda i: (i // per_b, 0, 0)),
            pl.BlockSpec((nq, d), lambda i: (0, 0)),
            pl.BlockSpec(sub_keys.shape, lambda i: (0, 0, 0)),
        ],
        out_specs=[pl.BlockSpec((tt, d), lambda i: (i, 0)),
                   pl.BlockSpec((tt * PEER_SLOTS,), lambda i: (i,)), out_blk],
        out_shape=[jax.ShapeDtypeStruct((t, d), F32),
                   jax.ShapeDtypeStruct((t * PEER_SLOTS,), I32),
                   jax.ShapeDtypeStruct((t, PEER_SLOTS), F32)],
        scratch_shapes=[pltpu.VMEM((n_chunks, nq, LANES), F32),
                        pltpu.VMEM((2 * ROUTE_CHUNKS_PER_STEP, PEER_TOPK, LANES), F32),
                        pltpu.VMEM((2 * ROUTE_CHUNKS_PER_STEP, PEER_TOPK, LANES), I32),
                        pltpu.VMEM((n_chunks, PEER_SLOTS, LANES), I32),
                        pltpu.VMEM((n_chunks, PEER_SLOTS, LANES), F32)],
        compiler_params=_cparams(1),
        name="peer_route",
    )(x2, g, shift, scale, wqt_bf, sub_keys)


def _chunk_row(c):
    return 2 * (c % PACK_ROWS) + c // PACK_ROWS


def _bf16_bits(a):
    return lax.bitcast_convert_type(a.astype(BF16).astype(F32), U32)


def _pack_kernel(t_ref, o_ref):
    n = t_ref.shape[1]
    half = t_ref.shape[2] // 2
    for s in range(PACK_ROWS):
        lo = _bf16_bits(t_ref[0, :, s * LANES:(s + 1) * LANES])
        hi = _bf16_bits(t_ref[0, :, half + s * LANES:half + (s + 1) * LANES])
        o_ref[pl.ds(s, n, stride=PACK_ROWS), :] = (lo >> 16) | hi


def _pack_table(tables, layer):
    _, e, d = tables.shape
    assert d == 2 * PACK_ROWS * LANES
    be = 512
    return pl.pallas_call(
        _pack_kernel,
        grid=(e // be,),
        in_specs=[pl.BlockSpec((1, be, d), lambda i: (layer, i, 0))],
        out_specs=pl.BlockSpec((be * PACK_ROWS, LANES), lambda i: (i, 0)),
        out_shape=jax.ShapeDtypeStruct((e * PACK_ROWS, LANES), U32),
        compiler_params=_cparams(1),
        name="pack_expert_table",
    )(tables)


def _table_spec(rows):
    return pl.BlockSpec((rows, LANES), lambda i: (0, 0), pipeline_mode=pl.Buffered(1))


def _gelu_exact(a):
    return 0.5 * a * (1.0 + lax.erf(a * (2.0 ** -0.5)))


FEAT_CHUNKS = 8
SLOT_WIDTH = PEER_SLOTS * FEAT_CHUNKS


def _gather_rows(ids_ref, base, tbl_ref, slot):
    for j in range(PEER_SLOTS):
        if j % ID_VIEW == 0:
            ids_part = ids_ref.at[pl.ds(base + j, ID_VIEW)]
        row0 = pl.multiple_of(ids_part[j % ID_VIEW], PACK_ROWS)
        slot[PACK_ROWS * j:PACK_ROWS * (j + 1), :] = tbl_ref[pl.ds(row0, PACK_ROWS), :]


def _pipelined_tokens(tt, ids_ref, next_ids_ref, tbl_ref, slots, compute):
    group = len(slots)
    n_steps = tt // group
    tiles = group // SUBLANES

    def compute_group(q):
        for k in range(group):
            compute(group * q + k, tiles * q + k // SUBLANES, k % SUBLANES, slots[k])

    @pl.when(pl.program_id(0) == 0)
    def _():
        for k in range(group):
            _gather_rows(ids_ref, k * PEER_SLOTS, tbl_ref, slots[k])

    def step(q, carry):
        compute_group(q)
        for k in range(group):
            _gather_rows(ids_ref, (group * (q + 1) + k) * PEER_SLOTS, tbl_ref, slots[k])
        return carry

    lax.fori_loop(0, n_steps - 1, step, 0)
    compute_group(n_steps - 1)
    for k in range(group):
        _gather_rows(next_ids_ref, k * PEER_SLOTS, tbl_ref, slots[k])


def _chunk_diag():
    return (_iota((FEAT_CHUNKS, SLOT_WIDTH), 1) % FEAT_CHUNKS) == _iota((FEAT_CHUNKS, SLOT_WIDTH), 0)


def _peer_u_kernel(ids_ref, next_ids_ref, h_ref, g_ref, tbl_ref, coef_ref, *scratch):
    slots, (hx_s, rs_s) = scratch[:N_SLOTS], scratch[N_SLOTS:]
    tt = h_ref.shape[0]
    for c in range(FEAT_CHUNKS):
        hx_s[pl.ds(_chunk_row(c), tt, stride=FEAT_CHUNKS), :] = h_ref[:, c * LANES:(c + 1) * LANES]
    diag = _chunk_diag()

    def compute(t, tile, sub, slot):
        rows = pltpu.bitcast(slot[...], BF16)
        x8 = hx_s[pl.ds(pl.multiple_of(t * FEAT_CHUNKS, FEAT_CHUNKS), FEAT_CHUNKS), :]
        part = _dot_nt(x8.astype(BF16), rows)
        rs_s[tile, sub:sub + 1, :] = jnp.sum(jnp.where(diag, part, 0.0), axis=0, keepdims=True)

    _pipelined_tokens(tt, ids_ref, next_ids_ref, tbl_ref, slots, compute)
    group = jnp.where(_iota((SLOT_WIDTH, PEER_SLOTS), 0) // FEAT_CHUNKS == _iota((SLOT_WIDTH, PEER_SLOTS), 1),
                      1.0, 0.0).astype(BF16)
    hi, lo = _split_bf16(rs_s[...].reshape(tt, SLOT_WIDTH))
    act = _dot(hi, group) + _dot(lo, group)
    coef_ref[...] = g_ref[...] * _gelu_exact(act)


N_SLOTS = 32
ID_VIEW = 16
PEER_TOKEN_TILE = 256


def _next_group_ids_spec(t, tt):
    groups_per_tile = tt // N_SLOTS
    last_tile = t // tt - 1
    return pl.BlockSpec((N_SLOTS * PEER_SLOTS,),
                        lambda i: (jnp.minimum(i + 1, last_tile) * groups_per_tile,),
                        memory_space=pltpu.SMEM)


def _slot_scratch():
    return [pltpu.VMEM((PEER_SLOTS * PACK_ROWS, LANES), U32)] * N_SLOTS


def _peer_u(ids_flat, h, gates, table, tt):
    t, d = h.shape
    return pl.pallas_call(
        _peer_u_kernel,
        grid=(t // tt,),
        in_specs=[
            pl.BlockSpec((tt * PEER_SLOTS,), lambda i: (i,), memory_space=pltpu.SMEM),
            _next_group_ids_spec(t, tt),
            pl.BlockSpec((tt, d), lambda i: (i, 0)),
            pl.BlockSpec((tt, PEER_SLOTS), lambda i: (i, 0)),
            _table_spec(table.shape[0]),
        ],
        out_specs=pl.BlockSpec((tt, PEER_SLOTS), lambda i: (i, 0)),
        out_shape=jax.ShapeDtypeStruct((t, PEER_SLOTS), F32),
        scratch_shapes=_slot_scratch() + [
                        pltpu.VMEM((tt * FEAT_CHUNKS, LANES), F32),
                        pltpu.VMEM((tt // SUBLANES, SUBLANES, SLOT_WIDTH), F32)],
        compiler_params=_cparams(1, TABLE_VMEM_LIMIT),
        name="peer_expert_in",
    )(ids_flat, ids_flat, h, gates, table)


def _peer_v_kernel(ids_ref, next_ids_ref, coef_ref, x_ref, gate_ref, tbl_ref, o_ref, *scratch):
    slots, (ce_hi_s, ce_lo_s, res_s) = scratch[:N_SLOTS], scratch[N_SLOTS:]
    tt = x_ref.shape[0]
    spread = jnp.where(_iota((PEER_SLOTS, SLOT_WIDTH), 1) // FEAT_CHUNKS == _iota((PEER_SLOTS, SLOT_WIDTH), 0),
                       1.0, 0.0).astype(BF16)
    hi, lo = _split_bf16(coef_ref[...])
    ce_hi_s[...] = _dot(hi, spread).reshape(ce_hi_s.shape)
    ce_lo_s[...] = _dot(lo, spread).reshape(ce_lo_s.shape)
    diag = _chunk_diag()

    def compute(t, tile, sub, slot):
        rows = pltpu.bitcast(slot[...], BF16)
        a_hi = jnp.where(diag, ce_hi_s[tile, sub:sub + 1, :], 0.0)
        a_lo = jnp.where(diag, ce_lo_s[tile, sub:sub + 1, :], 0.0)
        both = _dot(jnp.concatenate([a_hi, a_lo], axis=0).astype(BF16), rows)
        r0 = pl.multiple_of(t * FEAT_CHUNKS, FEAT_CHUNKS)
        res_s[pl.ds(r0, FEAT_CHUNKS), :] = both[0:FEAT_CHUNKS, :] + both[FEAT_CHUNKS:2 * FEAT_CHUNKS, :]

    _pipelined_tokens(tt, ids_ref, next_ids_ref, tbl_ref, slots, compute)
    for c in range(FEAT_CHUNKS):
        cols = slice(c * LANES, (c + 1) * LANES)
        y = res_s[pl.ds(_chunk_row(c), tt, stride=FEAT_CHUNKS), :]
        o_ref[:, cols] = x_ref[:, cols] + gate_ref[0][:, cols] * y


def _peer_v(ids_flat, coef, x2, gate, table, seq, tt):
    t, d = x2.shape
    per_b = seq // tt
    blk = pl.BlockSpec((tt, d), lambda i: (i, 0))
    return pl.pallas_call(
        _peer_v_kernel,
        grid=(t // tt,),
        in_specs=[
            pl.BlockSpec((tt * PEER_SLOTS,), lambda i: (i,), memory_space=pltpu.SMEM),
            _next_group_ids_spec(t, tt),
            pl.BlockSpec((tt, PEER_SLOTS), lambda i: (i, 0)),
            blk,
            pl.BlockSpec((1, 1, d), lambda i: (i // per_b, 0, 0)),
            _table_spec(table.shape[0]),
        ],
        out_specs=blk,
        out_shape=jax.ShapeDtypeStruct((t, d), F32),
        scratch_shapes=_slot_scratch() + [
                        pltpu.VMEM((tt // SUBLANES, SUBLANES, SLOT_WIDTH), F32),
                        pltpu.VMEM((tt // SUBLANES, SUBLANES, SLOT_WIDTH), F32),
                        pltpu.VMEM((tt * FEAT_CHUNKS, LANES), F32)],
        compiler_params=_cparams(1, TABLE_VMEM_LIMIT),
        name="peer_expert_out",
    )(ids_flat, ids_flat, coef, x2, gate, table)


def _peer_ffn(x2, g, shift, scale, gate, wq, sub_keys, table_u, table_v, seq):
    t, d = x2.shape
    tt = PEER_TOKEN_TILE
    h, ids, gates = _peer_route(x2, g, shift, scale, wq.T.astype(BF16), sub_keys, seq)
    ids_flat = ids
    coef = _peer_u(ids_flat, h, gates, table_u, tt)
    return _peer_v(ids_flat, coef, x2, gate, table_v, seq, tt)


def _rope_tables(seq):
    half = HEAD_DIM // 2
    inv_freq = ROPE_THETA ** (-jnp.arange(half, dtype=F32) / half)
    ang = jnp.arange(seq).astype(F32)[:, None] * inv_freq[None, :]
    reps = LANES // half
    return jnp.tile(jnp.cos(ang), (1, reps)), jnp.tile(jnp.sin(ang), (1, reps))


def _two_heads(gain):
    return jnp.tile(gain.reshape(1, HEAD_DIM), (1, LANES // HEAD_DIM))


def kernel(x, c, ada_w, ada_b, norm_mix_g, norm_ffn_g, w_in_ab, w_out_ab, sinks_a, qnorm_a, knorm_a,
           w_in_cd, w_out_cd, qnorm_c, knorm_c, qnorm_d, knorm_d, peer_wq, peer_subkeys, peer_u, peer_v):
    b, seq, d = x.shape
    depth = ada_w.shape[0]
    t = b * seq
    cos, sin = _rope_tables(seq)
    mod = _modulation(c, ada_w, ada_b)
    x2 = x.reshape(t, d)
    for layer in range(depth):
        shift_m, scale_m, gate_m, shift_f, scale_f, gate_f = [
            m.reshape(b, 1, d) for m in jnp.split(mod[layer], 6, axis=-1)]
        g_mix = norm_mix_g[layer].reshape(1, d)
        i = layer // 2
        if layer % 2 == 0:
            proj = _norm_proj(x2, g_mix, shift_m, scale_m, w_in_ab[i].astype(BF16), seq)
            proj = proj.reshape(b, seq, -1)
            ya = _swa_attention(proj, sinks_a[i], cos, sin, _two_heads(qnorm_a[i]), _two_heads(knorm_a[i]))
            b_col = (A_Q_HEADS + 2 * A_KV_HEADS) * HEAD_DIM // LANES
            yb = _stick_attention(proj, b_col)
            w_out = w_out_ab[i]
        else:
            proj = _norm_proj(x2, g_mix, shift_m, scale_m, w_in_cd[i].astype(BF16), seq)
            proj = proj.reshape(b, seq, -1)
            ya = _qkv_attention_call(_dilated_kernel, "dilated_attention", proj, 0, C_HEADS, cos, sin,
                                     _two_heads(qnorm_c[i]), _two_heads(knorm_c[i]))
            d_col = 3 * C_HEADS * HEAD_DIM // LANES
            yb = _qkv_attention_call(_moba_kernel, "moba_attention", proj, d_col, D_HEADS, cos, sin,
                                     _two_heads(qnorm_d[i]), _two_heads(knorm_d[i]),
                                     extra_scratch=(pltpu.VMEM((LANES, LANES), F32),
                                                    pltpu.VMEM((seq, LANES), F32),
                                                    pltpu.VMEM((seq, LANES), F32)))
            w_out = w_out_cd[i]
        x2 = _out_proj(x2, ya.reshape(t, -1), yb.reshape(t, -1), w_out.astype(BF16), gate_m, seq)
        x2 = _peer_ffn(x2, norm_ffn_g[layer].reshape(1, d), shift_f, scale_f, gate_f,
                       peer_wq[layer], peer_subkeys[layer],
                       _pack_table(peer_u, layer), _pack_table(peer_v, layer), seq)
    return x2.reshape(b, seq, d)
```

```python
import jax
import jax.numpy as jnp
from jax import lax
from jax.experimental import pallas as pl
from jax.experimental.pallas import tpu as pltpu

F32 = jnp.float32
BF16 = jnp.bfloat16
I32 = jnp.int32
U32 = jnp.uint32

HEAD_DIM = 64
ROPE_THETA = 10000.0
NORM_EPS = 1e-6
LANES = 128
QUERY_BLOCK = 128
A_Q_HEADS, A_KV_HEADS = 8, 2
B_HEADS = C_HEADS = D_HEADS = 8
C_PATTERNS = ((128, 1), (512, 4), (2048, 16))
MOBA_BLOCK, MOBA_TOPK = 256, 3
PEER_HEADS, PEER_N_KEYS, PEER_TOPK, PEER_D_KEY = 8, 128, 16, 256
PEER_SLOTS = PEER_HEADS * PEER_TOPK
NEG_BIG = -1e30
PEER_CAND_COUNTS = tuple(PEER_TOPK // (a + 1) for a in range(PEER_TOPK))
PACK_ROWS = 4
TOKEN_TILE = 512
MOD_COLUMN_TILE = 1024
PACK_EXPERT_TILE = 512
MIB = 1024 * 1024
V7X_VMEM_BYTES = 64 * MIB
STREAM_VMEM_LIMIT = 40 * MIB
TABLE_VMEM_LIMIT = V7X_VMEM_BYTES - 8 * MIB


def _cparams(n_axes, vmem_bytes=STREAM_VMEM_LIMIT):
    return pltpu.CompilerParams(
        dimension_semantics=("arbitrary",) * n_axes,
        vmem_limit_bytes=vmem_bytes)


def _split_bf16(a):
    hi = a.astype(BF16)
    lo = (a - hi.astype(F32)).astype(BF16)
    return hi, lo


def _dot(a, b):
    return jnp.dot(a, b, preferred_element_type=F32)


def _dot_nt(a, b):
    return lax.dot_general(a, b, (((1,), (1,)), ((), ())), preferred_element_type=F32)


def _dot3(a, b):
    ah, al = _split_bf16(a)
    bh, bl = _split_bf16(b)
    return _dot(ah, bh) + _dot(ah, bl) + _dot(al, bh)


def _dot3_nt(a, b):
    ah, al = _split_bf16(a)
    bh, bl = _split_bf16(b)
    return _dot_nt(ah, bh) + _dot_nt(ah, bl) + _dot_nt(al, bh)


def _iota(shape, dim):
    return lax.broadcasted_iota(I32, shape, dim)


def _mod_kernel(c_ref, w_ref, b_ref, o_ref):
    c = c_ref[...]
    cond = c * jax.nn.sigmoid(c)
    o_ref[0] = _dot3(cond, w_ref[0]) + b_ref[0]


def _modulation(c, ada_w, ada_b):
    depth, d, n = ada_w.shape
    b = c.shape[0]
    tn = MOD_COLUMN_TILE
    return pl.pallas_call(
        _mod_kernel,
        grid=(depth, n // tn),
        in_specs=[
            pl.BlockSpec((b, d), lambda l, j: (0, 0)),
            pl.BlockSpec((1, d, tn), lambda l, j: (l, 0, j)),
            pl.BlockSpec((1, 1, tn), lambda l, j: (l, 0, j)),
        ],
        out_specs=pl.BlockSpec((1, b, tn), lambda l, j: (l, 0, j)),
        out_shape=jax.ShapeDtypeStruct((depth, b, n), F32),
        compiler_params=_cparams(2),
        name="adaln_modulation",
    )(c, ada_w, ada_b.reshape(depth, 1, n))


def _adaln(x, g, shift, scale):
    ms = jnp.mean(x * x, axis=-1, keepdims=True)
    y = x * lax.rsqrt(ms + NORM_EPS) * g
    return y * (1.0 + scale) + shift


def _norm_proj_kernel(x_ref, g_ref, sh_ref, sc_ref, w_ref, o_ref):
    h = _adaln(x_ref[...], g_ref[...], sh_ref[0], sc_ref[0])
    o_ref[...] = _dot(h.astype(BF16), w_ref[...])


def _norm_proj(x2, g, shift, scale, w_bf, seq):
    t, d = x2.shape
    n = w_bf.shape[1]
    tt = TOKEN_TILE
    per_b = seq // tt
    return pl.pallas_call(
        _norm_proj_kernel,
        grid=(t // tt,),
        in_specs=[
            pl.BlockSpec((tt, d), lambda i: (i, 0)),
            pl.BlockSpec((1, d), lambda i: (0, 0)),
            pl.BlockSpec((1, 1, d), lambda i: (i // per_b, 0, 0)),
            pl.BlockSpec((1, 1, d), lambda i: (i // per_b, 0, 0)),
            pl.BlockSpec((d, n), lambda i: (0, 0)),
        ],
        out_specs=pl.BlockSpec((tt, n), lambda i: (i, 0)),
        out_shape=jax.ShapeDtypeStruct((t, n), F32),
        compiler_params=_cparams(1),
        name="adaln_in_proj",
    )(x2, g, shift, scale, w_bf)


def _out_proj_kernel(x_ref, ya_ref, yb_ref, w_ref, gate_ref, o_ref):
    half = ya_ref.shape[1]
    y = _dot(ya_ref[...].astype(BF16), w_ref[0:half, :])
    y = y + _dot(yb_ref[...].astype(BF16), w_ref[half:2 * half, :])
    o_ref[...] = x_ref[...] + gate_ref[0] * y


def _out_proj(x2, ya, yb, w_bf, gate, seq):
    t, d = x2.shape
    half = ya.shape[1]
    tt = TOKEN_TILE
    per_b = seq // tt
    return pl.pallas_call(
        _out_proj_kernel,
        grid=(t // tt,),
        in_specs=[
            pl.BlockSpec((tt, d), lambda i: (i, 0)),
            pl.BlockSpec((tt, half), lambda i: (i, 0)),
            pl.BlockSpec((tt, half), lambda i: (i, 0)),
            pl.BlockSpec((2 * half, d), lambda i: (0, 0)),
            pl.BlockSpec((1, 1, d), lambda i: (i // per_b, 0, 0)),
        ],
        out_specs=pl.BlockSpec((tt, d), lambda i: (i, 0)),
        out_shape=jax.ShapeDtypeStruct((t, d), F32),
        compiler_params=_cparams(1),
        name="mixer_out_proj",
    )(x2, ya, yb, w_bf, gate)


def _lane_row():
    return _iota((1, LANES), 1)


def _head_segment_ones():
    r = _iota((LANES, LANES), 0) // HEAD_DIM
    c = _iota((LANES, LANES), 1) // HEAD_DIM
    return jnp.where(r == c, 1.0, 0.0).astype(BF16)


def _headnorm_rope(a, g, cos, sin):
    hi, lo = _split_bf16(a * a)
    seg = _head_segment_ones()
    ms = (_dot(hi, seg) + _dot(lo, seg)) * (1.0 / HEAD_DIM)
    y = a * lax.rsqrt(ms + NORM_EPS) * g
    half = HEAD_DIM // 2
    upper = pltpu.roll(y, LANES - half, axis=1)
    lower = pltpu.roll(y, half, axis=1)
    first_half = (_lane_row() % HEAD_DIM) < half
    rot = jnp.where(first_half, -upper, lower)
    return y * cos + rot * sin


def _head_masks():
    lane = _lane_row()
    return lane < HEAD_DIM, lane >= HEAD_DIM


def _attn_specs(seq, qcol, kcol, vcol, kv_shared):
    blk = (1, seq, LANES)
    q_spec = pl.BlockSpec(blk, lambda b, p: (b, 0, qcol + p))
    if kv_shared:
        k_spec = pl.BlockSpec(blk, lambda b, p: (b, 0, kcol))
        v_spec = pl.BlockSpec(blk, lambda b, p: (b, 0, vcol))
    else:
        k_spec = pl.BlockSpec(blk, lambda b, p: (b, 0, kcol + p))
        v_spec = pl.BlockSpec(blk, lambda b, p: (b, 0, vcol + p))
    return q_spec, k_spec, v_spec


def _row_spec(seq):
    return pl.BlockSpec((seq, LANES), lambda b, p: (0, 0))


def _gain_spec():
    return pl.BlockSpec((1, LANES), lambda b, p: (0, 0))


def _store_heads(o_ref, r0, outs):
    first, _ = _head_masks()
    o_ref[0, pl.ds(r0, QUERY_BLOCK), :] = jnp.where(first, outs[0], outs[1])


SWA_QBLOCKS_PER_STEP = 4


def _swa_kernel(sinks_ref, q_ref, k_ref, v_ref, cos_ref, sin_ref, gq_ref, gk_ref, o_ref,
                q0_s, q1_s, k_s, v_s):
    p = pl.program_id(1)
    seq = q_ref.shape[1]
    cos, sin = cos_ref[...], sin_ref[...]
    first, second = _head_masks()
    qn = _headnorm_rope(q_ref[0], gq_ref[...], cos, sin) * (HEAD_DIM ** -0.5)
    q0_s[...] = jnp.where(first, qn, 0.0).astype(BF16)
    q1_s[...] = jnp.where(second, qn, 0.0).astype(BF16)
    pairs_per_kv = (A_Q_HEADS // A_KV_HEADS) // 2
    keep = jnp.logical_xor(first, (p // pairs_per_kv) == 1)
    kn = _headnorm_rope(k_ref[0], gk_ref[...], cos, sin)
    k_s[...] = jnp.where(keep, kn, pltpu.roll(kn, HEAD_DIM, axis=1)).astype(BF16)
    v = v_ref[0]
    v_s[...] = jnp.where(keep, v, pltpu.roll(v, HEAD_DIM, axis=1)).astype(BF16)

    qi = _iota((QUERY_BLOCK, QUERY_BLOCK), 0)
    ki = _iota((QUERY_BLOCK, QUERY_BLOCK), 1)

    mask_c = ki <= qi
    per_step = min(SWA_QBLOCKS_PER_STEP, seq // QUERY_BLOCK)

    def qstep(step, carry):
        chains, scores = [], []
        for u in range(per_step):
            i = step * per_step + u
            r0 = pl.multiple_of(i * QUERY_BLOCK, QUERY_BLOCK)
            rp = pl.multiple_of(jnp.maximum(i - 1, 0) * QUERY_BLOCK, QUERY_BLOCK)
            kc, kp = k_s[pl.ds(r0, QUERY_BLOCK), :], k_s[pl.ds(rp, QUERY_BLOCK), :]
            vc, vp = v_s[pl.ds(r0, QUERY_BLOCK), :], v_s[pl.ds(rp, QUERY_BLOCK), :]
            mask_p = jnp.logical_and(ki > qi, i > 0)
            chains.append((r0, vc, vp, mask_p))
            for q_s in (q0_s, q1_s):
                qh = q_s[pl.ds(r0, QUERY_BLOCK), :]
                scores.append((_dot_nt(qh, kc), _dot_nt(qh, kp)))
        probs, denoms = [], []
        for n, (s_cur, s_prev) in enumerate(scores):
            mask_p = chains[n // 2][3]
            sc = jnp.where(mask_c, s_cur, NEG_BIG)
            sp = jnp.where(mask_p, s_prev, NEG_BIG)
            sink = sinks_ref[2 * p + n % 2]
            m = jnp.maximum(jnp.max(sc, axis=1, keepdims=True), jnp.max(sp, axis=1, keepdims=True))
            m = jnp.maximum(m, sink)
            ec, ep = jnp.exp(sc - m), jnp.exp(sp - m)
            denoms.append(jnp.sum(ec, axis=1, keepdims=True) + jnp.sum(ep, axis=1, keepdims=True)
                          + jnp.exp(sink - m))
            probs.append((ec.astype(BF16), ep.astype(BF16)))
        for u, (r0, vc, vp, _) in enumerate(chains):
            outs = [(_dot(probs[2 * u + hh][0], vc) + _dot(probs[2 * u + hh][1], vp)) / denoms[2 * u + hh]
                    for hh in range(2)]
            _store_heads(o_ref, r0, outs)
        return carry

    lax.fori_loop(0, seq // (QUERY_BLOCK * per_step), qstep, 0)


def _swa_attention(proj, sinks, cos, sin, gq, gk):
    b, seq, _ = proj.shape
    n_pairs = A_Q_HEADS // 2
    kcol = A_Q_HEADS * HEAD_DIM // LANES
    vcol = kcol + A_KV_HEADS * HEAD_DIM // LANES
    q_spec, k_spec, v_spec = _attn_specs(seq, 0, kcol, vcol, True)
    return pl.pallas_call(
        _swa_kernel,
        grid=(b, n_pairs),
        in_specs=[pl.BlockSpec(memory_space=pltpu.SMEM), q_spec, k_spec, v_spec,
                  _row_spec(seq), _row_spec(seq), _gain_spec(), _gain_spec()],
        out_specs=pl.BlockSpec((1, seq, LANES), lambda b_, p: (b_, 0, p)),
        out_shape=jax.ShapeDtypeStruct((b, seq, n_pairs * LANES), F32),
        scratch_shapes=[pltpu.VMEM((seq, LANES), BF16)] * 4,
        compiler_params=_cparams(2),
        name="swa_gqa_attention",
    )(sinks, proj, proj, proj, cos, sin, gq, gk)


STICK_GROUP = 4
STICK_QUERY_ROWS = 256


def _stick_kernel(q_ref, k_ref, v_ref, o_ref, k_s, v_s):
    seq = q_ref.shape[1]
    k_s[...] = k_ref[0].astype(BF16)
    v_s[...] = v_ref[0].astype(BF16)
    first, second = _head_masks()
    qr = min(STICK_QUERY_ROWS, seq)
    kw = QUERY_BLOCK
    key_minus_query = _iota((qr, kw), 1) - _iota((qr, kw), 0)
    wr = _iota((2 * kw, 2 * kw), 0) % kw
    wc = _iota((2 * kw, 2 * kw), 1)
    suffix_w = jnp.where(jnp.logical_or(wc >= kw, wr > wc), 1.0, 0.0).astype(BF16)

    def qblock(i, carry):
        r0 = pl.multiple_of(i * qr, qr)
        q = q_ref[0, pl.ds(r0, qr), :] * (HEAD_DIM ** -0.5)
        qhs = [jnp.where(msk, q, 0.0).astype(BF16) for msk in (first, second)]
        n_blocks = (r0 + qr) // kw

        def kgroup(g, st):
            accs, laters = [st[0], st[1]], [st[2], st[3]]
            chains = [(u, hh) for u in range(STICK_GROUP) for hh in range(2)]
            vbs, pasts, zs = [], [], {}
            for u in range(STICK_GROUP):
                j = n_blocks - 1 - (g * STICK_GROUP + u)
                live = j >= 0
                c0 = pl.multiple_of(jnp.maximum(j, 0) * kw, kw)
                kb = k_s[pl.ds(c0, kw), :]
                vbs.append(v_s[pl.ds(c0, kw), :])
                pasts.append(jnp.logical_and(key_minus_query < r0 - c0, live))
                for hh in range(2):
                    zs[u, hh] = _dot_nt(qhs[hh], kb)
            logit, sums = {}, {}
            for u, hh in chains:
                z = zs[u, hh]
                sp = jnp.maximum(z, 0.0) + jnp.log(1.0 + jnp.exp(-jnp.abs(z)))
                neg_log_keep = jnp.where(pasts[u], sp, 0.0)
                logit[u, hh] = z - sp
                hi, lo = _split_bf16(neg_log_keep)
                sums[u, hh] = _dot(jnp.concatenate([hi, lo], axis=1), suffix_w)
            ws = {}
            for u, hh in chains:
                inner, total = sums[u, hh][:, :kw], sums[u, hh][:, kw:]
                ws[u, hh] = jnp.where(pasts[u], jnp.exp(logit[u, hh] - inner - laters[hh]), 0.0).astype(BF16)
                laters[hh] = laters[hh] + total
            for u, hh in chains:
                accs[hh] = accs[hh] + _dot(ws[u, hh], vbs[u])
            return accs[0], accs[1], laters[0], laters[1]

        zero = jnp.zeros((qr, LANES), F32)
        n_groups = (n_blocks + STICK_GROUP - 1) // STICK_GROUP
        st = lax.fori_loop(0, n_groups, kgroup, (zero, zero, zero, zero))
        o_ref[0, pl.ds(r0, qr), :] = jnp.where(first, st[0], st[1])
        return carry

    lax.fori_loop(0, seq // qr, qblock, 0)


def _stick_attention(proj, qcol):
    b, seq, _ = proj.shape
    n_pairs = B_HEADS // 2
    q_spec, k_spec, v_spec = _attn_specs(seq, qcol, qcol + n_pairs, qcol + 2 * n_pairs, False)
    return pl.pallas_call(
        _stick_kernel,
        grid=(b, n_pairs),
        in_specs=[q_spec, k_spec, v_spec],
        out_specs=pl.BlockSpec((1, seq, LANES), lambda b_, p: (b_, 0, p)),
        out_shape=jax.ShapeDtypeStruct((b, seq, n_pairs * LANES), F32),
        scratch_shapes=[pltpu.VMEM((seq, LANES), BF16)] * 2,
        compiler_params=_cparams(2),
        name="stick_breaking_attention",
    )(proj, proj, proj)


def _prep_qkv(q_ref, k_ref, v_ref, cos_ref, sin_ref, gq_ref, gk_ref, q0_s, q1_s, k_s, v_s):
    cos, sin = cos_ref[...], sin_ref[...]
    first, second = _head_masks()
    qn = _headnorm_rope(q_ref[0], gq_ref[...], cos, sin) * (HEAD_DIM ** -0.5)
    q0_s[...] = jnp.where(first, qn, 0.0).astype(BF16)
    q1_s[...] = jnp.where(second, qn, 0.0).astype(BF16)
    kn = _headnorm_rope(k_ref[0], gk_ref[...], cos, sin)
    k_s[...] = kn.astype(BF16)
    v_s[...] = v_ref[0].astype(BF16)
    return qn, kn


DILATED_KEY_TILE = 512
DILATED_QUERY_ROWS = 256


def _dilated_kernel(q_ref, k_ref, v_ref, cos_ref, sin_ref, gq_ref, gk_ref, o_ref,
                    q0_s, q1_s, k_s, v_s):
    seq = q_ref.shape[1]
    _prep_qkv(q_ref, k_ref, v_ref, cos_ref, sin_ref, gq_ref, gk_ref, q0_s, q1_s, k_s, v_s)
    kt = min(DILATED_KEY_TILE, seq)
    qr = min(DILATED_QUERY_ROWS, seq)
    qk = _iota((qr, kt), 0) - _iota((qr, kt), 1)
    on_stride = [jnp.where((qk & (dil - 1)) == 0, 1.0, 0.0) for _, dil in C_PATTERNS]
    first_head, _ = _head_masks()

    def qblock(i, carry):
        r0 = pl.multiple_of(i * qr, qr)
        qhs = [q_s[pl.ds(r0, qr), :] for q_s in (q0_s, q1_s)]

        def ktile(g, st):
            c0 = pl.multiple_of(g * kt, kt)
            d = (r0 - c0) + qk
            count = jnp.zeros(d.shape, F32)
            for (window, _), stride_ok in zip(C_PATTERNS, on_stride):
                count = count + jnp.where(d <= window, stride_ok, 0.0)
            count = jnp.where(d >= 0, count, 0.0)
            kb, vb = k_s[pl.ds(c0, kt), :], v_s[pl.ds(c0, kt), :]
            scores = [_dot_nt(qhs[hh], kb) for hh in range(2)]
            new, prs = [], []
            for hh in range(2):
                m, l, acc = st[3 * hh:3 * hh + 3]
                s = jnp.where(count > 0.0, scores[hh], NEG_BIG)
                m_new = jnp.maximum(m, jnp.max(s, axis=1, keepdims=True))
                pr = count * jnp.exp(s - m_new)
                alpha = jnp.exp(m - m_new)
                new += [m_new, alpha * l + jnp.sum(pr, axis=1, keepdims=True), alpha * acc]
                prs.append(pr.astype(BF16))
            for hh in range(2):
                new[3 * hh + 2] = new[3 * hh + 2] + _dot(prs[hh], vb)
            return tuple(new)

        init = (jnp.full((qr, 1), NEG_BIG, F32), jnp.zeros((qr, 1), F32),
                jnp.zeros((qr, LANES), F32)) * 2
        st = lax.fori_loop(0, (r0 + qr + kt - 1) // kt, ktile, init)
        o_ref[0, pl.ds(r0, qr), :] = jnp.where(first_head, st[2] / st[1], st[5] / st[4])
        return carry

    lax.fori_loop(0, seq // qr, qblock, 0)


def _qkv_attention_call(kernel, name, proj, qcol, n_heads, cos, sin, gq, gk, extra_scratch=()):
    b, seq, _ = proj.shape
    n_pairs = n_heads // 2
    q_spec, k_spec, v_spec = _attn_specs(seq, qcol, qcol + n_pairs, qcol + 2 * n_pairs, False)
    return pl.pallas_call(
        kernel,
        grid=(b, n_pairs),
        in_specs=[q_spec, k_spec, v_spec, _row_spec(seq), _row_spec(seq), _gain_spec(), _gain_spec()],
        out_specs=pl.BlockSpec((1, seq, LANES), lambda b_, p: (b_, 0, p)),
        out_shape=jax.ShapeDtypeStruct((b, seq, n_pairs * LANES), F32),
        scratch_shapes=[pltpu.VMEM((seq, LANES), BF16)] * 4 + list(extra_scratch),
        compiler_params=_cparams(2),
        name=name,
    )(proj, proj, proj, cos, sin, gq, gk)


def _moba_kernel(q_ref, k_ref, v_ref, cos_ref, sin_ref, gq_ref, gk_ref, o_ref,
                 q0_s, q1_s, k_s, v_s, km_s, sel0_s, sel1_s):
    seq = q_ref.shape[1]
    n_blocks = seq // MOBA_BLOCK
    qn, kn = _prep_qkv(q_ref, k_ref, v_ref, cos_ref, sin_ref, gq_ref, gk_ref, q0_s, q1_s, k_s, v_s)
    km_s[...] = jnp.zeros(km_s.shape, F32)
    km_s[0:n_blocks, :] = jnp.mean(kn.reshape(n_blocks, MOBA_BLOCK, LANES), axis=1)
    first, second = _head_masks()

    rows8 = _iota((8, seq), 0)
    own8 = _iota((8, seq), 1) // MOBA_BLOCK
    valid = rows8 < own8
    for msk, sel_s in ((first, sel0_s), (second, sel1_s)):
        gate = _dot3_nt(km_s[...], jnp.where(msk, qn, 0.0))[0:8, :]
        gm = jnp.where(valid, gate, -jnp.inf)
        rank = jnp.zeros((8, seq), F32)
        for n2 in range(n_blocks):
            g2 = gm[n2:n2 + 1, :]
            beats = jnp.logical_or(g2 > gm, jnp.logical_and(g2 == gm, n2 < rows8))
            rank = rank + jnp.where(jnp.logical_and(beats, n2 < own8), 1.0, 0.0)
        sel = jnp.where(jnp.logical_and(valid, rank < float(MOBA_TOPK)), 1.0, 0.0)
        sel = jnp.concatenate([sel, jnp.zeros((LANES - 8, seq), F32)], axis=0)
        sel_s[...] = sel.T

    qrows = MOBA_BLOCK
    lane_sq = _iota((qrows, LANES), 1)
    causal = _iota((qrows, MOBA_BLOCK), 1) <= _iota((qrows, MOBA_BLOCK), 0)
    second_block = _iota((qrows, 2 * MOBA_BLOCK), 1) >= MOBA_BLOCK
    first_head, _ = _head_masks()

    def qblock(own, carry):
        r0 = pl.multiple_of(own * qrows, qrows)
        qhs = [q_s[pl.ds(r0, qrows), :] for q_s in (q0_s, q1_s)]
        sels = [sel_s[pl.ds(r0, qrows), :] for sel_s in (sel0_s, sel1_s)]
        kb, vb = k_s[pl.ds(r0, MOBA_BLOCK), :], v_s[pl.ds(r0, MOBA_BLOCK), :]
        scores = [_dot_nt(qhs[hh], kb) for hh in range(2)]
        init, prs = [], []
        for hh in range(2):
            s = jnp.where(causal, scores[hh], NEG_BIG)
            m = jnp.max(s, axis=1, keepdims=True)
            pr = jnp.exp(s - m)
            init += [m, jnp.sum(pr, axis=1, keepdims=True), None]
            prs.append(pr.astype(BF16))
        for hh in range(2):
            init[3 * hh + 2] = _dot(prs[hh], vb)

        def kpair(g, st):
            c0 = pl.multiple_of(g * 2 * MOBA_BLOCK, 2 * MOBA_BLOCK)
            kb2, vb2 = k_s[pl.ds(c0, 2 * MOBA_BLOCK), :], v_s[pl.ds(c0, 2 * MOBA_BLOCK), :]
            scores = [_dot_nt(qhs[hh], kb2) for hh in range(2)]
            new, prs = [], []
            for hh in range(2):
                m, l, acc = st[3 * hh:3 * hh + 3]
                sel_a = jnp.sum(jnp.where(lane_sq == 2 * g, sels[hh], 0.0), axis=1, keepdims=True)
                sel_b = jnp.sum(jnp.where(lane_sq == 2 * g + 1, sels[hh], 0.0), axis=1, keepdims=True)
                keep = jnp.where(second_block, sel_b, sel_a) > 0.0
                s = jnp.where(keep, scores[hh], NEG_BIG)
                m_new = jnp.maximum(m, jnp.max(s, axis=1, keepdims=True))
                pr = jnp.exp(s - m_new)
                alpha = jnp.exp(m - m_new)
                new += [m_new, alpha * l + jnp.sum(pr, axis=1, keepdims=True), alpha * acc]
                prs.append(pr.astype(BF16))
            for hh in range(2):
                new[3 * hh + 2] = new[3 * hh + 2] + _dot(prs[hh], vb2)
            return tuple(new)

        st = lax.fori_loop(0, (own + 1) // 2, kpair, tuple(init))
        o_ref[0, pl.ds(r0, qrows), :] = jnp.where(first_head, st[2] / st[1], st[5] / st[4])
        return carry

    lax.fori_loop(0, seq // qrows, qblock, 0)


ROUTE_CHUNKS_PER_STEP = 4


def _oddeven_merge_sort_pairs(n):
    pairs = []

    def merge(lo, hi, r):
        step = 2 * r
        if step < hi - lo:
            merge(lo, hi, step)
            merge(lo + r, hi, step)
            pairs.extend((i, i + r) for i in range(lo + r, hi - r, step))
        else:
            pairs.append((lo, lo + r))

    def sort(lo, hi):
        if hi - lo >= 1:
            mid = lo + (hi - lo) // 2
            sort(lo, mid)
            sort(mid + 1, hi)
            merge(lo, hi, 1)

    sort(0, n - 1)
    return pairs


SUBLANES = 8


def _top16_rows(scores, n_rows, vals_refs, idx_refs):
    n_slabs = n_rows // SUBLANES
    sub = _iota((SUBLANES, LANES), 0)
    vals = [[s[SUBLANES * v:SUBLANES * (v + 1), :] for v in range(n_slabs)] for s in scores]
    idxs = [[sub + SUBLANES * v for v in range(n_slabs)] for _ in scores]
    for i, j in _oddeven_merge_sort_pairs(n_slabs):
        for va, ia in zip(vals, idxs):
            a, b = va[i], va[j]
            a_first = jnp.logical_or(a > b, jnp.logical_and(a == b, ia[i] < ia[j]))
            va[i], va[j] = jnp.maximum(a, b), jnp.minimum(a, b)
            ia[i], ia[j] = jnp.where(a_first, ia[i], ia[j]), jnp.where(a_first, ia[j], ia[i])
    for it in range(PEER_TOPK):
        for k, (va, ia) in enumerate(zip(vals, idxs)):
            m = jnp.max(va[0], axis=0, keepdims=True)
            pick = jnp.min(jnp.where(va[0] == m, ia[0], n_rows), axis=0, keepdims=True)
            vals_refs[k][it:it + 1, :] = m
            idx_refs[k][it:it + 1, :] = pick
            win = ia[0] == pick
            depth = PEER_TOPK - 1 - it
            for d in range(min(depth, n_slabs - 1)):
                va[d] = jnp.where(win, va[d + 1], va[d])
                ia[d] = jnp.where(win, ia[d + 1], ia[d])
            if depth >= n_slabs:
                va[n_slabs - 1] = jnp.where(win, -jnp.inf, va[n_slabs - 1])


def _peer_route_kernel(x_ref, g_ref, sh_ref, sc_ref, wqt_ref, sk_ref, h_ref, ids_ref, gts_ref,
                       q_s, val_s, idx_s, ids_s, gts_s):
    tt = x_ref.shape[0]
    n_chunks = tt // LANES
    per_step = ROUTE_CHUNKS_PER_STEP
    half = PEER_D_KEY // 2
    h = _adaln(x_ref[...], g_ref[...], sh_ref[0], sc_ref[0])
    h_ref[...] = h
    qt = _dot_nt(wqt_ref[...], h.astype(BF16))
    for c in range(n_chunks):
        q_s[c] = qt[:, c * LANES:(c + 1) * LANES]
    sk1, sk2 = sk_ref[0], sk_ref[1]
    col_id = _iota((PEER_TOPK, LANES), 0)
    sub = _iota((SUBLANES, LANES), 0)
    col_depth = jnp.zeros((SUBLANES, LANES), I32)
    for a in range(SUBLANES):
        col_depth = jnp.where(sub == a, PEER_CAND_COUNTS[a], col_depth)
    vals = [val_s.at[i] for i in range(2 * per_step)]
    idxs = [idx_s.at[i] for i in range(2 * per_step)]

    def body(step, carry):
        hh = step // (n_chunks // per_step)
        c0 = (step % (n_chunks // per_step)) * per_step
        q0 = pl.multiple_of(hh * PEER_D_KEY, PEER_D_KEY)
        scores = []
        for k in range(per_step):
            scores.append(_dot3(sk1, q_s[c0 + k, pl.ds(q0, half), :]))
            scores.append(_dot3(sk2, q_s[c0 + k, pl.ds(q0 + half, half), :]))
        _top16_rows(scores, PEER_N_KEYS, vals, idxs)
        state = []
        for k in range(per_step):
            v1, i1, v2, i2 = vals[2 * k], idxs[2 * k], vals[2 * k + 1], idxs[2 * k + 1]
            v1x, e1x = v1[0:SUBLANES, :], i1[0:SUBLANES, :] * PEER_N_KEYS
            xs = [jnp.where(col_depth > b, v1x + v2[b:b + 1, :], -jnp.inf) for b in range(PEER_TOPK)]
            ex = [e1x + i2[b:b + 1, :] for b in range(PEER_TOPK)]
            y = v1[SUBLANES:PEER_TOPK, :] + v2[0:1, :]
            ey = i1[SUBLANES:PEER_TOPK, :] * PEER_N_KEYS + i2[0:1, :]
            state.append([xs, ex, y, ey])
        for it in range(PEER_TOPK):
            for k in range(per_step):
                xs, ex, y, ey = state[k]
                heads = jnp.concatenate([xs[0], y], axis=0)
                m = jnp.max(heads, axis=0, keepdims=True)
                pick = jnp.min(jnp.where(heads == m, col_id, PEER_TOPK), axis=0, keepdims=True)
                win = col_id == pick
                eids = jnp.where(win, jnp.concatenate([ex[0], ey], axis=0), 0)
                idxs[2 * k][it:it + 1, :] = jnp.sum(eids, axis=0, keepdims=True)
                vals[2 * k][it:it + 1, :] = m
                win_x, win_y = win[0:SUBLANES, :], win[SUBLANES:PEER_TOPK, :]
                for dd in range(PEER_TOPK - 1 - it):
                    xs[dd] = jnp.where(win_x, xs[dd + 1], xs[dd])
                    ex[dd] = jnp.where(win_x, ex[dd + 1], ex[dd])
                state[k][2] = jnp.where(win_y, -jnp.inf, y)
        r0 = pl.multiple_of(hh * PEER_TOPK, PEER_TOPK)
        for k in range(per_step):
            top = vals[2 * k][...]
            e = jnp.exp(top - top[0:1, :])
            gts_s[c0 + k, pl.ds(r0, PEER_TOPK), :] = e / jnp.sum(e, axis=0, keepdims=True)
            ids_s[c0 + k, pl.ds(r0, PEER_TOPK), :] = idxs[2 * k][...] * PACK_ROWS
        return carry

    lax.fori_loop(0, PEER_HEADS * n_chunks // per_step, body, 0)
    for c in range(n_chunks):
        rows = slice(c * LANES, (c + 1) * LANES)
        ids_ref[pl.ds(c * LANES * PEER_SLOTS, LANES * PEER_SLOTS)] = ids_s[c].T.reshape(LANES * PEER_SLOTS)
        gts_ref[rows, :] = gts_s[c].T


def _peer_route(x2, g, shift, scale, wqt_bf, sub_keys, seq):
    t, d = x2.shape
    tt = TOKEN_TILE
    per_b = seq // tt
    n_chunks = tt // LANES
    nq = wqt_bf.shape[0]
    out_blk = pl.BlockSpec((tt, PEER_SLOTS), lambda i: (i, 0))
    return pl.pallas_call(
        _peer_route_kernel,
        grid=(t // tt,),
        in_specs=[
            pl.BlockSpec((tt, d), lambda i: (i, 0)),
            pl.BlockSpec((1, d), lambda i: (0, 0)),
            pl.BlockSpec((1, 1, d), lambda i: (i // per_b, 0, 0)),
            pl.BlockSpec((1, 1, d), lambda i: (i // per_b, 0, 0)),
            pl.BlockSpec((nq, d), lambda i: (0, 0)),
            pl.BlockSpec(sub_keys.shape, lambda i: (0, 0, 0)),
        ],
        out_specs=[pl.BlockSpec((tt, d), lambda i: (i, 0)),
                   pl.BlockSpec((tt * PEER_SLOTS,), lambda i: (i,)), out_blk],
        out_shape=[jax.ShapeDtypeStruct((t, d), F32),
                   jax.ShapeDtypeStruct((t * PEER_SLOTS,), I32),
                   jax.ShapeDtypeStruct((t, PEER_SLOTS), F32)],
        scratch_shapes=[pltpu.VMEM((n_chunks, nq, LANES), F32),
                        pltpu.VMEM((2 * ROUTE_CHUNKS_PER_STEP, PEER_TOPK, LANES), F32),
                        pltpu.VMEM((2 * ROUTE_CHUNKS_PER_STEP, PEER_TOPK, LANES), I32),
                        pltpu.VMEM((n_chunks, PEER_SLOTS, LANES), I32),
                        pltpu.VMEM((n_chunks, PEER_SLOTS, LANES), F32)],
        compiler_params=_cparams(1),
        name="peer_route",
    )(x2, g, shift, scale, wqt_bf, sub_keys)


def _chunk_row(c):
    return 2 * (c % PACK_ROWS) + c // PACK_ROWS


def _bf16_bits(a):
    return lax.bitcast_convert_type(a.astype(BF16).astype(F32), U32)


def _pack_kernel(t_ref, o_ref):
    n = t_ref.shape[1]
    half = t_ref.shape[2] // 2
    for s in range(PACK_ROWS):
        lo = _bf16_bits(t_ref[0, :, s * LANES:(s + 1) * LANES])
        hi = _bf16_bits(t_ref[0, :, half + s * LANES:half + (s + 1) * LANES])
        o_ref[pl.ds(s, n, stride=PACK_ROWS), :] = (lo >> 16) | hi


def _pack_table(tables, layer):
    _, e, d = tables.shape
    assert d == 2 * PACK_ROWS * LANES
    be = PACK_EXPERT_TILE
    return pl.pallas_call(
        _pack_kernel,
        grid=(e // be,),
        in_specs=[pl.BlockSpec((1, be, d), lambda i: (layer, i, 0))],
        out_specs=pl.BlockSpec((be * PACK_ROWS, LANES), lambda i: (i, 0)),
        out_shape=jax.ShapeDtypeStruct((e * PACK_ROWS, LANES), U32),
        compiler_params=_cparams(1),
        name="pack_expert_table",
    )(tables)


def _table_spec(rows):
    return pl.BlockSpec((rows, LANES), lambda i: (0, 0), pipeline_mode=pl.Buffered(1))


def _gelu_exact(a):
    return 0.5 * a * (1.0 + lax.erf(a * (2.0 ** -0.5)))


FEAT_CHUNKS = 8
SLOT_WIDTH = PEER_SLOTS * FEAT_CHUNKS


def _gather_rows(ids_ref, base, tbl_ref, slot):
    for j in range(PEER_SLOTS):
        if j % ID_VIEW == 0:
            ids_part = ids_ref.at[pl.ds(base + j, ID_VIEW)]
        row0 = pl.multiple_of(ids_part[j % ID_VIEW], PACK_ROWS)
        slot[PACK_ROWS * j:PACK_ROWS * (j + 1), :] = tbl_ref[pl.ds(row0, PACK_ROWS), :]


def _pipelined_tokens(tt, ids_ref, next_ids_ref, tbl_ref, slots, compute):
    group = len(slots)
    n_steps = tt // group
    tiles = group // SUBLANES

    def compute_group(q):
        for k in range(group):
            compute(group * q + k, tiles * q + k // SUBLANES, k % SUBLANES, slots[k])

    @pl.when(pl.program_id(0) == 0)
    def _():
        for k in range(group):
            _gather_rows(ids_ref, k * PEER_SLOTS, tbl_ref, slots[k])

    def step(q, carry):
        compute_group(q)
        for k in range(group):
            _gather_rows(ids_ref, (group * (q + 1) + k) * PEER_SLOTS, tbl_ref, slots[k])
        return carry

    lax.fori_loop(0, n_steps - 1, step, 0)
    compute_group(n_steps - 1)
    for k in range(group):
        _gather_rows(next_ids_ref, k * PEER_SLOTS, tbl_ref, slots[k])


def _chunk_diag():
    return (_iota((FEAT_CHUNKS, SLOT_WIDTH), 1) % FEAT_CHUNKS) == _iota((FEAT_CHUNKS, SLOT_WIDTH), 0)


def _peer_u_kernel(ids_ref, next_ids_ref, h_ref, g_ref, tbl_ref, coef_ref, *scratch):
    slots, (hx_s, rs_s) = scratch[:N_SLOTS], scratch[N_SLOTS:]
    tt = h_ref.shape[0]
    for c in range(FEAT_CHUNKS):
        hx_s[pl.ds(_chunk_row(c), tt, stride=FEAT_CHUNKS), :] = h_ref[:, c * LANES:(c + 1) * LANES]
    diag = _chunk_diag()

    def compute(t, tile, sub, slot):
        rows = pltpu.bitcast(slot[...], BF16)
        x8 = hx_s[pl.ds(pl.multiple_of(t * FEAT_CHUNKS, FEAT_CHUNKS), FEAT_CHUNKS), :]
        part = _dot_nt(x8.astype(BF16), rows)
        rs_s[tile, sub:sub + 1, :] = jnp.sum(jnp.where(diag, part, 0.0), axis=0, keepdims=True)

    _pipelined_tokens(tt, ids_ref, next_ids_ref, tbl_ref, slots, compute)
    group = jnp.where(_iota((SLOT_WIDTH, PEER_SLOTS), 0) // FEAT_CHUNKS == _iota((SLOT_WIDTH, PEER_SLOTS), 1),
                      1.0, 0.0).astype(BF16)
    hi, lo = _split_bf16(rs_s[...].reshape(tt, SLOT_WIDTH))
    act = _dot(hi, group) + _dot(lo, group)
    coef_ref[...] = g_ref[...] * _gelu_exact(act)


N_SLOTS = 32
ID_VIEW = 16
PEER_TOKEN_TILE = 256


def _next_group_ids_spec(t, tt):
    groups_per_tile = tt // N_SLOTS
    last_tile = t // tt - 1
    return pl.BlockSpec((N_SLOTS * PEER_SLOTS,),
                        lambda i: (jnp.minimum(i + 1, last_tile) * groups_per_tile,),
                        memory_space=pltpu.SMEM)


def _slot_scratch():
    return [pltpu.VMEM((PEER_SLOTS * PACK_ROWS, LANES), U32)] * N_SLOTS


def _peer_u(ids_flat, h, gates, table, tt):
    t, d = h.shape
    return pl.pallas_call(
        _peer_u_kernel,
        grid=(t // tt,),
        in_specs=[
            pl.BlockSpec((tt * PEER_SLOTS,), lambda i: (i,), memory_space=pltpu.SMEM),
            _next_group_ids_spec(t, tt),
            pl.BlockSpec((tt, d), lambda i: (i, 0)),
            pl.BlockSpec((tt, PEER_SLOTS), lambda i: (i, 0)),
            _table_spec(table.shape[0]),
        ],
        out_specs=pl.BlockSpec((tt, PEER_SLOTS), lambda i: (i, 0)),
        out_shape=jax.ShapeDtypeStruct((t, PEER_SLOTS), F32),
        scratch_shapes=_slot_scratch() + [
                        pltpu.VMEM((tt * FEAT_CHUNKS, LANES), F32),
                        pltpu.VMEM((tt // SUBLANES, SUBLANES, SLOT_WIDTH), F32)],
        compiler_params=_cparams(1, TABLE_VMEM_LIMIT),
        name="peer_expert_in",
    )(ids_flat, ids_flat, h, gates, table)


def _peer_v_kernel(ids_ref, next_ids_ref, coef_ref, x_ref, gate_ref, tbl_ref, o_ref, *scratch):
    slots, (ce_hi_s, ce_lo_s, res_s) = scratch[:N_SLOTS], scratch[N_SLOTS:]
    tt = x_ref.shape[0]
    spread = jnp.where(_iota((PEER_SLOTS, SLOT_WIDTH), 1) // FEAT_CHUNKS == _iota((PEER_SLOTS, SLOT_WIDTH), 0),
                       1.0, 0.0).astype(BF16)
    hi, lo = _split_bf16(coef_ref[...])
    ce_hi_s[...] = _dot(hi, spread).reshape(ce_hi_s.shape)
    ce_lo_s[...] = _dot(lo, spread).reshape(ce_lo_s.shape)
    diag = _chunk_diag()

    def compute(t, tile, sub, slot):
        rows = pltpu.bitcast(slot[...], BF16)
        a_hi = jnp.where(diag, ce_hi_s[tile, sub:sub + 1, :], 0.0)
        a_lo = jnp.where(diag, ce_lo_s[tile, sub:sub + 1, :], 0.0)
        both = _dot(jnp.concatenate([a_hi, a_lo], axis=0).astype(BF16), rows)
        r0 = pl.multiple_of(t * FEAT_CHUNKS, FEAT_CHUNKS)
        res_s[pl.ds(r0, FEAT_CHUNKS), :] = both[0:FEAT_CHUNKS, :] + both[FEAT_CHUNKS:2 * FEAT_CHUNKS, :]

    _pipelined_tokens(tt, ids_ref, next_ids_ref, tbl_ref, slots, compute)
    for c in range(FEAT_CHUNKS):
        cols = slice(c * LANES, (c + 1) * LANES)
        y = res_s[pl.ds(_chunk_row(c), tt, stride=FEAT_CHUNKS), :]
        o_ref[:, cols] = x_ref[:, cols] + gate_ref[0][:, cols] * y


def _peer_v(ids_flat, coef, x2, gate, table, seq, tt):
    t, d = x2.shape
    per_b = seq // tt
    blk = pl.BlockSpec((tt, d), lambda i: (i, 0))
    return pl.pallas_call(
        _peer_v_kernel,
        grid=(t // tt,),
        in_specs=[
            pl.BlockSpec((tt * PEER_SLOTS,), lambda i: (i,), memory_space=pltpu.SMEM),
            _next_group_ids_spec(t, tt),
            pl.BlockSpec((tt, PEER_SLOTS), lambda i: (i, 0)),
            blk,
            pl.BlockSpec((1, 1, d), lambda i: (i // per_b, 0, 0)),
            _table_spec(table.shape[0]),
        ],
        out_specs=blk,
        out_shape=jax.ShapeDtypeStruct((t, d), F32),
        scratch_shapes=_slot_scratch() + [
                        pltpu.VMEM((tt // SUBLANES, SUBLANES, SLOT_WIDTH), F32),
                        pltpu.VMEM((tt // SUBLANES, SUBLANES, SLOT_WIDTH), F32),
                        pltpu.VMEM((tt * FEAT_CHUNKS, LANES), F32)],
        compiler_params=_cparams(1, TABLE_VMEM_LIMIT),
        name="peer_expert_out",
    )(ids_flat, ids_flat, coef, x2, gate, table)


def _peer_ffn(x2, g, shift, scale, gate, wq, sub_keys, table_u, table_v, seq):
    t, d = x2.shape
    tt = PEER_TOKEN_TILE
    h, ids, gates = _peer_route(x2, g, shift, scale, wq.T.astype(BF16), sub_keys, seq)
    ids_flat = ids
    coef = _peer_u(ids_flat, h, gates, table_u, tt)
    return _peer_v(ids_flat, coef, x2, gate, table_v, seq, tt)


def _rope_tables(seq):
    half = HEAD_DIM // 2
    inv_freq = ROPE_THETA ** (-jnp.arange(half, dtype=F32) / half)
    ang = jnp.arange(seq).astype(F32)[:, None] * inv_freq[None, :]
    reps = LANES // half
    return jnp.tile(jnp.cos(ang), (1, reps)), jnp.tile(jnp.sin(ang), (1, reps))


def _two_heads(gain):
    return jnp.tile(gain.reshape(1, HEAD_DIM), (1, LANES // HEAD_DIM))


def kernel(x, c, ada_w, ada_b, norm_mix_g, norm_ffn_g, w_in_ab, w_out_ab, sinks_a, qnorm_a, knorm_a,
           w_in_cd, w_out_cd, qnorm_c, knorm_c, qnorm_d, knorm_d, peer_wq, peer_subkeys, peer_u, peer_v):
    b, seq, d = x.shape
    depth = ada_w.shape[0]
    t = b * seq
    cos, sin = _rope_tables(seq)
    mod = _modulation(c, ada_w, ada_b)
    x2 = x.reshape(t, d)
    for layer in range(depth):
        shift_m, scale_m, gate_m, shift_f, scale_f, gate_f = [
            m.reshape(b, 1, d) for m in jnp.split(mod[layer], 6, axis=-1)]
        g_mix = norm_mix_g[layer].reshape(1, d)
        i = layer // 2
        if layer % 2 == 0:
            proj = _norm_proj(x2, g_mix, shift_m, scale_m, w_in_ab[i].astype(BF16), seq)
            proj = proj.reshape(b, seq, -1)
            ya = _swa_attention(proj, sinks_a[i], cos, sin, _two_heads(qnorm_a[i]), _two_heads(knorm_a[i]))
            b_col = (A_Q_HEADS + 2 * A_KV_HEADS) * HEAD_DIM // LANES
            yb = _stick_attention(proj, b_col)
            w_out = w_out_ab[i]
        else:
            proj = _norm_proj(x2, g_mix, shift_m, scale_m, w_in_cd[i].astype(BF16), seq)
            proj = proj.reshape(b, seq, -1)
            ya = _qkv_attention_call(_dilated_kernel, "dilated_attention", proj, 0, C_HEADS, cos, sin,
                                     _two_heads(qnorm_c[i]), _two_heads(knorm_c[i]))
            d_col = 3 * C_HEADS * HEAD_DIM // LANES
            yb = _qkv_attention_call(_moba_kernel, "moba_attention", proj, d_col, D_HEADS, cos, sin,
                                     _two_heads(qnorm_d[i]), _two_heads(knorm_d[i]),
                                     extra_scratch=(pltpu.VMEM((LANES, LANES), F32),
                                                    pltpu.VMEM((seq, LANES), F32),
                                                    pltpu.VMEM((seq, LANES), F32)))
            w_out = w_out_cd[i]
        x2 = _out_proj(x2, ya.reshape(t, -1), yb.reshape(t, -1), w_out.astype(BF16), gate_m, seq)
        x2 = _peer_ffn(x2, norm_ffn_g[layer].reshape(1, d), shift_f, scale_f, gate_f,
                       peer_wq[layer], peer_subkeys[layer],
                       _pack_table(peer_u, layer), _pack_table(peer_v, layer), seq)
    return x2.reshape(b, seq, d)
```

```python
import jax
import jax.numpy as jnp
from jax import lax
from jax.experimental import pallas as pl
from jax.experimental.pallas import tpu as pltpu

F32 = jnp.float32
BF16 = jnp.bfloat16
I32 = jnp.int32
U32 = jnp.uint32

HEAD_DIM = 64
ROPE_THETA = 10000.0
NORM_EPS = 1e-6
LANES = 128
QUERY_BLOCK = 128
A_Q_HEADS, A_KV_HEADS = 8, 2
B_HEADS = C_HEADS = D_HEADS = 8
C_PATTERNS = ((128, 1), (512, 4), (2048, 16))
MOBA_BLOCK, MOBA_TOPK = 256, 3
PEER_HEADS, PEER_N_KEYS, PEER_TOPK, PEER_D_KEY = 8, 128, 16, 256
PEER_SLOTS = PEER_HEADS * PEER_TOPK
NEG_BIG = -1e30
PEER_CAND_COUNTS = tuple(PEER_TOPK // (a + 1) for a in range(PEER_TOPK))
PACK_ROWS = 4
TOKEN_TILE = 512
MOD_COLUMN_TILE = 1024
PACK_EXPERT_TILE = 512
MIB = 1024 * 1024
V7X_VMEM_BYTES = 64 * MIB
STREAM_VMEM_LIMIT = 48 * MIB
TABLE_VMEM_LIMIT = V7X_VMEM_BYTES - 8 * MIB


def _cparams(n_axes, vmem_bytes=STREAM_VMEM_LIMIT):
    return pltpu.CompilerParams(
        dimension_semantics=("arbitrary",) * n_axes,
        vmem_limit_bytes=vmem_bytes)


def _split_bf16(a):
    hi = a.astype(BF16)
    lo = (a - hi.astype(F32)).astype(BF16)
    return hi, lo


def _dot(a, b):
    return jnp.dot(a, b, preferred_element_type=F32)


def _dot_nt(a, b):
    return lax.dot_general(a, b, (((1,), (1,)), ((), ())), preferred_element_type=F32)


def _dot3(a, b):
    ah, al = _split_bf16(a)
    bh, bl = _split_bf16(b)
    return _dot(ah, bh) + _dot(ah, bl) + _dot(al, bh)


def _dot3_nt(a, b):
    ah, al = _split_bf16(a)
    bh, bl = _split_bf16(b)
    return _dot_nt(ah, bh) + _dot_nt(ah, bl) + _dot_nt(al, bh)


def _iota(shape, dim):
    return lax.broadcasted_iota(I32, shape, dim)


def _mod_kernel(c_ref, w_ref, b_ref, o_ref):
    c = c_ref[...]
    cond = c * jax.nn.sigmoid(c)
    o_ref[0] = _dot3(cond, w_ref[0]) + b_ref[0]


def _modulation(c, ada_w, ada_b):
    depth, d, n = ada_w.shape
    b = c.shape[0]
    tn = MOD_COLUMN_TILE
    return pl.pallas_call(
        _mod_kernel,
        grid=(depth, n // tn),
        in_specs=[
            pl.BlockSpec((b, d), lambda l, j: (0, 0)),
            pl.BlockSpec((1, d, tn), lambda l, j: (l, 0, j)),
            pl.BlockSpec((1, 1, tn), lambda l, j: (l, 0, j)),
        ],
        out_specs=pl.BlockSpec((1, b, tn), lambda l, j: (l, 0, j)),
        out_shape=jax.ShapeDtypeStruct((depth, b, n), F32),
        compiler_params=_cparams(2),
        name="adaln_modulation",
    )(c, ada_w, ada_b.reshape(depth, 1, n))


def _adaln(x, g, shift, scale):
    ms = jnp.mean(x * x, axis=-1, keepdims=True)
    y = x * lax.rsqrt(ms + NORM_EPS) * g
    return y * (1.0 + scale) + shift


def _norm_proj_kernel(x_ref, g_ref, sh_ref, sc_ref, w_ref, o_ref):
    h = _adaln(x_ref[...], g_ref[...], sh_ref[0], sc_ref[0])
    o_ref[...] = _dot(h.astype(BF16), w_ref[...])


def _norm_proj(x2, g, shift, scale, w_bf, seq):
    t, d = x2.shape
    n = w_bf.shape[1]
    tt = TOKEN_TILE
    per_b = seq // tt
    return pl.pallas_call(
        _norm_proj_kernel,
        grid=(t // tt,),
        in_specs=[
            pl.BlockSpec((tt, d), lambda i: (i, 0)),
            pl.BlockSpec((1, d), lambda i: (0, 0)),
            pl.BlockSpec((1, 1, d), lambda i: (i // per_b, 0, 0)),
            pl.BlockSpec((1, 1, d), lambda i: (i // per_b, 0, 0)),
            pl.BlockSpec((d, n), lambda i: (0, 0)),
        ],
        out_specs=pl.BlockSpec((tt, n), lambda i: (i, 0)),
        out_shape=jax.ShapeDtypeStruct((t, n), F32),
        compiler_params=_cparams(1),
        name="adaln_in_proj",
    )(x2, g, shift, scale, w_bf)


def _out_proj_kernel(x_ref, ya_ref, yb_ref, w_ref, gate_ref, o_ref):
    half = ya_ref.shape[1]
    y = _dot(ya_ref[...].astype(BF16), w_ref[0:half, :])
    y = y + _dot(yb_ref[...].astype(BF16), w_ref[half:2 * half, :])
    o_ref[...] = x_ref[...] + gate_ref[0] * y


def _out_proj(x2, ya, yb, w_bf, gate, seq):
    t, d = x2.shape
    half = ya.shape[1]
    tt = TOKEN_TILE
    per_b = seq // tt
    return pl.pallas_call(
        _out_proj_kernel,
        grid=(t // tt,),
        in_specs=[
            pl.BlockSpec((tt, d), lambda i: (i, 0)),
            pl.BlockSpec((tt, half), lambda i: (i, 0)),
            pl.BlockSpec((tt, half), lambda i: (i, 0)),
            pl.BlockSpec((2 * half, d), lambda i: (0, 0)),
            pl.BlockSpec((1, 1, d), lambda i: (i // per_b, 0, 0)),
        ],
        out_specs=pl.BlockSpec((tt, d), lambda i: (i, 0)),
        out_shape=jax.ShapeDtypeStruct((t, d), F32),
        compiler_params=_cparams(1),
        name="mixer_out_proj",
    )(x2, ya, yb, w_bf, gate)


def _lane_row():
    return _iota((1, LANES), 1)


def _head_segment_ones():
    r = _iota((LANES, LANES), 0) // HEAD_DIM
    c = _iota((LANES, LANES), 1) // HEAD_DIM
    return jnp.where(r == c, 1.0, 0.0).astype(BF16)


def _headnorm_rope(a, g, cos, sin):
    hi, lo = _split_bf16(a * a)
    seg = _head_segment_ones()
    ms = (_dot(hi, seg) + _dot(lo, seg)) * (1.0 / HEAD_DIM)
    y = a * lax.rsqrt(ms + NORM_EPS) * g
    half = HEAD_DIM // 2
    upper = pltpu.roll(y, LANES - half, axis=1)
    lower = pltpu.roll(y, half, axis=1)
    first_half = (_lane_row() % HEAD_DIM) < half
    rot = jnp.where(first_half, -upper, lower)
    return y * cos + rot * sin


def _head_masks():
    lane = _lane_row()
    return lane < HEAD_DIM, lane >= HEAD_DIM


def _attn_specs(seq, qcol, kcol, vcol, kv_shared):
    blk = (1, seq, LANES)
    q_spec = pl.BlockSpec(blk, lambda b, p: (b, 0, qcol + p))
    if kv_shared:
        k_spec = pl.BlockSpec(blk, lambda b, p: (b, 0, kcol))
        v_spec = pl.BlockSpec(blk, lambda b, p: (b, 0, vcol))
    else:
        k_spec = pl.BlockSpec(blk, lambda b, p: (b, 0, kcol + p))
        v_spec = pl.BlockSpec(blk, lambda b, p: (b, 0, vcol + p))
    return q_spec, k_spec, v_spec


def _row_spec(seq):
    return pl.BlockSpec((seq, LANES), lambda b, p: (0, 0))


def _gain_spec():
    return pl.BlockSpec((1, LANES), lambda b, p: (0, 0))


def _store_heads(o_ref, r0, outs):
    first, _ = _head_masks()
    o_ref[0, pl.ds(r0, QUERY_BLOCK), :] = jnp.where(first, outs[0], outs[1])


SWA_QBLOCKS_PER_STEP = 4


def _swa_kernel(sinks_ref, q_ref, k_ref, v_ref, cos_ref, sin_ref, gq_ref, gk_ref, o_ref,
                q0_s, q1_s, k_s, v_s):
    p = pl.program_id(1)
    seq = q_ref.shape[1]
    cos, sin = cos_ref[...], sin_ref[...]
    first, second = _head_masks()
    qn = _headnorm_rope(q_ref[0], gq_ref[...], cos, sin) * (HEAD_DIM ** -0.5)
    q0_s[...] = jnp.where(first, qn, 0.0).astype(BF16)
    q1_s[...] = jnp.where(second, qn, 0.0).astype(BF16)
    pairs_per_kv = (A_Q_HEADS // A_KV_HEADS) // 2
    keep = jnp.logical_xor(first, (p // pairs_per_kv) == 1)
    kn = _headnorm_rope(k_ref[0], gk_ref[...], cos, sin)
    k_s[...] = jnp.where(keep, kn, pltpu.roll(kn, HEAD_DIM, axis=1)).astype(BF16)
    v = v_ref[0]
    v_s[...] = jnp.where(keep, v, pltpu.roll(v, HEAD_DIM, axis=1)).astype(BF16)

    qi = _iota((QUERY_BLOCK, QUERY_BLOCK), 0)
    ki = _iota((QUERY_BLOCK, QUERY_BLOCK), 1)

    mask_c = ki <= qi
    per_step = min(SWA_QBLOCKS_PER_STEP, seq // QUERY_BLOCK)

    def qstep(step, carry):
        chains, scores = [], []
        for u in range(per_step):
            i = step * per_step + u
            r0 = pl.multiple_of(i * QUERY_BLOCK, QUERY_BLOCK)
            rp = pl.multiple_of(jnp.maximum(i - 1, 0) * QUERY_BLOCK, QUERY_BLOCK)
            kc, kp = k_s[pl.ds(r0, QUERY_BLOCK), :], k_s[pl.ds(rp, QUERY_BLOCK), :]
            vc, vp = v_s[pl.ds(r0, QUERY_BLOCK), :], v_s[pl.ds(rp, QUERY_BLOCK), :]
            mask_p = jnp.logical_and(ki > qi, i > 0)
            chains.append((r0, vc, vp, mask_p))
            for q_s in (q0_s, q1_s):
                qh = q_s[pl.ds(r0, QUERY_BLOCK), :]
                scores.append((_dot_nt(qh, kc), _dot_nt(qh, kp)))
        probs, denoms = [], []
        for n, (s_cur, s_prev) in enumerate(scores):
            mask_p = chains[n // 2][3]
            sc = jnp.where(mask_c, s_cur, NEG_BIG)
            sp = jnp.where(mask_p, s_prev, NEG_BIG)
            sink = sinks_ref[2 * p + n % 2]
            m = jnp.maximum(jnp.max(sc, axis=1, keepdims=True), jnp.max(sp, axis=1, keepdims=True))
            m = jnp.maximum(m, sink)
            ec, ep = jnp.exp(sc - m), jnp.exp(sp - m)
            denoms.append(jnp.sum(ec, axis=1, keepdims=True) + jnp.sum(ep, axis=1, keepdims=True)
                          + jnp.exp(sink - m))
            probs.append((ec.astype(BF16), ep.astype(BF16)))
        for u, (r0, vc, vp, _) in enumerate(chains):
            outs = [(_dot(probs[2 * u + hh][0], vc) + _dot(probs[2 * u + hh][1], vp)) / denoms[2 * u + hh]
                    for hh in range(2)]
            _store_heads(o_ref, r0, outs)
        return carry

    lax.fori_loop(0, seq // (QUERY_BLOCK * per_step), qstep, 0)


def _swa_attention(proj, sinks, cos, sin, gq, gk):
    b, seq, _ = proj.shape
    n_pairs = A_Q_HEADS // 2
    kcol = A_Q_HEADS * HEAD_DIM // LANES
    vcol = kcol + A_KV_HEADS * HEAD_DIM // LANES
    q_spec, k_spec, v_spec = _attn_specs(seq, 0, kcol, vcol, True)
    return pl.pallas_call(
        _swa_kernel,
        grid=(b, n_pairs),
        in_specs=[pl.BlockSpec(memory_space=pltpu.SMEM), q_spec, k_spec, v_spec,
                  _row_spec(seq), _row_spec(seq), _gain_spec(), _gain_spec()],
        out_specs=pl.BlockSpec((1, seq, LANES), lambda b_, p: (b_, 0, p)),
        out_shape=jax.ShapeDtypeStruct((b, seq, n_pairs * LANES), F32),
        scratch_shapes=[pltpu.VMEM((seq, LANES), BF16)] * 4,
        compiler_params=_cparams(2),
        name="swa_gqa_attention",
    )(sinks, proj, proj, proj, cos, sin, gq, gk)


STICK_GROUP = 4
STICK_QUERY_ROWS = 256


def _stick_kernel(q_ref, k_ref, v_ref, o_ref, k_s, v_s):
    seq = q_ref.shape[1]
    k_s[...] = k_ref[0].astype(BF16)
    v_s[...] = v_ref[0].astype(BF16)
    first, second = _head_masks()
    qr = min(STICK_QUERY_ROWS, seq)
    kw = QUERY_BLOCK
    key_minus_query = _iota((qr, kw), 1) - _iota((qr, kw), 0)
    wr = _iota((2 * kw, 2 * kw), 0) % kw
    wc = _iota((2 * kw, 2 * kw), 1)
    suffix_w = jnp.where(jnp.logical_or(wc >= kw, wr > wc), 1.0, 0.0).astype(BF16)

    def qblock(i, carry):
        r0 = pl.multiple_of(i * qr, qr)
        q = q_ref[0, pl.ds(r0, qr), :] * (HEAD_DIM ** -0.5)
        qhs = [jnp.where(msk, q, 0.0).astype(BF16) for msk in (first, second)]
        n_blocks = (r0 + qr) // kw

        def kgroup(g, st):
            accs, laters = [st[0], st[1]], [st[2], st[3]]
            chains = [(u, hh) for u in range(STICK_GROUP) for hh in range(2)]
            vbs, pasts, zs = [], [], {}
            for u in range(STICK_GROUP):
                j = n_blocks - 1 - (g * STICK_GROUP + u)
                live = j >= 0
                c0 = pl.multiple_of(jnp.maximum(j, 0) * kw, kw)
                kb = k_s[pl.ds(c0, kw), :]
                vbs.append(v_s[pl.ds(c0, kw), :])
                pasts.append(jnp.logical_and(key_minus_query < r0 - c0, live))
                for hh in range(2):
                    zs[u, hh] = _dot_nt(qhs[hh], kb)
            logit, sums = {}, {}
            for u, hh in chains:
                z = zs[u, hh]
                sp = jnp.maximum(z, 0.0) + jnp.log(1.0 + jnp.exp(-jnp.abs(z)))
                neg_log_keep = jnp.where(pasts[u], sp, 0.0)
                logit[u, hh] = z - sp
                hi, lo = _split_bf16(neg_log_keep)
                sums[u, hh] = _dot(jnp.concatenate([hi, lo], axis=1), suffix_w)
            ws = {}
            for u, hh in chains:
                inner, total = sums[u, hh][:, :kw], sums[u, hh][:, kw:]
                ws[u, hh] = jnp.where(pasts[u], jnp.exp(logit[u, hh] - inner - laters[hh]), 0.0).astype(BF16)
                laters[hh] = laters[hh] + total
            for u, hh in chains:
                accs[hh] = accs[hh] + _dot(ws[u, hh], vbs[u])
            return accs[0], accs[1], laters[0], laters[1]

        zero = jnp.zeros((qr, LANES), F32)
        n_groups = (n_blocks + STICK_GROUP - 1) // STICK_GROUP
        st = lax.fori_loop(0, n_groups, kgroup, (zero, zero, zero, zero))
        o_ref[0, pl.ds(r0, qr), :] = jnp.where(first, st[0], st[1])
        return carry

    lax.fori_loop(0, seq // qr, qblock, 0)


def _stick_attention(proj, qcol):
    b, seq, _ = proj.shape
    n_pairs = B_HEADS // 2
    q_spec, k_spec, v_spec = _attn_specs(seq, qcol, qcol + n_pairs, qcol + 2 * n_pairs, False)
    return pl.pallas_call(
        _stick_kernel,
        grid=(b, n_pairs),
        in_specs=[q_spec, k_spec, v_spec],
        out_specs=pl.BlockSpec((1, seq, LANES), lambda b_, p: (b_, 0, p)),
        out_shape=jax.ShapeDtypeStruct((b, seq, n_pairs * LANES), F32),
        scratch_shapes=[pltpu.VMEM((seq, LANES), BF16)] * 2,
        compiler_params=_cparams(2),
        name="stick_breaking_attention",
    )(proj, proj, proj)


def _prep_qkv(q_ref, k_ref, v_ref, cos_ref, sin_ref, gq_ref, gk_ref, q0_s, q1_s, k_s, v_s):
    cos, sin = cos_ref[...], sin_ref[...]
    first, second = _head_masks()
    qn = _headnorm_rope(q_ref[0], gq_ref[...], cos, sin) * (HEAD_DIM ** -0.5)
    q0_s[...] = jnp.where(first, qn, 0.0).astype(BF16)
    q1_s[...] = jnp.where(second, qn, 0.0).astype(BF16)
    kn = _headnorm_rope(k_ref[0], gk_ref[...], cos, sin)
    k_s[...] = kn.astype(BF16)
    v_s[...] = v_ref[0].astype(BF16)
    return qn, kn


DILATED_KEY_TILE = 512
DILATED_QUERY_ROWS = 256


def _dilated_kernel(q_ref, k_ref, v_ref, cos_ref, sin_ref, gq_ref, gk_ref, o_ref,
                    q0_s, q1_s, k_s, v_s):
    seq = q_ref.shape[1]
    _prep_qkv(q_ref, k_ref, v_ref, cos_ref, sin_ref, gq_ref, gk_ref, q0_s, q1_s, k_s, v_s)
    kt = min(DILATED_KEY_TILE, seq)
    qr = min(DILATED_QUERY_ROWS, seq)
    qk = _iota((qr, kt), 0) - _iota((qr, kt), 1)
    on_stride = [jnp.where((qk & (dil - 1)) == 0, 1.0, 0.0) for _, dil in C_PATTERNS]
    first_head, _ = _head_masks()

    def qblock(i, carry):
        r0 = pl.multiple_of(i * qr, qr)
        qhs = [q_s[pl.ds(r0, qr), :] for q_s in (q0_s, q1_s)]

        def ktile(g, st):
            c0 = pl.multiple_of(g * kt, kt)
            d = (r0 - c0) + qk
            count = jnp.zeros(d.shape, F32)
            for (window, _), stride_ok in zip(C_PATTERNS, on_stride):
                count = count + jnp.where(d <= window, stride_ok, 0.0)
            count = jnp.where(d >= 0, count, 0.0)
            kb, vb = k_s[pl.ds(c0, kt), :], v_s[pl.ds(c0, kt), :]
            scores = [_dot_nt(qhs[hh], kb) for hh in range(2)]
            new, prs = [], []
            for hh in range(2):
                m, l, acc = st[3 * hh:3 * hh + 3]
                s = jnp.where(count > 0.0, scores[hh], NEG_BIG)
                m_new = jnp.maximum(m, jnp.max(s, axis=1, keepdims=True))
                pr = count * jnp.exp(s - m_new)
                alpha = jnp.exp(m - m_new)
                new += [m_new, alpha * l + jnp.sum(pr, axis=1, keepdims=True), alpha * acc]
                prs.append(pr.astype(BF16))
            for hh in range(2):
                new[3 * hh + 2] = new[3 * hh + 2] + _dot(prs[hh], vb)
            return tuple(new)

        init = (jnp.full((qr, 1), NEG_BIG, F32), jnp.zeros((qr, 1), F32),
                jnp.zeros((qr, LANES), F32)) * 2
        st = lax.fori_loop(0, (r0 + qr + kt - 1) // kt, ktile, init)
        o_ref[0, pl.ds(r0, qr), :] = jnp.where(first_head, st[2] / st[1], st[5] / st[4])
        return carry

    lax.fori_loop(0, seq // qr, qblock, 0)


def _qkv_attention_call(kernel, name, proj, qcol, n_heads, cos, sin, gq, gk, extra_scratch=()):
    b, seq, _ = proj.shape
    n_pairs = n_heads // 2
    q_spec, k_spec, v_spec = _attn_specs(seq, qcol, qcol + n_pairs, qcol + 2 * n_pairs, False)
    return pl.pallas_call(
        kernel,
        grid=(b, n_pairs),
        in_specs=[q_spec, k_spec, v_spec, _row_spec(seq), _row_spec(seq), _gain_spec(), _gain_spec()],
        out_specs=pl.BlockSpec((1, seq, LANES), lambda b_, p: (b_, 0, p)),
        out_shape=jax.ShapeDtypeStruct((b, seq, n_pairs * LANES), F32),
        scratch_shapes=[pltpu.VMEM((seq, LANES), BF16)] * 4 + list(extra_scratch),
        compiler_params=_cparams(2),
        name=name,
    )(proj, proj, proj, cos, sin, gq, gk)


def _moba_kernel(q_ref, k_ref, v_ref, cos_ref, sin_ref, gq_ref, gk_ref, o_ref,
                 q0_s, q1_s, k_s, v_s, km_s, sel0_s, sel1_s):
    seq = q_ref.shape[1]
    n_blocks = seq // MOBA_BLOCK
    qn, kn = _prep_qkv(q_ref, k_ref, v_ref, cos_ref, sin_ref, gq_ref, gk_ref, q0_s, q1_s, k_s, v_s)
    km_s[...] = jnp.zeros(km_s.shape, F32)
    km_s[0:n_blocks, :] = jnp.mean(kn.reshape(n_blocks, MOBA_BLOCK, LANES), axis=1)
    first, second = _head_masks()

    rows8 = _iota((8, seq), 0)
    own8 = _iota((8, seq), 1) // MOBA_BLOCK
    valid = rows8 < own8
    for msk, sel_s in ((first, sel0_s), (second, sel1_s)):
        gate = _dot3_nt(km_s[...], jnp.where(msk, qn, 0.0))[0:8, :]
        gm = jnp.where(valid, gate, -jnp.inf)
        rank = jnp.zeros((8, seq), F32)
        for n2 in range(n_blocks):
            g2 = gm[n2:n2 + 1, :]
            beats = jnp.logical_or(g2 > gm, jnp.logical_and(g2 == gm, n2 < rows8))
            rank = rank + jnp.where(jnp.logical_and(beats, n2 < own8), 1.0, 0.0)
        sel = jnp.where(jnp.logical_and(valid, rank < float(MOBA_TOPK)), 1.0, 0.0)
        sel = jnp.concatenate([sel, jnp.zeros((LANES - 8, seq), F32)], axis=0)
        sel_s[...] = sel.T

    qrows = MOBA_BLOCK
    lane_sq = _iota((qrows, LANES), 1)
    causal = _iota((qrows, MOBA_BLOCK), 1) <= _iota((qrows, MOBA_BLOCK), 0)
    second_block = _iota((qrows, 2 * MOBA_BLOCK), 1) >= MOBA_BLOCK
    first_head, _ = _head_masks()

    def qblock(own, carry):
        r0 = pl.multiple_of(own * qrows, qrows)
        qhs = [q_s[pl.ds(r0, qrows), :] for q_s in (q0_s, q1_s)]
        sels = [sel_s[pl.ds(r0, qrows), :] for sel_s in (sel0_s, sel1_s)]
        kb, vb = k_s[pl.ds(r0, MOBA_BLOCK), :], v_s[pl.ds(r0, MOBA_BLOCK), :]
        scores = [_dot_nt(qhs[hh], kb) for hh in range(2)]
        init, prs = [], []
        for hh in range(2):
            s = jnp.where(causal, scores[hh], NEG_BIG)
            m = jnp.max(s, axis=1, keepdims=True)
            pr = jnp.exp(s - m)
            init += [m, jnp.sum(pr, axis=1, keepdims=True), None]
            prs.append(pr.astype(BF16))
        for hh in range(2):
            init[3 * hh + 2] = _dot(prs[hh], vb)

        def kpair(g, st):
            c0 = pl.multiple_of(g * 2 * MOBA_BLOCK, 2 * MOBA_BLOCK)
            kb2, vb2 = k_s[pl.ds(c0, 2 * MOBA_BLOCK), :], v_s[pl.ds(c0, 2 * MOBA_BLOCK), :]
            scores = [_dot_nt(qhs[hh], kb2) for hh in range(2)]
            new, prs = [], []
            for hh in range(2):
                m, l, acc = st[3 * hh:3 * hh + 3]
                sel_a = jnp.sum(jnp.where(lane_sq == 2 * g, sels[hh], 0.0), axis=1, keepdims=True)
                sel_b = jnp.sum(jnp.where(lane_sq == 2 * g + 1, sels[hh], 0.0), axis=1, keepdims=True)
                keep = jnp.where(second_block, sel_b, sel_a) > 0.0
                s = jnp.where(keep, scores[hh], NEG_BIG)
                m_new = jnp.maximum(m, jnp.max(s, axis=1, keepdims=True))
                pr = jnp.exp(s - m_new)
                alpha = jnp.exp(m - m_new)
                new += [m_new, alpha * l + jnp.sum(pr, axis=1, keepdims=True), alpha * acc]
                prs.append(pr.astype(BF16))
            for hh in range(2):
                new[3 * hh + 2] = new[3 * hh + 2] + _dot(prs[hh], vb2)
            return tuple(new)

        st = lax.fori_loop(0, (own + 1) // 2, kpair, tuple(init))
        o_ref[0, pl.ds(r0, qrows), :] = jnp.where(first_head, st[2] / st[1], st[5] / st[4])
        return carry

    lax.fori_loop(0, seq // qrows, qblock, 0)


ROUTE_CHUNKS_PER_STEP = 8
ROUTE_TOKEN_TILE = 1024


def _oddeven_merge_sort_pairs(n):
    pairs = []

    def merge(lo, hi, r):
        step = 2 * r
        if step < hi - lo:
            merge(lo, hi, step)
            merge(lo + r, hi, step)
            pairs.extend((i, i + r) for i in range(lo + r, hi - r, step))
        else:
            pairs.append((lo, lo + r))

    def sort(lo, hi):
        if hi - lo >= 1:
            mid = lo + (hi - lo) // 2
            sort(lo, mid)
            sort(mid + 1, hi)
            merge(lo, hi, 1)

    sort(0, n - 1)
    return pairs


SUBLANES = 8


def _top16_rows(scores, n_rows, vals_refs, idx_refs):
    n_slabs = n_rows // SUBLANES
    sub = _iota((SUBLANES, LANES), 0)
    vals = [[s[SUBLANES * v:SUBLANES * (v + 1), :] for v in range(n_slabs)] for s in scores]
    idxs = [[sub + SUBLANES * v for v in range(n_slabs)] for _ in scores]
    for i, j in _oddeven_merge_sort_pairs(n_slabs):
        for va, ia in zip(vals, idxs):
            a, b = va[i], va[j]
            a_first = jnp.logical_or(a > b, jnp.logical_and(a == b, ia[i] < ia[j]))
            va[i], va[j] = jnp.maximum(a, b), jnp.minimum(a, b)
            ia[i], ia[j] = jnp.where(a_first, ia[i], ia[j]), jnp.where(a_first, ia[j], ia[i])
    for it in range(PEER_TOPK):
        for k, (va, ia) in enumerate(zip(vals, idxs)):
            m = jnp.max(va[0], axis=0, keepdims=True)
            pick = jnp.min(jnp.where(va[0] == m, ia[0], n_rows), axis=0, keepdims=True)
            vals_refs[k][it:it + 1, :] = m
            idx_refs[k][it:it + 1, :] = pick
            win = ia[0] == pick
            depth = PEER_TOPK - 1 - it
            for d in range(min(depth, n_slabs - 1)):
                va[d] = jnp.where(win, va[d + 1], va[d])
                ia[d] = jnp.where(win, ia[d + 1], ia[d])
            if depth >= n_slabs:
                va[n_slabs - 1] = jnp.where(win, -jnp.inf, va[n_slabs - 1])


def _peer_route_kernel(x_ref, g_ref, sh_ref, sc_ref, wqt_ref, sk_ref, h_ref, ids_ref, gts_ref,
                       q_s, val_s, idx_s, ids_s, gts_s):
    tt = x_ref.shape[0]
    n_chunks = tt // LANES
    per_step = ROUTE_CHUNKS_PER_STEP
    half = PEER_D_KEY // 2
    h = _adaln(x_ref[...], g_ref[...], sh_ref[0], sc_ref[0])
    h_ref[...] = h
    qt = _dot_nt(wqt_ref[...], h.astype(BF16))
    for c in range(n_chunks):
        q_s[c] = qt[:, c * LANES:(c + 1) * LANES]
    sk1, sk2 = sk_ref[0], sk_ref[1]
    col_id = _iota((PEER_TOPK, LANES), 0)
    sub = _iota((SUBLANES, LANES), 0)
    col_depth = jnp.zeros((SUBLANES, LANES), I32)
    for a in range(SUBLANES):
        col_depth = jnp.where(sub == a, PEER_CAND_COUNTS[a], col_depth)
    vals = [val_s.at[i] for i in range(2 * per_step)]
    idxs = [idx_s.at[i] for i in range(2 * per_step)]

    def body(step, carry):
        hh = step // (n_chunks // per_step)
        c0 = (step % (n_chunks // per_step)) * per_step
        q0 = pl.multiple_of(hh * PEER_D_KEY, PEER_D_KEY)
        scores = []
        for k in range(per_step):
            scores.append(_dot3(sk1, q_s[c0 + k, pl.ds(q0, half), :]))
            scores.append(_dot3(sk2, q_s[c0 + k, pl.ds(q0 + half, half), :]))
        _top16_rows(scores, PEER_N_KEYS, vals, idxs)
        state = []
        for k in range(per_step):
            v1, i1, v2, i2 = vals[2 * k], idxs[2 * k], vals[2 * k + 1], idxs[2 * k + 1]
            v1x, e1x = v1[0:SUBLANES, :], i1[0:SUBLANES, :] * PEER_N_KEYS
            xs = [jnp.where(col_depth > b, v1x + v2[b:b + 1, :], -jnp.inf) for b in range(PEER_TOPK)]
            ex = [e1x + i2[b:b + 1, :] for b in range(PEER_TOPK)]
            y = v1[SUBLANES:PEER_TOPK, :] + v2[0:1, :]
            ey = i1[SUBLANES:PEER_TOPK, :] * PEER_N_KEYS + i2[0:1, :]
            state.append([xs, ex, y, ey])
        for it in range(PEER_TOPK):
            for k in range(per_step):
                xs, ex, y, ey = state[k]
                heads = jnp.concatenate([xs[0], y], axis=0)
                m = jnp.max(heads, axis=0, keepdims=True)
                pick = jnp.min(jnp.where(heads == m, col_id, PEER_TOPK), axis=0, keepdims=True)
                win = col_id == pick
                eids = jnp.where(win, jnp.concatenate([ex[0], ey], axis=0), 0)
                idxs[2 * k][it:it + 1, :] = jnp.sum(eids, axis=0, keepdims=True)
                vals[2 * k][it:it + 1, :] = m
                win_x, win_y = win[0:SUBLANES, :], win[SUBLANES:PEER_TOPK, :]
                for dd in range(PEER_TOPK - 1 - it):
                    xs[dd] = jnp.where(win_x, xs[dd + 1], xs[dd])
                    ex[dd] = jnp.where(win_x, ex[dd + 1], ex[dd])
                state[k][2] = jnp.where(win_y, -jnp.inf, y)
        r0 = pl.multiple_of(hh * PEER_TOPK, PEER_TOPK)
        for k in range(per_step):
            top = vals[2 * k][...]
            e = jnp.exp(top - top[0:1, :])
            gts_s[c0 + k, pl.ds(r0, PEER_TOPK), :] = e / jnp.sum(e, axis=0, keepdims=True)
            ids_s[c0 + k, pl.ds(r0, PEER_TOPK), :] = idxs[2 * k][...] * PACK_ROWS
        return carry

    lax.fori_loop(0, PEER_HEADS * n_chunks // per_step, body, 0)
    for c in range(n_chunks):
        rows = slice(c * LANES, (c + 1) * LANES)
        ids_ref[pl.ds(c * LANES * PEER_SLOTS, LANES * PEER_SLOTS)] = ids_s[c].T.reshape(LANES * PEER_SLOTS)
        gts_ref[rows, :] = gts_s[c].T


def _peer_route(x2, g, shift, scale, wqt_bf, sub_keys, seq):
    t, d = x2.shape
    tt = ROUTE_TOKEN_TILE
    per_b = seq // tt
    n_chunks = tt // LANES
    nq = wqt_bf.shape[0]
    out_blk = pl.BlockSpec((tt, PEER_SLOTS), lambda i: (i, 0))
    return pl.pallas_call(
        _peer_route_kernel,
        grid=(t // tt,),
        in_specs=[
            pl.BlockSpec((tt, d), lambda i: (i, 0)),
            pl.BlockSpec((1, d), lambda i: (0, 0)),
            pl.BlockSpec((1, 1, d), lambda i: (i // per_b, 0, 0)),
            pl.BlockSpec((1, 1, d), lambda i: (i // per_b, 0, 0)),
            pl.BlockSpec((nq, d), lambda i: (0, 0)),
            pl.BlockSpec(sub_keys.shape, lambda i: (0, 0, 0)),
        ],
        out_specs=[pl.BlockSpec((tt, d), lambda i: (i, 0)),
                   pl.BlockSpec((tt * PEER_SLOTS,), lambda i: (i,)), out_blk],
        out_shape=[jax.ShapeDtypeStruct((t, d), F32),
                   jax.ShapeDtypeStruct((t * PEER_SLOTS,), I32),
                   jax.ShapeDtypeStruct((t, PEER_SLOTS), F32)],
        scratch_shapes=[pltpu.VMEM((n_chunks, nq, LANES), F32),
                        pltpu.VMEM((2 * ROUTE_CHUNKS_PER_STEP, PEER_TOPK, LANES), F32),
                        pltpu.VMEM((2 * ROUTE_CHUNKS_PER_STEP, PEER_TOPK, LANES), I32),
                        pltpu.VMEM((n_chunks, PEER_SLOTS, LANES), I32),
                        pltpu.VMEM((n_chunks, PEER_SLOTS, LANES), F32)],
        compiler_params=_cparams(1),
        name="peer_route",
    )(x2, g, shift, scale, wqt_bf, sub_keys)


def _chunk_row(c):
    return 2 * (c % PACK_ROWS) + c // PACK_ROWS


def _bf16_bits(a):
    return lax.bitcast_convert_type(a.astype(BF16).astype(F32), U32)


def _pack_kernel(t_ref, o_ref):
    n = t_ref.shape[1]
    half = t_ref.shape[2] // 2
    for s in range(PACK_ROWS):
        lo = _bf16_bits(t_ref[0, :, s * LANES:(s + 1) * LANES])
        hi = _bf16_bits(t_ref[0, :, half + s * LANES:half + (s + 1) * LANES])
        o_ref[pl.ds(s, n, stride=PACK_ROWS), :] = (lo >> 16) | hi


def _pack_table(tables, layer):
    _, e, d = tables.shape
    assert d == 2 * PACK_ROWS * LANES
    be = PACK_EXPERT_TILE
    return pl.pallas_call(
        _pack_kernel,
        grid=(e // be,),
        in_specs=[pl.BlockSpec((1, be, d), lambda i: (layer, i, 0))],
        out_specs=pl.BlockSpec((be * PACK_ROWS, LANES), lambda i: (i, 0)),
        out_shape=jax.ShapeDtypeStruct((e * PACK_ROWS, LANES), U32),
        compiler_params=_cparams(1),
        name="pack_expert_table",
    )(tables)


def _table_spec(rows):
    return pl.BlockSpec((rows, LANES), lambda i: (0, 0), pipeline_mode=pl.Buffered(1))


def _gelu_exact(a):
    return 0.5 * a * (1.0 + lax.erf(a * (2.0 ** -0.5)))


FEAT_CHUNKS = 8
SLOT_WIDTH = PEER_SLOTS * FEAT_CHUNKS


def _gather_rows(ids_ref, base, tbl_ref, slot):
    for j in range(PEER_SLOTS):
        if j % ID_VIEW == 0:
            ids_part = ids_ref.at[pl.ds(base + j, ID_VIEW)]
        row0 = pl.multiple_of(ids_part[j % ID_VIEW], PACK_ROWS)
        slot[PACK_ROWS * j:PACK_ROWS * (j + 1), :] = tbl_ref[pl.ds(row0, PACK_ROWS), :]


def _pipelined_tokens(tt, ids_ref, next_ids_ref, tbl_ref, slots, compute):
    group = len(slots)
    n_steps = tt // group
    tiles = group // SUBLANES

    def compute_group(q):
        for k in range(group):
            compute(group * q + k, tiles * q + k // SUBLANES, k % SUBLANES, slots[k])

    @pl.when(pl.program_id(0) == 0)
    def _():
        for k in range(group):
            _gather_rows(ids_ref, k * PEER_SLOTS, tbl_ref, slots[k])

    def step(q, carry):
        compute_group(q)
        for k in range(group):
            _gather_rows(ids_ref, (group * (q + 1) + k) * PEER_SLOTS, tbl_ref, slots[k])
        return carry

    lax.fori_loop(0, n_steps - 1, step, 0)
    compute_group(n_steps - 1)
    for k in range(group):
        _gather_rows(next_ids_ref, k * PEER_SLOTS, tbl_ref, slots[k])


def _chunk_diag():
    return (_iota((FEAT_CHUNKS, SLOT_WIDTH), 1) % FEAT_CHUNKS) == _iota((FEAT_CHUNKS, SLOT_WIDTH), 0)


def _peer_u_kernel(ids_ref, next_ids_ref, h_ref, g_ref, tbl_ref, coef_ref, *scratch):
    slots, (hx_s, rs_s) = scratch[:N_SLOTS], scratch[N_SLOTS:]
    tt = h_ref.shape[0]
    for c in range(FEAT_CHUNKS):
        hx_s[pl.ds(_chunk_row(c), tt, stride=FEAT_CHUNKS), :] = h_ref[:, c * LANES:(c + 1) * LANES]
    diag = _chunk_diag()

    def compute(t, tile, sub, slot):
        rows = pltpu.bitcast(slot[...], BF16)
        x8 = hx_s[pl.ds(pl.multiple_of(t * FEAT_CHUNKS, FEAT_CHUNKS), FEAT_CHUNKS), :]
        part = _dot_nt(x8.astype(BF16), rows)
        rs_s[tile, sub:sub + 1, :] = jnp.sum(jnp.where(diag, part, 0.0), axis=0, keepdims=True)

    _pipelined_tokens(tt, ids_ref, next_ids_ref, tbl_ref, slots, compute)
    group = jnp.where(_iota((SLOT_WIDTH, PEER_SLOTS), 0) // FEAT_CHUNKS == _iota((SLOT_WIDTH, PEER_SLOTS), 1),
                      1.0, 0.0).astype(BF16)
    hi, lo = _split_bf16(rs_s[...].reshape(tt, SLOT_WIDTH))
    act = _dot(hi, group) + _dot(lo, group)
    coef_ref[...] = g_ref[...] * _gelu_exact(act)


N_SLOTS = 32
ID_VIEW = 16
PEER_TOKEN_TILE = 256


def _next_group_ids_spec(t, tt):
    groups_per_tile = tt // N_SLOTS
    last_tile = t // tt - 1
    return pl.BlockSpec((N_SLOTS * PEER_SLOTS,),
                        lambda i: (jnp.minimum(i + 1, last_tile) * groups_per_tile,),
                        memory_space=pltpu.SMEM)


def _slot_scratch():
    return [pltpu.VMEM((PEER_SLOTS * PACK_ROWS, LANES), U32)] * N_SLOTS


def _peer_u(ids_flat, h, gates, table, tt):
    t, d = h.shape
    return pl.pallas_call(
        _peer_u_kernel,
        grid=(t // tt,),
        in_specs=[
            pl.BlockSpec((tt * PEER_SLOTS,), lambda i: (i,), memory_space=pltpu.SMEM),
            _next_group_ids_spec(t, tt),
            pl.BlockSpec((tt, d), lambda i: (i, 0)),
            pl.BlockSpec((tt, PEER_SLOTS), lambda i: (i, 0)),
            _table_spec(table.shape[0]),
        ],
        out_specs=pl.BlockSpec((tt, PEER_SLOTS), lambda i: (i, 0)),
        out_shape=jax.ShapeDtypeStruct((t, PEER_SLOTS), F32),
        scratch_shapes=_slot_scratch() + [
                        pltpu.VMEM((tt * FEAT_CHUNKS, LANES), F32),
                        pltpu.VMEM((tt // SUBLANES, SUBLANES, SLOT_WIDTH), F32)],
        compiler_params=_cparams(1, TABLE_VMEM_LIMIT),
        name="peer_expert_in",
    )(ids_flat, ids_flat, h, gates, table)


def _peer_v_kernel(ids_ref, next_ids_ref, coef_ref, x_ref, gate_ref, tbl_ref, o_ref, *scratch):
    slots, (ce_hi_s, ce_lo_s, res_s) = scratch[:N_SLOTS], scratch[N_SLOTS:]
    tt = x_ref.shape[0]
    spread = jnp.where(_iota((PEER_SLOTS, SLOT_WIDTH), 1) // FEAT_CHUNKS == _iota((PEER_SLOTS, SLOT_WIDTH), 0),
                       1.0, 0.0).astype(BF16)
    hi, lo = _split_bf16(coef_ref[...])
    ce_hi_s[...] = _dot(hi, spread).reshape(ce_hi_s.shape)
    ce_lo_s[...] = _dot(lo, spread).reshape(ce_lo_s.shape)
    diag = _chunk_diag()

    def compute(t, tile, sub, slot):
        rows = pltpu.bitcast(slot[...], BF16)
        a_hi = jnp.where(diag, ce_hi_s[tile, sub:sub + 1, :], 0.0)
        a_lo = jnp.where(diag, ce_lo_s[tile, sub:sub + 1, :], 0.0)
        both = _dot(jnp.concatenate([a_hi, a_lo], axis=0).astype(BF16), rows)
        r0 = pl.multiple_of(t * FEAT_CHUNKS, FEAT_CHUNKS)
        res_s[pl.ds(r0, FEAT_CHUNKS), :] = both[0:FEAT_CHUNKS, :] + both[FEAT_CHUNKS:2 * FEAT_CHUNKS, :]

    _pipelined_tokens(tt, ids_ref, next_ids_ref, tbl_ref, slots, compute)
    for c in range(FEAT_CHUNKS):
        cols = slice(c * LANES, (c + 1) * LANES)
        y = res_s[pl.ds(_chunk_row(c), tt, stride=FEAT_CHUNKS), :]
        o_ref[:, cols] = x_ref[:, cols] + gate_ref[0][:, cols] * y


def _peer_v(ids_flat, coef, x2, gate, table, seq, tt):
    t, d = x2.shape
    per_b = seq // tt
    blk = pl.BlockSpec((tt, d), lambda i: (i, 0))
    return pl.pallas_call(
        _peer_v_kernel,
        grid=(t // tt,),
        in_specs=[
            pl.BlockSpec((tt * PEER_SLOTS,), lambda i: (i,), memory_space=pltpu.SMEM),
            _next_group_ids_spec(t, tt),
            pl.BlockSpec((tt, PEER_SLOTS), lambda i: (i, 0)),
            blk,
            pl.BlockSpec((1, 1, d), lambda i: (i // per_b, 0, 0)),
            _table_spec(table.shape[0]),
        ],
        out_specs=blk,
        out_shape=jax.ShapeDtypeStruct((t, d), F32),
        scratch_shapes=_slot_scratch() + [
                        pltpu.VMEM((tt // SUBLANES, SUBLANES, SLOT_WIDTH), F32),
                        pltpu.VMEM((tt // SUBLANES, SUBLANES, SLOT_WIDTH), F32),
                        pltpu.VMEM((tt * FEAT_CHUNKS, LANES), F32)],
        compiler_params=_cparams(1, TABLE_VMEM_LIMIT),
        name="peer_expert_out",
    )(ids_flat, ids_flat, coef, x2, gate, table)


def _peer_ffn(x2, g, shift, scale, gate, wq, sub_keys, table_u, table_v, seq):
    t, d = x2.shape
    tt = PEER_TOKEN_TILE
    h, ids, gates = _peer_route(x2, g, shift, scale, wq.T.astype(BF16), sub_keys, seq)
    ids_flat = ids
    coef = _peer_u(ids_flat, h, gates, table_u, tt)
    return _peer_v(ids_flat, coef, x2, gate, table_v, seq, tt)


def _rope_tables(seq):
    half = HEAD_DIM // 2
    inv_freq = ROPE_THETA ** (-jnp.arange(half, dtype=F32) / half)
    ang = jnp.arange(seq).astype(F32)[:, None] * inv_freq[None, :]
    reps = LANES // half
    return jnp.tile(jnp.cos(ang), (1, reps)), jnp.tile(jnp.sin(ang), (1, reps))


def _two_heads(gain):
    return jnp.tile(gain.reshape(1, HEAD_DIM), (1, LANES // HEAD_DIM))


def kernel(x, c, ada_w, ada_b, norm_mix_g, norm_ffn_g, w_in_ab, w_out_ab, sinks_a, qnorm_a, knorm_a,
           w_in_cd, w_out_cd, qnorm_c, knorm_c, qnorm_d, knorm_d, peer_wq, peer_subkeys, peer_u, peer_v):
    b, seq, d = x.shape
    depth = ada_w.shape[0]
    t = b * seq
    cos, sin = _rope_tables(seq)
    mod = _modulation(c, ada_w, ada_b)
    x2 = x.reshape(t, d)
    for layer in range(depth):
        shift_m, scale_m, gate_m, shift_f, scale_f, gate_f = [
            m.reshape(b, 1, d) for m in jnp.split(mod[layer], 6, axis=-1)]
        g_mix = norm_mix_g[layer].reshape(1, d)
        i = layer // 2
        if layer % 2 == 0:
            proj = _norm_proj(x2, g_mix, shift_m, scale_m, w_in_ab[i].astype(BF16), seq)
            proj = proj.reshape(b, seq, -1)
            ya = _swa_attention(proj, sinks_a[i], cos, sin, _two_heads(qnorm_a[i]), _two_heads(knorm_a[i]))
            b_col = (A_Q_HEADS + 2 * A_KV_HEADS) * HEAD_DIM // LANES
            yb = _stick_attention(proj, b_col)
            w_out = w_out_ab[i]
        else:
            proj = _norm_proj(x2, g_mix, shift_m, scale_m, w_in_cd[i].astype(BF16), seq)
            proj = proj.reshape(b, seq, -1)
            ya = _qkv_attention_call(_dilated_kernel, "dilated_attention", proj, 0, C_HEADS, cos, sin,
                                     _two_heads(qnorm_c[i]), _two_heads(knorm_c[i]))
            d_col = 3 * C_HEADS * HEAD_DIM // LANES
            yb = _qkv_attention_call(_moba_kernel, "moba_attention", proj, d_col, D_HEADS, cos, sin,
                                     _two_heads(qnorm_d[i]), _two_heads(knorm_d[i]),
                                     extra_scratch=(pltpu.VMEM((LANES, LANES), F32),
                                                    pltpu.VMEM((seq, LANES), F32),
                                                    pltpu.VMEM((seq, LANES), F32)))
            w_out = w_out_cd[i]
        x2 = _out_proj(x2, ya.reshape(t, -1), yb.reshape(t, -1), w_out.astype(BF16), gate_m, seq)
        x2 = _peer_ffn(x2, norm_ffn_g[layer].reshape(1, d), shift_f, scale_f, gate_f,
                       peer_wq[layer], peer_subkeys[layer],
                       _pack_table(peer_u, layer), _pack_table(peer_v, layer), seq)
    return x2.reshape(b, seq, d)
```
